```python
import jax, jax.numpy as jnp
from jax import lax
import numpy as np

D_MODEL = 1024
BATCH = 16
SEQ = 256
DEPTH = 2
DEC_BATCH = 4
DEC_SEQ = 1024
PAST_LEN = 256

GRID_W = 64
RMS_EPS = 1e-6
LN_EPS = 1e-5
N_MOD = 6
N_BRANCH = 3

RW_HEADS = 8
RW_HEAD_DIM = 64
RW_WIDTH = RW_HEADS * RW_HEAD_DIM
RW_LORA_W = 64
RW_LORA_A = 64
RW_LORA_G = 128
RW_GN_EPS = 64e-5
RW_SIZES = (RW_WIDTH, RW_WIDTH, RW_WIDTH, RW_LORA_W, RW_LORA_W, RW_LORA_A, RW_LORA_A, RW_LORA_G)
RW_COLS = sum(RW_SIZES)

CONV_WIDTH = 512
CONV_KERNEL = 31

ATT_HEADS = 8
ATT_KV_HEADS = 2
ATT_GROUP = ATT_HEADS // ATT_KV_HEADS
HEAD_DIM = 64
ROT_HALF = HEAD_DIM // 2
ROPE_THETA = 10000.0
Q_BLOCK = 128

PEER_HEADS = 8
PEER_N_KEYS = 128
PEER_N_EXPERTS = PEER_N_KEYS * PEER_N_KEYS
PEER_KEY_DIM = 128
PEER_HALF = PEER_KEY_DIM // 2
PEER_TOPK = 16
PEER_CHUNK = 128

IN_SIZES = (RW_COLS, 2 * CONV_WIDTH, ATT_HEADS * HEAD_DIM, ATT_KV_HEADS * HEAD_DIM, ATT_KV_HEADS * HEAD_DIM, N_BRANCH * D_MODEL)
IN_COLS = sum(IN_SIZES)

kernel_name = 'hybrid_rwkv7_conformer_gqa_peer_dit_step'


def _split(z, sizes):
    parts, start = [], 0
    for s in sizes:
        parts.append(z[..., start:start + s])
        start += s
    return parts


def rms_norm(x):
    xf = x.astype(jnp.float32)
    return (xf * lax.rsqrt(jnp.mean(xf * xf, axis=-1, keepdims=True) + RMS_EPS)).astype(x.dtype)


def head_rms_norm(x, g):
    xf = x.astype(jnp.float32)
    return (xf * lax.rsqrt(jnp.mean(xf * xf, axis=-1, keepdims=True) + RMS_EPS) * g).astype(x.dtype)


def layer_norm(x, w, b):
    xf = x.astype(jnp.float32)
    mu = jnp.mean(xf, axis=-1, keepdims=True)
    var = jnp.mean(jnp.square(xf - mu), axis=-1, keepdims=True)
    return ((xf - mu) * lax.rsqrt(var + LN_EPS) * w + b).astype(x.dtype)


def modulation(cond, w, b):
    return jnp.split(jax.nn.silu(cond) @ w + b, N_MOD, axis=-1)


def modulate(x, shift, scale):
    return rms_norm(x) * (1 + scale) + shift


def token_shift3(z, mu_prev, mu_next):
    z_prev = jnp.pad(z[:, :-1], ((0, 0), (1, 0), (0, 0)))
    z_next = jnp.pad(z[:, 1:], ((0, 0), (0, 1), (0, 0)))
    return z + mu_prev * (z_prev - z) + mu_next * (z_next - z)


def wkv_scan(r, w, k, v, kk, a, s0, reverse):
    def step(s, inp):
        r_t, w_t, k_t, v_t, kk_t, a_t = inp
        s_kk = jnp.einsum('bhvk,bhk->bhv', s, kk_t)
        s = (s * w_t[:, :, None, :]
             - s_kk[..., None] * (kk_t * a_t)[:, :, None, :]
             + v_t[..., None] * k_t[:, :, None, :])
        return s, jnp.einsum('bhvk,bhk->bhv', s, r_t)
    xs = tuple(jnp.moveaxis(t, 1, 0) for t in (r, w, k, v, kk, a))
    s_fin, out = lax.scan(step, s0, xs, reverse=reverse)
    return jnp.moveaxis(out, 0, 1), s_fin


def rwkv_mixer(z, lp, s0):
    B, T, _ = z.shape
    z = token_shift3(z, lp['shift_mu_prev'], lp['shift_mu_next'])
    r, k, v, zw_f, zw_b, za_f, za_b, zg = _split(z, RW_SIZES)
    heads = lambda t: t.astype(jnp.float32).reshape(B, T, RW_HEADS, RW_HEAD_DIM)
    kk = heads(k * lp['rw_k_k'])
    kk = kk * lax.rsqrt(jnp.sum(kk * kk, axis=-1, keepdims=True) + 1e-12)
    outs, finals = [], []
    for d, (zw, za) in enumerate(((zw_f, za_f), (zw_b, za_b))):
        w_raw = (lp['rw_w0'][d] + jnp.tanh(zw) @ lp['rw_w2'][d]).astype(jnp.float32)
        decay = jnp.exp(-jnp.exp(-jax.nn.softplus(-w_raw) - 0.5))
        a = jax.nn.sigmoid(lp['rw_a0'][d] + za @ lp['rw_a2'][d])
        k_d = k * (1 + (a - 1) * lp['rw_k_a'])
        o, s_fin = wkv_scan(heads(r), heads(decay), heads(k_d), heads(v), kk, heads(a),
                            s0[:, d].astype(jnp.float32), reverse=(d == 1))
        outs.append(o)
        finals.append(s_fin)
    o = outs[0] + outs[1]
    mu = jnp.mean(o, axis=-1, keepdims=True)
    var = jnp.mean(jnp.square(o - mu), axis=-1, keepdims=True)
    o = ((o - mu) * lax.rsqrt(var + RW_GN_EPS)).reshape(B, T, RW_WIDTH) * lp['rw_gn_w'] + lp['rw_gn_b']
    bonus = jnp.sum(heads(r * k * lp['rw_r_k']), axis=-1, keepdims=True) * heads(v)
    y = (o + bonus.reshape(B, T, RW_WIDTH)) * (jax.nn.sigmoid(zg) @ lp['rw_g2'])
    return y.astype(z.dtype), jnp.stack(finals, axis=1)


def conv_module(zc, lp):
    a, b = jnp.split(zc, 2, axis=-1)
    u = a * jax.nn.sigmoid(b)
    u = lax.conv_general_dilated(
        u, lp['conv_w'][:, None, :], window_strides=(1,),
        padding=[(CONV_KERNEL // 2, CONV_KERNEL // 2)],
        dimension_numbers=('NWC', 'WIO', 'NWC'),
        feature_group_count=CONV_WIDTH) + lp['conv_b']
    return jax.nn.silu(layer_norm(u, lp['conv_ln_w'], lp['conv_ln_b']))


def rope_2d_tables(T):
    rows = T // GRID_W
    row, col = jnp.meshgrid(jnp.arange(rows), jnp.arange(GRID_W), indexing='ij')
    inv = ROPE_THETA ** (-jnp.arange(0, ROT_HALF, 2, dtype=jnp.float32) / ROT_HALF)
    ang_r = row.reshape(-1, 1).astype(jnp.float32) * inv
    ang_c = col.reshape(-1, 1).astype(jnp.float32) * inv
    return jnp.cos(ang_r), jnp.sin(ang_r), jnp.cos(ang_c), jnp.sin(ang_c)


def _rotate(x, cos, sin):
    x1, x2 = jnp.split(x, 2, axis=-1)
    cos = cos[:, None, :]
    sin = sin[:, None, :]
    return jnp.concatenate([x1 * cos - x2 * sin, x1 * sin + x2 * cos], axis=-1)


def apply_rope_2d(x, tabs):
    cr, sr, cc, sc = tabs
    xr, xc = jnp.split(x, 2, axis=-1)
    return jnp.concatenate([_rotate(xr, cr, sr), _rotate(xc, cc, sc)], axis=-1).astype(x.dtype)


def block_attention(q, k, v):
    B, Tq = q.shape[:2]
    nb = Tq // Q_BLOCK
    qb = jnp.moveaxis(q.reshape(B, nb, Q_BLOCK, ATT_KV_HEADS, ATT_GROUP, HEAD_DIM), 1, 0)
    scale = HEAD_DIM ** -0.5

    def one_block(q_blk):
        s = jnp.einsum('bqhgd,bkhd->bhgqk', q_blk, k).astype(jnp.float32) * scale
        p = jax.nn.softmax(s, axis=-1)
        return jnp.einsum('bhgqk,bkhd->bqhgd', p.astype(v.dtype), v)

    out = lax.map(one_block, qb)
    return jnp.moveaxis(out, 0, 1).reshape(B, Tq, ATT_HEADS * HEAD_DIM)


def peer_ffn(h, lp):
    B, T, D = h.shape
    u_tab, v_tab, sub_keys, wq = lp['peer_u'], lp['peer_v'], lp['peer_keys'], lp['peer_wq']

    def chunk(hx):
        C = hx.shape[0]
        q = (hx @ wq).reshape(C, PEER_HEADS, 2, PEER_HALF)
        s = jnp.einsum('chpd,hpkd->chpk', q, sub_keys).astype(jnp.float32)
        sv, si = lax.top_k(s, PEER_TOPK)
        comb = (sv[:, :, 0, :, None] + sv[:, :, 1, None, :]).reshape(C, PEER_HEADS, PEER_TOPK * PEER_TOPK)
        fv, fi = lax.top_k(comb, PEER_TOPK)
        e1 = jnp.take_along_axis(si[:, :, 0], fi // PEER_TOPK, axis=-1)
        e2 = jnp.take_along_axis(si[:, :, 1], fi % PEER_TOPK, axis=-1)
        idx = e1 * PEER_N_KEYS + e2
        g = jax.nn.softmax(fv, axis=-1)
        u = jnp.take(u_tab, idx, axis=0)
        act = jax.nn.gelu(jnp.einsum('chkd,cd->chk', u, hx).astype(jnp.float32), approximate=False)
        vv = jnp.take(v_tab, idx, axis=0)
        return jnp.einsum('chk,chkd->cd', (g * act).astype(hx.dtype), vv)

    return lax.map(chunk, h.reshape(-1, PEER_CHUNK, D)).reshape(B, T, D)


def trunk_layer(x, mod, lp, is_latent, ctx_k, ctx_v, s0):
    shift1, scale1, gate1, shift2, scale2, gate2 = mod
    B, T, _ = x.shape
    h = modulate(x, shift1, scale1)
    z = h @ lp['w_in']
    z_rw, z_cv, z_q, z_k, z_v, z_gate = _split(z, IN_SIZES)
    y_rw, s_fin = rwkv_mixer(z_rw, lp, s0)
    y_cv = conv_module(z_cv, lp)
    q = head_rms_norm(z_q.reshape(B, T, ATT_HEADS, HEAD_DIM), lp['q_norm'])
    k = head_rms_norm(z_k.reshape(B, T, ATT_KV_HEADS, HEAD_DIM), lp['k_norm'])
    v = z_v.reshape(B, T, ATT_KV_HEADS, HEAD_DIM)
    if is_latent:
        tabs = rope_2d_tables(T)
        q = apply_rope_2d(q, tabs)
        k = apply_rope_2d(k, tabs)
        keys = jnp.concatenate([k, ctx_k.astype(k.dtype)], axis=1)
        vals = jnp.concatenate([v, ctx_v.astype(v.dtype)], axis=1)
    else:
        keys, vals = k, v
    y_at = block_attention(q.reshape(B, T, ATT_KV_HEADS, ATT_GROUP, HEAD_DIM), keys, vals)
    g_a, g_b, g_c = jnp.split(jax.nn.sigmoid(z_gate), N_BRANCH, axis=-1)
    merged = (g_a * (y_rw @ lp['w_out_rwkv'])
              + g_b * (y_cv @ lp['w_out_conv'])
              + g_c * (y_at @ lp['w_out_attn']))
    x = x + gate1 * (merged @ lp['w_o'])
    x = x + gate2 * peer_ffn(modulate(x, shift2, scale2), lp)
    return x, k, v, s_fin


def setup_inputs(seed: int = 0) -> dict:
    key = jax.random.key(seed)
    ks = iter(jax.random.split(key, 40))
    L, D = DEPTH, D_MODEL
    f32 = jnp.float32

    def nrm(shape, scale):
        return jax.random.normal(next(ks), shape, f32) * scale

    def near_one(shape):
        return 1.0 + nrm(shape, 0.1)

    return {
        'x_prompt': nrm((BATCH, SEQ, D), 1.0),
        'x_sample': nrm((DEC_BATCH, DEC_SEQ, D), 1.0),
        'c': nrm((DEC_BATCH, D), 1.0),
        'cache_k': nrm((DEC_BATCH, L, PAST_LEN, ATT_KV_HEADS, HEAD_DIM), 1.0),
        'cache_v': nrm((DEC_BATCH, L, PAST_LEN, ATT_KV_HEADS, HEAD_DIM), 1.0),
        'state_wkv': nrm((DEC_BATCH, L, 2, RW_HEADS, RW_HEAD_DIM, RW_HEAD_DIM), 0.5),
        'c_ctx': nrm((D,), 1.0),
        'w_mod': nrm((L, D, N_MOD * D), 0.5 * D ** -0.5),
        'b_mod': nrm((L, N_MOD * D), 0.02),
        'w_in': nrm((L, D, IN_COLS), D ** -0.5),
        'shift_mu_prev': 0.25 + nrm((L, RW_COLS), 0.05),
        'shift_mu_next': 0.25 + nrm((L, RW_COLS), 0.05),
        'rw_w0': jax.random.uniform(next(ks), (L, 2, RW_WIDTH), f32, -4.0, 0.0),
        'rw_w2': nrm((L, 2, RW_LORA_W, RW_WIDTH), 0.1 * RW_LORA_W ** -0.5),
        'rw_a0': nrm((L, 2, RW_WIDTH), 0.1),
        'rw_a2': nrm((L, 2, RW_LORA_A, RW_WIDTH), 0.5 * RW_LORA_A ** -0.5),
        'rw_g2': nrm((L, RW_LORA_G, RW_WIDTH), RW_LORA_G ** -0.5),
        'rw_k_k': near_one((L, RW_WIDTH)),
        'rw_k_a': near_one((L, RW_WIDTH)),
        'rw_r_k': nrm((L, RW_WIDTH), 0.1),
        'rw_gn_w': near_one((L, RW_WIDTH)),
        'rw_gn_b': nrm((L, RW_WIDTH), 0.02),
        'w_out_rwkv': nrm((L, RW_WIDTH, D), RW_WIDTH ** -0.5),
        'conv_w': nrm((L, CONV_KERNEL, CONV_WIDTH), CONV_KERNEL ** -0.5),
        'conv_b': nrm((L, CONV_WIDTH), 0.02),
        'conv_ln_w': near_one((L, CONV_WIDTH)),
        'conv_ln_b': nrm((L, CONV_WIDTH), 0.02),
        'w_out_conv': nrm((L, CONV_WIDTH, D), CONV_WIDTH ** -0.5),
        'q_norm': near_one((L, HEAD_DIM)),
        'k_norm': near_one((L, HEAD_DIM)),
        'w_out_attn': nrm((L, ATT_HEADS * HEAD_DIM, D), (ATT_HEADS * HEAD_DIM) ** -0.5),
        'w_o': nrm((L, D, D), D ** -0.5),
        'peer_wq': nrm((L, D, PEER_HEADS * PEER_KEY_DIM), D ** -0.5),
        'peer_keys': nrm((L, PEER_HEADS, 2, PEER_N_KEYS, PEER_HALF), PEER_HALF ** -0.5),
        'peer_u': nrm((L, PEER_N_EXPERTS, D), D ** -0.5),
        'peer_v': nrm((L, PEER_N_EXPERTS, D), PEER_HEADS ** -0.5),
    }


def reference(x_prompt, x_sample, c, cache_k, cache_v, state_wkv, c_ctx, w_mod, b_mod, w_in,
              shift_mu_prev, shift_mu_next, rw_w0, rw_w2, rw_a0, rw_a2, rw_g2, rw_k_k, rw_k_a,
              rw_r_k, rw_gn_w, rw_gn_b, w_out_rwkv, conv_w, conv_b, conv_ln_w, conv_ln_b,
              w_out_conv, q_norm, k_norm, w_out_attn, w_o, peer_wq, peer_keys, peer_u, peer_v):
    xp, xs = x_prompt, x_sample
    n_req = xp.shape[0]
    ctx_k, ctx_v, ctx_s = [], [], []
    for l in range(DEPTH):
        lp = dict(
            w_in=w_in[l], shift_mu_prev=shift_mu_prev[l], shift_mu_next=shift_mu_next[l],
            rw_w0=rw_w0[l], rw_w2=rw_w2[l], rw_a0=rw_a0[l], rw_a2=rw_a2[l], rw_g2=rw_g2[l],
            rw_k_k=rw_k_k[l], rw_k_a=rw_k_a[l], rw_r_k=rw_r_k[l], rw_gn_w=rw_gn_w[l],
            rw_gn_b=rw_gn_b[l], w_out_rwkv=w_out_rwkv[l], conv_w=conv_w[l], conv_b=conv_b[l],
            conv_ln_w=conv_ln_w[l], conv_ln_b=conv_ln_b[l], w_out_conv=w_out_conv[l],
            q_norm=q_norm[l], k_norm=k_norm[l], w_out_attn=w_out_attn[l], w_o=w_o[l],
            peer_wq=peer_wq[l], peer_keys=peer_keys[l], peer_u=peer_u[l], peer_v=peer_v[l])
        zero_state = jnp.zeros((n_req, 2, RW_HEADS, RW_HEAD_DIM, RW_HEAD_DIM), jnp.float32)
        mod_ctx = modulation(c_ctx, w_mod[l], b_mod[l])
        xp, k_l, v_l, s_l = trunk_layer(xp, mod_ctx, lp, False, None, None, zero_state)
        ctx_k.append(k_l)
        ctx_v.append(v_l)
        ctx_s.append(s_l)
        mod_lat = modulation(c[:, None, :], w_mod[l], b_mod[l])
        xs, _, _, _ = trunk_layer(xs, mod_lat, lp, True, cache_k[:, l], cache_v[:, l], state_wkv[:, l])
    new_cache_k = jnp.stack(ctx_k, axis=1)
    new_cache_v = jnp.stack(ctx_v, axis=1)
    new_state_wkv = jnp.stack(ctx_s, axis=1)
    return (xp, xs, new_cache_k, new_cache_v, new_state_wkv)
```

```python
import functools
import math

import jax
import jax.numpy as jnp
from jax import lax
from jax.experimental import pallas as pl
from jax.experimental.pallas import tpu as pltpu

F32 = jnp.float32
BF16 = jnp.bfloat16

LANES = 128
SCAN_CHUNK = 64
SCAN_SUB = 16
RW_HEAD_DIM = 64
HEADS_PER_PAIR = LANES // RW_HEAD_DIM
VMEM_LIMIT = 56 * 1024 * 1024


def _dot(a, b):
    return jnp.dot(a.astype(BF16), b.astype(BF16), preferred_element_type=F32)


def _dot_nt(a, b):
    return lax.dot_general(a.astype(BF16), b.astype(BF16), (((1,), (1,)), ((), ())),
                           preferred_element_type=F32)


def _dot_tn(a, b):
    return lax.dot_general(a.astype(BF16), b.astype(BF16), (((0,), (0,)), ((), ())),
                           preferred_element_type=F32)


def _dot3(a, b):
    a_hi = a.astype(BF16)
    b_hi = b.astype(BF16)
    a_lo = (a - a_hi.astype(F32)).astype(BF16)
    b_lo = (b - b_hi.astype(F32)).astype(BF16)
    d = functools.partial(jnp.dot, preferred_element_type=F32)
    return d(a_hi, b_hi) + (d(a_hi, b_lo) + d(a_lo, b_hi))


def _scan_pair(r, lw, k, v, kk, a, state, reverse):
    C = SCAN_CHUNK
    P = HEADS_PER_PAIR * C
    row = lax.broadcasted_iota(jnp.int32, (P, LANES), 0)
    col = lax.broadcasted_iota(jnp.int32, (P, LANES), 1)
    same_head = (row // C) == (col // RW_HEAD_DIM)
    tr = row % C
    tc = col % C
    incl = (tr >= tc) if not reverse else (tr <= tc)
    strict = (tr > tc) if not reverse else (tr < tc)
    diag_blk = (tr // SCAN_SUB) == (tc // SCAN_SUB)

    t_i = lax.broadcasted_iota(jnp.int32, (C, C), 0)
    s_i = lax.broadcasted_iota(jnp.int32, (C, C), 1)
    cum_mat = jnp.where((t_i >= s_i) if not reverse else (t_i <= s_i), 1.0, 0.0).astype(F32)
    cum = _dot3(cum_mat, lw)
    g = jnp.exp(cum)
    g_prev = jnp.exp(cum - lw)
    g_inv = jnp.exp(-cum)
    end = cum[C - 1:C, :] if not reverse else cum[0:1, :]
    g_end = jnp.exp(end)
    b = kk * a
    kap = kk * g_prev
    rt = r * g
    bt = b * g_inv
    kt = k * g_inv
    bh = bt * g_end
    kh = kt * g_end

    def stack(x):
        return jnp.where(same_head, jnp.concatenate([x, x], axis=0), 0.0)

    s_kap, s_rt, s_bt, s_kt, s_v = stack(kap), stack(rt), stack(bt), stack(kt), stack(v)
    s_bh, s_kh = stack(bh), stack(kh)

    lb = jnp.where(strict, _dot_nt(s_kap, s_bt), 0.0)
    lk = jnp.where(strict, _dot_nt(s_kap, s_kt), 0.0)
    pb = jnp.where(incl, _dot_nt(s_rt, s_bt), 0.0)
    pk = jnp.where(incl, _dot_nt(s_rt, s_kt), 0.0)
    lkv = _dot(lk, s_v)

    d = jnp.where(diag_blk, lb, 0.0)
    x = jnp.concatenate([lb - d, s_kap, lkv], axis=1)
    x = x - _dot(d, x)
    p = _dot(d, d)
    x = x + _dot(p, x)
    p = _dot(p, p)
    x = x + _dot(p, x)
    p = _dot(p, p)
    x = x + _dot(p, x)
    e = x[:, :LANES]
    rhs = x[:, LANES:]
    t = rhs + _dot(_dot(e, e), rhs)
    gy = t - _dot(e, t)

    lane_r = lax.broadcasted_iota(jnp.int32, (P, LANES), 0)
    lane_c = lax.broadcasted_iota(jnp.int32, (P, LANES), 1)
    m_mat = jnp.where(lane_r == lane_c, g_end, 0.0) - _dot_tn(s_bh, gy[:, :LANES])
    n_mat = _dot_tn(s_kh, s_v) - _dot_tn(s_bh, gy[:, LANES:])
    pbgy = _dot(pb, gy)
    q = s_rt - pbgy[:, :LANES]
    z = _dot(pk, s_v) - pbgy[:, LANES:]

    o_st = _dot3(q, state) + z
    new_state = _dot3(m_mat, state) + n_mat
    o = o_st[:C] + o_st[C:]
    return o, new_state


def _scan_kernel(rf, vf, kkf, lwf, af, kf, rb, vb, kkb, lwb, ab, kb, s0, of, ob, sfin, state):
    c = pl.program_id(1)
    n_pairs = rf.shape[-1] // LANES

    @pl.when(c == 0)
    def _():
        state[...] = s0[0]

    for d, (r_, v_, kk_, lw_, a_, k_, o_) in enumerate(((rf, vf, kkf, lwf, af, kf, of),
                                                        (rb, vb, kkb, lwb, ab, kb, ob))):
        for p in range(n_pairs):
            sl = slice(p * LANES, (p + 1) * LANES)
            o, new = _scan_pair(r_[:, sl], lw_[:, sl], k_[:, sl], v_[:, sl], kk_[:, sl], a_[:, sl],
                                state[d, p], reverse=(d == 1))
            o_[:, sl] = o
            state[d, p] = new

    @pl.when(c == pl.num_programs(1) - 1)
    def _():
        sfin[0] = state[...]


def rwkv_scan(r, v, kk, lw, a, kd, s0_bd, n_seq, seq_len):
    n_tok, width = r.shape
    n_pairs = width // LANES
    n_chunk = seq_len // SCAN_CHUNK
    fwd = lambda s, c: (s * n_chunk + c, 0)
    bwd = lambda s, c: (s * n_chunk + n_chunk - 1 - c, 0)
    blk = (SCAN_CHUNK, width)
    sspec = pl.BlockSpec((1, 2, n_pairs, LANES, LANES), lambda s, c: (s, 0, 0, 0, 0))
    in_specs = ([pl.BlockSpec(blk, fwd)] * 6 + [pl.BlockSpec(blk, bwd)] * 6 + [sspec])
    out_specs = [pl.BlockSpec(blk, fwd), pl.BlockSpec(blk, bwd), sspec]
    out_shape = [jax.ShapeDtypeStruct((n_tok, width), F32)] * 2 + [
        jax.ShapeDtypeStruct(s0_bd.shape, F32)]
    return pl.pallas_call(
        _scan_kernel,
        grid=(n_seq, n_chunk),
        in_specs=in_specs,
        out_specs=out_specs,
        out_shape=out_shape,
        scratch_shapes=[pltpu.VMEM((2, n_pairs, LANES, LANES), F32)],
        compiler_params=pltpu.CompilerParams(
            dimension_semantics=("parallel", "arbitrary"), vmem_limit_bytes=VMEM_LIMIT),
        name="rwkv_scan",
    )(r, v, kk, lw[0], a[0], kd[0], r, v, kk, lw[1], a[1], kd[1], s0_bd)


MOD_ROWS = 8
ROW_BLOCK = 256
RMS_EPS = 1e-6
LN_EPS = 1e-5
RW_GN_EPS = 64e-5


def _mod_kernel(c_ref, w_ref, b_ref, o_ref):
    c = c_ref[...]
    s = c * jax.nn.sigmoid(c)
    o_ref[...] = _dot(s, w_ref[...]) + b_ref[...]


def modulation(cond, w_mod, b_mod):
    n, d = cond.shape
    cols = w_mod.shape[1]
    blk = 1024
    return pl.pallas_call(
        _mod_kernel,
        grid=(cols // blk,),
        in_specs=[pl.BlockSpec((n, d), lambda j: (0, 0)),
                  pl.BlockSpec((d, blk), lambda j: (0, j)),
                  pl.BlockSpec((1, blk), lambda j: (0, j))],
        out_specs=pl.BlockSpec((n, blk), lambda j: (0, j)),
        out_shape=jax.ShapeDtypeStruct((n, cols), F32),
        name="modulation",
    )(cond, w_mod, b_mod)


def _modulate(x, shift, scale):
    ms = jnp.mean(x * x, axis=-1, keepdims=True)
    return x * lax.rsqrt(ms + RMS_EPS) * (1.0 + scale) + shift


def _inproj_kernel(x_ref, mod_ref, *refs):
    n = len(refs) // 2
    h = _modulate(x_ref[...], mod_ref[0, 0:1, :], mod_ref[0, 1:2, :]).astype(BF16)
    for w_ref, o_ref in zip(refs[:n], refs[n:]):
        o_ref[...] = jnp.dot(h, w_ref[...], preferred_element_type=F32)


def in_projection(x, mod_blocks, weights):
    n_tok, d = x.shape
    row = lambda i: (i, 0)
    const = lambda i: (0, 0)
    return pl.pallas_call(
        _inproj_kernel,
        grid=(n_tok // ROW_BLOCK,),
        in_specs=[pl.BlockSpec((ROW_BLOCK, d), row),
                  pl.BlockSpec((1, MOD_ROWS, d), lambda i: (i, 0, 0))]
                 + [pl.BlockSpec(w.shape, const, pipeline_mode=pl.Buffered(1)) for w in weights],
        out_specs=[pl.BlockSpec((ROW_BLOCK, w.shape[1]), row) for w in weights],
        out_shape=[jax.ShapeDtypeStruct((n_tok, w.shape[1]), F32) for w in weights],
        compiler_params=pltpu.CompilerParams(
            dimension_semantics=("parallel",), vmem_limit_bytes=VMEM_LIMIT),
        name="in_projection",
    )(x, mod_blocks, *weights)


ATT_Q_BLOCK = 256


def _attn_kernel(q_ref, k_ref, v_ref, o_ref):
    g, tq, dh = q_ref.shape[2:]
    q = q_ref[0, 0].reshape(g * tq, dh) * (dh ** -0.5)
    s = _dot_nt(q, k_ref[0, 0])
    m = jnp.max(s, axis=-1, keepdims=True)
    p = jnp.exp(s - m)
    p = p / jnp.sum(p, axis=-1, keepdims=True)
    o_ref[0, 0] = _dot(p, v_ref[0, 0]).reshape(g, tq, dh)


def attention(q, k, v):
    b, kvh, g, tq, dh = q.shape
    tk = k.shape[2]
    qb = min(ATT_Q_BLOCK, tq)
    qspec = pl.BlockSpec((1, 1, g, qb, dh), lambda i, j, t: (i, j, 0, t, 0))
    kspec = pl.BlockSpec((1, 1, tk, dh), lambda i, j, t: (i, j, 0, 0))
    return pl.pallas_call(
        _attn_kernel,
        grid=(b, kvh, tq // qb),
        in_specs=[qspec, kspec, kspec],
        out_specs=qspec,
        out_shape=jax.ShapeDtypeStruct(q.shape, F32),
        compiler_params=pltpu.CompilerParams(
            dimension_semantics=("parallel", "parallel", "arbitrary"),
            vmem_limit_bytes=VMEM_LIMIT),
        name="attention",
    )(q, k, v)


def _merge_kernel(x_ref, mod_ref, yrw_ref, ycv_ref, yat_ref, zg_ref,
                  wr_ref, wc_ref, wa_ref, wo_ref, wq_ref, xo_ref, h2_ref, q_ref):
    d = x_ref.shape[1]
    zg = zg_ref[...]
    merged = (jax.nn.sigmoid(zg[:, :d]) * _dot(yrw_ref[...], wr_ref[...])
              + jax.nn.sigmoid(zg[:, d:2 * d]) * _dot(ycv_ref[...], wc_ref[...])
              + jax.nn.sigmoid(zg[:, 2 * d:]) * _dot(yat_ref[...], wa_ref[...]))
    x = x_ref[...] + mod_ref[0, 2:3, :] * _dot(merged, wo_ref[...])
    xo_ref[...] = x
    h2 = _modulate(x, mod_ref[0, 3:4, :], mod_ref[0, 4:5, :]).astype(BF16)
    h2_ref[...] = h2
    q_ref[...] = jnp.dot(h2, wq_ref[...], preferred_element_type=F32)


def merge_project(x, mod_blocks, y_rw, y_cv, y_at, z_gate, w_r, w_c, w_a, w_o, w_q):
    n_tok, d = x.shape
    row = lambda i: (i, 0)
    const = lambda i: (0, 0)
    acts = [y_rw, y_cv, y_at, z_gate]
    weights = [w_r, w_c, w_a, w_o, w_q]
    return pl.pallas_call(
        _merge_kernel,
        grid=(n_tok // ROW_BLOCK,),
        in_specs=[pl.BlockSpec((ROW_BLOCK, d), row),
                  pl.BlockSpec((1, MOD_ROWS, d), lambda i: (i, 0, 0))]
                 + [pl.BlockSpec((ROW_BLOCK, a.shape[1]), row) for a in acts]
                 + [pl.BlockSpec(w.shape, const) for w in weights],
        out_specs=[pl.BlockSpec((ROW_BLOCK, d), row), pl.BlockSpec((ROW_BLOCK, d), row),
                   pl.BlockSpec((ROW_BLOCK, w_q.shape[1]), row)],
        out_shape=[jax.ShapeDtypeStruct((n_tok, d), F32), jax.ShapeDtypeStruct((n_tok, d), BF16),
                   jax.ShapeDtypeStruct((n_tok, w_q.shape[1]), F32)],
        compiler_params=pltpu.CompilerParams(
            dimension_semantics=("parallel",), vmem_limit_bytes=VMEM_LIMIT),
        name="merge_project",
    )(x, mod_blocks, *acts, *weights)


PEER_TOKEN_BLOCK = 1024
PEER_EXPERT_BLOCK = 1024


def _peer_kernel(h_ref, w_ref, u_ref, v_ref, x_ref, mod_ref, o_ref, acc_ref):
    e = pl.program_id(1)

    @pl.when(e == 0)
    def _():
        acc_ref[...] = jnp.zeros_like(acc_ref)

    s = _dot_nt(h_ref[...], u_ref[...])
    act = 0.5 * s * (1.0 + lax.erf(s * (2.0 ** -0.5)))
    acc_ref[...] += _dot(act * w_ref[...].astype(F32), v_ref[...])

    @pl.when(e == pl.num_programs(1) - 1)
    def _():
        for j in range(mod_ref.shape[0]):
            rows = slice(j * ROW_BLOCK, (j + 1) * ROW_BLOCK)
            o_ref[rows, :] = x_ref[rows, :] + mod_ref[j, 5:6, :] * acc_ref[rows, :]


def peer_dense(h2, w_sel, u_tab, v_tab, x, mod_blocks):
    n_tok, d = x.shape
    n_exp = u_tab.shape[0]
    tb, eb = PEER_TOKEN_BLOCK, PEER_EXPERT_BLOCK
    tok = lambda i, e: (i, 0)
    return pl.pallas_call(
        _peer_kernel,
        grid=(n_tok // tb, n_exp // eb),
        in_specs=[pl.BlockSpec((tb, d), tok),
                  pl.BlockSpec((tb, eb), lambda i, e: (i, e)),
                  pl.BlockSpec((eb, d), lambda i, e: (e, 0)),
                  pl.BlockSpec((eb, d), lambda i, e: (e, 0)),
                  pl.BlockSpec((tb, d), tok),
                  pl.BlockSpec((tb // ROW_BLOCK, MOD_ROWS, d), lambda i, e: (i, 0, 0))],
        out_specs=pl.BlockSpec((tb, d), tok),
        out_shape=jax.ShapeDtypeStruct((n_tok, d), F32),
        scratch_shapes=[pltpu.VMEM((tb, d), F32)],
        compiler_params=pltpu.CompilerParams(
            dimension_semantics=("parallel", "arbitrary"), vmem_limit_bytes=VMEM_LIMIT),
        name="peer_dense",
    )(h2, w_sel, u_tab, v_tab, x, mod_blocks)


RW_HEADS = 8
RW_WIDTH = RW_HEADS * RW_HEAD_DIM
RW_SIZES = (RW_WIDTH, RW_WIDTH, RW_WIDTH, 64, 64, 64, 64, 128)
CONV_WIDTH = 512
CONV_KERNEL = 31
ATT_HEADS = 8
ATT_KV_HEADS = 2
ATT_GROUP = ATT_HEADS // ATT_KV_HEADS
HEAD_DIM = 64
ROT_HALF = HEAD_DIM // 2
ROPE_THETA = 10000.0
GRID_W = 64
PEER_HEADS = 8
PEER_N_KEYS = 128
PEER_HALF = 64
PEER_TOPK = 16


def _split(z, sizes):
    parts, start = [], 0
    for s in sizes:
        parts.append(z[..., start:start + s])
        start += s
    return parts


def _bd_from_states(s):
    a = jnp.swapaxes(s, -1, -2)
    n, d, h, m, _ = a.shape
    a = a.reshape(n, d, h // 2, 2, m, m)
    z = jnp.zeros_like(a[:, :, :, 0])
    top = jnp.concatenate([a[:, :, :, 0], z], axis=-1)
    bot = jnp.concatenate([z, a[:, :, :, 1]], axis=-1)
    return jnp.concatenate([top, bot], axis=-2)


def _states_from_bd(bd):
    n, d, p, _, _ = bd.shape
    m = RW_HEAD_DIM
    a = jnp.stack([bd[:, :, :, :m, :m], bd[:, :, :, m:, m:]], axis=3).reshape(n, d, 2 * p, m, m)
    return jnp.swapaxes(a, -1, -2)


def _rope_tables(t):
    rows = t // GRID_W
    row, col = jnp.meshgrid(jnp.arange(rows), jnp.arange(GRID_W), indexing='ij')
    inv = ROPE_THETA ** (-jnp.arange(0, ROT_HALF, 2, dtype=F32) / ROT_HALF)
    ang_r = row.reshape(-1, 1).astype(F32) * inv
    ang_c = col.reshape(-1, 1).astype(F32) * inv
    return jnp.cos(ang_r), jnp.sin(ang_r), jnp.cos(ang_c), jnp.sin(ang_c)


def _rotate(x, cos, sin):
    x1, x2 = jnp.split(x, 2, axis=-1)
    cos = cos[:, None, :]
    sin = sin[:, None, :]
    return jnp.concatenate([x1 * cos - x2 * sin, x1 * sin + x2 * cos], axis=-1)


def _rope_2d(x, tabs):
    cr, sr, cc, sc = tabs
    xr, xc = jnp.split(x, 2, axis=-1)
    return jnp.concatenate([_rotate(xr, cr, sr), _rotate(xc, cc, sc)], axis=-1)


def _head_rms(x, g):
    return x * lax.rsqrt(jnp.mean(x * x, axis=-1, keepdims=True) + RMS_EPS) * g


def kernel(x_prompt, x_sample, c, cache_k, cache_v, state_wkv, c_ctx, w_mod, b_mod, w_in, shift_mu_prev, shift_mu_next, rw_w0, rw_w2, rw_a0, rw_a2, rw_g2, rw_k_k, rw_k_a, rw_r_k, rw_gn_w, rw_gn_b, w_out_rwkv, conv_w, conv_b, conv_ln_w, conv_ln_b, w_out_conv, q_norm, k_norm, w_out_attn, w_o, peer_wq, peer_keys, peer_u, peer_v):
    nb, seq, d = x_prompt.shape
    db, dseq, _ = x_sample.shape
    depth = w_in.shape[0]
    n_ctx = nb * seq
    n_lat = db * dseq
    n_tok = n_ctx + n_lat
    x = jnp.concatenate([x_prompt.reshape(n_ctx, d), x_sample.reshape(n_lat, d)], axis=0)

    cond = jnp.concatenate([c_ctx[None], c, jnp.zeros((MOD_ROWS - 1 - db, d), F32)], axis=0)
    blk_cond = jnp.concatenate([jnp.zeros((n_ctx // ROW_BLOCK,), jnp.int32),
                                1 + jnp.arange(n_lat // ROW_BLOCK, dtype=jnp.int32) // (dseq // ROW_BLOCK)])
    pos = jnp.concatenate([jnp.arange(n_ctx) % seq, jnp.arange(n_lat) % dseq])
    is_first = (pos == 0)[:, None]
    is_last = jnp.concatenate([jnp.arange(n_ctx) % seq == seq - 1,
                               jnp.arange(n_lat) % dseq == dseq - 1])[:, None]
    rope = _rope_tables(dseq)
    in_sizes = (sum(RW_SIZES), 2 * CONV_WIDTH, (ATT_HEADS + 2 * ATT_KV_HEADS) * HEAD_DIM, 3 * d)

    ctx_k, ctx_v, ctx_s = [], [], []
    for l in range(depth):
        mod = modulation(cond, w_mod[l].astype(BF16), b_mod[l][None])
        mod = mod.reshape(MOD_ROWS, 6, d)[blk_cond]
        mod_blocks = jnp.concatenate([mod, jnp.zeros((mod.shape[0], MOD_ROWS - 6, d), F32)], axis=1)

        z_rw, z_cv, z_qkv, z_gate = in_projection(x, mod_blocks, _split(w_in[l].astype(BF16), in_sizes))

        z_prev = jnp.where(is_first, 0.0, jnp.roll(z_rw, 1, axis=0))
        z_next = jnp.where(is_last, 0.0, jnp.roll(z_rw, -1, axis=0))
        zs = z_rw + shift_mu_prev[l] * (z_prev - z_rw) + shift_mu_next[l] * (z_next - z_rw)
        r, k, v, zw_f, zw_b, za_f, za_b, zg = _split(zs, RW_SIZES)
        kk = (k * rw_k_k[l]).reshape(n_tok, RW_HEADS, RW_HEAD_DIM)
        kk = (kk * lax.rsqrt(jnp.sum(kk * kk, axis=-1, keepdims=True) + 1e-12)).reshape(n_tok, RW_WIDTH)
        lws, aas, kds = [], [], []
        for dd, (zw, za) in enumerate(((zw_f, za_f), (zw_b, za_b))):
            w_raw = rw_w0[l, dd] + jnp.tanh(zw) @ rw_w2[l, dd]
            lws.append(-math.exp(-0.5) * jax.nn.sigmoid(w_raw))
            a = jax.nn.sigmoid(rw_a0[l, dd] + za @ rw_a2[l, dd])
            aas.append(a)
            kds.append(k * (1 + (a - 1) * rw_k_a[l]))
        lw, aa, kd = jnp.stack(lws), jnp.stack(aas), jnp.stack(kds)
        s0_ctx = jnp.zeros((nb, 2, RW_HEADS // 2, LANES, LANES), F32)
        of_c, ob_c, sfin = rwkv_scan(r[:n_ctx], v[:n_ctx], kk[:n_ctx], lw[:, :n_ctx], aa[:, :n_ctx],
                                     kd[:, :n_ctx], s0_ctx, nb, seq)
        of_l, ob_l, _ = rwkv_scan(r[n_ctx:], v[n_ctx:], kk[n_ctx:], lw[:, n_ctx:], aa[:, n_ctx:],
                                  kd[:, n_ctx:], _bd_from_states(state_wkv[:, l]), db, dseq)
        ctx_s.append(_states_from_bd(sfin))
        o = jnp.concatenate([of_c + ob_c, of_l + ob_l], axis=0).reshape(n_tok, RW_HEADS, RW_HEAD_DIM)
        mu = jnp.mean(o, axis=-1, keepdims=True)
        var = jnp.mean(jnp.square(o - mu), axis=-1, keepdims=True)
        o = ((o - mu) * lax.rsqrt(var + RW_GN_EPS)).reshape(n_tok, RW_WIDTH) * rw_gn_w[l] + rw_gn_b[l]
        bonus = jnp.sum((r * k * rw_r_k[l]).reshape(n_tok, RW_HEADS, RW_HEAD_DIM), axis=-1, keepdims=True)
        bonus = (bonus * v.reshape(n_tok, RW_HEADS, RW_HEAD_DIM)).reshape(n_tok, RW_WIDTH)
        y_rw = (o + bonus) * (jax.nn.sigmoid(zg) @ rw_g2[l])

        ca, cb = jnp.split(z_cv, 2, axis=-1)
        u = ca * jax.nn.sigmoid(cb)

        def conv(us):
            return lax.conv_general_dilated(
                us, conv_w[l][:, None, :], window_strides=(1,),
                padding=[(CONV_KERNEL // 2, CONV_KERNEL // 2)],
                dimension_numbers=('NWC', 'WIO', 'NWC'), feature_group_count=CONV_WIDTH)

        u = jnp.concatenate([conv(u[:n_ctx].reshape(nb, seq, CONV_WIDTH)).reshape(n_ctx, CONV_WIDTH),
                             conv(u[n_ctx:].reshape(db, dseq, CONV_WIDTH)).reshape(n_lat, CONV_WIDTH)])
        u = u + conv_b[l]
        mu = jnp.mean(u, axis=-1, keepdims=True)
        var = jnp.mean(jnp.square(u - mu), axis=-1, keepdims=True)
        u = (u - mu) * lax.rsqrt(var + LN_EPS) * conv_ln_w[l] + conv_ln_b[l]
        y_cv = u * jax.nn.sigmoid(u)

        nq, nkv = ATT_HEADS * HEAD_DIM, ATT_KV_HEADS * HEAD_DIM
        q = _head_rms(z_qkv[:, :nq].reshape(n_tok, ATT_HEADS, HEAD_DIM), q_norm[l])
        kx = _head_rms(z_qkv[:, nq:nq + nkv].reshape(n_tok, ATT_KV_HEADS, HEAD_DIM), k_norm[l])
        vx = z_qkv[:, nq + nkv:].reshape(n_tok, ATT_KV_HEADS, HEAD_DIM)
        ctx_k.append(kx[:n_ctx].reshape(nb, seq, ATT_KV_HEADS, HEAD_DIM))
        ctx_v.append(vx[:n_ctx].reshape(nb, seq, ATT_KV_HEADS, HEAD_DIM))

        def heads_q(t, b_, t_):
            return t.reshape(b_, t_, ATT_KV_HEADS, ATT_GROUP, HEAD_DIM).transpose(0, 2, 3, 1, 4)

        def heads_kv(t):
            return t.transpose(0, 2, 1, 3)

        y_c = attention(heads_q(q[:n_ctx], nb, seq),
                        heads_kv(kx[:n_ctx].reshape(nb, seq, ATT_KV_HEADS, HEAD_DIM)),
                        heads_kv(vx[:n_ctx].reshape(nb, seq, ATT_KV_HEADS, HEAD_DIM)))
        q_l = _rope_2d(q[n_ctx:].reshape(db, dseq, ATT_HEADS, HEAD_DIM).reshape(db * dseq, ATT_HEADS, HEAD_DIM)
                       .reshape(db, dseq, ATT_HEADS, HEAD_DIM), rope)
        k_l = _rope_2d(kx[n_ctx:].reshape(db, dseq, ATT_KV_HEADS, HEAD_DIM), rope)
        keys = jnp.concatenate([k_l, cache_k[:, l]], axis=1)
        vals = jnp.concatenate([vx[n_ctx:].reshape(db, dseq, ATT_KV_HEADS, HEAD_DIM), cache_v[:, l]], axis=1)
        y_l = attention(heads_q(q_l.reshape(n_lat, ATT_HEADS, HEAD_DIM), db, dseq), heads_kv(keys), heads_kv(vals))
        unheads = lambda y, n: y.transpose(0, 3, 1, 2, 4).reshape(n, nq)
        y_at = jnp.concatenate([unheads(y_c, n_ctx), unheads(y_l, n_lat)], axis=0)

        x, h2, pq = merge_project(x, mod_blocks, y_rw, y_cv, y_at, z_gate,
                                  w_out_rwkv[l].astype(BF16), w_out_conv[l].astype(BF16),
                                  w_out_attn[l].astype(BF16), w_o[l].astype(BF16), peer_wq[l].astype(BF16))

        pq = pq.reshape(n_tok, PEER_HEADS, 2, PEER_HALF)
        s = jnp.einsum('chpd,hpkd->chpk', pq, peer_keys[l])
        sv, si = lax.top_k(s, PEER_TOPK)
        comb = (sv[:, :, 0, :, None] + sv[:, :, 1, None, :]).reshape(n_tok, PEER_HEADS, PEER_TOPK * PEER_TOPK)
        fv, fi = lax.top_k(comb, PEER_TOPK)
        e1 = jnp.take_along_axis(si[:, :, 0], fi // PEER_TOPK, axis=-1)
        e2 = jnp.take_along_axis(si[:, :, 1], fi % PEER_TOPK, axis=-1)
        idx = (e1 * PEER_N_KEYS + e2).reshape(n_tok, PEER_HEADS * PEER_TOPK)
        g = jax.nn.softmax(fv, axis=-1).reshape(n_tok, PEER_HEADS * PEER_TOPK)
        w_sel = jnp.zeros((n_tok, PEER_N_KEYS * PEER_N_KEYS), F32).at[
            jnp.arange(n_tok)[:, None], idx].add(g).astype(BF16)
        x = peer_dense(h2, w_sel, peer_u[l].astype(BF16), peer_v[l].astype(BF16), x, mod_blocks)

    new_cache_k = jnp.stack(ctx_k, axis=1)
    new_cache_v = jnp.stack(ctx_v, axis=1)
    new_state = jnp.stack(ctx_s, axis=1)
    return (x[:n_ctx].reshape(nb, seq, d), x[n_ctx:].reshape(db, dseq, d),
            new_cache_k, new_cache_v, new_state)
```

```python
import functools
import math

import jax
import jax.numpy as jnp
from jax import lax
from jax.experimental import pallas as pl
from jax.experimental.pallas import tpu as pltpu

F32 = jnp.float32
BF16 = jnp.bfloat16

LANES = 128
SCAN_CHUNK = 64
SCAN_SUB = 16
RW_HEAD_DIM = 64
HEADS_PER_PAIR = LANES // RW_HEAD_DIM
VMEM_LIMIT = 56 * 1024 * 1024


def _dot(a, b):
    return jnp.dot(a.astype(BF16), b.astype(BF16), preferred_element_type=F32)


def _dot_nt(a, b):
    return lax.dot_general(a.astype(BF16), b.astype(BF16), (((1,), (1,)), ((), ())),
                           preferred_element_type=F32)


def _dot_tn(a, b):
    return lax.dot_general(a.astype(BF16), b.astype(BF16), (((0,), (0,)), ((), ())),
                           preferred_element_type=F32)


def _dot3(a, b):
    a_hi = a.astype(BF16)
    b_hi = b.astype(BF16)
    a_lo = (a - a_hi.astype(F32)).astype(BF16)
    b_lo = (b - b_hi.astype(F32)).astype(BF16)
    d = functools.partial(jnp.dot, preferred_element_type=F32)
    return d(a_hi, b_hi) + (d(a_hi, b_lo) + d(a_lo, b_hi))


def _scan_pair(r, lw, k, v, kk, a, state, reverse):
    C = SCAN_CHUNK
    P = HEADS_PER_PAIR * C
    row = lax.broadcasted_iota(jnp.int32, (P, LANES), 0)
    col = lax.broadcasted_iota(jnp.int32, (P, LANES), 1)
    same_head = (row // C) == (col // RW_HEAD_DIM)
    tr = row % C
    tc = col % C
    incl = (tr >= tc) if not reverse else (tr <= tc)
    strict = (tr > tc) if not reverse else (tr < tc)
    diag_blk = (tr // SCAN_SUB) == (tc // SCAN_SUB)

    t_i = lax.broadcasted_iota(jnp.int32, (C, C), 0)
    s_i = lax.broadcasted_iota(jnp.int32, (C, C), 1)
    cum_mat = jnp.where((t_i >= s_i) if not reverse else (t_i <= s_i), 1.0, 0.0).astype(F32)
    cum = _dot3(cum_mat, lw)
    g = jnp.exp(cum)
    g_prev = jnp.exp(cum - lw)
    g_inv = jnp.exp(-cum)
    end = cum[C - 1:C, :] if not reverse else cum[0:1, :]
    g_end = jnp.exp(end)
    b = kk * a
    kap = kk * g_prev
    rt = r * g
    bt = b * g_inv
    kt = k * g_inv
    bh = bt * g_end
    kh = kt * g_end

    def stack(x):
        return jnp.where(same_head, jnp.concatenate([x, x], axis=0), 0.0)

    s_kap, s_rt, s_bt, s_kt, s_v = stack(kap), stack(rt), stack(bt), stack(kt), stack(v)
    s_bh, s_kh = stack(bh), stack(kh)

    lb = jnp.where(strict, _dot_nt(s_kap, s_bt), 0.0)
    lk = jnp.where(strict, _dot_nt(s_kap, s_kt), 0.0)
    pb = jnp.where(incl, _dot_nt(s_rt, s_bt), 0.0)
    pk = jnp.where(incl, _dot_nt(s_rt, s_kt), 0.0)
    lkv = _dot(lk, s_v)

    d = jnp.where(diag_blk, lb, 0.0)
    x = jnp.concatenate([lb - d, s_kap, lkv], axis=1)
    x = x - _dot(d, x)
    p = _dot(d, d)
    x = x + _dot(p, x)
    p = _dot(p, p)
    x = x + _dot(p, x)
    p = _dot(p, p)
    x = x + _dot(p, x)
    e = x[:, :LANES]
    rhs = x[:, LANES:]
    t = rhs + _dot(_dot(e, e), rhs)
    gy = t - _dot(e, t)

    lane_r = lax.broadcasted_iota(jnp.int32, (P, LANES), 0)
    lane_c = lax.broadcasted_iota(jnp.int32, (P, LANES), 1)
    m_mat = jnp.where(lane_r == lane_c, g_end, 0.0) - _dot_tn(s_bh, gy[:, :LANES])
    n_mat = _dot_tn(s_kh, s_v) - _dot_tn(s_bh, gy[:, LANES:])
    pbgy = _dot(pb, gy)
    q = s_rt - pbgy[:, :LANES]
    z = _dot(pk, s_v) - pbgy[:, LANES:]

    o_st = _dot3(q, state) + z
    new_state = _dot3(m_mat, state) + n_mat
    o = o_st[:C] + o_st[C:]
    return o, new_state


def _scan_kernel(rf, vf, kkf, lwf, af, kf, rb, vb, kkb, lwb, ab, kb, s0, of, ob, sfin, state):
    c = pl.program_id(1)
    n_pairs = rf.shape[-1] // LANES

    @pl.when(c == 0)
    def _():
        state[...] = s0[0]

    for d, (r_, v_, kk_, lw_, a_, k_, o_) in enumerate(((rf, vf, kkf, lwf, af, kf, of),
                                                        (rb, vb, kkb, lwb, ab, kb, ob))):
        for p in range(n_pairs):
            sl = slice(p * LANES, (p + 1) * LANES)
            o, new = _scan_pair(r_[:, sl], lw_[:, sl], k_[:, sl], v_[:, sl], kk_[:, sl], a_[:, sl],
                                state[d, p], reverse=(d == 1))
            o_[:, sl] = o
            state[d, p] = new

    @pl.when(c == pl.num_programs(1) - 1)
    def _():
        sfin[0] = state[...]


def rwkv_scan(r, v, kk, lw, a, kd, s0_bd, n_seq, seq_len):
    n_tok, width = r.shape
    n_pairs = width // LANES
    n_chunk = seq_len // SCAN_CHUNK
    fwd = lambda s, c: (s * n_chunk + c, 0)
    bwd = lambda s, c: (s * n_chunk + n_chunk - 1 - c, 0)
    blk = (SCAN_CHUNK, width)
    sspec = pl.BlockSpec((1, 2, n_pairs, LANES, LANES), lambda s, c: (s, 0, 0, 0, 0))
    in_specs = ([pl.BlockSpec(blk, fwd)] * 6 + [pl.BlockSpec(blk, bwd)] * 6 + [sspec])
    out_specs = [pl.BlockSpec(blk, fwd), pl.BlockSpec(blk, bwd), sspec]
    out_shape = [jax.ShapeDtypeStruct((n_tok, width), F32)] * 2 + [
        jax.ShapeDtypeStruct(s0_bd.shape, F32)]
    return pl.pallas_call(
        _scan_kernel,
        grid=(n_seq, n_chunk),
        in_specs=in_specs,
        out_specs=out_specs,
        out_shape=out_shape,
        scratch_shapes=[pltpu.VMEM((2, n_pairs, LANES, LANES), F32)],
        compiler_params=pltpu.CompilerParams(
            dimension_semantics=("parallel", "arbitrary"), vmem_limit_bytes=VMEM_LIMIT),
        name="rwkv_scan",
    )(r, v, kk, lw[0], a[0], kd[0], r, v, kk, lw[1], a[1], kd[1], s0_bd)


MOD_ROWS = 8
ROW_BLOCK = 256
RMS_EPS = 1e-6
LN_EPS = 1e-5
RW_GN_EPS = 64e-5


def _mod_kernel(c_ref, w_ref, b_ref, o_ref):
    c = c_ref[...]
    s = c * jax.nn.sigmoid(c)
    o_ref[...] = _dot(s, w_ref[...]) + b_ref[...]


def modulation(cond, w_mod, b_mod):
    n, d = cond.shape
    cols = w_mod.shape[1]
    blk = 1024
    return pl.pallas_call(
        _mod_kernel,
        grid=(cols // blk,),
        in_specs=[pl.BlockSpec((n, d), lambda j: (0, 0)),
                  pl.BlockSpec((d, blk), lambda j: (0, j)),
                  pl.BlockSpec((1, blk), lambda j: (0, j))],
        out_specs=pl.BlockSpec((n, blk), lambda j: (0, j)),
        out_shape=jax.ShapeDtypeStruct((n, cols), F32),
        name="modulation",
    )(cond, w_mod, b_mod)


def _modulate(x, shift, scale):
    ms = jnp.mean(x * x, axis=-1, keepdims=True)
    return x * lax.rsqrt(ms + RMS_EPS) * (1.0 + scale) + shift


def _inproj_kernel(x_ref, mod_ref, *refs):
    n = len(refs) // 2
    h = _modulate(x_ref[...], mod_ref[0, 0:1, :], mod_ref[0, 1:2, :]).astype(BF16)
    for w_ref, o_ref in zip(refs[:n], refs[n:]):
        o_ref[...] = jnp.dot(h, w_ref[...], preferred_element_type=F32)


def in_projection(x, mod_blocks, weights):
    n_tok, d = x.shape
    row = lambda i: (i, 0)
    const = lambda i: (0, 0)
    return pl.pallas_call(
        _inproj_kernel,
        grid=(n_tok // ROW_BLOCK,),
        in_specs=[pl.BlockSpec((ROW_BLOCK, d), row),
                  pl.BlockSpec((1, MOD_ROWS, d), lambda i: (i, 0, 0))]
                 + [pl.BlockSpec(w.shape, const, pipeline_mode=pl.Buffered(1)) for w in weights],
        out_specs=[pl.BlockSpec((ROW_BLOCK, w.shape[1]), row) for w in weights],
        out_shape=[jax.ShapeDtypeStruct((n_tok, w.shape[1]), F32) for w in weights],
        compiler_params=pltpu.CompilerParams(
            dimension_semantics=("parallel",), vmem_limit_bytes=VMEM_LIMIT),
        name="in_projection",
    )(x, mod_blocks, *weights)


ATT_Q_BLOCK = 256


def _attn_kernel(q_ref, k_ref, v_ref, o_ref):
    g, tq, dh = q_ref.shape[2:]
    q = q_ref[0, 0].reshape(g * tq, dh) * (dh ** -0.5)
    s = _dot_nt(q, k_ref[0, 0])
    m = jnp.max(s, axis=-1, keepdims=True)
    p = jnp.exp(s - m)
    p = p / jnp.sum(p, axis=-1, keepdims=True)
    o_ref[0, 0] = _dot(p, v_ref[0, 0]).reshape(g, tq, dh)


def attention(q, k, v):
    b, kvh, g, tq, dh = q.shape
    tk = k.shape[2]
    qb = min(ATT_Q_BLOCK, tq)
    qspec = pl.BlockSpec((1, 1, g, qb, dh), lambda i, j, t: (i, j, 0, t, 0))
    kspec = pl.BlockSpec((1, 1, tk, dh), lambda i, j, t: (i, j, 0, 0))
    return pl.pallas_call(
        _attn_kernel,
        grid=(b, kvh, tq // qb),
        in_specs=[qspec, kspec, kspec],
        out_specs=qspec,
        out_shape=jax.ShapeDtypeStruct(q.shape, F32),
        compiler_params=pltpu.CompilerParams(
            dimension_semantics=("parallel", "parallel", "arbitrary"),
            vmem_limit_bytes=VMEM_LIMIT),
        name="attention",
    )(q, k, v)


def _merge_kernel(x_ref, mod_ref, yrw_ref, ycv_ref, yat_ref, zg_ref,
                  wr_ref, wc_ref, wa_ref, wo_ref, wq_ref, xo_ref, h2_ref, q_ref):
    d = x_ref.shape[1]
    zg = zg_ref[...]
    merged = (jax.nn.sigmoid(zg[:, :d]) * _dot(yrw_ref[...], wr_ref[...])
              + jax.nn.sigmoid(zg[:, d:2 * d]) * _dot(ycv_ref[...], wc_ref[...])
              + jax.nn.sigmoid(zg[:, 2 * d:]) * _dot(yat_ref[...], wa_ref[...]))
    x = x_ref[...] + mod_ref[0, 2:3, :] * _dot(merged, wo_ref[...])
    xo_ref[...] = x
    h2 = _modulate(x, mod_ref[0, 3:4, :], mod_ref[0, 4:5, :]).astype(BF16)
    h2_ref[...] = h2
    q_ref[...] = jnp.dot(h2, wq_ref[...], preferred_element_type=F32)


def merge_project(x, mod_blocks, y_rw, y_cv, y_at, z_gate, w_r, w_c, w_a, w_o, w_q):
    n_tok, d = x.shape
    row = lambda i: (i, 0)
    const = lambda i: (0, 0)
    acts = [y_rw, y_cv, y_at, z_gate]
    weights = [w_r, w_c, w_a, w_o, w_q]
    return pl.pallas_call(
        _merge_kernel,
        grid=(n_tok // ROW_BLOCK,),
        in_specs=[pl.BlockSpec((ROW_BLOCK, d), row),
                  pl.BlockSpec((1, MOD_ROWS, d), lambda i: (i, 0, 0))]
                 + [pl.BlockSpec((ROW_BLOCK, a.shape[1]), row) for a in acts]
                 + [pl.BlockSpec(w.shape, const) for w in weights],
        out_specs=[pl.BlockSpec((ROW_BLOCK, d), row), pl.BlockSpec((ROW_BLOCK, d), row),
                   pl.BlockSpec((ROW_BLOCK, w_q.shape[1]), row)],
        out_shape=[jax.ShapeDtypeStruct((n_tok, d), F32), jax.ShapeDtypeStruct((n_tok, d), BF16),
                   jax.ShapeDtypeStruct((n_tok, w_q.shape[1]), F32)],
        compiler_params=pltpu.CompilerParams(
            dimension_semantics=("parallel",), vmem_limit_bytes=VMEM_LIMIT),
        name="merge_project",
    )(x, mod_blocks, *acts, *weights)


PEER_HEADS = 8
PEER_N_KEYS = 128
PEER_HALF = 64
PEER_TOPK = 16
PEER_SEL_BLOCK = 128
NEG = -1e30


def _dot3_nt(a, b):
    a_hi = a.astype(BF16)
    b_hi = b.astype(BF16)
    a_lo = (a - a_hi.astype(F32)).astype(BF16)
    b_lo = (b - b_hi.astype(F32)).astype(BF16)
    d = lambda x, y: lax.dot_general(x, y, (((1,), (1,)), ((), ())), preferred_element_type=F32)
    return d(a_hi, b_hi) + (d(a_hi, b_lo) + d(a_lo, b_hi))


def _extract_max(s, code):
    m = jnp.max(s, axis=0, keepdims=True)
    pos = jnp.min(jnp.where(s == m, code, 1e9), axis=0, keepdims=True)
    return m, pos, jnp.where(code == pos, NEG, s)


def _pair_candidates(a0, a1):
    k = PEER_TOPK
    sub = lax.broadcasted_iota(jnp.int32, (k, a0.shape[1]), 0).astype(F32)
    sub8 = sub[:8]
    vals = [a0[0:1] + a1]
    codes = [sub]
    for r0 in range(1, 8):
        limit = k // (r0 + 1)
        vals.append(jnp.where(sub8 < limit, a0[r0:r0 + 1] + a1[:8], NEG))
        codes.append(sub8 + float(r0 * k))
    vals.append(a0[8:] + a1[0:1])
    codes.append((sub8 + 8.0) * float(k))
    return jnp.concatenate(vals, axis=0), jnp.concatenate(codes, axis=0)


def _peer_select_kernel(q_ref, keys_ref, w_ref, top_v, top_i, slot_i, slot_j, slot_g,
                        rows_i, rows_j, rows_g, w3_ref):
    k = PEER_TOPK
    ct = q_ref.shape[0]
    key_code = lax.broadcasted_iota(jnp.int32, (PEER_N_KEYS, ct), 0).astype(F32)
    rank = lax.broadcasted_iota(jnp.int32, (k, ct), 0).astype(F32)
    for h in range(PEER_HEADS):
        st = _dot3_nt(keys_ref[h], q_ref[:, h * LANES:(h + 1) * LANES])
        for p in range(2):
            s = st[p * PEER_N_KEYS:(p + 1) * PEER_N_KEYS]
            for it in range(k):
                m, pos, s = _extract_max(s, key_code)
                top_v[p, it:it + 1, :] = m
                top_i[p, it:it + 1, :] = pos
        a0, a1 = top_v[0], top_v[1]
        i0, i1 = top_i[0], top_i[1]
        cand, code = _pair_candidates(a0, a1)
        best = a0[0:1] + a1[0:1]
        z = jnp.zeros_like(best)
        for it in range(k):
            m, pos, cand = _extract_max(cand, code)
            r0 = jnp.floor(pos * (1.0 / k))
            r1 = pos - r0 * k
            e = jnp.exp(m - best)
            z = z + e
            slot = h * k + it
            slot_i[slot:slot + 1, :] = jnp.sum(jnp.where(rank == r0, i0, 0.0), axis=0, keepdims=True)
            slot_j[slot:slot + 1, :] = jnp.sum(jnp.where(rank == r1, i1, 0.0), axis=0, keepdims=True)
            slot_g[slot:slot + 1, :] = e
        slot_g[h * k:(h + 1) * k, :] = slot_g[h * k:(h + 1) * k, :] / z

    rows_i[...] = slot_i[...].T
    rows_j[...] = slot_j[...].T
    rows_g[...] = slot_g[...].T
    n = PEER_N_KEYS
    sub = lax.broadcasted_iota(jnp.int32, (n, n), 0).astype(F32)

    def per_token(c, carry):
        a_t = jnp.where(sub == rows_i[pl.ds(c, 1), :], rows_g[pl.ds(c, 1), :], 0.0)
        b_t = jnp.where(sub == rows_j[pl.ds(c, 1), :], 1.0, 0.0)
        w3_ref[pl.ds(pl.multiple_of(c * n, n), n), :] = _dot_nt(a_t, b_t)
        return carry

    lax.fori_loop(0, ct, per_token, 0)

    def per_key(i, carry):
        w_ref[:, pl.ds(pl.multiple_of(i * n, n), n)] = w3_ref[pl.ds(i, ct, stride=n), :].astype(BF16)
        return carry

    lax.fori_loop(0, n, per_key, 0)


def peer_select(q, keys_padded):
    n_tok, qd = q.shape
    ct = PEER_SEL_BLOCK
    n = PEER_N_KEYS
    slots = PEER_HEADS * PEER_TOPK
    return pl.pallas_call(
        _peer_select_kernel,
        grid=(n_tok // ct,),
        in_specs=[pl.BlockSpec((ct, qd), lambda i: (i, 0)),
                  pl.BlockSpec(keys_padded.shape, lambda i: (0, 0, 0))],
        out_specs=pl.BlockSpec((ct, n * n), lambda i: (i, 0)),
        out_shape=jax.ShapeDtypeStruct((n_tok, n * n), BF16),
        scratch_shapes=[pltpu.VMEM((2, PEER_TOPK, ct), F32), pltpu.VMEM((2, PEER_TOPK, ct), F32),
                        pltpu.VMEM((slots, ct), F32), pltpu.VMEM((slots, ct), F32),
                        pltpu.VMEM((slots, ct), F32),
                        pltpu.VMEM((ct, slots), F32), pltpu.VMEM((ct, slots), F32),
                        pltpu.VMEM((ct, slots), F32),
                        pltpu.VMEM((ct * n, n), F32)],
        compiler_params=pltpu.CompilerParams(
            dimension_semantics=("parallel",), vmem_limit_bytes=VMEM_LIMIT),
        name="peer_select",
    )(q, keys_padded)


PEER_TOKEN_BLOCK = 1024
PEER_EXPERT_BLOCK = 1024


def _peer_kernel(h_ref, w_ref, u_ref, v_ref, x_ref, mod_ref, o_ref, acc_ref):
    e = pl.program_id(1)

    @pl.when(e == 0)
    def _():
        acc_ref[...] = jnp.zeros_like(acc_ref)

    s = _dot_nt(h_ref[...], u_ref[...])
    act = 0.5 * s * (1.0 + lax.erf(s * (2.0 ** -0.5)))
    acc_ref[...] += _dot(act * w_ref[...].astype(F32), v_ref[...])

    @pl.when(e == pl.num_programs(1) - 1)
    def _():
        for j in range(mod_ref.shape[0]):
            rows = slice(j * ROW_BLOCK, (j + 1) * ROW_BLOCK)
            o_ref[rows, :] = x_ref[rows, :] + mod_ref[j, 5:6, :] * acc_ref[rows, :]


def peer_dense(h2, w_sel, u_tab, v_tab, x, mod_blocks):
    n_tok, d = x.shape
    n_exp = u_tab.shape[0]
    tb, eb = PEER_TOKEN_BLOCK, PEER_EXPERT_BLOCK
    tok = lambda i, e: (i, 0)
    return pl.pallas_call(
        _peer_kernel,
        grid=(n_tok // tb, n_exp // eb),
        in_specs=[pl.BlockSpec((tb, d), tok),
                  pl.BlockSpec((tb, eb), lambda i, e: (i, e)),
                  pl.BlockSpec((eb, d), lambda i, e: (e, 0)),
                  pl.BlockSpec((eb, d), lambda i, e: (e, 0)),
                  pl.BlockSpec((tb, d), tok),
                  pl.BlockSpec((tb // ROW_BLOCK, MOD_ROWS, d), lambda i, e: (i, 0, 0))],
        out_specs=pl.BlockSpec((tb, d), tok),
        out_shape=jax.ShapeDtypeStruct((n_tok, d), F32),
        scratch_shapes=[pltpu.VMEM((tb, d), F32)],
        compiler_params=pltpu.CompilerParams(
            dimension_semantics=("parallel", "arbitrary"), vmem_limit_bytes=VMEM_LIMIT),
        name="peer_dense",
    )(h2, w_sel, u_tab, v_tab, x, mod_blocks)


RW_HEADS = 8
RW_WIDTH = RW_HEADS * RW_HEAD_DIM
RW_SIZES = (RW_WIDTH, RW_WIDTH, RW_WIDTH, 64, 64, 64, 64, 128)
CONV_WIDTH = 512
CONV_KERNEL = 31
ATT_HEADS = 8
ATT_KV_HEADS = 2
ATT_GROUP = ATT_HEADS // ATT_KV_HEADS
HEAD_DIM = 64
ROT_HALF = HEAD_DIM // 2
ROPE_THETA = 10000.0
GRID_W = 64


def _split(z, sizes):
    parts, start = [], 0
    for s in sizes:
        parts.append(z[..., start:start + s])
        start += s
    return parts


def _bd_from_states(s):
    a = jnp.swapaxes(s, -1, -2)
    n, d, h, m, _ = a.shape
    a = a.reshape(n, d, h // 2, 2, m, m)
    z = jnp.zeros_like(a[:, :, :, 0])
    top = jnp.concatenate([a[:, :, :, 0], z], axis=-1)
    bot = jnp.concatenate([z, a[:, :, :, 1]], axis=-1)
    return jnp.concatenate([top, bot], axis=-2)


def _states_from_bd(bd):
    n, d, p, _, _ = bd.shape
    m = RW_HEAD_DIM
    a = jnp.stack([bd[:, :, :, :m, :m], bd[:, :, :, m:, m:]], axis=3).reshape(n, d, 2 * p, m, m)
    return jnp.swapaxes(a, -1, -2)


def _rope_tables(t):
    rows = t // GRID_W
    row, col = jnp.meshgrid(jnp.arange(rows), jnp.arange(GRID_W), indexing='ij')
    inv = ROPE_THETA ** (-jnp.arange(0, ROT_HALF, 2, dtype=F32) / ROT_HALF)
    ang_r = row.reshape(-1, 1).astype(F32) * inv
    ang_c = col.reshape(-1, 1).astype(F32) * inv
    return jnp.cos(ang_r), jnp.sin(ang_r), jnp.cos(ang_c), jnp.sin(ang_c)


def _rotate(x, cos, sin):
    x1, x2 = jnp.split(x, 2, axis=-1)
    cos = cos[:, None, :]
    sin = sin[:, None, :]
    return jnp.concatenate([x1 * cos - x2 * sin, x1 * sin + x2 * cos], axis=-1)


def _rope_2d(x, tabs):
    cr, sr, cc, sc = tabs
    xr, xc = jnp.split(x, 2, axis=-1)
    return jnp.concatenate([_rotate(xr, cr, sr), _rotate(xc, cc, sc)], axis=-1)


def _head_rms(x, g):
    return x * lax.rsqrt(jnp.mean(x * x, axis=-1, keepdims=True) + RMS_EPS) * g


def kernel(x_prompt, x_sample, c, cache_k, cache_v, state_wkv, c_ctx, w_mod, b_mod, w_in, shift_mu_prev, shift_mu_next, rw_w0, rw_w2, rw_a0, rw_a2, rw_g2, rw_k_k, rw_k_a, rw_r_k, rw_gn_w, rw_gn_b, w_out_rwkv, conv_w, conv_b, conv_ln_w, conv_ln_b, w_out_conv, q_norm, k_norm, w_out_attn, w_o, peer_wq, peer_keys, peer_u, peer_v):
    nb, seq, d = x_prompt.shape
    db, dseq, _ = x_sample.shape
    depth = w_in.shape[0]
    n_ctx = nb * seq
    n_lat = db * dseq
    n_tok = n_ctx + n_lat
    x = jnp.concatenate([x_prompt.reshape(n_ctx, d), x_sample.reshape(n_lat, d)], axis=0)

    cond = jnp.concatenate([c_ctx[None], c, jnp.zeros((MOD_ROWS - 1 - db, d), F32)], axis=0)
    blk_cond = jnp.concatenate([jnp.zeros((n_ctx // ROW_BLOCK,), jnp.int32),
                                1 + jnp.arange(n_lat // ROW_BLOCK, dtype=jnp.int32) // (dseq // ROW_BLOCK)])
    pos = jnp.concatenate([jnp.arange(n_ctx) % seq, jnp.arange(n_lat) % dseq])
    is_first = (pos == 0)[:, None]
    is_last = jnp.concatenate([jnp.arange(n_ctx) % seq == seq - 1,
                               jnp.arange(n_lat) % dseq == dseq - 1])[:, None]
    rope = _rope_tables(dseq)
    in_sizes = (sum(RW_SIZES), 2 * CONV_WIDTH, (ATT_HEADS + 2 * ATT_KV_HEADS) * HEAD_DIM, 3 * d)

    ctx_k, ctx_v, ctx_s = [], [], []
    for l in range(depth):
        mod = modulation(cond, w_mod[l].astype(BF16), b_mod[l][None])
        mod = mod.reshape(MOD_ROWS, 6, d)[blk_cond]
        mod_blocks = jnp.concatenate([mod, jnp.zeros((mod.shape[0], MOD_ROWS - 6, d), F32)], axis=1)

        z_rw, z_cv, z_qkv, z_gate = in_projection(x, mod_blocks, _split(w_in[l].astype(BF16), in_sizes))

        z_prev = jnp.where(is_first, 0.0, jnp.roll(z_rw, 1, axis=0))
        z_next = jnp.where(is_last, 0.0, jnp.roll(z_rw, -1, axis=0))
        zs = z_rw + shift_mu_prev[l] * (z_prev - z_rw) + shift_mu_next[l] * (z_next - z_rw)
        r, k, v, zw_f, zw_b, za_f, za_b, zg = _split(zs, RW_SIZES)
        kk = (k * rw_k_k[l]).reshape(n_tok, RW_HEADS, RW_HEAD_DIM)
        kk = (kk * lax.rsqrt(jnp.sum(kk * kk, axis=-1, keepdims=True) + 1e-12)).reshape(n_tok, RW_WIDTH)
        lws, aas, kds = [], [], []
        for dd, (zw, za) in enumerate(((zw_f, za_f), (zw_b, za_b))):
            w_raw = rw_w0[l, dd] + jnp.tanh(zw) @ rw_w2[l, dd]
            lws.append(-math.exp(-0.5) * jax.nn.sigmoid(w_raw))
            a = jax.nn.sigmoid(rw_a0[l, dd] + za @ rw_a2[l, dd])
            aas.append(a)
            kds.append(k * (1 + (a - 1) * rw_k_a[l]))
        lw, aa, kd = jnp.stack(lws), jnp.stack(aas), jnp.stack(kds)
        s0_ctx = jnp.zeros((nb, 2, RW_HEADS // 2, LANES, LANES), F32)
        of_c, ob_c, sfin = rwkv_scan(r[:n_ctx], v[:n_ctx], kk[:n_ctx], lw[:, :n_ctx], aa[:, :n_ctx],
                                     kd[:, :n_ctx], s0_ctx, nb, seq)
        of_l, ob_l, _ = rwkv_scan(r[n_ctx:], v[n_ctx:], kk[n_ctx:], lw[:, n_ctx:], aa[:, n_ctx:],
                                  kd[:, n_ctx:], _bd_from_states(state_wkv[:, l]), db, dseq)
        ctx_s.append(_states_from_bd(sfin))
        o = jnp.concatenate([of_c + ob_c, of_l + ob_l], axis=0).reshape(n_tok, RW_HEADS, RW_HEAD_DIM)
        mu = jnp.mean(o, axis=-1, keepdims=True)
        var = jnp.mean(jnp.square(o - mu), axis=-1, keepdims=True)
        o = ((o - mu) * lax.rsqrt(var + RW_GN_EPS)).reshape(n_tok, RW_WIDTH) * rw_gn_w[l] + rw_gn_b[l]
        bonus = jnp.sum((r * k * rw_r_k[l]).reshape(n_tok, RW_HEADS, RW_HEAD_DIM), axis=-1, keepdims=True)
        bonus = (bonus * v.reshape(n_tok, RW_HEADS, RW_HEAD_DIM)).reshape(n_tok, RW_WIDTH)
        y_rw = (o + bonus) * (jax.nn.sigmoid(zg) @ rw_g2[l])

        ca, cb = jnp.split(z_cv, 2, axis=-1)
        u = ca * jax.nn.sigmoid(cb)

        def conv(us):
            return lax.conv_general_dilated(
                us, conv_w[l][:, None, :], window_strides=(1,),
                padding=[(CONV_KERNEL // 2, CONV_KERNEL // 2)],
                dimension_numbers=('NWC', 'WIO', 'NWC'), feature_group_count=CONV_WIDTH)

        u = jnp.concatenate([conv(u[:n_ctx].reshape(nb, seq, CONV_WIDTH)).reshape(n_ctx, CONV_WIDTH),
                             conv(u[n_ctx:].reshape(db, dseq, CONV_WIDTH)).reshape(n_lat, CONV_WIDTH)])
        u = u + conv_b[l]
        mu = jnp.mean(u, axis=-1, keepdims=True)
        var = jnp.mean(jnp.square(u - mu), axis=-1, keepdims=True)
        u = (u - mu) * lax.rsqrt(var + LN_EPS) * conv_ln_w[l] + conv_ln_b[l]
        y_cv = u * jax.nn.sigmoid(u)

        nq, nkv = ATT_HEADS * HEAD_DIM, ATT_KV_HEADS * HEAD_DIM
        q = _head_rms(z_qkv[:, :nq].reshape(n_tok, ATT_HEADS, HEAD_DIM), q_norm[l])
        kx = _head_rms(z_qkv[:, nq:nq + nkv].reshape(n_tok, ATT_KV_HEADS, HEAD_DIM), k_norm[l])
        vx = z_qkv[:, nq + nkv:].reshape(n_tok, ATT_KV_HEADS, HEAD_DIM)
        ctx_k.append(kx[:n_ctx].reshape(nb, seq, ATT_KV_HEADS, HEAD_DIM))
        ctx_v.append(vx[:n_ctx].reshape(nb, seq, ATT_KV_HEADS, HEAD_DIM))

        def heads_q(t, b_, t_):
            return t.reshape(b_, t_, ATT_KV_HEADS, ATT_GROUP, HEAD_DIM).transpose(0, 2, 3, 1, 4)

        def heads_kv(t):
            return t.transpose(0, 2, 1, 3)

        y_c = attention(heads_q(q[:n_ctx], nb, seq),
                        heads_kv(kx[:n_ctx].reshape(nb, seq, ATT_KV_HEADS, HEAD_DIM)),
                        heads_kv(vx[:n_ctx].reshape(nb, seq, ATT_KV_HEADS, HEAD_DIM)))
        q_l = _rope_2d(q[n_ctx:].reshape(db, dseq, ATT_HEADS, HEAD_DIM).reshape(db * dseq, ATT_HEADS, HEAD_DIM)
                       .reshape(db, dseq, ATT_HEADS, HEAD_DIM), rope)
        k_l = _rope_2d(kx[n_ctx:].reshape(db, dseq, ATT_KV_HEADS, HEAD_DIM), rope)
        keys = jnp.concatenate([k_l, cache_k[:, l]], axis=1)
        vals = jnp.concatenate([vx[n_ctx:].reshape(db, dseq, ATT_KV_HEADS, HEAD_DIM), cache_v[:, l]], axis=1)
        y_l = attention(heads_q(q_l.reshape(n_lat, ATT_HEADS, HEAD_DIM), db, dseq), heads_kv(keys), heads_kv(vals))
        unheads = lambda y, n: y.transpose(0, 3, 1, 2, 4).reshape(n, nq)
        y_at = jnp.concatenate([unheads(y_c, n_ctx), unheads(y_l, n_lat)], axis=0)

        x, h2, pq = merge_project(x, mod_blocks, y_rw, y_cv, y_at, z_gate,
                                  w_out_rwkv[l].astype(BF16), w_out_conv[l].astype(BF16),
                                  w_out_attn[l].astype(BF16), w_o[l].astype(BF16), peer_wq[l].astype(BF16))

        kz = jnp.zeros_like(peer_keys[l][:, 0])
        keys_padded = jnp.concatenate([jnp.concatenate([peer_keys[l][:, 0], kz], axis=-1),
                                       jnp.concatenate([kz, peer_keys[l][:, 1]], axis=-1)], axis=1)
        w_sel = peer_select(pq, keys_padded)
        x = peer_dense(h2, w_sel, peer_u[l].astype(BF16), peer_v[l].astype(BF16), x, mod_blocks)

    new_cache_k = jnp.stack(ctx_k, axis=1)
    new_cache_v = jnp.stack(ctx_v, axis=1)
    new_state = jnp.stack(ctx_s, axis=1)
    return (x[:n_ctx].reshape(nb, seq, d), x[n_ctx:].reshape(db, dseq, d),
            new_cache_k, new_cache_v, new_state)
```

```python
import functools
import math

import jax
import jax.numpy as jnp
from jax import lax
from jax.experimental import pallas as pl
from jax.experimental.pallas import tpu as pltpu

F32 = jnp.float32
BF16 = jnp.bfloat16

LANES = 128
SCAN_CHUNK = 64
SCAN_SUB = 16
RW_HEAD_DIM = 64
HEADS_PER_PAIR = LANES // RW_HEAD_DIM
VMEM_LIMIT = 56 * 1024 * 1024


def _dot(a, b):
    return jnp.dot(a.astype(BF16), b.astype(BF16), preferred_element_type=F32)


def _dot_nt(a, b):
    return lax.dot_general(a.astype(BF16), b.astype(BF16), (((1,), (1,)), ((), ())),
                           preferred_element_type=F32)


def _dot_tn(a, b):
    return lax.dot_general(a.astype(BF16), b.astype(BF16), (((0,), (0,)), ((), ())),
                           preferred_element_type=F32)


def _dot3(a, b):
    a_hi = a.astype(BF16)
    b_hi = b.astype(BF16)
    a_lo = (a - a_hi.astype(F32)).astype(BF16)
    b_lo = (b - b_hi.astype(F32)).astype(BF16)
    d = functools.partial(jnp.dot, preferred_element_type=F32)
    return d(a_hi, b_hi) + (d(a_hi, b_lo) + d(a_lo, b_hi))


def _scan_masks(reverse):
    C = SCAN_CHUNK
    P = HEADS_PER_PAIR * C
    row = lax.broadcasted_iota(jnp.int32, (P, LANES), 0)
    col = lax.broadcasted_iota(jnp.int32, (P, LANES), 1)
    tr = row % C
    tc = col % C
    t_i = lax.broadcasted_iota(jnp.int32, (C, C), 0)
    s_i = lax.broadcasted_iota(jnp.int32, (C, C), 1)
    return dict(
        same_head=(row // C) == (col // RW_HEAD_DIM),
        incl=(tr >= tc) if not reverse else (tr <= tc),
        strict=(tr > tc) if not reverse else (tr < tc),
        diag_blk=(tr // SCAN_SUB) == (tc // SCAN_SUB),
        eye=row == col,
        cum_mat=jnp.where((t_i >= s_i) if not reverse else (t_i <= s_i), 1.0, 0.0).astype(F32))


def _scan_chunk(chains, states):
    C = SCAN_CHUNK
    n = range(len(chains))
    mk = [ch[6] for ch in chains]
    cum = [_dot3(mk[i]["cum_mat"], chains[i][1]) for i in n]
    pre = []
    for i in n:
        r, lw, k, v, kk, a, m, reverse = chains[i]
        g = jnp.exp(cum[i])
        g_prev = jnp.exp(cum[i] - lw)
        g_inv = jnp.exp(-cum[i])
        g_end = jnp.exp(cum[i][C - 1:C, :] if not reverse else cum[i][0:1, :])
        bt = kk * a * g_inv
        kt = k * g_inv
        stack = lambda x, m=m: jnp.where(m["same_head"], jnp.concatenate([x, x], axis=0), 0.0)
        pre.append(dict(kap=stack(kk * g_prev), rt=stack(r * g), bt=stack(bt), kt=stack(kt),
                        v=stack(v), bh=stack(bt * g_end), kh=stack(kt * g_end), g_end=g_end))
    lb = [jnp.where(mk[i]["strict"], _dot_nt(pre[i]["kap"], pre[i]["bt"]), 0.0) for i in n]
    lk = [jnp.where(mk[i]["strict"], _dot_nt(pre[i]["kap"], pre[i]["kt"]), 0.0) for i in n]
    pb = [jnp.where(mk[i]["incl"], _dot_nt(pre[i]["rt"], pre[i]["bt"]), 0.0) for i in n]
    pk = [jnp.where(mk[i]["incl"], _dot_nt(pre[i]["rt"], pre[i]["kt"]), 0.0) for i in n]
    lkv = [_dot(lk[i], pre[i]["v"]) for i in n]

    d = [jnp.where(mk[i]["diag_blk"], lb[i], 0.0) for i in n]
    x = [jnp.concatenate([lb[i] - d[i], pre[i]["kap"], lkv[i]], axis=1) for i in n]
    x = [x[i] - _dot(d[i], x[i]) for i in n]
    p = [_dot(d[i], d[i]) for i in n]
    x = [x[i] + _dot(p[i], x[i]) for i in n]
    p = [_dot(p[i], p[i]) for i in n]
    x = [x[i] + _dot(p[i], x[i]) for i in n]
    p = [_dot(p[i], p[i]) for i in n]
    x = [x[i] + _dot(p[i], x[i]) for i in n]
    e = [x[i][:, :LANES] for i in n]
    rhs = [x[i][:, LANES:] for i in n]
    e2 = [_dot(e[i], e[i]) for i in n]
    t = [rhs[i] + _dot(e2[i], rhs[i]) for i in n]
    gy = [t[i] - _dot(e[i], t[i]) for i in n]

    m_mat = [jnp.where(mk[i]["eye"], pre[i]["g_end"], 0.0) - _dot_tn(pre[i]["bh"], gy[i][:, :LANES])
             for i in n]
    n_mat = [_dot_tn(pre[i]["kh"], pre[i]["v"]) - _dot_tn(pre[i]["bh"], gy[i][:, LANES:]) for i in n]
    pbgy = [_dot(pb[i], gy[i]) for i in n]
    q = [pre[i]["rt"] - pbgy[i][:, :LANES] for i in n]
    z = [_dot(pk[i], pre[i]["v"]) - pbgy[i][:, LANES:] for i in n]
    o_st = [_dot3(q[i], states[i]) + z[i] for i in n]
    new = [_dot3(m_mat[i], states[i]) + n_mat[i] for i in n]
    return [(o_st[i][:C] + o_st[i][C:], new[i]) for i in n]


def _scan_kernel(rf, vf, kkf, lwf, af, kf, rb, vb, kkb, lwb, ab, kb, s0, of, ob, sfin, state):
    c = pl.program_id(1)
    n_pairs = rf.shape[-1] // LANES

    @pl.when(c == 0)
    def _():
        state[...] = s0[0]

    chains, states, outs = [], [], []
    for d, (r_, v_, kk_, lw_, a_, k_, o_) in enumerate(((rf, vf, kkf, lwf, af, kf, of),
                                                        (rb, vb, kkb, lwb, ab, kb, ob))):
        masks = _scan_masks(reverse=(d == 1))
        for p in range(n_pairs):
            sl = slice(p * LANES, (p + 1) * LANES)
            chains.append((r_[:, sl], lw_[:, sl], k_[:, sl], v_[:, sl], kk_[:, sl], a_[:, sl],
                           masks, d == 1))
            states.append(state[d, p])
            outs.append((o_, sl, d, p))
    for (o, new), (o_, sl, d, p) in zip(_scan_chunk(chains, states), outs):
        o_[:, sl] = o
        state[d, p] = new

    @pl.when(c == pl.num_programs(1) - 1)
    def _():
        sfin[0] = state[...]


def rwkv_scan(r, v, kk, lw, a, kd, s0_bd, n_seq, seq_len):
    n_tok, width = r.shape
    n_pairs = width // LANES
    n_chunk = seq_len // SCAN_CHUNK
    fwd = lambda s, c: (s * n_chunk + c, 0)
    bwd = lambda s, c: (s * n_chunk + n_chunk - 1 - c, 0)
    blk = (SCAN_CHUNK, width)
    sspec = pl.BlockSpec((1, 2, n_pairs, LANES, LANES), lambda s, c: (s, 0, 0, 0, 0))
    in_specs = ([pl.BlockSpec(blk, fwd)] * 6 + [pl.BlockSpec(blk, bwd)] * 6 + [sspec])
    out_specs = [pl.BlockSpec(blk, fwd), pl.BlockSpec(blk, bwd), sspec]
    out_shape = [jax.ShapeDtypeStruct((n_tok, width), F32)] * 2 + [
        jax.ShapeDtypeStruct(s0_bd.shape, F32)]
    return pl.pallas_call(
        _scan_kernel,
        grid=(n_seq, n_chunk),
        in_specs=in_specs,
        out_specs=out_specs,
        out_shape=out_shape,
        scratch_shapes=[pltpu.VMEM((2, n_pairs, LANES, LANES), F32)],
        compiler_params=pltpu.CompilerParams(
            dimension_semantics=("parallel", "arbitrary"), vmem_limit_bytes=VMEM_LIMIT),
        name="rwkv_scan",
    )(r, v, kk, lw[0], a[0], kd[0], r, v, kk, lw[1], a[1], kd[1], s0_bd)


MOD_ROWS = 8
ROW_BLOCK = 256
RMS_EPS = 1e-6
LN_EPS = 1e-5
RW_GN_EPS = 64e-5


def _mod_kernel(c_ref, w_ref, b_ref, o_ref):
    c = c_ref[...]
    s = c * jax.nn.sigmoid(c)
    o_ref[...] = _dot(s, w_ref[...]) + b_ref[...]


def modulation(cond, w_mod, b_mod):
    n, d = cond.shape
    cols = w_mod.shape[1]
    blk = 1024
    return pl.pallas_call(
        _mod_kernel,
        grid=(cols // blk,),
        in_specs=[pl.BlockSpec((n, d), lambda j: (0, 0)),
                  pl.BlockSpec((d, blk), lambda j: (0, j)),
                  pl.BlockSpec((1, blk), lambda j: (0, j))],
        out_specs=pl.BlockSpec((n, blk), lambda j: (0, j)),
        out_shape=jax.ShapeDtypeStruct((n, cols), F32),
        name="modulation",
    )(cond, w_mod, b_mod)


def _modulate(x, shift, scale):
    ms = jnp.mean(x * x, axis=-1, keepdims=True)
    return x * lax.rsqrt(ms + RMS_EPS) * (1.0 + scale) + shift


def _inproj_kernel(x_ref, mod_ref, *refs):
    n = len(refs) // 2
    h = _modulate(x_ref[...], mod_ref[0, 0:1, :], mod_ref[0, 1:2, :]).astype(BF16)
    for w_ref, o_ref in zip(refs[:n], refs[n:]):
        o_ref[...] = jnp.dot(h, w_ref[...], preferred_element_type=F32)


def in_projection(x, mod_blocks, weights):
    n_tok, d = x.shape
    row = lambda i: (i, 0)
    const = lambda i: (0, 0)
    return pl.pallas_call(
        _inproj_kernel,
        grid=(n_tok // ROW_BLOCK,),
        in_specs=[pl.BlockSpec((ROW_BLOCK, d), row),
                  pl.BlockSpec((1, MOD_ROWS, d), lambda i: (i, 0, 0))]
                 + [pl.BlockSpec(w.shape, const, pipeline_mode=pl.Buffered(1)) for w in weights],
        out_specs=[pl.BlockSpec((ROW_BLOCK, w.shape[1]), row) for w in weights],
        out_shape=[jax.ShapeDtypeStruct((n_tok, w.shape[1]), F32) for w in weights],
        compiler_params=pltpu.CompilerParams(
            dimension_semantics=("parallel",), vmem_limit_bytes=VMEM_LIMIT),
        name="in_projection",
    )(x, mod_blocks, *weights)


ATT_Q_BLOCK = 256


def _attn_kernel(q_ref, k_ref, v_ref, o_ref):
    g, tq, dh = q_ref.shape[2:]
    q = q_ref[0, 0].reshape(g * tq, dh) * (dh ** -0.5)
    s = _dot_nt(q, k_ref[0, 0])
    m = jnp.max(s, axis=-1, keepdims=True)
    p = jnp.exp(s - m)
    p = p / jnp.sum(p, axis=-1, keepdims=True)
    o_ref[0, 0] = _dot(p, v_ref[0, 0]).reshape(g, tq, dh)


def attention(q, k, v):
    b, kvh, g, tq, dh = q.shape
    tk = k.shape[2]
    qb = min(ATT_Q_BLOCK, tq)
    qspec = pl.BlockSpec((1, 1, g, qb, dh), lambda i, j, t: (i, j, 0, t, 0))
    kspec = pl.BlockSpec((1, 1, tk, dh), lambda i, j, t: (i, j, 0, 0))
    return pl.pallas_call(
        _attn_kernel,
        grid=(b, kvh, tq // qb),
        in_specs=[qspec, kspec, kspec],
        out_specs=qspec,
        out_shape=jax.ShapeDtypeStruct(q.shape, F32),
        compiler_params=pltpu.CompilerParams(
            dimension_semantics=("parallel", "parallel", "arbitrary"),
            vmem_limit_bytes=VMEM_LIMIT),
        name="attention",
    )(q, k, v)


def _merge_kernel(x_ref, mod_ref, yrw_ref, ycv_ref, yat_ref, zg_ref,
                  wr_ref, wc_ref, wa_ref, wo_ref, wq_ref, xo_ref, h2_ref, q_ref):
    d = x_ref.shape[1]
    zg = zg_ref[...]
    merged = (jax.nn.sigmoid(zg[:, :d]) * _dot(yrw_ref[...], wr_ref[...])
              + jax.nn.sigmoid(zg[:, d:2 * d]) * _dot(ycv_ref[...], wc_ref[...])
              + jax.nn.sigmoid(zg[:, 2 * d:]) * _dot(yat_ref[...], wa_ref[...]))
    x = x_ref[...] + mod_ref[0, 2:3, :] * _dot(merged, wo_ref[...])
    xo_ref[...] = x
    h2 = _modulate(x, mod_ref[0, 3:4, :], mod_ref[0, 4:5, :]).astype(BF16)
    h2_ref[...] = h2
    q_ref[...] = jnp.dot(h2, wq_ref[...], preferred_element_type=F32)


def merge_project(x, mod_blocks, y_rw, y_cv, y_at, z_gate, w_r, w_c, w_a, w_o, w_q):
    n_tok, d = x.shape
    row = lambda i: (i, 0)
    const = lambda i: (0, 0)
    acts = [y_rw, y_cv, y_at, z_gate]
    weights = [w_r, w_c, w_a, w_o, w_q]
    return pl.pallas_call(
        _merge_kernel,
        grid=(n_tok // ROW_BLOCK,),
        in_specs=[pl.BlockSpec((ROW_BLOCK, d), row),
                  pl.BlockSpec((1, MOD_ROWS, d), lambda i: (i, 0, 0))]
                 + [pl.BlockSpec((ROW_BLOCK, a.shape[1]), row) for a in acts]
                 + [pl.BlockSpec(w.shape, const) for w in weights],
        out_specs=[pl.BlockSpec((ROW_BLOCK, d), row), pl.BlockSpec((ROW_BLOCK, d), row),
                   pl.BlockSpec((ROW_BLOCK, w_q.shape[1]), row)],
        out_shape=[jax.ShapeDtypeStruct((n_tok, d), F32), jax.ShapeDtypeStruct((n_tok, d), BF16),
                   jax.ShapeDtypeStruct((n_tok, w_q.shape[1]), F32)],
        compiler_params=pltpu.CompilerParams(
            dimension_semantics=("parallel",), vmem_limit_bytes=VMEM_LIMIT),
        name="merge_project",
    )(x, mod_blocks, *acts, *weights)


PEER_HEADS = 8
PEER_N_KEYS = 128
PEER_HALF = 64
PEER_TOPK = 16
PEER_SEL_BLOCK = 128
PEER_LOOP_UNROLL = 8
NEG = -1e30


def _dot3_nt(a, b):
    a_hi = a.astype(BF16)
    b_hi = b.astype(BF16)
    a_lo = (a - a_hi.astype(F32)).astype(BF16)
    b_lo = (b - b_hi.astype(F32)).astype(BF16)
    d = lambda x, y: lax.dot_general(x, y, (((1,), (1,)), ((), ())), preferred_element_type=F32)
    return d(a_hi, b_hi) + (d(a_hi, b_lo) + d(a_lo, b_hi))


def _extract_max(s, code):
    m = jnp.max(s, axis=0, keepdims=True)
    pos = jnp.min(jnp.where(s == m, code, 1e9), axis=0, keepdims=True)
    return m, pos, jnp.where(code == pos, NEG, s)


def _pair_candidates(a0, a1):
    k = PEER_TOPK
    sub = lax.broadcasted_iota(jnp.int32, (k, a0.shape[1]), 0).astype(F32)
    sub8 = sub[:8]
    vals = [a0[0:1] + a1]
    codes = [sub]
    for r0 in range(1, 8):
        limit = k // (r0 + 1)
        vals.append(jnp.where(sub8 < limit, a0[r0:r0 + 1] + a1[:8], NEG))
        codes.append(sub8 + float(r0 * k))
    vals.append(a0[8:] + a1[0:1])
    codes.append((sub8 + 8.0) * float(k))
    return jnp.concatenate(vals, axis=0), jnp.concatenate(codes, axis=0)


def _peer_select_kernel(q_ref, keys_ref, w_ref, top_v, top_i, slot_i, slot_j, slot_g,
                        rows_i, rows_j, rows_g, w3_ref):
    k = PEER_TOPK
    ct = q_ref.shape[0]
    key_code = lax.broadcasted_iota(jnp.int32, (PEER_N_KEYS, ct), 0).astype(F32)
    rank = lax.broadcasted_iota(jnp.int32, (k, ct), 0).astype(F32)
    for h in range(PEER_HEADS):
        st = _dot3_nt(keys_ref[h], q_ref[:, h * LANES:(h + 1) * LANES])
        for p in range(2):
            s = st[p * PEER_N_KEYS:(p + 1) * PEER_N_KEYS]
            for it in range(k):
                m, pos, s = _extract_max(s, key_code)
                top_v[p, it:it + 1, :] = m
                top_i[p, it:it + 1, :] = pos
        a0, a1 = top_v[0], top_v[1]
        i0, i1 = top_i[0], top_i[1]
        cand, code = _pair_candidates(a0, a1)
        best = a0[0:1] + a1[0:1]
        z = jnp.zeros_like(best)
        for it in range(k):
            m, pos, cand = _extract_max(cand, code)
            r0 = jnp.floor(pos * (1.0 / k))
            r1 = pos - r0 * k
            e = jnp.exp(m - best)
            z = z + e
            slot = h * k + it
            slot_i[slot:slot + 1, :] = jnp.sum(jnp.where(rank == r0, i0, 0.0), axis=0, keepdims=True)
            slot_j[slot:slot + 1, :] = jnp.sum(jnp.where(rank == r1, i1, 0.0), axis=0, keepdims=True)
            slot_g[slot:slot + 1, :] = e
        slot_g[h * k:(h + 1) * k, :] = slot_g[h * k:(h + 1) * k, :] / z

    rows_i[...] = slot_i[...].T
    rows_j[...] = slot_j[...].T
    rows_g[...] = slot_g[...].T
    n = PEER_N_KEYS
    sub = lax.broadcasted_iota(jnp.int32, (n, n), 0).astype(F32)

    def per_token(c, carry):
        a_t = jnp.where(sub == rows_i[pl.ds(c, 1), :], rows_g[pl.ds(c, 1), :], 0.0)
        b_t = jnp.where(sub == rows_j[pl.ds(c, 1), :], 1.0, 0.0)
        w3_ref[pl.ds(pl.multiple_of(c * n, n), n), :] = _dot_nt(a_t, b_t)
        return carry

    lax.fori_loop(0, ct, per_token, 0, unroll=PEER_LOOP_UNROLL)

    def per_key(i, carry):
        w_ref[:, pl.ds(pl.multiple_of(i * n, n), n)] = w3_ref[pl.ds(i, ct, stride=n), :].astype(BF16)
        return carry

    lax.fori_loop(0, n, per_key, 0, unroll=PEER_LOOP_UNROLL)


def peer_select(q, keys_padded):
    n_tok, qd = q.shape
    ct = PEER_SEL_BLOCK
    n = PEER_N_KEYS
    slots = PEER_HEADS * PEER_TOPK
    return pl.pallas_call(
        _peer_select_kernel,
        grid=(n_tok // ct,),
        in_specs=[pl.BlockSpec((ct, qd), lambda i: (i, 0)),
                  pl.BlockSpec(keys_padded.shape, lambda i: (0, 0, 0))],
        out_specs=pl.BlockSpec((ct, n * n), lambda i: (i, 0)),
        out_shape=jax.ShapeDtypeStruct((n_tok, n * n), BF16),
        scratch_shapes=[pltpu.VMEM((2, PEER_TOPK, ct), F32), pltpu.VMEM((2, PEER_TOPK, ct), F32),
                        pltpu.VMEM((slots, ct), F32), pltpu.VMEM((slots, ct), F32),
                        pltpu.VMEM((slots, ct), F32),
                        pltpu.VMEM((ct, slots), F32), pltpu.VMEM((ct, slots), F32),
                        pltpu.VMEM((ct, slots), F32),
                        pltpu.VMEM((ct * n, n), F32)],
        compiler_params=pltpu.CompilerParams(
            dimension_semantics=("parallel",), vmem_limit_bytes=VMEM_LIMIT),
        name="peer_select",
    )(q, keys_padded)


PEER_TOKEN_BLOCK = 1024
PEER_EXPERT_BLOCK = 1024


def _peer_kernel(h_ref, w_ref, u_ref, v_ref, x_ref, mod_ref, o_ref, acc_ref):
    e = pl.program_id(1)

    @pl.when(e == 0)
    def _():
        acc_ref[...] = jnp.zeros_like(acc_ref)

    s = _dot_nt(h_ref[...], u_ref[...])
    act = 0.5 * s * (1.0 + lax.erf(s * (2.0 ** -0.5)))
    acc_ref[...] += _dot(act * w_ref[...].astype(F32), v_ref[...])

    @pl.when(e == pl.num_programs(1) - 1)
    def _():
        for j in range(mod_ref.shape[0]):
            rows = slice(j * ROW_BLOCK, (j + 1) * ROW_BLOCK)
            o_ref[rows, :] = x_ref[rows, :] + mod_ref[j, 5:6, :] * acc_ref[rows, :]


def peer_dense(h2, w_sel, u_tab, v_tab, x, mod_blocks):
    n_tok, d = x.shape
    n_exp = u_tab.shape[0]
    tb, eb = PEER_TOKEN_BLOCK, PEER_EXPERT_BLOCK
    tok = lambda i, e: (i, 0)
    return pl.pallas_call(
        _peer_kernel,
        grid=(n_tok // tb, n_exp // eb),
        in_specs=[pl.BlockSpec((tb, d), tok),
                  pl.BlockSpec((tb, eb), lambda i, e: (i, e)),
                  pl.BlockSpec((eb, d), lambda i, e: (e, 0)),
                  pl.BlockSpec((eb, d), lambda i, e: (e, 0)),
                  pl.BlockSpec((tb, d), tok),
                  pl.BlockSpec((tb // ROW_BLOCK, MOD_ROWS, d), lambda i, e: (i, 0, 0))],
        out_specs=pl.BlockSpec((tb, d), tok),
        out_shape=jax.ShapeDtypeStruct((n_tok, d), F32),
        scratch_shapes=[pltpu.VMEM((tb, d), F32)],
        compiler_params=pltpu.CompilerParams(
            dimension_semantics=("parallel", "arbitrary"), vmem_limit_bytes=VMEM_LIMIT),
        name="peer_dense",
    )(h2, w_sel, u_tab, v_tab, x, mod_blocks)


RW_HEADS = 8
RW_WIDTH = RW_HEADS * RW_HEAD_DIM
RW_SIZES = (RW_WIDTH, RW_WIDTH, RW_WIDTH, 64, 64, 64, 64, 128)
CONV_WIDTH = 512
CONV_KERNEL = 31
ATT_HEADS = 8
ATT_KV_HEADS = 2
ATT_GROUP = ATT_HEADS // ATT_KV_HEADS
HEAD_DIM = 64
ROT_HALF = HEAD_DIM // 2
ROPE_THETA = 10000.0
GRID_W = 64


def _split(z, sizes):
    parts, start = [], 0
    for s in sizes:
        parts.append(z[..., start:start + s])
        start += s
    return parts


def _bd_from_states(s):
    a = jnp.swapaxes(s, -1, -2)
    n, d, h, m, _ = a.shape
    a = a.reshape(n, d, h // 2, 2, m, m)
    z = jnp.zeros_like(a[:, :, :, 0])
    top = jnp.concatenate([a[:, :, :, 0], z], axis=-1)
    bot = jnp.concatenate([z, a[:, :, :, 1]], axis=-1)
    return jnp.concatenate([top, bot], axis=-2)


def _states_from_bd(bd):
    n, d, p, _, _ = bd.shape
    m = RW_HEAD_DIM
    a = jnp.stack([bd[:, :, :, :m, :m], bd[:, :, :, m:, m:]], axis=3).reshape(n, d, 2 * p, m, m)
    return jnp.swapaxes(a, -1, -2)


def _rope_tables(t):
    rows = t // GRID_W
    row, col = jnp.meshgrid(jnp.arange(rows), jnp.arange(GRID_W), indexing='ij')
    inv = ROPE_THETA ** (-jnp.arange(0, ROT_HALF, 2, dtype=F32) / ROT_HALF)
    ang_r = row.reshape(-1, 1).astype(F32) * inv
    ang_c = col.reshape(-1, 1).astype(F32) * inv
    return jnp.cos(ang_r), jnp.sin(ang_r), jnp.cos(ang_c), jnp.sin(ang_c)


def _rotate(x, cos, sin):
    x1, x2 = jnp.split(x, 2, axis=-1)
    cos = cos[:, None, :]
    sin = sin[:, None, :]
    return jnp.concatenate([x1 * cos - x2 * sin, x1 * sin + x2 * cos], axis=-1)


def _rope_2d(x, tabs):
    cr, sr, cc, sc = tabs
    xr, xc = jnp.split(x, 2, axis=-1)
    return jnp.concatenate([_rotate(xr, cr, sr), _rotate(xc, cc, sc)], axis=-1)


def _head_rms(x, g):
    return x * lax.rsqrt(jnp.mean(x * x, axis=-1, keepdims=True) + RMS_EPS) * g


def kernel(x_prompt, x_sample, c, cache_k, cache_v, state_wkv, c_ctx, w_mod, b_mod, w_in, shift_mu_prev, shift_mu_next, rw_w0, rw_w2, rw_a0, rw_a2, rw_g2, rw_k_k, rw_k_a, rw_r_k, rw_gn_w, rw_gn_b, w_out_rwkv, conv_w, conv_b, conv_ln_w, conv_ln_b, w_out_conv, q_norm, k_norm, w_out_attn, w_o, peer_wq, peer_keys, peer_u, peer_v):
    nb, seq, d = x_prompt.shape
    db, dseq, _ = x_sample.shape
    depth = w_in.shape[0]
    n_ctx = nb * seq
    n_lat = db * dseq
    n_tok = n_ctx + n_lat
    x = jnp.concatenate([x_prompt.reshape(n_ctx, d), x_sample.reshape(n_lat, d)], axis=0)

    cond = jnp.concatenate([c_ctx[None], c, jnp.zeros((MOD_ROWS - 1 - db, d), F32)], axis=0)
    blk_cond = jnp.concatenate([jnp.zeros((n_ctx // ROW_BLOCK,), jnp.int32),
                                1 + jnp.arange(n_lat // ROW_BLOCK, dtype=jnp.int32) // (dseq // ROW_BLOCK)])
    pos = jnp.concatenate([jnp.arange(n_ctx) % seq, jnp.arange(n_lat) % dseq])
    is_first = (pos == 0)[:, None]
    is_last = jnp.concatenate([jnp.arange(n_ctx) % seq == seq - 1,
                               jnp.arange(n_lat) % dseq == dseq - 1])[:, None]
    rope = _rope_tables(dseq)
    in_sizes = (sum(RW_SIZES), 2 * CONV_WIDTH, (ATT_HEADS + 2 * ATT_KV_HEADS) * HEAD_DIM, 3 * d)

    ctx_k, ctx_v, ctx_s = [], [], []
    for l in range(depth):
        mod = modulation(cond, w_mod[l].astype(BF16), b_mod[l][None])
        mod = mod.reshape(MOD_ROWS, 6, d)[blk_cond]
        mod_blocks = jnp.concatenate([mod, jnp.zeros((mod.shape[0], MOD_ROWS - 6, d), F32)], axis=1)

        z_rw, z_cv, z_qkv, z_gate = in_projection(x, mod_blocks, _split(w_in[l].astype(BF16), in_sizes))

        z_prev = jnp.where(is_first, 0.0, jnp.roll(z_rw, 1, axis=0))
        z_next = jnp.where(is_last, 0.0, jnp.roll(z_rw, -1, axis=0))
        zs = z_rw + shift_mu_prev[l] * (z_prev - z_rw) + shift_mu_next[l] * (z_next - z_rw)
        r, k, v, zw_f, zw_b, za_f, za_b, zg = _split(zs, RW_SIZES)
        kk = (k * rw_k_k[l]).reshape(n_tok, RW_HEADS, RW_HEAD_DIM)
        kk = (kk * lax.rsqrt(jnp.sum(kk * kk, axis=-1, keepdims=True) + 1e-12)).reshape(n_tok, RW_WIDTH)
        lws, aas, kds = [], [], []
        for dd, (zw, za) in enumerate(((zw_f, za_f), (zw_b, za_b))):
            w_raw = rw_w0[l, dd] + jnp.tanh(zw) @ rw_w2[l, dd]
            lws.append(-math.exp(-0.5) * jax.nn.sigmoid(w_raw))
            a = jax.nn.sigmoid(rw_a0[l, dd] + za @ rw_a2[l, dd])
            aas.append(a)
            kds.append(k * (1 + (a - 1) * rw_k_a[l]))
        lw, aa, kd = jnp.stack(lws), jnp.stack(aas), jnp.stack(kds)
        s0_ctx = jnp.zeros((nb, 2, RW_HEADS // 2, LANES, LANES), F32)
        of_c, ob_c, sfin = rwkv_scan(r[:n_ctx], v[:n_ctx], kk[:n_ctx], lw[:, :n_ctx], aa[:, :n_ctx],
                                     kd[:, :n_ctx], s0_ctx, nb, seq)
        of_l, ob_l, _ = rwkv_scan(r[n_ctx:], v[n_ctx:], kk[n_ctx:], lw[:, n_ctx:], aa[:, n_ctx:],
                                  kd[:, n_ctx:], _bd_from_states(state_wkv[:, l]), db, dseq)
        ctx_s.append(_states_from_bd(sfin))
        o = jnp.concatenate([of_c + ob_c, of_l + ob_l], axis=0).reshape(n_tok, RW_HEADS, RW_HEAD_DIM)
        mu = jnp.mean(o, axis=-1, keepdims=True)
        var = jnp.mean(jnp.square(o - mu), axis=-1, keepdims=True)
        o = ((o - mu) * lax.rsqrt(var + RW_GN_EPS)).reshape(n_tok, RW_WIDTH) * rw_gn_w[l] + rw_gn_b[l]
        bonus = jnp.sum((r * k * rw_r_k[l]).reshape(n_tok, RW_HEADS, RW_HEAD_DIM), axis=-1, keepdims=True)
        bonus = (bonus * v.reshape(n_tok, RW_HEADS, RW_HEAD_DIM)).reshape(n_tok, RW_WIDTH)
        y_rw = (o + bonus) * (jax.nn.sigmoid(zg) @ rw_g2[l])

        ca, cb = jnp.split(z_cv, 2, axis=-1)
        u = ca * jax.nn.sigmoid(cb)

        def conv(us):
            return lax.conv_general_dilated(
                us, conv_w[l][:, None, :], window_strides=(1,),
                padding=[(CONV_KERNEL // 2, CONV_KERNEL // 2)],
                dimension_numbers=('NWC', 'WIO', 'NWC'), feature_group_count=CONV_WIDTH)

        u = jnp.concatenate([conv(u[:n_ctx].reshape(nb, seq, CONV_WIDTH)).reshape(n_ctx, CONV_WIDTH),
                             conv(u[n_ctx:].reshape(db, dseq, CONV_WIDTH)).reshape(n_lat, CONV_WIDTH)])
        u = u + conv_b[l]
        mu = jnp.mean(u, axis=-1, keepdims=True)
        var = jnp.mean(jnp.square(u - mu), axis=-1, keepdims=True)
        u = (u - mu) * lax.rsqrt(var + LN_EPS) * conv_ln_w[l] + conv_ln_b[l]
        y_cv = u * jax.nn.sigmoid(u)

        nq, nkv = ATT_HEADS * HEAD_DIM, ATT_KV_HEADS * HEAD_DIM
        q = _head_rms(z_qkv[:, :nq].reshape(n_tok, ATT_HEADS, HEAD_DIM), q_norm[l])
        kx = _head_rms(z_qkv[:, nq:nq + nkv].reshape(n_tok, ATT_KV_HEADS, HEAD_DIM), k_norm[l])
        vx = z_qkv[:, nq + nkv:].reshape(n_tok, ATT_KV_HEADS, HEAD_DIM)
        ctx_k.append(kx[:n_ctx].reshape(nb, seq, ATT_KV_HEADS, HEAD_DIM))
        ctx_v.append(vx[:n_ctx].reshape(nb, seq, ATT_KV_HEADS, HEAD_DIM))

        def heads_q(t, b_, t_):
            return t.reshape(b_, t_, ATT_KV_HEADS, ATT_GROUP, HEAD_DIM).transpose(0, 2, 3, 1, 4)

        def heads_kv(t):
            return t.transpose(0, 2, 1, 3)

        y_c = attention(heads_q(q[:n_ctx], nb, seq),
                        heads_kv(kx[:n_ctx].reshape(nb, seq, ATT_KV_HEADS, HEAD_DIM)),
                        heads_kv(vx[:n_ctx].reshape(nb, seq, ATT_KV_HEADS, HEAD_DIM)))
        q_l = _rope_2d(q[n_ctx:].reshape(db, dseq, ATT_HEADS, HEAD_DIM).reshape(db * dseq, ATT_HEADS, HEAD_DIM)
                       .reshape(db, dseq, ATT_HEADS, HEAD_DIM), rope)
        k_l = _rope_2d(kx[n_ctx:].reshape(db, dseq, ATT_KV_HEADS, HEAD_DIM), rope)
        keys = jnp.concatenate([k_l, cache_k[:, l]], axis=1)
        vals = jnp.concatenate([vx[n_ctx:].reshape(db, dseq, ATT_KV_HEADS, HEAD_DIM), cache_v[:, l]], axis=1)
        y_l = attention(heads_q(q_l.reshape(n_lat, ATT_HEADS, HEAD_DIM), db, dseq), heads_kv(keys), heads_kv(vals))
        unheads = lambda y, n: y.transpose(0, 3, 1, 2, 4).reshape(n, nq)
        y_at = jnp.concatenate([unheads(y_c, n_ctx), unheads(y_l, n_lat)], axis=0)

        x, h2, pq = merge_project(x, mod_blocks, y_rw, y_cv, y_at, z_gate,
                                  w_out_rwkv[l].astype(BF16), w_out_conv[l].astype(BF16),
                                  w_out_attn[l].astype(BF16), w_o[l].astype(BF16), peer_wq[l].astype(BF16))

        kz = jnp.zeros_like(peer_keys[l][:, 0])
        keys_padded = jnp.concatenate([jnp.concatenate([peer_keys[l][:, 0], kz], axis=-1),
                                       jnp.concatenate([kz, peer_keys[l][:, 1]], axis=-1)], axis=1)
        w_sel = peer_select(pq, keys_padded)
        x = peer_dense(h2, w_sel, peer_u[l].astype(BF16), peer_v[l].astype(BF16), x, mod_blocks)

    new_cache_k = jnp.stack(ctx_k, axis=1)
    new_cache_v = jnp.stack(ctx_v, axis=1)
    new_state = jnp.stack(ctx_s, axis=1)
    return (x[:n_ctx].reshape(nb, seq, d), x[n_ctx:].reshape(db, dseq, d),
            new_cache_k, new_cache_v, new_state)
```

```python
import functools
import math

import jax
import jax.numpy as jnp
from jax import lax
from jax.experimental import pallas as pl
from jax.experimental.pallas import tpu as pltpu

F32 = jnp.float32
BF16 = jnp.bfloat16

LANES = 128
SCAN_CHUNK = 64
SCAN_SUB = 16
RW_HEAD_DIM = 64
HEADS_PER_PAIR = LANES // RW_HEAD_DIM
VMEM_LIMIT = 56 * 1024 * 1024


def _dot(a, b):
    return jnp.dot(a.astype(BF16), b.astype(BF16), preferred_element_type=F32)


def _dot_nt(a, b):
    return lax.dot_general(a.astype(BF16), b.astype(BF16), (((1,), (1,)), ((), ())),
                           preferred_element_type=F32)


def _dot_tn(a, b):
    return lax.dot_general(a.astype(BF16), b.astype(BF16), (((0,), (0,)), ((), ())),
                           preferred_element_type=F32)


def _dot3(a, b):
    a_hi = a.astype(BF16)
    b_hi = b.astype(BF16)
    a_lo = (a - a_hi.astype(F32)).astype(BF16)
    b_lo = (b - b_hi.astype(F32)).astype(BF16)
    d = functools.partial(jnp.dot, preferred_element_type=F32)
    return d(a_hi, b_hi) + (d(a_hi, b_lo) + d(a_lo, b_hi))


def _scan_masks(reverse):
    C = SCAN_CHUNK
    P = HEADS_PER_PAIR * C
    row = lax.broadcasted_iota(jnp.int32, (P, LANES), 0)
    col = lax.broadcasted_iota(jnp.int32, (P, LANES), 1)
    tr = row % C
    tc = col % C
    t_i = lax.broadcasted_iota(jnp.int32, (C, C), 0)
    s_i = lax.broadcasted_iota(jnp.int32, (C, C), 1)
    return dict(
        same_head=(row // C) == (col // RW_HEAD_DIM),
        incl=(tr >= tc) if not reverse else (tr <= tc),
        strict=(tr > tc) if not reverse else (tr < tc),
        diag_blk=(tr // SCAN_SUB) == (tc // SCAN_SUB),
        eye=row == col,
        cum_mat=jnp.where((t_i >= s_i) if not reverse else (t_i <= s_i), 1.0, 0.0).astype(F32))


def _scan_chunk(chains, states):
    C = SCAN_CHUNK
    n = range(len(chains))
    mk = [ch[6] for ch in chains]
    cum = [_dot3(mk[i]["cum_mat"], chains[i][1]) for i in n]
    pre = []
    for i in n:
        r, lw, k, v, kk, a, m, reverse = chains[i]
        g = jnp.exp(cum[i])
        g_prev = jnp.exp(cum[i] - lw)
        g_inv = jnp.exp(-cum[i])
        g_end = jnp.exp(cum[i][C - 1:C, :] if not reverse else cum[i][0:1, :])
        bt = kk * a * g_inv
        kt = k * g_inv
        stack = lambda x, m=m: jnp.where(m["same_head"], jnp.concatenate([x, x], axis=0), 0.0)
        pre.append(dict(kap=stack(kk * g_prev), rt=stack(r * g), bt=stack(bt), kt=stack(kt),
                        v=stack(v), bh=stack(bt * g_end), kh=stack(kt * g_end), g_end=g_end))
    lb = [jnp.where(mk[i]["strict"], _dot_nt(pre[i]["kap"], pre[i]["bt"]), 0.0) for i in n]
    lk = [jnp.where(mk[i]["strict"], _dot_nt(pre[i]["kap"], pre[i]["kt"]), 0.0) for i in n]
    pb = [jnp.where(mk[i]["incl"], _dot_nt(pre[i]["rt"], pre[i]["bt"]), 0.0) for i in n]
    pk = [jnp.where(mk[i]["incl"], _dot_nt(pre[i]["rt"], pre[i]["kt"]), 0.0) for i in n]
    lkv = [_dot(lk[i], pre[i]["v"]) for i in n]

    d = [jnp.where(mk[i]["diag_blk"], lb[i], 0.0) for i in n]
    x = [jnp.concatenate([lb[i] - d[i], pre[i]["kap"], lkv[i]], axis=1) for i in n]
    x = [x[i] - _dot(d[i], x[i]) for i in n]
    p = [_dot(d[i], d[i]) for i in n]
    x = [x[i] + _dot(p[i], x[i]) for i in n]
    p = [_dot(p[i], p[i]) for i in n]
    x = [x[i] + _dot(p[i], x[i]) for i in n]
    p = [_dot(p[i], p[i]) for i in n]
    x = [x[i] + _dot(p[i], x[i]) for i in n]
    e = [x[i][:, :LANES] for i in n]
    rhs = [x[i][:, LANES:] for i in n]
    e2 = [_dot(e[i], e[i]) for i in n]
    t = [rhs[i] + _dot(e2[i], rhs[i]) for i in n]
    gy = [t[i] - _dot(e[i], t[i]) for i in n]

    m_mat = [jnp.where(mk[i]["eye"], pre[i]["g_end"], 0.0) - _dot_tn(pre[i]["bh"], gy[i][:, :LANES])
             for i in n]
    n_mat = [_dot_tn(pre[i]["kh"], pre[i]["v"]) - _dot_tn(pre[i]["bh"], gy[i][:, LANES:]) for i in n]
    pbgy = [_dot(pb[i], gy[i]) for i in n]
    q = [pre[i]["rt"] - pbgy[i][:, :LANES] for i in n]
    z = [_dot(pk[i], pre[i]["v"]) - pbgy[i][:, LANES:] for i in n]
    o_st = [_dot3(q[i], states[i]) + z[i] for i in n]
    new = [_dot3(m_mat[i], states[i]) + n_mat[i] for i in n]
    return [(o_st[i][:C] + o_st[i][C:], new[i]) for i in n]


def _scan_kernel(rf, vf, kkf, lwf, af, kf, rb, vb, kkb, lwb, ab, kb, s0, of, ob, sfin, state):
    c = pl.program_id(1)
    n_pairs = rf.shape[-1] // LANES

    @pl.when(c == 0)
    def _():
        state[...] = s0[0]

    chains, states, outs = [], [], []
    for d, (r_, v_, kk_, lw_, a_, k_, o_) in enumerate(((rf, vf, kkf, lwf, af, kf, of),
                                                        (rb, vb, kkb, lwb, ab, kb, ob))):
        masks = _scan_masks(reverse=(d == 1))
        for p in range(n_pairs):
            sl = slice(p * LANES, (p + 1) * LANES)
            chains.append((r_[:, sl], lw_[:, sl], k_[:, sl], v_[:, sl], kk_[:, sl], a_[:, sl],
                           masks, d == 1))
            states.append(state[d, p])
            outs.append((o_, sl, d, p))
    for (o, new), (o_, sl, d, p) in zip(_scan_chunk(chains, states), outs):
        o_[:, sl] = o
        state[d, p] = new

    @pl.when(c == pl.num_programs(1) - 1)
    def _():
        sfin[0] = state[...]


def rwkv_scan(r, v, kk, lw, a, kd, s0_bd, n_seq, seq_len):
    n_tok, width = r.shape
    n_pairs = width // LANES
    n_chunk = seq_len // SCAN_CHUNK
    fwd = lambda s, c: (s * n_chunk + c, 0)
    bwd = lambda s, c: (s * n_chunk + n_chunk - 1 - c, 0)
    blk = (SCAN_CHUNK, width)
    sspec = pl.BlockSpec((1, 2, n_pairs, LANES, LANES), lambda s, c: (s, 0, 0, 0, 0))
    in_specs = ([pl.BlockSpec(blk, fwd)] * 6 + [pl.BlockSpec(blk, bwd)] * 6 + [sspec])
    out_specs = [pl.BlockSpec(blk, fwd), pl.BlockSpec(blk, bwd), sspec]
    out_shape = [jax.ShapeDtypeStruct((n_tok, width), F32)] * 2 + [
        jax.ShapeDtypeStruct(s0_bd.shape, F32)]
    return pl.pallas_call(
        _scan_kernel,
        grid=(n_seq, n_chunk),
        in_specs=in_specs,
        out_specs=out_specs,
        out_shape=out_shape,
        scratch_shapes=[pltpu.VMEM((2, n_pairs, LANES, LANES), F32)],
        compiler_params=pltpu.CompilerParams(
            dimension_semantics=("parallel", "arbitrary"), vmem_limit_bytes=VMEM_LIMIT),
        name="rwkv_scan",
    )(r, v, kk, lw[0], a[0], kd[0], r, v, kk, lw[1], a[1], kd[1], s0_bd)


MOD_ROWS = 8
ROW_BLOCK = 256
RMS_EPS = 1e-6
LN_EPS = 1e-5
RW_GN_EPS = 64e-5


def _mod_kernel(c_ref, w_ref, b_ref, o_ref):
    c = c_ref[...]
    s = c * jax.nn.sigmoid(c)
    o_ref[...] = _dot(s, w_ref[...]) + b_ref[...]


def modulation(cond, w_mod, b_mod):
    n, d = cond.shape
    cols = w_mod.shape[1]
    blk = 1024
    return pl.pallas_call(
        _mod_kernel,
        grid=(cols // blk,),
        in_specs=[pl.BlockSpec((n, d), lambda j: (0, 0)),
                  pl.BlockSpec((d, blk), lambda j: (0, j)),
                  pl.BlockSpec((1, blk), lambda j: (0, j))],
        out_specs=pl.BlockSpec((n, blk), lambda j: (0, j)),
        out_shape=jax.ShapeDtypeStruct((n, cols), F32),
        name="modulation",
    )(cond, w_mod, b_mod)


def _modulate(x, shift, scale):
    ms = jnp.mean(x * x, axis=-1, keepdims=True)
    return x * lax.rsqrt(ms + RMS_EPS) * (1.0 + scale) + shift


def _inproj_kernel(x_ref, mod_ref, *refs):
    n = len(refs) // 2
    h = _modulate(x_ref[...], mod_ref[0, 0:1, :], mod_ref[0, 1:2, :]).astype(BF16)
    for w_ref, o_ref in zip(refs[:n], refs[n:]):
        o_ref[...] = jnp.dot(h, w_ref[...], preferred_element_type=F32)


def in_projection(x, mod_blocks, weights):
    n_tok, d = x.shape
    row = lambda i: (i, 0)
    const = lambda i: (0, 0)
    return pl.pallas_call(
        _inproj_kernel,
        grid=(n_tok // ROW_BLOCK,),
        in_specs=[pl.BlockSpec((ROW_BLOCK, d), row),
                  pl.BlockSpec((1, MOD_ROWS, d), lambda i: (i, 0, 0))]
                 + [pl.BlockSpec(w.shape, const, pipeline_mode=pl.Buffered(1)) for w in weights],
        out_specs=[pl.BlockSpec((ROW_BLOCK, w.shape[1]), row) for w in weights],
        out_shape=[jax.ShapeDtypeStruct((n_tok, w.shape[1]), F32) for w in weights],
        compiler_params=pltpu.CompilerParams(
            dimension_semantics=("parallel",), vmem_limit_bytes=VMEM_LIMIT),
        name="in_projection",
    )(x, mod_blocks, *weights)


HEAD_DIM = 64


def _group_ones(width):
    r = lax.broadcasted_iota(jnp.int32, (width, width), 0) // HEAD_DIM
    c = lax.broadcasted_iota(jnp.int32, (width, width), 1) // HEAD_DIM
    return jnp.where(r == c, 1.0, 0.0).astype(BF16)


def _group_sum(x, ones):
    hi = x.astype(BF16)
    lo = (x - hi.astype(F32)).astype(BF16)
    d = functools.partial(jnp.dot, preferred_element_type=F32)
    return d(hi, ones) + d(lo, ones)


RW_WIDTH = 512
RW_COLS = 3 * RW_WIDTH + 3 * LANES
RW_VEC_ROWS = 8


def _rwkv_prep_kernel(z_ref, hp_ref, hn_ref, mu_ref, vec_ref, w2_ref, a2_ref, g2_ref,
                      r_ref, v_ref, kk_ref, lw_ref, a_ref, kd_ref, bonus_ref, gate_ref):
    z = z_ref[...]
    rb = z.shape[0]
    row = lax.broadcasted_iota(jnp.int32, (rb, 1), 0)
    z_prev = jnp.where(row == 0, hp_ref[0], pltpu.roll(z, 1, 0))
    z_next = jnp.where(row == rb - 1, hn_ref[0], pltpu.roll(z, rb - 1, 0))
    zs = z + mu_ref[0:1, :] * (z_prev - z) + mu_ref[1:2, :] * (z_next - z)
    w = RW_WIDTH
    r, k, v = zs[:, :w], zs[:, w:2 * w], zs[:, 2 * w:3 * w]
    zw = zs[:, 3 * w:3 * w + LANES]
    za = zs[:, 3 * w + LANES:3 * w + 2 * LANES]
    zg = zs[:, 3 * w + 2 * LANES:]
    ones = _group_ones(w)
    kk = k * vec_ref[0:1, :]
    kk = kk * lax.rsqrt(_group_sum(kk * kk, ones) + 1e-12)
    r_ref[...] = r
    v_ref[...] = v
    kk_ref[...] = kk
    tz = jnp.tanh(zw)
    for d in range(2):
        w_raw = vec_ref[3 + d:4 + d, :] + _dot(tz, w2_ref[d])
        lw_ref[d] = -math.exp(-0.5) * jax.nn.sigmoid(w_raw)
        a = jax.nn.sigmoid(vec_ref[5 + d:6 + d, :] + _dot(za, a2_ref[d]))
        a_ref[d] = a
        kd_ref[d] = k * (1.0 + (a - 1.0) * vec_ref[1:2, :])
    bonus_ref[...] = _group_sum(r * k * vec_ref[2:3, :], ones) * v
    gate_ref[...] = _dot(jax.nn.sigmoid(zg), g2_ref[...])


def rwkv_prep(z_rw, halo_prev, halo_next, mu, vecs, w2_pad, a2_pad, g2):
    n_tok = z_rw.shape[0]
    w = RW_WIDTH
    row = lambda i: (i, 0)
    const2 = lambda i: (0, 0)
    const3 = lambda i: (0, 0, 0)
    halo = pl.BlockSpec((1, 1, RW_COLS), lambda i: (i, 0, 0))
    one = pl.BlockSpec((ROW_BLOCK, w), row)
    two = pl.BlockSpec((2, ROW_BLOCK, w), lambda i: (0, i, 0))
    s1 = jax.ShapeDtypeStruct((n_tok, w), F32)
    s2 = jax.ShapeDtypeStruct((2, n_tok, w), F32)
    return pl.pallas_call(
        _rwkv_prep_kernel,
        grid=(n_tok // ROW_BLOCK,),
        in_specs=[pl.BlockSpec((ROW_BLOCK, RW_COLS), row), halo, halo,
                  pl.BlockSpec(mu.shape, const2), pl.BlockSpec(vecs.shape, const2),
                  pl.BlockSpec(w2_pad.shape, const3), pl.BlockSpec(a2_pad.shape, const3),
                  pl.BlockSpec(g2.shape, const2)],
        out_specs=[one, one, one, two, two, two, one, one],
        out_shape=[s1, s1, s1, s2, s2, s2, s1, s1],
        compiler_params=pltpu.CompilerParams(
            dimension_semantics=("parallel",), vmem_limit_bytes=VMEM_LIMIT),
        name="rwkv_prep",
    )(z_rw, halo_prev, halo_next, mu, vecs, w2_pad, a2_pad, g2)


CONV_WIDTH = 512
CONV_KERNEL = 31
CONV_PAD = 16
CONV_ROWS = 64


def _conv_kernel(z_ref, w_ref, vec_ref, o_ref, pad_ref):
    t = z_ref.shape[0]
    cw = CONV_WIDTH
    half = CONV_KERNEL // 2
    zeros = jnp.zeros((CONV_PAD, cw), F32)
    pad_ref[0:CONV_PAD, :] = zeros
    pad_ref[CONV_PAD + t:CONV_PAD + t + CONV_PAD, :] = zeros
    pad_ref[CONV_PAD:CONV_PAD + t, :] = z_ref[:, :cw] * jax.nn.sigmoid(z_ref[:, cw:])

    def chunk(i, carry):
        base = pl.multiple_of(i * CONV_ROWS, CONV_ROWS)
        window = pad_ref[pl.ds(base, CONV_ROWS + 2 * CONV_PAD), :]
        acc = jnp.zeros((CONV_ROWS, cw), F32)
        for j in range(CONV_KERNEL):
            lo = CONV_PAD - half + j
            acc = acc + w_ref[j:j + 1, :] * window[lo:lo + CONV_ROWS]
        u = acc + vec_ref[0:1, :]
        mu = jnp.mean(u, axis=-1, keepdims=True)
        cen = u - mu
        var = jnp.mean(cen * cen, axis=-1, keepdims=True)
        y = cen * lax.rsqrt(var + LN_EPS) * vec_ref[1:2, :] + vec_ref[2:3, :]
        o_ref[pl.ds(base, CONV_ROWS), :] = y * jax.nn.sigmoid(y)
        return carry

    lax.fori_loop(0, t // CONV_ROWS, chunk, 0)


def conv_module(z_cv, conv_w, vecs, row0, n_seq, seq_len):
    cw = CONV_WIDTH
    blk0 = row0 // seq_len
    return pl.pallas_call(
        _conv_kernel,
        grid=(n_seq,),
        in_specs=[pl.BlockSpec((seq_len, 2 * cw), lambda i: (blk0 + i, 0)),
                  pl.BlockSpec(conv_w.shape, lambda i: (0, 0)),
                  pl.BlockSpec(vecs.shape, lambda i: (0, 0))],
        out_specs=pl.BlockSpec((seq_len, cw), lambda i: (i, 0)),
        out_shape=jax.ShapeDtypeStruct((n_seq * seq_len, cw), F32),
        scratch_shapes=[pltpu.VMEM((seq_len + 2 * CONV_PAD, cw), F32)],
        compiler_params=pltpu.CompilerParams(
            dimension_semantics=("parallel",), vmem_limit_bytes=VMEM_LIMIT),
        name="conv_module",
    )(z_cv, conv_w, vecs)


ATT_Q_BLOCK = 256
ATT_HEADS = 8
ATT_KV_HEADS = 2
ATT_GROUP = ATT_HEADS // ATT_KV_HEADS


def _rot_partner():
    r = lax.broadcasted_iota(jnp.int32, (LANES, LANES), 0)
    c = lax.broadcasted_iota(jnp.int32, (LANES, LANES), 1)
    lo = (r % 32) < 16
    return jnp.where((c == r + 16) & lo, 1.0, jnp.where((c == r - 16) & ~lo, -1.0, 0.0)).astype(BF16)


def _attn_kernel(*refs, rotary):
    if rotary:
        (zq_ref, zkv_ref, nq_ref, nk_ref, cosq_ref, sinq_ref, cosk_ref, sink_ref, ck_ref, cv_ref,
         o_ref) = refs
    else:
        zq_ref, zkv_ref, nq_ref, nk_ref, o_ref, kn_ref = refs
    tq = zq_ref.shape[0]
    ones = _group_ones(LANES)
    lane = lax.broadcasted_iota(jnp.int32, (1, LANES), 1)
    first = lane < HEAD_DIM
    row_head = lax.broadcasted_iota(jnp.int32, (2 * tq, LANES), 0) // tq
    own = row_head == (lax.broadcasted_iota(jnp.int32, (2 * tq, LANES), 1) // HEAD_DIM)

    def norm(x, g):
        return x * lax.rsqrt(_group_sum(x * x, ones) * (1.0 / HEAD_DIM) + RMS_EPS) * g

    def rope(x, cos, sin):
        return x * cos + jnp.dot(x.astype(BF16), _rot_partner(), preferred_element_type=F32) * sin

    def dup(x):
        sw = pltpu.roll(x, HEAD_DIM, 1)
        return jnp.where(first, x, sw), jnp.where(first, sw, x)

    k = norm(zkv_ref[:, :LANES], nk_ref[...])
    v = zkv_ref[:, LANES:]
    if rotary:
        k = rope(k, cosk_ref[...], sink_ref[...])
        ck, cv = dup(ck_ref[...]), dup(cv_ref[...])
    else:
        kn_ref[...] = k
    kd, vd = dup(k), dup(v)
    for pair in range(ATT_HEADS // 2):
        g = pair // (ATT_GROUP // 2)
        q = norm(zq_ref[:, pair * LANES:(pair + 1) * LANES], nq_ref[...])
        if rotary:
            q = rope(q, cosq_ref[...], sinq_ref[...])
        q = q * (HEAD_DIM ** -0.5)
        qs = jnp.where(own, jnp.concatenate([q, q], axis=0), 0.0)
        s = _dot_nt(qs, kd[g])
        m = jnp.max(s, axis=-1, keepdims=True)
        if rotary:
            s2 = _dot_nt(qs, ck[g])
            m = jnp.maximum(m, jnp.max(s2, axis=-1, keepdims=True))
            p2 = jnp.exp(s2 - m)
        p = jnp.exp(s - m)
        den = jnp.sum(p, axis=-1, keepdims=True)
        o = _dot(p, vd[g])
        if rotary:
            den = den + jnp.sum(p2, axis=-1, keepdims=True)
            o = o + _dot(p2, cv[g])
        o = o / den
        o_ref[:, pair * LANES:(pair + 1) * LANES] = jnp.where(first, o[:tq], o[tq:])


def attention(z_q, z_kv, nq, nk, row0, n_seq, seq_len, rope=None, cache=None):
    qb = min(ATT_Q_BLOCK, seq_len)
    nqb = seq_len // qb
    qblk0 = row0 // qb
    sblk0 = row0 // seq_len
    wq = z_q.shape[1]
    in_specs = [pl.BlockSpec((qb, wq), lambda b, t: (qblk0 + b * nqb + t, 0)),
                pl.BlockSpec((seq_len, 2 * LANES), lambda b, t: (sblk0 + b, 0)),
                pl.BlockSpec((1, LANES), lambda b, t: (0, 0)),
                pl.BlockSpec((1, LANES), lambda b, t: (0, 0))]
    args = [z_q, z_kv, nq, nk]
    out_q = pl.BlockSpec((qb, wq), lambda b, t: (b * nqb + t, 0))
    y_shape = jax.ShapeDtypeStruct((n_seq * seq_len, wq), F32)
    if rope is None:
        out_specs = [out_q, pl.BlockSpec((seq_len, LANES), lambda b, t: (b, 0))]
        out_shape = [y_shape, jax.ShapeDtypeStruct((n_seq * seq_len, LANES), F32)]
    else:
        past = cache[0].shape[0] // n_seq
        in_specs += [pl.BlockSpec((qb, LANES), lambda b, t: (t, 0))] * 2
        in_specs += [pl.BlockSpec((seq_len, LANES), lambda b, t: (0, 0))] * 2
        in_specs += [pl.BlockSpec((past, LANES), lambda b, t: (b, 0))] * 2
        args += [rope[0], rope[1], rope[0], rope[1], cache[0], cache[1]]
        out_specs = out_q
        out_shape = y_shape
    return pl.pallas_call(
        functools.partial(_attn_kernel, rotary=rope is not None),
        grid=(n_seq, nqb),
        in_specs=in_specs,
        out_specs=out_specs,
        out_shape=out_shape,
        compiler_params=pltpu.CompilerParams(
            dimension_semantics=("parallel", "arbitrary"), vmem_limit_bytes=VMEM_LIMIT),
        name="attention",
    )(*args)


def _merge_kernel(x_ref, mod_ref, of_ref, ob_ref, bonus_ref, gate_ref, ycv_ref, yat_ref, zg_ref,
                  gn_ref, wr_ref, wc_ref, wa_ref, wo_ref, wq_ref, xo_ref, h2_ref, q_ref):
    d = x_ref.shape[1]
    zg = zg_ref[...]
    ones = _group_ones(of_ref.shape[1])
    o = of_ref[...] + ob_ref[...]
    cen = o - _group_sum(o, ones) * (1.0 / HEAD_DIM)
    var = _group_sum(cen * cen, ones) * (1.0 / HEAD_DIM)
    o = cen * lax.rsqrt(var + RW_GN_EPS) * gn_ref[0:1, :] + gn_ref[1:2, :]
    y_rw = (o + bonus_ref[...]) * gate_ref[...]
    merged = (jax.nn.sigmoid(zg[:, :d]) * _dot(y_rw, wr_ref[...])
              + jax.nn.sigmoid(zg[:, d:2 * d]) * _dot(ycv_ref[...], wc_ref[...])
              + jax.nn.sigmoid(zg[:, 2 * d:]) * _dot(yat_ref[...], wa_ref[...]))
    x = x_ref[...] + mod_ref[0, 2:3, :] * _dot(merged, wo_ref[...])
    xo_ref[...] = x
    h2 = _modulate(x, mod_ref[0, 3:4, :], mod_ref[0, 4:5, :]).astype(BF16)
    h2_ref[...] = h2
    q_ref[...] = jnp.dot(h2, wq_ref[...], preferred_element_type=F32)


def merge_project(x, mod_blocks, o_f, o_b, bonus, gate, y_cv, y_at, z_gate, gn, w_r, w_c, w_a, w_o, w_q):
    n_tok, d = x.shape
    row = lambda i: (i, 0)
    const = lambda i: (0, 0)
    acts = [o_f, o_b, bonus, gate, y_cv, y_at, z_gate]
    weights = [gn, w_r, w_c, w_a, w_o, w_q]
    return pl.pallas_call(
        _merge_kernel,
        grid=(n_tok // ROW_BLOCK,),
        in_specs=[pl.BlockSpec((ROW_BLOCK, d), row),
                  pl.BlockSpec((1, MOD_ROWS, d), lambda i: (i, 0, 0))]
                 + [pl.BlockSpec((ROW_BLOCK, a.shape[1]), row) for a in acts]
                 + [pl.BlockSpec(w.shape, const) for w in weights],
        out_specs=[pl.BlockSpec((ROW_BLOCK, d), row), pl.BlockSpec((ROW_BLOCK, d), row),
                   pl.BlockSpec((ROW_BLOCK, w_q.shape[1]), row)],
        out_shape=[jax.ShapeDtypeStruct((n_tok, d), F32), jax.ShapeDtypeStruct((n_tok, d), BF16),
                   jax.ShapeDtypeStruct((n_tok, w_q.shape[1]), F32)],
        compiler_params=pltpu.CompilerParams(
            dimension_semantics=("parallel",), vmem_limit_bytes=VMEM_LIMIT),
        name="merge_project",
    )(x, mod_blocks, *acts, *weights)


PEER_HEADS = 8
PEER_N_KEYS = 128
PEER_HALF = 64
PEER_TOPK = 16
PEER_SEL_BLOCK = 128
PEER_LOOP_UNROLL = 8
NEG = -1e30


def _dot3_nt(a, b):
    a_hi = a.astype(BF16)
    b_hi = b.astype(BF16)
    a_lo = (a - a_hi.astype(F32)).astype(BF16)
    b_lo = (b - b_hi.astype(F32)).astype(BF16)
    d = lambda x, y: lax.dot_general(x, y, (((1,), (1,)), ((), ())), preferred_element_type=F32)
    return d(a_hi, b_hi) + (d(a_hi, b_lo) + d(a_lo, b_hi))


def _extract_max(s, code):
    m = jnp.max(s, axis=0, keepdims=True)
    pos = jnp.min(jnp.where(s == m, code, 1e9), axis=0, keepdims=True)
    return m, pos, jnp.where(code == pos, NEG, s)


def _pair_candidates(a0, a1):
    k = PEER_TOPK
    sub = lax.broadcasted_iota(jnp.int32, (k, a0.shape[1]), 0).astype(F32)
    sub8 = sub[:8]
    vals = [a0[0:1] + a1]
    codes = [sub]
    for r0 in range(1, 8):
        limit = k // (r0 + 1)
        vals.append(jnp.where(sub8 < limit, a0[r0:r0 + 1] + a1[:8], NEG))
        codes.append(sub8 + float(r0 * k))
    vals.append(a0[8:] + a1[0:1])
    codes.append((sub8 + 8.0) * float(k))
    return jnp.concatenate(vals, axis=0), jnp.concatenate(codes, axis=0)


def _peer_select_kernel(q_ref, keys_ref, w_ref, top_v, top_i, slot_i, slot_j, slot_g,
                        rows_i, rows_j, rows_g, w3_ref):
    k = PEER_TOPK
    ct = q_ref.shape[0]
    key_code = lax.broadcasted_iota(jnp.int32, (PEER_N_KEYS, ct), 0).astype(F32)
    rank = lax.broadcasted_iota(jnp.int32, (k, ct), 0).astype(F32)
    for h in range(PEER_HEADS):
        st = _dot3_nt(keys_ref[h], q_ref[:, h * LANES:(h + 1) * LANES])
        for p in range(2):
            s = st[p * PEER_N_KEYS:(p + 1) * PEER_N_KEYS]
            for it in range(k):
                m, pos, s = _extract_max(s, key_code)
                top_v[p, it:it + 1, :] = m
                top_i[p, it:it + 1, :] = pos
        a0, a1 = top_v[0], top_v[1]
        i0, i1 = top_i[0], top_i[1]
        cand, code = _pair_candidates(a0, a1)
        best = a0[0:1] + a1[0:1]
        z = jnp.zeros_like(best)
        for it in range(k):
            m, pos, cand = _extract_max(cand, code)
            r0 = jnp.floor(pos * (1.0 / k))
            r1 = pos - r0 * k
            e = jnp.exp(m - best)
            z = z + e
            slot = h * k + it
            slot_i[slot:slot + 1, :] = jnp.sum(jnp.where(rank == r0, i0, 0.0), axis=0, keepdims=True)
            slot_j[slot:slot + 1, :] = jnp.sum(jnp.where(rank == r1, i1, 0.0), axis=0, keepdims=True)
            slot_g[slot:slot + 1, :] = e
        slot_g[h * k:(h + 1) * k, :] = slot_g[h * k:(h + 1) * k, :] / z

    rows_i[...] = slot_i[...].T
    rows_j[...] = slot_j[...].T
    rows_g[...] = slot_g[...].T
    n = PEER_N_KEYS
    sub = lax.broadcasted_iota(jnp.int32, (n, n), 0).astype(F32)

    def per_token(c, carry):
        a_t = jnp.where(sub == rows_i[pl.ds(c, 1), :], rows_g[pl.ds(c, 1), :], 0.0)
        b_t = jnp.where(sub == rows_j[pl.ds(c, 1), :], 1.0, 0.0)
        w3_ref[pl.ds(pl.multiple_of(c * n, n), n), :] = _dot_nt(a_t, b_t)
        return carry

    lax.fori_loop(0, ct, per_token, 0, unroll=PEER_LOOP_UNROLL)

    def per_key(i, carry):
        w_ref[:, pl.ds(pl.multiple_of(i * n, n), n)] = w3_ref[pl.ds(i, ct, stride=n), :].astype(BF16)
        return carry

    lax.fori_loop(0, n, per_key, 0, unroll=PEER_LOOP_UNROLL)


def peer_select(q, keys_padded):
    n_tok, qd = q.shape
    ct = PEER_SEL_BLOCK
    n = PEER_N_KEYS
    slots = PEER_HEADS * PEER_TOPK
    return pl.pallas_call(
        _peer_select_kernel,
        grid=(n_tok // ct,),
        in_specs=[pl.BlockSpec((ct, qd), lambda i: (i, 0)),
                  pl.BlockSpec(keys_padded.shape, lambda i: (0, 0, 0))],
        out_specs=pl.BlockSpec((ct, n * n), lambda i: (i, 0)),
        out_shape=jax.ShapeDtypeStruct((n_tok, n * n), BF16),
        scratch_shapes=[pltpu.VMEM((2, PEER_TOPK, ct), F32), pltpu.VMEM((2, PEER_TOPK, ct), F32),
                        pltpu.VMEM((slots, ct), F32), pltpu.VMEM((slots, ct), F32),
                        pltpu.VMEM((slots, ct), F32),
                        pltpu.VMEM((ct, slots), F32), pltpu.VMEM((ct, slots), F32),
                        pltpu.VMEM((ct, slots), F32),
                        pltpu.VMEM((ct * n, n), F32)],
        compiler_params=pltpu.CompilerParams(
            dimension_semantics=("parallel",), vmem_limit_bytes=VMEM_LIMIT),
        name="peer_select",
    )(q, keys_padded)


PEER_TOKEN_BLOCK = 1024
PEER_EXPERT_BLOCK = 1024


def _peer_kernel(h_ref, w_ref, u_ref, v_ref, x_ref, mod_ref, o_ref, acc_ref):
    e = pl.program_id(1)

    @pl.when(e == 0)
    def _():
        acc_ref[...] = jnp.zeros_like(acc_ref)

    s = _dot_nt(h_ref[...], u_ref[...])
    act = 0.5 * s * (1.0 + lax.erf(s * (2.0 ** -0.5)))
    acc_ref[...] += _dot(act * w_ref[...].astype(F32), v_ref[...])

    @pl.when(e == pl.num_programs(1) - 1)
    def _():
        for j in range(mod_ref.shape[0]):
            rows = slice(j * ROW_BLOCK, (j + 1) * ROW_BLOCK)
            o_ref[rows, :] = x_ref[rows, :] + mod_ref[j, 5:6, :] * acc_ref[rows, :]


def peer_dense(h2, w_sel, u_tab, v_tab, x, mod_blocks):
    n_tok, d = x.shape
    n_exp = u_tab.shape[0]
    tb, eb = PEER_TOKEN_BLOCK, PEER_EXPERT_BLOCK
    tok = lambda i, e: (i, 0)
    return pl.pallas_call(
        _peer_kernel,
        grid=(n_tok // tb, n_exp // eb),
        in_specs=[pl.BlockSpec((tb, d), tok),
                  pl.BlockSpec((tb, eb), lambda i, e: (i, e)),
                  pl.BlockSpec((eb, d), lambda i, e: (e, 0)),
                  pl.BlockSpec((eb, d), lambda i, e: (e, 0)),
                  pl.BlockSpec((tb, d), tok),
                  pl.BlockSpec((tb // ROW_BLOCK, MOD_ROWS, d), lambda i, e: (i, 0, 0))],
        out_specs=pl.BlockSpec((tb, d), tok),
        out_shape=jax.ShapeDtypeStruct((n_tok, d), F32),
        scratch_shapes=[pltpu.VMEM((tb, d), F32)],
        compiler_params=pltpu.CompilerParams(
            dimension_semantics=("parallel", "arbitrary"), vmem_limit_bytes=VMEM_LIMIT),
        name="peer_dense",
    )(h2, w_sel, u_tab, v_tab, x, mod_blocks)


RW_HEADS = RW_WIDTH // RW_HEAD_DIM
ROT_HALF = HEAD_DIM // 2
ROPE_THETA = 10000.0
GRID_W = 64


def _split(z, sizes):
    parts, start = [], 0
    for s in sizes:
        parts.append(z[..., start:start + s])
        start += s
    return parts


def _bd_from_states(s):
    a = jnp.swapaxes(s, -1, -2)
    n, d, h, m, _ = a.shape
    a = a.reshape(n, d, h // 2, 2, m, m)
    z = jnp.zeros_like(a[:, :, :, 0])
    top = jnp.concatenate([a[:, :, :, 0], z], axis=-1)
    bot = jnp.concatenate([z, a[:, :, :, 1]], axis=-1)
    return jnp.concatenate([top, bot], axis=-2)


def _states_from_bd(bd):
    n, d, p, _, _ = bd.shape
    m = RW_HEAD_DIM
    a = jnp.stack([bd[:, :, :, :m, :m], bd[:, :, :, m:, m:]], axis=3).reshape(n, d, 2 * p, m, m)
    return jnp.swapaxes(a, -1, -2)


def _rope_tables(t):
    rows = t // GRID_W
    row, col = jnp.meshgrid(jnp.arange(rows), jnp.arange(GRID_W), indexing='ij')
    inv = ROPE_THETA ** (-jnp.arange(0, ROT_HALF, 2, dtype=F32) / ROT_HALF)
    ang_r = row.reshape(-1, 1).astype(F32) * inv
    ang_c = col.reshape(-1, 1).astype(F32) * inv
    cr, sr, cc, sc = jnp.cos(ang_r), jnp.sin(ang_r), jnp.cos(ang_c), jnp.sin(ang_c)
    cos = jnp.concatenate([cr, cr, cc, cc], axis=-1)
    sin = jnp.concatenate([sr, sr, sc, sc], axis=-1)
    return jnp.tile(cos, (1, LANES // HEAD_DIM)), jnp.tile(sin, (1, LANES // HEAD_DIM))


def _pad_rows(w, top):
    z = jnp.zeros_like(w)
    return jnp.concatenate([w, z] if top else [z, w], axis=0)


def kernel(x_prompt, x_sample, c, cache_k, cache_v, state_wkv, c_ctx, w_mod, b_mod, w_in, shift_mu_prev, shift_mu_next, rw_w0, rw_w2, rw_a0, rw_a2, rw_g2, rw_k_k, rw_k_a, rw_r_k, rw_gn_w, rw_gn_b, w_out_rwkv, conv_w, conv_b, conv_ln_w, conv_ln_b, w_out_conv, q_norm, k_norm, w_out_attn, w_o, peer_wq, peer_keys, peer_u, peer_v):
    nb, seq, d = x_prompt.shape
    db, dseq, _ = x_sample.shape
    depth = w_in.shape[0]
    n_ctx = nb * seq
    n_lat = db * dseq
    n_tok = n_ctx + n_lat
    x = jnp.concatenate([x_prompt.reshape(n_ctx, d), x_sample.reshape(n_lat, d)], axis=0)

    cond = jnp.concatenate([c_ctx[None], c, jnp.zeros((MOD_ROWS - 1 - db, d), F32)], axis=0)
    blk_cond = jnp.concatenate([jnp.zeros((n_ctx // ROW_BLOCK,), jnp.int32),
                                1 + jnp.arange(n_lat // ROW_BLOCK, dtype=jnp.int32) // (dseq // ROW_BLOCK)])
    n_blk = n_tok // ROW_BLOCK
    blk_row = jnp.arange(n_blk) * ROW_BLOCK
    blk_pos = jnp.where(blk_row < n_ctx, blk_row % seq, (blk_row - n_ctx) % dseq)
    blk_len = jnp.where(blk_row < n_ctx, seq, dseq)
    starts_seq = (blk_pos == 0)[:, None]
    ends_seq = (blk_pos + ROW_BLOCK == blk_len)[:, None]
    rope = _rope_tables(dseq)
    in_sizes = (RW_COLS, 2 * CONV_WIDTH, ATT_HEADS * HEAD_DIM, 2 * ATT_KV_HEADS * HEAD_DIM, 3 * d)
    past = cache_k.shape[2]

    ctx_k, ctx_v, ctx_s = [], [], []
    for l in range(depth):
        mod = modulation(cond, w_mod[l].astype(BF16), b_mod[l][None])
        mod = mod.reshape(MOD_ROWS, 6, d)[blk_cond]
        mod_blocks = jnp.concatenate([mod, jnp.zeros((mod.shape[0], MOD_ROWS - 6, d), F32)], axis=1)

        z_rw, z_cv, z_q, z_kv, z_gate = in_projection(x, mod_blocks, _split(w_in[l].astype(BF16), in_sizes))

        zb = z_rw.reshape(n_blk, ROW_BLOCK, RW_COLS)
        halo_prev = jnp.where(starts_seq, 0.0, jnp.roll(zb[:, -1], 1, axis=0))[:, None]
        halo_next = jnp.where(ends_seq, 0.0, jnp.roll(zb[:, 0], -1, axis=0))[:, None]
        mu = jnp.stack([shift_mu_prev[l], shift_mu_next[l]])
        vecs = jnp.stack([rw_k_k[l], rw_k_a[l], rw_r_k[l], rw_w0[l, 0], rw_w0[l, 1], rw_a0[l, 0],
                          rw_a0[l, 1], jnp.zeros((RW_WIDTH,), F32)])
        w2_pad = jnp.stack([_pad_rows(rw_w2[l, 0], True), _pad_rows(rw_w2[l, 1], False)]).astype(BF16)
        a2_pad = jnp.stack([_pad_rows(rw_a2[l, 0], True), _pad_rows(rw_a2[l, 1], False)]).astype(BF16)
        r, v, kk, lw, aa, kd, bonus, gate = rwkv_prep(z_rw, halo_prev, halo_next, mu, vecs, w2_pad, a2_pad,
                                                      rw_g2[l].astype(BF16))
        s0_ctx = jnp.zeros((nb, 2, RW_HEADS // 2, LANES, LANES), F32)
        of_c, ob_c, sfin = rwkv_scan(r[:n_ctx], v[:n_ctx], kk[:n_ctx], lw[:, :n_ctx], aa[:, :n_ctx],
                                     kd[:, :n_ctx], s0_ctx, nb, seq)
        of_l, ob_l, _ = rwkv_scan(r[n_ctx:], v[n_ctx:], kk[n_ctx:], lw[:, n_ctx:], aa[:, n_ctx:],
                                  kd[:, n_ctx:], _bd_from_states(state_wkv[:, l]), db, dseq)
        ctx_s.append(_states_from_bd(sfin))
        o_f = jnp.concatenate([of_c, of_l], axis=0)
        o_b = jnp.concatenate([ob_c, ob_l], axis=0)

        cvec = jnp.stack([conv_b[l], conv_ln_w[l], conv_ln_b[l]] + [jnp.zeros((CONV_WIDTH,), F32)] * 5)
        y_cv = jnp.concatenate([conv_module(z_cv, conv_w[l], cvec, 0, nb, seq),
                                conv_module(z_cv, conv_w[l], cvec, n_ctx, db, dseq)], axis=0)

        nq = jnp.tile(q_norm[l], LANES // HEAD_DIM)[None]
        nk = jnp.tile(k_norm[l], LANES // HEAD_DIM)[None]
        y_c, k_ctx = attention(z_q, z_kv, nq, nk, 0, nb, seq)
        y_l = attention(z_q, z_kv, nq, nk, n_ctx, db, dseq, rope=rope,
                        cache=(cache_k[:, l].reshape(db * past, LANES), cache_v[:, l].reshape(db * past, LANES)))
        y_at = jnp.concatenate([y_c, y_l], axis=0)
        ctx_k.append(k_ctx.reshape(nb, seq, ATT_KV_HEADS, HEAD_DIM))
        ctx_v.append(z_kv[:n_ctx, LANES:].reshape(nb, seq, ATT_KV_HEADS, HEAD_DIM))

        x, h2, pq = merge_project(x, mod_blocks, o_f, o_b, bonus, gate, y_cv, y_at, z_gate,
                                  jnp.stack([rw_gn_w[l], rw_gn_b[l]]),
                                  w_out_rwkv[l].astype(BF16), w_out_conv[l].astype(BF16),
                                  w_out_attn[l].astype(BF16), w_o[l].astype(BF16), peer_wq[l].astype(BF16))

        kz = jnp.zeros_like(peer_keys[l][:, 0])
        keys_padded = jnp.concatenate([jnp.concatenate([peer_keys[l][:, 0], kz], axis=-1),
                                       jnp.concatenate([kz, peer_keys[l][:, 1]], axis=-1)], axis=1)
        w_sel = peer_select(pq, keys_padded)
        x = peer_dense(h2, w_sel, peer_u[l].astype(BF16), peer_v[l].astype(BF16), x, mod_blocks)

    new_cache_k = jnp.stack(ctx_k, axis=1)
    new_cache_v = jnp.stack(ctx_v, axis=1)
    new_state = jnp.stack(ctx_s, axis=1)
    return (x[:n_ctx].reshape(nb, seq, d), x[n_ctx:].reshape(db, dseq, d),
            new_cache_k, new_cache_v, new_state)
```

```python
import functools
import math

import jax
import jax.numpy as jnp
from jax import lax
from jax.experimental import pallas as pl
from jax.experimental.pallas import tpu as pltpu

F32 = jnp.float32
BF16 = jnp.bfloat16

LANES = 128
SCAN_CHUNK = 64
SCAN_SUB = 16
RW_HEAD_DIM = 64
HEADS_PER_PAIR = LANES // RW_HEAD_DIM
VMEM_LIMIT = 56 * 1024 * 1024


def _dot(a, b):
    return jnp.dot(a.astype(BF16), b.astype(BF16), preferred_element_type=F32)


def _dot_nt(a, b):
    return lax.dot_general(a.astype(BF16), b.astype(BF16), (((1,), (1,)), ((), ())),
                           preferred_element_type=F32)


def _dot_tn(a, b):
    return lax.dot_general(a.astype(BF16), b.astype(BF16), (((0,), (0,)), ((), ())),
                           preferred_element_type=F32)


def _dot3(a, b):
    a_hi = a.astype(BF16)
    b_hi = b.astype(BF16)
    a_lo = (a - a_hi.astype(F32)).astype(BF16)
    b_lo = (b - b_hi.astype(F32)).astype(BF16)
    d = functools.partial(jnp.dot, preferred_element_type=F32)
    return d(a_hi, b_hi) + (d(a_hi, b_lo) + d(a_lo, b_hi))


def _scan_masks(reverse):
    C = SCAN_CHUNK
    P = HEADS_PER_PAIR * C
    row = lax.broadcasted_iota(jnp.int32, (P, LANES), 0)
    col = lax.broadcasted_iota(jnp.int32, (P, LANES), 1)
    tr = row % C
    tc = col % C
    t_i = lax.broadcasted_iota(jnp.int32, (C, C), 0)
    s_i = lax.broadcasted_iota(jnp.int32, (C, C), 1)
    return dict(
        same_head=(row // C) == (col // RW_HEAD_DIM),
        incl=(tr >= tc) if not reverse else (tr <= tc),
        strict=(tr > tc) if not reverse else (tr < tc),
        diag_blk=(tr // SCAN_SUB) == (tc // SCAN_SUB),
        eye=row == col,
        cum_mat=jnp.where((t_i >= s_i) if not reverse else (t_i <= s_i), 1.0, 0.0).astype(F32))


def _scan_chunk(chains, states):
    C = SCAN_CHUNK
    n = range(len(chains))
    mk = [ch[6] for ch in chains]
    cum = [_dot3(mk[i]["cum_mat"], chains[i][1]) for i in n]
    pre = []
    for i in n:
        r, lw, k, v, kk, a, m, reverse = chains[i]
        g = jnp.exp(cum[i])
        g_prev = jnp.exp(cum[i] - lw)
        g_inv = jnp.exp(-cum[i])
        g_end = jnp.exp(cum[i][C - 1:C, :] if not reverse else cum[i][0:1, :])
        bt = kk * a * g_inv
        kt = k * g_inv
        stack = lambda x, m=m: jnp.where(m["same_head"], jnp.concatenate([x, x], axis=0), 0.0)
        pre.append(dict(kap=stack(kk * g_prev), rt=stack(r * g), bt=stack(bt), kt=stack(kt),
                        v=stack(v), bh=stack(bt * g_end), kh=stack(kt * g_end), g_end=g_end))
    P = HEADS_PER_PAIR * C
    tri = [_dot_nt(jnp.concatenate([pre[i]["kap"], pre[i]["rt"]], axis=0),
                   jnp.concatenate([pre[i]["bt"], pre[i]["kt"]], axis=0)) for i in n]
    lb = [jnp.where(mk[i]["strict"], tri[i][:P, :P], 0.0) for i in n]
    lk = [jnp.where(mk[i]["strict"], tri[i][:P, P:], 0.0) for i in n]
    pb = [jnp.where(mk[i]["incl"], tri[i][P:, :P], 0.0) for i in n]
    pk = [jnp.where(mk[i]["incl"], tri[i][P:, P:], 0.0) for i in n]
    lkv = [_dot(lk[i], pre[i]["v"]) for i in n]

    d = [jnp.where(mk[i]["diag_blk"], lb[i], 0.0) for i in n]
    x = [jnp.concatenate([lb[i] - d[i], pre[i]["kap"], lkv[i]], axis=1) for i in n]
    x = [x[i] - _dot(d[i], x[i]) for i in n]
    p = [_dot(d[i], d[i]) for i in n]
    x = [x[i] + _dot(p[i], x[i]) for i in n]
    p = [_dot(p[i], p[i]) for i in n]
    x = [x[i] + _dot(p[i], x[i]) for i in n]
    p = [_dot(p[i], p[i]) for i in n]
    x = [x[i] + _dot(p[i], x[i]) for i in n]
    e = [x[i][:, :LANES] for i in n]
    rhs = [x[i][:, LANES:] for i in n]
    e2 = [_dot(e[i], e[i]) for i in n]
    t = [rhs[i] + _dot(e2[i], rhs[i]) for i in n]
    gy = [t[i] - _dot(e[i], t[i]) for i in n]

    m_mat = [jnp.where(mk[i]["eye"], pre[i]["g_end"], 0.0) - _dot_tn(pre[i]["bh"], gy[i][:, :LANES])
             for i in n]
    n_mat = [_dot_tn(pre[i]["kh"], pre[i]["v"]) - _dot_tn(pre[i]["bh"], gy[i][:, LANES:]) for i in n]
    pbgy = [_dot(pb[i], gy[i]) for i in n]
    q = [pre[i]["rt"] - pbgy[i][:, :LANES] for i in n]
    z = [_dot(pk[i], pre[i]["v"]) - pbgy[i][:, LANES:] for i in n]
    o_st = [_dot3(q[i], states[i]) + z[i] for i in n]
    new = [_dot3(m_mat[i], states[i]) + n_mat[i] for i in n]
    return [(o_st[i][:C] + o_st[i][C:], new[i]) for i in n]


def _scan_kernel(rf, vf, kkf, lwf, af, kf, rb, vb, kkb, lwb, ab, kb, s0, of, ob, sfin, state):
    c = pl.program_id(1)
    n_pairs = rf.shape[-1] // LANES

    @pl.when(c == 0)
    def _():
        state[...] = s0[0]

    chains, states, outs = [], [], []
    for d, (r_, v_, kk_, lw_, a_, k_, o_) in enumerate(((rf, vf, kkf, lwf, af, kf, of),
                                                        (rb, vb, kkb, lwb, ab, kb, ob))):
        masks = _scan_masks(reverse=(d == 1))
        for p in range(n_pairs):
            sl = slice(p * LANES, (p + 1) * LANES)
            chains.append((r_[:, sl], lw_[:, sl], k_[:, sl], v_[:, sl], kk_[:, sl], a_[:, sl],
                           masks, d == 1))
            states.append(state[d, p])
            outs.append((o_, sl, d, p))
    for (o, new), (o_, sl, d, p) in zip(_scan_chunk(chains, states), outs):
        o_[:, sl] = o
        state[d, p] = new

    @pl.when(c == pl.num_programs(1) - 1)
    def _():
        sfin[0] = state[...]


def rwkv_scan(r, v, kk, lw, a, kd, s0_bd, row0, n_seq, seq_len):
    width = r.shape[1]
    n_tok = n_seq * seq_len
    n_pairs = width // LANES
    n_chunk = seq_len // SCAN_CHUNK
    c0 = row0 // SCAN_CHUNK
    blk = (SCAN_CHUNK, width)
    dblk = (None, SCAN_CHUNK, width)
    fwd = lambda s, c: (s * n_chunk + c, 0)
    bwd = lambda s, c: (s * n_chunk + n_chunk - 1 - c, 0)
    fwd_in = lambda s, c: (c0 + s * n_chunk + c, 0)
    bwd_in = lambda s, c: (c0 + s * n_chunk + n_chunk - 1 - c, 0)
    fwd_d = lambda s, c: (0, c0 + s * n_chunk + c, 0)
    bwd_d = lambda s, c: (1, c0 + s * n_chunk + n_chunk - 1 - c, 0)
    sspec = pl.BlockSpec((1, 2, n_pairs, LANES, LANES), lambda s, c: (s, 0, 0, 0, 0))
    in_specs = ([pl.BlockSpec(blk, fwd_in)] * 3 + [pl.BlockSpec(dblk, fwd_d)] * 3
                + [pl.BlockSpec(blk, bwd_in)] * 3 + [pl.BlockSpec(dblk, bwd_d)] * 3 + [sspec])
    out_specs = [pl.BlockSpec(blk, fwd), pl.BlockSpec(blk, bwd), sspec]
    out_shape = [jax.ShapeDtypeStruct((n_tok, width), F32)] * 2 + [
        jax.ShapeDtypeStruct(s0_bd.shape, F32)]
    return pl.pallas_call(
        _scan_kernel,
        grid=(n_seq, n_chunk),
        in_specs=in_specs,
        out_specs=out_specs,
        out_shape=out_shape,
        scratch_shapes=[pltpu.VMEM((2, n_pairs, LANES, LANES), F32)],
        compiler_params=pltpu.CompilerParams(
            dimension_semantics=("parallel", "arbitrary"), vmem_limit_bytes=VMEM_LIMIT),
        name="rwkv_scan",
    )(r, v, kk, lw, a, kd, r, v, kk, lw, a, kd, s0_bd)


MOD_ROWS = 8
ROW_BLOCK = 256
RMS_EPS = 1e-6
LN_EPS = 1e-5
RW_GN_EPS = 64e-5


def _mod_kernel(c_ref, w_ref, b_ref, o_ref):
    c = c_ref[...]
    s = c * jax.nn.sigmoid(c)
    o_ref[...] = _dot(s, w_ref[...]) + b_ref[...]


def modulation(cond, w_mod, b_mod):
    n, d = cond.shape
    cols = w_mod.shape[1]
    blk = 1024
    return pl.pallas_call(
        _mod_kernel,
        grid=(cols // blk,),
        in_specs=[pl.BlockSpec((n, d), lambda j: (0, 0)),
                  pl.BlockSpec((d, blk), lambda j: (0, j)),
                  pl.BlockSpec((1, blk), lambda j: (0, j))],
        out_specs=pl.BlockSpec((n, blk), lambda j: (0, j)),
        out_shape=jax.ShapeDtypeStruct((n, cols), F32),
        name="modulation",
    )(cond, w_mod, b_mod)


def _modulate(x, shift, scale):
    ms = jnp.mean(x * x, axis=-1, keepdims=True)
    return x * lax.rsqrt(ms + RMS_EPS) * (1.0 + scale) + shift


def _inproj_kernel(x_ref, mod_ref, *refs):
    n = len(refs) // 2
    h = _modulate(x_ref[...], mod_ref[0, 0:1, :], mod_ref[0, 1:2, :]).astype(BF16)
    for w_ref, o_ref in zip(refs[:n], refs[n:]):
        o_ref[...] = jnp.dot(h, w_ref[...], preferred_element_type=F32)


def in_projection(x, mod_blocks, weights):
    n_tok, d = x.shape
    row = lambda i: (i, 0)
    const = lambda i: (0, 0)
    return pl.pallas_call(
        _inproj_kernel,
        grid=(n_tok // ROW_BLOCK,),
        in_specs=[pl.BlockSpec((ROW_BLOCK, d), row),
                  pl.BlockSpec((1, MOD_ROWS, d), lambda i: (i, 0, 0))]
                 + [pl.BlockSpec(w.shape, const, pipeline_mode=pl.Buffered(1)) for w in weights],
        out_specs=[pl.BlockSpec((ROW_BLOCK, w.shape[1]), row) for w in weights],
        out_shape=[jax.ShapeDtypeStruct((n_tok, w.shape[1]), F32) for w in weights],
        compiler_params=pltpu.CompilerParams(
            dimension_semantics=("parallel",), vmem_limit_bytes=VMEM_LIMIT),
        name="in_projection",
    )(x, mod_blocks, *weights)


HEAD_DIM = 64


def _group_ones(width):
    r = lax.broadcasted_iota(jnp.int32, (width, width), 0) // HEAD_DIM
    c = lax.broadcasted_iota(jnp.int32, (width, width), 1) // HEAD_DIM
    return jnp.where(r == c, 1.0, 0.0).astype(BF16)


def _group_sum(x, ones):
    hi = x.astype(BF16)
    lo = (x - hi.astype(F32)).astype(BF16)
    d = functools.partial(jnp.dot, preferred_element_type=F32)
    return d(hi, ones) + d(lo, ones)


RW_WIDTH = 512
RW_COLS = 3 * RW_WIDTH + 3 * LANES
RW_VEC_ROWS = 8


def _rwkv_prep_kernel(z_ref, hp_ref, hn_ref, mu_ref, vec_ref, w2_ref, a2_ref, g2_ref,
                      r_ref, v_ref, kk_ref, lw_ref, a_ref, kd_ref, bonus_ref, gate_ref):
    z = z_ref[...]
    rb = z.shape[0]
    row = lax.broadcasted_iota(jnp.int32, (rb, 1), 0)
    z_prev = jnp.where(row == 0, hp_ref[0], pltpu.roll(z, 1, 0))
    z_next = jnp.where(row == rb - 1, hn_ref[0], pltpu.roll(z, rb - 1, 0))
    zs = z + mu_ref[0:1, :] * (z_prev - z) + mu_ref[1:2, :] * (z_next - z)
    w = RW_WIDTH
    r, k, v = zs[:, :w], zs[:, w:2 * w], zs[:, 2 * w:3 * w]
    zw = zs[:, 3 * w:3 * w + LANES]
    za = zs[:, 3 * w + LANES:3 * w + 2 * LANES]
    zg = zs[:, 3 * w + 2 * LANES:]
    ones = _group_ones(w)
    kk = k * vec_ref[0:1, :]
    kk = kk * lax.rsqrt(_group_sum(kk * kk, ones) + 1e-12)
    r_ref[...] = r
    v_ref[...] = v
    kk_ref[...] = kk
    tz = jnp.tanh(zw)
    for d in range(2):
        w_raw = vec_ref[3 + d:4 + d, :] + _dot(tz, w2_ref[d])
        lw_ref[d] = -math.exp(-0.5) * jax.nn.sigmoid(w_raw)
        a = jax.nn.sigmoid(vec_ref[5 + d:6 + d, :] + _dot(za, a2_ref[d]))
        a_ref[d] = a
        kd_ref[d] = k * (1.0 + (a - 1.0) * vec_ref[1:2, :])
    bonus_ref[...] = _group_sum(r * k * vec_ref[2:3, :], ones) * v
    gate_ref[...] = _dot(jax.nn.sigmoid(zg), g2_ref[...])


def rwkv_prep(z_rw, halo_prev, halo_next, mu, vecs, w2_pad, a2_pad, g2):
    n_tok = z_rw.shape[0]
    w = RW_WIDTH
    row = lambda i: (i, 0)
    const2 = lambda i: (0, 0)
    const3 = lambda i: (0, 0, 0)
    halo = pl.BlockSpec((1, 1, RW_COLS), lambda i: (i, 0, 0))
    one = pl.BlockSpec((ROW_BLOCK, w), row)
    two = pl.BlockSpec((2, ROW_BLOCK, w), lambda i: (0, i, 0))
    s1 = jax.ShapeDtypeStruct((n_tok, w), F32)
    s2 = jax.ShapeDtypeStruct((2, n_tok, w), F32)
    return pl.pallas_call(
        _rwkv_prep_kernel,
        grid=(n_tok // ROW_BLOCK,),
        in_specs=[pl.BlockSpec((ROW_BLOCK, RW_COLS), row), halo, halo,
                  pl.BlockSpec(mu.shape, const2), pl.BlockSpec(vecs.shape, const2),
                  pl.BlockSpec(w2_pad.shape, const3), pl.BlockSpec(a2_pad.shape, const3),
                  pl.BlockSpec(g2.shape, const2)],
        out_specs=[one, one, one, two, two, two, one, one],
        out_shape=[s1, s1, s1, s2, s2, s2, s1, s1],
        compiler_params=pltpu.CompilerParams(
            dimension_semantics=("parallel",), vmem_limit_bytes=VMEM_LIMIT),
        name="rwkv_prep",
    )(z_rw, halo_prev, halo_next, mu, vecs, w2_pad, a2_pad, g2)


CONV_WIDTH = 512
CONV_KERNEL = 31
CONV_PAD = 16
CONV_ROWS = 64


def _conv_kernel(z_ref, w_ref, vec_ref, o_ref, pad_ref):
    t = z_ref.shape[0]
    cw = CONV_WIDTH
    half = CONV_KERNEL // 2
    zeros = jnp.zeros((CONV_PAD, cw), F32)
    pad_ref[0:CONV_PAD, :] = zeros
    pad_ref[CONV_PAD + t:CONV_PAD + t + CONV_PAD, :] = zeros
    pad_ref[CONV_PAD:CONV_PAD + t, :] = z_ref[:, :cw] * jax.nn.sigmoid(z_ref[:, cw:])

    def chunk(i, carry):
        base = pl.multiple_of(i * CONV_ROWS, CONV_ROWS)
        window = pad_ref[pl.ds(base, CONV_ROWS + 2 * CONV_PAD), :]
        acc = jnp.zeros((CONV_ROWS, cw), F32)
        for j in range(CONV_KERNEL):
            lo = CONV_PAD - half + j
            acc = acc + w_ref[j:j + 1, :] * window[lo:lo + CONV_ROWS]
        u = acc + vec_ref[0:1, :]
        mu = jnp.mean(u, axis=-1, keepdims=True)
        cen = u - mu
        var = jnp.mean(cen * cen, axis=-1, keepdims=True)
        y = cen * lax.rsqrt(var + LN_EPS) * vec_ref[1:2, :] + vec_ref[2:3, :]
        o_ref[pl.ds(base, CONV_ROWS), :] = y * jax.nn.sigmoid(y)
        return carry

    lax.fori_loop(0, t // CONV_ROWS, chunk, 0)


def conv_module(z_cv, conv_w, vecs, row0, n_seq, seq_len):
    cw = CONV_WIDTH
    blk0 = row0 // seq_len
    return pl.pallas_call(
        _conv_kernel,
        grid=(n_seq,),
        in_specs=[pl.BlockSpec((seq_len, 2 * cw), lambda i: (blk0 + i, 0)),
                  pl.BlockSpec(conv_w.shape, lambda i: (0, 0)),
                  pl.BlockSpec(vecs.shape, lambda i: (0, 0))],
        out_specs=pl.BlockSpec((seq_len, cw), lambda i: (i, 0)),
        out_shape=jax.ShapeDtypeStruct((n_seq * seq_len, cw), F32),
        scratch_shapes=[pltpu.VMEM((seq_len + 2 * CONV_PAD, cw), F32)],
        compiler_params=pltpu.CompilerParams(
            dimension_semantics=("parallel",), vmem_limit_bytes=VMEM_LIMIT),
        name="conv_module",
    )(z_cv, conv_w, vecs)


ATT_Q_BLOCK = 256
ATT_HEADS = 8
ATT_KV_HEADS = 2
ATT_GROUP = ATT_HEADS // ATT_KV_HEADS


def _rot_partner():
    r = lax.broadcasted_iota(jnp.int32, (LANES, LANES), 0)
    c = lax.broadcasted_iota(jnp.int32, (LANES, LANES), 1)
    lo = (r % 32) < 16
    return jnp.where((c == r + 16) & lo, 1.0, jnp.where((c == r - 16) & ~lo, -1.0, 0.0)).astype(BF16)


def _attn_kernel(*refs, rotary):
    if rotary:
        (zq_ref, zkv_ref, nq_ref, nk_ref, cosq_ref, sinq_ref, cosk_ref, sink_ref, ck_ref, cv_ref,
         o_ref) = refs
    else:
        zq_ref, zkv_ref, nq_ref, nk_ref, o_ref, kn_ref = refs
    tq = zq_ref.shape[0]
    ones = _group_ones(LANES)
    lane = lax.broadcasted_iota(jnp.int32, (1, LANES), 1)
    first = lane < HEAD_DIM
    row_head = lax.broadcasted_iota(jnp.int32, (2 * tq, LANES), 0) // tq
    own = row_head == (lax.broadcasted_iota(jnp.int32, (2 * tq, LANES), 1) // HEAD_DIM)

    def norm(x, g):
        return x * lax.rsqrt(_group_sum(x * x, ones) * (1.0 / HEAD_DIM) + RMS_EPS) * g

    def rope(x, cos, sin):
        return x * cos + jnp.dot(x.astype(BF16), _rot_partner(), preferred_element_type=F32) * sin

    def dup(x):
        sw = pltpu.roll(x, HEAD_DIM, 1)
        return jnp.where(first, x, sw), jnp.where(first, sw, x)

    k = norm(zkv_ref[:, :LANES], nk_ref[...])
    v = zkv_ref[:, LANES:]
    if rotary:
        k = rope(k, cosk_ref[...], sink_ref[...])
        ck, cv = dup(ck_ref[...]), dup(cv_ref[...])
    else:
        kn_ref[...] = k
    kd, vd = dup(k), dup(v)
    for pair in range(ATT_HEADS // 2):
        g = pair // (ATT_GROUP // 2)
        q = norm(zq_ref[:, pair * LANES:(pair + 1) * LANES], nq_ref[...])
        if rotary:
            q = rope(q, cosq_ref[...], sinq_ref[...])
        q = q * (HEAD_DIM ** -0.5)
        qs = jnp.where(own, jnp.concatenate([q, q], axis=0), 0.0)
        s = _dot_nt(qs, kd[g])
        m = jnp.max(s, axis=-1, keepdims=True)
        if rotary:
            s2 = _dot_nt(qs, ck[g])
            m = jnp.maximum(m, jnp.max(s2, axis=-1, keepdims=True))
            p2 = jnp.exp(s2 - m)
        p = jnp.exp(s - m)
        den = jnp.sum(p, axis=-1, keepdims=True)
        o = _dot(p, vd[g])
        if rotary:
            den = den + jnp.sum(p2, axis=-1, keepdims=True)
            o = o + _dot(p2, cv[g])
        o = o / den
        o_ref[:, pair * LANES:(pair + 1) * LANES] = jnp.where(first, o[:tq], o[tq:])


def attention(z_q, z_kv, nq, nk, row0, n_seq, seq_len, rope=None, cache=None):
    qb = min(ATT_Q_BLOCK, seq_len)
    nqb = seq_len // qb
    qblk0 = row0 // qb
    sblk0 = row0 // seq_len
    wq = z_q.shape[1]
    in_specs = [pl.BlockSpec((qb, wq), lambda b, t: (qblk0 + b * nqb + t, 0)),
                pl.BlockSpec((seq_len, 2 * LANES), lambda b, t: (sblk0 + b, 0)),
                pl.BlockSpec((1, LANES), lambda b, t: (0, 0)),
                pl.BlockSpec((1, LANES), lambda b, t: (0, 0))]
    args = [z_q, z_kv, nq, nk]
    out_q = pl.BlockSpec((qb, wq), lambda b, t: (b * nqb + t, 0))
    y_shape = jax.ShapeDtypeStruct((n_seq * seq_len, wq), F32)
    if rope is None:
        out_specs = [out_q, pl.BlockSpec((seq_len, LANES), lambda b, t: (b, 0))]
        out_shape = [y_shape, jax.ShapeDtypeStruct((n_seq * seq_len, LANES), F32)]
    else:
        past = cache[0].shape[0] // n_seq
        in_specs += [pl.BlockSpec((qb, LANES), lambda b, t: (t, 0))] * 2
        in_specs += [pl.BlockSpec((seq_len, LANES), lambda b, t: (0, 0))] * 2
        in_specs += [pl.BlockSpec((past, LANES), lambda b, t: (b, 0))] * 2
        args += [rope[0], rope[1], rope[0], rope[1], cache[0], cache[1]]
        out_specs = out_q
        out_shape = y_shape
    return pl.pallas_call(
        functools.partial(_attn_kernel, rotary=rope is not None),
        grid=(n_seq, nqb),
        in_specs=in_specs,
        out_specs=out_specs,
        out_shape=out_shape,
        compiler_params=pltpu.CompilerParams(
            dimension_semantics=("parallel", "arbitrary"), vmem_limit_bytes=VMEM_LIMIT),
        name="attention",
    )(*args)


def _merge_kernel(x_ref, mod_ref, bonus_ref, gate_ref, zg_ref,
                  ofc_ref, ofl_ref, obc_ref, obl_ref, cvc_ref, cvl_ref, atc_ref, atl_ref,
                  gn_ref, wr_ref, wc_ref, wa_ref, wo_ref, wq_ref, xo_ref, h2_ref, q_ref, *, n_ctx_blocks):
    d = x_ref.shape[1]
    zg = zg_ref[...]
    is_ctx = pl.program_id(0) < n_ctx_blocks
    pick = lambda c_ref, l_ref: jnp.where(is_ctx, c_ref[...], l_ref[...])
    ones = _group_ones(ofc_ref.shape[1])
    o = pick(ofc_ref, ofl_ref) + pick(obc_ref, obl_ref)
    cen = o - _group_sum(o, ones) * (1.0 / HEAD_DIM)
    var = _group_sum(cen * cen, ones) * (1.0 / HEAD_DIM)
    o = cen * lax.rsqrt(var + RW_GN_EPS) * gn_ref[0:1, :] + gn_ref[1:2, :]
    y_rw = (o + bonus_ref[...]) * gate_ref[...]
    merged = (jax.nn.sigmoid(zg[:, :d]) * _dot(y_rw, wr_ref[...])
              + jax.nn.sigmoid(zg[:, d:2 * d]) * _dot(pick(cvc_ref, cvl_ref), wc_ref[...])
              + jax.nn.sigmoid(zg[:, 2 * d:]) * _dot(pick(atc_ref, atl_ref), wa_ref[...]))
    x = x_ref[...] + mod_ref[0, 2:3, :] * _dot(merged, wo_ref[...])
    xo_ref[...] = x
    h2 = _modulate(x, mod_ref[0, 3:4, :], mod_ref[0, 4:5, :]).astype(BF16)
    h2_ref[...] = h2
    q_ref[...] = jnp.dot(h2, wq_ref[...], preferred_element_type=F32)


def merge_project(x, mod_blocks, bonus, gate, z_gate, o_f, o_b, y_cv, y_at, gn, w_r, w_c, w_a, w_o, w_q):
    n_tok, d = x.shape
    nc = o_f[0].shape[0] // ROW_BLOCK
    row = lambda i: (i, 0)
    const = lambda i: (0, 0)
    ctx_row = lambda i: (jnp.minimum(i, nc - 1), 0)
    lat_row = lambda i: (jnp.maximum(i - nc, 0), 0)
    acts = [bonus, gate, z_gate]
    weights = [gn, w_r, w_c, w_a, w_o, w_q]
    pair_specs, pair_args = [], []
    for c_arr, l_arr in (o_f, o_b, y_cv, y_at):
        pair_specs += [pl.BlockSpec((ROW_BLOCK, c_arr.shape[1]), ctx_row),
                       pl.BlockSpec((ROW_BLOCK, l_arr.shape[1]), lat_row)]
        pair_args += [c_arr, l_arr]
    return pl.pallas_call(
        functools.partial(_merge_kernel, n_ctx_blocks=nc),
        grid=(n_tok // ROW_BLOCK,),
        in_specs=[pl.BlockSpec((ROW_BLOCK, d), row),
                  pl.BlockSpec((1, MOD_ROWS, d), lambda i: (i, 0, 0))]
                 + [pl.BlockSpec((ROW_BLOCK, a.shape[1]), row) for a in acts]
                 + pair_specs
                 + [pl.BlockSpec(w.shape, const) for w in weights],
        out_specs=[pl.BlockSpec((ROW_BLOCK, d), row), pl.BlockSpec((ROW_BLOCK, d), row),
                   pl.BlockSpec((ROW_BLOCK, w_q.shape[1]), row)],
        out_shape=[jax.ShapeDtypeStruct((n_tok, d), F32), jax.ShapeDtypeStruct((n_tok, d), BF16),
                   jax.ShapeDtypeStruct((n_tok, w_q.shape[1]), F32)],
        compiler_params=pltpu.CompilerParams(
            dimension_semantics=("parallel",), vmem_limit_bytes=VMEM_LIMIT),
        name="merge_project",
    )(x, mod_blocks, *acts, *pair_args, *weights)


PEER_HEADS = 8
PEER_N_KEYS = 128
PEER_HALF = 64
PEER_TOPK = 16
PEER_SEL_BLOCK = 128
PEER_LOOP_UNROLL = 8
PEER_RELAYOUT_TOKENS = 32
PEER_RELAYOUT_KEYS = 8
NEG = -1e30


def _dot3_nt(a, b):
    a_hi = a.astype(BF16)
    b_hi = b.astype(BF16)
    a_lo = (a - a_hi.astype(F32)).astype(BF16)
    b_lo = (b - b_hi.astype(F32)).astype(BF16)
    d = lambda x, y: lax.dot_general(x, y, (((1,), (1,)), ((), ())), preferred_element_type=F32)
    return d(a_hi, b_hi) + (d(a_hi, b_lo) + d(a_lo, b_hi))


def _extract_max(s, code):
    m = jnp.max(s, axis=0, keepdims=True)
    pos = jnp.min(jnp.where(s == m, code, 1e9), axis=0, keepdims=True)
    return m, pos, jnp.where(code == pos, NEG, s)


def _pair_candidates(a0, a1):
    k = PEER_TOPK
    sub = lax.broadcasted_iota(jnp.int32, (k, a0.shape[1]), 0).astype(F32)
    sub8 = sub[:8]
    vals = [a0[0:1] + a1]
    codes = [sub]
    for r0 in range(1, 8):
        limit = k // (r0 + 1)
        vals.append(jnp.where(sub8 < limit, a0[r0:r0 + 1] + a1[:8], NEG))
        codes.append(sub8 + float(r0 * k))
    vals.append(a0[8:] + a1[0:1])
    codes.append((sub8 + 8.0) * float(k))
    return jnp.concatenate(vals, axis=0), jnp.concatenate(codes, axis=0)


def _peer_select_kernel(q_ref, keys_ref, w_ref, top_v, top_i, slot_i, slot_j, slot_g,
                        rows_i, rows_j, rows_g, w3_ref):
    k = PEER_TOPK
    ct = q_ref.shape[0]
    key_code = lax.broadcasted_iota(jnp.int32, (PEER_N_KEYS, ct), 0).astype(F32)
    rank = lax.broadcasted_iota(jnp.int32, (k, ct), 0).astype(F32)
    for h in range(PEER_HEADS):
        st = _dot3_nt(keys_ref[h], q_ref[:, h * LANES:(h + 1) * LANES])
        for p in range(2):
            s = st[p * PEER_N_KEYS:(p + 1) * PEER_N_KEYS]
            for it in range(k):
                m, pos, s = _extract_max(s, key_code)
                top_v[p, it:it + 1, :] = m
                top_i[p, it:it + 1, :] = pos
        a0, a1 = top_v[0], top_v[1]
        i0, i1 = top_i[0], top_i[1]
        cand, code = _pair_candidates(a0, a1)
        best = a0[0:1] + a1[0:1]
        z = jnp.zeros_like(best)
        for it in range(k):
            m, pos, cand = _extract_max(cand, code)
            r0 = jnp.floor(pos * (1.0 / k))
            r1 = pos - r0 * k
            e = jnp.exp(m - best)
            z = z + e
            slot = h * k + it
            slot_i[slot:slot + 1, :] = jnp.sum(jnp.where(rank == r0, i0, 0.0), axis=0, keepdims=True)
            slot_j[slot:slot + 1, :] = jnp.sum(jnp.where(rank == r1, i1, 0.0), axis=0, keepdims=True)
            slot_g[slot:slot + 1, :] = e
        slot_g[h * k:(h + 1) * k, :] = slot_g[h * k:(h + 1) * k, :] / z

    rows_i[...] = slot_i[...].T
    rows_j[...] = slot_j[...].T
    rows_g[...] = slot_g[...].T
    n = PEER_N_KEYS
    sub = lax.broadcasted_iota(jnp.int32, (n, n), 0).astype(F32)

    def per_token(c, carry):
        a_t = jnp.where(sub == rows_i[pl.ds(c, 1), :], rows_g[pl.ds(c, 1), :], 0.0)
        b_t = jnp.where(sub == rows_j[pl.ds(c, 1), :], 1.0, 0.0)
        w3_ref[c] = _dot_nt(a_t, b_t)
        return carry

    lax.fori_loop(0, ct, per_token, 0, unroll=PEER_LOOP_UNROLL)

    tg, ig = PEER_RELAYOUT_TOKENS, PEER_RELAYOUT_KEYS
    r_out = lax.broadcasted_iota(jnp.int32, (tg * ig, tg * ig), 0)
    r_in = lax.broadcasted_iota(jnp.int32, (tg * ig, tg * ig), 1)
    perm = jnp.where(r_in == (r_out % tg) * ig + r_out // tg, 1.0, 0.0).astype(BF16)

    def per_slab(ib, carry):
        for t in range(ct // tg):
            z = w3_ref[t * tg:(t + 1) * tg, pl.ds(pl.multiple_of(ib * ig, ig), ig), :]
            y = jnp.dot(perm, z.reshape(tg * ig, n).astype(BF16), preferred_element_type=F32)
            for kk in range(ig):
                col = pl.multiple_of((ib * ig + kk) * n, n)
                w_ref[t * tg:(t + 1) * tg, pl.ds(col, n)] = y[kk * tg:(kk + 1) * tg].astype(BF16)
        return carry

    lax.fori_loop(0, n // ig, per_slab, 0)


def peer_select(q, keys_padded):
    n_tok, qd = q.shape
    ct = PEER_SEL_BLOCK
    n = PEER_N_KEYS
    slots = PEER_HEADS * PEER_TOPK
    return pl.pallas_call(
        _peer_select_kernel,
        grid=(n_tok // ct,),
        in_specs=[pl.BlockSpec((ct, qd), lambda i: (i, 0)),
                  pl.BlockSpec(keys_padded.shape, lambda i: (0, 0, 0))],
        out_specs=pl.BlockSpec((ct, n * n), lambda i: (i, 0)),
        out_shape=jax.ShapeDtypeStruct((n_tok, n * n), BF16),
        scratch_shapes=[pltpu.VMEM((2, PEER_TOPK, ct), F32), pltpu.VMEM((2, PEER_TOPK, ct), F32),
                        pltpu.VMEM((slots, ct), F32), pltpu.VMEM((slots, ct), F32),
                        pltpu.VMEM((slots, ct), F32),
                        pltpu.VMEM((ct, slots), F32), pltpu.VMEM((ct, slots), F32),
                        pltpu.VMEM((ct, slots), F32),
                        pltpu.VMEM((ct, n, n), F32)],
        compiler_params=pltpu.CompilerParams(
            dimension_semantics=("parallel",), vmem_limit_bytes=VMEM_LIMIT),
        name="peer_select",
    )(q, keys_padded)


PEER_TOKEN_BLOCK = 1024
PEER_EXPERT_BLOCK = 1024


def _peer_kernel(h_ref, w_ref, u_ref, v_ref, x_ref, mod_ref, o_ref, acc_ref):
    e = pl.program_id(1)

    @pl.when(e == 0)
    def _():
        acc_ref[...] = jnp.zeros_like(acc_ref)

    s = _dot_nt(h_ref[...], u_ref[...])
    act = 0.5 * s * (1.0 + lax.erf(s * (2.0 ** -0.5)))
    acc_ref[...] += _dot(act * w_ref[...].astype(F32), v_ref[...])

    @pl.when(e == pl.num_programs(1) - 1)
    def _():
        for j in range(mod_ref.shape[0]):
            rows = slice(j * ROW_BLOCK, (j + 1) * ROW_BLOCK)
            o_ref[rows, :] = x_ref[rows, :] + mod_ref[j, 5:6, :] * acc_ref[rows, :]


def peer_dense(h2, w_sel, u_tab, v_tab, x, mod_blocks):
    n_tok, d = x.shape
    n_exp = u_tab.shape[0]
    tb, eb = PEER_TOKEN_BLOCK, PEER_EXPERT_BLOCK
    tok = lambda i, e: (i, 0)
    return pl.pallas_call(
        _peer_kernel,
        grid=(n_tok // tb, n_exp // eb),
        in_specs=[pl.BlockSpec((tb, d), tok),
                  pl.BlockSpec((tb, eb), lambda i, e: (i, e)),
                  pl.BlockSpec((eb, d), lambda i, e: (e, 0)),
                  pl.BlockSpec((eb, d), lambda i, e: (e, 0)),
                  pl.BlockSpec((tb, d), tok),
                  pl.BlockSpec((tb // ROW_BLOCK, MOD_ROWS, d), lambda i, e: (i, 0, 0))],
        out_specs=pl.BlockSpec((tb, d), tok),
        out_shape=jax.ShapeDtypeStruct((n_tok, d), F32),
        scratch_shapes=[pltpu.VMEM((tb, d), F32)],
        compiler_params=pltpu.CompilerParams(
            dimension_semantics=("parallel", "arbitrary"), vmem_limit_bytes=VMEM_LIMIT),
        name="peer_dense",
    )(h2, w_sel, u_tab, v_tab, x, mod_blocks)


RW_HEADS = RW_WIDTH // RW_HEAD_DIM
ROT_HALF = HEAD_DIM // 2
ROPE_THETA = 10000.0
GRID_W = 64


def _split(z, sizes):
    parts, start = [], 0
    for s in sizes:
        parts.append(z[..., start:start + s])
        start += s
    return parts


def _bd_from_states(s):
    a = jnp.swapaxes(s, -1, -2)
    n, d, h, m, _ = a.shape
    a = a.reshape(n, d, h // 2, 2, m, m)
    z = jnp.zeros_like(a[:, :, :, 0])
    top = jnp.concatenate([a[:, :, :, 0], z], axis=-1)
    bot = jnp.concatenate([z, a[:, :, :, 1]], axis=-1)
    return jnp.concatenate([top, bot], axis=-2)


def _states_from_bd(bd):
    n, d, p, _, _ = bd.shape
    m = RW_HEAD_DIM
    a = jnp.stack([bd[:, :, :, :m, :m], bd[:, :, :, m:, m:]], axis=3).reshape(n, d, 2 * p, m, m)
    return jnp.swapaxes(a, -1, -2)


def _rope_tables(t):
    rows = t // GRID_W
    row, col = jnp.meshgrid(jnp.arange(rows), jnp.arange(GRID_W), indexing='ij')
    inv = ROPE_THETA ** (-jnp.arange(0, ROT_HALF, 2, dtype=F32) / ROT_HALF)
    ang_r = row.reshape(-1, 1).astype(F32) * inv
    ang_c = col.reshape(-1, 1).astype(F32) * inv
    cr, sr, cc, sc = jnp.cos(ang_r), jnp.sin(ang_r), jnp.cos(ang_c), jnp.sin(ang_c)
    cos = jnp.concatenate([cr, cr, cc, cc], axis=-1)
    sin = jnp.concatenate([sr, sr, sc, sc], axis=-1)
    return jnp.tile(cos, (1, LANES // HEAD_DIM)), jnp.tile(sin, (1, LANES // HEAD_DIM))


def _pad_rows(w, top):
    z = jnp.zeros_like(w)
    return jnp.concatenate([w, z] if top else [z, w], axis=0)


def kernel(x_prompt, x_sample, c, cache_k, cache_v, state_wkv, c_ctx, w_mod, b_mod, w_in, shift_mu_prev, shift_mu_next, rw_w0, rw_w2, rw_a0, rw_a2, rw_g2, rw_k_k, rw_k_a, rw_r_k, rw_gn_w, rw_gn_b, w_out_rwkv, conv_w, conv_b, conv_ln_w, conv_ln_b, w_out_conv, q_norm, k_norm, w_out_attn, w_o, peer_wq, peer_keys, peer_u, peer_v):
    nb, seq, d = x_prompt.shape
    db, dseq, _ = x_sample.shape
    depth = w_in.shape[0]
    n_ctx = nb * seq
    n_lat = db * dseq
    n_tok = n_ctx + n_lat
    x = jnp.concatenate([x_prompt.reshape(n_ctx, d), x_sample.reshape(n_lat, d)], axis=0)

    cond = jnp.concatenate([c_ctx[None], c, jnp.zeros((MOD_ROWS - 1 - db, d), F32)], axis=0)
    blk_cond = jnp.concatenate([jnp.zeros((n_ctx // ROW_BLOCK,), jnp.int32),
                                1 + jnp.arange(n_lat // ROW_BLOCK, dtype=jnp.int32) // (dseq // ROW_BLOCK)])
    n_blk = n_tok // ROW_BLOCK
    blk_row = jnp.arange(n_blk) * ROW_BLOCK
    blk_pos = jnp.where(blk_row < n_ctx, blk_row % seq, (blk_row - n_ctx) % dseq)
    blk_len = jnp.where(blk_row < n_ctx, seq, dseq)
    starts_seq = (blk_pos == 0)[:, None]
    ends_seq = (blk_pos + ROW_BLOCK == blk_len)[:, None]
    rope = _rope_tables(dseq)
    in_sizes = (RW_COLS, 2 * CONV_WIDTH, ATT_HEADS * HEAD_DIM, 2 * ATT_KV_HEADS * HEAD_DIM, 3 * d)
    past = cache_k.shape[2]

    ctx_k, ctx_v, ctx_s = [], [], []
    for l in range(depth):
        mod = modulation(cond, w_mod[l].astype(BF16), b_mod[l][None])
        mod = mod.reshape(MOD_ROWS, 6, d)[blk_cond]
        mod_blocks = jnp.concatenate([mod, jnp.zeros((mod.shape[0], MOD_ROWS - 6, d), F32)], axis=1)

        z_rw, z_cv, z_q, z_kv, z_gate = in_projection(x, mod_blocks, _split(w_in[l].astype(BF16), in_sizes))

        zb = z_rw.reshape(n_blk, ROW_BLOCK, RW_COLS)
        halo_prev = jnp.where(starts_seq, 0.0, jnp.roll(zb[:, -1], 1, axis=0))[:, None]
        halo_next = jnp.where(ends_seq, 0.0, jnp.roll(zb[:, 0], -1, axis=0))[:, None]
        mu = jnp.stack([shift_mu_prev[l], shift_mu_next[l]])
        vecs = jnp.stack([rw_k_k[l], rw_k_a[l], rw_r_k[l], rw_w0[l, 0], rw_w0[l, 1], rw_a0[l, 0],
                          rw_a0[l, 1], jnp.zeros((RW_WIDTH,), F32)])
        w2_pad = jnp.stack([_pad_rows(rw_w2[l, 0], True), _pad_rows(rw_w2[l, 1], False)]).astype(BF16)
        a2_pad = jnp.stack([_pad_rows(rw_a2[l, 0], True), _pad_rows(rw_a2[l, 1], False)]).astype(BF16)
        r, v, kk, lw, aa, kd, bonus, gate = rwkv_prep(z_rw, halo_prev, halo_next, mu, vecs, w2_pad, a2_pad,
                                                      rw_g2[l].astype(BF16))
        s0_ctx = jnp.zeros((nb, 2, RW_HEADS // 2, LANES, LANES), F32)
        of_c, ob_c, sfin = rwkv_scan(r, v, kk, lw, aa, kd, s0_ctx, 0, nb, seq)
        of_l, ob_l, _ = rwkv_scan(r, v, kk, lw, aa, kd, _bd_from_states(state_wkv[:, l]), n_ctx, db, dseq)
        ctx_s.append(_states_from_bd(sfin))

        cvec = jnp.stack([conv_b[l], conv_ln_w[l], conv_ln_b[l]] + [jnp.zeros((CONV_WIDTH,), F32)] * 5)
        y_cv = (conv_module(z_cv, conv_w[l], cvec, 0, nb, seq),
                conv_module(z_cv, conv_w[l], cvec, n_ctx, db, dseq))

        nq = jnp.tile(q_norm[l], LANES // HEAD_DIM)[None]
        nk = jnp.tile(k_norm[l], LANES // HEAD_DIM)[None]
        y_c, k_ctx = attention(z_q, z_kv, nq, nk, 0, nb, seq)
        y_l = attention(z_q, z_kv, nq, nk, n_ctx, db, dseq, rope=rope,
                        cache=(cache_k[:, l].reshape(db * past, LANES), cache_v[:, l].reshape(db * past, LANES)))
        ctx_k.append(k_ctx.reshape(nb, seq, ATT_KV_HEADS, HEAD_DIM))
        ctx_v.append(z_kv[:n_ctx, LANES:].reshape(nb, seq, ATT_KV_HEADS, HEAD_DIM))

        x, h2, pq = merge_project(x, mod_blocks, bonus, gate, z_gate,
                                  (of_c, of_l), (ob_c, ob_l), y_cv, (y_c, y_l),
                                  jnp.stack([rw_gn_w[l], rw_gn_b[l]]),
                                  w_out_rwkv[l].astype(BF16), w_out_conv[l].astype(BF16),
                                  w_out_attn[l].astype(BF16), w_o[l].astype(BF16), peer_wq[l].astype(BF16))

        kz = jnp.zeros_like(peer_keys[l][:, 0])
        keys_padded = jnp.concatenate([jnp.concatenate([peer_keys[l][:, 0], kz], axis=-1),
                                       jnp.concatenate([kz, peer_keys[l][:, 1]], axis=-1)], axis=1)
        w_sel = peer_select(pq, keys_padded)
        x = peer_dense(h2, w_sel, peer_u[l].astype(BF16), peer_v[l].astype(BF16), x, mod_blocks)

    new_cache_k = jnp.stack(ctx_k, axis=1)
    new_cache_v = jnp.stack(ctx_v, axis=1)
    new_state = jnp.stack(ctx_s, axis=1)
    return (x[:n_ctx].reshape(nb, seq, d), x[n_ctx:].reshape(db, dseq, d),
            new_cache_k, new_cache_v, new_state)
```

```python
import functools
import math

import jax
import jax.numpy as jnp
from jax import lax
from jax.experimental import pallas as pl
from jax.experimental.pallas import tpu as pltpu

F32 = jnp.float32
BF16 = jnp.bfloat16

LANES = 128
SCAN_CHUNK = 64
SCAN_SUB = 16
RW_HEAD_DIM = 64
HEADS_PER_PAIR = LANES // RW_HEAD_DIM
VMEM_LIMIT = 56 * 1024 * 1024


def _dot(a, b):
    return jnp.dot(a.astype(BF16), b.astype(BF16), preferred_element_type=F32)


def _dot_nt(a, b):
    return lax.dot_general(a.astype(BF16), b.astype(BF16), (((1,), (1,)), ((), ())),
                           preferred_element_type=F32)


def _dot_tn(a, b):
    return lax.dot_general(a.astype(BF16), b.astype(BF16), (((0,), (0,)), ((), ())),
                           preferred_element_type=F32)


def _dot3(a, b):
    a_hi = a.astype(BF16)
    b_hi = b.astype(BF16)
    a_lo = (a - a_hi.astype(F32)).astype(BF16)
    b_lo = (b - b_hi.astype(F32)).astype(BF16)
    d = functools.partial(jnp.dot, preferred_element_type=F32)
    return d(a_hi, b_hi) + (d(a_hi, b_lo) + d(a_lo, b_hi))


def _scan_masks(reverse):
    C = SCAN_CHUNK
    P = HEADS_PER_PAIR * C
    row = lax.broadcasted_iota(jnp.int32, (P, LANES), 0)
    col = lax.broadcasted_iota(jnp.int32, (P, LANES), 1)
    tr = row % C
    tc = col % C
    t_i = lax.broadcasted_iota(jnp.int32, (C, C), 0)
    s_i = lax.broadcasted_iota(jnp.int32, (C, C), 1)
    return dict(
        same_head=(row // C) == (col // RW_HEAD_DIM),
        incl=(tr >= tc) if not reverse else (tr <= tc),
        strict=(tr > tc) if not reverse else (tr < tc),
        diag_blk=(tr // SCAN_SUB) == (tc // SCAN_SUB),
        eye=row == col,
        cum_mat=jnp.where((t_i >= s_i) if not reverse else (t_i <= s_i), 1.0, 0.0).astype(F32))


def _scan_chunk(chains, states):
    C = SCAN_CHUNK
    n = range(len(chains))
    mk = [ch[6] for ch in chains]
    cum = [ch[8] for ch in chains]
    pre = []
    for i in n:
        r, lw, k, v, kk, a, m, reverse, _ = chains[i]
        g = jnp.exp(cum[i])
        g_prev = jnp.exp(cum[i] - lw)
        g_inv = jnp.exp(-cum[i])
        g_end = jnp.exp(cum[i][C - 1:C, :] if not reverse else cum[i][0:1, :])
        bt = kk * a * g_inv
        kt = k * g_inv
        stack = lambda x, m=m: jnp.where(m["same_head"], jnp.concatenate([x, x], axis=0), 0.0)
        pre.append(dict(kap=stack(kk * g_prev), rt=stack(r * g), bt=stack(bt), kt=stack(kt),
                        v=stack(v), bh=stack(bt * g_end), kh=stack(kt * g_end), g_end=g_end))
    P = HEADS_PER_PAIR * C
    tri = [_dot_nt(jnp.concatenate([pre[i]["kap"], pre[i]["rt"]], axis=0),
                   jnp.concatenate([pre[i]["bt"], pre[i]["kt"]], axis=0)) for i in n]
    lb = [jnp.where(mk[i]["strict"], tri[i][:P, :P], 0.0) for i in n]
    lk = [jnp.where(mk[i]["strict"], tri[i][:P, P:], 0.0) for i in n]
    pb = [jnp.where(mk[i]["incl"], tri[i][P:, :P], 0.0) for i in n]
    pk = [jnp.where(mk[i]["incl"], tri[i][P:, P:], 0.0) for i in n]
    lkv = [_dot(lk[i], pre[i]["v"]) for i in n]

    d = [jnp.where(mk[i]["diag_blk"], lb[i], 0.0) for i in n]
    x = [jnp.concatenate([lb[i] - d[i], pre[i]["kap"], lkv[i]], axis=1) for i in n]
    x = [x[i] - _dot(d[i], x[i]) for i in n]
    p = [_dot(d[i], d[i]) for i in n]
    x = [x[i] + _dot(p[i], x[i]) for i in n]
    p = [_dot(p[i], p[i]) for i in n]
    x = [x[i] + _dot(p[i], x[i]) for i in n]
    p = [_dot(p[i], p[i]) for i in n]
    x = [x[i] + _dot(p[i], x[i]) for i in n]
    e = [x[i][:, :LANES] for i in n]
    rhs = [x[i][:, LANES:] for i in n]
    e2 = [_dot(e[i], e[i]) for i in n]
    t = [rhs[i] + _dot(e2[i], rhs[i]) for i in n]
    gy = [t[i] - _dot(e[i], t[i]) for i in n]

    m_mat = [jnp.where(mk[i]["eye"], pre[i]["g_end"], 0.0) - _dot_tn(pre[i]["bh"], gy[i][:, :LANES])
             for i in n]
    n_mat = [_dot_tn(pre[i]["kh"], pre[i]["v"]) - _dot_tn(pre[i]["bh"], gy[i][:, LANES:]) for i in n]
    pbgy = [_dot(pb[i], gy[i]) for i in n]
    q = [pre[i]["rt"] - pbgy[i][:, :LANES] for i in n]
    z = [_dot(pk[i], pre[i]["v"]) - pbgy[i][:, LANES:] for i in n]
    o_st = [_dot(q[i], states[i]) + z[i] for i in n]
    new = [_dot3(m_mat[i], states[i]) + n_mat[i] for i in n]
    return [(o_st[i][:C] + o_st[i][C:], new[i]) for i in n]


def _scan_kernel(rf, vf, kkf, lwf, af, kf, rb, vb, kkb, lwb, ab, kb, s0, of, ob, sfin, state):
    c = pl.program_id(1)
    n_pairs = rf.shape[-1] // LANES

    @pl.when(c == 0)
    def _():
        state[...] = s0[0]

    chains, states, outs = [], [], []
    for d, (r_, v_, kk_, lw_, a_, k_, o_) in enumerate(((rf, vf, kkf, lwf, af, kf, of),
                                                        (rb, vb, kkb, lwb, ab, kb, ob))):
        masks = _scan_masks(reverse=(d == 1))
        cum = _dot3(masks["cum_mat"], lw_[...])
        for p in range(n_pairs):
            sl = slice(p * LANES, (p + 1) * LANES)
            chains.append((r_[:, sl], lw_[:, sl], k_[:, sl], v_[:, sl], kk_[:, sl], a_[:, sl],
                           masks, d == 1, cum[:, sl]))
            states.append(state[d, p])
            outs.append((o_, sl, d, p))
    for (o, new), (o_, sl, d, p) in zip(_scan_chunk(chains, states), outs):
        o_[:, sl] = o
        state[d, p] = new

    @pl.when(c == pl.num_programs(1) - 1)
    def _():
        sfin[0] = state[...]


def rwkv_scan(r, v, kk, lw, a, kd, s0_bd, row0, n_seq, seq_len):
    width = r.shape[1]
    n_tok = n_seq * seq_len
    n_pairs = width // LANES
    n_chunk = seq_len // SCAN_CHUNK
    c0 = row0 // SCAN_CHUNK
    blk = (SCAN_CHUNK, width)
    dblk = (None, SCAN_CHUNK, width)
    fwd = lambda s, c: (s * n_chunk + c, 0)
    bwd = lambda s, c: (s * n_chunk + n_chunk - 1 - c, 0)
    fwd_in = lambda s, c: (c0 + s * n_chunk + c, 0)
    bwd_in = lambda s, c: (c0 + s * n_chunk + n_chunk - 1 - c, 0)
    fwd_d = lambda s, c: (0, c0 + s * n_chunk + c, 0)
    bwd_d = lambda s, c: (1, c0 + s * n_chunk + n_chunk - 1 - c, 0)
    sspec = pl.BlockSpec((1, 2, n_pairs, LANES, LANES), lambda s, c: (s, 0, 0, 0, 0))
    in_specs = ([pl.BlockSpec(blk, fwd_in)] * 3 + [pl.BlockSpec(dblk, fwd_d)] * 3
                + [pl.BlockSpec(blk, bwd_in)] * 3 + [pl.BlockSpec(dblk, bwd_d)] * 3 + [sspec])
    out_specs = [pl.BlockSpec(blk, fwd), pl.BlockSpec(blk, bwd), sspec]
    out_shape = [jax.ShapeDtypeStruct((n_tok, width), F32)] * 2 + [
        jax.ShapeDtypeStruct(s0_bd.shape, F32)]
    return pl.pallas_call(
        _scan_kernel,
        grid=(n_seq, n_chunk),
        in_specs=in_specs,
        out_specs=out_specs,
        out_shape=out_shape,
        scratch_shapes=[pltpu.VMEM((2, n_pairs, LANES, LANES), F32)],
        compiler_params=pltpu.CompilerParams(
            dimension_semantics=("parallel", "arbitrary"), vmem_limit_bytes=VMEM_LIMIT),
        name="rwkv_scan",
    )(r, v, kk, lw, a, kd, r, v, kk, lw, a, kd, s0_bd)


MOD_ROWS = 8
ROW_BLOCK = 256
RMS_EPS = 1e-6
LN_EPS = 1e-5
RW_GN_EPS = 64e-5


def _mod_kernel(c_ref, w_ref, b_ref, o_ref):
    c = c_ref[...]
    s = c * jax.nn.sigmoid(c)
    o_ref[...] = _dot(s, w_ref[...]) + b_ref[...]


def modulation(cond, w_mod, b_mod):
    n, d = cond.shape
    cols = w_mod.shape[1]
    blk = 1024
    return pl.pallas_call(
        _mod_kernel,
        grid=(cols // blk,),
        in_specs=[pl.BlockSpec((n, d), lambda j: (0, 0)),
                  pl.BlockSpec((d, blk), lambda j: (0, j)),
                  pl.BlockSpec((1, blk), lambda j: (0, j))],
        out_specs=pl.BlockSpec((n, blk), lambda j: (0, j)),
        out_shape=jax.ShapeDtypeStruct((n, cols), F32),
        name="modulation",
    )(cond, w_mod, b_mod)


def _modulate(x, shift, scale):
    ms = jnp.mean(x * x, axis=-1, keepdims=True)
    return x * lax.rsqrt(ms + RMS_EPS) * (1.0 + scale) + shift


def _inproj_kernel(x_ref, mod_ref, *refs):
    n = len(refs) // 2
    h = _modulate(x_ref[...], mod_ref[0, 0:1, :], mod_ref[0, 1:2, :]).astype(BF16)
    for w_ref, o_ref in zip(refs[:n], refs[n:]):
        o_ref[...] = jnp.dot(h, w_ref[...], preferred_element_type=F32)


def in_projection(x, mod_blocks, weights):
    n_tok, d = x.shape
    row = lambda i: (i, 0)
    const = lambda i: (0, 0)
    return pl.pallas_call(
        _inproj_kernel,
        grid=(n_tok // ROW_BLOCK,),
        in_specs=[pl.BlockSpec((ROW_BLOCK, d), row),
                  pl.BlockSpec((1, MOD_ROWS, d), lambda i: (i, 0, 0))]
                 + [pl.BlockSpec(w.shape, const, pipeline_mode=pl.Buffered(1)) for w in weights],
        out_specs=[pl.BlockSpec((ROW_BLOCK, w.shape[1]), row) for w in weights],
        out_shape=[jax.ShapeDtypeStruct((n_tok, w.shape[1]), F32) for w in weights],
        compiler_params=pltpu.CompilerParams(
            dimension_semantics=("parallel",), vmem_limit_bytes=VMEM_LIMIT),
        name="in_projection",
    )(x, mod_blocks, *weights)


HEAD_DIM = 64


def _group_ones(width):
    r = lax.broadcasted_iota(jnp.int32, (width, width), 0) // HEAD_DIM
    c = lax.broadcasted_iota(jnp.int32, (width, width), 1) // HEAD_DIM
    return jnp.where(r == c, 1.0, 0.0).astype(BF16)


def _group_sum(x, ones):
    hi = x.astype(BF16)
    lo = (x - hi.astype(F32)).astype(BF16)
    d = functools.partial(jnp.dot, preferred_element_type=F32)
    return d(hi, ones) + d(lo, ones)


RW_WIDTH = 512
RW_COLS = 3 * RW_WIDTH + 3 * LANES
RW_VEC_ROWS = 8


def _rwkv_prep_kernel(z_ref, hp_ref, hn_ref, mu_ref, vec_ref, w2_ref, a2_ref, g2_ref,
                      r_ref, v_ref, kk_ref, lw_ref, a_ref, kd_ref, bonus_ref, gate_ref):
    z = z_ref[...]
    rb = z.shape[0]
    row = lax.broadcasted_iota(jnp.int32, (rb, 1), 0)
    z_prev = jnp.where(row == 0, hp_ref[0], pltpu.roll(z, 1, 0))
    z_next = jnp.where(row == rb - 1, hn_ref[0], pltpu.roll(z, rb - 1, 0))
    zs = z + mu_ref[0:1, :] * (z_prev - z) + mu_ref[1:2, :] * (z_next - z)
    w = RW_WIDTH
    r, k, v = zs[:, :w], zs[:, w:2 * w], zs[:, 2 * w:3 * w]
    zw = zs[:, 3 * w:3 * w + LANES]
    za = zs[:, 3 * w + LANES:3 * w + 2 * LANES]
    zg = zs[:, 3 * w + 2 * LANES:]
    ones = _group_ones(w)
    kk = k * vec_ref[0:1, :]
    kk = kk * lax.rsqrt(_group_sum(kk * kk, ones) + 1e-12)
    r_ref[...] = r
    v_ref[...] = v
    kk_ref[...] = kk
    tz = jnp.tanh(zw)
    for d in range(2):
        w_raw = vec_ref[3 + d:4 + d, :] + _dot(tz, w2_ref[d])
        lw_ref[d] = -math.exp(-0.5) * jax.nn.sigmoid(w_raw)
        a = jax.nn.sigmoid(vec_ref[5 + d:6 + d, :] + _dot(za, a2_ref[d]))
        a_ref[d] = a
        kd_ref[d] = k * (1.0 + (a - 1.0) * vec_ref[1:2, :])
    bonus_ref[...] = _group_sum(r * k * vec_ref[2:3, :], ones) * v
    gate_ref[...] = _dot(jax.nn.sigmoid(zg), g2_ref[...])


def rwkv_prep(z_rw, halo_prev, halo_next, mu, vecs, w2_pad, a2_pad, g2):
    n_tok = z_rw.shape[0]
    w = RW_WIDTH
    row = lambda i: (i, 0)
    const2 = lambda i: (0, 0)
    const3 = lambda i: (0, 0, 0)
    halo = pl.BlockSpec((1, 1, RW_COLS), lambda i: (i, 0, 0))
    one = pl.BlockSpec((ROW_BLOCK, w), row)
    two = pl.BlockSpec((2, ROW_BLOCK, w), lambda i: (0, i, 0))
    s1 = jax.ShapeDtypeStruct((n_tok, w), F32)
    s2 = jax.ShapeDtypeStruct((2, n_tok, w), F32)
    return pl.pallas_call(
        _rwkv_prep_kernel,
        grid=(n_tok // ROW_BLOCK,),
        in_specs=[pl.BlockSpec((ROW_BLOCK, RW_COLS), row), halo, halo,
                  pl.BlockSpec(mu.shape, const2), pl.BlockSpec(vecs.shape, const2),
                  pl.BlockSpec(w2_pad.shape, const3), pl.BlockSpec(a2_pad.shape, const3),
                  pl.BlockSpec(g2.shape, const2)],
        out_specs=[one, one, one, two, two, two, one, one],
        out_shape=[s1, s1, s1, s2, s2, s2, s1, s1],
        compiler_params=pltpu.CompilerParams(
            dimension_semantics=("parallel",), vmem_limit_bytes=VMEM_LIMIT),
        name="rwkv_prep",
    )(z_rw, halo_prev, halo_next, mu, vecs, w2_pad, a2_pad, g2)


CONV_WIDTH = 512
CONV_KERNEL = 31
CONV_PAD = 16
CONV_ROWS = 64


def _conv_kernel(z_ref, w_ref, vec_ref, o_ref, pad_ref):
    t = z_ref.shape[0]
    cw = CONV_WIDTH
    half = CONV_KERNEL // 2
    zeros = jnp.zeros((CONV_PAD, cw), F32)
    pad_ref[0:CONV_PAD, :] = zeros
    pad_ref[CONV_PAD + t:CONV_PAD + t + CONV_PAD, :] = zeros
    pad_ref[CONV_PAD:CONV_PAD + t, :] = z_ref[:, :cw] * jax.nn.sigmoid(z_ref[:, cw:])

    def chunk(i, carry):
        base = pl.multiple_of(i * CONV_ROWS, CONV_ROWS)
        window = pad_ref[pl.ds(base, CONV_ROWS + 2 * CONV_PAD), :]
        acc = jnp.zeros((CONV_ROWS, cw), F32)
        for j in range(CONV_KERNEL):
            lo = CONV_PAD - half + j
            acc = acc + w_ref[j:j + 1, :] * window[lo:lo + CONV_ROWS]
        u = acc + vec_ref[0:1, :]
        mu = jnp.mean(u, axis=-1, keepdims=True)
        cen = u - mu
        var = jnp.mean(cen * cen, axis=-1, keepdims=True)
        y = cen * lax.rsqrt(var + LN_EPS) * vec_ref[1:2, :] + vec_ref[2:3, :]
        o_ref[pl.ds(base, CONV_ROWS), :] = y * jax.nn.sigmoid(y)
        return carry

    lax.fori_loop(0, t // CONV_ROWS, chunk, 0)


def conv_module(z_cv, conv_w, vecs, row0, n_seq, seq_len):
    cw = CONV_WIDTH
    blk0 = row0 // seq_len
    return pl.pallas_call(
        _conv_kernel,
        grid=(n_seq,),
        in_specs=[pl.BlockSpec((seq_len, 2 * cw), lambda i: (blk0 + i, 0)),
                  pl.BlockSpec(conv_w.shape, lambda i: (0, 0)),
                  pl.BlockSpec(vecs.shape, lambda i: (0, 0))],
        out_specs=pl.BlockSpec((seq_len, cw), lambda i: (i, 0)),
        out_shape=jax.ShapeDtypeStruct((n_seq * seq_len, cw), F32),
        scratch_shapes=[pltpu.VMEM((seq_len + 2 * CONV_PAD, cw), F32)],
        compiler_params=pltpu.CompilerParams(
            dimension_semantics=("parallel",), vmem_limit_bytes=VMEM_LIMIT),
        name="conv_module",
    )(z_cv, conv_w, vecs)


ATT_Q_BLOCK = 256
ATT_HEADS = 8
ATT_KV_HEADS = 2
ATT_GROUP = ATT_HEADS // ATT_KV_HEADS


def _rot_partner():
    r = lax.broadcasted_iota(jnp.int32, (LANES, LANES), 0)
    c = lax.broadcasted_iota(jnp.int32, (LANES, LANES), 1)
    lo = (r % 32) < 16
    return jnp.where((c == r + 16) & lo, 1.0, jnp.where((c == r - 16) & ~lo, -1.0, 0.0)).astype(BF16)


def _attn_kernel(*refs, rotary):
    if rotary:
        (zq_ref, zkv_ref, nq_ref, nk_ref, cosq_ref, sinq_ref, cosk_ref, sink_ref, ck_ref, cv_ref,
         o_ref) = refs
    else:
        zq_ref, zkv_ref, nq_ref, nk_ref, o_ref, kn_ref = refs
    tq = zq_ref.shape[0]
    ones = _group_ones(LANES)
    lane = lax.broadcasted_iota(jnp.int32, (1, LANES), 1)
    first = lane < HEAD_DIM
    row_head = lax.broadcasted_iota(jnp.int32, (2 * tq, LANES), 0) // tq
    own = row_head == (lax.broadcasted_iota(jnp.int32, (2 * tq, LANES), 1) // HEAD_DIM)

    def norm(x, g):
        return x * lax.rsqrt(_group_sum(x * x, ones) * (1.0 / HEAD_DIM) + RMS_EPS) * g

    def rope(x, cos, sin):
        return x * cos + jnp.dot(x.astype(BF16), _rot_partner(), preferred_element_type=F32) * sin

    def dup(x):
        sw = pltpu.roll(x, HEAD_DIM, 1)
        return jnp.where(first, x, sw), jnp.where(first, sw, x)

    k = norm(zkv_ref[:, :LANES], nk_ref[...])
    v = zkv_ref[:, LANES:]
    if rotary:
        k = rope(k, cosk_ref[...], sink_ref[...])
        ck, cv = dup(ck_ref[...]), dup(cv_ref[...])
    else:
        kn_ref[...] = k
    kd, vd = dup(k), dup(v)
    for pair in range(ATT_HEADS // 2):
        g = pair // (ATT_GROUP // 2)
        q = norm(zq_ref[:, pair * LANES:(pair + 1) * LANES], nq_ref[...])
        if rotary:
            q = rope(q, cosq_ref[...], sinq_ref[...])
        q = q * (HEAD_DIM ** -0.5)
        qs = jnp.where(own, jnp.concatenate([q, q], axis=0), 0.0)
        s = _dot_nt(qs, kd[g])
        m = jnp.max(s, axis=-1, keepdims=True)
        if rotary:
            s2 = _dot_nt(qs, ck[g])
            m = jnp.maximum(m, jnp.max(s2, axis=-1, keepdims=True))
            p2 = jnp.exp(s2 - m)
        p = jnp.exp(s - m)
        den = jnp.sum(p, axis=-1, keepdims=True)
        o = _dot(p, vd[g])
        if rotary:
            den = den + jnp.sum(p2, axis=-1, keepdims=True)
            o = o + _dot(p2, cv[g])
        o = o / den
        o_ref[:, pair * LANES:(pair + 1) * LANES] = jnp.where(first, o[:tq], o[tq:])


def attention(z_q, z_kv, nq, nk, row0, n_seq, seq_len, rope=None, cache=None):
    qb = min(ATT_Q_BLOCK, seq_len)
    nqb = seq_len // qb
    qblk0 = row0 // qb
    sblk0 = row0 // seq_len
    wq = z_q.shape[1]
    in_specs = [pl.BlockSpec((qb, wq), lambda b, t: (qblk0 + b * nqb + t, 0)),
                pl.BlockSpec((seq_len, 2 * LANES), lambda b, t: (sblk0 + b, 0)),
                pl.BlockSpec((1, LANES), lambda b, t: (0, 0)),
                pl.BlockSpec((1, LANES), lambda b, t: (0, 0))]
    args = [z_q, z_kv, nq, nk]
    out_q = pl.BlockSpec((qb, wq), lambda b, t: (b * nqb + t, 0))
    y_shape = jax.ShapeDtypeStruct((n_seq * seq_len, wq), F32)
    if rope is None:
        out_specs = [out_q, pl.BlockSpec((seq_len, LANES), lambda b, t: (b, 0))]
        out_shape = [y_shape, jax.ShapeDtypeStruct((n_seq * seq_len, LANES), F32)]
    else:
        past = cache[0].shape[0] // n_seq
        in_specs += [pl.BlockSpec((qb, LANES), lambda b, t: (t, 0))] * 2
        in_specs += [pl.BlockSpec((seq_len, LANES), lambda b, t: (0, 0))] * 2
        in_specs += [pl.BlockSpec((past, LANES), lambda b, t: (b, 0))] * 2
        args += [rope[0], rope[1], rope[0], rope[1], cache[0], cache[1]]
        out_specs = out_q
        out_shape = y_shape
    return pl.pallas_call(
        functools.partial(_attn_kernel, rotary=rope is not None),
        grid=(n_seq, nqb),
        in_specs=in_specs,
        out_specs=out_specs,
        out_shape=out_shape,
        compiler_params=pltpu.CompilerParams(
            dimension_semantics=("parallel", "arbitrary"), vmem_limit_bytes=VMEM_LIMIT),
        name="attention",
    )(*args)


def _merge_kernel(x_ref, mod_ref, bonus_ref, gate_ref, zg_ref,
                  ofc_ref, ofl_ref, obc_ref, obl_ref, cvc_ref, cvl_ref, atc_ref, atl_ref,
                  gn_ref, wr_ref, wc_ref, wa_ref, wo_ref, wq_ref, xo_ref, h2_ref, q_ref, *, n_ctx_blocks):
    d = x_ref.shape[1]
    zg = zg_ref[...]
    is_ctx = pl.program_id(0) < n_ctx_blocks
    pick = lambda c_ref, l_ref: jnp.where(is_ctx, c_ref[...], l_ref[...])
    ones = _group_ones(ofc_ref.shape[1])
    o = pick(ofc_ref, ofl_ref) + pick(obc_ref, obl_ref)
    cen = o - _group_sum(o, ones) * (1.0 / HEAD_DIM)
    var = _group_sum(cen * cen, ones) * (1.0 / HEAD_DIM)
    o = cen * lax.rsqrt(var + RW_GN_EPS) * gn_ref[0:1, :] + gn_ref[1:2, :]
    y_rw = (o + bonus_ref[...]) * gate_ref[...]
    merged = (jax.nn.sigmoid(zg[:, :d]) * _dot(y_rw, wr_ref[...])
              + jax.nn.sigmoid(zg[:, d:2 * d]) * _dot(pick(cvc_ref, cvl_ref), wc_ref[...])
              + jax.nn.sigmoid(zg[:, 2 * d:]) * _dot(pick(atc_ref, atl_ref), wa_ref[...]))
    x = x_ref[...] + mod_ref[0, 2:3, :] * _dot(merged, wo_ref[...])
    xo_ref[...] = x
    h2 = _modulate(x, mod_ref[0, 3:4, :], mod_ref[0, 4:5, :]).astype(BF16)
    h2_ref[...] = h2
    q_ref[...] = jnp.dot(h2, wq_ref[...], preferred_element_type=F32)


def merge_project(x, mod_blocks, bonus, gate, z_gate, o_f, o_b, y_cv, y_at, gn, w_r, w_c, w_a, w_o, w_q):
    n_tok, d = x.shape
    nc = o_f[0].shape[0] // ROW_BLOCK
    row = lambda i: (i, 0)
    const = lambda i: (0, 0)
    ctx_row = lambda i: (jnp.minimum(i, nc - 1), 0)
    lat_row = lambda i: (jnp.maximum(i - nc, 0), 0)
    acts = [bonus, gate, z_gate]
    weights = [gn, w_r, w_c, w_a, w_o, w_q]
    pair_specs, pair_args = [], []
    for c_arr, l_arr in (o_f, o_b, y_cv, y_at):
        pair_specs += [pl.BlockSpec((ROW_BLOCK, c_arr.shape[1]), ctx_row),
                       pl.BlockSpec((ROW_BLOCK, l_arr.shape[1]), lat_row)]
        pair_args += [c_arr, l_arr]
    return pl.pallas_call(
        functools.partial(_merge_kernel, n_ctx_blocks=nc),
        grid=(n_tok // ROW_BLOCK,),
        in_specs=[pl.BlockSpec((ROW_BLOCK, d), row),
                  pl.BlockSpec((1, MOD_ROWS, d), lambda i: (i, 0, 0))]
                 + [pl.BlockSpec((ROW_BLOCK, a.shape[1]), row) for a in acts]
                 + pair_specs
                 + [pl.BlockSpec(w.shape, const) for w in weights],
        out_specs=[pl.BlockSpec((ROW_BLOCK, d), row), pl.BlockSpec((ROW_BLOCK, d), row),
                   pl.BlockSpec((ROW_BLOCK, w_q.shape[1]), row)],
        out_shape=[jax.ShapeDtypeStruct((n_tok, d), F32), jax.ShapeDtypeStruct((n_tok, d), BF16),
                   jax.ShapeDtypeStruct((n_tok, w_q.shape[1]), F32)],
        compiler_params=pltpu.CompilerParams(
            dimension_semantics=("parallel",), vmem_limit_bytes=VMEM_LIMIT),
        name="merge_project",
    )(x, mod_blocks, *acts, *pair_args, *weights)


PEER_HEADS = 8
PEER_N_KEYS = 128
PEER_HALF = 64
PEER_TOPK = 16
PEER_SEL_BLOCK = 128
PEER_LOOP_UNROLL = 16
PEER_RELAYOUT_TOKENS = 32
PEER_RELAYOUT_KEYS = 8
NEG = -1e30


def _dot3_nt(a, b):
    a_hi = a.astype(BF16)
    b_hi = b.astype(BF16)
    a_lo = (a - a_hi.astype(F32)).astype(BF16)
    b_lo = (b - b_hi.astype(F32)).astype(BF16)
    d = lambda x, y: lax.dot_general(x, y, (((1,), (1,)), ((), ())), preferred_element_type=F32)
    return d(a_hi, b_hi) + (d(a_hi, b_lo) + d(a_lo, b_hi))


def _extract_max(s, code):
    m = jnp.max(s, axis=0, keepdims=True)
    pos = jnp.min(jnp.where(s == m, code, 1e9), axis=0, keepdims=True)
    return m, pos, jnp.where(code == pos, NEG, s)


def _pair_candidates(a0, a1):
    k = PEER_TOPK
    sub = lax.broadcasted_iota(jnp.int32, (k, a0.shape[1]), 0).astype(F32)
    sub8 = sub[:8]
    vals = [a0[0:1] + a1]
    codes = [sub]
    for r0 in range(1, 8):
        limit = k // (r0 + 1)
        vals.append(jnp.where(sub8 < limit, a0[r0:r0 + 1] + a1[:8], NEG))
        codes.append(sub8 + float(r0 * k))
    vals.append(a0[8:] + a1[0:1])
    codes.append((sub8 + 8.0) * float(k))
    return jnp.concatenate(vals, axis=0), jnp.concatenate(codes, axis=0)


def _peer_select_kernel(q_ref, keys_ref, w_ref, top_v, top_i, slot_i, slot_j, slot_g,
                        rows_i, rows_j, rows_g, w3_ref):
    k = PEER_TOPK
    ct = q_ref.shape[0]
    key_code = lax.broadcasted_iota(jnp.int32, (PEER_N_KEYS, ct), 0).astype(F32)
    rank = lax.broadcasted_iota(jnp.int32, (k, ct), 0).astype(F32)
    for h in range(PEER_HEADS):
        st = _dot3_nt(keys_ref[h], q_ref[:, h * LANES:(h + 1) * LANES])
        for p in range(2):
            s = st[p * PEER_N_KEYS:(p + 1) * PEER_N_KEYS]
            for it in range(k):
                m, pos, s = _extract_max(s, key_code)
                top_v[p, it:it + 1, :] = m
                top_i[p, it:it + 1, :] = pos
        a0, a1 = top_v[0], top_v[1]
        i0, i1 = top_i[0], top_i[1]
        cand, code = _pair_candidates(a0, a1)
        best = a0[0:1] + a1[0:1]
        z = jnp.zeros_like(best)
        for it in range(k):
            m, pos, cand = _extract_max(cand, code)
            r0 = jnp.floor(pos * (1.0 / k))
            r1 = pos - r0 * k
            e = jnp.exp(m - best)
            z = z + e
            slot = h * k + it
            slot_i[slot:slot + 1, :] = jnp.sum(jnp.where(rank == r0, i0, 0.0), axis=0, keepdims=True)
            slot_j[slot:slot + 1, :] = jnp.sum(jnp.where(rank == r1, i1, 0.0), axis=0, keepdims=True)
            slot_g[slot:slot + 1, :] = e
        slot_g[h * k:(h + 1) * k, :] = slot_g[h * k:(h + 1) * k, :] / z

    rows_i[...] = slot_i[...].T
    rows_j[...] = slot_j[...].T
    rows_g[...] = slot_g[...].T
    n = PEER_N_KEYS
    sub = lax.broadcasted_iota(jnp.int32, (n, n), 0).astype(F32)

    def per_token(c, carry):
        a_t = jnp.where(sub == rows_i[pl.ds(c, 1), :], rows_g[pl.ds(c, 1), :], 0.0)
        b_t = jnp.where(sub == rows_j[pl.ds(c, 1), :], 1.0, 0.0)
        w3_ref[c] = _dot_nt(a_t, b_t)
        return carry

    lax.fori_loop(0, ct, per_token, 0, unroll=PEER_LOOP_UNROLL)

    tg, ig = PEER_RELAYOUT_TOKENS, PEER_RELAYOUT_KEYS
    r_out = lax.broadcasted_iota(jnp.int32, (tg * ig, tg * ig), 0)
    r_in = lax.broadcasted_iota(jnp.int32, (tg * ig, tg * ig), 1)
    perm = jnp.where(r_in == (r_out % tg) * ig + r_out // tg, 1.0, 0.0).astype(BF16)

    for ib in range(n // ig):
        for t in range(ct // tg):
            z = w3_ref[t * tg:(t + 1) * tg, ib * ig:(ib + 1) * ig, :]
            y = jnp.dot(perm, z.reshape(tg * ig, n).astype(BF16), preferred_element_type=F32)
            for kk in range(ig):
                col = (ib * ig + kk) * n
                w_ref[t * tg:(t + 1) * tg, col:col + n] = y[kk * tg:(kk + 1) * tg].astype(BF16)


def peer_select(q, keys_padded):
    n_tok, qd = q.shape
    ct = PEER_SEL_BLOCK
    n = PEER_N_KEYS
    slots = PEER_HEADS * PEER_TOPK
    return pl.pallas_call(
        _peer_select_kernel,
        grid=(n_tok // ct,),
        in_specs=[pl.BlockSpec((ct, qd), lambda i: (i, 0)),
                  pl.BlockSpec(keys_padded.shape, lambda i: (0, 0, 0))],
        out_specs=pl.BlockSpec((ct, n * n), lambda i: (i, 0)),
        out_shape=jax.ShapeDtypeStruct((n_tok, n * n), BF16),
        scratch_shapes=[pltpu.VMEM((2, PEER_TOPK, ct), F32), pltpu.VMEM((2, PEER_TOPK, ct), F32),
                        pltpu.VMEM((slots, ct), F32), pltpu.VMEM((slots, ct), F32),
                        pltpu.VMEM((slots, ct), F32),
                        pltpu.VMEM((ct, slots), F32), pltpu.VMEM((ct, slots), F32),
                        pltpu.VMEM((ct, slots), F32),
                        pltpu.VMEM((ct, n, n), F32)],
        compiler_params=pltpu.CompilerParams(
            dimension_semantics=("parallel",), vmem_limit_bytes=VMEM_LIMIT),
        name="peer_select",
    )(q, keys_padded)


PEER_TOKEN_BLOCK = 1024
PEER_EXPERT_BLOCK = 1024


def _peer_kernel(h_ref, w_ref, u_ref, v_ref, x_ref, mod_ref, o_ref, acc_ref):
    e = pl.program_id(1)

    @pl.when(e == 0)
    def _():
        acc_ref[...] = jnp.zeros_like(acc_ref)

    s = _dot_nt(h_ref[...], u_ref[...])
    act = 0.5 * s * (1.0 + lax.erf(s * (2.0 ** -0.5)))
    acc_ref[...] += _dot(act * w_ref[...].astype(F32), v_ref[...])

    @pl.when(e == pl.num_programs(1) - 1)
    def _():
        for j in range(mod_ref.shape[0]):
            rows = slice(j * ROW_BLOCK, (j + 1) * ROW_BLOCK)
            o_ref[rows, :] = x_ref[rows, :] + mod_ref[j, 5:6, :] * acc_ref[rows, :]


def peer_dense(h2, w_sel, u_tab, v_tab, x, mod_blocks):
    n_tok, d = x.shape
    n_exp = u_tab.shape[0]
    tb, eb = PEER_TOKEN_BLOCK, PEER_EXPERT_BLOCK
    tok = lambda i, e: (i, 0)
    return pl.pallas_call(
        _peer_kernel,
        grid=(n_tok // tb, n_exp // eb),
        in_specs=[pl.BlockSpec((tb, d), tok),
                  pl.BlockSpec((tb, eb), lambda i, e: (i, e)),
                  pl.BlockSpec((eb, d), lambda i, e: (e, 0)),
                  pl.BlockSpec((eb, d), lambda i, e: (e, 0)),
                  pl.BlockSpec((tb, d), tok),
                  pl.BlockSpec((tb // ROW_BLOCK, MOD_ROWS, d), lambda i, e: (i, 0, 0))],
        out_specs=pl.BlockSpec((tb, d), tok),
        out_shape=jax.ShapeDtypeStruct((n_tok, d), F32),
        scratch_shapes=[pltpu.VMEM((tb, d), F32)],
        compiler_params=pltpu.CompilerParams(
            dimension_semantics=("parallel", "arbitrary"), vmem_limit_bytes=VMEM_LIMIT),
        name="peer_dense",
    )(h2, w_sel, u_tab, v_tab, x, mod_blocks)


RW_HEADS = RW_WIDTH // RW_HEAD_DIM
ROT_HALF = HEAD_DIM // 2
ROPE_THETA = 10000.0
GRID_W = 64


def _split(z, sizes):
    parts, start = [], 0
    for s in sizes:
        parts.append(z[..., start:start + s])
        start += s
    return parts


def _bd_from_states(s):
    a = jnp.swapaxes(s, -1, -2)
    n, d, h, m, _ = a.shape
    a = a.reshape(n, d, h // 2, 2, m, m)
    z = jnp.zeros_like(a[:, :, :, 0])
    top = jnp.concatenate([a[:, :, :, 0], z], axis=-1)
    bot = jnp.concatenate([z, a[:, :, :, 1]], axis=-1)
    return jnp.concatenate([top, bot], axis=-2)


def _states_from_bd(bd):
    n, d, p, _, _ = bd.shape
    m = RW_HEAD_DIM
    a = jnp.stack([bd[:, :, :, :m, :m], bd[:, :, :, m:, m:]], axis=3).reshape(n, d, 2 * p, m, m)
    return jnp.swapaxes(a, -1, -2)


def _rope_tables(t):
    rows = t // GRID_W
    row, col = jnp.meshgrid(jnp.arange(rows), jnp.arange(GRID_W), indexing='ij')
    inv = ROPE_THETA ** (-jnp.arange(0, ROT_HALF, 2, dtype=F32) / ROT_HALF)
    ang_r = row.reshape(-1, 1).astype(F32) * inv
    ang_c = col.reshape(-1, 1).astype(F32) * inv
    cr, sr, cc, sc = jnp.cos(ang_r), jnp.sin(ang_r), jnp.cos(ang_c), jnp.sin(ang_c)
    cos = jnp.concatenate([cr, cr, cc, cc], axis=-1)
    sin = jnp.concatenate([sr, sr, sc, sc], axis=-1)
    return jnp.tile(cos, (1, LANES // HEAD_DIM)), jnp.tile(sin, (1, LANES // HEAD_DIM))


def _pad_rows(w, top):
    z = jnp.zeros_like(w)
    return jnp.concatenate([w, z] if top else [z, w], axis=0)


def kernel(x_prompt, x_sample, c, cache_k, cache_v, state_wkv, c_ctx, w_mod, b_mod, w_in, shift_mu_prev, shift_mu_next, rw_w0, rw_w2, rw_a0, rw_a2, rw_g2, rw_k_k, rw_k_a, rw_r_k, rw_gn_w, rw_gn_b, w_out_rwkv, conv_w, conv_b, conv_ln_w, conv_ln_b, w_out_conv, q_norm, k_norm, w_out_attn, w_o, peer_wq, peer_keys, peer_u, peer_v):
    nb, seq, d = x_prompt.shape
    db, dseq, _ = x_sample.shape
    depth = w_in.shape[0]
    n_ctx = nb * seq
    n_lat = db * dseq
    n_tok = n_ctx + n_lat
    x = jnp.concatenate([x_prompt.reshape(n_ctx, d), x_sample.reshape(n_lat, d)], axis=0)

    cond = jnp.concatenate([c_ctx[None], c, jnp.zeros((MOD_ROWS - 1 - db, d), F32)], axis=0)
    blk_cond = jnp.concatenate([jnp.zeros((n_ctx // ROW_BLOCK,), jnp.int32),
                                1 + jnp.arange(n_lat // ROW_BLOCK, dtype=jnp.int32) // (dseq // ROW_BLOCK)])
    n_blk = n_tok // ROW_BLOCK
    blk_row = jnp.arange(n_blk) * ROW_BLOCK
    blk_pos = jnp.where(blk_row < n_ctx, blk_row % seq, (blk_row - n_ctx) % dseq)
    blk_len = jnp.where(blk_row < n_ctx, seq, dseq)
    starts_seq = (blk_pos == 0)[:, None]
    ends_seq = (blk_pos + ROW_BLOCK == blk_len)[:, None]
    rope = _rope_tables(dseq)
    in_sizes = (RW_COLS, 2 * CONV_WIDTH, ATT_HEADS * HEAD_DIM, 2 * ATT_KV_HEADS * HEAD_DIM, 3 * d)
    past = cache_k.shape[2]

    ctx_k, ctx_v, ctx_s = [], [], []
    for l in range(depth):
        mod = modulation(cond, w_mod[l].astype(BF16), b_mod[l][None])
        mod = mod.reshape(MOD_ROWS, 6, d)[blk_cond]
        mod_blocks = jnp.concatenate([mod, jnp.zeros((mod.shape[0], MOD_ROWS - 6, d), F32)], axis=1)

        z_rw, z_cv, z_q, z_kv, z_gate = in_projection(x, mod_blocks, _split(w_in[l].astype(BF16), in_sizes))

        zb = z_rw.reshape(n_blk, ROW_BLOCK, RW_COLS)
        halo_prev = jnp.where(starts_seq, 0.0, jnp.roll(zb[:, -1], 1, axis=0))[:, None]
        halo_next = jnp.where(ends_seq, 0.0, jnp.roll(zb[:, 0], -1, axis=0))[:, None]
        mu = jnp.stack([shift_mu_prev[l], shift_mu_next[l]])
        vecs = jnp.stack([rw_k_k[l], rw_k_a[l], rw_r_k[l], rw_w0[l, 0], rw_w0[l, 1], rw_a0[l, 0],
                          rw_a0[l, 1], jnp.zeros((RW_WIDTH,), F32)])
        w2_pad = jnp.stack([_pad_rows(rw_w2[l, 0], True), _pad_rows(rw_w2[l, 1], False)]).astype(BF16)
        a2_pad = jnp.stack([_pad_rows(rw_a2[l, 0], True), _pad_rows(rw_a2[l, 1], False)]).astype(BF16)
        r, v, kk, lw, aa, kd, bonus, gate = rwkv_prep(z_rw, halo_prev, halo_next, mu, vecs, w2_pad, a2_pad,
                                                      rw_g2[l].astype(BF16))
        s0_ctx = jnp.zeros((nb, 2, RW_HEADS // 2, LANES, LANES), F32)
        of_c, ob_c, sfin = rwkv_scan(r, v, kk, lw, aa, kd, s0_ctx, 0, nb, seq)
        of_l, ob_l, _ = rwkv_scan(r, v, kk, lw, aa, kd, _bd_from_states(state_wkv[:, l]), n_ctx, db, dseq)
        ctx_s.append(_states_from_bd(sfin))

        cvec = jnp.stack([conv_b[l], conv_ln_w[l], conv_ln_b[l]] + [jnp.zeros((CONV_WIDTH,), F32)] * 5)
        y_cv = (conv_module(z_cv, conv_w[l], cvec, 0, nb, seq),
                conv_module(z_cv, conv_w[l], cvec, n_ctx, db, dseq))

        nq = jnp.tile(q_norm[l], LANES // HEAD_DIM)[None]
        nk = jnp.tile(k_norm[l], LANES // HEAD_DIM)[None]
        y_c, k_ctx = attention(z_q, z_kv, nq, nk, 0, nb, seq)
        y_l = attention(z_q, z_kv, nq, nk, n_ctx, db, dseq, rope=rope,
                        cache=(cache_k[:, l].reshape(db * past, LANES), cache_v[:, l].reshape(db * past, LANES)))
        ctx_k.append(k_ctx.reshape(nb, seq, ATT_KV_HEADS, HEAD_DIM))
        ctx_v.append(z_kv[:n_ctx, LANES:].reshape(nb, seq, ATT_KV_HEADS, HEAD_DIM))

        x, h2, pq = merge_project(x, mod_blocks, bonus, gate, z_gate,
                                  (of_c, of_l), (ob_c, ob_l), y_cv, (y_c, y_l),
                                  jnp.stack([rw_gn_w[l], rw_gn_b[l]]),
                                  w_out_rwkv[l].astype(BF16), w_out_conv[l].astype(BF16),
                                  w_out_attn[l].astype(BF16), w_o[l].astype(BF16), peer_wq[l].astype(BF16))

        kz = jnp.zeros_like(peer_keys[l][:, 0])
        keys_padded = jnp.concatenate([jnp.concatenate([peer_keys[l][:, 0], kz], axis=-1),
                                       jnp.concatenate([kz, peer_keys[l][:, 1]], axis=-1)], axis=1)
        w_sel = peer_select(pq, keys_padded)
        x = peer_dense(h2, w_sel, peer_u[l].astype(BF16), peer_v[l].astype(BF16), x, mod_blocks)

    new_cache_k = jnp.stack(ctx_k, axis=1)
    new_cache_v = jnp.stack(ctx_v, axis=1)
    new_state = jnp.stack(ctx_s, axis=1)
    return (x[:n_ctx].reshape(nb, seq, d), x[n_ctx:].reshape(db, dseq, d),
            new_cache_k, new_cache_v, new_state)
```

```python
import functools
import math

import jax
import jax.numpy as jnp
from jax import lax
from jax.experimental import pallas as pl
from jax.experimental.pallas import tpu as pltpu

F32 = jnp.float32
BF16 = jnp.bfloat16

LANES = 128
SCAN_CHUNK = 64
SCAN_SUB = 16
RW_HEAD_DIM = 64
HEADS_PER_PAIR = LANES // RW_HEAD_DIM
VMEM_LIMIT = 56 * 1024 * 1024


def _dot(a, b):
    return jnp.dot(a.astype(BF16), b.astype(BF16), preferred_element_type=F32)


def _dot_nt(a, b):
    return lax.dot_general(a.astype(BF16), b.astype(BF16), (((1,), (1,)), ((), ())),
                           preferred_element_type=F32)


def _dot_tn(a, b):
    return lax.dot_general(a.astype(BF16), b.astype(BF16), (((0,), (0,)), ((), ())),
                           preferred_element_type=F32)


def _dot3(a, b):
    a_hi = a.astype(BF16)
    b_hi = b.astype(BF16)
    a_lo = (a - a_hi.astype(F32)).astype(BF16)
    b_lo = (b - b_hi.astype(F32)).astype(BF16)
    d = functools.partial(jnp.dot, preferred_element_type=F32)
    return d(a_hi, b_hi) + (d(a_hi, b_lo) + d(a_lo, b_hi))


def _scan_masks(reverse):
    C = SCAN_CHUNK
    P = HEADS_PER_PAIR * C
    row = lax.broadcasted_iota(jnp.int32, (P, LANES), 0)
    col = lax.broadcasted_iota(jnp.int32, (P, LANES), 1)
    tr = row % C
    tc = col % C
    t_i = lax.broadcasted_iota(jnp.int32, (C, C), 0)
    s_i = lax.broadcasted_iota(jnp.int32, (C, C), 1)
    return dict(
        same_head=(row // C) == (col // RW_HEAD_DIM),
        incl=(tr >= tc) if not reverse else (tr <= tc),
        strict=(tr > tc) if not reverse else (tr < tc),
        diag_blk=(tr // SCAN_SUB) == (tc // SCAN_SUB),
        eye=row == col,
        cum_mat=jnp.where((t_i >= s_i) if not reverse else (t_i <= s_i), 1.0, 0.0).astype(F32))


def _scan_chunk(chains, states):
    C = SCAN_CHUNK
    n = range(len(chains))
    mk = [ch[6] for ch in chains]
    cum = [ch[8] for ch in chains]
    pre = []
    for i in n:
        r, lw, k, v, kk, a, m, reverse, _ = chains[i]
        g = jnp.exp(cum[i])
        g_prev = jnp.exp(cum[i] - lw)
        g_inv = jnp.exp(-cum[i])
        g_end = jnp.exp(cum[i][C - 1:C, :] if not reverse else cum[i][0:1, :])
        bt = kk * a * g_inv
        kt = k * g_inv
        stack = lambda x, m=m: jnp.where(m["same_head"], jnp.concatenate([x, x], axis=0), 0.0)
        pre.append(dict(kap=stack(kk * g_prev), rt=stack(r * g), bt=stack(bt), kt=stack(kt),
                        v=stack(v), bh=stack(bt * g_end), kh=stack(kt * g_end), g_end=g_end))
    P = HEADS_PER_PAIR * C
    tri = [_dot_nt(jnp.concatenate([pre[i]["kap"], pre[i]["rt"]], axis=0),
                   jnp.concatenate([pre[i]["bt"], pre[i]["kt"]], axis=0)) for i in n]
    lb = [jnp.where(mk[i]["strict"], tri[i][:P, :P], 0.0) for i in n]
    lk = [jnp.where(mk[i]["strict"], tri[i][:P, P:], 0.0) for i in n]
    pb = [jnp.where(mk[i]["incl"], tri[i][P:, :P], 0.0) for i in n]
    pk = [jnp.where(mk[i]["incl"], tri[i][P:, P:], 0.0) for i in n]
    lkv = [_dot(lk[i], pre[i]["v"]) for i in n]

    d = [jnp.where(mk[i]["diag_blk"], lb[i], 0.0) for i in n]
    x = [jnp.concatenate([lb[i] - d[i], pre[i]["kap"], lkv[i]], axis=1) for i in n]
    x = [x[i] - _dot(d[i], x[i]) for i in n]
    p = [_dot(d[i], d[i]) for i in n]
    x = [x[i] + _dot(p[i], x[i]) for i in n]
    p = [_dot(p[i], p[i]) for i in n]
    x = [x[i] + _dot(p[i], x[i]) for i in n]
    p = [_dot(p[i], p[i]) for i in n]
    x = [x[i] + _dot(p[i], x[i]) for i in n]
    e = [x[i][:, :LANES] for i in n]
    rhs = [x[i][:, LANES:] for i in n]
    e2 = [_dot(e[i], e[i]) for i in n]
    t = [rhs[i] + _dot(e2[i], rhs[i]) for i in n]
    gy = [t[i] - _dot(e[i], t[i]) for i in n]

    m_mat = [jnp.where(mk[i]["eye"], pre[i]["g_end"], 0.0) - _dot_tn(pre[i]["bh"], gy[i][:, :LANES])
             for i in n]
    n_mat = [_dot_tn(pre[i]["kh"], pre[i]["v"]) - _dot_tn(pre[i]["bh"], gy[i][:, LANES:]) for i in n]
    pbgy = [_dot(pb[i], gy[i]) for i in n]
    q = [pre[i]["rt"] - pbgy[i][:, :LANES] for i in n]
    z = [_dot(pk[i], pre[i]["v"]) - pbgy[i][:, LANES:] for i in n]
    o_st = [_dot(q[i], states[i]) + z[i] for i in n]
    new = [_dot3(m_mat[i], states[i]) + n_mat[i] for i in n]
    return [(o_st[i][:C] + o_st[i][C:], new[i]) for i in n]


def _scan_kernel(rf, vf, kkf, lwf, af, kf, rb, vb, kkb, lwb, ab, kb, s0, of, ob, sfin, state):
    c = pl.program_id(1)
    n_pairs = rf.shape[-1] // LANES

    @pl.when(c == 0)
    def _():
        state[...] = s0[0]

    chains, states, outs = [], [], []
    for d, (r_, v_, kk_, lw_, a_, k_, o_) in enumerate(((rf, vf, kkf, lwf, af, kf, of),
                                                        (rb, vb, kkb, lwb, ab, kb, ob))):
        masks = _scan_masks(reverse=(d == 1))
        cum = _dot3(masks["cum_mat"], lw_[...])
        for p in range(n_pairs):
            sl = slice(p * LANES, (p + 1) * LANES)
            chains.append((r_[:, sl], lw_[:, sl], k_[:, sl], v_[:, sl], kk_[:, sl], a_[:, sl],
                           masks, d == 1, cum[:, sl]))
            states.append(state[d, p])
            outs.append((o_, sl, d, p))
    for (o, new), (o_, sl, d, p) in zip(_scan_chunk(chains, states), outs):
        o_[:, sl] = o
        state[d, p] = new

    @pl.when(c == pl.num_programs(1) - 1)
    def _():
        sfin[0] = state[...]


def rwkv_scan(r, v, kk, lw, a, kd, s0_bd, row0, n_seq, seq_len):
    width = r.shape[1]
    n_tok = n_seq * seq_len
    n_pairs = width // LANES
    n_chunk = seq_len // SCAN_CHUNK
    c0 = row0 // SCAN_CHUNK
    blk = (SCAN_CHUNK, width)
    dblk = (None, SCAN_CHUNK, width)
    fwd = lambda s, c: (s * n_chunk + c, 0)
    bwd = lambda s, c: (s * n_chunk + n_chunk - 1 - c, 0)
    fwd_in = lambda s, c: (c0 + s * n_chunk + c, 0)
    bwd_in = lambda s, c: (c0 + s * n_chunk + n_chunk - 1 - c, 0)
    fwd_d = lambda s, c: (0, c0 + s * n_chunk + c, 0)
    bwd_d = lambda s, c: (1, c0 + s * n_chunk + n_chunk - 1 - c, 0)
    sspec = pl.BlockSpec((1, 2, n_pairs, LANES, LANES), lambda s, c: (s, 0, 0, 0, 0))
    in_specs = ([pl.BlockSpec(blk, fwd_in)] * 3 + [pl.BlockSpec(dblk, fwd_d)] * 3
                + [pl.BlockSpec(blk, bwd_in)] * 3 + [pl.BlockSpec(dblk, bwd_d)] * 3 + [sspec])
    out_specs = [pl.BlockSpec(blk, fwd), pl.BlockSpec(blk, bwd), sspec]
    out_shape = [jax.ShapeDtypeStruct((n_tok, width), F32)] * 2 + [
        jax.ShapeDtypeStruct(s0_bd.shape, F32)]
    return pl.pallas_call(
        _scan_kernel,
        grid=(n_seq, n_chunk),
        in_specs=in_specs,
        out_specs=out_specs,
        out_shape=out_shape,
        scratch_shapes=[pltpu.VMEM((2, n_pairs, LANES, LANES), F32)],
        compiler_params=pltpu.CompilerParams(
            dimension_semantics=("parallel", "arbitrary"), vmem_limit_bytes=VMEM_LIMIT),
        name="rwkv_scan",
    )(r, v, kk, lw, a, kd, r, v, kk, lw, a, kd, s0_bd)


MOD_ROWS = 8
ROW_BLOCK = 256
RMS_EPS = 1e-6
LN_EPS = 1e-5
RW_GN_EPS = 64e-5


def _mod_kernel(c_ref, w_ref, b_ref, o_ref):
    c = c_ref[...]
    s = c * jax.nn.sigmoid(c)
    o_ref[...] = _dot(s, w_ref[...]) + b_ref[...]


def modulation(cond, w_mod, b_mod):
    n, d = cond.shape
    cols = w_mod.shape[1]
    blk = 1024
    return pl.pallas_call(
        _mod_kernel,
        grid=(cols // blk,),
        in_specs=[pl.BlockSpec((n, d), lambda j: (0, 0)),
                  pl.BlockSpec((d, blk), lambda j: (0, j)),
                  pl.BlockSpec((1, blk), lambda j: (0, j))],
        out_specs=pl.BlockSpec((n, blk), lambda j: (0, j)),
        out_shape=jax.ShapeDtypeStruct((n, cols), F32),
        name="modulation",
    )(cond, w_mod, b_mod)


def _modulate(x, shift, scale):
    ms = jnp.mean(x * x, axis=-1, keepdims=True)
    return x * lax.rsqrt(ms + RMS_EPS) * (1.0 + scale) + shift


def _inproj_kernel(x_ref, mod_ref, *refs):
    n = len(refs) // 2
    h = _modulate(x_ref[...], mod_ref[0, 0:1, :], mod_ref[0, 1:2, :]).astype(BF16)
    for w_ref, o_ref in zip(refs[:n], refs[n:]):
        o_ref[...] = jnp.dot(h, w_ref[...], preferred_element_type=F32)


def in_projection(x, mod_blocks, weights):
    n_tok, d = x.shape
    row = lambda i: (i, 0)
    const = lambda i: (0, 0)
    return pl.pallas_call(
        _inproj_kernel,
        grid=(n_tok // ROW_BLOCK,),
        in_specs=[pl.BlockSpec((ROW_BLOCK, d), row),
                  pl.BlockSpec((1, MOD_ROWS, d), lambda i: (i, 0, 0))]
                 + [pl.BlockSpec(w.shape, const, pipeline_mode=pl.Buffered(1)) for w in weights],
        out_specs=[pl.BlockSpec((ROW_BLOCK, w.shape[1]), row) for w in weights],
        out_shape=[jax.ShapeDtypeStruct((n_tok, w.shape[1]), F32) for w in weights],
        compiler_params=pltpu.CompilerParams(
            dimension_semantics=("parallel",), vmem_limit_bytes=VMEM_LIMIT),
        name="in_projection",
    )(x, mod_blocks, *weights)


HEAD_DIM = 64


def _group_ones(width):
    r = lax.broadcasted_iota(jnp.int32, (width, width), 0) // HEAD_DIM
    c = lax.broadcasted_iota(jnp.int32, (width, width), 1) // HEAD_DIM
    return jnp.where(r == c, 1.0, 0.0).astype(BF16)


def _group_sum(x, ones):
    hi = x.astype(BF16)
    lo = (x - hi.astype(F32)).astype(BF16)
    d = functools.partial(jnp.dot, preferred_element_type=F32)
    return d(hi, ones) + d(lo, ones)


RW_WIDTH = 512
RW_COLS = 3 * RW_WIDTH + 3 * LANES
RW_VEC_ROWS = 8


def _rwkv_prep_kernel(z_ref, hp_ref, hn_ref, mu_ref, vec_ref, w2_ref, a2_ref, g2_ref,
                      r_ref, v_ref, kk_ref, lw_ref, a_ref, kd_ref, bonus_ref, gate_ref):
    z = z_ref[...]
    rb = z.shape[0]
    row = lax.broadcasted_iota(jnp.int32, (rb, 1), 0)
    z_prev = jnp.where(row == 0, hp_ref[0], pltpu.roll(z, 1, 0))
    z_next = jnp.where(row == rb - 1, hn_ref[0], pltpu.roll(z, rb - 1, 0))
    zs = z + mu_ref[0:1, :] * (z_prev - z) + mu_ref[1:2, :] * (z_next - z)
    w = RW_WIDTH
    r, k, v = zs[:, :w], zs[:, w:2 * w], zs[:, 2 * w:3 * w]
    zw = zs[:, 3 * w:3 * w + LANES]
    za = zs[:, 3 * w + LANES:3 * w + 2 * LANES]
    zg = zs[:, 3 * w + 2 * LANES:]
    ones = _group_ones(w)
    kk = k * vec_ref[0:1, :]
    kk = kk * lax.rsqrt(_group_sum(kk * kk, ones) + 1e-12)
    r_ref[...] = r
    v_ref[...] = v
    kk_ref[...] = kk
    tz = jnp.tanh(zw)
    for d in range(2):
        w_raw = vec_ref[3 + d:4 + d, :] + _dot(tz, w2_ref[d])
        lw_ref[d] = -math.exp(-0.5) * jax.nn.sigmoid(w_raw)
        a = jax.nn.sigmoid(vec_ref[5 + d:6 + d, :] + _dot(za, a2_ref[d]))
        a_ref[d] = a
        kd_ref[d] = k * (1.0 + (a - 1.0) * vec_ref[1:2, :])
    bonus_ref[...] = _group_sum(r * k * vec_ref[2:3, :], ones) * v
    gate_ref[...] = _dot(jax.nn.sigmoid(zg), g2_ref[...])


def rwkv_prep(z_rw, halo_prev, halo_next, mu, vecs, w2_pad, a2_pad, g2):
    n_tok = z_rw.shape[0]
    w = RW_WIDTH
    row = lambda i: (i, 0)
    const2 = lambda i: (0, 0)
    const3 = lambda i: (0, 0, 0)
    halo = pl.BlockSpec((1, 1, RW_COLS), lambda i: (i, 0, 0))
    one = pl.BlockSpec((ROW_BLOCK, w), row)
    two = pl.BlockSpec((2, ROW_BLOCK, w), lambda i: (0, i, 0))
    s1 = jax.ShapeDtypeStruct((n_tok, w), F32)
    s2 = jax.ShapeDtypeStruct((2, n_tok, w), F32)
    return pl.pallas_call(
        _rwkv_prep_kernel,
        grid=(n_tok // ROW_BLOCK,),
        in_specs=[pl.BlockSpec((ROW_BLOCK, RW_COLS), row), halo, halo,
                  pl.BlockSpec(mu.shape, const2), pl.BlockSpec(vecs.shape, const2),
                  pl.BlockSpec(w2_pad.shape, const3), pl.BlockSpec(a2_pad.shape, const3),
                  pl.BlockSpec(g2.shape, const2)],
        out_specs=[one, one, one, two, two, two, one, one],
        out_shape=[s1, s1, s1, s2, s2, s2, s1, s1],
        compiler_params=pltpu.CompilerParams(
            dimension_semantics=("parallel",), vmem_limit_bytes=VMEM_LIMIT),
        name="rwkv_prep",
    )(z_rw, halo_prev, halo_next, mu, vecs, w2_pad, a2_pad, g2)


CONV_WIDTH = 512
CONV_KERNEL = 31
CONV_PAD = 16
CONV_ROWS = 64


def _conv_kernel(z_ref, w_ref, vec_ref, o_ref, pad_ref):
    t = z_ref.shape[0]
    cw = CONV_WIDTH
    half = CONV_KERNEL // 2
    zeros = jnp.zeros((CONV_PAD, cw), F32)
    pad_ref[0:CONV_PAD, :] = zeros
    pad_ref[CONV_PAD + t:CONV_PAD + t + CONV_PAD, :] = zeros
    pad_ref[CONV_PAD:CONV_PAD + t, :] = z_ref[:, :cw] * jax.nn.sigmoid(z_ref[:, cw:])

    def chunk(i, carry):
        base = pl.multiple_of(i * CONV_ROWS, CONV_ROWS)
        window = pad_ref[pl.ds(base, CONV_ROWS + 2 * CONV_PAD), :]
        acc = jnp.zeros((CONV_ROWS, cw), F32)
        for j in range(CONV_KERNEL):
            lo = CONV_PAD - half + j
            acc = acc + w_ref[j:j + 1, :] * window[lo:lo + CONV_ROWS]
        u = acc + vec_ref[0:1, :]
        mu = jnp.mean(u, axis=-1, keepdims=True)
        cen = u - mu
        var = jnp.mean(cen * cen, axis=-1, keepdims=True)
        y = cen * lax.rsqrt(var + LN_EPS) * vec_ref[1:2, :] + vec_ref[2:3, :]
        o_ref[pl.ds(base, CONV_ROWS), :] = y * jax.nn.sigmoid(y)
        return carry

    lax.fori_loop(0, t // CONV_ROWS, chunk, 0)


def conv_module(z_cv, conv_w, vecs, row0, n_seq, seq_len):
    cw = CONV_WIDTH
    blk0 = row0 // seq_len
    return pl.pallas_call(
        _conv_kernel,
        grid=(n_seq,),
        in_specs=[pl.BlockSpec((seq_len, 2 * cw), lambda i: (blk0 + i, 0)),
                  pl.BlockSpec(conv_w.shape, lambda i: (0, 0)),
                  pl.BlockSpec(vecs.shape, lambda i: (0, 0))],
        out_specs=pl.BlockSpec((seq_len, cw), lambda i: (i, 0)),
        out_shape=jax.ShapeDtypeStruct((n_seq * seq_len, cw), F32),
        scratch_shapes=[pltpu.VMEM((seq_len + 2 * CONV_PAD, cw), F32)],
        compiler_params=pltpu.CompilerParams(
            dimension_semantics=("parallel",), vmem_limit_bytes=VMEM_LIMIT),
        name="conv_module",
    )(z_cv, conv_w, vecs)


ATT_Q_BLOCK = 256
ATT_HEADS = 8
ATT_KV_HEADS = 2
ATT_GROUP = ATT_HEADS // ATT_KV_HEADS


def _rot_partner():
    r = lax.broadcasted_iota(jnp.int32, (LANES, LANES), 0)
    c = lax.broadcasted_iota(jnp.int32, (LANES, LANES), 1)
    lo = (r % 32) < 16
    return jnp.where((c == r + 16) & lo, 1.0, jnp.where((c == r - 16) & ~lo, -1.0, 0.0)).astype(BF16)


def _attn_kernel(*refs, rotary):
    if rotary:
        (zq_ref, zkv_ref, nq_ref, nk_ref, cosq_ref, sinq_ref, cosk_ref, sink_ref, ck_ref, cv_ref,
         o_ref) = refs
    else:
        zq_ref, zkv_ref, nq_ref, nk_ref, o_ref, kn_ref = refs
    tq = zq_ref.shape[0]
    ones = _group_ones(LANES)
    lane = lax.broadcasted_iota(jnp.int32, (1, LANES), 1)
    first = lane < HEAD_DIM
    row_head = lax.broadcasted_iota(jnp.int32, (2 * tq, LANES), 0) // tq
    own = row_head == (lax.broadcasted_iota(jnp.int32, (2 * tq, LANES), 1) // HEAD_DIM)

    def norm(x, g):
        return x * lax.rsqrt(_group_sum(x * x, ones) * (1.0 / HEAD_DIM) + RMS_EPS) * g

    def rope(x, cos, sin):
        return x * cos + jnp.dot(x.astype(BF16), _rot_partner(), preferred_element_type=F32) * sin

    def dup(x):
        sw = pltpu.roll(x, HEAD_DIM, 1)
        return jnp.where(first, x, sw), jnp.where(first, sw, x)

    k = norm(zkv_ref[:, :LANES], nk_ref[...])
    v = zkv_ref[:, LANES:]
    if rotary:
        k = rope(k, cosk_ref[...], sink_ref[...])
        ck, cv = dup(ck_ref[...]), dup(cv_ref[...])
    else:
        kn_ref[...] = k
    kd, vd = dup(k), dup(v)
    for pair in range(ATT_HEADS // 2):
        g = pair // (ATT_GROUP // 2)
        q = norm(zq_ref[:, pair * LANES:(pair + 1) * LANES], nq_ref[...])
        if rotary:
            q = rope(q, cosq_ref[...], sinq_ref[...])
        q = q * (HEAD_DIM ** -0.5)
        qs = jnp.where(own, jnp.concatenate([q, q], axis=0), 0.0)
        s = _dot_nt(qs, kd[g])
        m = jnp.max(s, axis=-1, keepdims=True)
        if rotary:
            s2 = _dot_nt(qs, ck[g])
            m = jnp.maximum(m, jnp.max(s2, axis=-1, keepdims=True))
            p2 = jnp.exp(s2 - m)
        p = jnp.exp(s - m)
        den = jnp.sum(p, axis=-1, keepdims=True)
        o = _dot(p, vd[g])
        if rotary:
            den = den + jnp.sum(p2, axis=-1, keepdims=True)
            o = o + _dot(p2, cv[g])
        o = o / den
        o_ref[:, pair * LANES:(pair + 1) * LANES] = jnp.where(first, o[:tq], o[tq:])


def attention(z_q, z_kv, nq, nk, row0, n_seq, seq_len, rope=None, cache=None):
    qb = min(ATT_Q_BLOCK, seq_len)
    nqb = seq_len // qb
    qblk0 = row0 // qb
    sblk0 = row0 // seq_len
    wq = z_q.shape[1]
    in_specs = [pl.BlockSpec((qb, wq), lambda b, t: (qblk0 + b * nqb + t, 0)),
                pl.BlockSpec((seq_len, 2 * LANES), lambda b, t: (sblk0 + b, 0)),
                pl.BlockSpec((1, LANES), lambda b, t: (0, 0)),
                pl.BlockSpec((1, LANES), lambda b, t: (0, 0))]
    args = [z_q, z_kv, nq, nk]
    out_q = pl.BlockSpec((qb, wq), lambda b, t: (b * nqb + t, 0))
    y_shape = jax.ShapeDtypeStruct((n_seq * seq_len, wq), F32)
    if rope is None:
        out_specs = [out_q, pl.BlockSpec((seq_len, LANES), lambda b, t: (b, 0))]
        out_shape = [y_shape, jax.ShapeDtypeStruct((n_seq * seq_len, LANES), F32)]
    else:
        past = cache[0].shape[0] // n_seq
        in_specs += [pl.BlockSpec((qb, LANES), lambda b, t: (t, 0))] * 2
        in_specs += [pl.BlockSpec((seq_len, LANES), lambda b, t: (0, 0))] * 2
        in_specs += [pl.BlockSpec((past, LANES), lambda b, t: (b, 0))] * 2
        args += [rope[0], rope[1], rope[0], rope[1], cache[0], cache[1]]
        out_specs = out_q
        out_shape = y_shape
    return pl.pallas_call(
        functools.partial(_attn_kernel, rotary=rope is not None),
        grid=(n_seq, nqb),
        in_specs=in_specs,
        out_specs=out_specs,
        out_shape=out_shape,
        compiler_params=pltpu.CompilerParams(
            dimension_semantics=("parallel", "arbitrary"), vmem_limit_bytes=VMEM_LIMIT),
        name="attention",
    )(*args)


def _merge_kernel(x_ref, mod_ref, bonus_ref, gate_ref, zg_ref,
                  ofc_ref, ofl_ref, obc_ref, obl_ref, cvc_ref, cvl_ref, atc_ref, atl_ref,
                  gn_ref, wr_ref, wc_ref, wa_ref, wo_ref, wq_ref, xo_ref, h2_ref, q_ref, *, n_ctx_blocks):
    d = x_ref.shape[1]
    zg = zg_ref[...]
    is_ctx = pl.program_id(0) < n_ctx_blocks
    pick = lambda c_ref, l_ref: jnp.where(is_ctx, c_ref[...], l_ref[...])
    ones = _group_ones(ofc_ref.shape[1])
    o = pick(ofc_ref, ofl_ref) + pick(obc_ref, obl_ref)
    cen = o - _group_sum(o, ones) * (1.0 / HEAD_DIM)
    var = _group_sum(cen * cen, ones) * (1.0 / HEAD_DIM)
    o = cen * lax.rsqrt(var + RW_GN_EPS) * gn_ref[0:1, :] + gn_ref[1:2, :]
    y_rw = (o + bonus_ref[...]) * gate_ref[...]
    merged = (jax.nn.sigmoid(zg[:, :d]) * _dot(y_rw, wr_ref[...])
              + jax.nn.sigmoid(zg[:, d:2 * d]) * _dot(pick(cvc_ref, cvl_ref), wc_ref[...])
              + jax.nn.sigmoid(zg[:, 2 * d:]) * _dot(pick(atc_ref, atl_ref), wa_ref[...]))
    x = x_ref[...] + mod_ref[0, 2:3, :] * _dot(merged, wo_ref[...])
    xo_ref[...] = x
    h2 = _modulate(x, mod_ref[0, 3:4, :], mod_ref[0, 4:5, :]).astype(BF16)
    h2_ref[...] = h2
    q_ref[...] = jnp.dot(h2, wq_ref[...], preferred_element_type=F32)


def merge_project(x, mod_blocks, bonus, gate, z_gate, o_f, o_b, y_cv, y_at, gn, w_r, w_c, w_a, w_o, w_q):
    n_tok, d = x.shape
    nc = o_f[0].shape[0] // ROW_BLOCK
    row = lambda i: (i, 0)
    const = lambda i: (0, 0)
    ctx_row = lambda i: (jnp.minimum(i, nc - 1), 0)
    lat_row = lambda i: (jnp.maximum(i - nc, 0), 0)
    acts = [bonus, gate, z_gate]
    weights = [gn, w_r, w_c, w_a, w_o, w_q]
    pair_specs, pair_args = [], []
    for c_arr, l_arr in (o_f, o_b, y_cv, y_at):
        pair_specs += [pl.BlockSpec((ROW_BLOCK, c_arr.shape[1]), ctx_row),
                       pl.BlockSpec((ROW_BLOCK, l_arr.shape[1]), lat_row)]
        pair_args += [c_arr, l_arr]
    return pl.pallas_call(
        functools.partial(_merge_kernel, n_ctx_blocks=nc),
        grid=(n_tok // ROW_BLOCK,),
        in_specs=[pl.BlockSpec((ROW_BLOCK, d), row),
                  pl.BlockSpec((1, MOD_ROWS, d), lambda i: (i, 0, 0))]
                 + [pl.BlockSpec((ROW_BLOCK, a.shape[1]), row) for a in acts]
                 + pair_specs
                 + [pl.BlockSpec(w.shape, const) for w in weights],
        out_specs=[pl.BlockSpec((ROW_BLOCK, d), row), pl.BlockSpec((ROW_BLOCK, d), row),
                   pl.BlockSpec((ROW_BLOCK, w_q.shape[1]), row)],
        out_shape=[jax.ShapeDtypeStruct((n_tok, d), F32), jax.ShapeDtypeStruct((n_tok, d), BF16),
                   jax.ShapeDtypeStruct((n_tok, w_q.shape[1]), F32)],
        compiler_params=pltpu.CompilerParams(
            dimension_semantics=("parallel",), vmem_limit_bytes=VMEM_LIMIT),
        name="merge_project",
    )(x, mod_blocks, *acts, *pair_args, *weights)


PEER_HEADS = 8
PEER_N_KEYS = 128
PEER_HALF = 64
PEER_TOPK = 16
PEER_SEL_BLOCK = 128
PEER_LOOP_UNROLL = 16
PEER_RELAYOUT_TOKENS = 32
PEER_RELAYOUT_KEYS = 8
NEG = -1e30


def _dot3_nt(a, b):
    a_hi = a.astype(BF16)
    b_hi = b.astype(BF16)
    a_lo = (a - a_hi.astype(F32)).astype(BF16)
    b_lo = (b - b_hi.astype(F32)).astype(BF16)
    d = lambda x, y: lax.dot_general(x, y, (((1,), (1,)), ((), ())), preferred_element_type=F32)
    return d(a_hi, b_hi) + (d(a_hi, b_lo) + d(a_lo, b_hi))


def _extract_max(s, code):
    m = jnp.max(s, axis=0, keepdims=True)
    pos = jnp.min(jnp.where(s == m, code, 1e9), axis=0, keepdims=True)
    return m, pos, jnp.where(code == pos, NEG, s)


def _pair_candidates(a0, a1):
    k = PEER_TOPK
    sub = lax.broadcasted_iota(jnp.int32, (k, a0.shape[1]), 0).astype(F32)
    sub8 = sub[:8]
    vals = [a0[0:1] + a1]
    codes = [sub]
    for r0 in range(1, 8):
        limit = k // (r0 + 1)
        vals.append(jnp.where(sub8 < limit, a0[r0:r0 + 1] + a1[:8], NEG))
        codes.append(sub8 + float(r0 * k))
    vals.append(a0[8:] + a1[0:1])
    codes.append((sub8 + 8.0) * float(k))
    return jnp.concatenate(vals, axis=0), jnp.concatenate(codes, axis=0)


def _peer_select_kernel(q_ref, keys_ref, w_ref, top_v, top_i, slot_i, slot_j, slot_g,
                        rows_i, rows_j, rows_g, w3_ref):
    k = PEER_TOPK
    ct = q_ref.shape[0]
    key_code = lax.broadcasted_iota(jnp.int32, (PEER_N_KEYS, ct), 0).astype(F32)
    rank = lax.broadcasted_iota(jnp.int32, (k, ct), 0).astype(F32)
    for h in range(PEER_HEADS):
        st = _dot3_nt(keys_ref[h], q_ref[:, h * LANES:(h + 1) * LANES])
        for p in range(2):
            s = st[p * PEER_N_KEYS:(p + 1) * PEER_N_KEYS]
            for it in range(k):
                m, pos, s = _extract_max(s, key_code)
                top_v[p, it:it + 1, :] = m
                top_i[p, it:it + 1, :] = pos
        a0, a1 = top_v[0], top_v[1]
        i0, i1 = top_i[0], top_i[1]
        cand, code = _pair_candidates(a0, a1)
        best = a0[0:1] + a1[0:1]
        z = jnp.zeros_like(best)
        for it in range(k):
            m, pos, cand = _extract_max(cand, code)
            r0 = jnp.floor(pos * (1.0 / k))
            r1 = pos - r0 * k
            e = jnp.exp(m - best)
            z = z + e
            slot = h * k + it
            slot_i[slot:slot + 1, :] = jnp.sum(jnp.where(rank == r0, i0, 0.0), axis=0, keepdims=True)
            slot_j[slot:slot + 1, :] = jnp.sum(jnp.where(rank == r1, i1, 0.0), axis=0, keepdims=True)
            slot_g[slot:slot + 1, :] = e
        slot_g[h * k:(h + 1) * k, :] = slot_g[h * k:(h + 1) * k, :] / z

    rows_i[...] = slot_i[...].T
    rows_j[...] = slot_j[...].T
    rows_g[...] = slot_g[...].T
    n = PEER_N_KEYS
    sub = lax.broadcasted_iota(jnp.int32, (n, n), 0).astype(F32)

    def per_token(c, carry):
        a_t = jnp.where(sub == rows_i[pl.ds(c, 1), :], rows_g[pl.ds(c, 1), :], 0.0)
        b_t = jnp.where(sub == rows_j[pl.ds(c, 1), :], 1.0, 0.0)
        w3_ref[c] = _dot_nt(a_t, b_t)
        return carry

    lax.fori_loop(0, ct, per_token, 0, unroll=PEER_LOOP_UNROLL)

    tg, ig = PEER_RELAYOUT_TOKENS, PEER_RELAYOUT_KEYS
    r_out = lax.broadcasted_iota(jnp.int32, (tg * ig, tg * ig), 0)
    r_in = lax.broadcasted_iota(jnp.int32, (tg * ig, tg * ig), 1)
    perm = jnp.where(r_in == (r_out % tg) * ig + r_out // tg, 1.0, 0.0).astype(BF16)

    for ib in range(n // ig):
        for t in range(ct // tg):
            z = w3_ref[t * tg:(t + 1) * tg, ib * ig:(ib + 1) * ig, :]
            y = jnp.dot(perm, z.reshape(tg * ig, n).astype(BF16), preferred_element_type=F32)
            for kk in range(ig):
                col = (ib * ig + kk) * n
                w_ref[t * tg:(t + 1) * tg, col:col + n] = y[kk * tg:(kk + 1) * tg].astype(BF16)


def peer_select(q, keys_padded):
    n_tok, qd = q.shape
    ct = PEER_SEL_BLOCK
    n = PEER_N_KEYS
    slots = PEER_HEADS * PEER_TOPK
    return pl.pallas_call(
        _peer_select_kernel,
        grid=(n_tok // ct,),
        in_specs=[pl.BlockSpec((ct, qd), lambda i: (i, 0)),
                  pl.BlockSpec(keys_padded.shape, lambda i: (0, 0, 0))],
        out_specs=pl.BlockSpec((ct, n * n), lambda i: (i, 0)),
        out_shape=jax.ShapeDtypeStruct((n_tok, n * n), BF16),
        scratch_shapes=[pltpu.VMEM((2, PEER_TOPK, ct), F32), pltpu.VMEM((2, PEER_TOPK, ct), F32),
                        pltpu.VMEM((slots, ct), F32), pltpu.VMEM((slots, ct), F32),
                        pltpu.VMEM((slots, ct), F32),
                        pltpu.VMEM((ct, slots), F32), pltpu.VMEM((ct, slots), F32),
                        pltpu.VMEM((ct, slots), F32),
                        pltpu.VMEM((ct, n, n), F32)],
        compiler_params=pltpu.CompilerParams(
            dimension_semantics=("parallel",), vmem_limit_bytes=VMEM_LIMIT),
        name="peer_select",
    )(q, keys_padded)


PEER_TOKEN_BLOCK = 1024
PEER_EXPERT_BLOCK = 1024


def _peer_kernel(h_ref, w_ref, u_ref, v_ref, x_ref, mod_ref, o_ref, acc_ref):
    e = pl.program_id(1)

    @pl.when(e == 0)
    def _():
        acc_ref[...] = jnp.zeros_like(acc_ref)

    s = _dot_nt(h_ref[...], u_ref[...])
    act = 0.5 * s * (1.0 + lax.erf(s * (2.0 ** -0.5)))
    acc_ref[...] += _dot(act * w_ref[...].astype(F32), v_ref[...])

    @pl.when(e == pl.num_programs(1) - 1)
    def _():
        for j in range(mod_ref.shape[0]):
            rows = slice(j * ROW_BLOCK, (j + 1) * ROW_BLOCK)
            o_ref[rows, :] = x_ref[rows, :] + mod_ref[j, 5:6, :] * acc_ref[rows, :]


def peer_dense(h2, w_sel, u_tab, v_tab, x, mod_blocks):
    n_tok, d = x.shape
    n_exp = u_tab.shape[0]
    tb, eb = PEER_TOKEN_BLOCK, PEER_EXPERT_BLOCK
    tok = lambda i, e: (i, 0)
    return pl.pallas_call(
        _peer_kernel,
        grid=(n_tok // tb, n_exp // eb),
        in_specs=[pl.BlockSpec((tb, d), tok),
                  pl.BlockSpec((tb, eb), lambda i, e: (i, e)),
                  pl.BlockSpec((eb, d), lambda i, e: (e, 0)),
                  pl.BlockSpec((eb, d), lambda i, e: (e, 0)),
                  pl.BlockSpec((tb, d), tok),
                  pl.BlockSpec((tb // ROW_BLOCK, MOD_ROWS, d), lambda i, e: (i, 0, 0))],
        out_specs=pl.BlockSpec((tb, d), tok),
        out_shape=jax.ShapeDtypeStruct((n_tok, d), F32),
        scratch_shapes=[pltpu.VMEM((tb, d), F32)],
        compiler_params=pltpu.CompilerParams(
            dimension_semantics=("parallel", "arbitrary"), vmem_limit_bytes=VMEM_LIMIT),
        name="peer_dense",
    )(h2, w_sel, u_tab, v_tab, x, mod_blocks)


FUSED_TOKEN_BLOCK = 256
FUSED_EXPERT_BLOCK = 2048
FUSED_PHASES = 4
FUSED_RELAYOUT = 16
STAGE1_SPLIT = 5


def _fused_stage1(q_ref, keys_ref, row0, heads, top_v, top_i):
    ct = PEER_SEL_BLOCK
    key_code = lax.broadcasted_iota(jnp.int32, (PEER_N_KEYS, ct), 0).astype(F32)
    for h in heads:
        st = _dot3_nt(keys_ref[h], q_ref[pl.ds(row0, ct), h * LANES:(h + 1) * LANES])
        for p in range(2):
            s = st[p * PEER_N_KEYS:(p + 1) * PEER_N_KEYS]
            for it in range(PEER_TOPK):
                m, pos, s = _extract_max(s, key_code)
                top_v[h, p, it:it + 1, :] = m
                top_i[h, p, it:it + 1, :] = pos


def _fused_stage2(top_v, top_i, slot_i, slot_j, slot_g, rows_i, rows_j, rows_g):
    k = PEER_TOPK
    ct = PEER_SEL_BLOCK
    rank = lax.broadcasted_iota(jnp.int32, (k, ct), 0).astype(F32)
    for h in range(PEER_HEADS):
        a0, a1 = top_v[h, 0], top_v[h, 1]
        i0, i1 = top_i[h, 0], top_i[h, 1]
        cand, code = _pair_candidates(a0, a1)
        best = a0[0:1] + a1[0:1]
        z = jnp.zeros_like(best)
        for it in range(k):
            m, pos, cand = _extract_max(cand, code)
            r0 = jnp.floor(pos * (1.0 / k))
            r1 = pos - r0 * k
            e = jnp.exp(m - best)
            z = z + e
            slot = h * k + it
            slot_i[slot:slot + 1, :] = jnp.sum(jnp.where(rank == r0, i0, 0.0), axis=0, keepdims=True)
            slot_j[slot:slot + 1, :] = jnp.sum(jnp.where(rank == r1, i1, 0.0), axis=0, keepdims=True)
            slot_g[slot:slot + 1, :] = e
        slot_g[h * k:(h + 1) * k, :] = slot_g[h * k:(h + 1) * k, :] / z
    rows_i[...] = slot_i[...].T
    rows_j[...] = slot_j[...].T
    rows_g[...] = slot_g[...].T


def _fused_assemble(rows_i, rows_j, rows_g, w3_ref):
    n = PEER_N_KEYS
    sub = lax.broadcasted_iota(jnp.int32, (n, n), 0).astype(F32)

    def per_token(c, carry):
        a_t = jnp.where(sub == rows_i[pl.ds(c, 1), :], rows_g[pl.ds(c, 1), :], 0.0)
        b_t = jnp.where(sub == rows_j[pl.ds(c, 1), :], 1.0, 0.0)
        w3_ref[c] = _dot_nt(a_t, b_t).astype(BF16)
        return carry

    lax.fori_loop(0, PEER_SEL_BLOCK, per_token, 0, unroll=PEER_LOOP_UNROLL)


def _fused_relayout(w3_ref, wbuf, buf, row0):
    n = PEER_N_KEYS
    g = FUSED_RELAYOUT
    r_out = lax.broadcasted_iota(jnp.int32, (g * g, g * g), 0)
    r_in = lax.broadcasted_iota(jnp.int32, (g * g, g * g), 1)
    perm = jnp.where(r_in == (r_out % g) * g + r_out // g, 1.0, 0.0).astype(BF16)
    for ib in range(n // g):
        for tt in range(PEER_SEL_BLOCK // g):
            z = w3_ref[tt * g:(tt + 1) * g, ib * g:(ib + 1) * g, :].reshape(g * g, n)
            y = jnp.dot(perm, z, preferred_element_type=F32)
            rows = pl.ds(pl.multiple_of(row0 + tt * g, g), g)
            for kk in range(g):
                col = (ib * g + kk) * n
                wbuf[buf, rows, col:col + n] = y[kk * g:(kk + 1) * g].astype(BF16)


def _peer_fused_kernel(q_ref, keys_ref, h_ref, u_ref, v_ref, x_ref, mod_ref, o_ref,
                       acc_ref, wbuf, top_v, top_i, slot_i, slot_j, slot_g, rows_i, rows_j, rows_g,
                       w3_ref):
    t = pl.program_id(0)
    e = pl.program_id(1)
    fill = t % 2
    use = 1 - fill
    eb = u_ref.shape[0]

    @pl.when((t == 0) & (e == 0))
    def _():
        wbuf[1] = jnp.zeros(wbuf.shape[1:], BF16)

    @pl.when(e == 0)
    def _():
        acc_ref[...] = jnp.zeros_like(acc_ref)

    def dense():
        s = _dot_nt(h_ref[...], u_ref[...])
        act = 0.5 * s * (1.0 + lax.erf(s * (2.0 ** -0.5)))
        w = wbuf[use, :, pl.ds(pl.multiple_of(e * eb, eb), eb)]
        acc_ref[...] += jnp.dot(act.astype(BF16) * w, v_ref[...], preferred_element_type=F32)

    row0 = pl.multiple_of((e // FUSED_PHASES) * PEER_SEL_BLOCK, PEER_SEL_BLOCK)
    phase = e % FUSED_PHASES

    @pl.when(phase == 0)
    def _():
        dense()
        _fused_stage1(q_ref, keys_ref, row0, range(STAGE1_SPLIT), top_v, top_i)

    @pl.when(phase == 1)
    def _():
        dense()
        _fused_stage1(q_ref, keys_ref, row0, range(STAGE1_SPLIT, PEER_HEADS), top_v, top_i)
        _fused_stage2(top_v, top_i, slot_i, slot_j, slot_g, rows_i, rows_j, rows_g)

    @pl.when(phase == 2)
    def _():
        dense()
        _fused_assemble(rows_i, rows_j, rows_g, w3_ref)

    @pl.when(phase == 3)
    def _():
        dense()
        _fused_relayout(w3_ref, wbuf, fill, row0)

    @pl.when(e == pl.num_programs(1) - 1)
    def _():
        o_ref[...] = x_ref[...] + mod_ref[0, 5:6, :] * acc_ref[...]


def peer_fused(q, keys_padded, h2, u_tab, v_tab, x, mod_blocks):
    n_tok, d = x.shape
    n_exp = u_tab.shape[0]
    tb, eb = FUSED_TOKEN_BLOCK, FUSED_EXPERT_BLOCK
    ct = PEER_SEL_BLOCK
    n = PEER_N_KEYS
    slots = PEER_HEADS * PEER_TOPK
    n_blk = n_tok // tb
    assert tb == ROW_BLOCK and (tb // ct) * FUSED_PHASES == n_exp // eb
    cur = lambda t, e: (jnp.minimum(t, n_blk - 1), 0)
    prev = lambda t, e: (jnp.maximum(t - 1, 0), 0)
    return pl.pallas_call(
        _peer_fused_kernel,
        grid=(n_blk + 1, n_exp // eb),
        in_specs=[pl.BlockSpec((tb, q.shape[1]), cur),
                  pl.BlockSpec(keys_padded.shape, lambda t, e: (0, 0, 0)),
                  pl.BlockSpec((tb, d), prev),
                  pl.BlockSpec((eb, d), lambda t, e: (e, 0)),
                  pl.BlockSpec((eb, d), lambda t, e: (e, 0)),
                  pl.BlockSpec((tb, d), prev),
                  pl.BlockSpec((1, MOD_ROWS, d), lambda t, e: (jnp.maximum(t - 1, 0), 0, 0))],
        out_specs=pl.BlockSpec((tb, d), prev),
        out_shape=jax.ShapeDtypeStruct((n_tok, d), F32),
        scratch_shapes=[pltpu.VMEM((tb, d), F32),
                        pltpu.VMEM((2, tb, n * n), BF16),
                        pltpu.VMEM((PEER_HEADS, 2, PEER_TOPK, ct), F32),
                        pltpu.VMEM((PEER_HEADS, 2, PEER_TOPK, ct), F32),
                        pltpu.VMEM((slots, ct), F32), pltpu.VMEM((slots, ct), F32),
                        pltpu.VMEM((slots, ct), F32),
                        pltpu.VMEM((ct, slots), F32), pltpu.VMEM((ct, slots), F32),
                        pltpu.VMEM((ct, slots), F32),
                        pltpu.VMEM((ct, n, n), BF16)],
        compiler_params=pltpu.CompilerParams(
            dimension_semantics=("arbitrary", "arbitrary"), vmem_limit_bytes=VMEM_LIMIT),
        name="peer_fused",
    )(q, keys_padded, h2, u_tab, v_tab, x, mod_blocks)


RW_HEADS = RW_WIDTH // RW_HEAD_DIM
ROT_HALF = HEAD_DIM // 2
ROPE_THETA = 10000.0
GRID_W = 64


def _split(z, sizes):
    parts, start = [], 0
    for s in sizes:
        parts.append(z[..., start:start + s])
        start += s
    return parts


def _bd_from_states(s):
    a = jnp.swapaxes(s, -1, -2)
    n, d, h, m, _ = a.shape
    a = a.reshape(n, d, h // 2, 2, m, m)
    z = jnp.zeros_like(a[:, :, :, 0])
    top = jnp.concatenate([a[:, :, :, 0], z], axis=-1)
    bot = jnp.concatenate([z, a[:, :, :, 1]], axis=-1)
    return jnp.concatenate([top, bot], axis=-2)


def _states_from_bd(bd):
    n, d, p, _, _ = bd.shape
    m = RW_HEAD_DIM
    a = jnp.stack([bd[:, :, :, :m, :m], bd[:, :, :, m:, m:]], axis=3).reshape(n, d, 2 * p, m, m)
    return jnp.swapaxes(a, -1, -2)


def _rope_tables(t):
    rows = t // GRID_W
    row, col = jnp.meshgrid(jnp.arange(rows), jnp.arange(GRID_W), indexing='ij')
    inv = ROPE_THETA ** (-jnp.arange(0, ROT_HALF, 2, dtype=F32) / ROT_HALF)
    ang_r = row.reshape(-1, 1).astype(F32) * inv
    ang_c = col.reshape(-1, 1).astype(F32) * inv
    cr, sr, cc, sc = jnp.cos(ang_r), jnp.sin(ang_r), jnp.cos(ang_c), jnp.sin(ang_c)
    cos = jnp.concatenate([cr, cr, cc, cc], axis=-1)
    sin = jnp.concatenate([sr, sr, sc, sc], axis=-1)
    return jnp.tile(cos, (1, LANES // HEAD_DIM)), jnp.tile(sin, (1, LANES // HEAD_DIM))


def _pad_rows(w, top):
    z = jnp.zeros_like(w)
    return jnp.concatenate([w, z] if top else [z, w], axis=0)


def kernel(x_prompt, x_sample, c, cache_k, cache_v, state_wkv, c_ctx, w_mod, b_mod, w_in, shift_mu_prev, shift_mu_next, rw_w0, rw_w2, rw_a0, rw_a2, rw_g2, rw_k_k, rw_k_a, rw_r_k, rw_gn_w, rw_gn_b, w_out_rwkv, conv_w, conv_b, conv_ln_w, conv_ln_b, w_out_conv, q_norm, k_norm, w_out_attn, w_o, peer_wq, peer_keys, peer_u, peer_v):
    nb, seq, d = x_prompt.shape
    db, dseq, _ = x_sample.shape
    depth = w_in.shape[0]
    n_ctx = nb * seq
    n_lat = db * dseq
    n_tok = n_ctx + n_lat
    x = jnp.concatenate([x_prompt.reshape(n_ctx, d), x_sample.reshape(n_lat, d)], axis=0)

    cond = jnp.concatenate([c_ctx[None], c, jnp.zeros((MOD_ROWS - 1 - db, d), F32)], axis=0)
    blk_cond = jnp.concatenate([jnp.zeros((n_ctx // ROW_BLOCK,), jnp.int32),
                                1 + jnp.arange(n_lat // ROW_BLOCK, dtype=jnp.int32) // (dseq // ROW_BLOCK)])
    n_blk = n_tok // ROW_BLOCK
    blk_row = jnp.arange(n_blk) * ROW_BLOCK
    blk_pos = jnp.where(blk_row < n_ctx, blk_row % seq, (blk_row - n_ctx) % dseq)
    blk_len = jnp.where(blk_row < n_ctx, seq, dseq)
    starts_seq = (blk_pos == 0)[:, None]
    ends_seq = (blk_pos + ROW_BLOCK == blk_len)[:, None]
    rope = _rope_tables(dseq)
    in_sizes = (RW_COLS, 2 * CONV_WIDTH, ATT_HEADS * HEAD_DIM, 2 * ATT_KV_HEADS * HEAD_DIM, 3 * d)
    past = cache_k.shape[2]

    ctx_k, ctx_v, ctx_s = [], [], []
    for l in range(depth):
        mod = modulation(cond, w_mod[l].astype(BF16), b_mod[l][None])
        mod = mod.reshape(MOD_ROWS, 6, d)[blk_cond]
        mod_blocks = jnp.concatenate([mod, jnp.zeros((mod.shape[0], MOD_ROWS - 6, d), F32)], axis=1)

        z_rw, z_cv, z_q, z_kv, z_gate = in_projection(x, mod_blocks, _split(w_in[l].astype(BF16), in_sizes))

        zb = z_rw.reshape(n_blk, ROW_BLOCK, RW_COLS)
        halo_prev = jnp.where(starts_seq, 0.0, jnp.roll(zb[:, -1], 1, axis=0))[:, None]
        halo_next = jnp.where(ends_seq, 0.0, jnp.roll(zb[:, 0], -1, axis=0))[:, None]
        mu = jnp.stack([shift_mu_prev[l], shift_mu_next[l]])
        vecs = jnp.stack([rw_k_k[l], rw_k_a[l], rw_r_k[l], rw_w0[l, 0], rw_w0[l, 1], rw_a0[l, 0],
                          rw_a0[l, 1], jnp.zeros((RW_WIDTH,), F32)])
        w2_pad = jnp.stack([_pad_rows(rw_w2[l, 0], True), _pad_rows(rw_w2[l, 1], False)]).astype(BF16)
        a2_pad = jnp.stack([_pad_rows(rw_a2[l, 0], True), _pad_rows(rw_a2[l, 1], False)]).astype(BF16)
        r, v, kk, lw, aa, kd, bonus, gate = rwkv_prep(z_rw, halo_prev, halo_next, mu, vecs, w2_pad, a2_pad,
                                                      rw_g2[l].astype(BF16))
        s0_ctx = jnp.zeros((nb, 2, RW_HEADS // 2, LANES, LANES), F32)
        of_c, ob_c, sfin = rwkv_scan(r, v, kk, lw, aa, kd, s0_ctx, 0, nb, seq)
        of_l, ob_l, _ = rwkv_scan(r, v, kk, lw, aa, kd, _bd_from_states(state_wkv[:, l]), n_ctx, db, dseq)
        ctx_s.append(_states_from_bd(sfin))

        cvec = jnp.stack([conv_b[l], conv_ln_w[l], conv_ln_b[l]] + [jnp.zeros((CONV_WIDTH,), F32)] * 5)
        y_cv = (conv_module(z_cv, conv_w[l], cvec, 0, nb, seq),
                conv_module(z_cv, conv_w[l], cvec, n_ctx, db, dseq))

        nq = jnp.tile(q_norm[l], LANES // HEAD_DIM)[None]
        nk = jnp.tile(k_norm[l], LANES // HEAD_DIM)[None]
        y_c, k_ctx = attention(z_q, z_kv, nq, nk, 0, nb, seq)
        y_l = attention(z_q, z_kv, nq, nk, n_ctx, db, dseq, rope=rope,
                        cache=(cache_k[:, l].reshape(db * past, LANES), cache_v[:, l].reshape(db * past, LANES)))
        ctx_k.append(k_ctx.reshape(nb, seq, ATT_KV_HEADS, HEAD_DIM))
        ctx_v.append(z_kv[:n_ctx, LANES:].reshape(nb, seq, ATT_KV_HEADS, HEAD_DIM))

        x, h2, pq = merge_project(x, mod_blocks, bonus, gate, z_gate,
                                  (of_c, of_l), (ob_c, ob_l), y_cv, (y_c, y_l),
                                  jnp.stack([rw_gn_w[l], rw_gn_b[l]]),
                                  w_out_rwkv[l].astype(BF16), w_out_conv[l].astype(BF16),
                                  w_out_attn[l].astype(BF16), w_o[l].astype(BF16), peer_wq[l].astype(BF16))

        kz = jnp.zeros_like(peer_keys[l][:, 0])
        keys_padded = jnp.concatenate([jnp.concatenate([peer_keys[l][:, 0], kz], axis=-1),
                                       jnp.concatenate([kz, peer_keys[l][:, 1]], axis=-1)], axis=1)
        x = peer_fused(pq, keys_padded, h2, peer_u[l].astype(BF16), peer_v[l].astype(BF16), x, mod_blocks)

    new_cache_k = jnp.stack(ctx_k, axis=1)
    new_cache_v = jnp.stack(ctx_v, axis=1)
    new_state = jnp.stack(ctx_s, axis=1)
    return (x[:n_ctx].reshape(nb, seq, d), x[n_ctx:].reshape(db, dseq, d),
            new_cache_k, new_cache_v, new_state)
```

```python
import functools
import math

import jax
import jax.numpy as jnp
from jax import lax
from jax.experimental import pallas as pl
from jax.experimental.pallas import tpu as pltpu

F32 = jnp.float32
BF16 = jnp.bfloat16

LANES = 128
SCAN_CHUNK = 64
SCAN_SUB = 16
RW_HEAD_DIM = 64
HEADS_PER_PAIR = LANES // RW_HEAD_DIM
VMEM_LIMIT = 56 * 1024 * 1024


def _dot(a, b):
    return jnp.dot(a.astype(BF16), b.astype(BF16), preferred_element_type=F32)


def _dot_nt(a, b):
    return lax.dot_general(a.astype(BF16), b.astype(BF16), (((1,), (1,)), ((), ())),
                           preferred_element_type=F32)


def _dot_tn(a, b):
    return lax.dot_general(a.astype(BF16), b.astype(BF16), (((0,), (0,)), ((), ())),
                           preferred_element_type=F32)


def _dot3(a, b):
    a_hi = a.astype(BF16)
    b_hi = b.astype(BF16)
    a_lo = (a - a_hi.astype(F32)).astype(BF16)
    b_lo = (b - b_hi.astype(F32)).astype(BF16)
    d = functools.partial(jnp.dot, preferred_element_type=F32)
    return d(a_hi, b_hi) + (d(a_hi, b_lo) + d(a_lo, b_hi))


def _scan_masks(reverse):
    C = SCAN_CHUNK
    P = HEADS_PER_PAIR * C
    row = lax.broadcasted_iota(jnp.int32, (P, LANES), 0)
    col = lax.broadcasted_iota(jnp.int32, (P, LANES), 1)
    tr = row % C
    tc = col % C
    t_i = lax.broadcasted_iota(jnp.int32, (C, C), 0)
    s_i = lax.broadcasted_iota(jnp.int32, (C, C), 1)
    return dict(
        same_head=(row // C) == (col // RW_HEAD_DIM),
        incl=(tr >= tc) if not reverse else (tr <= tc),
        strict=(tr > tc) if not reverse else (tr < tc),
        diag_blk=(tr // SCAN_SUB) == (tc // SCAN_SUB),
        eye=row == col,
        cum_mat=jnp.where((t_i >= s_i) if not reverse else (t_i <= s_i), 1.0, 0.0).astype(F32))


def _scan_chunk(chains, states):
    C = SCAN_CHUNK
    n = range(len(chains))
    mk = [ch[6] for ch in chains]
    cum = [ch[8] for ch in chains]
    pre = []
    for i in n:
        r, lw, k, v, kk, a, m, reverse, _ = chains[i]
        g = jnp.exp(cum[i])
        g_prev = jnp.exp(cum[i] - lw)
        g_inv = jnp.exp(-cum[i])
        g_end = jnp.exp(cum[i][C - 1:C, :] if not reverse else cum[i][0:1, :])
        bt = kk * a * g_inv
        kt = k * g_inv
        stack = lambda x, m=m: jnp.where(m["same_head"], jnp.concatenate([x, x], axis=0), 0.0)
        pre.append(dict(kap=stack(kk * g_prev), rt=stack(r * g), bt=stack(bt), kt=stack(kt),
                        v=stack(v), bh=stack(bt * g_end), kh=stack(kt * g_end), g_end=g_end))
    P = HEADS_PER_PAIR * C
    tri = [_dot_nt(jnp.concatenate([pre[i]["kap"], pre[i]["rt"]], axis=0),
                   jnp.concatenate([pre[i]["bt"], pre[i]["kt"]], axis=0)) for i in n]
    lb = [jnp.where(mk[i]["strict"], tri[i][:P, :P], 0.0) for i in n]
    lk = [jnp.where(mk[i]["strict"], tri[i][:P, P:], 0.0) for i in n]
    pb = [jnp.where(mk[i]["incl"], tri[i][P:, :P], 0.0) for i in n]
    pk = [jnp.where(mk[i]["incl"], tri[i][P:, P:], 0.0) for i in n]
    lkv = [_dot(lk[i], pre[i]["v"]) for i in n]

    d = [jnp.where(mk[i]["diag_blk"], lb[i], 0.0) for i in n]
    x = [jnp.concatenate([lb[i] - d[i], pre[i]["kap"], lkv[i]], axis=1) for i in n]
    x = [x[i] - _dot(d[i], x[i]) for i in n]
    p = [_dot(d[i], d[i]) for i in n]
    x = [x[i] + _dot(p[i], x[i]) for i in n]
    p = [_dot(p[i], p[i]) for i in n]
    x = [x[i] + _dot(p[i], x[i]) for i in n]
    p = [_dot(p[i], p[i]) for i in n]
    x = [x[i] + _dot(p[i], x[i]) for i in n]
    e = [x[i][:, :LANES] for i in n]
    rhs = [x[i][:, LANES:] for i in n]
    e2 = [_dot(e[i], e[i]) for i in n]
    t = [rhs[i] + _dot(e2[i], rhs[i]) for i in n]
    gy = [t[i] - _dot(e[i], t[i]) for i in n]

    m_mat = [jnp.where(mk[i]["eye"], pre[i]["g_end"], 0.0) - _dot_tn(pre[i]["bh"], gy[i][:, :LANES])
             for i in n]
    n_mat = [_dot_tn(pre[i]["kh"], pre[i]["v"]) - _dot_tn(pre[i]["bh"], gy[i][:, LANES:]) for i in n]
    pbgy = [_dot(pb[i], gy[i]) for i in n]
    q = [pre[i]["rt"] - pbgy[i][:, :LANES] for i in n]
    z = [_dot(pk[i], pre[i]["v"]) - pbgy[i][:, LANES:] for i in n]
    o_st = [_dot(q[i], states[i]) + z[i] for i in n]
    new = [_dot3(m_mat[i], states[i]) + n_mat[i] for i in n]
    return [(o_st[i][:C] + o_st[i][C:], new[i]) for i in n]


def _scan_kernel(rf, vf, kkf, lwf, af, kf, rb, vb, kkb, lwb, ab, kb, s0, of, ob, sfin, state):
    c = pl.program_id(1)
    n_pairs = rf.shape[-1] // LANES

    @pl.when(c == 0)
    def _():
        state[...] = s0[0]

    chains, states, outs = [], [], []
    for d, (r_, v_, kk_, lw_, a_, k_, o_) in enumerate(((rf, vf, kkf, lwf, af, kf, of),
                                                        (rb, vb, kkb, lwb, ab, kb, ob))):
        masks = _scan_masks(reverse=(d == 1))
        cum = _dot3(masks["cum_mat"], lw_[...])
        for p in range(n_pairs):
            sl = slice(p * LANES, (p + 1) * LANES)
            chains.append((r_[:, sl], lw_[:, sl], k_[:, sl], v_[:, sl], kk_[:, sl], a_[:, sl],
                           masks, d == 1, cum[:, sl]))
            states.append(state[d, p])
            outs.append((o_, sl, d, p))
    for (o, new), (o_, sl, d, p) in zip(_scan_chunk(chains, states), outs):
        o_[:, sl] = o
        state[d, p] = new

    @pl.when(c == pl.num_programs(1) - 1)
    def _():
        sfin[0] = state[...]


def rwkv_scan(r, v, kk, lw, a, kd, s0_bd, row0, n_seq, seq_len):
    width = r.shape[1]
    n_tok = n_seq * seq_len
    n_pairs = width // LANES
    n_chunk = seq_len // SCAN_CHUNK
    c0 = row0 // SCAN_CHUNK
    blk = (SCAN_CHUNK, width)
    dblk = (None, SCAN_CHUNK, width)
    fwd = lambda s, c: (s * n_chunk + c, 0)
    bwd = lambda s, c: (s * n_chunk + n_chunk - 1 - c, 0)
    fwd_in = lambda s, c: (c0 + s * n_chunk + c, 0)
    bwd_in = lambda s, c: (c0 + s * n_chunk + n_chunk - 1 - c, 0)
    fwd_d = lambda s, c: (0, c0 + s * n_chunk + c, 0)
    bwd_d = lambda s, c: (1, c0 + s * n_chunk + n_chunk - 1 - c, 0)
    sspec = pl.BlockSpec((1, 2, n_pairs, LANES, LANES), lambda s, c: (s, 0, 0, 0, 0))
    in_specs = ([pl.BlockSpec(blk, fwd_in)] * 3 + [pl.BlockSpec(dblk, fwd_d)] * 3
                + [pl.BlockSpec(blk, bwd_in)] * 3 + [pl.BlockSpec(dblk, bwd_d)] * 3 + [sspec])
    out_specs = [pl.BlockSpec(blk, fwd), pl.BlockSpec(blk, bwd), sspec]
    out_shape = [jax.ShapeDtypeStruct((n_tok, width), F32)] * 2 + [
        jax.ShapeDtypeStruct(s0_bd.shape, F32)]
    return pl.pallas_call(
        _scan_kernel,
        grid=(n_seq, n_chunk),
        in_specs=in_specs,
        out_specs=out_specs,
        out_shape=out_shape,
        scratch_shapes=[pltpu.VMEM((2, n_pairs, LANES, LANES), F32)],
        compiler_params=pltpu.CompilerParams(
            dimension_semantics=("parallel", "arbitrary"), vmem_limit_bytes=VMEM_LIMIT),
        name="rwkv_scan",
    )(r, v, kk, lw, a, kd, r, v, kk, lw, a, kd, s0_bd)


MOD_ROWS = 8
ROW_BLOCK = 256
RMS_EPS = 1e-6
LN_EPS = 1e-5
RW_GN_EPS = 64e-5


def _mod_kernel(c_ref, w_ref, b_ref, o_ref):
    c = c_ref[...]
    s = c * jax.nn.sigmoid(c)
    o_ref[...] = _dot(s, w_ref[...]) + b_ref[...]


def modulation(cond, w_mod, b_mod):
    n, d = cond.shape
    cols = w_mod.shape[1]
    blk = 1024
    return pl.pallas_call(
        _mod_kernel,
        grid=(cols // blk,),
        in_specs=[pl.BlockSpec((n, d), lambda j: (0, 0)),
                  pl.BlockSpec((d, blk), lambda j: (0, j)),
                  pl.BlockSpec((1, blk), lambda j: (0, j))],
        out_specs=pl.BlockSpec((n, blk), lambda j: (0, j)),
        out_shape=jax.ShapeDtypeStruct((n, cols), F32),
        name="modulation",
    )(cond, w_mod, b_mod)


def _modulate(x, shift, scale):
    ms = jnp.mean(x * x, axis=-1, keepdims=True)
    return x * lax.rsqrt(ms + RMS_EPS) * (1.0 + scale) + shift


def _inproj_kernel(x_ref, mod_ref, *refs):
    n = len(refs) // 2
    h = _modulate(x_ref[...], mod_ref[0, 0:1, :], mod_ref[0, 1:2, :]).astype(BF16)
    for w_ref, o_ref in zip(refs[:n], refs[n:]):
        o_ref[...] = jnp.dot(h, w_ref[...], preferred_element_type=F32)


def in_projection(x, mod_blocks, weights):
    n_tok, d = x.shape
    row = lambda i: (i, 0)
    const = lambda i: (0, 0)
    return pl.pallas_call(
        _inproj_kernel,
        grid=(n_tok // ROW_BLOCK,),
        in_specs=[pl.BlockSpec((ROW_BLOCK, d), row),
                  pl.BlockSpec((1, MOD_ROWS, d), lambda i: (i, 0, 0))]
                 + [pl.BlockSpec(w.shape, const, pipeline_mode=pl.Buffered(1)) for w in weights],
        out_specs=[pl.BlockSpec((ROW_BLOCK, w.shape[1]), row) for w in weights],
        out_shape=[jax.ShapeDtypeStruct((n_tok, w.shape[1]), F32) for w in weights],
        compiler_params=pltpu.CompilerParams(
            dimension_semantics=("parallel",), vmem_limit_bytes=VMEM_LIMIT),
        name="in_projection",
    )(x, mod_blocks, *weights)


HEAD_DIM = 64


def _group_ones(width):
    r = lax.broadcasted_iota(jnp.int32, (width, width), 0) // HEAD_DIM
    c = lax.broadcasted_iota(jnp.int32, (width, width), 1) // HEAD_DIM
    return jnp.where(r == c, 1.0, 0.0).astype(BF16)


def _group_sum(x, ones):
    hi = x.astype(BF16)
    lo = (x - hi.astype(F32)).astype(BF16)
    d = functools.partial(jnp.dot, preferred_element_type=F32)
    return d(hi, ones) + d(lo, ones)


RW_WIDTH = 512
RW_COLS = 3 * RW_WIDTH + 3 * LANES
RW_VEC_ROWS = 8


def _rwkv_prep_kernel(z_ref, hp_ref, hn_ref, mu_ref, vec_ref, w2_ref, a2_ref, g2_ref,
                      r_ref, v_ref, kk_ref, lw_ref, a_ref, kd_ref, bonus_ref, gate_ref):
    z = z_ref[...]
    rb = z.shape[0]
    row = lax.broadcasted_iota(jnp.int32, (rb, 1), 0)
    z_prev = jnp.where(row == 0, hp_ref[0], pltpu.roll(z, 1, 0))
    z_next = jnp.where(row == rb - 1, hn_ref[0], pltpu.roll(z, rb - 1, 0))
    zs = z + mu_ref[0:1, :] * (z_prev - z) + mu_ref[1:2, :] * (z_next - z)
    w = RW_WIDTH
    r, k, v = zs[:, :w], zs[:, w:2 * w], zs[:, 2 * w:3 * w]
    zw = zs[:, 3 * w:3 * w + LANES]
    za = zs[:, 3 * w + LANES:3 * w + 2 * LANES]
    zg = zs[:, 3 * w + 2 * LANES:]
    ones = _group_ones(w)
    kk = k * vec_ref[0:1, :]
    kk = kk * lax.rsqrt(_group_sum(kk * kk, ones) + 1e-12)
    r_ref[...] = r
    v_ref[...] = v
    kk_ref[...] = kk
    tz = jnp.tanh(zw)
    for d in range(2):
        w_raw = vec_ref[3 + d:4 + d, :] + _dot(tz, w2_ref[d])
        lw_ref[d] = -math.exp(-0.5) * jax.nn.sigmoid(w_raw)
        a = jax.nn.sigmoid(vec_ref[5 + d:6 + d, :] + _dot(za, a2_ref[d]))
        a_ref[d] = a
        kd_ref[d] = k * (1.0 + (a - 1.0) * vec_ref[1:2, :])
    bonus_ref[...] = _group_sum(r * k * vec_ref[2:3, :], ones) * v
    gate_ref[...] = _dot(jax.nn.sigmoid(zg), g2_ref[...])


def rwkv_prep(z_rw, halo_prev, halo_next, mu, vecs, w2_pad, a2_pad, g2):
    n_tok = z_rw.shape[0]
    w = RW_WIDTH
    row = lambda i: (i, 0)
    const2 = lambda i: (0, 0)
    const3 = lambda i: (0, 0, 0)
    halo = pl.BlockSpec((1, 1, RW_COLS), lambda i: (i, 0, 0))
    one = pl.BlockSpec((ROW_BLOCK, w), row)
    two = pl.BlockSpec((2, ROW_BLOCK, w), lambda i: (0, i, 0))
    s1 = jax.ShapeDtypeStruct((n_tok, w), F32)
    s2 = jax.ShapeDtypeStruct((2, n_tok, w), F32)
    return pl.pallas_call(
        _rwkv_prep_kernel,
        grid=(n_tok // ROW_BLOCK,),
        in_specs=[pl.BlockSpec((ROW_BLOCK, RW_COLS), row), halo, halo,
                  pl.BlockSpec(mu.shape, const2), pl.BlockSpec(vecs.shape, const2),
                  pl.BlockSpec(w2_pad.shape, const3), pl.BlockSpec(a2_pad.shape, const3),
                  pl.BlockSpec(g2.shape, const2)],
        out_specs=[one, one, one, two, two, two, one, one],
        out_shape=[s1, s1, s1, s2, s2, s2, s1, s1],
        compiler_params=pltpu.CompilerParams(
            dimension_semantics=("parallel",), vmem_limit_bytes=VMEM_LIMIT),
        name="rwkv_prep",
    )(z_rw, halo_prev, halo_next, mu, vecs, w2_pad, a2_pad, g2)


CONV_WIDTH = 512
CONV_KERNEL = 31
CONV_PAD = 16
CONV_ROWS = 64


def _conv_kernel(z_ref, w_ref, vec_ref, o_ref, pad_ref):
    t = z_ref.shape[0]
    cw = CONV_WIDTH
    half = CONV_KERNEL // 2
    zeros = jnp.zeros((CONV_PAD, cw), F32)
    pad_ref[0:CONV_PAD, :] = zeros
    pad_ref[CONV_PAD + t:CONV_PAD + t + CONV_PAD, :] = zeros
    pad_ref[CONV_PAD:CONV_PAD + t, :] = z_ref[:, :cw] * jax.nn.sigmoid(z_ref[:, cw:])

    def chunk(i, carry):
        base = pl.multiple_of(i * CONV_ROWS, CONV_ROWS)
        window = pad_ref[pl.ds(base, CONV_ROWS + 2 * CONV_PAD), :]
        acc = jnp.zeros((CONV_ROWS, cw), F32)
        for j in range(CONV_KERNEL):
            lo = CONV_PAD - half + j
            acc = acc + w_ref[j:j + 1, :] * window[lo:lo + CONV_ROWS]
        u = acc + vec_ref[0:1, :]
        mu = jnp.mean(u, axis=-1, keepdims=True)
        cen = u - mu
        var = jnp.mean(cen * cen, axis=-1, keepdims=True)
        y = cen * lax.rsqrt(var + LN_EPS) * vec_ref[1:2, :] + vec_ref[2:3, :]
        o_ref[pl.ds(base, CONV_ROWS), :] = y * jax.nn.sigmoid(y)
        return carry

    lax.fori_loop(0, t // CONV_ROWS, chunk, 0)


def conv_module(z_cv, conv_w, vecs, row0, n_seq, seq_len):
    cw = CONV_WIDTH
    blk0 = row0 // seq_len
    return pl.pallas_call(
        _conv_kernel,
        grid=(n_seq,),
        in_specs=[pl.BlockSpec((seq_len, 2 * cw), lambda i: (blk0 + i, 0)),
                  pl.BlockSpec(conv_w.shape, lambda i: (0, 0)),
                  pl.BlockSpec(vecs.shape, lambda i: (0, 0))],
        out_specs=pl.BlockSpec((seq_len, cw), lambda i: (i, 0)),
        out_shape=jax.ShapeDtypeStruct((n_seq * seq_len, cw), F32),
        scratch_shapes=[pltpu.VMEM((seq_len + 2 * CONV_PAD, cw), F32)],
        compiler_params=pltpu.CompilerParams(
            dimension_semantics=("parallel",), vmem_limit_bytes=VMEM_LIMIT),
        name="conv_module",
    )(z_cv, conv_w, vecs)


ATT_Q_BLOCK = 256
ATT_HEADS = 8
ATT_KV_HEADS = 2
ATT_GROUP = ATT_HEADS // ATT_KV_HEADS


def _rot_partner():
    r = lax.broadcasted_iota(jnp.int32, (LANES, LANES), 0)
    c = lax.broadcasted_iota(jnp.int32, (LANES, LANES), 1)
    lo = (r % 32) < 16
    return jnp.where((c == r + 16) & lo, 1.0, jnp.where((c == r - 16) & ~lo, -1.0, 0.0)).astype(BF16)


def _attn_kernel(*refs, rotary):
    if rotary:
        (zq_ref, zkv_ref, nq_ref, nk_ref, cosq_ref, sinq_ref, cosk_ref, sink_ref, ck_ref, cv_ref,
         o_ref) = refs
    else:
        zq_ref, zkv_ref, nq_ref, nk_ref, o_ref, kn_ref = refs
    tq = zq_ref.shape[0]
    ones = _group_ones(LANES)
    lane = lax.broadcasted_iota(jnp.int32, (1, LANES), 1)
    first = lane < HEAD_DIM
    row_head = lax.broadcasted_iota(jnp.int32, (2 * tq, LANES), 0) // tq
    own = row_head == (lax.broadcasted_iota(jnp.int32, (2 * tq, LANES), 1) // HEAD_DIM)

    def norm(x, g):
        return x * lax.rsqrt(_group_sum(x * x, ones) * (1.0 / HEAD_DIM) + RMS_EPS) * g

    def rope(x, cos, sin):
        return x * cos + jnp.dot(x.astype(BF16), _rot_partner(), preferred_element_type=F32) * sin

    def dup(x):
        sw = pltpu.roll(x, HEAD_DIM, 1)
        return jnp.where(first, x, sw), jnp.where(first, sw, x)

    k = norm(zkv_ref[:, :LANES], nk_ref[...])
    v = zkv_ref[:, LANES:]
    if rotary:
        k = rope(k, cosk_ref[...], sink_ref[...])
        ck, cv = dup(ck_ref[...]), dup(cv_ref[...])
    else:
        kn_ref[...] = k
    kd, vd = dup(k), dup(v)
    for pair in range(ATT_HEADS // 2):
        g = pair // (ATT_GROUP // 2)
        q = norm(zq_ref[:, pair * LANES:(pair + 1) * LANES], nq_ref[...])
        if rotary:
            q = rope(q, cosq_ref[...], sinq_ref[...])
        q = q * (HEAD_DIM ** -0.5)
        qs = jnp.where(own, jnp.concatenate([q, q], axis=0), 0.0)
        s = _dot_nt(qs, kd[g])
        m = jnp.max(s, axis=-1, keepdims=True)
        if rotary:
            s2 = _dot_nt(qs, ck[g])
            m = jnp.maximum(m, jnp.max(s2, axis=-1, keepdims=True))
            p2 = jnp.exp(s2 - m)
        p = jnp.exp(s - m)
        den = jnp.sum(p, axis=-1, keepdims=True)
        o = _dot(p, vd[g])
        if rotary:
            den = den + jnp.sum(p2, axis=-1, keepdims=True)
            o = o + _dot(p2, cv[g])
        o = o / den
        o_ref[:, pair * LANES:(pair + 1) * LANES] = jnp.where(first, o[:tq], o[tq:])


def attention(z_q, z_kv, nq, nk, row0, n_seq, seq_len, rope=None, cache=None):
    qb = min(ATT_Q_BLOCK, seq_len)
    nqb = seq_len // qb
    qblk0 = row0 // qb
    sblk0 = row0 // seq_len
    wq = z_q.shape[1]
    in_specs = [pl.BlockSpec((qb, wq), lambda b, t: (qblk0 + b * nqb + t, 0)),
                pl.BlockSpec((seq_len, 2 * LANES), lambda b, t: (sblk0 + b, 0)),
                pl.BlockSpec((1, LANES), lambda b, t: (0, 0)),
                pl.BlockSpec((1, LANES), lambda b, t: (0, 0))]
    args = [z_q, z_kv, nq, nk]
    out_q = pl.BlockSpec((qb, wq), lambda b, t: (b * nqb + t, 0))
    y_shape = jax.ShapeDtypeStruct((n_seq * seq_len, wq), F32)
    if rope is None:
        out_specs = [out_q, pl.BlockSpec((seq_len, LANES), lambda b, t: (b, 0))]
        out_shape = [y_shape, jax.ShapeDtypeStruct((n_seq * seq_len, LANES), F32)]
    else:
        past = cache[0].shape[0] // n_seq
        in_specs += [pl.BlockSpec((qb, LANES), lambda b, t: (t, 0))] * 2
        in_specs += [pl.BlockSpec((seq_len, LANES), lambda b, t: (0, 0))] * 2
        in_specs += [pl.BlockSpec((past, LANES), lambda b, t: (b, 0))] * 2
        args += [rope[0], rope[1], rope[0], rope[1], cache[0], cache[1]]
        out_specs = out_q
        out_shape = y_shape
    return pl.pallas_call(
        functools.partial(_attn_kernel, rotary=rope is not None),
        grid=(n_seq, nqb),
        in_specs=in_specs,
        out_specs=out_specs,
        out_shape=out_shape,
        compiler_params=pltpu.CompilerParams(
            dimension_semantics=("parallel", "arbitrary"), vmem_limit_bytes=VMEM_LIMIT),
        name="attention",
    )(*args)


def _merge_kernel(x_ref, mod_ref, bonus_ref, gate_ref, zg_ref,
                  ofc_ref, ofl_ref, obc_ref, obl_ref, cvc_ref, cvl_ref, atc_ref, atl_ref,
                  gn_ref, wr_ref, wc_ref, wa_ref, wo_ref, wq_ref, xo_ref, h2_ref, q_ref, *, n_ctx_blocks):
    d = x_ref.shape[1]
    zg = zg_ref[...]
    is_ctx = pl.program_id(0) < n_ctx_blocks
    pick = lambda c_ref, l_ref: jnp.where(is_ctx, c_ref[...], l_ref[...])
    ones = _group_ones(ofc_ref.shape[1])
    o = pick(ofc_ref, ofl_ref) + pick(obc_ref, obl_ref)
    cen = o - _group_sum(o, ones) * (1.0 / HEAD_DIM)
    var = _group_sum(cen * cen, ones) * (1.0 / HEAD_DIM)
    o = cen * lax.rsqrt(var + RW_GN_EPS) * gn_ref[0:1, :] + gn_ref[1:2, :]
    y_rw = (o + bonus_ref[...]) * gate_ref[...]
    merged = (jax.nn.sigmoid(zg[:, :d]) * _dot(y_rw, wr_ref[...])
              + jax.nn.sigmoid(zg[:, d:2 * d]) * _dot(pick(cvc_ref, cvl_ref), wc_ref[...])
              + jax.nn.sigmoid(zg[:, 2 * d:]) * _dot(pick(atc_ref, atl_ref), wa_ref[...]))
    x = x_ref[...] + mod_ref[0, 2:3, :] * _dot(merged, wo_ref[...])
    xo_ref[...] = x
    h2 = _modulate(x, mod_ref[0, 3:4, :], mod_ref[0, 4:5, :]).astype(BF16)
    h2_ref[...] = h2
    q_ref[...] = jnp.dot(h2, wq_ref[...], preferred_element_type=F32)


def merge_project(x, mod_blocks, bonus, gate, z_gate, o_f, o_b, y_cv, y_at, gn, w_r, w_c, w_a, w_o, w_q):
    n_tok, d = x.shape
    nc = o_f[0].shape[0] // ROW_BLOCK
    row = lambda i: (i, 0)
    const = lambda i: (0, 0)
    ctx_row = lambda i: (jnp.minimum(i, nc - 1), 0)
    lat_row = lambda i: (jnp.maximum(i - nc, 0), 0)
    acts = [bonus, gate, z_gate]
    weights = [gn, w_r, w_c, w_a, w_o, w_q]
    pair_specs, pair_args = [], []
    for c_arr, l_arr in (o_f, o_b, y_cv, y_at):
        pair_specs += [pl.BlockSpec((ROW_BLOCK, c_arr.shape[1]), ctx_row),
                       pl.BlockSpec((ROW_BLOCK, l_arr.shape[1]), lat_row)]
        pair_args += [c_arr, l_arr]
    return pl.pallas_call(
        functools.partial(_merge_kernel, n_ctx_blocks=nc),
        grid=(n_tok // ROW_BLOCK,),
        in_specs=[pl.BlockSpec((ROW_BLOCK, d), row),
                  pl.BlockSpec((1, MOD_ROWS, d), lambda i: (i, 0, 0))]
                 + [pl.BlockSpec((ROW_BLOCK, a.shape[1]), row) for a in acts]
                 + pair_specs
                 + [pl.BlockSpec(w.shape, const) for w in weights],
        out_specs=[pl.BlockSpec((ROW_BLOCK, d), row), pl.BlockSpec((ROW_BLOCK, d), row),
                   pl.BlockSpec((ROW_BLOCK, w_q.shape[1]), row)],
        out_shape=[jax.ShapeDtypeStruct((n_tok, d), F32), jax.ShapeDtypeStruct((n_tok, d), BF16),
                   jax.ShapeDtypeStruct((n_tok, w_q.shape[1]), F32)],
        compiler_params=pltpu.CompilerParams(
            dimension_semantics=("parallel",), vmem_limit_bytes=VMEM_LIMIT),
        name="merge_project",
    )(x, mod_blocks, *acts, *pair_args, *weights)


PEER_HEADS = 8
PEER_N_KEYS = 128
PEER_HALF = 64
PEER_TOPK = 16
PEER_SEL_BLOCK = 128
PEER_LOOP_UNROLL = 16
PEER_RELAYOUT_TOKENS = 32
PEER_RELAYOUT_KEYS = 8
NEG = -1e30


def _dot3_nt(a, b):
    a_hi = a.astype(BF16)
    b_hi = b.astype(BF16)
    a_lo = (a - a_hi.astype(F32)).astype(BF16)
    b_lo = (b - b_hi.astype(F32)).astype(BF16)
    d = lambda x, y: lax.dot_general(x, y, (((1,), (1,)), ((), ())), preferred_element_type=F32)
    return d(a_hi, b_hi) + (d(a_hi, b_lo) + d(a_lo, b_hi))


def _extract_max(s, code):
    m = jnp.max(s, axis=0, keepdims=True)
    pos = jnp.min(jnp.where(s == m, code, 1e9), axis=0, keepdims=True)
    return m, pos, jnp.where(code == pos, NEG, s)


def _pair_candidates(a0, a1):
    k = PEER_TOPK
    sub = lax.broadcasted_iota(jnp.int32, (k, a0.shape[1]), 0).astype(F32)
    sub8 = sub[:8]
    vals = [a0[0:1] + a1]
    codes = [sub]
    for r0 in range(1, 8):
        limit = k // (r0 + 1)
        vals.append(jnp.where(sub8 < limit, a0[r0:r0 + 1] + a1[:8], NEG))
        codes.append(sub8 + float(r0 * k))
    vals.append(a0[8:] + a1[0:1])
    codes.append((sub8 + 8.0) * float(k))
    return jnp.concatenate(vals, axis=0), jnp.concatenate(codes, axis=0)


def _peer_select_kernel(q_ref, keys_ref, w_ref, top_v, top_i, slot_i, slot_j, slot_g,
                        rows_i, rows_j, rows_g, w3_ref):
    k = PEER_TOPK
    ct = q_ref.shape[0]
    key_code = lax.broadcasted_iota(jnp.int32, (PEER_N_KEYS, ct), 0).astype(F32)
    rank = lax.broadcasted_iota(jnp.int32, (k, ct), 0).astype(F32)
    for h in range(PEER_HEADS):
        st = _dot3_nt(keys_ref[h], q_ref[:, h * LANES:(h + 1) * LANES])
        for p in range(2):
            s = st[p * PEER_N_KEYS:(p + 1) * PEER_N_KEYS]
            for it in range(k):
                m, pos, s = _extract_max(s, key_code)
                top_v[p, it:it + 1, :] = m
                top_i[p, it:it + 1, :] = pos
        a0, a1 = top_v[0], top_v[1]
        i0, i1 = top_i[0], top_i[1]
        cand, code = _pair_candidates(a0, a1)
        best = a0[0:1] + a1[0:1]
        z = jnp.zeros_like(best)
        for it in range(k):
            m, pos, cand = _extract_max(cand, code)
            r0 = jnp.floor(pos * (1.0 / k))
            r1 = pos - r0 * k
            e = jnp.exp(m - best)
            z = z + e
            slot = h * k + it
            slot_i[slot:slot + 1, :] = jnp.sum(jnp.where(rank == r0, i0, 0.0), axis=0, keepdims=True)
            slot_j[slot:slot + 1, :] = jnp.sum(jnp.where(rank == r1, i1, 0.0), axis=0, keepdims=True)
            slot_g[slot:slot + 1, :] = e
        slot_g[h * k:(h + 1) * k, :] = slot_g[h * k:(h + 1) * k, :] / z

    rows_i[...] = slot_i[...].T
    rows_j[...] = slot_j[...].T
    rows_g[...] = slot_g[...].T
    n = PEER_N_KEYS
    sub = lax.broadcasted_iota(jnp.int32, (n, n), 0).astype(F32)

    def per_token(c, carry):
        a_t = jnp.where(sub == rows_i[pl.ds(c, 1), :], rows_g[pl.ds(c, 1), :], 0.0)
        b_t = jnp.where(sub == rows_j[pl.ds(c, 1), :], 1.0, 0.0)
        w3_ref[c] = _dot_nt(a_t, b_t)
        return carry

    lax.fori_loop(0, ct, per_token, 0, unroll=PEER_LOOP_UNROLL)

    tg, ig = PEER_RELAYOUT_TOKENS, PEER_RELAYOUT_KEYS
    r_out = lax.broadcasted_iota(jnp.int32, (tg * ig, tg * ig), 0)
    r_in = lax.broadcasted_iota(jnp.int32, (tg * ig, tg * ig), 1)
    perm = jnp.where(r_in == (r_out % tg) * ig + r_out // tg, 1.0, 0.0).astype(BF16)

    for ib in range(n // ig):
        for t in range(ct // tg):
            z = w3_ref[t * tg:(t + 1) * tg, ib * ig:(ib + 1) * ig, :]
            y = jnp.dot(perm, z.reshape(tg * ig, n).astype(BF16), preferred_element_type=F32)
            for kk in range(ig):
                col = (ib * ig + kk) * n
                w_ref[t * tg:(t + 1) * tg, col:col + n] = y[kk * tg:(kk + 1) * tg].astype(BF16)


def peer_select(q, keys_padded):
    n_tok, qd = q.shape
    ct = PEER_SEL_BLOCK
    n = PEER_N_KEYS
    slots = PEER_HEADS * PEER_TOPK
    return pl.pallas_call(
        _peer_select_kernel,
        grid=(n_tok // ct,),
        in_specs=[pl.BlockSpec((ct, qd), lambda i: (i, 0)),
                  pl.BlockSpec(keys_padded.shape, lambda i: (0, 0, 0))],
        out_specs=pl.BlockSpec((ct, n * n), lambda i: (i, 0)),
        out_shape=jax.ShapeDtypeStruct((n_tok, n * n), BF16),
        scratch_shapes=[pltpu.VMEM((2, PEER_TOPK, ct), F32), pltpu.VMEM((2, PEER_TOPK, ct), F32),
                        pltpu.VMEM((slots, ct), F32), pltpu.VMEM((slots, ct), F32),
                        pltpu.VMEM((slots, ct), F32),
                        pltpu.VMEM((ct, slots), F32), pltpu.VMEM((ct, slots), F32),
                        pltpu.VMEM((ct, slots), F32),
                        pltpu.VMEM((ct, n, n), F32)],
        compiler_params=pltpu.CompilerParams(
            dimension_semantics=("parallel",), vmem_limit_bytes=VMEM_LIMIT),
        name="peer_select",
    )(q, keys_padded)


PEER_TOKEN_BLOCK = 1024
PEER_EXPERT_BLOCK = 1024


def _peer_kernel(h_ref, w_ref, u_ref, v_ref, x_ref, mod_ref, o_ref, acc_ref):
    e = pl.program_id(1)

    @pl.when(e == 0)
    def _():
        acc_ref[...] = jnp.zeros_like(acc_ref)

    s = _dot_nt(h_ref[...], u_ref[...])
    act = 0.5 * s * (1.0 + lax.erf(s * (2.0 ** -0.5)))
    acc_ref[...] += _dot(act * w_ref[...].astype(F32), v_ref[...])

    @pl.when(e == pl.num_programs(1) - 1)
    def _():
        for j in range(mod_ref.shape[0]):
            rows = slice(j * ROW_BLOCK, (j + 1) * ROW_BLOCK)
            o_ref[rows, :] = x_ref[rows, :] + mod_ref[j, 5:6, :] * acc_ref[rows, :]


def peer_dense(h2, w_sel, u_tab, v_tab, x, mod_blocks):
    n_tok, d = x.shape
    n_exp = u_tab.shape[0]
    tb, eb = PEER_TOKEN_BLOCK, PEER_EXPERT_BLOCK
    tok = lambda i, e: (i, 0)
    return pl.pallas_call(
        _peer_kernel,
        grid=(n_tok // tb, n_exp // eb),
        in_specs=[pl.BlockSpec((tb, d), tok),
                  pl.BlockSpec((tb, eb), lambda i, e: (i, e)),
                  pl.BlockSpec((eb, d), lambda i, e: (e, 0)),
                  pl.BlockSpec((eb, d), lambda i, e: (e, 0)),
                  pl.BlockSpec((tb, d), tok),
                  pl.BlockSpec((tb // ROW_BLOCK, MOD_ROWS, d), lambda i, e: (i, 0, 0))],
        out_specs=pl.BlockSpec((tb, d), tok),
        out_shape=jax.ShapeDtypeStruct((n_tok, d), F32),
        scratch_shapes=[pltpu.VMEM((tb, d), F32)],
        compiler_params=pltpu.CompilerParams(
            dimension_semantics=("parallel", "arbitrary"), vmem_limit_bytes=VMEM_LIMIT),
        name="peer_dense",
    )(h2, w_sel, u_tab, v_tab, x, mod_blocks)


FUSED_TOKEN_BLOCK = 256
FUSED_EXPERT_BLOCK = 2048
FUSED_PHASES = 4
FUSED_RELAYOUT = 16
STAGE1_SPLIT = 5
FUSED_DENSE_CHUNKS = 8


def _stage1_items(q_ref, keys_ref, row0, heads, top_v, top_i):
    ct = PEER_SEL_BLOCK
    key_code = lax.broadcasted_iota(jnp.int32, (PEER_N_KEYS, ct), 0).astype(F32)
    st = {}
    items = []

    def score(h):
        st["both"] = _dot3_nt(keys_ref[h], q_ref[pl.ds(row0, ct), h * LANES:(h + 1) * LANES])

    def start(p):
        st["s"] = st["both"][p * PEER_N_KEYS:(p + 1) * PEER_N_KEYS]

    def step(h, p, it):
        m, pos, st["s"] = _extract_max(st["s"], key_code)
        top_v[h, p, it:it + 1, :] = m
        top_i[h, p, it:it + 1, :] = pos

    for h in heads:
        items.append(functools.partial(score, h))
        for p in range(2):
            items.append(functools.partial(start, p))
            items += [functools.partial(step, h, p, it) for it in range(PEER_TOPK)]
    return items


def _stage2_items(top_v, top_i, slot_i, slot_j, slot_g, rows_i, rows_j, rows_g):
    k = PEER_TOPK
    ct = PEER_SEL_BLOCK
    rank = lax.broadcasted_iota(jnp.int32, (k, ct), 0).astype(F32)
    st = {}
    items = []

    def start(h):
        a0, a1 = top_v[h, 0], top_v[h, 1]
        st["cand"], st["code"] = _pair_candidates(a0, a1)
        st["best"] = a0[0:1] + a1[0:1]
        st["z"] = jnp.zeros_like(st["best"])

    def step(h, it):
        m, pos, st["cand"] = _extract_max(st["cand"], st["code"])
        r0 = jnp.floor(pos * (1.0 / k))
        r1 = pos - r0 * k
        e = jnp.exp(m - st["best"])
        st["z"] = st["z"] + e
        slot = h * k + it
        slot_i[slot:slot + 1, :] = jnp.sum(jnp.where(rank == r0, top_i[h, 0], 0.0), axis=0, keepdims=True)
        slot_j[slot:slot + 1, :] = jnp.sum(jnp.where(rank == r1, top_i[h, 1], 0.0), axis=0, keepdims=True)
        slot_g[slot:slot + 1, :] = e

    def finish(h):
        slot_g[h * k:(h + 1) * k, :] = slot_g[h * k:(h + 1) * k, :] / st["z"]

    def transpose():
        rows_i[...] = slot_i[...].T
        rows_j[...] = slot_j[...].T
        rows_g[...] = slot_g[...].T

    for h in range(PEER_HEADS):
        items.append(functools.partial(start, h))
        items += [functools.partial(step, h, it) for it in range(k)]
        items.append(functools.partial(finish, h))
    items.append(transpose)
    return items


def _assemble_items(rows_i, rows_j, rows_g, w3_ref):
    n = PEER_N_KEYS
    sub = lax.broadcasted_iota(jnp.int32, (n, n), 0).astype(F32)

    def token(c):
        a_t = jnp.where(sub == rows_i[c:c + 1, :], rows_g[c:c + 1, :], 0.0)
        b_t = jnp.where(sub == rows_j[c:c + 1, :], 1.0, 0.0)
        w3_ref[c] = _dot_nt(a_t, b_t).astype(BF16)

    return [functools.partial(token, c) for c in range(PEER_SEL_BLOCK)]


def _relayout_items(w3_ref, wbuf, buf, row0):
    n = PEER_N_KEYS
    g = FUSED_RELAYOUT
    r_out = lax.broadcasted_iota(jnp.int32, (g * g, g * g), 0)
    r_in = lax.broadcasted_iota(jnp.int32, (g * g, g * g), 1)
    perm = jnp.where(r_in == (r_out % g) * g + r_out // g, 1.0, 0.0).astype(BF16)

    def slab(ib, tt):
        z = w3_ref[tt * g:(tt + 1) * g, ib * g:(ib + 1) * g, :].reshape(g * g, n)
        y = jnp.dot(perm, z, preferred_element_type=F32)
        rows = pl.ds(pl.multiple_of(row0 + tt * g, g), g)
        for kk in range(g):
            col = (ib * g + kk) * n
            wbuf[buf, rows, col:col + n] = y[kk * g:(kk + 1) * g].astype(BF16)

    return [functools.partial(slab, ib, tt) for ib in range(n // g) for tt in range(PEER_SEL_BLOCK // g)]


def _dense_items(h_ref, u_ref, v_ref, wbuf, use, col0, acc_ref):
    cs = u_ref.shape[0] // FUSED_DENSE_CHUNKS
    st = {}

    def scores(c):
        st["s"] = _dot_nt(h_ref[...], u_ref[c * cs:(c + 1) * cs, :])

    def gate(c):
        s = st["s"]
        act = 0.5 * s * (1.0 + lax.erf(s * (2.0 ** -0.5)))
        w = wbuf[use, :, pl.ds(pl.multiple_of(col0 + c * cs, cs), cs)]
        st["g"] = act.astype(BF16) * w

    def accumulate(c):
        acc_ref[...] += jnp.dot(st["g"], v_ref[c * cs:(c + 1) * cs, :], preferred_element_type=F32)

    return [functools.partial(f, c) for c in range(FUSED_DENSE_CHUNKS) for f in (scores, gate, accumulate)]


def _interleave(a, b):
    done = 0
    for i, item in enumerate(a):
        item()
        upto = (i + 1) * len(b) // len(a)
        for other in b[done:upto]:
            other()
        done = upto


def _peer_fused_kernel(q_ref, keys_ref, h_ref, u_ref, v_ref, x_ref, mod_ref, o_ref,
                       acc_ref, wbuf, top_v, top_i, slot_i, slot_j, slot_g, rows_i, rows_j, rows_g,
                       w3_ref):
    t = pl.program_id(0)
    e = pl.program_id(1)
    fill = t % 2
    use = 1 - fill
    eb = u_ref.shape[0]

    @pl.when((t == 0) & (e == 0))
    def _():
        wbuf[1] = jnp.zeros(wbuf.shape[1:], BF16)

    @pl.when(e == 0)
    def _():
        acc_ref[...] = jnp.zeros_like(acc_ref)

    def dense():
        return _dense_items(h_ref, u_ref, v_ref, wbuf, use, e * eb, acc_ref)

    row0 = pl.multiple_of((e // FUSED_PHASES) * PEER_SEL_BLOCK, PEER_SEL_BLOCK)
    phase = e % FUSED_PHASES

    @pl.when(phase == 0)
    def _():
        _interleave(dense(), _stage1_items(q_ref, keys_ref, row0, range(STAGE1_SPLIT), top_v, top_i))

    @pl.when(phase == 1)
    def _():
        _interleave(dense(),
                    _stage1_items(q_ref, keys_ref, row0, range(STAGE1_SPLIT, PEER_HEADS), top_v, top_i)
                    + _stage2_items(top_v, top_i, slot_i, slot_j, slot_g, rows_i, rows_j, rows_g))

    @pl.when(phase == 2)
    def _():
        _interleave(dense(), _assemble_items(rows_i, rows_j, rows_g, w3_ref))

    @pl.when(phase == 3)
    def _():
        _interleave(dense(), _relayout_items(w3_ref, wbuf, fill, row0))

    @pl.when(e == pl.num_programs(1) - 1)
    def _():
        o_ref[...] = x_ref[...] + mod_ref[0, 5:6, :] * acc_ref[...]


def peer_fused(q, keys_padded, h2, u_tab, v_tab, x, mod_blocks):
    n_tok, d = x.shape
    n_exp = u_tab.shape[0]
    tb, eb = FUSED_TOKEN_BLOCK, FUSED_EXPERT_BLOCK
    ct = PEER_SEL_BLOCK
    n = PEER_N_KEYS
    slots = PEER_HEADS * PEER_TOPK
    n_blk = n_tok // tb
    assert tb == ROW_BLOCK and (tb // ct) * FUSED_PHASES == n_exp // eb
    cur = lambda t, e: (jnp.minimum(t, n_blk - 1), 0)
    prev = lambda t, e: (jnp.maximum(t - 1, 0), 0)
    return pl.pallas_call(
        _peer_fused_kernel,
        grid=(n_blk + 1, n_exp // eb),
        in_specs=[pl.BlockSpec((tb, q.shape[1]), cur),
                  pl.BlockSpec(keys_padded.shape, lambda t, e: (0, 0, 0)),
                  pl.BlockSpec((tb, d), prev),
                  pl.BlockSpec((eb, d), lambda t, e: (e, 0)),
                  pl.BlockSpec((eb, d), lambda t, e: (e, 0)),
                  pl.BlockSpec((tb, d), prev),
                  pl.BlockSpec((1, MOD_ROWS, d), lambda t, e: (jnp.maximum(t - 1, 0), 0, 0))],
        out_specs=pl.BlockSpec((tb, d), prev),
        out_shape=jax.ShapeDtypeStruct((n_tok, d), F32),
        scratch_shapes=[pltpu.VMEM((tb, d), F32),
                        pltpu.VMEM((2, tb, n * n), BF16),
                        pltpu.VMEM((PEER_HEADS, 2, PEER_TOPK, ct), F32),
                        pltpu.VMEM((PEER_HEADS, 2, PEER_TOPK, ct), F32),
                        pltpu.VMEM((slots, ct), F32), pltpu.VMEM((slots, ct), F32),
                        pltpu.VMEM((slots, ct), F32),
                        pltpu.VMEM((ct, slots), F32), pltpu.VMEM((ct, slots), F32),
                        pltpu.VMEM((ct, slots), F32),
                        pltpu.VMEM((ct, n, n), BF16)],
        compiler_params=pltpu.CompilerParams(
            dimension_semantics=("arbitrary", "arbitrary"), vmem_limit_bytes=VMEM_LIMIT),
        name="peer_fused",
    )(q, keys_padded, h2, u_tab, v_tab, x, mod_blocks)


RW_HEADS = RW_WIDTH // RW_HEAD_DIM
ROT_HALF = HEAD_DIM // 2
ROPE_THETA = 10000.0
GRID_W = 64


def _split(z, sizes):
    parts, start = [], 0
    for s in sizes:
        parts.append(z[..., start:start + s])
        start += s
    return parts


def _bd_from_states(s):
    a = jnp.swapaxes(s, -1, -2)
    n, d, h, m, _ = a.shape
    a = a.reshape(n, d, h // 2, 2, m, m)
    z = jnp.zeros_like(a[:, :, :, 0])
    top = jnp.concatenate([a[:, :, :, 0], z], axis=-1)
    bot = jnp.concatenate([z, a[:, :, :, 1]], axis=-1)
    return jnp.concatenate([top, bot], axis=-2)


def _states_from_bd(bd):
    n, d, p, _, _ = bd.shape
    m = RW_HEAD_DIM
    a = jnp.stack([bd[:, :, :, :m, :m], bd[:, :, :, m:, m:]], axis=3).reshape(n, d, 2 * p, m, m)
    return jnp.swapaxes(a, -1, -2)


def _rope_tables(t):
    rows = t // GRID_W
    row, col = jnp.meshgrid(jnp.arange(rows), jnp.arange(GRID_W), indexing='ij')
    inv = ROPE_THETA ** (-jnp.arange(0, ROT_HALF, 2, dtype=F32) / ROT_HALF)
    ang_r = row.reshape(-1, 1).astype(F32) * inv
    ang_c = col.reshape(-1, 1).astype(F32) * inv
    cr, sr, cc, sc = jnp.cos(ang_r), jnp.sin(ang_r), jnp.cos(ang_c), jnp.sin(ang_c)
    cos = jnp.concatenate([cr, cr, cc, cc], axis=-1)
    sin = jnp.concatenate([sr, sr, sc, sc], axis=-1)
    return jnp.tile(cos, (1, LANES // HEAD_DIM)), jnp.tile(sin, (1, LANES // HEAD_DIM))


def _pad_rows(w, top):
    z = jnp.zeros_like(w)
    return jnp.concatenate([w, z] if top else [z, w], axis=0)


def kernel(x_prompt, x_sample, c, cache_k, cache_v, state_wkv, c_ctx, w_mod, b_mod, w_in, shift_mu_prev, shift_mu_next, rw_w0, rw_w2, rw_a0, rw_a2, rw_g2, rw_k_k, rw_k_a, rw_r_k, rw_gn_w, rw_gn_b, w_out_rwkv, conv_w, conv_b, conv_ln_w, conv_ln_b, w_out_conv, q_norm, k_norm, w_out_attn, w_o, peer_wq, peer_keys, peer_u, peer_v):
    nb, seq, d = x_prompt.shape
    db, dseq, _ = x_sample.shape
    depth = w_in.shape[0]
    n_ctx = nb * seq
    n_lat = db * dseq
    n_tok = n_ctx + n_lat
    x = jnp.concatenate([x_prompt.reshape(n_ctx, d), x_sample.reshape(n_lat, d)], axis=0)

    cond = jnp.concatenate([c_ctx[None], c, jnp.zeros((MOD_ROWS - 1 - db, d), F32)], axis=0)
    blk_cond = jnp.concatenate([jnp.zeros((n_ctx // ROW_BLOCK,), jnp.int32),
                                1 + jnp.arange(n_lat // ROW_BLOCK, dtype=jnp.int32) // (dseq // ROW_BLOCK)])
    n_blk = n_tok // ROW_BLOCK
    blk_row = jnp.arange(n_blk) * ROW_BLOCK
    blk_pos = jnp.where(blk_row < n_ctx, blk_row % seq, (blk_row - n_ctx) % dseq)
    blk_len = jnp.where(blk_row < n_ctx, seq, dseq)
    starts_seq = (blk_pos == 0)[:, None]
    ends_seq = (blk_pos + ROW_BLOCK == blk_len)[:, None]
    rope = _rope_tables(dseq)
    in_sizes = (RW_COLS, 2 * CONV_WIDTH, ATT_HEADS * HEAD_DIM, 2 * ATT_KV_HEADS * HEAD_DIM, 3 * d)
    past = cache_k.shape[2]

    ctx_k, ctx_v, ctx_s = [], [], []
    for l in range(depth):
        mod = modulation(cond, w_mod[l].astype(BF16), b_mod[l][None])
        mod = mod.reshape(MOD_ROWS, 6, d)[blk_cond]
        mod_blocks = jnp.concatenate([mod, jnp.zeros((mod.shape[0], MOD_ROWS - 6, d), F32)], axis=1)

        z_rw, z_cv, z_q, z_kv, z_gate = in_projection(x, mod_blocks, _split(w_in[l].astype(BF16), in_sizes))

        zb = z_rw.reshape(n_blk, ROW_BLOCK, RW_COLS)
        halo_prev = jnp.where(starts_seq, 0.0, jnp.roll(zb[:, -1], 1, axis=0))[:, None]
        halo_next = jnp.where(ends_seq, 0.0, jnp.roll(zb[:, 0], -1, axis=0))[:, None]
        mu = jnp.stack([shift_mu_prev[l], shift_mu_next[l]])
        vecs = jnp.stack([rw_k_k[l], rw_k_a[l], rw_r_k[l], rw_w0[l, 0], rw_w0[l, 1], rw_a0[l, 0],
                          rw_a0[l, 1], jnp.zeros((RW_WIDTH,), F32)])
        w2_pad = jnp.stack([_pad_rows(rw_w2[l, 0], True), _pad_rows(rw_w2[l, 1], False)]).astype(BF16)
        a2_pad = jnp.stack([_pad_rows(rw_a2[l, 0], True), _pad_rows(rw_a2[l, 1], False)]).astype(BF16)
        r, v, kk, lw, aa, kd, bonus, gate = rwkv_prep(z_rw, halo_prev, halo_next, mu, vecs, w2_pad, a2_pad,
                                                      rw_g2[l].astype(BF16))
        s0_ctx = jnp.zeros((nb, 2, RW_HEADS // 2, LANES, LANES), F32)
        of_c, ob_c, sfin = rwkv_scan(r, v, kk, lw, aa, kd, s0_ctx, 0, nb, seq)
        of_l, ob_l, _ = rwkv_scan(r, v, kk, lw, aa, kd, _bd_from_states(state_wkv[:, l]), n_ctx, db, dseq)
        ctx_s.append(_states_from_bd(sfin))

        cvec = jnp.stack([conv_b[l], conv_ln_w[l], conv_ln_b[l]] + [jnp.zeros((CONV_WIDTH,), F32)] * 5)
        y_cv = (conv_module(z_cv, conv_w[l], cvec, 0, nb, seq),
                conv_module(z_cv, conv_w[l], cvec, n_ctx, db, dseq))

        nq = jnp.tile(q_norm[l], LANES // HEAD_DIM)[None]
        nk = jnp.tile(k_norm[l], LANES // HEAD_DIM)[None]
        y_c, k_ctx = attention(z_q, z_kv, nq, nk, 0, nb, seq)
        y_l = attention(z_q, z_kv, nq, nk, n_ctx, db, dseq, rope=rope,
                        cache=(cache_k[:, l].reshape(db * past, LANES), cache_v[:, l].reshape(db * past, LANES)))
        ctx_k.append(k_ctx.reshape(nb, seq, ATT_KV_HEADS, HEAD_DIM))
        ctx_v.append(z_kv[:n_ctx, LANES:].reshape(nb, seq, ATT_KV_HEADS, HEAD_DIM))

        x, h2, pq = merge_project(x, mod_blocks, bonus, gate, z_gate,
                                  (of_c, of_l), (ob_c, ob_l), y_cv, (y_c, y_l),
                                  jnp.stack([rw_gn_w[l], rw_gn_b[l]]),
                                  w_out_rwkv[l].astype(BF16), w_out_conv[l].astype(BF16),
                                  w_out_attn[l].astype(BF16), w_o[l].astype(BF16), peer_wq[l].astype(BF16))

        kz = jnp.zeros_like(peer_keys[l][:, 0])
        keys_padded = jnp.concatenate([jnp.concatenate([peer_keys[l][:, 0], kz], axis=-1),
                                       jnp.concatenate([kz, peer_keys[l][:, 1]], axis=-1)], axis=1)
        x = peer_fused(pq, keys_padded, h2, peer_u[l].astype(BF16), peer_v[l].astype(BF16), x, mod_blocks)

    new_cache_k = jnp.stack(ctx_k, axis=1)
    new_cache_v = jnp.stack(ctx_v, axis=1)
    new_state = jnp.stack(ctx_s, axis=1)
    return (x[:n_ctx].reshape(nb, seq, d), x[n_ctx:].reshape(db, dseq, d),
            new_cache_k, new_cache_v, new_state)
```

```python
import functools
import math

import jax
import jax.numpy as jnp
from jax import lax
from jax.experimental import pallas as pl
from jax.experimental.pallas import tpu as pltpu

F32 = jnp.float32
BF16 = jnp.bfloat16

LANES = 128
SUBLANES = 8
SCAN_CHUNK = 64
SCAN_SUB = 16
RW_HEAD_DIM = 64
HEADS_PER_PAIR = LANES // RW_HEAD_DIM
VMEM_LIMIT = 56 * 1024 * 1024


def _dot(a, b):
    return jnp.dot(a.astype(BF16), b.astype(BF16), preferred_element_type=F32)


def _dot_nt(a, b):
    return lax.dot_general(a.astype(BF16), b.astype(BF16), (((1,), (1,)), ((), ())),
                           preferred_element_type=F32)


def _dot_tn(a, b):
    return lax.dot_general(a.astype(BF16), b.astype(BF16), (((0,), (0,)), ((), ())),
                           preferred_element_type=F32)


def _dot3(a, b):
    a_hi = a.astype(BF16)
    b_hi = b.astype(BF16)
    a_lo = (a - a_hi.astype(F32)).astype(BF16)
    b_lo = (b - b_hi.astype(F32)).astype(BF16)
    d = functools.partial(jnp.dot, preferred_element_type=F32)
    return d(a_hi, b_hi) + (d(a_hi, b_lo) + d(a_lo, b_hi))


def _scan_masks(reverse):
    C = SCAN_CHUNK
    P = HEADS_PER_PAIR * C
    row = lax.broadcasted_iota(jnp.int32, (P, LANES), 0)
    col = lax.broadcasted_iota(jnp.int32, (P, LANES), 1)
    tr = row % C
    tc = col % C
    t_i = lax.broadcasted_iota(jnp.int32, (C, C), 0)
    s_i = lax.broadcasted_iota(jnp.int32, (C, C), 1)
    return dict(
        same_head=(row // C) == (col // RW_HEAD_DIM),
        incl=(tr >= tc) if not reverse else (tr <= tc),
        strict=(tr > tc) if not reverse else (tr < tc),
        diag_blk=(tr // SCAN_SUB) == (tc // SCAN_SUB),
        eye=row == col,
        cum_mat=jnp.where((t_i >= s_i) if not reverse else (t_i <= s_i), 1.0, 0.0).astype(F32))


def _scan_chunk(chains, states):
    C = SCAN_CHUNK
    n = range(len(chains))
    mk = [ch[6] for ch in chains]
    cum = [ch[8] for ch in chains]
    pre = []
    for i in n:
        r, lw, k, v, kk, a, m, reverse, _ = chains[i]
        g = jnp.exp(cum[i])
        g_prev = jnp.exp(cum[i] - lw)
        g_inv = jnp.exp(-cum[i])
        g_end = jnp.exp(cum[i][C - 1:C, :] if not reverse else cum[i][0:1, :])
        bt = kk * a * g_inv
        kt = k * g_inv
        stack = lambda x, m=m: jnp.where(m["same_head"], jnp.concatenate([x, x], axis=0), 0.0)
        pre.append(dict(kap=stack(kk * g_prev), rt=stack(r * g), bt=stack(bt), kt=stack(kt),
                        v=stack(v), bh=stack(bt * g_end), kh=stack(kt * g_end), g_end=g_end))
    P = HEADS_PER_PAIR * C
    tri = [_dot_nt(jnp.concatenate([pre[i]["kap"], pre[i]["rt"]], axis=0),
                   jnp.concatenate([pre[i]["bt"], pre[i]["kt"]], axis=0)) for i in n]
    lb = [jnp.where(mk[i]["strict"], tri[i][:P, :P], 0.0) for i in n]
    lk = [jnp.where(mk[i]["strict"], tri[i][:P, P:], 0.0) for i in n]
    pb = [jnp.where(mk[i]["incl"], tri[i][P:, :P], 0.0) for i in n]
    pk = [jnp.where(mk[i]["incl"], tri[i][P:, P:], 0.0) for i in n]
    lkv = [_dot(lk[i], pre[i]["v"]) for i in n]

    d = [jnp.where(mk[i]["diag_blk"], lb[i], 0.0) for i in n]
    x = [jnp.concatenate([lb[i] - d[i], pre[i]["kap"], lkv[i]], axis=1) for i in n]
    x = [x[i] - _dot(d[i], x[i]) for i in n]
    p = [_dot(d[i], d[i]) for i in n]
    x = [x[i] + _dot(p[i], x[i]) for i in n]
    p = [_dot(p[i], p[i]) for i in n]
    x = [x[i] + _dot(p[i], x[i]) for i in n]
    p = [_dot(p[i], p[i]) for i in n]
    x = [x[i] + _dot(p[i], x[i]) for i in n]
    e = [x[i][:, :LANES] for i in n]
    rhs = [x[i][:, LANES:] for i in n]
    e2 = [_dot(e[i], e[i]) for i in n]
    t = [rhs[i] + _dot(e2[i], rhs[i]) for i in n]
    gy = [t[i] - _dot(e[i], t[i]) for i in n]

    m_mat = [jnp.where(mk[i]["eye"], pre[i]["g_end"], 0.0) - _dot_tn(pre[i]["bh"], gy[i][:, :LANES])
             for i in n]
    n_mat = [_dot_tn(pre[i]["kh"], pre[i]["v"]) - _dot_tn(pre[i]["bh"], gy[i][:, LANES:]) for i in n]
    pbgy = [_dot(pb[i], gy[i]) for i in n]
    q = [pre[i]["rt"] - pbgy[i][:, :LANES] for i in n]
    z = [_dot(pk[i], pre[i]["v"]) - pbgy[i][:, LANES:] for i in n]
    o_st = [_dot(q[i], states[i]) + z[i] for i in n]
    new = [_dot3(m_mat[i], states[i]) + n_mat[i] for i in n]
    return [(o_st[i][:C] + o_st[i][C:], new[i]) for i in n]


def _scan_kernel(rf, vf, kkf, lwf, af, kf, rb, vb, kkb, lwb, ab, kb, s0, of, ob, sfin, state):
    c = pl.program_id(1)
    n_pairs = rf.shape[-1] // LANES

    @pl.when(c == 0)
    def _():
        state[...] = s0[0]

    chains, states, outs = [], [], []
    for d, (r_, v_, kk_, lw_, a_, k_, o_) in enumerate(((rf, vf, kkf, lwf, af, kf, of),
                                                        (rb, vb, kkb, lwb, ab, kb, ob))):
        masks = _scan_masks(reverse=(d == 1))
        cum = _dot3(masks["cum_mat"], lw_[...])
        for p in range(n_pairs):
            sl = slice(p * LANES, (p + 1) * LANES)
            chains.append((r_[:, sl], lw_[:, sl], k_[:, sl], v_[:, sl], kk_[:, sl], a_[:, sl],
                           masks, d == 1, cum[:, sl]))
            states.append(state[d, p])
            outs.append((o_, sl, d, p))
    for (o, new), (o_, sl, d, p) in zip(_scan_chunk(chains, states), outs):
        o_[:, sl] = o
        state[d, p] = new

    @pl.when(c == pl.num_programs(1) - 1)
    def _():
        sfin[0] = state[...]


def rwkv_scan(r, v, kk, lw, a, kd, s0_bd, row0, n_seq, seq_len):
    width = r.shape[1]
    n_tok = n_seq * seq_len
    n_pairs = width // LANES
    n_chunk = seq_len // SCAN_CHUNK
    c0 = row0 // SCAN_CHUNK
    blk = (SCAN_CHUNK, width)
    dblk = (None, SCAN_CHUNK, width)
    fwd = lambda s, c: (s * n_chunk + c, 0)
    bwd = lambda s, c: (s * n_chunk + n_chunk - 1 - c, 0)
    fwd_in = lambda s, c: (c0 + s * n_chunk + c, 0)
    bwd_in = lambda s, c: (c0 + s * n_chunk + n_chunk - 1 - c, 0)
    fwd_d = lambda s, c: (0, c0 + s * n_chunk + c, 0)
    bwd_d = lambda s, c: (1, c0 + s * n_chunk + n_chunk - 1 - c, 0)
    sspec = pl.BlockSpec((1, 2, n_pairs, LANES, LANES), lambda s, c: (s, 0, 0, 0, 0))
    in_specs = ([pl.BlockSpec(blk, fwd_in)] * 3 + [pl.BlockSpec(dblk, fwd_d)] * 3
                + [pl.BlockSpec(blk, bwd_in)] * 3 + [pl.BlockSpec(dblk, bwd_d)] * 3 + [sspec])
    out_specs = [pl.BlockSpec(blk, fwd), pl.BlockSpec(blk, bwd), sspec]
    out_shape = [jax.ShapeDtypeStruct((n_tok, width), F32)] * 2 + [
        jax.ShapeDtypeStruct(s0_bd.shape, F32)]
    return pl.pallas_call(
        _scan_kernel,
        grid=(n_seq, n_chunk),
        in_specs=in_specs,
        out_specs=out_specs,
        out_shape=out_shape,
        scratch_shapes=[pltpu.VMEM((2, n_pairs, LANES, LANES), F32)],
        compiler_params=pltpu.CompilerParams(
            dimension_semantics=("parallel", "arbitrary"), vmem_limit_bytes=VMEM_LIMIT),
        name="rwkv_scan",
    )(r, v, kk, lw, a, kd, r, v, kk, lw, a, kd, s0_bd)


MOD_ROWS = 8
ROW_BLOCK = 256
RMS_EPS = 1e-6
LN_EPS = 1e-5
RW_GN_EPS = 64e-5


def _mod_kernel(c_ref, w_ref, b_ref, o_ref):
    c = c_ref[...]
    s = c * jax.nn.sigmoid(c)
    o_ref[...] = _dot(s, w_ref[...]) + b_ref[...]


def modulation(cond, w_mod, b_mod):
    n, d = cond.shape
    cols = w_mod.shape[1]
    blk = 1024
    return pl.pallas_call(
        _mod_kernel,
        grid=(cols // blk,),
        in_specs=[pl.BlockSpec((n, d), lambda j: (0, 0)),
                  pl.BlockSpec((d, blk), lambda j: (0, j)),
                  pl.BlockSpec((1, blk), lambda j: (0, j))],
        out_specs=pl.BlockSpec((n, blk), lambda j: (0, j)),
        out_shape=jax.ShapeDtypeStruct((n, cols), F32),
        name="modulation",
    )(cond, w_mod, b_mod)


def _modulate(x, shift, scale):
    ms = jnp.mean(x * x, axis=-1, keepdims=True)
    return x * lax.rsqrt(ms + RMS_EPS) * (1.0 + scale) + shift


def _inproj_kernel(x_ref, mod_ref, *refs):
    n = len(refs) // 2
    h = _modulate(x_ref[...], mod_ref[0, 0:1, :], mod_ref[0, 1:2, :]).astype(BF16)
    for w_ref, o_ref in zip(refs[:n], refs[n:]):
        o_ref[...] = jnp.dot(h, w_ref[...], preferred_element_type=F32)


def in_projection(x, mod_blocks, weights):
    n_tok, d = x.shape
    row = lambda i: (i, 0)
    const = lambda i: (0, 0)
    return pl.pallas_call(
        _inproj_kernel,
        grid=(n_tok // ROW_BLOCK,),
        in_specs=[pl.BlockSpec((ROW_BLOCK, d), row),
                  pl.BlockSpec((1, MOD_ROWS, d), lambda i: (i, 0, 0))]
                 + [pl.BlockSpec(w.shape, const, pipeline_mode=pl.Buffered(1)) for w in weights],
        out_specs=[pl.BlockSpec((ROW_BLOCK, w.shape[1]), row) for w in weights],
        out_shape=[jax.ShapeDtypeStruct((n_tok, w.shape[1]), F32) for w in weights],
        compiler_params=pltpu.CompilerParams(
            dimension_semantics=("parallel",), vmem_limit_bytes=VMEM_LIMIT),
        name="in_projection",
    )(x, mod_blocks, *weights)


HEAD_DIM = 64


def _group_ones(width):
    r = lax.broadcasted_iota(jnp.int32, (width, width), 0) // HEAD_DIM
    c = lax.broadcasted_iota(jnp.int32, (width, width), 1) // HEAD_DIM
    return jnp.where(r == c, 1.0, 0.0).astype(BF16)


def _group_sum(x, ones):
    hi = x.astype(BF16)
    lo = (x - hi.astype(F32)).astype(BF16)
    d = functools.partial(jnp.dot, preferred_element_type=F32)
    return d(hi, ones) + d(lo, ones)


RW_WIDTH = 512
RW_COLS = 3 * RW_WIDTH + 3 * LANES
RW_VEC_ROWS = 8


def _rwkv_prep_kernel(z_ref, hp_ref, hn_ref, mu_ref, vec_ref, w2_ref, a2_ref, g2_ref,
                      r_ref, v_ref, kk_ref, lw_ref, a_ref, kd_ref, bonus_ref, gate_ref):
    z = z_ref[...]
    rb = z.shape[0]
    row = lax.broadcasted_iota(jnp.int32, (rb, 1), 0)
    z_prev = jnp.where(row == 0, hp_ref[0], pltpu.roll(z, 1, 0))
    z_next = jnp.where(row == rb - 1, hn_ref[0], pltpu.roll(z, rb - 1, 0))
    zs = z + mu_ref[0:1, :] * (z_prev - z) + mu_ref[1:2, :] * (z_next - z)
    w = RW_WIDTH
    r, k, v = zs[:, :w], zs[:, w:2 * w], zs[:, 2 * w:3 * w]
    zw = zs[:, 3 * w:3 * w + LANES]
    za = zs[:, 3 * w + LANES:3 * w + 2 * LANES]
    zg = zs[:, 3 * w + 2 * LANES:]
    ones = _group_ones(w)
    kk = k * vec_ref[0:1, :]
    kk = kk * lax.rsqrt(_group_sum(kk * kk, ones) + 1e-12)
    r_ref[...] = r
    v_ref[...] = v
    kk_ref[...] = kk
    tz = jnp.tanh(zw)
    for d in range(2):
        w_raw = vec_ref[3 + d:4 + d, :] + _dot(tz, w2_ref[d])
        lw_ref[d] = -math.exp(-0.5) * jax.nn.sigmoid(w_raw)
        a = jax.nn.sigmoid(vec_ref[5 + d:6 + d, :] + _dot(za, a2_ref[d]))
        a_ref[d] = a
        kd_ref[d] = k * (1.0 + (a - 1.0) * vec_ref[1:2, :])
    bonus_ref[...] = _group_sum(r * k * vec_ref[2:3, :], ones) * v
    gate_ref[...] = _dot(jax.nn.sigmoid(zg), g2_ref[...])


def rwkv_prep(z_rw, halo_prev, halo_next, mu, vecs, w2_pad, a2_pad, g2):
    n_tok = z_rw.shape[0]
    w = RW_WIDTH
    row = lambda i: (i, 0)
    const2 = lambda i: (0, 0)
    const3 = lambda i: (0, 0, 0)
    halo = pl.BlockSpec((1, 1, RW_COLS), lambda i: (i, 0, 0))
    one = pl.BlockSpec((ROW_BLOCK, w), row)
    two = pl.BlockSpec((2, ROW_BLOCK, w), lambda i: (0, i, 0))
    s1 = jax.ShapeDtypeStruct((n_tok, w), F32)
    s2 = jax.ShapeDtypeStruct((2, n_tok, w), F32)
    return pl.pallas_call(
        _rwkv_prep_kernel,
        grid=(n_tok // ROW_BLOCK,),
        in_specs=[pl.BlockSpec((ROW_BLOCK, RW_COLS), row), halo, halo,
                  pl.BlockSpec(mu.shape, const2), pl.BlockSpec(vecs.shape, const2),
                  pl.BlockSpec(w2_pad.shape, const3), pl.BlockSpec(a2_pad.shape, const3),
                  pl.BlockSpec(g2.shape, const2)],
        out_specs=[one, one, one, two, two, two, one, one],
        out_shape=[s1, s1, s1, s2, s2, s2, s1, s1],
        compiler_params=pltpu.CompilerParams(
            dimension_semantics=("parallel",), vmem_limit_bytes=VMEM_LIMIT),
        name="rwkv_prep",
    )(z_rw, halo_prev, halo_next, mu, vecs, w2_pad, a2_pad, g2)


CONV_WIDTH = 512
CONV_KERNEL = 31
CONV_PAD = 16
CONV_ROWS = 64


def _conv_kernel(z_ref, w_ref, vec_ref, o_ref, pad_ref):
    t = z_ref.shape[0]
    cw = CONV_WIDTH
    half = CONV_KERNEL // 2
    zeros = jnp.zeros((CONV_PAD, cw), F32)
    pad_ref[0:CONV_PAD, :] = zeros
    pad_ref[CONV_PAD + t:CONV_PAD + t + CONV_PAD, :] = zeros
    pad_ref[CONV_PAD:CONV_PAD + t, :] = z_ref[:, :cw] * jax.nn.sigmoid(z_ref[:, cw:])

    def chunk(i, carry):
        base = pl.multiple_of(i * CONV_ROWS, CONV_ROWS)
        window = pad_ref[pl.ds(base, CONV_ROWS + 2 * CONV_PAD), :]
        span = CONV_ROWS + 2 * CONV_PAD - SUBLANES
        shifted = [window[rho:rho + span] for rho in range(SUBLANES)]
        acc = jnp.zeros((CONV_ROWS, cw), F32)
        for j in range(CONV_KERNEL):
            lo = CONV_PAD - half + j
            tile = (lo // SUBLANES) * SUBLANES
            acc = acc + w_ref[j:j + 1, :] * shifted[lo % SUBLANES][tile:tile + CONV_ROWS]
        u = acc + vec_ref[0:1, :]
        mu = jnp.mean(u, axis=-1, keepdims=True)
        cen = u - mu
        var = jnp.mean(cen * cen, axis=-1, keepdims=True)
        y = cen * lax.rsqrt(var + LN_EPS) * vec_ref[1:2, :] + vec_ref[2:3, :]
        o_ref[pl.ds(base, CONV_ROWS), :] = y * jax.nn.sigmoid(y)
        return carry

    lax.fori_loop(0, t // CONV_ROWS, chunk, 0)


def conv_module(z_cv, conv_w, vecs, row0, n_seq, seq_len):
    cw = CONV_WIDTH
    blk0 = row0 // seq_len
    return pl.pallas_call(
        _conv_kernel,
        grid=(n_seq,),
        in_specs=[pl.BlockSpec((seq_len, 2 * cw), lambda i: (blk0 + i, 0)),
                  pl.BlockSpec(conv_w.shape, lambda i: (0, 0)),
                  pl.BlockSpec(vecs.shape, lambda i: (0, 0))],
        out_specs=pl.BlockSpec((seq_len, cw), lambda i: (i, 0)),
        out_shape=jax.ShapeDtypeStruct((n_seq * seq_len, cw), F32),
        scratch_shapes=[pltpu.VMEM((seq_len + 2 * CONV_PAD, cw), F32)],
        compiler_params=pltpu.CompilerParams(
            dimension_semantics=("parallel",), vmem_limit_bytes=VMEM_LIMIT),
        name="conv_module",
    )(z_cv, conv_w, vecs)


ATT_Q_BLOCK = 256
ATT_HEADS = 8
ATT_KV_HEADS = 2
ATT_GROUP = ATT_HEADS // ATT_KV_HEADS


def _rot_partner():
    r = lax.broadcasted_iota(jnp.int32, (LANES, LANES), 0)
    c = lax.broadcasted_iota(jnp.int32, (LANES, LANES), 1)
    lo = (r % 32) < 16
    return jnp.where((c == r + 16) & lo, 1.0, jnp.where((c == r - 16) & ~lo, -1.0, 0.0)).astype(BF16)


def _attn_kernel(*refs, rotary):
    if rotary:
        (zq_ref, zkv_ref, nq_ref, nk_ref, cosq_ref, sinq_ref, cosk_ref, sink_ref, ck_ref, cv_ref,
         o_ref) = refs
    else:
        zq_ref, zkv_ref, nq_ref, nk_ref, o_ref, kn_ref = refs
    tq = zq_ref.shape[0]
    ones = _group_ones(LANES)
    lane = lax.broadcasted_iota(jnp.int32, (1, LANES), 1)
    first = lane < HEAD_DIM
    row_head = lax.broadcasted_iota(jnp.int32, (2 * tq, LANES), 0) // tq
    own = row_head == (lax.broadcasted_iota(jnp.int32, (2 * tq, LANES), 1) // HEAD_DIM)

    def norm(x, g):
        return x * lax.rsqrt(_group_sum(x * x, ones) * (1.0 / HEAD_DIM) + RMS_EPS) * g

    def rope(x, cos, sin):
        return x * cos + jnp.dot(x.astype(BF16), _rot_partner(), preferred_element_type=F32) * sin

    def dup(x):
        sw = pltpu.roll(x, HEAD_DIM, 1)
        return jnp.where(first, x, sw), jnp.where(first, sw, x)

    k = norm(zkv_ref[:, :LANES], nk_ref[...])
    v = zkv_ref[:, LANES:]
    if rotary:
        k = rope(k, cosk_ref[...], sink_ref[...])
        ck, cv = dup(ck_ref[...]), dup(cv_ref[...])
    else:
        kn_ref[...] = k
    kd, vd = dup(k), dup(v)
    for pair in range(ATT_HEADS // 2):
        g = pair // (ATT_GROUP // 2)
        q = norm(zq_ref[:, pair * LANES:(pair + 1) * LANES], nq_ref[...])
        if rotary:
            q = rope(q, cosq_ref[...], sinq_ref[...])
        q = q * (HEAD_DIM ** -0.5)
        qs = jnp.where(own, jnp.concatenate([q, q], axis=0), 0.0)
        s = _dot_nt(qs, kd[g])
        m = jnp.max(s, axis=-1, keepdims=True)
        if rotary:
            s2 = _dot_nt(qs, ck[g])
            m = jnp.maximum(m, jnp.max(s2, axis=-1, keepdims=True))
            p2 = jnp.exp(s2 - m)
        p = jnp.exp(s - m)
        den = jnp.sum(p, axis=-1, keepdims=True)
        o = _dot(p, vd[g])
        if rotary:
            den = den + jnp.sum(p2, axis=-1, keepdims=True)
            o = o + _dot(p2, cv[g])
        o = o / den
        o_ref[:, pair * LANES:(pair + 1) * LANES] = jnp.where(first, o[:tq], o[tq:])


def attention(z_q, z_kv, nq, nk, row0, n_seq, seq_len, rope=None, cache=None):
    qb = min(ATT_Q_BLOCK, seq_len)
    nqb = seq_len // qb
    qblk0 = row0 // qb
    sblk0 = row0 // seq_len
    wq = z_q.shape[1]
    in_specs = [pl.BlockSpec((qb, wq), lambda b, t: (qblk0 + b * nqb + t, 0)),
                pl.BlockSpec((seq_len, 2 * LANES), lambda b, t: (sblk0 + b, 0)),
                pl.BlockSpec((1, LANES), lambda b, t: (0, 0)),
                pl.BlockSpec((1, LANES), lambda b, t: (0, 0))]
    args = [z_q, z_kv, nq, nk]
    out_q = pl.BlockSpec((qb, wq), lambda b, t: (b * nqb + t, 0))
    y_shape = jax.ShapeDtypeStruct((n_seq * seq_len, wq), F32)
    if rope is None:
        out_specs = [out_q, pl.BlockSpec((seq_len, LANES), lambda b, t: (b, 0))]
        out_shape = [y_shape, jax.ShapeDtypeStruct((n_seq * seq_len, LANES), F32)]
    else:
        past = cache[0].shape[0] // n_seq
        in_specs += [pl.BlockSpec((qb, LANES), lambda b, t: (t, 0))] * 2
        in_specs += [pl.BlockSpec((seq_len, LANES), lambda b, t: (0, 0))] * 2
        in_specs += [pl.BlockSpec((past, LANES), lambda b, t: (b, 0))] * 2
        args += [rope[0], rope[1], rope[0], rope[1], cache[0], cache[1]]
        out_specs = out_q
        out_shape = y_shape
    return pl.pallas_call(
        functools.partial(_attn_kernel, rotary=rope is not None),
        grid=(n_seq, nqb),
        in_specs=in_specs,
        out_specs=out_specs,
        out_shape=out_shape,
        compiler_params=pltpu.CompilerParams(
            dimension_semantics=("parallel", "arbitrary"), vmem_limit_bytes=VMEM_LIMIT),
        name="attention",
    )(*args)


def _merge_kernel(x_ref, mod_ref, bonus_ref, gate_ref, zg_ref,
                  ofc_ref, ofl_ref, obc_ref, obl_ref, cvc_ref, cvl_ref, atc_ref, atl_ref,
                  gn_ref, wr_ref, wc_ref, wa_ref, wo_ref, wq_ref, xo_ref, h2_ref, q_ref, *, n_ctx_blocks):
    d = x_ref.shape[1]
    zg = zg_ref[...]
    is_ctx = pl.program_id(0) < n_ctx_blocks
    pick = lambda c_ref, l_ref: jnp.where(is_ctx, c_ref[...], l_ref[...])
    ones = _group_ones(ofc_ref.shape[1])
    o = pick(ofc_ref, ofl_ref) + pick(obc_ref, obl_ref)
    cen = o - _group_sum(o, ones) * (1.0 / HEAD_DIM)
    var = _group_sum(cen * cen, ones) * (1.0 / HEAD_DIM)
    o = cen * lax.rsqrt(var + RW_GN_EPS) * gn_ref[0:1, :] + gn_ref[1:2, :]
    y_rw = (o + bonus_ref[...]) * gate_ref[...]
    merged = (jax.nn.sigmoid(zg[:, :d]) * _dot(y_rw, wr_ref[...])
              + jax.nn.sigmoid(zg[:, d:2 * d]) * _dot(pick(cvc_ref, cvl_ref), wc_ref[...])
              + jax.nn.sigmoid(zg[:, 2 * d:]) * _dot(pick(atc_ref, atl_ref), wa_ref[...]))
    x = x_ref[...] + mod_ref[0, 2:3, :] * _dot(merged, wo_ref[...])
    xo_ref[...] = x
    h2 = _modulate(x, mod_ref[0, 3:4, :], mod_ref[0, 4:5, :]).astype(BF16)
    h2_ref[...] = h2
    q_ref[...] = jnp.dot(h2, wq_ref[...], preferred_element_type=F32)


def merge_project(x, mod_blocks, bonus, gate, z_gate, o_f, o_b, y_cv, y_at, gn, w_r, w_c, w_a, w_o, w_q):
    n_tok, d = x.shape
    nc = o_f[0].shape[0] // ROW_BLOCK
    row = lambda i: (i, 0)
    const = lambda i: (0, 0)
    ctx_row = lambda i: (jnp.minimum(i, nc - 1), 0)
    lat_row = lambda i: (jnp.maximum(i - nc, 0), 0)
    acts = [bonus, gate, z_gate]
    weights = [gn, w_r, w_c, w_a, w_o, w_q]
    pair_specs, pair_args = [], []
    for c_arr, l_arr in (o_f, o_b, y_cv, y_at):
        pair_specs += [pl.BlockSpec((ROW_BLOCK, c_arr.shape[1]), ctx_row),
                       pl.BlockSpec((ROW_BLOCK, l_arr.shape[1]), lat_row)]
        pair_args += [c_arr, l_arr]
    return pl.pallas_call(
        functools.partial(_merge_kernel, n_ctx_blocks=nc),
        grid=(n_tok // ROW_BLOCK,),
        in_specs=[pl.BlockSpec((ROW_BLOCK, d), row),
                  pl.BlockSpec((1, MOD_ROWS, d), lambda i: (i, 0, 0))]
                 + [pl.BlockSpec((ROW_BLOCK, a.shape[1]), row) for a in acts]
                 + pair_specs
                 + [pl.BlockSpec(w.shape, const) for w in weights],
        out_specs=[pl.BlockSpec((ROW_BLOCK, d), row), pl.BlockSpec((ROW_BLOCK, d), row),
                   pl.BlockSpec((ROW_BLOCK, w_q.shape[1]), row)],
        out_shape=[jax.ShapeDtypeStruct((n_tok, d), F32), jax.ShapeDtypeStruct((n_tok, d), BF16),
                   jax.ShapeDtypeStruct((n_tok, w_q.shape[1]), F32)],
        compiler_params=pltpu.CompilerParams(
            dimension_semantics=("parallel",), vmem_limit_bytes=VMEM_LIMIT),
        name="merge_project",
    )(x, mod_blocks, *acts, *pair_args, *weights)


PEER_HEADS = 8
PEER_N_KEYS = 128
PEER_HALF = 64
PEER_TOPK = 16
PEER_SEL_BLOCK = 128
PEER_LOOP_UNROLL = 16
PEER_RELAYOUT_TOKENS = 32
PEER_RELAYOUT_KEYS = 8
NEG = -1e30


def _dot3_nt(a, b):
    a_hi = a.astype(BF16)
    b_hi = b.astype(BF16)
    a_lo = (a - a_hi.astype(F32)).astype(BF16)
    b_lo = (b - b_hi.astype(F32)).astype(BF16)
    d = lambda x, y: lax.dot_general(x, y, (((1,), (1,)), ((), ())), preferred_element_type=F32)
    return d(a_hi, b_hi) + (d(a_hi, b_lo) + d(a_lo, b_hi))


def _extract_max(s, code):
    m = jnp.max(s, axis=0, keepdims=True)
    pos = jnp.min(jnp.where(s == m, code, 1e9), axis=0, keepdims=True)
    return m, pos, jnp.where(code == pos, NEG, s)


def _pair_candidates(a0, a1):
    k = PEER_TOPK
    sub = lax.broadcasted_iota(jnp.int32, (k, a0.shape[1]), 0).astype(F32)
    sub8 = sub[:8]
    vals = [a0[0:1] + a1]
    codes = [sub]
    for r0 in range(1, 8):
        limit = k // (r0 + 1)
        vals.append(jnp.where(sub8 < limit, a0[r0:r0 + 1] + a1[:8], NEG))
        codes.append(sub8 + float(r0 * k))
    vals.append(a0[8:] + a1[0:1])
    codes.append((sub8 + 8.0) * float(k))
    return jnp.concatenate(vals, axis=0), jnp.concatenate(codes, axis=0)


def _peer_select_kernel(q_ref, keys_ref, w_ref, top_v, top_i, slot_i, slot_j, slot_g,
                        rows_i, rows_j, rows_g, w3_ref):
    k = PEER_TOPK
    ct = q_ref.shape[0]
    key_code = lax.broadcasted_iota(jnp.int32, (PEER_N_KEYS, ct), 0).astype(F32)
    rank = lax.broadcasted_iota(jnp.int32, (k, ct), 0).astype(F32)
    for h in range(PEER_HEADS):
        st = _dot3_nt(keys_ref[h], q_ref[:, h * LANES:(h + 1) * LANES])
        s = [st[:PEER_N_KEYS], st[PEER_N_KEYS:]]
        for it in range(k):
            for p in range(2):
                m, pos, s[p] = _extract_max(s[p], key_code)
                top_v[h, p, it:it + 1, :] = m
                top_i[h, p, it:it + 1, :] = pos
    for h0 in range(0, PEER_HEADS, 2):
        heads = (h0, h0 + 1)
        cand, code, best, z = {}, {}, {}, {}
        for h in heads:
            a0, a1 = top_v[h, 0], top_v[h, 1]
            cand[h], code[h] = _pair_candidates(a0, a1)
            best[h] = a0[0:1] + a1[0:1]
            z[h] = jnp.zeros_like(best[h])
        for it in range(k):
            for h in heads:
                m, pos, cand[h] = _extract_max(cand[h], code[h])
                r0 = jnp.floor(pos * (1.0 / k))
                r1 = pos - r0 * k
                e = jnp.exp(m - best[h])
                z[h] = z[h] + e
                slot = h * k + it
                slot_i[slot:slot + 1, :] = jnp.sum(jnp.where(rank == r0, top_i[h, 0], 0.0), axis=0,
                                                   keepdims=True)
                slot_j[slot:slot + 1, :] = jnp.sum(jnp.where(rank == r1, top_i[h, 1], 0.0), axis=0,
                                                   keepdims=True)
                slot_g[slot:slot + 1, :] = e
        for h in heads:
            slot_g[h * k:(h + 1) * k, :] = slot_g[h * k:(h + 1) * k, :] / z[h]

    rows_i[...] = slot_i[...].T
    rows_j[...] = slot_j[...].T
    rows_g[...] = slot_g[...].T
    n = PEER_N_KEYS
    sub = lax.broadcasted_iota(jnp.int32, (n, n), 0).astype(F32)

    def per_token(c, carry):
        a_t = jnp.where(sub == rows_i[pl.ds(c, 1), :], rows_g[pl.ds(c, 1), :], 0.0)
        b_t = jnp.where(sub == rows_j[pl.ds(c, 1), :], 1.0, 0.0)
        w3_ref[c] = _dot_nt(a_t, b_t)
        return carry

    lax.fori_loop(0, ct, per_token, 0, unroll=PEER_LOOP_UNROLL)

    tg, ig = PEER_RELAYOUT_TOKENS, PEER_RELAYOUT_KEYS
    r_out = lax.broadcasted_iota(jnp.int32, (tg * ig, tg * ig), 0)
    r_in = lax.broadcasted_iota(jnp.int32, (tg * ig, tg * ig), 1)
    perm = jnp.where(r_in == (r_out % tg) * ig + r_out // tg, 1.0, 0.0).astype(BF16)

    for ib in range(n // ig):
        for t in range(ct // tg):
            z = w3_ref[t * tg:(t + 1) * tg, ib * ig:(ib + 1) * ig, :]
            y = jnp.dot(perm, z.reshape(tg * ig, n).astype(BF16), preferred_element_type=F32)
            for kk in range(ig):
                col = (ib * ig + kk) * n
                w_ref[t * tg:(t + 1) * tg, col:col + n] = y[kk * tg:(kk + 1) * tg].astype(BF16)


def peer_select(q, keys_padded):
    n_tok, qd = q.shape
    ct = PEER_SEL_BLOCK
    n = PEER_N_KEYS
    slots = PEER_HEADS * PEER_TOPK
    return pl.pallas_call(
        _peer_select_kernel,
        grid=(n_tok // ct,),
        in_specs=[pl.BlockSpec((ct, qd), lambda i: (i, 0)),
                  pl.BlockSpec(keys_padded.shape, lambda i: (0, 0, 0))],
        out_specs=pl.BlockSpec((ct, n * n), lambda i: (i, 0)),
        out_shape=jax.ShapeDtypeStruct((n_tok, n * n), BF16),
        scratch_shapes=[pltpu.VMEM((PEER_HEADS, 2, PEER_TOPK, ct), F32),
                        pltpu.VMEM((PEER_HEADS, 2, PEER_TOPK, ct), F32),
                        pltpu.VMEM((slots, ct), F32), pltpu.VMEM((slots, ct), F32),
                        pltpu.VMEM((slots, ct), F32),
                        pltpu.VMEM((ct, slots), F32), pltpu.VMEM((ct, slots), F32),
                        pltpu.VMEM((ct, slots), F32),
                        pltpu.VMEM((ct, n, n), F32)],
        compiler_params=pltpu.CompilerParams(
            dimension_semantics=("parallel",), vmem_limit_bytes=VMEM_LIMIT),
        name="peer_select",
    )(q, keys_padded)


PEER_TOKEN_BLOCK = 1024
PEER_EXPERT_BLOCK = 1024


def _peer_kernel(h_ref, w_ref, u_ref, v_ref, x_ref, mod_ref, o_ref, acc_ref):
    e = pl.program_id(1)

    @pl.when(e == 0)
    def _():
        acc_ref[...] = jnp.zeros_like(acc_ref)

    s = _dot_nt(h_ref[...], u_ref[...])
    act = 0.5 * s * (1.0 + lax.erf(s * (2.0 ** -0.5)))
    acc_ref[...] += jnp.dot(act.astype(BF16) * w_ref[...], v_ref[...], preferred_element_type=F32)

    @pl.when(e == pl.num_programs(1) - 1)
    def _():
        for j in range(mod_ref.shape[0]):
            rows = slice(j * ROW_BLOCK, (j + 1) * ROW_BLOCK)
            o_ref[rows, :] = x_ref[rows, :] + mod_ref[j, 5:6, :] * acc_ref[rows, :]


def peer_dense(h2, w_sel, u_tab, v_tab, x, mod_blocks):
    n_tok, d = x.shape
    n_exp = u_tab.shape[0]
    tb, eb = PEER_TOKEN_BLOCK, PEER_EXPERT_BLOCK
    tok = lambda i, e: (i, 0)
    return pl.pallas_call(
        _peer_kernel,
        grid=(n_tok // tb, n_exp // eb),
        in_specs=[pl.BlockSpec((tb, d), tok),
                  pl.BlockSpec((tb, eb), lambda i, e: (i, e)),
                  pl.BlockSpec((eb, d), lambda i, e: (e, 0)),
                  pl.BlockSpec((eb, d), lambda i, e: (e, 0)),
                  pl.BlockSpec((tb, d), tok),
                  pl.BlockSpec((tb // ROW_BLOCK, MOD_ROWS, d), lambda i, e: (i, 0, 0))],
        out_specs=pl.BlockSpec((tb, d), tok),
        out_shape=jax.ShapeDtypeStruct((n_tok, d), F32),
        scratch_shapes=[pltpu.VMEM((tb, d), F32)],
        compiler_params=pltpu.CompilerParams(
            dimension_semantics=("parallel", "arbitrary"), vmem_limit_bytes=VMEM_LIMIT),
        name="peer_dense",
    )(h2, w_sel, u_tab, v_tab, x, mod_blocks)


FUSED_TOKEN_BLOCK = 256
FUSED_EXPERT_BLOCK = 2048
FUSED_PHASES = 4
FUSED_RELAYOUT = 16
STAGE1_SPLIT = 5
FUSED_DENSE_CHUNKS = 8


def _stage1_items(q_ref, keys_ref, row0, heads, top_v, top_i):
    ct = PEER_SEL_BLOCK
    key_code = lax.broadcasted_iota(jnp.int32, (PEER_N_KEYS, ct), 0).astype(F32)
    st = {}
    items = []

    def score(h):
        st["both"] = _dot3_nt(keys_ref[h], q_ref[pl.ds(row0, ct), h * LANES:(h + 1) * LANES])

    def start(p):
        st["s"] = st["both"][p * PEER_N_KEYS:(p + 1) * PEER_N_KEYS]

    def step(h, p, it):
        m, pos, st["s"] = _extract_max(st["s"], key_code)
        top_v[h, p, it:it + 1, :] = m
        top_i[h, p, it:it + 1, :] = pos

    for h in heads:
        items.append(functools.partial(score, h))
        for p in range(2):
            items.append(functools.partial(start, p))
            items += [functools.partial(step, h, p, it) for it in range(PEER_TOPK)]
    return items


def _stage2_items(top_v, top_i, slot_i, slot_j, slot_g, rows_i, rows_j, rows_g):
    k = PEER_TOPK
    ct = PEER_SEL_BLOCK
    rank = lax.broadcasted_iota(jnp.int32, (k, ct), 0).astype(F32)
    st = {}
    items = []

    def start(h):
        a0, a1 = top_v[h, 0], top_v[h, 1]
        st["cand"], st["code"] = _pair_candidates(a0, a1)
        st["best"] = a0[0:1] + a1[0:1]
        st["z"] = jnp.zeros_like(st["best"])

    def step(h, it):
        m, pos, st["cand"] = _extract_max(st["cand"], st["code"])
        r0 = jnp.floor(pos * (1.0 / k))
        r1 = pos - r0 * k
        e = jnp.exp(m - st["best"])
        st["z"] = st["z"] + e
        slot = h * k + it
        slot_i[slot:slot + 1, :] = jnp.sum(jnp.where(rank == r0, top_i[h, 0], 0.0), axis=0, keepdims=True)
        slot_j[slot:slot + 1, :] = jnp.sum(jnp.where(rank == r1, top_i[h, 1], 0.0), axis=0, keepdims=True)
        slot_g[slot:slot + 1, :] = e

    def finish(h):
        slot_g[h * k:(h + 1) * k, :] = slot_g[h * k:(h + 1) * k, :] / st["z"]

    def transpose():
        rows_i[...] = slot_i[...].T
        rows_j[...] = slot_j[...].T
        rows_g[...] = slot_g[...].T

    for h in range(PEER_HEADS):
        items.append(functools.partial(start, h))
        items += [functools.partial(step, h, it) for it in range(k)]
        items.append(functools.partial(finish, h))
    items.append(transpose)
    return items


def _assemble_items(rows_i, rows_j, rows_g, w3_ref):
    n = PEER_N_KEYS
    sub = lax.broadcasted_iota(jnp.int32, (n, n), 0).astype(F32)

    def token(c):
        a_t = jnp.where(sub == rows_i[c:c + 1, :], rows_g[c:c + 1, :], 0.0)
        b_t = jnp.where(sub == rows_j[c:c + 1, :], 1.0, 0.0)
        w3_ref[c] = _dot_nt(a_t, b_t).astype(BF16)

    return [functools.partial(token, c) for c in range(PEER_SEL_BLOCK)]


def _relayout_items(w3_ref, wbuf, buf, row0):
    n = PEER_N_KEYS
    g = FUSED_RELAYOUT
    r_out = lax.broadcasted_iota(jnp.int32, (g * g, g * g), 0)
    r_in = lax.broadcasted_iota(jnp.int32, (g * g, g * g), 1)
    perm = jnp.where(r_in == (r_out % g) * g + r_out // g, 1.0, 0.0).astype(BF16)

    def slab(ib, tt):
        z = w3_ref[tt * g:(tt + 1) * g, ib * g:(ib + 1) * g, :].reshape(g * g, n)
        y = jnp.dot(perm, z, preferred_element_type=F32)
        rows = pl.ds(pl.multiple_of(row0 + tt * g, g), g)
        for kk in range(g):
            col = (ib * g + kk) * n
            wbuf[buf, rows, col:col + n] = y[kk * g:(kk + 1) * g].astype(BF16)

    return [functools.partial(slab, ib, tt) for ib in range(n // g) for tt in range(PEER_SEL_BLOCK // g)]


def _dense_items(h_ref, u_ref, v_ref, wbuf, use, col0, acc_ref):
    cs = u_ref.shape[0] // FUSED_DENSE_CHUNKS
    st = {}

    def scores(c):
        st["s"] = _dot_nt(h_ref[...], u_ref[c * cs:(c + 1) * cs, :])

    def gate(c):
        s = st["s"]
        act = 0.5 * s * (1.0 + lax.erf(s * (2.0 ** -0.5)))
        w = wbuf[use, :, pl.ds(pl.multiple_of(col0 + c * cs, cs), cs)]
        st["g"] = act.astype(BF16) * w

    def accumulate(c):
        acc_ref[...] += jnp.dot(st["g"], v_ref[c * cs:(c + 1) * cs, :], preferred_element_type=F32)

    return [functools.partial(f, c) for c in range(FUSED_DENSE_CHUNKS) for f in (scores, gate, accumulate)]


def _interleave(a, b):
    done = 0
    for i, item in enumerate(a):
        item()
        upto = (i + 1) * len(b) // len(a)
        for other in b[done:upto]:
            other()
        done = upto


def _peer_fused_kernel(q_ref, keys_ref, h_ref, u_ref, v_ref, x_ref, mod_ref, o_ref,
                       acc_ref, wbuf, top_v, top_i, slot_i, slot_j, slot_g, rows_i, rows_j, rows_g,
                       w3_ref):
    t = pl.program_id(0)
    e = pl.program_id(1)
    fill = t % 2
    use = 1 - fill
    eb = u_ref.shape[0]

    @pl.when((t == 0) & (e == 0))
    def _():
        wbuf[1] = jnp.zeros(wbuf.shape[1:], BF16)

    @pl.when(e == 0)
    def _():
        acc_ref[...] = jnp.zeros_like(acc_ref)

    def dense():
        return _dense_items(h_ref, u_ref, v_ref, wbuf, use, e * eb, acc_ref)

    row0 = pl.multiple_of((e // FUSED_PHASES) * PEER_SEL_BLOCK, PEER_SEL_BLOCK)
    phase = e % FUSED_PHASES

    @pl.when(phase == 0)
    def _():
        _interleave(dense(), _stage1_items(q_ref, keys_ref, row0, range(STAGE1_SPLIT), top_v, top_i))

    @pl.when(phase == 1)
    def _():
        _interleave(dense(),
                    _stage1_items(q_ref, keys_ref, row0, range(STAGE1_SPLIT, PEER_HEADS), top_v, top_i)
                    + _stage2_items(top_v, top_i, slot_i, slot_j, slot_g, rows_i, rows_j, rows_g))

    @pl.when(phase == 2)
    def _():
        _interleave(dense(), _assemble_items(rows_i, rows_j, rows_g, w3_ref))

    @pl.when(phase == 3)
    def _():
        _interleave(dense(), _relayout_items(w3_ref, wbuf, fill, row0))

    @pl.when(e == pl.num_programs(1) - 1)
    def _():
        o_ref[...] = x_ref[...] + mod_ref[0, 5:6, :] * acc_ref[...]


def peer_fused(q, keys_padded, h2, u_tab, v_tab, x, mod_blocks):
    n_tok, d = x.shape
    n_exp = u_tab.shape[0]
    tb, eb = FUSED_TOKEN_BLOCK, FUSED_EXPERT_BLOCK
    ct = PEER_SEL_BLOCK
    n = PEER_N_KEYS
    slots = PEER_HEADS * PEER_TOPK
    n_blk = n_tok // tb
    assert tb == ROW_BLOCK and (tb // ct) * FUSED_PHASES == n_exp // eb
    cur = lambda t, e: (jnp.minimum(t, n_blk - 1), 0)
    prev = lambda t, e: (jnp.maximum(t - 1, 0), 0)
    return pl.pallas_call(
        _peer_fused_kernel,
        grid=(n_blk + 1, n_exp // eb),
        in_specs=[pl.BlockSpec((tb, q.shape[1]), cur),
                  pl.BlockSpec(keys_padded.shape, lambda t, e: (0, 0, 0)),
                  pl.BlockSpec((tb, d), prev),
                  pl.BlockSpec((eb, d), lambda t, e: (e, 0)),
                  pl.BlockSpec((eb, d), lambda t, e: (e, 0)),
                  pl.BlockSpec((tb, d), prev),
                  pl.BlockSpec((1, MOD_ROWS, d), lambda t, e: (jnp.maximum(t - 1, 0), 0, 0))],
        out_specs=pl.BlockSpec((tb, d), prev),
        out_shape=jax.ShapeDtypeStruct((n_tok, d), F32),
        scratch_shapes=[pltpu.VMEM((tb, d), F32),
                        pltpu.VMEM((2, tb, n * n), BF16),
                        pltpu.VMEM((PEER_HEADS, 2, PEER_TOPK, ct), F32),
                        pltpu.VMEM((PEER_HEADS, 2, PEER_TOPK, ct), F32),
                        pltpu.VMEM((slots, ct), F32), pltpu.VMEM((slots, ct), F32),
                        pltpu.VMEM((slots, ct), F32),
                        pltpu.VMEM((ct, slots), F32), pltpu.VMEM((ct, slots), F32),
                        pltpu.VMEM((ct, slots), F32),
                        pltpu.VMEM((ct, n, n), BF16)],
        compiler_params=pltpu.CompilerParams(
            dimension_semantics=("arbitrary", "arbitrary"), vmem_limit_bytes=VMEM_LIMIT),
        name="peer_fused",
    )(q, keys_padded, h2, u_tab, v_tab, x, mod_blocks)


RW_HEADS = RW_WIDTH // RW_HEAD_DIM
ROT_HALF = HEAD_DIM // 2
ROPE_THETA = 10000.0
GRID_W = 64


def _split(z, sizes):
    parts, start = [], 0
    for s in sizes:
        parts.append(z[..., start:start + s])
        start += s
    return parts


def _bd_from_states(s):
    a = jnp.swapaxes(s, -1, -2)
    n, d, h, m, _ = a.shape
    a = a.reshape(n, d, h // 2, 2, m, m)
    z = jnp.zeros_like(a[:, :, :, 0])
    top = jnp.concatenate([a[:, :, :, 0], z], axis=-1)
    bot = jnp.concatenate([z, a[:, :, :, 1]], axis=-1)
    return jnp.concatenate([top, bot], axis=-2)


def _states_from_bd(bd):
    n, d, p, _, _ = bd.shape
    m = RW_HEAD_DIM
    a = jnp.stack([bd[:, :, :, :m, :m], bd[:, :, :, m:, m:]], axis=3).reshape(n, d, 2 * p, m, m)
    return jnp.swapaxes(a, -1, -2)


def _rope_tables(t):
    rows = t // GRID_W
    row, col = jnp.meshgrid(jnp.arange(rows), jnp.arange(GRID_W), indexing='ij')
    inv = ROPE_THETA ** (-jnp.arange(0, ROT_HALF, 2, dtype=F32) / ROT_HALF)
    ang_r = row.reshape(-1, 1).astype(F32) * inv
    ang_c = col.reshape(-1, 1).astype(F32) * inv
    cr, sr, cc, sc = jnp.cos(ang_r), jnp.sin(ang_r), jnp.cos(ang_c), jnp.sin(ang_c)
    cos = jnp.concatenate([cr, cr, cc, cc], axis=-1)
    sin = jnp.concatenate([sr, sr, sc, sc], axis=-1)
    return jnp.tile(cos, (1, LANES // HEAD_DIM)), jnp.tile(sin, (1, LANES // HEAD_DIM))


def _pad_rows(w, top):
    z = jnp.zeros_like(w)
    return jnp.concatenate([w, z] if top else [z, w], axis=0)


def kernel(x_prompt, x_sample, c, cache_k, cache_v, state_wkv, c_ctx, w_mod, b_mod, w_in, shift_mu_prev, shift_mu_next, rw_w0, rw_w2, rw_a0, rw_a2, rw_g2, rw_k_k, rw_k_a, rw_r_k, rw_gn_w, rw_gn_b, w_out_rwkv, conv_w, conv_b, conv_ln_w, conv_ln_b, w_out_conv, q_norm, k_norm, w_out_attn, w_o, peer_wq, peer_keys, peer_u, peer_v):
    nb, seq, d = x_prompt.shape
    db, dseq, _ = x_sample.shape
    depth = w_in.shape[0]
    n_ctx = nb * seq
    n_lat = db * dseq
    n_tok = n_ctx + n_lat
    x = jnp.concatenate([x_prompt.reshape(n_ctx, d), x_sample.reshape(n_lat, d)], axis=0)

    cond = jnp.concatenate([c_ctx[None], c, jnp.zeros((MOD_ROWS - 1 - db, d), F32)], axis=0)
    blk_cond = jnp.concatenate([jnp.zeros((n_ctx // ROW_BLOCK,), jnp.int32),
                                1 + jnp.arange(n_lat // ROW_BLOCK, dtype=jnp.int32) // (dseq // ROW_BLOCK)])
    n_blk = n_tok // ROW_BLOCK
    blk_row = jnp.arange(n_blk) * ROW_BLOCK
    blk_pos = jnp.where(blk_row < n_ctx, blk_row % seq, (blk_row - n_ctx) % dseq)
    blk_len = jnp.where(blk_row < n_ctx, seq, dseq)
    starts_seq = (blk_pos == 0)[:, None]
    ends_seq = (blk_pos + ROW_BLOCK == blk_len)[:, None]
    rope = _rope_tables(dseq)
    in_sizes = (RW_COLS, 2 * CONV_WIDTH, ATT_HEADS * HEAD_DIM, 2 * ATT_KV_HEADS * HEAD_DIM, 3 * d)
    past = cache_k.shape[2]

    ctx_k, ctx_v, ctx_s = [], [], []
    for l in range(depth):
        mod = modulation(cond, w_mod[l].astype(BF16), b_mod[l][None])
        mod = mod.reshape(MOD_ROWS, 6, d)[blk_cond]
        mod_blocks = jnp.concatenate([mod, jnp.zeros((mod.shape[0], MOD_ROWS - 6, d), F32)], axis=1)

        z_rw, z_cv, z_q, z_kv, z_gate = in_projection(x, mod_blocks, _split(w_in[l].astype(BF16), in_sizes))

        zb = z_rw.reshape(n_blk, ROW_BLOCK, RW_COLS)
        halo_prev = jnp.where(starts_seq, 0.0, jnp.roll(zb[:, -1], 1, axis=0))[:, None]
        halo_next = jnp.where(ends_seq, 0.0, jnp.roll(zb[:, 0], -1, axis=0))[:, None]
        mu = jnp.stack([shift_mu_prev[l], shift_mu_next[l]])
        vecs = jnp.stack([rw_k_k[l], rw_k_a[l], rw_r_k[l], rw_w0[l, 0], rw_w0[l, 1], rw_a0[l, 0],
                          rw_a0[l, 1], jnp.zeros((RW_WIDTH,), F32)])
        w2_pad = jnp.stack([_pad_rows(rw_w2[l, 0], True), _pad_rows(rw_w2[l, 1], False)]).astype(BF16)
        a2_pad = jnp.stack([_pad_rows(rw_a2[l, 0], True), _pad_rows(rw_a2[l, 1], False)]).astype(BF16)
        r, v, kk, lw, aa, kd, bonus, gate = rwkv_prep(z_rw, halo_prev, halo_next, mu, vecs, w2_pad, a2_pad,
                                                      rw_g2[l].astype(BF16))
        s0_ctx = jnp.zeros((nb, 2, RW_HEADS // 2, LANES, LANES), F32)
        of_c, ob_c, sfin = rwkv_scan(r, v, kk, lw, aa, kd, s0_ctx, 0, nb, seq)
        of_l, ob_l, _ = rwkv_scan(r, v, kk, lw, aa, kd, _bd_from_states(state_wkv[:, l]), n_ctx, db, dseq)
        ctx_s.append(_states_from_bd(sfin))

        cvec = jnp.stack([conv_b[l], conv_ln_w[l], conv_ln_b[l]] + [jnp.zeros((CONV_WIDTH,), F32)] * 5)
        y_cv = (conv_module(z_cv, conv_w[l], cvec, 0, nb, seq),
                conv_module(z_cv, conv_w[l], cvec, n_ctx, db, dseq))

        nq = jnp.tile(q_norm[l], LANES // HEAD_DIM)[None]
        nk = jnp.tile(k_norm[l], LANES // HEAD_DIM)[None]
        y_c, k_ctx = attention(z_q, z_kv, nq, nk, 0, nb, seq)
        y_l = attention(z_q, z_kv, nq, nk, n_ctx, db, dseq, rope=rope,
                        cache=(cache_k[:, l].reshape(db * past, LANES), cache_v[:, l].reshape(db * past, LANES)))
        ctx_k.append(k_ctx.reshape(nb, seq, ATT_KV_HEADS, HEAD_DIM))
        ctx_v.append(z_kv[:n_ctx, LANES:].reshape(nb, seq, ATT_KV_HEADS, HEAD_DIM))

        x, h2, pq = merge_project(x, mod_blocks, bonus, gate, z_gate,
                                  (of_c, of_l), (ob_c, ob_l), y_cv, (y_c, y_l),
                                  jnp.stack([rw_gn_w[l], rw_gn_b[l]]),
                                  w_out_rwkv[l].astype(BF16), w_out_conv[l].astype(BF16),
                                  w_out_attn[l].astype(BF16), w_o[l].astype(BF16), peer_wq[l].astype(BF16))

        kz = jnp.zeros_like(peer_keys[l][:, 0])
        keys_padded = jnp.concatenate([jnp.concatenate([peer_keys[l][:, 0], kz], axis=-1),
                                       jnp.concatenate([kz, peer_keys[l][:, 1]], axis=-1)], axis=1)
        w_sel = peer_select(pq, keys_padded)
        x = peer_dense(h2, w_sel, peer_u[l].astype(BF16), peer_v[l].astype(BF16), x, mod_blocks)

    new_cache_k = jnp.stack(ctx_k, axis=1)
    new_cache_v = jnp.stack(ctx_v, axis=1)
    new_state = jnp.stack(ctx_s, axis=1)
    return (x[:n_ctx].reshape(nb, seq, d), x[n_ctx:].reshape(db, dseq, d),
            new_cache_k, new_cache_v, new_state)
```

```python
import functools
import math

import jax
import jax.numpy as jnp
from jax import lax
from jax.experimental import pallas as pl
from jax.experimental.pallas import tpu as pltpu

F32 = jnp.float32
BF16 = jnp.bfloat16

LANES = 128
SUBLANES = 8
SCAN_CHUNK = 64
SCAN_SUB = 16
RW_HEAD_DIM = 64
HEADS_PER_PAIR = LANES // RW_HEAD_DIM
VMEM_LIMIT = 56 * 1024 * 1024


def _dot(a, b):
    return jnp.dot(a.astype(BF16), b.astype(BF16), preferred_element_type=F32)


def _dot_nt(a, b):
    return lax.dot_general(a.astype(BF16), b.astype(BF16), (((1,), (1,)), ((), ())),
                           preferred_element_type=F32)


def _dot_tn(a, b):
    return lax.dot_general(a.astype(BF16), b.astype(BF16), (((0,), (0,)), ((), ())),
                           preferred_element_type=F32)


def _dot3(a, b):
    a_hi = a.astype(BF16)
    b_hi = b.astype(BF16)
    a_lo = (a - a_hi.astype(F32)).astype(BF16)
    b_lo = (b - b_hi.astype(F32)).astype(BF16)
    d = functools.partial(jnp.dot, preferred_element_type=F32)
    return d(a_hi, b_hi) + (d(a_hi, b_lo) + d(a_lo, b_hi))


def _scan_masks(reverse):
    C = SCAN_CHUNK
    P = HEADS_PER_PAIR * C
    row = lax.broadcasted_iota(jnp.int32, (P, LANES), 0)
    col = lax.broadcasted_iota(jnp.int32, (P, LANES), 1)
    tr = row % C
    tc = col % C
    t_i = lax.broadcasted_iota(jnp.int32, (C, C), 0)
    s_i = lax.broadcasted_iota(jnp.int32, (C, C), 1)
    return dict(
        same_head=(row // C) == (col // RW_HEAD_DIM),
        incl=(tr >= tc) if not reverse else (tr <= tc),
        strict=(tr > tc) if not reverse else (tr < tc),
        diag_blk=(tr // SCAN_SUB) == (tc // SCAN_SUB),
        eye=row == col,
        cum_mat=jnp.where((t_i >= s_i) if not reverse else (t_i <= s_i), 1.0, 0.0).astype(F32))


def _scan_chunk(chains, states):
    C = SCAN_CHUNK
    n = range(len(chains))
    mk = [ch[6] for ch in chains]
    cum = [ch[8] for ch in chains]
    pre = []
    for i in n:
        r, lw, k, v, kk, a, m, reverse, _ = chains[i]
        g = jnp.exp(cum[i])
        g_prev = jnp.exp(cum[i] - lw)
        g_inv = jnp.exp(-cum[i])
        g_end = jnp.exp(cum[i][C - 1:C, :] if not reverse else cum[i][0:1, :])
        bt = kk * a * g_inv
        kt = k * g_inv
        stack = lambda x, m=m: jnp.where(m["same_head"], jnp.concatenate([x, x], axis=0), 0.0)
        pre.append(dict(kap=stack(kk * g_prev), rt=stack(r * g), bt=stack(bt), kt=stack(kt),
                        v=stack(v), bh=stack(bt * g_end), kh=stack(kt * g_end), g_end=g_end))
    P = HEADS_PER_PAIR * C
    tri = [_dot_nt(jnp.concatenate([pre[i]["kap"], pre[i]["rt"]], axis=0),
                   jnp.concatenate([pre[i]["bt"], pre[i]["kt"]], axis=0)) for i in n]
    lb = [jnp.where(mk[i]["strict"], tri[i][:P, :P], 0.0) for i in n]
    lk = [jnp.where(mk[i]["strict"], tri[i][:P, P:], 0.0) for i in n]
    pb = [jnp.where(mk[i]["incl"], tri[i][P:, :P], 0.0) for i in n]
    pk = [jnp.where(mk[i]["incl"], tri[i][P:, P:], 0.0) for i in n]
    lkv = [_dot(lk[i], pre[i]["v"]) for i in n]

    d = [jnp.where(mk[i]["diag_blk"], lb[i], 0.0) for i in n]
    x = [jnp.concatenate([lb[i] - d[i], pre[i]["kap"], lkv[i]], axis=1) for i in n]
    x = [x[i] - _dot(d[i], x[i]) for i in n]
    p = [_dot(d[i], d[i]) for i in n]
    x = [x[i] + _dot(p[i], x[i]) for i in n]
    p = [_dot(p[i], p[i]) for i in n]
    x = [x[i] + _dot(p[i], x[i]) for i in n]
    p = [_dot(p[i], p[i]) for i in n]
    x = [x[i] + _dot(p[i], x[i]) for i in n]
    e = [x[i][:, :LANES] for i in n]
    rhs = [x[i][:, LANES:] for i in n]
    e2 = [_dot(e[i], e[i]) for i in n]
    t = [rhs[i] + _dot(e2[i], rhs[i]) for i in n]
    gy = [t[i] - _dot(e[i], t[i]) for i in n]

    m_mat = [jnp.where(mk[i]["eye"], pre[i]["g_end"], 0.0) - _dot_tn(pre[i]["bh"], gy[i][:, :LANES])
             for i in n]
    n_mat = [_dot_tn(pre[i]["kh"], pre[i]["v"]) - _dot_tn(pre[i]["bh"], gy[i][:, LANES:]) for i in n]
    pbgy = [_dot(pb[i], gy[i]) for i in n]
    q = [pre[i]["rt"] - pbgy[i][:, :LANES] for i in n]
    z = [_dot(pk[i], pre[i]["v"]) - pbgy[i][:, LANES:] for i in n]
    o_st = [_dot(q[i], states[i]) + z[i] for i in n]
    new = [_dot3(m_mat[i], states[i]) + n_mat[i] for i in n]
    return [(o_st[i][:C] + o_st[i][C:], new[i]) for i in n]


def _scan_kernel(rf, vf, kkf, lwf, af, kf, rb, vb, kkb, lwb, ab, kb, s0, of, ob, sfin, state):
    c = pl.program_id(1)
    n_pairs = rf.shape[-1] // LANES

    @pl.when(c == 0)
    def _():
        state[...] = s0[0]

    chains, states, outs = [], [], []
    for d, (r_, v_, kk_, lw_, a_, k_, o_) in enumerate(((rf, vf, kkf, lwf, af, kf, of),
                                                        (rb, vb, kkb, lwb, ab, kb, ob))):
        masks = _scan_masks(reverse=(d == 1))
        cum = _dot3(masks["cum_mat"], lw_[...])
        for p in range(n_pairs):
            sl = slice(p * LANES, (p + 1) * LANES)
            chains.append((r_[:, sl], lw_[:, sl], k_[:, sl], v_[:, sl], kk_[:, sl], a_[:, sl],
                           masks, d == 1, cum[:, sl]))
            states.append(state[d, p])
            outs.append((o_, sl, d, p))
    for (o, new), (o_, sl, d, p) in zip(_scan_chunk(chains, states), outs):
        o_[:, sl] = o
        state[d, p] = new

    @pl.when(c == pl.num_programs(1) - 1)
    def _():
        sfin[0] = state[...]


def rwkv_scan(r, v, kk, lw, a, kd, s0_bd, row0, n_seq, seq_len):
    width = r.shape[1]
    n_tok = n_seq * seq_len
    n_pairs = width // LANES
    n_chunk = seq_len // SCAN_CHUNK
    c0 = row0 // SCAN_CHUNK
    blk = (SCAN_CHUNK, width)
    dblk = (None, SCAN_CHUNK, width)
    fwd = lambda s, c: (s * n_chunk + c, 0)
    bwd = lambda s, c: (s * n_chunk + n_chunk - 1 - c, 0)
    fwd_in = lambda s, c: (c0 + s * n_chunk + c, 0)
    bwd_in = lambda s, c: (c0 + s * n_chunk + n_chunk - 1 - c, 0)
    fwd_d = lambda s, c: (0, c0 + s * n_chunk + c, 0)
    bwd_d = lambda s, c: (1, c0 + s * n_chunk + n_chunk - 1 - c, 0)
    sspec = pl.BlockSpec((1, 2, n_pairs, LANES, LANES), lambda s, c: (s, 0, 0, 0, 0))
    in_specs = ([pl.BlockSpec(blk, fwd_in)] * 3 + [pl.BlockSpec(dblk, fwd_d)] * 3
                + [pl.BlockSpec(blk, bwd_in)] * 3 + [pl.BlockSpec(dblk, bwd_d)] * 3 + [sspec])
    out_specs = [pl.BlockSpec(blk, fwd), pl.BlockSpec(blk, bwd), sspec]
    out_shape = [jax.ShapeDtypeStruct((n_tok, width), F32)] * 2 + [
        jax.ShapeDtypeStruct(s0_bd.shape, F32)]
    return pl.pallas_call(
        _scan_kernel,
        grid=(n_seq, n_chunk),
        in_specs=in_specs,
        out_specs=out_specs,
        out_shape=out_shape,
        scratch_shapes=[pltpu.VMEM((2, n_pairs, LANES, LANES), F32)],
        compiler_params=pltpu.CompilerParams(
            dimension_semantics=("parallel", "arbitrary"), vmem_limit_bytes=VMEM_LIMIT),
        name="rwkv_scan",
    )(r, v, kk, lw, a, kd, r, v, kk, lw, a, kd, s0_bd)


MOD_ROWS = 8
ROW_BLOCK = 256
RMS_EPS = 1e-6
LN_EPS = 1e-5
RW_GN_EPS = 64e-5


def _mod_kernel(c_ref, w_ref, b_ref, o_ref):
    c = c_ref[...]
    s = c * jax.nn.sigmoid(c)
    o_ref[...] = _dot(s, w_ref[...]) + b_ref[...]


def modulation(cond, w_mod, b_mod):
    n, d = cond.shape
    cols = w_mod.shape[1]
    blk = 1024
    return pl.pallas_call(
        _mod_kernel,
        grid=(cols // blk,),
        in_specs=[pl.BlockSpec((n, d), lambda j: (0, 0)),
                  pl.BlockSpec((d, blk), lambda j: (0, j)),
                  pl.BlockSpec((1, blk), lambda j: (0, j))],
        out_specs=pl.BlockSpec((n, blk), lambda j: (0, j)),
        out_shape=jax.ShapeDtypeStruct((n, cols), F32),
        name="modulation",
    )(cond, w_mod, b_mod)


def _modulate(x, shift, scale):
    ms = jnp.mean(x * x, axis=-1, keepdims=True)
    return x * lax.rsqrt(ms + RMS_EPS) * (1.0 + scale) + shift


def _inproj_kernel(x_ref, mod_ref, *refs):
    n = len(refs) // 2
    h = _modulate(x_ref[...], mod_ref[0, 0:1, :], mod_ref[0, 1:2, :]).astype(BF16)
    for w_ref, o_ref in zip(refs[:n], refs[n:]):
        o_ref[...] = jnp.dot(h, w_ref[...], preferred_element_type=F32)


def in_projection(x, mod_blocks, weights):
    n_tok, d = x.shape
    row = lambda i: (i, 0)
    const = lambda i: (0, 0)
    return pl.pallas_call(
        _inproj_kernel,
        grid=(n_tok // ROW_BLOCK,),
        in_specs=[pl.BlockSpec((ROW_BLOCK, d), row),
                  pl.BlockSpec((1, MOD_ROWS, d), lambda i: (i, 0, 0))]
                 + [pl.BlockSpec(w.shape, const, pipeline_mode=pl.Buffered(1)) for w in weights],
        out_specs=[pl.BlockSpec((ROW_BLOCK, w.shape[1]), row) for w in weights],
        out_shape=[jax.ShapeDtypeStruct((n_tok, w.shape[1]), F32) for w in weights],
        compiler_params=pltpu.CompilerParams(
            dimension_semantics=("parallel",), vmem_limit_bytes=VMEM_LIMIT),
        name="in_projection",
    )(x, mod_blocks, *weights)


HEAD_DIM = 64


def _group_ones(width):
    r = lax.broadcasted_iota(jnp.int32, (width, width), 0) // HEAD_DIM
    c = lax.broadcasted_iota(jnp.int32, (width, width), 1) // HEAD_DIM
    return jnp.where(r == c, 1.0, 0.0).astype(BF16)


def _group_sum(x, ones):
    hi = x.astype(BF16)
    lo = (x - hi.astype(F32)).astype(BF16)
    d = functools.partial(jnp.dot, preferred_element_type=F32)
    return d(hi, ones) + d(lo, ones)


RW_WIDTH = 512
RW_COLS = 3 * RW_WIDTH + 3 * LANES
RW_VEC_ROWS = 8


def _rwkv_prep_kernel(z_ref, hp_ref, hn_ref, mu_ref, vec_ref, w2_ref, a2_ref, g2_ref,
                      r_ref, v_ref, kk_ref, lw_ref, a_ref, kd_ref, bonus_ref, gate_ref):
    z = z_ref[...]
    rb = z.shape[0]
    row = lax.broadcasted_iota(jnp.int32, (rb, 1), 0)
    z_prev = jnp.where(row == 0, hp_ref[0], pltpu.roll(z, 1, 0))
    z_next = jnp.where(row == rb - 1, hn_ref[0], pltpu.roll(z, rb - 1, 0))
    zs = z + mu_ref[0:1, :] * (z_prev - z) + mu_ref[1:2, :] * (z_next - z)
    w = RW_WIDTH
    r, k, v = zs[:, :w], zs[:, w:2 * w], zs[:, 2 * w:3 * w]
    zw = zs[:, 3 * w:3 * w + LANES]
    za = zs[:, 3 * w + LANES:3 * w + 2 * LANES]
    zg = zs[:, 3 * w + 2 * LANES:]
    ones = _group_ones(w)
    kk = k * vec_ref[0:1, :]
    kk = kk * lax.rsqrt(_group_sum(kk * kk, ones) + 1e-12)
    r_ref[...] = r
    v_ref[...] = v
    kk_ref[...] = kk
    tz = jnp.tanh(zw)
    for d in range(2):
        w_raw = vec_ref[3 + d:4 + d, :] + _dot(tz, w2_ref[d])
        lw_ref[d] = -math.exp(-0.5) * jax.nn.sigmoid(w_raw)
        a = jax.nn.sigmoid(vec_ref[5 + d:6 + d, :] + _dot(za, a2_ref[d]))
        a_ref[d] = a
        kd_ref[d] = k * (1.0 + (a - 1.0) * vec_ref[1:2, :])
    bonus_ref[...] = _group_sum(r * k * vec_ref[2:3, :], ones) * v
    gate_ref[...] = _dot(jax.nn.sigmoid(zg), g2_ref[...])


def rwkv_prep(z_rw, halo_prev, halo_next, mu, vecs, w2_pad, a2_pad, g2):
    n_tok = z_rw.shape[0]
    w = RW_WIDTH
    row = lambda i: (i, 0)
    const2 = lambda i: (0, 0)
    const3 = lambda i: (0, 0, 0)
    halo = pl.BlockSpec((1, 1, RW_COLS), lambda i: (i, 0, 0))
    one = pl.BlockSpec((ROW_BLOCK, w), row)
    two = pl.BlockSpec((2, ROW_BLOCK, w), lambda i: (0, i, 0))
    s1 = jax.ShapeDtypeStruct((n_tok, w), F32)
    s2 = jax.ShapeDtypeStruct((2, n_tok, w), F32)
    return pl.pallas_call(
        _rwkv_prep_kernel,
        grid=(n_tok // ROW_BLOCK,),
        in_specs=[pl.BlockSpec((ROW_BLOCK, RW_COLS), row), halo, halo,
                  pl.BlockSpec(mu.shape, const2), pl.BlockSpec(vecs.shape, const2),
                  pl.BlockSpec(w2_pad.shape, const3), pl.BlockSpec(a2_pad.shape, const3),
                  pl.BlockSpec(g2.shape, const2)],
        out_specs=[one, one, one, two, two, two, one, one],
        out_shape=[s1, s1, s1, s2, s2, s2, s1, s1],
        compiler_params=pltpu.CompilerParams(
            dimension_semantics=("parallel",), vmem_limit_bytes=VMEM_LIMIT),
        name="rwkv_prep",
    )(z_rw, halo_prev, halo_next, mu, vecs, w2_pad, a2_pad, g2)


CONV_WIDTH = 512
CONV_KERNEL = 31
CONV_PAD = 16
CONV_ROWS = 64


def _conv_kernel(z_ref, w_ref, vec_ref, o_ref, pad_ref):
    t = z_ref.shape[0]
    cw = CONV_WIDTH
    half = CONV_KERNEL // 2
    zeros = jnp.zeros((CONV_PAD, cw), F32)
    pad_ref[0:CONV_PAD, :] = zeros
    pad_ref[CONV_PAD + t:CONV_PAD + t + CONV_PAD, :] = zeros
    pad_ref[CONV_PAD:CONV_PAD + t, :] = z_ref[:, :cw] * jax.nn.sigmoid(z_ref[:, cw:])

    def chunk(i, carry):
        base = pl.multiple_of(i * CONV_ROWS, CONV_ROWS)
        window = pad_ref[pl.ds(base, CONV_ROWS + 2 * CONV_PAD), :]
        span = CONV_ROWS + 2 * CONV_PAD - SUBLANES
        shifted = [window[rho:rho + span] for rho in range(SUBLANES)]
        acc = jnp.zeros((CONV_ROWS, cw), F32)
        for j in range(CONV_KERNEL):
            lo = CONV_PAD - half + j
            tile = (lo // SUBLANES) * SUBLANES
            acc = acc + w_ref[j:j + 1, :] * shifted[lo % SUBLANES][tile:tile + CONV_ROWS]
        u = acc + vec_ref[0:1, :]
        mu = jnp.mean(u, axis=-1, keepdims=True)
        cen = u - mu
        var = jnp.mean(cen * cen, axis=-1, keepdims=True)
        y = cen * lax.rsqrt(var + LN_EPS) * vec_ref[1:2, :] + vec_ref[2:3, :]
        o_ref[pl.ds(base, CONV_ROWS), :] = y * jax.nn.sigmoid(y)
        return carry

    lax.fori_loop(0, t // CONV_ROWS, chunk, 0)


def conv_module(z_cv, conv_w, vecs, row0, n_seq, seq_len):
    cw = CONV_WIDTH
    blk0 = row0 // seq_len
    return pl.pallas_call(
        _conv_kernel,
        grid=(n_seq,),
        in_specs=[pl.BlockSpec((seq_len, 2 * cw), lambda i: (blk0 + i, 0)),
                  pl.BlockSpec(conv_w.shape, lambda i: (0, 0)),
                  pl.BlockSpec(vecs.shape, lambda i: (0, 0))],
        out_specs=pl.BlockSpec((seq_len, cw), lambda i: (i, 0)),
        out_shape=jax.ShapeDtypeStruct((n_seq * seq_len, cw), F32),
        scratch_shapes=[pltpu.VMEM((seq_len + 2 * CONV_PAD, cw), F32)],
        compiler_params=pltpu.CompilerParams(
            dimension_semantics=("parallel",), vmem_limit_bytes=VMEM_LIMIT),
        name="conv_module",
    )(z_cv, conv_w, vecs)


ATT_Q_BLOCK = 256
ATT_HEADS = 8
ATT_KV_HEADS = 2
ATT_GROUP = ATT_HEADS // ATT_KV_HEADS


def _rot_partner():
    r = lax.broadcasted_iota(jnp.int32, (LANES, LANES), 0)
    c = lax.broadcasted_iota(jnp.int32, (LANES, LANES), 1)
    lo = (r % 32) < 16
    return jnp.where((c == r + 16) & lo, 1.0, jnp.where((c == r - 16) & ~lo, -1.0, 0.0)).astype(BF16)


def _attn_kernel(*refs, rotary):
    if rotary:
        (zq_ref, zkv_ref, nq_ref, nk_ref, cosq_ref, sinq_ref, cosk_ref, sink_ref, ck_ref, cv_ref,
         o_ref) = refs
    else:
        zq_ref, zkv_ref, nq_ref, nk_ref, o_ref, kn_ref = refs
    tq = zq_ref.shape[0]
    ones = _group_ones(LANES)
    lane = lax.broadcasted_iota(jnp.int32, (1, LANES), 1)
    first = lane < HEAD_DIM
    row_head = lax.broadcasted_iota(jnp.int32, (2 * tq, LANES), 0) // tq
    own = row_head == (lax.broadcasted_iota(jnp.int32, (2 * tq, LANES), 1) // HEAD_DIM)

    def norm(x, g):
        return x * lax.rsqrt(_group_sum(x * x, ones) * (1.0 / HEAD_DIM) + RMS_EPS) * g

    def rope(x, cos, sin):
        return x * cos + jnp.dot(x.astype(BF16), _rot_partner(), preferred_element_type=F32) * sin

    def dup(x):
        sw = pltpu.roll(x, HEAD_DIM, 1)
        return jnp.where(first, x, sw), jnp.where(first, sw, x)

    k = norm(zkv_ref[:, :LANES], nk_ref[...])
    v = zkv_ref[:, LANES:]
    if rotary:
        k = rope(k, cosk_ref[...], sink_ref[...])
        ck, cv = dup(ck_ref[...]), dup(cv_ref[...])
    else:
        kn_ref[...] = k
    kd, vd = dup(k), dup(v)
    for pair in range(ATT_HEADS // 2):
        g = pair // (ATT_GROUP // 2)
        q = norm(zq_ref[:, pair * LANES:(pair + 1) * LANES], nq_ref[...])
        if rotary:
            q = rope(q, cosq_ref[...], sinq_ref[...])
        q = q * (HEAD_DIM ** -0.5)
        qs = jnp.where(own, jnp.concatenate([q, q], axis=0), 0.0)
        s = _dot_nt(qs, kd[g])
        m = jnp.max(s, axis=-1, keepdims=True)
        if rotary:
            s2 = _dot_nt(qs, ck[g])
            m = jnp.maximum(m, jnp.max(s2, axis=-1, keepdims=True))
            p2 = jnp.exp(s2 - m)
        p = jnp.exp(s - m)
        den = jnp.sum(p, axis=-1, keepdims=True)
        o = _dot(p, vd[g])
        if rotary:
            den = den + jnp.sum(p2, axis=-1, keepdims=True)
            o = o + _dot(p2, cv[g])
        o = o / den
        o_ref[:, pair * LANES:(pair + 1) * LANES] = jnp.where(first, o[:tq], o[tq:])


def attention(z_q, z_kv, nq, nk, row0, n_seq, seq_len, rope=None, cache=None):
    qb = min(ATT_Q_BLOCK, seq_len)
    nqb = seq_len // qb
    qblk0 = row0 // qb
    sblk0 = row0 // seq_len
    wq = z_q.shape[1]
    in_specs = [pl.BlockSpec((qb, wq), lambda b, t: (qblk0 + b * nqb + t, 0)),
                pl.BlockSpec((seq_len, 2 * LANES), lambda b, t: (sblk0 + b, 0)),
                pl.BlockSpec((1, LANES), lambda b, t: (0, 0)),
                pl.BlockSpec((1, LANES), lambda b, t: (0, 0))]
    args = [z_q, z_kv, nq, nk]
    out_q = pl.BlockSpec((qb, wq), lambda b, t: (b * nqb + t, 0))
    y_shape = jax.ShapeDtypeStruct((n_seq * seq_len, wq), F32)
    if rope is None:
        out_specs = [out_q, pl.BlockSpec((seq_len, LANES), lambda b, t: (b, 0))]
        out_shape = [y_shape, jax.ShapeDtypeStruct((n_seq * seq_len, LANES), F32)]
    else:
        past = cache[0].shape[0] // n_seq
        in_specs += [pl.BlockSpec((qb, LANES), lambda b, t: (t, 0))] * 2
        in_specs += [pl.BlockSpec((seq_len, LANES), lambda b, t: (0, 0))] * 2
        in_specs += [pl.BlockSpec((past, LANES), lambda b, t: (b, 0))] * 2
        args += [rope[0], rope[1], rope[0], rope[1], cache[0], cache[1]]
        out_specs = out_q
        out_shape = y_shape
    return pl.pallas_call(
        functools.partial(_attn_kernel, rotary=rope is not None),
        grid=(n_seq, nqb),
        in_specs=in_specs,
        out_specs=out_specs,
        out_shape=out_shape,
        compiler_params=pltpu.CompilerParams(
            dimension_semantics=("parallel", "arbitrary"), vmem_limit_bytes=VMEM_LIMIT),
        name="attention",
    )(*args)


def _merge_kernel(x_ref, mod_ref, bonus_ref, gate_ref, zg_ref,
                  ofc_ref, ofl_ref, obc_ref, obl_ref, cvc_ref, cvl_ref, atc_ref, atl_ref,
                  gn_ref, wr_ref, wc_ref, wa_ref, wo_ref, wq_ref, xo_ref, h2_ref, q_ref, *, n_ctx_blocks):
    d = x_ref.shape[1]
    zg = zg_ref[...]
    is_ctx = pl.program_id(0) < n_ctx_blocks
    pick = lambda c_ref, l_ref: jnp.where(is_ctx, c_ref[...], l_ref[...])
    ones = _group_ones(ofc_ref.shape[1])
    o = pick(ofc_ref, ofl_ref) + pick(obc_ref, obl_ref)
    cen = o - _group_sum(o, ones) * (1.0 / HEAD_DIM)
    var = _group_sum(cen * cen, ones) * (1.0 / HEAD_DIM)
    o = cen * lax.rsqrt(var + RW_GN_EPS) * gn_ref[0:1, :] + gn_ref[1:2, :]
    y_rw = (o + bonus_ref[...]) * gate_ref[...]
    merged = (jax.nn.sigmoid(zg[:, :d]) * _dot(y_rw, wr_ref[...])
              + jax.nn.sigmoid(zg[:, d:2 * d]) * _dot(pick(cvc_ref, cvl_ref), wc_ref[...])
              + jax.nn.sigmoid(zg[:, 2 * d:]) * _dot(pick(atc_ref, atl_ref), wa_ref[...]))
    x = x_ref[...] + mod_ref[0, 2:3, :] * _dot(merged, wo_ref[...])
    xo_ref[...] = x
    h2 = _modulate(x, mod_ref[0, 3:4, :], mod_ref[0, 4:5, :]).astype(BF16)
    h2_ref[...] = h2
    q_ref[...] = jnp.dot(h2, wq_ref[...], preferred_element_type=F32)


def merge_project(x, mod_blocks, bonus, gate, z_gate, o_f, o_b, y_cv, y_at, gn, w_r, w_c, w_a, w_o, w_q):
    n_tok, d = x.shape
    nc = o_f[0].shape[0] // ROW_BLOCK
    row = lambda i: (i, 0)
    const = lambda i: (0, 0)
    ctx_row = lambda i: (jnp.minimum(i, nc - 1), 0)
    lat_row = lambda i: (jnp.maximum(i - nc, 0), 0)
    acts = [bonus, gate, z_gate]
    weights = [gn, w_r, w_c, w_a, w_o, w_q]
    pair_specs, pair_args = [], []
    for c_arr, l_arr in (o_f, o_b, y_cv, y_at):
        pair_specs += [pl.BlockSpec((ROW_BLOCK, c_arr.shape[1]), ctx_row),
                       pl.BlockSpec((ROW_BLOCK, l_arr.shape[1]), lat_row)]
        pair_args += [c_arr, l_arr]
    return pl.pallas_call(
        functools.partial(_merge_kernel, n_ctx_blocks=nc),
        grid=(n_tok // ROW_BLOCK,),
        in_specs=[pl.BlockSpec((ROW_BLOCK, d), row),
                  pl.BlockSpec((1, MOD_ROWS, d), lambda i: (i, 0, 0))]
                 + [pl.BlockSpec((ROW_BLOCK, a.shape[1]), row) for a in acts]
                 + pair_specs
                 + [pl.BlockSpec(w.shape, const) for w in weights],
        out_specs=[pl.BlockSpec((ROW_BLOCK, d), row), pl.BlockSpec((ROW_BLOCK, d), row),
                   pl.BlockSpec((ROW_BLOCK, w_q.shape[1]), row)],
        out_shape=[jax.ShapeDtypeStruct((n_tok, d), F32), jax.ShapeDtypeStruct((n_tok, d), BF16),
                   jax.ShapeDtypeStruct((n_tok, w_q.shape[1]), F32)],
        compiler_params=pltpu.CompilerParams(
            dimension_semantics=("parallel",), vmem_limit_bytes=VMEM_LIMIT),
        name="merge_project",
    )(x, mod_blocks, *acts, *pair_args, *weights)


PEER_HEADS = 8
PEER_N_KEYS = 128
PEER_HALF = 64
PEER_TOPK = 16
PEER_SEL_BLOCK = 128
PEER_LOOP_UNROLL = 16
PEER_RELAYOUT_TOKENS = 32
PEER_RELAYOUT_KEYS = 8
NEG = -1e30


def _dot3_nt(a, b):
    a_hi = a.astype(BF16)
    b_hi = b.astype(BF16)
    a_lo = (a - a_hi.astype(F32)).astype(BF16)
    b_lo = (b - b_hi.astype(F32)).astype(BF16)
    d = lambda x, y: lax.dot_general(x, y, (((1,), (1,)), ((), ())), preferred_element_type=F32)
    return d(a_hi, b_hi) + (d(a_hi, b_lo) + d(a_lo, b_hi))


def _extract_max(s, code):
    m = jnp.max(s, axis=0, keepdims=True)
    pos = jnp.min(jnp.where(s == m, code, 1e9), axis=0, keepdims=True)
    return m, pos, jnp.where(code == pos, NEG, s)


def _pair_candidates(a0, a1):
    k = PEER_TOPK
    sub = lax.broadcasted_iota(jnp.int32, (k, a0.shape[1]), 0).astype(F32)
    sub8 = sub[:8]
    vals = [a0[0:1] + a1]
    codes = [sub]
    for r0 in range(1, 8):
        limit = k // (r0 + 1)
        vals.append(jnp.where(sub8 < limit, a0[r0:r0 + 1] + a1[:8], NEG))
        codes.append(sub8 + float(r0 * k))
    vals.append(a0[8:] + a1[0:1])
    codes.append((sub8 + 8.0) * float(k))
    return jnp.concatenate(vals, axis=0), jnp.concatenate(codes, axis=0)


def _peer_select_kernel(q_ref, keys_ref, w_ref, top_v, top_i, slot_i, slot_j, slot_g,
                        rows_i, rows_j, rows_g, w3_ref):
    k = PEER_TOPK
    ct = q_ref.shape[0]
    key_code = lax.broadcasted_iota(jnp.int32, (PEER_N_KEYS, ct), 0).astype(F32)
    rank = lax.broadcasted_iota(jnp.int32, (k, ct), 0).astype(F32)
    for h in range(PEER_HEADS):
        st = _dot3_nt(keys_ref[h], q_ref[:, h * LANES:(h + 1) * LANES])
        s = [st[:PEER_N_KEYS], st[PEER_N_KEYS:]]
        for it in range(k):
            for p in range(2):
                m, pos, s[p] = _extract_max(s[p], key_code)
                top_v[h, p, it:it + 1, :] = m
                top_i[h, p, it:it + 1, :] = pos
    for h0 in range(0, PEER_HEADS, 2):
        heads = (h0, h0 + 1)
        cand, code, best, z = {}, {}, {}, {}
        for h in heads:
            a0, a1 = top_v[h, 0], top_v[h, 1]
            cand[h], code[h] = _pair_candidates(a0, a1)
            best[h] = a0[0:1] + a1[0:1]
            z[h] = jnp.zeros_like(best[h])
        for it in range(k):
            for h in heads:
                m, pos, cand[h] = _extract_max(cand[h], code[h])
                r0 = jnp.floor(pos * (1.0 / k))
                r1 = pos - r0 * k
                e = jnp.exp(m - best[h])
                z[h] = z[h] + e
                slot = h * k + it
                slot_i[slot:slot + 1, :] = jnp.sum(jnp.where(rank == r0, top_i[h, 0], 0.0), axis=0,
                                                   keepdims=True)
                slot_j[slot:slot + 1, :] = jnp.sum(jnp.where(rank == r1, top_i[h, 1], 0.0), axis=0,
                                                   keepdims=True)
                slot_g[slot:slot + 1, :] = e
        for h in heads:
            slot_g[h * k:(h + 1) * k, :] = slot_g[h * k:(h + 1) * k, :] / z[h]

    rows_i[...] = slot_i[...].T
    rows_j[...] = slot_j[...].T
    rows_g[...] = slot_g[...].T
    n = PEER_N_KEYS
    sub = lax.broadcasted_iota(jnp.int32, (n, n), 0).astype(F32)

    def per_token(c, carry):
        a_t = jnp.where(sub == rows_i[pl.ds(c, 1), :], rows_g[pl.ds(c, 1), :], 0.0)
        b_t = jnp.where(sub == rows_j[pl.ds(c, 1), :], 1.0, 0.0)
        w3_ref[c] = _dot_nt(a_t, b_t)
        return carry

    lax.fori_loop(0, ct, per_token, 0, unroll=PEER_LOOP_UNROLL)

    tg, ig = PEER_RELAYOUT_TOKENS, PEER_RELAYOUT_KEYS
    r_out = lax.broadcasted_iota(jnp.int32, (tg * ig, tg * ig), 0)
    r_in = lax.broadcasted_iota(jnp.int32, (tg * ig, tg * ig), 1)
    perm = jnp.where(r_in == (r_out % tg) * ig + r_out // tg, 1.0, 0.0).astype(BF16)

    for ib in range(n // ig):
        for t in range(ct // tg):
            z = w3_ref[t * tg:(t + 1) * tg, ib * ig:(ib + 1) * ig, :]
            y = jnp.dot(perm, z.reshape(tg * ig, n).astype(BF16), preferred_element_type=F32)
            for kk in range(ig):
                col = (ib * ig + kk) * n
                w_ref[t * tg:(t + 1) * tg, col:col + n] = y[kk * tg:(kk + 1) * tg].astype(BF16)


def peer_select(q, keys_padded):
    n_tok, qd = q.shape
    ct = PEER_SEL_BLOCK
    n = PEER_N_KEYS
    slots = PEER_HEADS * PEER_TOPK
    return pl.pallas_call(
        _peer_select_kernel,
        grid=(n_tok // ct,),
        in_specs=[pl.BlockSpec((ct, qd), lambda i: (i, 0)),
                  pl.BlockSpec(keys_padded.shape, lambda i: (0, 0, 0))],
        out_specs=pl.BlockSpec((ct, n * n), lambda i: (i, 0)),
        out_shape=jax.ShapeDtypeStruct((n_tok, n * n), BF16),
        scratch_shapes=[pltpu.VMEM((PEER_HEADS, 2, PEER_TOPK, ct), F32),
                        pltpu.VMEM((PEER_HEADS, 2, PEER_TOPK, ct), F32),
                        pltpu.VMEM((slots, ct), F32), pltpu.VMEM((slots, ct), F32),
                        pltpu.VMEM((slots, ct), F32),
                        pltpu.VMEM((ct, slots), F32), pltpu.VMEM((ct, slots), F32),
                        pltpu.VMEM((ct, slots), F32),
                        pltpu.VMEM((ct, n, n), F32)],
        compiler_params=pltpu.CompilerParams(
            dimension_semantics=("parallel",), vmem_limit_bytes=VMEM_LIMIT),
        name="peer_select",
    )(q, keys_padded)


PEER_TOKEN_BLOCK = 1024
PEER_EXPERT_BLOCK = 1024


def _peer_kernel(h_ref, w_ref, u_ref, v_ref, x_ref, mod_ref, o_ref, acc_ref):
    e = pl.program_id(1)

    @pl.when(e == 0)
    def _():
        acc_ref[...] = jnp.zeros_like(acc_ref)

    s = _dot_nt(h_ref[...], u_ref[...])
    act = 0.5 * s * (1.0 + lax.erf(s * (2.0 ** -0.5)))
    acc_ref[...] += jnp.dot(act.astype(BF16) * w_ref[...], v_ref[...], preferred_element_type=F32)

    @pl.when(e == pl.num_programs(1) - 1)
    def _():
        for j in range(mod_ref.shape[0]):
            rows = slice(j * ROW_BLOCK, (j + 1) * ROW_BLOCK)
            o_ref[rows, :] = x_ref[rows, :] + mod_ref[j, 5:6, :] * acc_ref[rows, :]


def peer_dense(h2, w_sel, u_tab, v_tab, x, mod_blocks):
    n_tok, d = x.shape
    n_exp = u_tab.shape[0]
    tb, eb = PEER_TOKEN_BLOCK, PEER_EXPERT_BLOCK
    tok = lambda i, e: (i, 0)
    return pl.pallas_call(
        _peer_kernel,
        grid=(n_tok // tb, n_exp // eb),
        in_specs=[pl.BlockSpec((tb, d), tok),
                  pl.BlockSpec((tb, eb), lambda i, e: (i, e)),
                  pl.BlockSpec((eb, d), lambda i, e: (e, 0)),
                  pl.BlockSpec((eb, d), lambda i, e: (e, 0)),
                  pl.BlockSpec((tb, d), tok),
                  pl.BlockSpec((tb // ROW_BLOCK, MOD_ROWS, d), lambda i, e: (i, 0, 0))],
        out_specs=pl.BlockSpec((tb, d), tok),
        out_shape=jax.ShapeDtypeStruct((n_tok, d), F32),
        scratch_shapes=[pltpu.VMEM((tb, d), F32)],
        compiler_params=pltpu.CompilerParams(
            dimension_semantics=("parallel", "arbitrary"), vmem_limit_bytes=VMEM_LIMIT),
        name="peer_dense",
    )(h2, w_sel, u_tab, v_tab, x, mod_blocks)


FUSED_TOKEN_BLOCK = 256
FUSED_EXPERT_BLOCK = 2048
FUSED_PHASES = 4
FUSED_RELAYOUT = 16
STAGE1_SPLIT = 5
FUSED_DENSE_CHUNKS = 4


def _stage1_items(q_ref, keys_ref, row0, heads, top_v, top_i):
    ct = PEER_SEL_BLOCK
    key_code = lax.broadcasted_iota(jnp.int32, (PEER_N_KEYS, ct), 0).astype(F32)
    st = {}
    items = []

    def score(h):
        st["both"] = _dot3_nt(keys_ref[h], q_ref[pl.ds(row0, ct), h * LANES:(h + 1) * LANES])

    def start(p):
        st["s"] = st["both"][p * PEER_N_KEYS:(p + 1) * PEER_N_KEYS]

    def step(h, p, it):
        m, pos, st["s"] = _extract_max(st["s"], key_code)
        top_v[h, p, it:it + 1, :] = m
        top_i[h, p, it:it + 1, :] = pos

    for h in heads:
        items.append(functools.partial(score, h))
        for p in range(2):
            items.append(functools.partial(start, p))
            items += [functools.partial(step, h, p, it) for it in range(PEER_TOPK)]
    return items


def _stage2_items(top_v, top_i, slot_i, slot_j, slot_g, rows_i, rows_j, rows_g):
    k = PEER_TOPK
    ct = PEER_SEL_BLOCK
    rank = lax.broadcasted_iota(jnp.int32, (k, ct), 0).astype(F32)
    st = {}
    items = []

    def start(h):
        a0, a1 = top_v[h, 0], top_v[h, 1]
        st["cand"], st["code"] = _pair_candidates(a0, a1)
        st["best"] = a0[0:1] + a1[0:1]
        st["z"] = jnp.zeros_like(st["best"])

    def step(h, it):
        m, pos, st["cand"] = _extract_max(st["cand"], st["code"])
        r0 = jnp.floor(pos * (1.0 / k))
        r1 = pos - r0 * k
        e = jnp.exp(m - st["best"])
        st["z"] = st["z"] + e
        slot = h * k + it
        slot_i[slot:slot + 1, :] = jnp.sum(jnp.where(rank == r0, top_i[h, 0], 0.0), axis=0, keepdims=True)
        slot_j[slot:slot + 1, :] = jnp.sum(jnp.where(rank == r1, top_i[h, 1], 0.0), axis=0, keepdims=True)
        slot_g[slot:slot + 1, :] = e

    def finish(h):
        slot_g[h * k:(h + 1) * k, :] = slot_g[h * k:(h + 1) * k, :] / st["z"]

    def transpose():
        rows_i[...] = slot_i[...].T
        rows_j[...] = slot_j[...].T
        rows_g[...] = slot_g[...].T

    for h in range(PEER_HEADS):
        items.append(functools.partial(start, h))
        items += [functools.partial(step, h, it) for it in range(k)]
        items.append(functools.partial(finish, h))
    items.append(transpose)
    return items


def _assemble_items(rows_i, rows_j, rows_g, w3_ref):
    n = PEER_N_KEYS
    sub = lax.broadcasted_iota(jnp.int32, (n, n), 0).astype(F32)

    def token(c):
        a_t = jnp.where(sub == rows_i[c:c + 1, :], rows_g[c:c + 1, :], 0.0)
        b_t = jnp.where(sub == rows_j[c:c + 1, :], 1.0, 0.0)
        w3_ref[c] = _dot_nt(a_t, b_t).astype(BF16)

    return [functools.partial(token, c) for c in range(PEER_SEL_BLOCK)]


def _relayout_items(w3_ref, wbuf, buf, row0):
    n = PEER_N_KEYS
    g = FUSED_RELAYOUT
    r_out = lax.broadcasted_iota(jnp.int32, (g * g, g * g), 0)
    r_in = lax.broadcasted_iota(jnp.int32, (g * g, g * g), 1)
    perm = jnp.where(r_in == (r_out % g) * g + r_out // g, 1.0, 0.0).astype(BF16)

    def slab(ib, tt):
        z = w3_ref[tt * g:(tt + 1) * g, ib * g:(ib + 1) * g, :].reshape(g * g, n)
        y = jnp.dot(perm, z, preferred_element_type=F32)
        rows = pl.ds(pl.multiple_of(row0 + tt * g, g), g)
        for kk in range(g):
            col = (ib * g + kk) * n
            wbuf[buf, rows, col:col + n] = y[kk * g:(kk + 1) * g].astype(BF16)

    return [functools.partial(slab, ib, tt) for ib in range(n // g) for tt in range(PEER_SEL_BLOCK // g)]


def _dense_items(h_ref, u_ref, v_ref, wbuf, use, col0, acc_ref):
    nc = FUSED_DENSE_CHUNKS
    cs = u_ref.shape[0] // nc
    st = {}

    def scores(c):
        st["s", c] = _dot_nt(h_ref[...], u_ref[c * cs:(c + 1) * cs, :])

    def gate(c):
        s = st.pop(("s", c))
        act = 0.5 * s * (1.0 + lax.erf(s * (2.0 ** -0.5)))
        w = wbuf[use, :, pl.ds(pl.multiple_of(col0 + c * cs, cs), cs)]
        st["g"] = act.astype(BF16) * w

    def accumulate(c):
        acc_ref[...] += jnp.dot(st["g"], v_ref[c * cs:(c + 1) * cs, :], preferred_element_type=F32)

    items = [functools.partial(scores, 0)]
    for c in range(nc):
        if c + 1 < nc:
            items.append(functools.partial(scores, c + 1))
        items += [functools.partial(gate, c), functools.partial(accumulate, c)]
    return items


def _interleave(a, b):
    done = 0
    for i, item in enumerate(a):
        item()
        upto = (i + 1) * len(b) // len(a)
        for other in b[done:upto]:
            other()
        done = upto


def _peer_fused_kernel(q_ref, keys_ref, h_ref, u_ref, v_ref, x_ref, mod_ref, o_ref,
                       acc_ref, wbuf, top_v, top_i, slot_i, slot_j, slot_g, rows_i, rows_j, rows_g,
                       w3_ref):
    t = pl.program_id(0)
    e = pl.program_id(1)
    fill = t % 2
    use = 1 - fill
    eb = u_ref.shape[0]

    @pl.when((t == 0) & (e == 0))
    def _():
        wbuf[1] = jnp.zeros(wbuf.shape[1:], BF16)

    @pl.when(e == 0)
    def _():
        acc_ref[...] = jnp.zeros_like(acc_ref)

    def dense():
        return _dense_items(h_ref, u_ref, v_ref, wbuf, use, e * eb, acc_ref)

    row0 = pl.multiple_of((e // FUSED_PHASES) * PEER_SEL_BLOCK, PEER_SEL_BLOCK)
    phase = e % FUSED_PHASES

    @pl.when(phase == 0)
    def _():
        _interleave(dense(), _stage1_items(q_ref, keys_ref, row0, range(STAGE1_SPLIT), top_v, top_i))

    @pl.when(phase == 1)
    def _():
        _interleave(dense(),
                    _stage1_items(q_ref, keys_ref, row0, range(STAGE1_SPLIT, PEER_HEADS), top_v, top_i)
                    + _stage2_items(top_v, top_i, slot_i, slot_j, slot_g, rows_i, rows_j, rows_g))

    @pl.when(phase == 2)
    def _():
        _interleave(dense(), _assemble_items(rows_i, rows_j, rows_g, w3_ref))

    @pl.when(phase == 3)
    def _():
        _interleave(dense(), _relayout_items(w3_ref, wbuf, fill, row0))

    @pl.when(e == pl.num_programs(1) - 1)
    def _():
        o_ref[...] = x_ref[...] + mod_ref[0, 5:6, :] * acc_ref[...]


def peer_fused(q, keys_padded, h2, u_tab, v_tab, x, mod_blocks):
    n_tok, d = x.shape
    n_exp = u_tab.shape[0]
    tb, eb = FUSED_TOKEN_BLOCK, FUSED_EXPERT_BLOCK
    ct = PEER_SEL_BLOCK
    n = PEER_N_KEYS
    slots = PEER_HEADS * PEER_TOPK
    n_blk = n_tok // tb
    assert tb == ROW_BLOCK and (tb // ct) * FUSED_PHASES == n_exp // eb
    cur = lambda t, e: (jnp.minimum(t, n_blk - 1), 0)
    prev = lambda t, e: (jnp.maximum(t - 1, 0), 0)
    return pl.pallas_call(
        _peer_fused_kernel,
        grid=(n_blk + 1, n_exp // eb),
        in_specs=[pl.BlockSpec((tb, q.shape[1]), cur),
                  pl.BlockSpec(keys_padded.shape, lambda t, e: (0, 0, 0)),
                  pl.BlockSpec((tb, d), prev),
                  pl.BlockSpec((eb, d), lambda t, e: (e, 0)),
                  pl.BlockSpec((eb, d), lambda t, e: (e, 0)),
                  pl.BlockSpec((tb, d), prev),
                  pl.BlockSpec((1, MOD_ROWS, d), lambda t, e: (jnp.maximum(t - 1, 0), 0, 0))],
        out_specs=pl.BlockSpec((tb, d), prev),
        out_shape=jax.ShapeDtypeStruct((n_tok, d), F32),
        scratch_shapes=[pltpu.VMEM((tb, d), F32),
                        pltpu.VMEM((2, tb, n * n), BF16),
                        pltpu.VMEM((PEER_HEADS, 2, PEER_TOPK, ct), F32),
                        pltpu.VMEM((PEER_HEADS, 2, PEER_TOPK, ct), F32),
                        pltpu.VMEM((slots, ct), F32), pltpu.VMEM((slots, ct), F32),
                        pltpu.VMEM((slots, ct), F32),
                        pltpu.VMEM((ct, slots), F32), pltpu.VMEM((ct, slots), F32),
                        pltpu.VMEM((ct, slots), F32),
                        pltpu.VMEM((ct, n, n), BF16)],
        compiler_params=pltpu.CompilerParams(
            dimension_semantics=("arbitrary", "arbitrary"), vmem_limit_bytes=VMEM_LIMIT),
        name="peer_fused",
    )(q, keys_padded, h2, u_tab, v_tab, x, mod_blocks)


RW_HEADS = RW_WIDTH // RW_HEAD_DIM
ROT_HALF = HEAD_DIM // 2
ROPE_THETA = 10000.0
GRID_W = 64


def _split(z, sizes):
    parts, start = [], 0
    for s in sizes:
        parts.append(z[..., start:start + s])
        start += s
    return parts


def _bd_from_states(s):
    a = jnp.swapaxes(s, -1, -2)
    n, d, h, m, _ = a.shape
    a = a.reshape(n, d, h // 2, 2, m, m)
    z = jnp.zeros_like(a[:, :, :, 0])
    top = jnp.concatenate([a[:, :, :, 0], z], axis=-1)
    bot = jnp.concatenate([z, a[:, :, :, 1]], axis=-1)
    return jnp.concatenate([top, bot], axis=-2)


def _states_from_bd(bd):
    n, d, p, _, _ = bd.shape
    m = RW_HEAD_DIM
    a = jnp.stack([bd[:, :, :, :m, :m], bd[:, :, :, m:, m:]], axis=3).reshape(n, d, 2 * p, m, m)
    return jnp.swapaxes(a, -1, -2)


def _rope_tables(t):
    rows = t // GRID_W
    row, col = jnp.meshgrid(jnp.arange(rows), jnp.arange(GRID_W), indexing='ij')
    inv = ROPE_THETA ** (-jnp.arange(0, ROT_HALF, 2, dtype=F32) / ROT_HALF)
    ang_r = row.reshape(-1, 1).astype(F32) * inv
    ang_c = col.reshape(-1, 1).astype(F32) * inv
    cr, sr, cc, sc = jnp.cos(ang_r), jnp.sin(ang_r), jnp.cos(ang_c), jnp.sin(ang_c)
    cos = jnp.concatenate([cr, cr, cc, cc], axis=-1)
    sin = jnp.concatenate([sr, sr, sc, sc], axis=-1)
    return jnp.tile(cos, (1, LANES // HEAD_DIM)), jnp.tile(sin, (1, LANES // HEAD_DIM))


def _pad_rows(w, top):
    z = jnp.zeros_like(w)
    return jnp.concatenate([w, z] if top else [z, w], axis=0)


def kernel(x_prompt, x_sample, c, cache_k, cache_v, state_wkv, c_ctx, w_mod, b_mod, w_in, shift_mu_prev, shift_mu_next, rw_w0, rw_w2, rw_a0, rw_a2, rw_g2, rw_k_k, rw_k_a, rw_r_k, rw_gn_w, rw_gn_b, w_out_rwkv, conv_w, conv_b, conv_ln_w, conv_ln_b, w_out_conv, q_norm, k_norm, w_out_attn, w_o, peer_wq, peer_keys, peer_u, peer_v):
    nb, seq, d = x_prompt.shape
    db, dseq, _ = x_sample.shape
    depth = w_in.shape[0]
    n_ctx = nb * seq
    n_lat = db * dseq
    n_tok = n_ctx + n_lat
    x = jnp.concatenate([x_prompt.reshape(n_ctx, d), x_sample.reshape(n_lat, d)], axis=0)

    cond = jnp.concatenate([c_ctx[None], c, jnp.zeros((MOD_ROWS - 1 - db, d), F32)], axis=0)
    blk_cond = jnp.concatenate([jnp.zeros((n_ctx // ROW_BLOCK,), jnp.int32),
                                1 + jnp.arange(n_lat // ROW_BLOCK, dtype=jnp.int32) // (dseq // ROW_BLOCK)])
    n_blk = n_tok // ROW_BLOCK
    blk_row = jnp.arange(n_blk) * ROW_BLOCK
    blk_pos = jnp.where(blk_row < n_ctx, blk_row % seq, (blk_row - n_ctx) % dseq)
    blk_len = jnp.where(blk_row < n_ctx, seq, dseq)
    starts_seq = (blk_pos == 0)[:, None]
    ends_seq = (blk_pos + ROW_BLOCK == blk_len)[:, None]
    rope = _rope_tables(dseq)
    in_sizes = (RW_COLS, 2 * CONV_WIDTH, ATT_HEADS * HEAD_DIM, 2 * ATT_KV_HEADS * HEAD_DIM, 3 * d)
    past = cache_k.shape[2]

    ctx_k, ctx_v, ctx_s = [], [], []
    for l in range(depth):
        mod = modulation(cond, w_mod[l].astype(BF16), b_mod[l][None])
        mod = mod.reshape(MOD_ROWS, 6, d)[blk_cond]
        mod_blocks = jnp.concatenate([mod, jnp.zeros((mod.shape[0], MOD_ROWS - 6, d), F32)], axis=1)

        z_rw, z_cv, z_q, z_kv, z_gate = in_projection(x, mod_blocks, _split(w_in[l].astype(BF16), in_sizes))

        zb = z_rw.reshape(n_blk, ROW_BLOCK, RW_COLS)
        halo_prev = jnp.where(starts_seq, 0.0, jnp.roll(zb[:, -1], 1, axis=0))[:, None]
        halo_next = jnp.where(ends_seq, 0.0, jnp.roll(zb[:, 0], -1, axis=0))[:, None]
        mu = jnp.stack([shift_mu_prev[l], shift_mu_next[l]])
        vecs = jnp.stack([rw_k_k[l], rw_k_a[l], rw_r_k[l], rw_w0[l, 0], rw_w0[l, 1], rw_a0[l, 0],
                          rw_a0[l, 1], jnp.zeros((RW_WIDTH,), F32)])
        w2_pad = jnp.stack([_pad_rows(rw_w2[l, 0], True), _pad_rows(rw_w2[l, 1], False)]).astype(BF16)
        a2_pad = jnp.stack([_pad_rows(rw_a2[l, 0], True), _pad_rows(rw_a2[l, 1], False)]).astype(BF16)
        r, v, kk, lw, aa, kd, bonus, gate = rwkv_prep(z_rw, halo_prev, halo_next, mu, vecs, w2_pad, a2_pad,
                                                      rw_g2[l].astype(BF16))
        s0_ctx = jnp.zeros((nb, 2, RW_HEADS // 2, LANES, LANES), F32)
        of_c, ob_c, sfin = rwkv_scan(r, v, kk, lw, aa, kd, s0_ctx, 0, nb, seq)
        of_l, ob_l, _ = rwkv_scan(r, v, kk, lw, aa, kd, _bd_from_states(state_wkv[:, l]), n_ctx, db, dseq)
        ctx_s.append(_states_from_bd(sfin))

        cvec = jnp.stack([conv_b[l], conv_ln_w[l], conv_ln_b[l]] + [jnp.zeros((CONV_WIDTH,), F32)] * 5)
        y_cv = (conv_module(z_cv, conv_w[l], cvec, 0, nb, seq),
                conv_module(z_cv, conv_w[l], cvec, n_ctx, db, dseq))

        nq = jnp.tile(q_norm[l], LANES // HEAD_DIM)[None]
        nk = jnp.tile(k_norm[l], LANES // HEAD_DIM)[None]
        y_c, k_ctx = attention(z_q, z_kv, nq, nk, 0, nb, seq)
        y_l = attention(z_q, z_kv, nq, nk, n_ctx, db, dseq, rope=rope,
                        cache=(cache_k[:, l].reshape(db * past, LANES), cache_v[:, l].reshape(db * past, LANES)))
        ctx_k.append(k_ctx.reshape(nb, seq, ATT_KV_HEADS, HEAD_DIM))
        ctx_v.append(z_kv[:n_ctx, LANES:].reshape(nb, seq, ATT_KV_HEADS, HEAD_DIM))

        x, h2, pq = merge_project(x, mod_blocks, bonus, gate, z_gate,
                                  (of_c, of_l), (ob_c, ob_l), y_cv, (y_c, y_l),
                                  jnp.stack([rw_gn_w[l], rw_gn_b[l]]),
                                  w_out_rwkv[l].astype(BF16), w_out_conv[l].astype(BF16),
                                  w_out_attn[l].astype(BF16), w_o[l].astype(BF16), peer_wq[l].astype(BF16))

        kz = jnp.zeros_like(peer_keys[l][:, 0])
        keys_padded = jnp.concatenate([jnp.concatenate([peer_keys[l][:, 0], kz], axis=-1),
                                       jnp.concatenate([kz, peer_keys[l][:, 1]], axis=-1)], axis=1)
        x = peer_fused(pq, keys_padded, h2, peer_u[l].astype(BF16), peer_v[l].astype(BF16), x, mod_blocks)

    new_cache_k = jnp.stack(ctx_k, axis=1)
    new_cache_v = jnp.stack(ctx_v, axis=1)
    new_state = jnp.stack(ctx_s, axis=1)
    return (x[:n_ctx].reshape(nb, seq, d), x[n_ctx:].reshape(db, dseq, d),
            new_cache_k, new_cache_v, new_state)
```

```python
import functools
import math

import jax
import jax.numpy as jnp
from jax import lax
from jax.experimental import pallas as pl
from jax.experimental.pallas import tpu as pltpu

F32 = jnp.float32
BF16 = jnp.bfloat16

LANES = 128
SUBLANES = 8
SCAN_CHUNK = 64
SCAN_SUB = 16
RW_HEAD_DIM = 64
HEADS_PER_PAIR = LANES // RW_HEAD_DIM
VMEM_LIMIT = 56 * 1024 * 1024


def _dot(a, b):
    return jnp.dot(a.astype(BF16), b.astype(BF16), preferred_element_type=F32)


def _dot_nt(a, b):
    return lax.dot_general(a.astype(BF16), b.astype(BF16), (((1,), (1,)), ((), ())),
                           preferred_element_type=F32)


def _dot_tn(a, b):
    return lax.dot_general(a.astype(BF16), b.astype(BF16), (((0,), (0,)), ((), ())),
                           preferred_element_type=F32)


def _dot3(a, b):
    a_hi = a.astype(BF16)
    b_hi = b.astype(BF16)
    a_lo = (a - a_hi.astype(F32)).astype(BF16)
    b_lo = (b - b_hi.astype(F32)).astype(BF16)
    d = functools.partial(jnp.dot, preferred_element_type=F32)
    return d(a_hi, b_hi) + (d(a_hi, b_lo) + d(a_lo, b_hi))


def _scan_masks(reverse):
    C = SCAN_CHUNK
    P = HEADS_PER_PAIR * C
    row = lax.broadcasted_iota(jnp.int32, (P, LANES), 0)
    col = lax.broadcasted_iota(jnp.int32, (P, LANES), 1)
    tr = row % C
    tc = col % C
    t_i = lax.broadcasted_iota(jnp.int32, (C, C), 0)
    s_i = lax.broadcasted_iota(jnp.int32, (C, C), 1)
    return dict(
        same_head=(row // C) == (col // RW_HEAD_DIM),
        incl=(tr >= tc) if not reverse else (tr <= tc),
        strict=(tr > tc) if not reverse else (tr < tc),
        diag_blk=(tr // SCAN_SUB) == (tc // SCAN_SUB),
        eye=row == col,
        cum_mat=jnp.where((t_i >= s_i) if not reverse else (t_i <= s_i), 1.0, 0.0).astype(F32))


def _scan_chunk(chains, states):
    C = SCAN_CHUNK
    n = range(len(chains))
    mk = [ch[6] for ch in chains]
    cum = [ch[8] for ch in chains]
    pre = []
    for i in n:
        r, lw, k, v, kk, a, m, reverse, _ = chains[i]
        g = jnp.exp(cum[i])
        g_prev = jnp.exp(cum[i] - lw)
        g_inv = jnp.exp(-cum[i])
        g_end = jnp.exp(cum[i][C - 1:C, :] if not reverse else cum[i][0:1, :])
        bt = kk * a * g_inv
        kt = k * g_inv
        stack = lambda x, m=m: jnp.where(m["same_head"], jnp.concatenate([x, x], axis=0), 0.0)
        pre.append(dict(kap=stack(kk * g_prev), rt=stack(r * g), bt=stack(bt), kt=stack(kt),
                        v=stack(v), bh=stack(bt * g_end), kh=stack(kt * g_end), g_end=g_end))
    P = HEADS_PER_PAIR * C
    tri = [_dot_nt(jnp.concatenate([pre[i]["kap"], pre[i]["rt"]], axis=0),
                   jnp.concatenate([pre[i]["bt"], pre[i]["kt"]], axis=0)) for i in n]
    lb = [jnp.where(mk[i]["strict"], tri[i][:P, :P], 0.0) for i in n]
    lk = [jnp.where(mk[i]["strict"], tri[i][:P, P:], 0.0) for i in n]
    pb = [jnp.where(mk[i]["incl"], tri[i][P:, :P], 0.0) for i in n]
    pk = [jnp.where(mk[i]["incl"], tri[i][P:, P:], 0.0) for i in n]
    lkv = [_dot(lk[i], pre[i]["v"]) for i in n]

    d = [jnp.where(mk[i]["diag_blk"], lb[i], 0.0) for i in n]
    inv = [jnp.where(mk[i]["eye"], 1.0, 0.0) - d[i] for i in n]
    p = d
    for _ in range(3):
        p = [_dot(p[i], p[i]) for i in n]
        inv = [inv[i] + _dot(inv[i], p[i]) for i in n]
    x = [_dot(inv[i], jnp.concatenate([lb[i] - d[i], pre[i]["kap"], lkv[i]], axis=1)) for i in n]
    e = [x[i][:, :LANES] for i in n]
    rhs = [x[i][:, LANES:] for i in n]
    e2 = [_dot(e[i], e[i]) for i in n]
    t = [rhs[i] + _dot(e2[i], rhs[i]) for i in n]
    gy = [t[i] - _dot(e[i], t[i]) for i in n]

    bhgy = [_dot_tn(pre[i]["bh"], gy[i]) for i in n]
    m_mat = [jnp.where(mk[i]["eye"], pre[i]["g_end"], 0.0) - bhgy[i][:, :LANES] for i in n]
    n_mat = [_dot_tn(pre[i]["kh"], pre[i]["v"]) - bhgy[i][:, LANES:] for i in n]
    pbgy = [_dot(pb[i], gy[i]) for i in n]
    q = [pre[i]["rt"] - pbgy[i][:, :LANES] for i in n]
    z = [_dot(pk[i], pre[i]["v"]) - pbgy[i][:, LANES:] for i in n]
    o_st = [_dot(q[i], states[i]) + z[i] for i in n]
    new = [_dot3(m_mat[i], states[i]) + n_mat[i] for i in n]
    return [(o_st[i][:C] + o_st[i][C:], new[i]) for i in n]


def _scan_kernel(rf, vf, kkf, lwf, af, kf, rb, vb, kkb, lwb, ab, kb, s0, of, ob, sfin, state):
    c = pl.program_id(1)
    n_pairs = rf.shape[-1] // LANES

    @pl.when(c == 0)
    def _():
        state[...] = s0[0]

    chains, states, outs = [], [], []
    for d, (r_, v_, kk_, lw_, a_, k_, o_) in enumerate(((rf, vf, kkf, lwf, af, kf, of),
                                                        (rb, vb, kkb, lwb, ab, kb, ob))):
        masks = _scan_masks(reverse=(d == 1))
        cum = _dot3(masks["cum_mat"], lw_[...])
        for p in range(n_pairs):
            sl = slice(p * LANES, (p + 1) * LANES)
            chains.append((r_[:, sl], lw_[:, sl], k_[:, sl], v_[:, sl], kk_[:, sl], a_[:, sl],
                           masks, d == 1, cum[:, sl]))
            states.append(state[d, p])
            outs.append((o_, sl, d, p))
    for (o, new), (o_, sl, d, p) in zip(_scan_chunk(chains, states), outs):
        o_[:, sl] = o
        state[d, p] = new

    @pl.when(c == pl.num_programs(1) - 1)
    def _():
        sfin[0] = state[...]


def rwkv_scan(r, v, kk, lw, a, kd, s0_bd, row0, n_seq, seq_len):
    width = r.shape[1]
    n_tok = n_seq * seq_len
    n_pairs = width // LANES
    n_chunk = seq_len // SCAN_CHUNK
    c0 = row0 // SCAN_CHUNK
    blk = (SCAN_CHUNK, width)
    dblk = (None, SCAN_CHUNK, width)
    fwd = lambda s, c: (s * n_chunk + c, 0)
    bwd = lambda s, c: (s * n_chunk + n_chunk - 1 - c, 0)
    fwd_in = lambda s, c: (c0 + s * n_chunk + c, 0)
    bwd_in = lambda s, c: (c0 + s * n_chunk + n_chunk - 1 - c, 0)
    fwd_d = lambda s, c: (0, c0 + s * n_chunk + c, 0)
    bwd_d = lambda s, c: (1, c0 + s * n_chunk + n_chunk - 1 - c, 0)
    sspec = pl.BlockSpec((1, 2, n_pairs, LANES, LANES), lambda s, c: (s, 0, 0, 0, 0))
    in_specs = ([pl.BlockSpec(blk, fwd_in)] * 3 + [pl.BlockSpec(dblk, fwd_d)] * 3
                + [pl.BlockSpec(blk, bwd_in)] * 3 + [pl.BlockSpec(dblk, bwd_d)] * 3 + [sspec])
    out_specs = [pl.BlockSpec(blk, fwd), pl.BlockSpec(blk, bwd), sspec]
    out_shape = [jax.ShapeDtypeStruct((n_tok, width), F32)] * 2 + [
        jax.ShapeDtypeStruct(s0_bd.shape, F32)]
    return pl.pallas_call(
        _scan_kernel,
        grid=(n_seq, n_chunk),
        in_specs=in_specs,
        out_specs=out_specs,
        out_shape=out_shape,
        scratch_shapes=[pltpu.VMEM((2, n_pairs, LANES, LANES), F32)],
        compiler_params=pltpu.CompilerParams(
            dimension_semantics=("parallel", "arbitrary"), vmem_limit_bytes=VMEM_LIMIT),
        name="rwkv_scan",
    )(r, v, kk, lw, a, kd, r, v, kk, lw, a, kd, s0_bd)


MOD_ROWS = 8
ROW_BLOCK = 256
RMS_EPS = 1e-6
LN_EPS = 1e-5
RW_GN_EPS = 64e-5


def _mod_kernel(c_ref, w_ref, b_ref, o_ref):
    c = c_ref[...]
    s = c * jax.nn.sigmoid(c)
    o_ref[...] = _dot(s, w_ref[...]) + b_ref[...]


def modulation(cond, w_mod, b_mod):
    n, d = cond.shape
    cols = w_mod.shape[1]
    blk = 1024
    return pl.pallas_call(
        _mod_kernel,
        grid=(cols // blk,),
        in_specs=[pl.BlockSpec((n, d), lambda j: (0, 0)),
                  pl.BlockSpec((d, blk), lambda j: (0, j)),
                  pl.BlockSpec((1, blk), lambda j: (0, j))],
        out_specs=pl.BlockSpec((n, blk), lambda j: (0, j)),
        out_shape=jax.ShapeDtypeStruct((n, cols), F32),
        name="modulation",
    )(cond, w_mod, b_mod)


def _modulate(x, shift, scale):
    ms = jnp.mean(x * x, axis=-1, keepdims=True)
    return x * lax.rsqrt(ms + RMS_EPS) * (1.0 + scale) + shift


def _inproj_kernel(x_ref, mod_ref, *refs):
    n = len(refs) // 2
    h = _modulate(x_ref[...], mod_ref[0, 0:1, :], mod_ref[0, 1:2, :]).astype(BF16)
    for w_ref, o_ref in zip(refs[:n], refs[n:]):
        o_ref[...] = jnp.dot(h, w_ref[...], preferred_element_type=F32)


def in_projection(x, mod_blocks, weights):
    n_tok, d = x.shape
    row = lambda i: (i, 0)
    const = lambda i: (0, 0)
    return pl.pallas_call(
        _inproj_kernel,
        grid=(n_tok // ROW_BLOCK,),
        in_specs=[pl.BlockSpec((ROW_BLOCK, d), row),
                  pl.BlockSpec((1, MOD_ROWS, d), lambda i: (i, 0, 0))]
                 + [pl.BlockSpec(w.shape, const, pipeline_mode=pl.Buffered(1)) for w in weights],
        out_specs=[pl.BlockSpec((ROW_BLOCK, w.shape[1]), row) for w in weights],
        out_shape=[jax.ShapeDtypeStruct((n_tok, w.shape[1]), F32) for w in weights],
        compiler_params=pltpu.CompilerParams(
            dimension_semantics=("parallel",), vmem_limit_bytes=VMEM_LIMIT),
        name="in_projection",
    )(x, mod_blocks, *weights)


HEAD_DIM = 64


def _group_ones(width):
    r = lax.broadcasted_iota(jnp.int32, (width, width), 0) // HEAD_DIM
    c = lax.broadcasted_iota(jnp.int32, (width, width), 1) // HEAD_DIM
    return jnp.where(r == c, 1.0, 0.0).astype(BF16)


def _group_sum(x, ones):
    hi = x.astype(BF16)
    lo = (x - hi.astype(F32)).astype(BF16)
    d = functools.partial(jnp.dot, preferred_element_type=F32)
    return d(hi, ones) + d(lo, ones)


RW_WIDTH = 512
RW_COLS = 3 * RW_WIDTH + 3 * LANES
RW_VEC_ROWS = 8


def _rwkv_prep_kernel(z_ref, hp_ref, hn_ref, mu_ref, vec_ref, w2_ref, a2_ref, g2_ref,
                      r_ref, v_ref, kk_ref, lw_ref, a_ref, kd_ref, bonus_ref, gate_ref):
    z = z_ref[...]
    rb = z.shape[0]
    row = lax.broadcasted_iota(jnp.int32, (rb, 1), 0)
    z_prev = jnp.where(row == 0, hp_ref[0], pltpu.roll(z, 1, 0))
    z_next = jnp.where(row == rb - 1, hn_ref[0], pltpu.roll(z, rb - 1, 0))
    zs = z + mu_ref[0:1, :] * (z_prev - z) + mu_ref[1:2, :] * (z_next - z)
    w = RW_WIDTH
    r, k, v = zs[:, :w], zs[:, w:2 * w], zs[:, 2 * w:3 * w]
    zw = zs[:, 3 * w:3 * w + LANES]
    za = zs[:, 3 * w + LANES:3 * w + 2 * LANES]
    zg = zs[:, 3 * w + 2 * LANES:]
    ones = _group_ones(w)
    kk = k * vec_ref[0:1, :]
    kk = kk * lax.rsqrt(_group_sum(kk * kk, ones) + 1e-12)
    r_ref[...] = r
    v_ref[...] = v
    kk_ref[...] = kk
    tz = jnp.tanh(zw)
    for d in range(2):
        w_raw = vec_ref[3 + d:4 + d, :] + _dot(tz, w2_ref[d])
        lw_ref[d] = -math.exp(-0.5) * jax.nn.sigmoid(w_raw)
        a = jax.nn.sigmoid(vec_ref[5 + d:6 + d, :] + _dot(za, a2_ref[d]))
        a_ref[d] = a
        kd_ref[d] = k * (1.0 + (a - 1.0) * vec_ref[1:2, :])
    bonus_ref[...] = _group_sum(r * k * vec_ref[2:3, :], ones) * v
    gate_ref[...] = _dot(jax.nn.sigmoid(zg), g2_ref[...])


def rwkv_prep(z_rw, halo_prev, halo_next, mu, vecs, w2_pad, a2_pad, g2):
    n_tok = z_rw.shape[0]
    w = RW_WIDTH
    row = lambda i: (i, 0)
    const2 = lambda i: (0, 0)
    const3 = lambda i: (0, 0, 0)
    halo = pl.BlockSpec((1, 1, RW_COLS), lambda i: (i, 0, 0))
    one = pl.BlockSpec((ROW_BLOCK, w), row)
    two = pl.BlockSpec((2, ROW_BLOCK, w), lambda i: (0, i, 0))
    s1 = jax.ShapeDtypeStruct((n_tok, w), F32)
    s2 = jax.ShapeDtypeStruct((2, n_tok, w), F32)
    return pl.pallas_call(
        _rwkv_prep_kernel,
        grid=(n_tok // ROW_BLOCK,),
        in_specs=[pl.BlockSpec((ROW_BLOCK, RW_COLS), row), halo, halo,
                  pl.BlockSpec(mu.shape, const2), pl.BlockSpec(vecs.shape, const2),
                  pl.BlockSpec(w2_pad.shape, const3), pl.BlockSpec(a2_pad.shape, const3),
                  pl.BlockSpec(g2.shape, const2)],
        out_specs=[one, one, one, two, two, two, one, one],
        out_shape=[s1, s1, s1, s2, s2, s2, s1, s1],
        compiler_params=pltpu.CompilerParams(
            dimension_semantics=("parallel",), vmem_limit_bytes=VMEM_LIMIT),
        name="rwkv_prep",
    )(z_rw, halo_prev, halo_next, mu, vecs, w2_pad, a2_pad, g2)


CONV_WIDTH = 512
CONV_KERNEL = 31
CONV_PAD = 16
CONV_ROWS = 64


def _conv_kernel(z_ref, w_ref, vec_ref, o_ref, pad_ref):
    t = z_ref.shape[0]
    cw = CONV_WIDTH
    half = CONV_KERNEL // 2
    zeros = jnp.zeros((CONV_PAD, cw), F32)
    pad_ref[0:CONV_PAD, :] = zeros
    pad_ref[CONV_PAD + t:CONV_PAD + t + CONV_PAD, :] = zeros
    pad_ref[CONV_PAD:CONV_PAD + t, :] = z_ref[:, :cw] * jax.nn.sigmoid(z_ref[:, cw:])

    def chunk(i, carry):
        base = pl.multiple_of(i * CONV_ROWS, CONV_ROWS)
        window = pad_ref[pl.ds(base, CONV_ROWS + 2 * CONV_PAD), :]
        span = CONV_ROWS + 2 * CONV_PAD - SUBLANES
        shifted = [window[rho:rho + span] for rho in range(SUBLANES)]
        acc = jnp.zeros((CONV_ROWS, cw), F32)
        for j in range(CONV_KERNEL):
            lo = CONV_PAD - half + j
            tile = (lo // SUBLANES) * SUBLANES
            acc = acc + w_ref[j:j + 1, :] * shifted[lo % SUBLANES][tile:tile + CONV_ROWS]
        u = acc + vec_ref[0:1, :]
        mu = jnp.mean(u, axis=-1, keepdims=True)
        cen = u - mu
        var = jnp.mean(cen * cen, axis=-1, keepdims=True)
        y = cen * lax.rsqrt(var + LN_EPS) * vec_ref[1:2, :] + vec_ref[2:3, :]
        o_ref[pl.ds(base, CONV_ROWS), :] = y * jax.nn.sigmoid(y)
        return carry

    lax.fori_loop(0, t // CONV_ROWS, chunk, 0)


def conv_module(z_cv, conv_w, vecs, row0, n_seq, seq_len):
    cw = CONV_WIDTH
    blk0 = row0 // seq_len
    return pl.pallas_call(
        _conv_kernel,
        grid=(n_seq,),
        in_specs=[pl.BlockSpec((seq_len, 2 * cw), lambda i: (blk0 + i, 0)),
                  pl.BlockSpec(conv_w.shape, lambda i: (0, 0)),
                  pl.BlockSpec(vecs.shape, lambda i: (0, 0))],
        out_specs=pl.BlockSpec((seq_len, cw), lambda i: (i, 0)),
        out_shape=jax.ShapeDtypeStruct((n_seq * seq_len, cw), F32),
        scratch_shapes=[pltpu.VMEM((seq_len + 2 * CONV_PAD, cw), F32)],
        compiler_params=pltpu.CompilerParams(
            dimension_semantics=("parallel",), vmem_limit_bytes=VMEM_LIMIT),
        name="conv_module",
    )(z_cv, conv_w, vecs)


ATT_Q_BLOCK = 256
ATT_HEADS = 8
ATT_KV_HEADS = 2
ATT_GROUP = ATT_HEADS // ATT_KV_HEADS


def _rot_partner():
    r = lax.broadcasted_iota(jnp.int32, (LANES, LANES), 0)
    c = lax.broadcasted_iota(jnp.int32, (LANES, LANES), 1)
    lo = (r % 32) < 16
    return jnp.where((c == r + 16) & lo, 1.0, jnp.where((c == r - 16) & ~lo, -1.0, 0.0)).astype(BF16)


def _attn_kernel(*refs, rotary):
    if rotary:
        (zq_ref, zkv_ref, nq_ref, nk_ref, cosq_ref, sinq_ref, cosk_ref, sink_ref, ck_ref, cv_ref,
         o_ref) = refs
    else:
        zq_ref, zkv_ref, nq_ref, nk_ref, o_ref, kn_ref = refs
    tq = zq_ref.shape[0]
    ones = _group_ones(LANES)
    lane = lax.broadcasted_iota(jnp.int32, (1, LANES), 1)
    first = lane < HEAD_DIM
    row_head = lax.broadcasted_iota(jnp.int32, (2 * tq, LANES), 0) // tq
    own = row_head == (lax.broadcasted_iota(jnp.int32, (2 * tq, LANES), 1) // HEAD_DIM)

    def norm(x, g):
        return x * lax.rsqrt(_group_sum(x * x, ones) * (1.0 / HEAD_DIM) + RMS_EPS) * g

    def rope(x, cos, sin):
        return x * cos + jnp.dot(x.astype(BF16), _rot_partner(), preferred_element_type=F32) * sin

    def dup(x):
        sw = pltpu.roll(x, HEAD_DIM, 1)
        return jnp.where(first, x, sw), jnp.where(first, sw, x)

    k = norm(zkv_ref[:, :LANES], nk_ref[...])
    v = zkv_ref[:, LANES:]
    if rotary:
        k = rope(k, cosk_ref[...], sink_ref[...])
        ck, cv = dup(ck_ref[...]), dup(cv_ref[...])
    else:
        kn_ref[...] = k
    kd, vd = dup(k), dup(v)
    for pair in range(ATT_HEADS // 2):
        g = pair // (ATT_GROUP // 2)
        q = norm(zq_ref[:, pair * LANES:(pair + 1) * LANES], nq_ref[...])
        if rotary:
            q = rope(q, cosq_ref[...], sinq_ref[...])
        q = q * (HEAD_DIM ** -0.5)
        qs = jnp.where(own, jnp.concatenate([q, q], axis=0), 0.0)
        s = _dot_nt(qs, kd[g])
        m = jnp.max(s, axis=-1, keepdims=True)
        if rotary:
            s2 = _dot_nt(qs, ck[g])
            m = jnp.maximum(m, jnp.max(s2, axis=-1, keepdims=True))
            p2 = jnp.exp(s2 - m)
        p = jnp.exp(s - m)
        den = jnp.sum(p, axis=-1, keepdims=True)
        o = _dot(p, vd[g])
        if rotary:
            den = den + jnp.sum(p2, axis=-1, keepdims=True)
            o = o + _dot(p2, cv[g])
        o = o / den
        o_ref[:, pair * LANES:(pair + 1) * LANES] = jnp.where(first, o[:tq], o[tq:])


def attention(z_q, z_kv, nq, nk, row0, n_seq, seq_len, rope=None, cache=None):
    qb = min(ATT_Q_BLOCK, seq_len)
    nqb = seq_len // qb
    qblk0 = row0 // qb
    sblk0 = row0 // seq_len
    wq = z_q.shape[1]
    in_specs = [pl.BlockSpec((qb, wq), lambda b, t: (qblk0 + b * nqb + t, 0)),
                pl.BlockSpec((seq_len, 2 * LANES), lambda b, t: (sblk0 + b, 0)),
                pl.BlockSpec((1, LANES), lambda b, t: (0, 0)),
                pl.BlockSpec((1, LANES), lambda b, t: (0, 0))]
    args = [z_q, z_kv, nq, nk]
    out_q = pl.BlockSpec((qb, wq), lambda b, t: (b * nqb + t, 0))
    y_shape = jax.ShapeDtypeStruct((n_seq * seq_len, wq), F32)
    if rope is None:
        out_specs = [out_q, pl.BlockSpec((seq_len, LANES), lambda b, t: (b, 0))]
        out_shape = [y_shape, jax.ShapeDtypeStruct((n_seq * seq_len, LANES), F32)]
    else:
        past = cache[0].shape[0] // n_seq
        in_specs += [pl.BlockSpec((qb, LANES), lambda b, t: (t, 0))] * 2
        in_specs += [pl.BlockSpec((seq_len, LANES), lambda b, t: (0, 0))] * 2
        in_specs += [pl.BlockSpec((past, LANES), lambda b, t: (b, 0))] * 2
        args += [rope[0], rope[1], rope[0], rope[1], cache[0], cache[1]]
        out_specs = out_q
        out_shape = y_shape
    return pl.pallas_call(
        functools.partial(_attn_kernel, rotary=rope is not None),
        grid=(n_seq, nqb),
        in_specs=in_specs,
        out_specs=out_specs,
        out_shape=out_shape,
        compiler_params=pltpu.CompilerParams(
            dimension_semantics=("parallel", "arbitrary"), vmem_limit_bytes=VMEM_LIMIT),
        name="attention",
    )(*args)


def _merge_kernel(x_ref, mod_ref, bonus_ref, gate_ref, zg_ref,
                  ofc_ref, ofl_ref, obc_ref, obl_ref, cvc_ref, cvl_ref, atc_ref, atl_ref,
                  gn_ref, wr_ref, wc_ref, wa_ref, wo_ref, wq_ref, xo_ref, h2_ref, q_ref, *, n_ctx_blocks):
    d = x_ref.shape[1]
    zg = zg_ref[...]
    is_ctx = pl.program_id(0) < n_ctx_blocks
    pick = lambda c_ref, l_ref: jnp.where(is_ctx, c_ref[...], l_ref[...])
    ones = _group_ones(ofc_ref.shape[1])
    o = pick(ofc_ref, ofl_ref) + pick(obc_ref, obl_ref)
    cen = o - _group_sum(o, ones) * (1.0 / HEAD_DIM)
    var = _group_sum(cen * cen, ones) * (1.0 / HEAD_DIM)
    o = cen * lax.rsqrt(var + RW_GN_EPS) * gn_ref[0:1, :] + gn_ref[1:2, :]
    y_rw = (o + bonus_ref[...]) * gate_ref[...]
    merged = (jax.nn.sigmoid(zg[:, :d]) * _dot(y_rw, wr_ref[...])
              + jax.nn.sigmoid(zg[:, d:2 * d]) * _dot(pick(cvc_ref, cvl_ref), wc_ref[...])
              + jax.nn.sigmoid(zg[:, 2 * d:]) * _dot(pick(atc_ref, atl_ref), wa_ref[...]))
    x = x_ref[...] + mod_ref[0, 2:3, :] * _dot(merged, wo_ref[...])
    xo_ref[...] = x
    h2 = _modulate(x, mod_ref[0, 3:4, :], mod_ref[0, 4:5, :]).astype(BF16)
    h2_ref[...] = h2
    q_ref[...] = jnp.dot(h2, wq_ref[...], preferred_element_type=F32)


def merge_project(x, mod_blocks, bonus, gate, z_gate, o_f, o_b, y_cv, y_at, gn, w_r, w_c, w_a, w_o, w_q):
    n_tok, d = x.shape
    nc = o_f[0].shape[0] // ROW_BLOCK
    row = lambda i: (i, 0)
    const = lambda i: (0, 0)
    ctx_row = lambda i: (jnp.minimum(i, nc - 1), 0)
    lat_row = lambda i: (jnp.maximum(i - nc, 0), 0)
    acts = [bonus, gate, z_gate]
    weights = [gn, w_r, w_c, w_a, w_o, w_q]
    pair_specs, pair_args = [], []
    for c_arr, l_arr in (o_f, o_b, y_cv, y_at):
        pair_specs += [pl.BlockSpec((ROW_BLOCK, c_arr.shape[1]), ctx_row),
                       pl.BlockSpec((ROW_BLOCK, l_arr.shape[1]), lat_row)]
        pair_args += [c_arr, l_arr]
    return pl.pallas_call(
        functools.partial(_merge_kernel, n_ctx_blocks=nc),
        grid=(n_tok // ROW_BLOCK,),
        in_specs=[pl.BlockSpec((ROW_BLOCK, d), row),
                  pl.BlockSpec((1, MOD_ROWS, d), lambda i: (i, 0, 0))]
                 + [pl.BlockSpec((ROW_BLOCK, a.shape[1]), row) for a in acts]
                 + pair_specs
                 + [pl.BlockSpec(w.shape, const) for w in weights],
        out_specs=[pl.BlockSpec((ROW_BLOCK, d), row), pl.BlockSpec((ROW_BLOCK, d), row),
                   pl.BlockSpec((ROW_BLOCK, w_q.shape[1]), row)],
        out_shape=[jax.ShapeDtypeStruct((n_tok, d), F32), jax.ShapeDtypeStruct((n_tok, d), BF16),
                   jax.ShapeDtypeStruct((n_tok, w_q.shape[1]), F32)],
        compiler_params=pltpu.CompilerParams(
            dimension_semantics=("parallel",), vmem_limit_bytes=VMEM_LIMIT),
        name="merge_project",
    )(x, mod_blocks, *acts, *pair_args, *weights)


PEER_HEADS = 8
PEER_N_KEYS = 128
PEER_HALF = 64
PEER_TOPK = 16
PEER_SEL_BLOCK = 128
PEER_LOOP_UNROLL = 16
PEER_RELAYOUT_TOKENS = 32
PEER_RELAYOUT_KEYS = 8
NEG = -1e30


def _dot3_nt(a, b):
    a_hi = a.astype(BF16)
    b_hi = b.astype(BF16)
    a_lo = (a - a_hi.astype(F32)).astype(BF16)
    b_lo = (b - b_hi.astype(F32)).astype(BF16)
    d = lambda x, y: lax.dot_general(x, y, (((1,), (1,)), ((), ())), preferred_element_type=F32)
    return d(a_hi, b_hi) + (d(a_hi, b_lo) + d(a_lo, b_hi))


def _extract_max(s, code):
    m = jnp.max(s, axis=0, keepdims=True)
    pos = jnp.min(jnp.where(s == m, code, 1e9), axis=0, keepdims=True)
    return m, pos, jnp.where(code == pos, NEG, s)


def _pair_candidates(a0, a1):
    k = PEER_TOPK
    sub = lax.broadcasted_iota(jnp.int32, (k, a0.shape[1]), 0).astype(F32)
    sub8 = sub[:8]
    vals = [a0[0:1] + a1]
    codes = [sub]
    for r0 in range(1, 8):
        limit = k // (r0 + 1)
        vals.append(jnp.where(sub8 < limit, a0[r0:r0 + 1] + a1[:8], NEG))
        codes.append(sub8 + float(r0 * k))
    vals.append(a0[8:] + a1[0:1])
    codes.append((sub8 + 8.0) * float(k))
    return jnp.concatenate(vals, axis=0), jnp.concatenate(codes, axis=0)


def _peer_select_kernel(q_ref, keys_ref, w_ref, top_v, top_i, slot_i, slot_j, slot_g,
                        rows_i, rows_j, rows_g, w3_ref):
    k = PEER_TOPK
    ct = q_ref.shape[0]
    key_code = lax.broadcasted_iota(jnp.int32, (PEER_N_KEYS, ct), 0).astype(F32)
    rank = lax.broadcasted_iota(jnp.int32, (k, ct), 0).astype(F32)
    for h in range(PEER_HEADS):
        st = _dot3_nt(keys_ref[h], q_ref[:, h * LANES:(h + 1) * LANES])
        s = [st[:PEER_N_KEYS], st[PEER_N_KEYS:]]
        for it in range(k):
            for p in range(2):
                m, pos, s[p] = _extract_max(s[p], key_code)
                top_v[h, p, it:it + 1, :] = m
                top_i[h, p, it:it + 1, :] = pos
    for h0 in range(0, PEER_HEADS, 2):
        heads = (h0, h0 + 1)
        cand, code, best, z = {}, {}, {}, {}
        for h in heads:
            a0, a1 = top_v[h, 0], top_v[h, 1]
            cand[h], code[h] = _pair_candidates(a0, a1)
            best[h] = a0[0:1] + a1[0:1]
            z[h] = jnp.zeros_like(best[h])
        for it in range(k):
            for h in heads:
                m, pos, cand[h] = _extract_max(cand[h], code[h])
                r0 = jnp.floor(pos * (1.0 / k))
                r1 = pos - r0 * k
                e = jnp.exp(m - best[h])
                z[h] = z[h] + e
                slot = h * k + it
                slot_i[slot:slot + 1, :] = jnp.sum(jnp.where(rank == r0, top_i[h, 0], 0.0), axis=0,
                                                   keepdims=True)
                slot_j[slot:slot + 1, :] = jnp.sum(jnp.where(rank == r1, top_i[h, 1], 0.0), axis=0,
                                                   keepdims=True)
                slot_g[slot:slot + 1, :] = e
        for h in heads:
            slot_g[h * k:(h + 1) * k, :] = slot_g[h * k:(h + 1) * k, :] / z[h]

    rows_i[...] = slot_i[...].T
    rows_j[...] = slot_j[...].T
    rows_g[...] = slot_g[...].T
    n = PEER_N_KEYS
    sub = lax.broadcasted_iota(jnp.int32, (n, n), 0).astype(F32)

    def per_token(c, carry):
        a_t = jnp.where(sub == rows_i[pl.ds(c, 1), :], rows_g[pl.ds(c, 1), :], 0.0)
        b_t = jnp.where(sub == rows_j[pl.ds(c, 1), :], 1.0, 0.0)
        w3_ref[c] = _dot_nt(a_t, b_t)
        return carry

    lax.fori_loop(0, ct, per_token, 0, unroll=PEER_LOOP_UNROLL)

    tg, ig = PEER_RELAYOUT_TOKENS, PEER_RELAYOUT_KEYS
    r_out = lax.broadcasted_iota(jnp.int32, (tg * ig, tg * ig), 0)
    r_in = lax.broadcasted_iota(jnp.int32, (tg * ig, tg * ig), 1)
    perm = jnp.where(r_in == (r_out % tg) * ig + r_out // tg, 1.0, 0.0).astype(BF16)

    for ib in range(n // ig):
        for t in range(ct // tg):
            z = w3_ref[t * tg:(t + 1) * tg, ib * ig:(ib + 1) * ig, :]
            y = jnp.dot(perm, z.reshape(tg * ig, n).astype(BF16), preferred_element_type=F32)
            for kk in range(ig):
                col = (ib * ig + kk) * n
                w_ref[t * tg:(t + 1) * tg, col:col + n] = y[kk * tg:(kk + 1) * tg].astype(BF16)


def peer_select(q, keys_padded):
    n_tok, qd = q.shape
    ct = PEER_SEL_BLOCK
    n = PEER_N_KEYS
    slots = PEER_HEADS * PEER_TOPK
    return pl.pallas_call(
        _peer_select_kernel,
        grid=(n_tok // ct,),
        in_specs=[pl.BlockSpec((ct, qd), lambda i: (i, 0)),
                  pl.BlockSpec(keys_padded.shape, lambda i: (0, 0, 0))],
        out_specs=pl.BlockSpec((ct, n * n), lambda i: (i, 0)),
        out_shape=jax.ShapeDtypeStruct((n_tok, n * n), BF16),
        scratch_shapes=[pltpu.VMEM((PEER_HEADS, 2, PEER_TOPK, ct), F32),
                        pltpu.VMEM((PEER_HEADS, 2, PEER_TOPK, ct), F32),
                        pltpu.VMEM((slots, ct), F32), pltpu.VMEM((slots, ct), F32),
                        pltpu.VMEM((slots, ct), F32),
                        pltpu.VMEM((ct, slots), F32), pltpu.VMEM((ct, slots), F32),
                        pltpu.VMEM((ct, slots), F32),
                        pltpu.VMEM((ct, n, n), F32)],
        compiler_params=pltpu.CompilerParams(
            dimension_semantics=("parallel",), vmem_limit_bytes=VMEM_LIMIT),
        name="peer_select",
    )(q, keys_padded)


PEER_TOKEN_BLOCK = 1024
PEER_EXPERT_BLOCK = 1024


def _peer_kernel(h_ref, w_ref, u_ref, v_ref, x_ref, mod_ref, o_ref, acc_ref):
    e = pl.program_id(1)

    @pl.when(e == 0)
    def _():
        acc_ref[...] = jnp.zeros_like(acc_ref)

    s = _dot_nt(h_ref[...], u_ref[...])
    act = 0.5 * s * (1.0 + lax.erf(s * (2.0 ** -0.5)))
    acc_ref[...] += jnp.dot(act.astype(BF16) * w_ref[...], v_ref[...], preferred_element_type=F32)

    @pl.when(e == pl.num_programs(1) - 1)
    def _():
        for j in range(mod_ref.shape[0]):
            rows = slice(j * ROW_BLOCK, (j + 1) * ROW_BLOCK)
            o_ref[rows, :] = x_ref[rows, :] + mod_ref[j, 5:6, :] * acc_ref[rows, :]


def peer_dense(h2, w_sel, u_tab, v_tab, x, mod_blocks):
    n_tok, d = x.shape
    n_exp = u_tab.shape[0]
    tb, eb = PEER_TOKEN_BLOCK, PEER_EXPERT_BLOCK
    tok = lambda i, e: (i, 0)
    return pl.pallas_call(
        _peer_kernel,
        grid=(n_tok // tb, n_exp // eb),
        in_specs=[pl.BlockSpec((tb, d), tok),
                  pl.BlockSpec((tb, eb), lambda i, e: (i, e)),
                  pl.BlockSpec((eb, d), lambda i, e: (e, 0)),
                  pl.BlockSpec((eb, d), lambda i, e: (e, 0)),
                  pl.BlockSpec((tb, d), tok),
                  pl.BlockSpec((tb // ROW_BLOCK, MOD_ROWS, d), lambda i, e: (i, 0, 0))],
        out_specs=pl.BlockSpec((tb, d), tok),
        out_shape=jax.ShapeDtypeStruct((n_tok, d), F32),
        scratch_shapes=[pltpu.VMEM((tb, d), F32)],
        compiler_params=pltpu.CompilerParams(
            dimension_semantics=("parallel", "arbitrary"), vmem_limit_bytes=VMEM_LIMIT),
        name="peer_dense",
    )(h2, w_sel, u_tab, v_tab, x, mod_blocks)


FUSED_TOKEN_BLOCK = 256
FUSED_EXPERT_BLOCK = 2048
FUSED_PHASES = 4
FUSED_RELAYOUT = 16
STAGE1_SPLIT = 5
FUSED_DENSE_CHUNKS = 4


def _stage1_items(q_ref, keys_ref, row0, heads, top_v, top_i):
    ct = PEER_SEL_BLOCK
    key_code = lax.broadcasted_iota(jnp.int32, (PEER_N_KEYS, ct), 0).astype(F32)
    st = {}
    items = []

    def score(h):
        st["both"] = _dot3_nt(keys_ref[h], q_ref[pl.ds(row0, ct), h * LANES:(h + 1) * LANES])

    def start(p):
        st["s"] = st["both"][p * PEER_N_KEYS:(p + 1) * PEER_N_KEYS]

    def step(h, p, it):
        m, pos, st["s"] = _extract_max(st["s"], key_code)
        top_v[h, p, it:it + 1, :] = m
        top_i[h, p, it:it + 1, :] = pos

    for h in heads:
        items.append(functools.partial(score, h))
        for p in range(2):
            items.append(functools.partial(start, p))
            items += [functools.partial(step, h, p, it) for it in range(PEER_TOPK)]
    return items


def _stage2_items(top_v, top_i, slot_i, slot_j, slot_g, rows_i, rows_j, rows_g):
    k = PEER_TOPK
    ct = PEER_SEL_BLOCK
    rank = lax.broadcasted_iota(jnp.int32, (k, ct), 0).astype(F32)
    st = {}
    items = []

    def start(h):
        a0, a1 = top_v[h, 0], top_v[h, 1]
        st["cand"], st["code"] = _pair_candidates(a0, a1)
        st["best"] = a0[0:1] + a1[0:1]
        st["z"] = jnp.zeros_like(st["best"])

    def step(h, it):
        m, pos, st["cand"] = _extract_max(st["cand"], st["code"])
        r0 = jnp.floor(pos * (1.0 / k))
        r1 = pos - r0 * k
        e = jnp.exp(m - st["best"])
        st["z"] = st["z"] + e
        slot = h * k + it
        slot_i[slot:slot + 1, :] = jnp.sum(jnp.where(rank == r0, top_i[h, 0], 0.0), axis=0, keepdims=True)
        slot_j[slot:slot + 1, :] = jnp.sum(jnp.where(rank == r1, top_i[h, 1], 0.0), axis=0, keepdims=True)
        slot_g[slot:slot + 1, :] = e

    def finish(h):
        slot_g[h * k:(h + 1) * k, :] = slot_g[h * k:(h + 1) * k, :] / st["z"]

    def transpose():
        rows_i[...] = slot_i[...].T
        rows_j[...] = slot_j[...].T
        rows_g[...] = slot_g[...].T

    for h in range(PEER_HEADS):
        items.append(functools.partial(start, h))
        items += [functools.partial(step, h, it) for it in range(k)]
        items.append(functools.partial(finish, h))
    items.append(transpose)
    return items


def _assemble_items(rows_i, rows_j, rows_g, w3_ref):
    n = PEER_N_KEYS
    sub = lax.broadcasted_iota(jnp.int32, (n, n), 0).astype(F32)

    def token(c):
        a_t = jnp.where(sub == rows_i[c:c + 1, :], rows_g[c:c + 1, :], 0.0)
        b_t = jnp.where(sub == rows_j[c:c + 1, :], 1.0, 0.0)
        w3_ref[c] = _dot_nt(a_t, b_t).astype(BF16)

    return [functools.partial(token, c) for c in range(PEER_SEL_BLOCK)]


def _relayout_items(w3_ref, wbuf, buf, row0):
    n = PEER_N_KEYS
    g = FUSED_RELAYOUT
    r_out = lax.broadcasted_iota(jnp.int32, (g * g, g * g), 0)
    r_in = lax.broadcasted_iota(jnp.int32, (g * g, g * g), 1)
    perm = jnp.where(r_in == (r_out % g) * g + r_out // g, 1.0, 0.0).astype(BF16)

    def slab(ib, tt):
        z = w3_ref[tt * g:(tt + 1) * g, ib * g:(ib + 1) * g, :].reshape(g * g, n)
        y = jnp.dot(perm, z, preferred_element_type=F32)
        rows = pl.ds(pl.multiple_of(row0 + tt * g, g), g)
        for kk in range(g):
            col = (ib * g + kk) * n
            wbuf[buf, rows, col:col + n] = y[kk * g:(kk + 1) * g].astype(BF16)

    return [functools.partial(slab, ib, tt) for ib in range(n // g) for tt in range(PEER_SEL_BLOCK // g)]


def _dense_items(h_ref, u_ref, v_ref, wbuf, use, col0, acc_ref):
    nc = FUSED_DENSE_CHUNKS
    cs = u_ref.shape[0] // nc
    st = {}

    def scores(c):
        st["s", c] = _dot_nt(h_ref[...], u_ref[c * cs:(c + 1) * cs, :])

    def gate(c):
        s = st.pop(("s", c))
        act = 0.5 * s * (1.0 + lax.erf(s * (2.0 ** -0.5)))
        w = wbuf[use, :, pl.ds(pl.multiple_of(col0 + c * cs, cs), cs)]
        st["g"] = act.astype(BF16) * w

    def accumulate(c):
        acc_ref[...] += jnp.dot(st["g"], v_ref[c * cs:(c + 1) * cs, :], preferred_element_type=F32)

    items = [functools.partial(scores, 0)]
    for c in range(nc):
        if c + 1 < nc:
            items.append(functools.partial(scores, c + 1))
        items += [functools.partial(gate, c), functools.partial(accumulate, c)]
    return items


def _interleave(a, b):
    done = 0
    for i, item in enumerate(a):
        item()
        upto = (i + 1) * len(b) // len(a)
        for other in b[done:upto]:
            other()
        done = upto


def _peer_fused_kernel(q_ref, keys_ref, h_ref, u_ref, v_ref, x_ref, mod_ref, o_ref,
                       acc_ref, wbuf, top_v, top_i, slot_i, slot_j, slot_g, rows_i, rows_j, rows_g,
                       w3_ref):
    t = pl.program_id(0)
    e = pl.program_id(1)
    fill = t % 2
    use = 1 - fill
    eb = u_ref.shape[0]

    @pl.when((t == 0) & (e == 0))
    def _():
        wbuf[1] = jnp.zeros(wbuf.shape[1:], BF16)

    @pl.when(e == 0)
    def _():
        acc_ref[...] = jnp.zeros_like(acc_ref)

    def dense():
        return _dense_items(h_ref, u_ref, v_ref, wbuf, use, e * eb, acc_ref)

    row0 = pl.multiple_of((e // FUSED_PHASES) * PEER_SEL_BLOCK, PEER_SEL_BLOCK)
    phase = e % FUSED_PHASES

    @pl.when(phase == 0)
    def _():
        _interleave(dense(), _stage1_items(q_ref, keys_ref, row0, range(STAGE1_SPLIT), top_v, top_i))

    @pl.when(phase == 1)
    def _():
        _interleave(dense(),
                    _stage1_items(q_ref, keys_ref, row0, range(STAGE1_SPLIT, PEER_HEADS), top_v, top_i)
                    + _stage2_items(top_v, top_i, slot_i, slot_j, slot_g, rows_i, rows_j, rows_g))

    @pl.when(phase == 2)
    def _():
        _interleave(dense(), _assemble_items(rows_i, rows_j, rows_g, w3_ref))

    @pl.when(phase == 3)
    def _():
        _interleave(dense(), _relayout_items(w3_ref, wbuf, fill, row0))

    @pl.when(e == pl.num_programs(1) - 1)
    def _():
        o_ref[...] = x_ref[...] + mod_ref[0, 5:6, :] * acc_ref[...]


def peer_fused(q, keys_padded, h2, u_tab, v_tab, x, mod_blocks):
    n_tok, d = x.shape
    n_exp = u_tab.shape[0]
    tb, eb = FUSED_TOKEN_BLOCK, FUSED_EXPERT_BLOCK
    ct = PEER_SEL_BLOCK
    n = PEER_N_KEYS
    slots = PEER_HEADS * PEER_TOPK
    n_blk = n_tok // tb
    assert tb == ROW_BLOCK and (tb // ct) * FUSED_PHASES == n_exp // eb
    cur = lambda t, e: (jnp.minimum(t, n_blk - 1), 0)
    prev = lambda t, e: (jnp.maximum(t - 1, 0), 0)
    return pl.pallas_call(
        _peer_fused_kernel,
        grid=(n_blk + 1, n_exp // eb),
        in_specs=[pl.BlockSpec((tb, q.shape[1]), cur),
                  pl.BlockSpec(keys_padded.shape, lambda t, e: (0, 0, 0)),
                  pl.BlockSpec((tb, d), prev),
                  pl.BlockSpec((eb, d), lambda t, e: (e, 0)),
                  pl.BlockSpec((eb, d), lambda t, e: (e, 0)),
                  pl.BlockSpec((tb, d), prev),
                  pl.BlockSpec((1, MOD_ROWS, d), lambda t, e: (jnp.maximum(t - 1, 0), 0, 0))],
        out_specs=pl.BlockSpec((tb, d), prev),
        out_shape=jax.ShapeDtypeStruct((n_tok, d), F32),
        scratch_shapes=[pltpu.VMEM((tb, d), F32),
                        pltpu.VMEM((2, tb, n * n), BF16),
                        pltpu.VMEM((PEER_HEADS, 2, PEER_TOPK, ct), F32),
                        pltpu.VMEM((PEER_HEADS, 2, PEER_TOPK, ct), F32),
                        pltpu.VMEM((slots, ct), F32), pltpu.VMEM((slots, ct), F32),
                        pltpu.VMEM((slots, ct), F32),
                        pltpu.VMEM((ct, slots), F32), pltpu.VMEM((ct, slots), F32),
                        pltpu.VMEM((ct, slots), F32),
                        pltpu.VMEM((ct, n, n), BF16)],
        compiler_params=pltpu.CompilerParams(
            dimension_semantics=("arbitrary", "arbitrary"), vmem_limit_bytes=VMEM_LIMIT),
        name="peer_fused",
    )(q, keys_padded, h2, u_tab, v_tab, x, mod_blocks)


RW_HEADS = RW_WIDTH // RW_HEAD_DIM
ROT_HALF = HEAD_DIM // 2
ROPE_THETA = 10000.0
GRID_W = 64


def _split(z, sizes):
    parts, start = [], 0
    for s in sizes:
        parts.append(z[..., start:start + s])
        start += s
    return parts


def _bd_from_states(s):
    a = jnp.swapaxes(s, -1, -2)
    n, d, h, m, _ = a.shape
    a = a.reshape(n, d, h // 2, 2, m, m)
    z = jnp.zeros_like(a[:, :, :, 0])
    top = jnp.concatenate([a[:, :, :, 0], z], axis=-1)
    bot = jnp.concatenate([z, a[:, :, :, 1]], axis=-1)
    return jnp.concatenate([top, bot], axis=-2)


def _states_from_bd(bd):
    n, d, p, _, _ = bd.shape
    m = RW_HEAD_DIM
    a = jnp.stack([bd[:, :, :, :m, :m], bd[:, :, :, m:, m:]], axis=3).reshape(n, d, 2 * p, m, m)
    return jnp.swapaxes(a, -1, -2)


def _rope_tables(t):
    rows = t // GRID_W
    row, col = jnp.meshgrid(jnp.arange(rows), jnp.arange(GRID_W), indexing='ij')
    inv = ROPE_THETA ** (-jnp.arange(0, ROT_HALF, 2, dtype=F32) / ROT_HALF)
    ang_r = row.reshape(-1, 1).astype(F32) * inv
    ang_c = col.reshape(-1, 1).astype(F32) * inv
    cr, sr, cc, sc = jnp.cos(ang_r), jnp.sin(ang_r), jnp.cos(ang_c), jnp.sin(ang_c)
    cos = jnp.concatenate([cr, cr, cc, cc], axis=-1)
    sin = jnp.concatenate([sr, sr, sc, sc], axis=-1)
    return jnp.tile(cos, (1, LANES // HEAD_DIM)), jnp.tile(sin, (1, LANES // HEAD_DIM))


def _pad_rows(w, top):
    z = jnp.zeros_like(w)
    return jnp.concatenate([w, z] if top else [z, w], axis=0)


def kernel(x_prompt, x_sample, c, cache_k, cache_v, state_wkv, c_ctx, w_mod, b_mod, w_in, shift_mu_prev, shift_mu_next, rw_w0, rw_w2, rw_a0, rw_a2, rw_g2, rw_k_k, rw_k_a, rw_r_k, rw_gn_w, rw_gn_b, w_out_rwkv, conv_w, conv_b, conv_ln_w, conv_ln_b, w_out_conv, q_norm, k_norm, w_out_attn, w_o, peer_wq, peer_keys, peer_u, peer_v):
    nb, seq, d = x_prompt.shape
    db, dseq, _ = x_sample.shape
    depth = w_in.shape[0]
    n_ctx = nb * seq
    n_lat = db * dseq
    n_tok = n_ctx + n_lat
    x = jnp.concatenate([x_prompt.reshape(n_ctx, d), x_sample.reshape(n_lat, d)], axis=0)

    cond = jnp.concatenate([c_ctx[None], c, jnp.zeros((MOD_ROWS - 1 - db, d), F32)], axis=0)
    blk_cond = jnp.concatenate([jnp.zeros((n_ctx // ROW_BLOCK,), jnp.int32),
                                1 + jnp.arange(n_lat // ROW_BLOCK, dtype=jnp.int32) // (dseq // ROW_BLOCK)])
    n_blk = n_tok // ROW_BLOCK
    blk_row = jnp.arange(n_blk) * ROW_BLOCK
    blk_pos = jnp.where(blk_row < n_ctx, blk_row % seq, (blk_row - n_ctx) % dseq)
    blk_len = jnp.where(blk_row < n_ctx, seq, dseq)
    starts_seq = (blk_pos == 0)[:, None]
    ends_seq = (blk_pos + ROW_BLOCK == blk_len)[:, None]
    rope = _rope_tables(dseq)
    in_sizes = (RW_COLS, 2 * CONV_WIDTH, ATT_HEADS * HEAD_DIM, 2 * ATT_KV_HEADS * HEAD_DIM, 3 * d)
    past = cache_k.shape[2]

    ctx_k, ctx_v, ctx_s = [], [], []
    for l in range(depth):
        mod = modulation(cond, w_mod[l], b_mod[l][None])
        mod = mod.reshape(MOD_ROWS, 6, d)[blk_cond]
        mod_blocks = jnp.concatenate([mod, jnp.zeros((mod.shape[0], MOD_ROWS - 6, d), F32)], axis=1)

        z_rw, z_cv, z_q, z_kv, z_gate = in_projection(x, mod_blocks, _split(w_in[l].astype(BF16), in_sizes))

        zb = z_rw.reshape(n_blk, ROW_BLOCK, RW_COLS)
        halo_prev = jnp.where(starts_seq, 0.0, jnp.roll(zb[:, -1], 1, axis=0))[:, None]
        halo_next = jnp.where(ends_seq, 0.0, jnp.roll(zb[:, 0], -1, axis=0))[:, None]
        mu = jnp.stack([shift_mu_prev[l], shift_mu_next[l]])
        vecs = jnp.stack([rw_k_k[l], rw_k_a[l], rw_r_k[l], rw_w0[l, 0], rw_w0[l, 1], rw_a0[l, 0],
                          rw_a0[l, 1], jnp.zeros((RW_WIDTH,), F32)])
        w2_pad = jnp.stack([_pad_rows(rw_w2[l, 0], True), _pad_rows(rw_w2[l, 1], False)]).astype(BF16)
        a2_pad = jnp.stack([_pad_rows(rw_a2[l, 0], True), _pad_rows(rw_a2[l, 1], False)]).astype(BF16)
        r, v, kk, lw, aa, kd, bonus, gate = rwkv_prep(z_rw, halo_prev, halo_next, mu, vecs, w2_pad, a2_pad,
                                                      rw_g2[l].astype(BF16))
        s0_ctx = jnp.zeros((nb, 2, RW_HEADS // 2, LANES, LANES), F32)
        of_c, ob_c, sfin = rwkv_scan(r, v, kk, lw, aa, kd, s0_ctx, 0, nb, seq)
        of_l, ob_l, _ = rwkv_scan(r, v, kk, lw, aa, kd, _bd_from_states(state_wkv[:, l]), n_ctx, db, dseq)
        ctx_s.append(_states_from_bd(sfin))

        cvec = jnp.stack([conv_b[l], conv_ln_w[l], conv_ln_b[l]] + [jnp.zeros((CONV_WIDTH,), F32)] * 5)
        y_cv = (conv_module(z_cv, conv_w[l], cvec, 0, nb, seq),
                conv_module(z_cv, conv_w[l], cvec, n_ctx, db, dseq))

        nq = jnp.tile(q_norm[l], LANES // HEAD_DIM)[None]
        nk = jnp.tile(k_norm[l], LANES // HEAD_DIM)[None]
        y_c, k_ctx = attention(z_q, z_kv, nq, nk, 0, nb, seq)
        y_l = attention(z_q, z_kv, nq, nk, n_ctx, db, dseq, rope=rope,
                        cache=(cache_k[:, l].reshape(db * past, LANES), cache_v[:, l].reshape(db * past, LANES)))
        ctx_k.append(k_ctx.reshape(nb, seq, ATT_KV_HEADS, HEAD_DIM))
        ctx_v.append(z_kv[:n_ctx, LANES:].reshape(nb, seq, ATT_KV_HEADS, HEAD_DIM))

        x, h2, pq = merge_project(x, mod_blocks, bonus, gate, z_gate,
                                  (of_c, of_l), (ob_c, ob_l), y_cv, (y_c, y_l),
                                  jnp.stack([rw_gn_w[l], rw_gn_b[l]]),
                                  w_out_rwkv[l].astype(BF16), w_out_conv[l].astype(BF16),
                                  w_out_attn[l].astype(BF16), w_o[l].astype(BF16), peer_wq[l].astype(BF16))

        kz = jnp.zeros_like(peer_keys[l][:, 0])
        keys_padded = jnp.concatenate([jnp.concatenate([peer_keys[l][:, 0], kz], axis=-1),
                                       jnp.concatenate([kz, peer_keys[l][:, 1]], axis=-1)], axis=1)
        w_sel = peer_select(pq, keys_padded)
        x = peer_dense(h2, w_sel, peer_u[l].astype(BF16), peer_v[l].astype(BF16), x, mod_blocks)

    new_cache_k = jnp.stack(ctx_k, axis=1)
    new_cache_v = jnp.stack(ctx_v, axis=1)
    new_state = jnp.stack(ctx_s, axis=1)
    return (x[:n_ctx].reshape(nb, seq, d), x[n_ctx:].reshape(db, dseq, d),
            new_cache_k, new_cache_v, new_state)
```

```python
import functools
import math

import jax
import jax.numpy as jnp
from jax import lax
from jax.experimental import pallas as pl
from jax.experimental.pallas import tpu as pltpu

F32 = jnp.float32
BF16 = jnp.bfloat16

LANES = 128
SUBLANES = 8
SCAN_CHUNK = 64
SCAN_SUB = 16
RW_HEAD_DIM = 64
HEADS_PER_PAIR = LANES // RW_HEAD_DIM
VMEM_LIMIT = 56 * 1024 * 1024


def _dot(a, b):
    return jnp.dot(a.astype(BF16), b.astype(BF16), preferred_element_type=F32)


def _dot_nt(a, b):
    return lax.dot_general(a.astype(BF16), b.astype(BF16), (((1,), (1,)), ((), ())),
                           preferred_element_type=F32)


def _dot_tn(a, b):
    return lax.dot_general(a.astype(BF16), b.astype(BF16), (((0,), (0,)), ((), ())),
                           preferred_element_type=F32)


def _dot3(a, b):
    a_hi = a.astype(BF16)
    b_hi = b.astype(BF16)
    a_lo = (a - a_hi.astype(F32)).astype(BF16)
    b_lo = (b - b_hi.astype(F32)).astype(BF16)
    d = functools.partial(jnp.dot, preferred_element_type=F32)
    return d(a_hi, b_hi) + (d(a_hi, b_lo) + d(a_lo, b_hi))


def _scan_masks(reverse):
    C = SCAN_CHUNK
    P = HEADS_PER_PAIR * C
    row = lax.broadcasted_iota(jnp.int32, (P, LANES), 0)
    col = lax.broadcasted_iota(jnp.int32, (P, LANES), 1)
    tr = row % C
    tc = col % C
    t_i = lax.broadcasted_iota(jnp.int32, (C, C), 0)
    s_i = lax.broadcasted_iota(jnp.int32, (C, C), 1)
    return dict(
        same_head=(row // C) == (col // RW_HEAD_DIM),
        incl=(tr >= tc) if not reverse else (tr <= tc),
        strict=(tr > tc) if not reverse else (tr < tc),
        diag_blk=(tr // SCAN_SUB) == (tc // SCAN_SUB),
        eye=row == col,
        cum_mat=jnp.where((t_i >= s_i) if not reverse else (t_i <= s_i), 1.0, 0.0).astype(F32))


def _scan_chunk(chains, states):
    C = SCAN_CHUNK
    n = range(len(chains))
    mk = [ch[6] for ch in chains]
    cum = [ch[8] for ch in chains]
    pre = []
    for i in n:
        r, lw, k, v, kk, a, m, reverse, _ = chains[i]
        g = jnp.exp(cum[i])
        g_prev = jnp.exp(cum[i] - lw)
        g_inv = jnp.exp(-cum[i])
        g_end = jnp.exp(cum[i][C - 1:C, :] if not reverse else cum[i][0:1, :])
        bt = kk * a * g_inv
        kt = k * g_inv
        stack = lambda x, m=m: jnp.where(m["same_head"], jnp.concatenate([x, x], axis=0), 0.0)
        pre.append(dict(kap=stack(kk * g_prev), rt=stack(r * g), bt=stack(bt), kt=stack(kt),
                        v=stack(v), bh=stack(bt * g_end), kh=stack(kt * g_end), g_end=g_end))
    P = HEADS_PER_PAIR * C
    tri = [_dot_nt(jnp.concatenate([pre[i]["kap"], pre[i]["rt"]], axis=0),
                   jnp.concatenate([pre[i]["bt"], pre[i]["kt"]], axis=0)) for i in n]
    lb = [jnp.where(mk[i]["strict"], tri[i][:P, :P], 0.0) for i in n]
    lk = [jnp.where(mk[i]["strict"], tri[i][:P, P:], 0.0) for i in n]
    pb = [jnp.where(mk[i]["incl"], tri[i][P:, :P], 0.0) for i in n]
    pk = [jnp.where(mk[i]["incl"], tri[i][P:, P:], 0.0) for i in n]
    lkv = [_dot(lk[i], pre[i]["v"]) for i in n]

    d = [jnp.where(mk[i]["diag_blk"], lb[i], 0.0) for i in n]
    inv = [jnp.where(mk[i]["eye"], 1.0, 0.0) - d[i] for i in n]
    p = d
    for _ in range(3):
        p = [_dot(p[i], p[i]) for i in n]
        inv = [inv[i] + _dot(inv[i], p[i]) for i in n]
    x = [_dot(inv[i], jnp.concatenate([lb[i] - d[i], pre[i]["kap"], lkv[i]], axis=1)) for i in n]
    e = [x[i][:, :LANES] for i in n]
    rhs = [x[i][:, LANES:] for i in n]
    e2 = [_dot(e[i], e[i]) for i in n]
    t = [rhs[i] + _dot(e2[i], rhs[i]) for i in n]
    gy = [t[i] - _dot(e[i], t[i]) for i in n]

    bhgy = [_dot_tn(pre[i]["bh"], gy[i]) for i in n]
    m_mat = [jnp.where(mk[i]["eye"], pre[i]["g_end"], 0.0) - bhgy[i][:, :LANES] for i in n]
    n_mat = [_dot_tn(pre[i]["kh"], pre[i]["v"]) - bhgy[i][:, LANES:] for i in n]
    pbgy = [_dot(pb[i], gy[i]) for i in n]
    q = [pre[i]["rt"] - pbgy[i][:, :LANES] for i in n]
    z = [_dot(pk[i], pre[i]["v"]) - pbgy[i][:, LANES:] for i in n]
    o_st = [_dot(q[i], states[i]) + z[i] for i in n]
    new = [_dot3(m_mat[i], states[i]) + n_mat[i] for i in n]
    return [(o_st[i][:C] + o_st[i][C:], new[i]) for i in n]


def _scan_kernel(rf, vf, kkf, lwf, af, kf, rb, vb, kkb, lwb, ab, kb, s0, of, ob, sfin, state):
    c = pl.program_id(1)
    n_pairs = rf.shape[-1] // LANES

    @pl.when(c == 0)
    def _():
        state[...] = s0[0]

    chains, states, outs = [], [], []
    for d, (r_, v_, kk_, lw_, a_, k_, o_) in enumerate(((rf, vf, kkf, lwf, af, kf, of),
                                                        (rb, vb, kkb, lwb, ab, kb, ob))):
        masks = _scan_masks(reverse=(d == 1))
        cum = _dot3(masks["cum_mat"], lw_[...])
        for p in range(n_pairs):
            sl = slice(p * LANES, (p + 1) * LANES)
            chains.append((r_[:, sl], lw_[:, sl], k_[:, sl], v_[:, sl], kk_[:, sl], a_[:, sl],
                           masks, d == 1, cum[:, sl]))
            states.append(state[d, p])
            outs.append((o_, sl, d, p))
    for (o, new), (o_, sl, d, p) in zip(_scan_chunk(chains, states), outs):
        o_[:, sl] = o
        state[d, p] = new

    @pl.when(c == pl.num_programs(1) - 1)
    def _():
        sfin[0] = state[...]


def rwkv_scan(r, v, kk, lw, a, kd, s0_bd, row0, n_seq, seq_len):
    width = r.shape[1]
    n_tok = n_seq * seq_len
    n_pairs = width // LANES
    n_chunk = seq_len // SCAN_CHUNK
    c0 = row0 // SCAN_CHUNK
    blk = (SCAN_CHUNK, width)
    dblk = (None, SCAN_CHUNK, width)
    fwd = lambda s, c: (s * n_chunk + c, 0)
    bwd = lambda s, c: (s * n_chunk + n_chunk - 1 - c, 0)
    fwd_in = lambda s, c: (c0 + s * n_chunk + c, 0)
    bwd_in = lambda s, c: (c0 + s * n_chunk + n_chunk - 1 - c, 0)
    fwd_d = lambda s, c: (0, c0 + s * n_chunk + c, 0)
    bwd_d = lambda s, c: (1, c0 + s * n_chunk + n_chunk - 1 - c, 0)
    sspec = pl.BlockSpec((1, 2, n_pairs, LANES, LANES), lambda s, c: (s, 0, 0, 0, 0))
    in_specs = ([pl.BlockSpec(blk, fwd_in)] * 3 + [pl.BlockSpec(dblk, fwd_d)] * 3
                + [pl.BlockSpec(blk, bwd_in)] * 3 + [pl.BlockSpec(dblk, bwd_d)] * 3 + [sspec])
    out_specs = [pl.BlockSpec(blk, fwd), pl.BlockSpec(blk, bwd), sspec]
    out_shape = [jax.ShapeDtypeStruct((n_tok, width), F32)] * 2 + [
        jax.ShapeDtypeStruct(s0_bd.shape, F32)]
    return pl.pallas_call(
        _scan_kernel,
        grid=(n_seq, n_chunk),
        in_specs=in_specs,
        out_specs=out_specs,
        out_shape=out_shape,
        scratch_shapes=[pltpu.VMEM((2, n_pairs, LANES, LANES), F32)],
        compiler_params=pltpu.CompilerParams(
            dimension_semantics=("parallel", "arbitrary"), vmem_limit_bytes=VMEM_LIMIT),
        name="rwkv_scan",
    )(r, v, kk, lw, a, kd, r, v, kk, lw, a, kd, s0_bd)


MOD_ROWS = 8
ROW_BLOCK = 256
RMS_EPS = 1e-6
LN_EPS = 1e-5
RW_GN_EPS = 64e-5


def _mod_kernel(c_ref, w_ref, b_ref, o_ref):
    c = c_ref[...]
    s = c * jax.nn.sigmoid(c)
    o_ref[...] = _dot(s, w_ref[...]) + b_ref[...]


def modulation(cond, w_mod, b_mod):
    n, d = cond.shape
    cols = w_mod.shape[1]
    blk = 1024
    return pl.pallas_call(
        _mod_kernel,
        grid=(cols // blk,),
        in_specs=[pl.BlockSpec((n, d), lambda j: (0, 0)),
                  pl.BlockSpec((d, blk), lambda j: (0, j)),
                  pl.BlockSpec((1, blk), lambda j: (0, j))],
        out_specs=pl.BlockSpec((n, blk), lambda j: (0, j)),
        out_shape=jax.ShapeDtypeStruct((n, cols), F32),
        name="modulation",
    )(cond, w_mod, b_mod)


def _modulate(x, shift, scale):
    ms = jnp.mean(x * x, axis=-1, keepdims=True)
    return x * lax.rsqrt(ms + RMS_EPS) * (1.0 + scale) + shift


def _inproj_kernel(x_ref, mod_ref, w_ref, *o_refs):
    h = _modulate(x_ref[...], mod_ref[0, 0:1, :], mod_ref[0, 1:2, :]).astype(BF16)
    start = 0
    for o_ref in o_refs:
        n = o_ref.shape[1]
        o_ref[...] = jnp.dot(h, w_ref[:, start:start + n], preferred_element_type=F32)
        start += n


def in_projection(x, mod_blocks, w_in, layer, sizes):
    n_tok, d = x.shape
    row = lambda i: (i, 0)
    return pl.pallas_call(
        _inproj_kernel,
        grid=(n_tok // ROW_BLOCK,),
        in_specs=[pl.BlockSpec((ROW_BLOCK, d), row),
                  pl.BlockSpec((1, MOD_ROWS, d), lambda i: (i, 0, 0)),
                  pl.BlockSpec((None, d, w_in.shape[2]), lambda i: (layer, 0, 0),
                               pipeline_mode=pl.Buffered(1))],
        out_specs=[pl.BlockSpec((ROW_BLOCK, n), row) for n in sizes],
        out_shape=[jax.ShapeDtypeStruct((n_tok, n), F32) for n in sizes],
        compiler_params=pltpu.CompilerParams(
            dimension_semantics=("parallel",), vmem_limit_bytes=VMEM_LIMIT),
        name="in_projection",
    )(x, mod_blocks, w_in)


HEAD_DIM = 64


def _group_ones(width):
    r = lax.broadcasted_iota(jnp.int32, (width, width), 0) // HEAD_DIM
    c = lax.broadcasted_iota(jnp.int32, (width, width), 1) // HEAD_DIM
    return jnp.where(r == c, 1.0, 0.0).astype(BF16)


def _group_sum(x, ones):
    hi = x.astype(BF16)
    lo = (x - hi.astype(F32)).astype(BF16)
    d = functools.partial(jnp.dot, preferred_element_type=F32)
    return d(hi, ones) + d(lo, ones)


RW_WIDTH = 512
RW_COLS = 3 * RW_WIDTH + 3 * LANES
RW_VEC_ROWS = 8


def _rwkv_prep_kernel(z_ref, hp_ref, hn_ref, mu_ref, vec_ref, w2_ref, a2_ref, g2_ref,
                      r_ref, v_ref, kk_ref, lw_ref, a_ref, kd_ref, bonus_ref, gate_ref):
    z = z_ref[...]
    rb = z.shape[0]
    row = lax.broadcasted_iota(jnp.int32, (rb, 1), 0)
    z_prev = jnp.where(row == 0, hp_ref[0], pltpu.roll(z, 1, 0))
    z_next = jnp.where(row == rb - 1, hn_ref[0], pltpu.roll(z, rb - 1, 0))
    zs = z + mu_ref[0:1, :] * (z_prev - z) + mu_ref[1:2, :] * (z_next - z)
    w = RW_WIDTH
    r, k, v = zs[:, :w], zs[:, w:2 * w], zs[:, 2 * w:3 * w]
    zw = zs[:, 3 * w:3 * w + LANES]
    za = zs[:, 3 * w + LANES:3 * w + 2 * LANES]
    zg = zs[:, 3 * w + 2 * LANES:]
    ones = _group_ones(w)
    kk = k * vec_ref[0:1, :]
    kk = kk * lax.rsqrt(_group_sum(kk * kk, ones) + 1e-12)
    r_ref[...] = r
    v_ref[...] = v
    kk_ref[...] = kk
    tz = jnp.tanh(zw)
    for d in range(2):
        w_raw = vec_ref[3 + d:4 + d, :] + _dot(tz, w2_ref[d])
        lw_ref[d] = -math.exp(-0.5) * jax.nn.sigmoid(w_raw)
        a = jax.nn.sigmoid(vec_ref[5 + d:6 + d, :] + _dot(za, a2_ref[d]))
        a_ref[d] = a
        kd_ref[d] = k * (1.0 + (a - 1.0) * vec_ref[1:2, :])
    bonus_ref[...] = _group_sum(r * k * vec_ref[2:3, :], ones) * v
    gate_ref[...] = _dot(jax.nn.sigmoid(zg), g2_ref[...])


def rwkv_prep(z_rw, halo_prev, halo_next, mu, vecs, w2_pad, a2_pad, g2):
    n_tok = z_rw.shape[0]
    w = RW_WIDTH
    row = lambda i: (i, 0)
    const2 = lambda i: (0, 0)
    const3 = lambda i: (0, 0, 0)
    halo = pl.BlockSpec((1, 1, RW_COLS), lambda i: (i, 0, 0))
    one = pl.BlockSpec((ROW_BLOCK, w), row)
    two = pl.BlockSpec((2, ROW_BLOCK, w), lambda i: (0, i, 0))
    s1 = jax.ShapeDtypeStruct((n_tok, w), F32)
    s2 = jax.ShapeDtypeStruct((2, n_tok, w), F32)
    return pl.pallas_call(
        _rwkv_prep_kernel,
        grid=(n_tok // ROW_BLOCK,),
        in_specs=[pl.BlockSpec((ROW_BLOCK, RW_COLS), row), halo, halo,
                  pl.BlockSpec(mu.shape, const2), pl.BlockSpec(vecs.shape, const2),
                  pl.BlockSpec(w2_pad.shape, const3), pl.BlockSpec(a2_pad.shape, const3),
                  pl.BlockSpec(g2.shape, const2)],
        out_specs=[one, one, one, two, two, two, one, one],
        out_shape=[s1, s1, s1, s2, s2, s2, s1, s1],
        compiler_params=pltpu.CompilerParams(
            dimension_semantics=("parallel",), vmem_limit_bytes=VMEM_LIMIT),
        name="rwkv_prep",
    )(z_rw, halo_prev, halo_next, mu, vecs, w2_pad, a2_pad, g2)


CONV_WIDTH = 512
CONV_KERNEL = 31
CONV_PAD = 16
CONV_ROWS = 64


def _conv_kernel(z_ref, w_ref, vec_ref, o_ref, pad_ref):
    t = z_ref.shape[0]
    cw = CONV_WIDTH
    half = CONV_KERNEL // 2
    zeros = jnp.zeros((CONV_PAD, cw), F32)
    pad_ref[0:CONV_PAD, :] = zeros
    pad_ref[CONV_PAD + t:CONV_PAD + t + CONV_PAD, :] = zeros
    pad_ref[CONV_PAD:CONV_PAD + t, :] = z_ref[:, :cw] * jax.nn.sigmoid(z_ref[:, cw:])

    def chunk(i, carry):
        base = pl.multiple_of(i * CONV_ROWS, CONV_ROWS)
        window = pad_ref[pl.ds(base, CONV_ROWS + 2 * CONV_PAD), :]
        span = CONV_ROWS + 2 * CONV_PAD - SUBLANES
        shifted = [window[rho:rho + span] for rho in range(SUBLANES)]
        acc = jnp.zeros((CONV_ROWS, cw), F32)
        for j in range(CONV_KERNEL):
            lo = CONV_PAD - half + j
            tile = (lo // SUBLANES) * SUBLANES
            acc = acc + w_ref[j:j + 1, :] * shifted[lo % SUBLANES][tile:tile + CONV_ROWS]
        u = acc + vec_ref[0:1, :]
        mu = jnp.mean(u, axis=-1, keepdims=True)
        cen = u - mu
        var = jnp.mean(cen * cen, axis=-1, keepdims=True)
        y = cen * lax.rsqrt(var + LN_EPS) * vec_ref[1:2, :] + vec_ref[2:3, :]
        o_ref[pl.ds(base, CONV_ROWS), :] = y * jax.nn.sigmoid(y)
        return carry

    lax.fori_loop(0, t // CONV_ROWS, chunk, 0)


def conv_module(z_cv, conv_w, vecs, row0, n_seq, seq_len):
    cw = CONV_WIDTH
    blk0 = row0 // seq_len
    return pl.pallas_call(
        _conv_kernel,
        grid=(n_seq,),
        in_specs=[pl.BlockSpec((seq_len, 2 * cw), lambda i: (blk0 + i, 0)),
                  pl.BlockSpec(conv_w.shape, lambda i: (0, 0)),
                  pl.BlockSpec(vecs.shape, lambda i: (0, 0))],
        out_specs=pl.BlockSpec((seq_len, cw), lambda i: (i, 0)),
        out_shape=jax.ShapeDtypeStruct((n_seq * seq_len, cw), F32),
        scratch_shapes=[pltpu.VMEM((seq_len + 2 * CONV_PAD, cw), F32)],
        compiler_params=pltpu.CompilerParams(
            dimension_semantics=("parallel",), vmem_limit_bytes=VMEM_LIMIT),
        name="conv_module",
    )(z_cv, conv_w, vecs)


ATT_Q_BLOCK = 256
ATT_HEADS = 8
ATT_KV_HEADS = 2
ATT_GROUP = ATT_HEADS // ATT_KV_HEADS


def _rot_partner():
    r = lax.broadcasted_iota(jnp.int32, (LANES, LANES), 0)
    c = lax.broadcasted_iota(jnp.int32, (LANES, LANES), 1)
    lo = (r % 32) < 16
    return jnp.where((c == r + 16) & lo, 1.0, jnp.where((c == r - 16) & ~lo, -1.0, 0.0)).astype(BF16)


def _attn_kernel(*refs, rotary):
    if rotary:
        (zq_ref, zkv_ref, nq_ref, nk_ref, cosq_ref, sinq_ref, cosk_ref, sink_ref, ck_ref, cv_ref,
         o_ref) = refs
    else:
        zq_ref, zkv_ref, nq_ref, nk_ref, o_ref, kn_ref = refs
    tq = zq_ref.shape[0]
    ones = _group_ones(LANES)
    lane = lax.broadcasted_iota(jnp.int32, (1, LANES), 1)
    first = lane < HEAD_DIM
    row_head = lax.broadcasted_iota(jnp.int32, (2 * tq, LANES), 0) // tq
    own = row_head == (lax.broadcasted_iota(jnp.int32, (2 * tq, LANES), 1) // HEAD_DIM)

    def norm(x, g):
        return x * lax.rsqrt(_group_sum(x * x, ones) * (1.0 / HEAD_DIM) + RMS_EPS) * g

    def rope(x, cos, sin):
        return x * cos + jnp.dot(x.astype(BF16), _rot_partner(), preferred_element_type=F32) * sin

    def dup(x):
        sw = pltpu.roll(x, HEAD_DIM, 1)
        return jnp.where(first, x, sw), jnp.where(first, sw, x)

    k = norm(zkv_ref[:, :LANES], nk_ref[...])
    v = zkv_ref[:, LANES:]
    if rotary:
        k = rope(k, cosk_ref[...], sink_ref[...])
        ck, cv = dup(ck_ref[...]), dup(cv_ref[...])
    else:
        kn_ref[...] = k
    kd, vd = dup(k), dup(v)
    for pair in range(ATT_HEADS // 2):
        g = pair // (ATT_GROUP // 2)
        q = norm(zq_ref[:, pair * LANES:(pair + 1) * LANES], nq_ref[...])
        if rotary:
            q = rope(q, cosq_ref[...], sinq_ref[...])
        q = q * (HEAD_DIM ** -0.5)
        qs = jnp.where(own, jnp.concatenate([q, q], axis=0), 0.0)
        s = _dot_nt(qs, kd[g])
        m = jnp.max(s, axis=-1, keepdims=True)
        if rotary:
            s2 = _dot_nt(qs, ck[g])
            m = jnp.maximum(m, jnp.max(s2, axis=-1, keepdims=True))
            p2 = jnp.exp(s2 - m)
        p = jnp.exp(s - m)
        den = jnp.sum(p, axis=-1, keepdims=True)
        o = _dot(p, vd[g])
        if rotary:
            den = den + jnp.sum(p2, axis=-1, keepdims=True)
            o = o + _dot(p2, cv[g])
        o = o / den
        o_ref[:, pair * LANES:(pair + 1) * LANES] = jnp.where(first, o[:tq], o[tq:])


def attention(z_q, z_kv, nq, nk, row0, n_seq, seq_len, rope=None, cache=None):
    qb = min(ATT_Q_BLOCK, seq_len)
    nqb = seq_len // qb
    qblk0 = row0 // qb
    sblk0 = row0 // seq_len
    wq = z_q.shape[1]
    in_specs = [pl.BlockSpec((qb, wq), lambda b, t: (qblk0 + b * nqb + t, 0)),
                pl.BlockSpec((seq_len, 2 * LANES), lambda b, t: (sblk0 + b, 0)),
                pl.BlockSpec((1, LANES), lambda b, t: (0, 0)),
                pl.BlockSpec((1, LANES), lambda b, t: (0, 0))]
    args = [z_q, z_kv, nq, nk]
    out_q = pl.BlockSpec((qb, wq), lambda b, t: (b * nqb + t, 0))
    y_shape = jax.ShapeDtypeStruct((n_seq * seq_len, wq), F32)
    if rope is None:
        out_specs = [out_q, pl.BlockSpec((seq_len, LANES), lambda b, t: (b, 0))]
        out_shape = [y_shape, jax.ShapeDtypeStruct((n_seq * seq_len, LANES), F32)]
    else:
        past = cache[0].shape[0] // n_seq
        in_specs += [pl.BlockSpec((qb, LANES), lambda b, t: (t, 0))] * 2
        in_specs += [pl.BlockSpec((seq_len, LANES), lambda b, t: (0, 0))] * 2
        in_specs += [pl.BlockSpec((past, LANES), lambda b, t: (b, 0))] * 2
        args += [rope[0], rope[1], rope[0], rope[1], cache[0], cache[1]]
        out_specs = out_q
        out_shape = y_shape
    return pl.pallas_call(
        functools.partial(_attn_kernel, rotary=rope is not None),
        grid=(n_seq, nqb),
        in_specs=in_specs,
        out_specs=out_specs,
        out_shape=out_shape,
        compiler_params=pltpu.CompilerParams(
            dimension_semantics=("parallel", "arbitrary"), vmem_limit_bytes=VMEM_LIMIT),
        name="attention",
    )(*args)


def _merge_kernel(x_ref, mod_ref, bonus_ref, gate_ref, zg_ref,
                  ofc_ref, ofl_ref, obc_ref, obl_ref, cvc_ref, cvl_ref, atc_ref, atl_ref,
                  gn_ref, wr_ref, wc_ref, wa_ref, wo_ref, wq_ref, xo_ref, h2_ref, q_ref, *, n_ctx_blocks):
    d = x_ref.shape[1]
    zg = zg_ref[...]
    is_ctx = pl.program_id(0) < n_ctx_blocks
    pick = lambda c_ref, l_ref: jnp.where(is_ctx, c_ref[...], l_ref[...])
    ones = _group_ones(ofc_ref.shape[1])
    o = pick(ofc_ref, ofl_ref) + pick(obc_ref, obl_ref)
    cen = o - _group_sum(o, ones) * (1.0 / HEAD_DIM)
    var = _group_sum(cen * cen, ones) * (1.0 / HEAD_DIM)
    o = cen * lax.rsqrt(var + RW_GN_EPS) * gn_ref[0:1, :] + gn_ref[1:2, :]
    y_rw = (o + bonus_ref[...]) * gate_ref[...]
    merged = (jax.nn.sigmoid(zg[:, :d]) * _dot(y_rw, wr_ref[...])
              + jax.nn.sigmoid(zg[:, d:2 * d]) * _dot(pick(cvc_ref, cvl_ref), wc_ref[...])
              + jax.nn.sigmoid(zg[:, 2 * d:]) * _dot(pick(atc_ref, atl_ref), wa_ref[...]))
    x = x_ref[...] + mod_ref[0, 2:3, :] * _dot(merged, wo_ref[...])
    xo_ref[...] = x
    h2 = _modulate(x, mod_ref[0, 3:4, :], mod_ref[0, 4:5, :]).astype(BF16)
    h2_ref[...] = h2
    q_ref[...] = jnp.dot(h2, wq_ref[...], preferred_element_type=F32)


def merge_project(x, mod_blocks, bonus, gate, z_gate, o_f, o_b, y_cv, y_at, gn, w_r, w_c, w_a, w_o, w_q):
    n_tok, d = x.shape
    nc = o_f[0].shape[0] // ROW_BLOCK
    row = lambda i: (i, 0)
    const = lambda i: (0, 0)
    ctx_row = lambda i: (jnp.minimum(i, nc - 1), 0)
    lat_row = lambda i: (jnp.maximum(i - nc, 0), 0)
    acts = [bonus, gate, z_gate]
    weights = [gn, w_r, w_c, w_a, w_o, w_q]
    pair_specs, pair_args = [], []
    for c_arr, l_arr in (o_f, o_b, y_cv, y_at):
        pair_specs += [pl.BlockSpec((ROW_BLOCK, c_arr.shape[1]), ctx_row),
                       pl.BlockSpec((ROW_BLOCK, l_arr.shape[1]), lat_row)]
        pair_args += [c_arr, l_arr]
    return pl.pallas_call(
        functools.partial(_merge_kernel, n_ctx_blocks=nc),
        grid=(n_tok // ROW_BLOCK,),
        in_specs=[pl.BlockSpec((ROW_BLOCK, d), row),
                  pl.BlockSpec((1, MOD_ROWS, d), lambda i: (i, 0, 0))]
                 + [pl.BlockSpec((ROW_BLOCK, a.shape[1]), row) for a in acts]
                 + pair_specs
                 + [pl.BlockSpec(w.shape, const) for w in weights],
        out_specs=[pl.BlockSpec((ROW_BLOCK, d), row), pl.BlockSpec((ROW_BLOCK, d), row),
                   pl.BlockSpec((ROW_BLOCK, w_q.shape[1]), row)],
        out_shape=[jax.ShapeDtypeStruct((n_tok, d), F32), jax.ShapeDtypeStruct((n_tok, d), BF16),
                   jax.ShapeDtypeStruct((n_tok, w_q.shape[1]), F32)],
        compiler_params=pltpu.CompilerParams(
            dimension_semantics=("parallel",), vmem_limit_bytes=VMEM_LIMIT),
        name="merge_project",
    )(x, mod_blocks, *acts, *pair_args, *weights)


PEER_HEADS = 8
PEER_N_KEYS = 128
PEER_HALF = 64
PEER_TOPK = 16
PEER_SEL_BLOCK = 128
PEER_LOOP_UNROLL = 16
PEER_RELAYOUT_TOKENS = 32
PEER_RELAYOUT_KEYS = 8
NEG = -1e30


def _dot3_nt(a, b):
    a_hi = a.astype(BF16)
    b_hi = b.astype(BF16)
    a_lo = (a - a_hi.astype(F32)).astype(BF16)
    b_lo = (b - b_hi.astype(F32)).astype(BF16)
    d = lambda x, y: lax.dot_general(x, y, (((1,), (1,)), ((), ())), preferred_element_type=F32)
    return d(a_hi, b_hi) + (d(a_hi, b_lo) + d(a_lo, b_hi))


def _extract_max(s, code):
    m = jnp.max(s, axis=0, keepdims=True)
    pos = jnp.min(jnp.where(s == m, code, 1e9), axis=0, keepdims=True)
    return m, pos, jnp.where(code == pos, NEG, s)


def _pair_candidates(a0, a1):
    k = PEER_TOPK
    sub = lax.broadcasted_iota(jnp.int32, (k, a0.shape[1]), 0).astype(F32)
    sub8 = sub[:8]
    vals = [a0[0:1] + a1]
    codes = [sub]
    for r0 in range(1, 8):
        limit = k // (r0 + 1)
        vals.append(jnp.where(sub8 < limit, a0[r0:r0 + 1] + a1[:8], NEG))
        codes.append(sub8 + float(r0 * k))
    vals.append(a0[8:] + a1[0:1])
    codes.append((sub8 + 8.0) * float(k))
    return jnp.concatenate(vals, axis=0), jnp.concatenate(codes, axis=0)


def _peer_select_kernel(q_ref, keys_ref, w_ref, top_v, top_i, slot_i, slot_j, slot_g,
                        rows_i, rows_j, rows_g, w3_ref):
    k = PEER_TOPK
    ct = q_ref.shape[0]
    key_code = lax.broadcasted_iota(jnp.int32, (PEER_N_KEYS, ct), 0).astype(F32)
    rank = lax.broadcasted_iota(jnp.int32, (k, ct), 0).astype(F32)
    for h in range(PEER_HEADS):
        st = _dot3_nt(keys_ref[h], q_ref[:, h * LANES:(h + 1) * LANES])
        s = [st[:PEER_N_KEYS], st[PEER_N_KEYS:]]
        for it in range(k):
            for p in range(2):
                m, pos, s[p] = _extract_max(s[p], key_code)
                top_v[h, p, it:it + 1, :] = m
                top_i[h, p, it:it + 1, :] = pos
    for h0 in range(0, PEER_HEADS, 2):
        heads = (h0, h0 + 1)
        cand, code, best, z = {}, {}, {}, {}
        for h in heads:
            a0, a1 = top_v[h, 0], top_v[h, 1]
            cand[h], code[h] = _pair_candidates(a0, a1)
            best[h] = a0[0:1] + a1[0:1]
            z[h] = jnp.zeros_like(best[h])
        for it in range(k):
            for h in heads:
                m, pos, cand[h] = _extract_max(cand[h], code[h])
                r0 = jnp.floor(pos * (1.0 / k))
                r1 = pos - r0 * k
                e = jnp.exp(m - best[h])
                z[h] = z[h] + e
                slot = h * k + it
                slot_i[slot:slot + 1, :] = jnp.sum(jnp.where(rank == r0, top_i[h, 0], 0.0), axis=0,
                                                   keepdims=True)
                slot_j[slot:slot + 1, :] = jnp.sum(jnp.where(rank == r1, top_i[h, 1], 0.0), axis=0,
                                                   keepdims=True)
                slot_g[slot:slot + 1, :] = e
        for h in heads:
            slot_g[h * k:(h + 1) * k, :] = slot_g[h * k:(h + 1) * k, :] / z[h]

    rows_i[...] = slot_i[...].T
    rows_j[...] = slot_j[...].T
    rows_g[...] = slot_g[...].T
    n = PEER_N_KEYS
    sub = lax.broadcasted_iota(jnp.int32, (n, n), 0).astype(F32)

    def per_token(c, carry):
        a_t = jnp.where(sub == rows_i[pl.ds(c, 1), :], rows_g[pl.ds(c, 1), :], 0.0)
        b_t = jnp.where(sub == rows_j[pl.ds(c, 1), :], 1.0, 0.0)
        w3_ref[c] = _dot_nt(a_t, b_t)
        return carry

    lax.fori_loop(0, ct, per_token, 0, unroll=PEER_LOOP_UNROLL)

    tg, ig = PEER_RELAYOUT_TOKENS, PEER_RELAYOUT_KEYS
    r_out = lax.broadcasted_iota(jnp.int32, (tg * ig, tg * ig), 0)
    r_in = lax.broadcasted_iota(jnp.int32, (tg * ig, tg * ig), 1)
    perm = jnp.where(r_in == (r_out % tg) * ig + r_out // tg, 1.0, 0.0).astype(BF16)

    for ib in range(n // ig):
        for t in range(ct // tg):
            z = w3_ref[t * tg:(t + 1) * tg, ib * ig:(ib + 1) * ig, :]
            y = jnp.dot(perm, z.reshape(tg * ig, n).astype(BF16), preferred_element_type=F32)
            for kk in range(ig):
                col = (ib * ig + kk) * n
                w_ref[t * tg:(t + 1) * tg, col:col + n] = y[kk * tg:(kk + 1) * tg].astype(BF16)


def peer_select(q, keys_padded):
    n_tok, qd = q.shape
    ct = PEER_SEL_BLOCK
    n = PEER_N_KEYS
    slots = PEER_HEADS * PEER_TOPK
    return pl.pallas_call(
        _peer_select_kernel,
        grid=(n_tok // ct,),
        in_specs=[pl.BlockSpec((ct, qd), lambda i: (i, 0)),
                  pl.BlockSpec(keys_padded.shape, lambda i: (0, 0, 0))],
        out_specs=pl.BlockSpec((ct, n * n), lambda i: (i, 0)),
        out_shape=jax.ShapeDtypeStruct((n_tok, n * n), BF16),
        scratch_shapes=[pltpu.VMEM((PEER_HEADS, 2, PEER_TOPK, ct), F32),
                        pltpu.VMEM((PEER_HEADS, 2, PEER_TOPK, ct), F32),
                        pltpu.VMEM((slots, ct), F32), pltpu.VMEM((slots, ct), F32),
                        pltpu.VMEM((slots, ct), F32),
                        pltpu.VMEM((ct, slots), F32), pltpu.VMEM((ct, slots), F32),
                        pltpu.VMEM((ct, slots), F32),
                        pltpu.VMEM((ct, n, n), F32)],
        compiler_params=pltpu.CompilerParams(
            dimension_semantics=("parallel",), vmem_limit_bytes=VMEM_LIMIT),
        name="peer_select",
    )(q, keys_padded)


PEER_TOKEN_BLOCK = 1024
PEER_EXPERT_BLOCK = 1024


def _peer_kernel(h_ref, w_ref, u_ref, v_ref, x_ref, mod_ref, o_ref, acc_ref):
    e = pl.program_id(1)

    @pl.when(e == 0)
    def _():
        acc_ref[...] = jnp.zeros_like(acc_ref)

    s = _dot_nt(h_ref[...], u_ref[...])
    act = 0.5 * s * (1.0 + lax.erf(s * (2.0 ** -0.5)))
    acc_ref[...] += jnp.dot(act.astype(BF16) * w_ref[...], v_ref[...], preferred_element_type=F32)

    @pl.when(e == pl.num_programs(1) - 1)
    def _():
        for j in range(mod_ref.shape[0]):
            rows = slice(j * ROW_BLOCK, (j + 1) * ROW_BLOCK)
            o_ref[rows, :] = x_ref[rows, :] + mod_ref[j, 5:6, :] * acc_ref[rows, :]


def peer_dense(h2, w_sel, u_tabs, v_tabs, layer, x, mod_blocks):
    n_tok, d = x.shape
    n_exp = u_tabs.shape[1]
    tb, eb = PEER_TOKEN_BLOCK, PEER_EXPERT_BLOCK
    tok = lambda i, e: (i, 0)
    return pl.pallas_call(
        _peer_kernel,
        grid=(n_tok // tb, n_exp // eb),
        in_specs=[pl.BlockSpec((tb, d), tok),
                  pl.BlockSpec((tb, eb), lambda i, e: (i, e)),
                  pl.BlockSpec((None, eb, d), lambda i, e: (layer, e, 0)),
                  pl.BlockSpec((None, eb, d), lambda i, e: (layer, e, 0)),
                  pl.BlockSpec((tb, d), tok),
                  pl.BlockSpec((tb // ROW_BLOCK, MOD_ROWS, d), lambda i, e: (i, 0, 0))],
        out_specs=pl.BlockSpec((tb, d), tok),
        out_shape=jax.ShapeDtypeStruct((n_tok, d), F32),
        scratch_shapes=[pltpu.VMEM((tb, d), F32)],
        compiler_params=pltpu.CompilerParams(
            dimension_semantics=("parallel", "arbitrary"), vmem_limit_bytes=VMEM_LIMIT),
        name="peer_dense",
    )(h2, w_sel, u_tabs, v_tabs, x, mod_blocks)


RW_HEADS = RW_WIDTH // RW_HEAD_DIM
ROT_HALF = HEAD_DIM // 2
ROPE_THETA = 10000.0
GRID_W = 64


def _bd_from_states(s):
    a = jnp.swapaxes(s, -1, -2)
    n, d, h, m, _ = a.shape
    a = a.reshape(n, d, h // 2, 2, m, m)
    z = jnp.zeros_like(a[:, :, :, 0])
    top = jnp.concatenate([a[:, :, :, 0], z], axis=-1)
    bot = jnp.concatenate([z, a[:, :, :, 1]], axis=-1)
    return jnp.concatenate([top, bot], axis=-2)


def _states_from_bd(bd):
    n, d, p, _, _ = bd.shape
    m = RW_HEAD_DIM
    a = jnp.stack([bd[:, :, :, :m, :m], bd[:, :, :, m:, m:]], axis=3).reshape(n, d, 2 * p, m, m)
    return jnp.swapaxes(a, -1, -2)


def _rope_tables(t):
    rows = t // GRID_W
    row, col = jnp.meshgrid(jnp.arange(rows), jnp.arange(GRID_W), indexing='ij')
    inv = ROPE_THETA ** (-jnp.arange(0, ROT_HALF, 2, dtype=F32) / ROT_HALF)
    ang_r = row.reshape(-1, 1).astype(F32) * inv
    ang_c = col.reshape(-1, 1).astype(F32) * inv
    cr, sr, cc, sc = jnp.cos(ang_r), jnp.sin(ang_r), jnp.cos(ang_c), jnp.sin(ang_c)
    cos = jnp.concatenate([cr, cr, cc, cc], axis=-1)
    sin = jnp.concatenate([sr, sr, sc, sc], axis=-1)
    return jnp.tile(cos, (1, LANES // HEAD_DIM)), jnp.tile(sin, (1, LANES // HEAD_DIM))


def _pad_rows(w, top):
    z = jnp.zeros_like(w)
    return jnp.concatenate([w, z] if top else [z, w], axis=0)


def kernel(x_prompt, x_sample, c, cache_k, cache_v, state_wkv, c_ctx, w_mod, b_mod, w_in, shift_mu_prev, shift_mu_next, rw_w0, rw_w2, rw_a0, rw_a2, rw_g2, rw_k_k, rw_k_a, rw_r_k, rw_gn_w, rw_gn_b, w_out_rwkv, conv_w, conv_b, conv_ln_w, conv_ln_b, w_out_conv, q_norm, k_norm, w_out_attn, w_o, peer_wq, peer_keys, peer_u, peer_v):
    nb, seq, d = x_prompt.shape
    db, dseq, _ = x_sample.shape
    depth = w_in.shape[0]
    n_ctx = nb * seq
    n_lat = db * dseq
    n_tok = n_ctx + n_lat
    x = jnp.concatenate([x_prompt.reshape(n_ctx, d), x_sample.reshape(n_lat, d)], axis=0)

    cond = jnp.concatenate([c_ctx[None], c, jnp.zeros((MOD_ROWS - 1 - db, d), F32)], axis=0)
    blk_cond = jnp.concatenate([jnp.zeros((n_ctx // ROW_BLOCK,), jnp.int32),
                                1 + jnp.arange(n_lat // ROW_BLOCK, dtype=jnp.int32) // (dseq // ROW_BLOCK)])
    n_blk = n_tok // ROW_BLOCK
    blk_row = jnp.arange(n_blk) * ROW_BLOCK
    blk_pos = jnp.where(blk_row < n_ctx, blk_row % seq, (blk_row - n_ctx) % dseq)
    blk_len = jnp.where(blk_row < n_ctx, seq, dseq)
    starts_seq = (blk_pos == 0)[:, None]
    ends_seq = (blk_pos + ROW_BLOCK == blk_len)[:, None]
    rope = _rope_tables(dseq)
    in_sizes = (RW_COLS, 2 * CONV_WIDTH, ATT_HEADS * HEAD_DIM, 2 * ATT_KV_HEADS * HEAD_DIM, 3 * d)
    past = cache_k.shape[2]

    w_in_bf, u_bf, v_bf = w_in.astype(BF16), peer_u.astype(BF16), peer_v.astype(BF16)
    ctx_k, ctx_v, ctx_s = [], [], []
    for l in range(depth):
        mod = modulation(cond, w_mod[l], b_mod[l][None])
        mod = mod.reshape(MOD_ROWS, 6, d)[blk_cond]
        mod_blocks = jnp.concatenate([mod, jnp.zeros((mod.shape[0], MOD_ROWS - 6, d), F32)], axis=1)

        z_rw, z_cv, z_q, z_kv, z_gate = in_projection(x, mod_blocks, w_in_bf, l, in_sizes)

        zb = z_rw.reshape(n_blk, ROW_BLOCK, RW_COLS)
        halo_prev = jnp.where(starts_seq, 0.0, jnp.roll(zb[:, -1], 1, axis=0))[:, None]
        halo_next = jnp.where(ends_seq, 0.0, jnp.roll(zb[:, 0], -1, axis=0))[:, None]
        mu = jnp.stack([shift_mu_prev[l], shift_mu_next[l]])
        vecs = jnp.stack([rw_k_k[l], rw_k_a[l], rw_r_k[l], rw_w0[l, 0], rw_w0[l, 1], rw_a0[l, 0],
                          rw_a0[l, 1], jnp.zeros((RW_WIDTH,), F32)])
        w2_pad = jnp.stack([_pad_rows(rw_w2[l, 0], True), _pad_rows(rw_w2[l, 1], False)]).astype(BF16)
        a2_pad = jnp.stack([_pad_rows(rw_a2[l, 0], True), _pad_rows(rw_a2[l, 1], False)]).astype(BF16)
        r, v, kk, lw, aa, kd, bonus, gate = rwkv_prep(z_rw, halo_prev, halo_next, mu, vecs, w2_pad, a2_pad,
                                                      rw_g2[l].astype(BF16))
        s0_ctx = jnp.zeros((nb, 2, RW_HEADS // 2, LANES, LANES), F32)
        of_c, ob_c, sfin = rwkv_scan(r, v, kk, lw, aa, kd, s0_ctx, 0, nb, seq)
        of_l, ob_l, _ = rwkv_scan(r, v, kk, lw, aa, kd, _bd_from_states(state_wkv[:, l]), n_ctx, db, dseq)
        ctx_s.append(_states_from_bd(sfin))

        cvec = jnp.stack([conv_b[l], conv_ln_w[l], conv_ln_b[l]] + [jnp.zeros((CONV_WIDTH,), F32)] * 5)
        y_cv = (conv_module(z_cv, conv_w[l], cvec, 0, nb, seq),
                conv_module(z_cv, conv_w[l], cvec, n_ctx, db, dseq))

        nq = jnp.tile(q_norm[l], LANES // HEAD_DIM)[None]
        nk = jnp.tile(k_norm[l], LANES // HEAD_DIM)[None]
        y_c, k_ctx = attention(z_q, z_kv, nq, nk, 0, nb, seq)
        y_l = attention(z_q, z_kv, nq, nk, n_ctx, db, dseq, rope=rope,
                        cache=(cache_k[:, l].reshape(db * past, LANES), cache_v[:, l].reshape(db * past, LANES)))
        ctx_k.append(k_ctx.reshape(nb, seq, ATT_KV_HEADS, HEAD_DIM))
        ctx_v.append(z_kv[:n_ctx, LANES:].reshape(nb, seq, ATT_KV_HEADS, HEAD_DIM))

        x, h2, pq = merge_project(x, mod_blocks, bonus, gate, z_gate,
                                  (of_c, of_l), (ob_c, ob_l), y_cv, (y_c, y_l),
                                  jnp.stack([rw_gn_w[l], rw_gn_b[l]]),
                                  w_out_rwkv[l].astype(BF16), w_out_conv[l].astype(BF16),
                                  w_out_attn[l].astype(BF16), w_o[l].astype(BF16), peer_wq[l].astype(BF16))

        kz = jnp.zeros_like(peer_keys[l][:, 0])
        keys_padded = jnp.concatenate([jnp.concatenate([peer_keys[l][:, 0], kz], axis=-1),
                                       jnp.concatenate([kz, peer_keys[l][:, 1]], axis=-1)], axis=1)
        w_sel = peer_select(pq, keys_padded)
        x = peer_dense(h2, w_sel, u_bf, v_bf, l, x, mod_blocks)

    new_cache_k = jnp.stack(ctx_k, axis=1)
    new_cache_v = jnp.stack(ctx_v, axis=1)
    new_state = jnp.stack(ctx_s, axis=1)
    return (x[:n_ctx].reshape(nb, seq, d), x[n_ctx:].reshape(db, dseq, d),
            new_cache_k, new_cache_v, new_state)
```

```python
import functools
import math

import jax
import jax.numpy as jnp
from jax import lax
from jax.experimental import pallas as pl
from jax.experimental.pallas import tpu as pltpu

F32 = jnp.float32
BF16 = jnp.bfloat16

LANES = 128
SCAN_CHUNK = 64
SCAN_SUB = 16
RW_HEAD_DIM = 64
HEADS_PER_PAIR = LANES // RW_HEAD_DIM
VMEM_LIMIT = 56 * 1024 * 1024


def _dot(a, b):
    return jnp.dot(a.astype(BF16), b.astype(BF16), preferred_element_type=F32)


def _dot_nt(a, b):
    return lax.dot_general(a.astype(BF16), b.astype(BF16), (((1,), (1,)), ((), ())),
                           preferred_element_type=F32)


def _dot_tn(a, b):
    return lax.dot_general(a.astype(BF16), b.astype(BF16), (((0,), (0,)), ((), ())),
                           preferred_element_type=F32)


def _dot3(a, b):
    a_hi = a.astype(BF16)
    b_hi = b.astype(BF16)
    a_lo = (a - a_hi.astype(F32)).astype(BF16)
    b_lo = (b - b_hi.astype(F32)).astype(BF16)
    d = functools.partial(jnp.dot, preferred_element_type=F32)
    return d(a_hi, b_hi) + (d(a_hi, b_lo) + d(a_lo, b_hi))


def _scan_masks(reverse):
    C = SCAN_CHUNK
    P = HEADS_PER_PAIR * C
    row = lax.broadcasted_iota(jnp.int32, (P, LANES), 0)
    col = lax.broadcasted_iota(jnp.int32, (P, LANES), 1)
    tr = row % C
    tc = col % C
    t_i = lax.broadcasted_iota(jnp.int32, (C, C), 0)
    s_i = lax.broadcasted_iota(jnp.int32, (C, C), 1)
    return dict(
        same_head=(row // C) == (col // RW_HEAD_DIM),
        incl=(tr >= tc) if not reverse else (tr <= tc),
        strict=(tr > tc) if not reverse else (tr < tc),
        diag_blk=(tr // SCAN_SUB) == (tc // SCAN_SUB),
        eye=row == col,
        cum_mat=jnp.where((t_i >= s_i) if not reverse else (t_i <= s_i), 1.0, 0.0).astype(F32))


def _scan_chunk(chains, states):
    C = SCAN_CHUNK
    n = range(len(chains))
    mk = [ch[6] for ch in chains]
    cum = [ch[8] for ch in chains]
    pre = []
    for i in n:
        r, lw, k, v, kk, a, m, reverse, _ = chains[i]
        g = jnp.exp(cum[i])
        g_prev = jnp.exp(cum[i] - lw)
        g_inv = jnp.exp(-cum[i])
        g_end = jnp.exp(cum[i][C - 1:C, :] if not reverse else cum[i][0:1, :])
        bt = kk * a * g_inv
        kt = k * g_inv
        stack = lambda x, m=m: jnp.where(m["same_head"], jnp.concatenate([x, x], axis=0), 0.0)
        pre.append(dict(kap=stack(kk * g_prev), rt=stack(r * g), bt=stack(bt), kt=stack(kt),
                        v=stack(v), bh=stack(bt * g_end), kh=stack(kt * g_end), g_end=g_end))
    P = HEADS_PER_PAIR * C
    tri = [_dot_nt(jnp.concatenate([pre[i]["kap"], pre[i]["rt"]], axis=0),
                   jnp.concatenate([pre[i]["bt"], pre[i]["kt"]], axis=0)) for i in n]
    lb = [jnp.where(mk[i]["strict"], tri[i][:P, :P], 0.0) for i in n]
    lk = [jnp.where(mk[i]["strict"], tri[i][:P, P:], 0.0) for i in n]
    pb = [jnp.where(mk[i]["incl"], tri[i][P:, :P], 0.0) for i in n]
    pk = [jnp.where(mk[i]["incl"], tri[i][P:, P:], 0.0) for i in n]
    lkv = [_dot(lk[i], pre[i]["v"]) for i in n]

    d = [jnp.where(mk[i]["diag_blk"], lb[i], 0.0) for i in n]
    inv = [jnp.where(mk[i]["eye"], 1.0, 0.0) - d[i] for i in n]
    p = d
    for _ in range(3):
        p = [_dot(p[i], p[i]) for i in n]
        inv = [inv[i] + _dot(inv[i], p[i]) for i in n]
    x = [_dot(inv[i], jnp.concatenate([lb[i] - d[i], pre[i]["kap"], lkv[i]], axis=1)) for i in n]
    e = [x[i][:, :LANES] for i in n]
    rhs = [x[i][:, LANES:] for i in n]
    e2 = [_dot(e[i], e[i]) for i in n]
    t = [rhs[i] + _dot(e2[i], rhs[i]) for i in n]
    gy = [t[i] - _dot(e[i], t[i]) for i in n]

    bhgy = [_dot_tn(pre[i]["bh"], gy[i]) for i in n]
    m_mat = [jnp.where(mk[i]["eye"], pre[i]["g_end"], 0.0) - bhgy[i][:, :LANES] for i in n]
    n_mat = [_dot_tn(pre[i]["kh"], pre[i]["v"]) - bhgy[i][:, LANES:] for i in n]
    pbgy = [_dot(pb[i], gy[i]) for i in n]
    q = [pre[i]["rt"] - pbgy[i][:, :LANES] for i in n]
    z = [_dot(pk[i], pre[i]["v"]) - pbgy[i][:, LANES:] for i in n]
    o_st = [_dot(q[i], states[i]) + z[i] for i in n]
    new = [_dot3(m_mat[i], states[i]) + n_mat[i] for i in n]
    return [(o_st[i][:C] + o_st[i][C:], new[i]) for i in n]


def _scan_kernel(rf, vf, kkf, lwf, af, kf, rb, vb, kkb, lwb, ab, kb, s0, of, ob, sfin, state):
    c = pl.program_id(1)
    n_pairs = rf.shape[-1] // LANES

    @pl.when(c == 0)
    def _():
        state[...] = s0[0]

    chains, states, outs = [], [], []
    for d, (r_, v_, kk_, lw_, a_, k_, o_) in enumerate(((rf, vf, kkf, lwf, af, kf, of),
                                                        (rb, vb, kkb, lwb, ab, kb, ob))):
        masks = _scan_masks(reverse=(d == 1))
        cum = _dot3(masks["cum_mat"], lw_[...])
        for p in range(n_pairs):
            sl = slice(p * LANES, (p + 1) * LANES)
            chains.append((r_[:, sl], lw_[:, sl], k_[:, sl], v_[:, sl], kk_[:, sl], a_[:, sl],
                           masks, d == 1, cum[:, sl]))
            states.append(state[d, p])
            outs.append((o_, sl, d, p))
    for (o, new), (o_, sl, d, p) in zip(_scan_chunk(chains, states), outs):
        o_[:, sl] = o
        state[d, p] = new

    @pl.when(c == pl.num_programs(1) - 1)
    def _():
        sfin[0] = state[...]


def rwkv_scan(r, v, kk, lw, a, kd, s0_bd, row0, n_seq, seq_len):
    width = r.shape[1]
    n_tok = n_seq * seq_len
    n_pairs = width // LANES
    n_chunk = seq_len // SCAN_CHUNK
    c0 = row0 // SCAN_CHUNK
    blk = (SCAN_CHUNK, width)
    dblk = (None, SCAN_CHUNK, width)
    fwd = lambda s, c: (s * n_chunk + c, 0)
    bwd = lambda s, c: (s * n_chunk + n_chunk - 1 - c, 0)
    fwd_in = lambda s, c: (c0 + s * n_chunk + c, 0)
    bwd_in = lambda s, c: (c0 + s * n_chunk + n_chunk - 1 - c, 0)
    fwd_d = lambda s, c: (0, c0 + s * n_chunk + c, 0)
    bwd_d = lambda s, c: (1, c0 + s * n_chunk + n_chunk - 1 - c, 0)
    sspec = pl.BlockSpec((1, 2, n_pairs, LANES, LANES), lambda s, c: (s, 0, 0, 0, 0))
    in_specs = ([pl.BlockSpec(blk, fwd_in)] * 3 + [pl.BlockSpec(dblk, fwd_d)] * 3
                + [pl.BlockSpec(blk, bwd_in)] * 3 + [pl.BlockSpec(dblk, bwd_d)] * 3 + [sspec])
    out_specs = [pl.BlockSpec(blk, fwd), pl.BlockSpec(blk, bwd), sspec]
    out_shape = [jax.ShapeDtypeStruct((n_tok, width), F32)] * 2 + [
        jax.ShapeDtypeStruct(s0_bd.shape, F32)]
    return pl.pallas_call(
        _scan_kernel,
        grid=(n_seq, n_chunk),
        in_specs=in_specs,
        out_specs=out_specs,
        out_shape=out_shape,
        scratch_shapes=[pltpu.VMEM((2, n_pairs, LANES, LANES), F32)],
        compiler_params=pltpu.CompilerParams(
            dimension_semantics=("parallel", "arbitrary"), vmem_limit_bytes=VMEM_LIMIT),
        name="rwkv_scan",
    )(r, v, kk, lw, a, kd, r, v, kk, lw, a, kd, s0_bd)


MOD_ROWS = 8
ROW_BLOCK = 256
RMS_EPS = 1e-6
LN_EPS = 1e-5
RW_GN_EPS = 64e-5


def _mod_kernel(c_ref, w_ref, b_ref, o_ref):
    c = c_ref[...]
    s = c * jax.nn.sigmoid(c)
    o_ref[...] = _dot(s, w_ref[...]) + b_ref[...]


def modulation(cond, w_mod, b_mod):
    n, d = cond.shape
    cols = w_mod.shape[1]
    blk = 1024
    return pl.pallas_call(
        _mod_kernel,
        grid=(cols // blk,),
        in_specs=[pl.BlockSpec((n, d), lambda j: (0, 0)),
                  pl.BlockSpec((d, blk), lambda j: (0, j)),
                  pl.BlockSpec((1, blk), lambda j: (0, j))],
        out_specs=pl.BlockSpec((n, blk), lambda j: (0, j)),
        out_shape=jax.ShapeDtypeStruct((n, cols), F32),
        name="modulation",
    )(cond, w_mod, b_mod)


def _modulate(x, shift, scale):
    ms = jnp.mean(x * x, axis=-1, keepdims=True)
    return x * lax.rsqrt(ms + RMS_EPS) * (1.0 + scale) + shift


def _inproj_kernel(x_ref, mod_ref, w_ref, *o_refs):
    h = _modulate(x_ref[...], mod_ref[0, 0:1, :], mod_ref[0, 1:2, :]).astype(BF16)
    start = 0
    for o_ref in o_refs:
        n = o_ref.shape[1]
        o_ref[...] = jnp.dot(h, w_ref[:, start:start + n], preferred_element_type=F32)
        start += n


def in_projection(x, mod_blocks, w_in, layer, sizes):
    n_tok, d = x.shape
    row = lambda i: (i, 0)
    return pl.pallas_call(
        _inproj_kernel,
        grid=(n_tok // ROW_BLOCK,),
        in_specs=[pl.BlockSpec((ROW_BLOCK, d), row),
                  pl.BlockSpec((1, MOD_ROWS, d), lambda i: (i, 0, 0)),
                  pl.BlockSpec((None, d, w_in.shape[2]), lambda i: (layer, 0, 0),
                               pipeline_mode=pl.Buffered(1))],
        out_specs=[pl.BlockSpec((ROW_BLOCK, n), row) for n in sizes],
        out_shape=[jax.ShapeDtypeStruct((n_tok, n), F32) for n in sizes],
        compiler_params=pltpu.CompilerParams(
            dimension_semantics=("parallel",), vmem_limit_bytes=VMEM_LIMIT),
        name="in_projection",
    )(x, mod_blocks, w_in)


HEAD_DIM = 64


def _group_ones(width):
    r = lax.broadcasted_iota(jnp.int32, (width, width), 0) // HEAD_DIM
    c = lax.broadcasted_iota(jnp.int32, (width, width), 1) // HEAD_DIM
    return jnp.where(r == c, 1.0, 0.0).astype(BF16)


def _group_sum(x, ones):
    hi = x.astype(BF16)
    lo = (x - hi.astype(F32)).astype(BF16)
    d = functools.partial(jnp.dot, preferred_element_type=F32)
    return d(hi, ones) + d(lo, ones)


RW_WIDTH = 512
RW_COLS = 3 * RW_WIDTH + 3 * LANES
RW_VEC_ROWS = 8


def _rwkv_prep_kernel(z_ref, hp_ref, hn_ref, mu_ref, vec_ref, w2_ref, a2_ref, g2_ref,
                      r_ref, v_ref, kk_ref, lw_ref, a_ref, kd_ref, bonus_ref, gate_ref):
    z = z_ref[...]
    rb = z.shape[0]
    row = lax.broadcasted_iota(jnp.int32, (rb, 1), 0)
    z_prev = jnp.where(row == 0, hp_ref[0], pltpu.roll(z, 1, 0))
    z_next = jnp.where(row == rb - 1, hn_ref[0], pltpu.roll(z, rb - 1, 0))
    zs = z + mu_ref[0:1, :] * (z_prev - z) + mu_ref[1:2, :] * (z_next - z)
    w = RW_WIDTH
    r, k, v = zs[:, :w], zs[:, w:2 * w], zs[:, 2 * w:3 * w]
    zw = zs[:, 3 * w:3 * w + LANES]
    za = zs[:, 3 * w + LANES:3 * w + 2 * LANES]
    zg = zs[:, 3 * w + 2 * LANES:]
    ones = _group_ones(w)
    kk = k * vec_ref[0:1, :]
    kk = kk * lax.rsqrt(_group_sum(kk * kk, ones) + 1e-12)
    r_ref[...] = r
    v_ref[...] = v
    kk_ref[...] = kk
    tz = jnp.tanh(zw)
    for d in range(2):
        w_raw = vec_ref[3 + d:4 + d, :] + _dot(tz, w2_ref[d])
        lw_ref[d] = -math.exp(-0.5) * jax.nn.sigmoid(w_raw)
        a = jax.nn.sigmoid(vec_ref[5 + d:6 + d, :] + _dot(za, a2_ref[d]))
        a_ref[d] = a
        kd_ref[d] = k * (1.0 + (a - 1.0) * vec_ref[1:2, :])
    bonus_ref[...] = _group_sum(r * k * vec_ref[2:3, :], ones) * v
    gate_ref[...] = _dot(jax.nn.sigmoid(zg), g2_ref[...])


def rwkv_prep(z_rw, halo_prev, halo_next, mu, vecs, w2_pad, a2_pad, g2):
    n_tok = z_rw.shape[0]
    w = RW_WIDTH
    row = lambda i: (i, 0)
    const2 = lambda i: (0, 0)
    const3 = lambda i: (0, 0, 0)
    halo = pl.BlockSpec((1, 1, RW_COLS), lambda i: (i, 0, 0))
    one = pl.BlockSpec((ROW_BLOCK, w), row)
    two = pl.BlockSpec((2, ROW_BLOCK, w), lambda i: (0, i, 0))
    s1 = jax.ShapeDtypeStruct((n_tok, w), F32)
    s2 = jax.ShapeDtypeStruct((2, n_tok, w), F32)
    return pl.pallas_call(
        _rwkv_prep_kernel,
        grid=(n_tok // ROW_BLOCK,),
        in_specs=[pl.BlockSpec((ROW_BLOCK, RW_COLS), row), halo, halo,
                  pl.BlockSpec(mu.shape, const2), pl.BlockSpec(vecs.shape, const2),
                  pl.BlockSpec(w2_pad.shape, const3), pl.BlockSpec(a2_pad.shape, const3),
                  pl.BlockSpec(g2.shape, const2)],
        out_specs=[one, one, one, two, two, two, one, one],
        out_shape=[s1, s1, s1, s2, s2, s2, s1, s1],
        compiler_params=pltpu.CompilerParams(
            dimension_semantics=("parallel",), vmem_limit_bytes=VMEM_LIMIT),
        name="rwkv_prep",
    )(z_rw, halo_prev, halo_next, mu, vecs, w2_pad, a2_pad, g2)


CONV_WIDTH = 512
CONV_KERNEL = 31
CONV_PAD = 16
CONV_ROWS = 64


def _conv_kernel(z_ref, w_ref, vec_ref, o_ref, pad_ref):
    t = z_ref.shape[0]
    cw = CONV_WIDTH
    half = CONV_KERNEL // 2
    zeros = jnp.zeros((CONV_PAD, cw), F32)
    pad_ref[0:CONV_PAD, :] = zeros
    pad_ref[CONV_PAD + t:CONV_PAD + t + CONV_PAD, :] = zeros
    pad_ref[CONV_PAD:CONV_PAD + t, :] = z_ref[:, :cw] * jax.nn.sigmoid(z_ref[:, cw:])

    win = CONV_ROWS + 2 * CONV_PAD
    r = lax.broadcasted_iota(jnp.int32, (CONV_KERNEL * CONV_ROWS, 2 * win), 0)
    c = lax.broadcasted_iota(jnp.int32, (CONV_KERNEL * CONV_ROWS, 2 * win), 1)
    select = jnp.where(c % win == r % CONV_ROWS + r // CONV_ROWS + (CONV_PAD - half), 1.0, 0.0).astype(BF16)

    def chunk(i, carry):
        base = pl.multiple_of(i * CONV_ROWS, CONV_ROWS)
        window = pad_ref[pl.ds(base, win), :]
        hi = window.astype(BF16)
        lo = (window - hi.astype(F32)).astype(BF16)
        shifted = jnp.dot(select, jnp.concatenate([hi, lo], axis=0), preferred_element_type=F32)
        acc = jnp.zeros((CONV_ROWS, cw), F32)
        for j in range(CONV_KERNEL):
            acc = acc + w_ref[j:j + 1, :] * shifted[j * CONV_ROWS:(j + 1) * CONV_ROWS]
        u = acc + vec_ref[0:1, :]
        mu = jnp.mean(u, axis=-1, keepdims=True)
        cen = u - mu
        var = jnp.mean(cen * cen, axis=-1, keepdims=True)
        y = cen * lax.rsqrt(var + LN_EPS) * vec_ref[1:2, :] + vec_ref[2:3, :]
        o_ref[pl.ds(base, CONV_ROWS), :] = y * jax.nn.sigmoid(y)
        return carry

    lax.fori_loop(0, t // CONV_ROWS, chunk, 0)


def conv_module(z_cv, conv_w, vecs, row0, n_seq, seq_len):
    cw = CONV_WIDTH
    blk0 = row0 // seq_len
    return pl.pallas_call(
        _conv_kernel,
        grid=(n_seq,),
        in_specs=[pl.BlockSpec((seq_len, 2 * cw), lambda i: (blk0 + i, 0)),
                  pl.BlockSpec(conv_w.shape, lambda i: (0, 0)),
                  pl.BlockSpec(vecs.shape, lambda i: (0, 0))],
        out_specs=pl.BlockSpec((seq_len, cw), lambda i: (i, 0)),
        out_shape=jax.ShapeDtypeStruct((n_seq * seq_len, cw), F32),
        scratch_shapes=[pltpu.VMEM((seq_len + 2 * CONV_PAD, cw), F32)],
        compiler_params=pltpu.CompilerParams(
            dimension_semantics=("parallel",), vmem_limit_bytes=VMEM_LIMIT),
        name="conv_module",
    )(z_cv, conv_w, vecs)


ATT_Q_BLOCK = 256
ATT_HEADS = 8
ATT_KV_HEADS = 2
ATT_GROUP = ATT_HEADS // ATT_KV_HEADS


def _rot_partner():
    r = lax.broadcasted_iota(jnp.int32, (LANES, LANES), 0)
    c = lax.broadcasted_iota(jnp.int32, (LANES, LANES), 1)
    lo = (r % 32) < 16
    return jnp.where((c == r + 16) & lo, 1.0, jnp.where((c == r - 16) & ~lo, -1.0, 0.0)).astype(BF16)


def _attn_kernel(*refs, rotary):
    if rotary:
        (zq_ref, zkv_ref, nq_ref, nk_ref, cosq_ref, sinq_ref, cosk_ref, sink_ref, ck_ref, cv_ref,
         o_ref) = refs
    else:
        zq_ref, zkv_ref, nq_ref, nk_ref, o_ref, kn_ref = refs
    tq = zq_ref.shape[0]
    ones = _group_ones(LANES)
    lane = lax.broadcasted_iota(jnp.int32, (1, LANES), 1)
    first = lane < HEAD_DIM
    row_head = lax.broadcasted_iota(jnp.int32, (2 * tq, LANES), 0) // tq
    own = row_head == (lax.broadcasted_iota(jnp.int32, (2 * tq, LANES), 1) // HEAD_DIM)

    def norm(x, g):
        return x * lax.rsqrt(_group_sum(x * x, ones) * (1.0 / HEAD_DIM) + RMS_EPS) * g

    def rope(x, cos, sin):
        return x * cos + jnp.dot(x.astype(BF16), _rot_partner(), preferred_element_type=F32) * sin

    def dup(x):
        sw = pltpu.roll(x, HEAD_DIM, 1)
        return jnp.where(first, x, sw), jnp.where(first, sw, x)

    k = norm(zkv_ref[:, :LANES], nk_ref[...])
    v = zkv_ref[:, LANES:]
    if rotary:
        k = rope(k, cosk_ref[...], sink_ref[...])
        ck, cv = dup(ck_ref[...]), dup(cv_ref[...])
    else:
        kn_ref[...] = k
    kd, vd = dup(k), dup(v)
    for pair in range(ATT_HEADS // 2):
        g = pair // (ATT_GROUP // 2)
        q = norm(zq_ref[:, pair * LANES:(pair + 1) * LANES], nq_ref[...])
        if rotary:
            q = rope(q, cosq_ref[...], sinq_ref[...])
        q = q * (HEAD_DIM ** -0.5)
        qs = jnp.where(own, jnp.concatenate([q, q], axis=0), 0.0)
        s = _dot_nt(qs, kd[g])
        m = jnp.max(s, axis=-1, keepdims=True)
        if rotary:
            s2 = _dot_nt(qs, ck[g])
            m = jnp.maximum(m, jnp.max(s2, axis=-1, keepdims=True))
            p2 = jnp.exp(s2 - m)
        p = jnp.exp(s - m)
        den = jnp.sum(p, axis=-1, keepdims=True)
        o = _dot(p, vd[g])
        if rotary:
            den = den + jnp.sum(p2, axis=-1, keepdims=True)
            o = o + _dot(p2, cv[g])
        o = o / den
        o_ref[:, pair * LANES:(pair + 1) * LANES] = jnp.where(first, o[:tq], o[tq:])


def attention(z_q, z_kv, nq, nk, row0, n_seq, seq_len, rope=None, cache=None):
    qb = min(ATT_Q_BLOCK, seq_len)
    nqb = seq_len // qb
    qblk0 = row0 // qb
    sblk0 = row0 // seq_len
    wq = z_q.shape[1]
    in_specs = [pl.BlockSpec((qb, wq), lambda b, t: (qblk0 + b * nqb + t, 0)),
                pl.BlockSpec((seq_len, 2 * LANES), lambda b, t: (sblk0 + b, 0)),
                pl.BlockSpec((1, LANES), lambda b, t: (0, 0)),
                pl.BlockSpec((1, LANES), lambda b, t: (0, 0))]
    args = [z_q, z_kv, nq, nk]
    out_q = pl.BlockSpec((qb, wq), lambda b, t: (b * nqb + t, 0))
    y_shape = jax.ShapeDtypeStruct((n_seq * seq_len, wq), F32)
    if rope is None:
        out_specs = [out_q, pl.BlockSpec((seq_len, LANES), lambda b, t: (b, 0))]
        out_shape = [y_shape, jax.ShapeDtypeStruct((n_seq * seq_len, LANES), F32)]
    else:
        past = cache[0].shape[0] // n_seq
        in_specs += [pl.BlockSpec((qb, LANES), lambda b, t: (t, 0))] * 2
        in_specs += [pl.BlockSpec((seq_len, LANES), lambda b, t: (0, 0))] * 2
        in_specs += [pl.BlockSpec((past, LANES), lambda b, t: (b, 0))] * 2
        args += [rope[0], rope[1], rope[0], rope[1], cache[0], cache[1]]
        out_specs = out_q
        out_shape = y_shape
    return pl.pallas_call(
        functools.partial(_attn_kernel, rotary=rope is not None),
        grid=(n_seq, nqb),
        in_specs=in_specs,
        out_specs=out_specs,
        out_shape=out_shape,
        compiler_params=pltpu.CompilerParams(
            dimension_semantics=("parallel", "arbitrary"), vmem_limit_bytes=VMEM_LIMIT),
        name="attention",
    )(*args)


def _merge_kernel(x_ref, mod_ref, bonus_ref, gate_ref, zg_ref,
                  ofc_ref, ofl_ref, obc_ref, obl_ref, cvc_ref, cvl_ref, atc_ref, atl_ref,
                  gn_ref, wr_ref, wc_ref, wa_ref, wo_ref, wq_ref, xo_ref, h2_ref, q_ref, *, n_ctx_blocks):
    d = x_ref.shape[1]
    zg = zg_ref[...]
    is_ctx = pl.program_id(0) < n_ctx_blocks
    pick = lambda c_ref, l_ref: jnp.where(is_ctx, c_ref[...], l_ref[...])
    ones = _group_ones(ofc_ref.shape[1])
    o = pick(ofc_ref, ofl_ref) + pick(obc_ref, obl_ref)
    cen = o - _group_sum(o, ones) * (1.0 / HEAD_DIM)
    var = _group_sum(cen * cen, ones) * (1.0 / HEAD_DIM)
    o = cen * lax.rsqrt(var + RW_GN_EPS) * gn_ref[0:1, :] + gn_ref[1:2, :]
    y_rw = (o + bonus_ref[...]) * gate_ref[...]
    merged = (jax.nn.sigmoid(zg[:, :d]) * _dot(y_rw, wr_ref[...])
              + jax.nn.sigmoid(zg[:, d:2 * d]) * _dot(pick(cvc_ref, cvl_ref), wc_ref[...])
              + jax.nn.sigmoid(zg[:, 2 * d:]) * _dot(pick(atc_ref, atl_ref), wa_ref[...]))
    x = x_ref[...] + mod_ref[0, 2:3, :] * _dot(merged, wo_ref[...])
    xo_ref[...] = x
    h2 = _modulate(x, mod_ref[0, 3:4, :], mod_ref[0, 4:5, :]).astype(BF16)
    h2_ref[...] = h2
    q_ref[...] = jnp.dot(h2, wq_ref[...], preferred_element_type=F32)


def merge_project(x, mod_blocks, bonus, gate, z_gate, o_f, o_b, y_cv, y_at, gn, w_r, w_c, w_a, w_o, w_q):
    n_tok, d = x.shape
    nc = o_f[0].shape[0] // ROW_BLOCK
    row = lambda i: (i, 0)
    const = lambda i: (0, 0)
    ctx_row = lambda i: (jnp.minimum(i, nc - 1), 0)
    lat_row = lambda i: (jnp.maximum(i - nc, 0), 0)
    acts = [bonus, gate, z_gate]
    weights = [gn, w_r, w_c, w_a, w_o, w_q]
    pair_specs, pair_args = [], []
    for c_arr, l_arr in (o_f, o_b, y_cv, y_at):
        pair_specs += [pl.BlockSpec((ROW_BLOCK, c_arr.shape[1]), ctx_row),
                       pl.BlockSpec((ROW_BLOCK, l_arr.shape[1]), lat_row)]
        pair_args += [c_arr, l_arr]
    return pl.pallas_call(
        functools.partial(_merge_kernel, n_ctx_blocks=nc),
        grid=(n_tok // ROW_BLOCK,),
        in_specs=[pl.BlockSpec((ROW_BLOCK, d), row),
                  pl.BlockSpec((1, MOD_ROWS, d), lambda i: (i, 0, 0))]
                 + [pl.BlockSpec((ROW_BLOCK, a.shape[1]), row) for a in acts]
                 + pair_specs
                 + [pl.BlockSpec(w.shape, const) for w in weights],
        out_specs=[pl.BlockSpec((ROW_BLOCK, d), row), pl.BlockSpec((ROW_BLOCK, d), row),
                   pl.BlockSpec((ROW_BLOCK, w_q.shape[1]), row)],
        out_shape=[jax.ShapeDtypeStruct((n_tok, d), F32), jax.ShapeDtypeStruct((n_tok, d), BF16),
                   jax.ShapeDtypeStruct((n_tok, w_q.shape[1]), F32)],
        compiler_params=pltpu.CompilerParams(
            dimension_semantics=("parallel",), vmem_limit_bytes=VMEM_LIMIT),
        name="merge_project",
    )(x, mod_blocks, *acts, *pair_args, *weights)


PEER_HEADS = 8
PEER_N_KEYS = 128
PEER_HALF = 64
PEER_TOPK = 16
PEER_SEL_BLOCK = 128
PEER_LOOP_UNROLL = 16
PEER_RELAYOUT_TOKENS = 32
PEER_RELAYOUT_KEYS = 8
NEG = -1e30


def _dot3_nt(a, b):
    a_hi = a.astype(BF16)
    b_hi = b.astype(BF16)
    a_lo = (a - a_hi.astype(F32)).astype(BF16)
    b_lo = (b - b_hi.astype(F32)).astype(BF16)
    d = lambda x, y: lax.dot_general(x, y, (((1,), (1,)), ((), ())), preferred_element_type=F32)
    return d(a_hi, b_hi) + (d(a_hi, b_lo) + d(a_lo, b_hi))


def _extract_max(s, code):
    m = jnp.max(s, axis=0, keepdims=True)
    pos = jnp.min(jnp.where(s == m, code, 1e9), axis=0, keepdims=True)
    return m, pos, jnp.where(code == pos, NEG, s)


def _pair_candidates(a0, a1):
    k = PEER_TOPK
    sub = lax.broadcasted_iota(jnp.int32, (k, a0.shape[1]), 0).astype(F32)
    sub8 = sub[:8]
    vals = [a0[0:1] + a1]
    codes = [sub]
    for r0 in range(1, 8):
        limit = k // (r0 + 1)
        vals.append(jnp.where(sub8 < limit, a0[r0:r0 + 1] + a1[:8], NEG))
        codes.append(sub8 + float(r0 * k))
    vals.append(a0[8:] + a1[0:1])
    codes.append((sub8 + 8.0) * float(k))
    return jnp.concatenate(vals, axis=0), jnp.concatenate(codes, axis=0)


def _peer_select_kernel(q_ref, keys_ref, w_ref, top_v, top_i, slot_i, slot_j, slot_g,
                        rows_i, rows_j, rows_g, w3_ref):
    k = PEER_TOPK
    ct = q_ref.shape[0]
    key_code = lax.broadcasted_iota(jnp.int32, (PEER_N_KEYS, ct), 0).astype(F32)
    rank = lax.broadcasted_iota(jnp.int32, (k, ct), 0).astype(F32)
    for h in range(PEER_HEADS):
        st = _dot3_nt(keys_ref[h], q_ref[:, h * LANES:(h + 1) * LANES])
        s = [st[:PEER_N_KEYS], st[PEER_N_KEYS:]]
        for it in range(k):
            for p in range(2):
                m, pos, s[p] = _extract_max(s[p], key_code)
                top_v[h, p, it:it + 1, :] = m
                top_i[h, p, it:it + 1, :] = pos
    for h0 in range(0, PEER_HEADS, 2):
        heads = (h0, h0 + 1)
        cand, code, best, z = {}, {}, {}, {}
        for h in heads:
            a0, a1 = top_v[h, 0], top_v[h, 1]
            cand[h], code[h] = _pair_candidates(a0, a1)
            best[h] = a0[0:1] + a1[0:1]
            z[h] = jnp.zeros_like(best[h])
        for it in range(k):
            for h in heads:
                m, pos, cand[h] = _extract_max(cand[h], code[h])
                r0 = jnp.floor(pos * (1.0 / k))
                r1 = pos - r0 * k
                e = jnp.exp(m - best[h])
                z[h] = z[h] + e
                slot = h * k + it
                slot_i[slot:slot + 1, :] = jnp.sum(jnp.where(rank == r0, top_i[h, 0], 0.0), axis=0,
                                                   keepdims=True)
                slot_j[slot:slot + 1, :] = jnp.sum(jnp.where(rank == r1, top_i[h, 1], 0.0), axis=0,
                                                   keepdims=True)
                slot_g[slot:slot + 1, :] = e
        for h in heads:
            slot_g[h * k:(h + 1) * k, :] = slot_g[h * k:(h + 1) * k, :] / z[h]

    rows_i[...] = slot_i[...].T
    rows_j[...] = slot_j[...].T
    rows_g[...] = slot_g[...].T
    n = PEER_N_KEYS
    sub = lax.broadcasted_iota(jnp.int32, (n, n), 0).astype(F32)

    def per_token(c, carry):
        a_t = jnp.where(sub == rows_i[pl.ds(c, 1), :], rows_g[pl.ds(c, 1), :], 0.0)
        b_t = jnp.where(sub == rows_j[pl.ds(c, 1), :], 1.0, 0.0)
        w3_ref[c] = _dot_nt(a_t, b_t)
        return carry

    lax.fori_loop(0, ct, per_token, 0, unroll=PEER_LOOP_UNROLL)

    tg, ig = PEER_RELAYOUT_TOKENS, PEER_RELAYOUT_KEYS
    r_out = lax.broadcasted_iota(jnp.int32, (tg * ig, tg * ig), 0)
    r_in = lax.broadcasted_iota(jnp.int32, (tg * ig, tg * ig), 1)
    perm = jnp.where(r_in == (r_out % tg) * ig + r_out // tg, 1.0, 0.0).astype(BF16)

    for ib in range(n // ig):
        for t in range(ct // tg):
            z = w3_ref[t * tg:(t + 1) * tg, ib * ig:(ib + 1) * ig, :]
            y = jnp.dot(perm, z.reshape(tg * ig, n).astype(BF16), preferred_element_type=F32)
            for kk in range(ig):
                col = (ib * ig + kk) * n
                w_ref[t * tg:(t + 1) * tg, col:col + n] = y[kk * tg:(kk + 1) * tg].astype(BF16)


def peer_select(q, keys_padded):
    n_tok, qd = q.shape
    ct = PEER_SEL_BLOCK
    n = PEER_N_KEYS
    slots = PEER_HEADS * PEER_TOPK
    return pl.pallas_call(
        _peer_select_kernel,
        grid=(n_tok // ct,),
        in_specs=[pl.BlockSpec((ct, qd), lambda i: (i, 0)),
                  pl.BlockSpec(keys_padded.shape, lambda i: (0, 0, 0))],
        out_specs=pl.BlockSpec((ct, n * n), lambda i: (i, 0)),
        out_shape=jax.ShapeDtypeStruct((n_tok, n * n), BF16),
        scratch_shapes=[pltpu.VMEM((PEER_HEADS, 2, PEER_TOPK, ct), F32),
                        pltpu.VMEM((PEER_HEADS, 2, PEER_TOPK, ct), F32),
                        pltpu.VMEM((slots, ct), F32), pltpu.VMEM((slots, ct), F32),
                        pltpu.VMEM((slots, ct), F32),
                        pltpu.VMEM((ct, slots), F32), pltpu.VMEM((ct, slots), F32),
                        pltpu.VMEM((ct, slots), F32),
                        pltpu.VMEM((ct, n, n), F32)],
        compiler_params=pltpu.CompilerParams(
            dimension_semantics=("parallel",), vmem_limit_bytes=VMEM_LIMIT),
        name="peer_select",
    )(q, keys_padded)


PEER_TOKEN_BLOCK = 1024
PEER_EXPERT_BLOCK = 1024


def _peer_kernel(h_ref, w_ref, u_ref, v_ref, x_ref, mod_ref, o_ref, acc_ref):
    e = pl.program_id(1)

    @pl.when(e == 0)
    def _():
        acc_ref[...] = jnp.zeros_like(acc_ref)

    s = _dot_nt(h_ref[...], u_ref[...])
    act = 0.5 * s * (1.0 + lax.erf(s * (2.0 ** -0.5)))
    acc_ref[...] += jnp.dot(act.astype(BF16) * w_ref[...], v_ref[...], preferred_element_type=F32)

    @pl.when(e == pl.num_programs(1) - 1)
    def _():
        for j in range(mod_ref.shape[0]):
            rows = slice(j * ROW_BLOCK, (j + 1) * ROW_BLOCK)
            o_ref[rows, :] = x_ref[rows, :] + mod_ref[j, 5:6, :] * acc_ref[rows, :]


def peer_dense(h2, w_sel, u_tabs, v_tabs, layer, x, mod_blocks):
    n_tok, d = x.shape
    n_exp = u_tabs.shape[1]
    tb, eb = PEER_TOKEN_BLOCK, PEER_EXPERT_BLOCK
    tok = lambda i, e: (i, 0)
    return pl.pallas_call(
        _peer_kernel,
        grid=(n_tok // tb, n_exp // eb),
        in_specs=[pl.BlockSpec((tb, d), tok),
                  pl.BlockSpec((tb, eb), lambda i, e: (i, e)),
                  pl.BlockSpec((None, eb, d), lambda i, e: (layer, e, 0)),
                  pl.BlockSpec((None, eb, d), lambda i, e: (layer, e, 0)),
                  pl.BlockSpec((tb, d), tok),
                  pl.BlockSpec((tb // ROW_BLOCK, MOD_ROWS, d), lambda i, e: (i, 0, 0))],
        out_specs=pl.BlockSpec((tb, d), tok),
        out_shape=jax.ShapeDtypeStruct((n_tok, d), F32),
        scratch_shapes=[pltpu.VMEM((tb, d), F32)],
        compiler_params=pltpu.CompilerParams(
            dimension_semantics=("parallel", "arbitrary"), vmem_limit_bytes=VMEM_LIMIT),
        name="peer_dense",
    )(h2, w_sel, u_tabs, v_tabs, x, mod_blocks)


RW_HEADS = RW_WIDTH // RW_HEAD_DIM
ROT_HALF = HEAD_DIM // 2
ROPE_THETA = 10000.0
GRID_W = 64


def _bd_from_states(s):
    a = jnp.swapaxes(s, -1, -2)
    n, d, h, m, _ = a.shape
    a = a.reshape(n, d, h // 2, 2, m, m)
    z = jnp.zeros_like(a[:, :, :, 0])
    top = jnp.concatenate([a[:, :, :, 0], z], axis=-1)
    bot = jnp.concatenate([z, a[:, :, :, 1]], axis=-1)
    return jnp.concatenate([top, bot], axis=-2)


def _states_from_bd(bd):
    n, d, p, _, _ = bd.shape
    m = RW_HEAD_DIM
    a = jnp.stack([bd[:, :, :, :m, :m], bd[:, :, :, m:, m:]], axis=3).reshape(n, d, 2 * p, m, m)
    return jnp.swapaxes(a, -1, -2)


def _rope_tables(t):
    rows = t // GRID_W
    row, col = jnp.meshgrid(jnp.arange(rows), jnp.arange(GRID_W), indexing='ij')
    inv = ROPE_THETA ** (-jnp.arange(0, ROT_HALF, 2, dtype=F32) / ROT_HALF)
    ang_r = row.reshape(-1, 1).astype(F32) * inv
    ang_c = col.reshape(-1, 1).astype(F32) * inv
    cr, sr, cc, sc = jnp.cos(ang_r), jnp.sin(ang_r), jnp.cos(ang_c), jnp.sin(ang_c)
    cos = jnp.concatenate([cr, cr, cc, cc], axis=-1)
    sin = jnp.concatenate([sr, sr, sc, sc], axis=-1)
    return jnp.tile(cos, (1, LANES // HEAD_DIM)), jnp.tile(sin, (1, LANES // HEAD_DIM))


def _pad_rows(w, top):
    z = jnp.zeros_like(w)
    return jnp.concatenate([w, z] if top else [z, w], axis=0)


def kernel(x_prompt, x_sample, c, cache_k, cache_v, state_wkv, c_ctx, w_mod, b_mod, w_in, shift_mu_prev, shift_mu_next, rw_w0, rw_w2, rw_a0, rw_a2, rw_g2, rw_k_k, rw_k_a, rw_r_k, rw_gn_w, rw_gn_b, w_out_rwkv, conv_w, conv_b, conv_ln_w, conv_ln_b, w_out_conv, q_norm, k_norm, w_out_attn, w_o, peer_wq, peer_keys, peer_u, peer_v):
    nb, seq, d = x_prompt.shape
    db, dseq, _ = x_sample.shape
    depth = w_in.shape[0]
    n_ctx = nb * seq
    n_lat = db * dseq
    n_tok = n_ctx + n_lat
    x = jnp.concatenate([x_prompt.reshape(n_ctx, d), x_sample.reshape(n_lat, d)], axis=0)

    cond = jnp.concatenate([c_ctx[None], c, jnp.zeros((MOD_ROWS - 1 - db, d), F32)], axis=0)
    blk_cond = jnp.concatenate([jnp.zeros((n_ctx // ROW_BLOCK,), jnp.int32),
                                1 + jnp.arange(n_lat // ROW_BLOCK, dtype=jnp.int32) // (dseq // ROW_BLOCK)])
    n_blk = n_tok // ROW_BLOCK
    blk_row = jnp.arange(n_blk) * ROW_BLOCK
    blk_pos = jnp.where(blk_row < n_ctx, blk_row % seq, (blk_row - n_ctx) % dseq)
    blk_len = jnp.where(blk_row < n_ctx, seq, dseq)
    starts_seq = (blk_pos == 0)[:, None]
    ends_seq = (blk_pos + ROW_BLOCK == blk_len)[:, None]
    rope = _rope_tables(dseq)
    in_sizes = (RW_COLS, 2 * CONV_WIDTH, ATT_HEADS * HEAD_DIM, 2 * ATT_KV_HEADS * HEAD_DIM, 3 * d)
    past = cache_k.shape[2]

    w_in_bf, u_bf, v_bf = w_in.astype(BF16), peer_u.astype(BF16), peer_v.astype(BF16)
    ctx_k, ctx_v, ctx_s = [], [], []
    for l in range(depth):
        mod = modulation(cond, w_mod[l], b_mod[l][None])
        mod = mod.reshape(MOD_ROWS, 6, d)[blk_cond]
        mod_blocks = jnp.concatenate([mod, jnp.zeros((mod.shape[0], MOD_ROWS - 6, d), F32)], axis=1)

        z_rw, z_cv, z_q, z_kv, z_gate = in_projection(x, mod_blocks, w_in_bf, l, in_sizes)

        zb = z_rw.reshape(n_blk, ROW_BLOCK, RW_COLS)
        halo_prev = jnp.where(starts_seq, 0.0, jnp.roll(zb[:, -1], 1, axis=0))[:, None]
        halo_next = jnp.where(ends_seq, 0.0, jnp.roll(zb[:, 0], -1, axis=0))[:, None]
        mu = jnp.stack([shift_mu_prev[l], shift_mu_next[l]])
        vecs = jnp.stack([rw_k_k[l], rw_k_a[l], rw_r_k[l], rw_w0[l, 0], rw_w0[l, 1], rw_a0[l, 0],
                          rw_a0[l, 1], jnp.zeros((RW_WIDTH,), F32)])
        w2_pad = jnp.stack([_pad_rows(rw_w2[l, 0], True), _pad_rows(rw_w2[l, 1], False)]).astype(BF16)
        a2_pad = jnp.stack([_pad_rows(rw_a2[l, 0], True), _pad_rows(rw_a2[l, 1], False)]).astype(BF16)
        r, v, kk, lw, aa, kd, bonus, gate = rwkv_prep(z_rw, halo_prev, halo_next, mu, vecs, w2_pad, a2_pad,
                                                      rw_g2[l].astype(BF16))
        s0_ctx = jnp.zeros((nb, 2, RW_HEADS // 2, LANES, LANES), F32)
        of_c, ob_c, sfin = rwkv_scan(r, v, kk, lw, aa, kd, s0_ctx, 0, nb, seq)
        of_l, ob_l, _ = rwkv_scan(r, v, kk, lw, aa, kd, _bd_from_states(state_wkv[:, l]), n_ctx, db, dseq)
        ctx_s.append(_states_from_bd(sfin))

        cvec = jnp.stack([conv_b[l], conv_ln_w[l], conv_ln_b[l]] + [jnp.zeros((CONV_WIDTH,), F32)] * 5)
        y_cv = (conv_module(z_cv, conv_w[l], cvec, 0, nb, seq),
                conv_module(z_cv, conv_w[l], cvec, n_ctx, db, dseq))

        nq = jnp.tile(q_norm[l], LANES // HEAD_DIM)[None]
        nk = jnp.tile(k_norm[l], LANES // HEAD_DIM)[None]
        y_c, k_ctx = attention(z_q, z_kv, nq, nk, 0, nb, seq)
        y_l = attention(z_q, z_kv, nq, nk, n_ctx, db, dseq, rope=rope,
                        cache=(cache_k[:, l].reshape(db * past, LANES), cache_v[:, l].reshape(db * past, LANES)))
        ctx_k.append(k_ctx.reshape(nb, seq, ATT_KV_HEADS, HEAD_DIM))
        ctx_v.append(z_kv[:n_ctx, LANES:].reshape(nb, seq, ATT_KV_HEADS, HEAD_DIM))

        x, h2, pq = merge_project(x, mod_blocks, bonus, gate, z_gate,
                                  (of_c, of_l), (ob_c, ob_l), y_cv, (y_c, y_l),
                                  jnp.stack([rw_gn_w[l], rw_gn_b[l]]),
                                  w_out_rwkv[l].astype(BF16), w_out_conv[l].astype(BF16),
                                  w_out_attn[l].astype(BF16), w_o[l].astype(BF16), peer_wq[l].astype(BF16))

        kz = jnp.zeros_like(peer_keys[l][:, 0])
        keys_padded = jnp.concatenate([jnp.concatenate([peer_keys[l][:, 0], kz], axis=-1),
                                       jnp.concatenate([kz, peer_keys[l][:, 1]], axis=-1)], axis=1)
        w_sel = peer_select(pq, keys_padded)
        x = peer_dense(h2, w_sel, u_bf, v_bf, l, x, mod_blocks)

    new_cache_k = jnp.stack(ctx_k, axis=1)
    new_cache_v = jnp.stack(ctx_v, axis=1)
    new_state = jnp.stack(ctx_s, axis=1)
    return (x[:n_ctx].reshape(nb, seq, d), x[n_ctx:].reshape(db, dseq, d),
            new_cache_k, new_cache_v, new_state)
```

```python
import functools
import math

import jax
import jax.numpy as jnp
from jax import lax
from jax.experimental import pallas as pl
from jax.experimental.pallas import tpu as pltpu

F32 = jnp.float32
BF16 = jnp.bfloat16

LANES = 128
SUBLANES = 8
SCAN_CHUNK = 64
SCAN_SUB = 16
RW_HEAD_DIM = 64
HEADS_PER_PAIR = LANES // RW_HEAD_DIM
VMEM_LIMIT = 56 * 1024 * 1024


def _dot(a, b):
    return jnp.dot(a.astype(BF16), b.astype(BF16), preferred_element_type=F32)


def _dot_nt(a, b):
    return lax.dot_general(a.astype(BF16), b.astype(BF16), (((1,), (1,)), ((), ())),
                           preferred_element_type=F32)


def _dot_tn(a, b):
    return lax.dot_general(a.astype(BF16), b.astype(BF16), (((0,), (0,)), ((), ())),
                           preferred_element_type=F32)


def _dot3(a, b):
    a_hi = a.astype(BF16)
    b_hi = b.astype(BF16)
    a_lo = (a - a_hi.astype(F32)).astype(BF16)
    b_lo = (b - b_hi.astype(F32)).astype(BF16)
    d = functools.partial(jnp.dot, preferred_element_type=F32)
    return d(a_hi, b_hi) + (d(a_hi, b_lo) + d(a_lo, b_hi))


def _scan_masks(reverse):
    C = SCAN_CHUNK
    P = HEADS_PER_PAIR * C
    row = lax.broadcasted_iota(jnp.int32, (P, LANES), 0)
    col = lax.broadcasted_iota(jnp.int32, (P, LANES), 1)
    tr = row % C
    tc = col % C
    t_i = lax.broadcasted_iota(jnp.int32, (C, C), 0)
    s_i = lax.broadcasted_iota(jnp.int32, (C, C), 1)
    return dict(
        same_head=(row // C) == (col // RW_HEAD_DIM),
        incl=(tr >= tc) if not reverse else (tr <= tc),
        strict=(tr > tc) if not reverse else (tr < tc),
        diag_blk=(tr // SCAN_SUB) == (tc // SCAN_SUB),
        eye=row == col,
        cum_mat=jnp.where((t_i >= s_i) if not reverse else (t_i <= s_i), 1.0, 0.0).astype(F32))


def _scan_chunk(chains, states):
    C = SCAN_CHUNK
    n = range(len(chains))
    mk = [ch[6] for ch in chains]
    cum = [ch[8] for ch in chains]
    pre = []
    for i in n:
        r, lw, k, v, kk, a, m, reverse, _ = chains[i]
        g = jnp.exp(cum[i])
        g_prev = jnp.exp(cum[i] - lw)
        g_inv = jnp.exp(-cum[i])
        g_end = jnp.exp(cum[i][C - 1:C, :] if not reverse else cum[i][0:1, :])
        bt = kk * a * g_inv
        kt = k * g_inv
        stack = lambda x, m=m: jnp.where(m["same_head"], jnp.concatenate([x, x], axis=0), 0.0)
        pre.append(dict(kap=stack(kk * g_prev), rt=stack(r * g), bt=stack(bt), kt=stack(kt),
                        v=stack(v), bh=stack(bt * g_end), kh=stack(kt * g_end), g_end=g_end))
    P = HEADS_PER_PAIR * C
    tri = [_dot_nt(jnp.concatenate([pre[i]["kap"], pre[i]["rt"]], axis=0),
                   jnp.concatenate([pre[i]["bt"], pre[i]["kt"]], axis=0)) for i in n]
    lb = [jnp.where(mk[i]["strict"], tri[i][:P, :P], 0.0) for i in n]
    lk = [jnp.where(mk[i]["strict"], tri[i][:P, P:], 0.0) for i in n]
    pb = [jnp.where(mk[i]["incl"], tri[i][P:, :P], 0.0) for i in n]
    pk = [jnp.where(mk[i]["incl"], tri[i][P:, P:], 0.0) for i in n]
    lkv = [_dot(lk[i], pre[i]["v"]) for i in n]

    d = [jnp.where(mk[i]["diag_blk"], lb[i], 0.0) for i in n]
    inv = [jnp.where(mk[i]["eye"], 1.0, 0.0) - d[i] for i in n]
    p = d
    for _ in range(3):
        p = [_dot(p[i], p[i]) for i in n]
        inv = [inv[i] + _dot(inv[i], p[i]) for i in n]
    x = [_dot(inv[i], jnp.concatenate([lb[i] - d[i], pre[i]["kap"], lkv[i]], axis=1)) for i in n]
    e = [x[i][:, :LANES] for i in n]
    rhs = [x[i][:, LANES:] for i in n]
    e2 = [_dot(e[i], e[i]) for i in n]
    t = [rhs[i] + _dot(e2[i], rhs[i]) for i in n]
    gy = [t[i] - _dot(e[i], t[i]) for i in n]

    bhgy = [_dot_tn(pre[i]["bh"], gy[i]) for i in n]
    m_mat = [jnp.where(mk[i]["eye"], pre[i]["g_end"], 0.0) - bhgy[i][:, :LANES] for i in n]
    n_mat = [_dot_tn(pre[i]["kh"], pre[i]["v"]) - bhgy[i][:, LANES:] for i in n]
    pbgy = [_dot(pb[i], gy[i]) for i in n]
    q = [pre[i]["rt"] - pbgy[i][:, :LANES] for i in n]
    z = [_dot(pk[i], pre[i]["v"]) - pbgy[i][:, LANES:] for i in n]
    o_st = [_dot(q[i], states[i]) + z[i] for i in n]
    new = [_dot3(m_mat[i], states[i]) + n_mat[i] for i in n]
    return [(o_st[i][:C] + o_st[i][C:], new[i]) for i in n]


def _scan_kernel(rf, vf, kkf, lwf, af, kf, rb, vb, kkb, lwb, ab, kb, s0, of, ob, sfin, state):
    c = pl.program_id(1)
    n_pairs = rf.shape[-1] // LANES

    @pl.when(c == 0)
    def _():
        state[...] = s0[0]

    chains, states, outs = [], [], []
    for d, (r_, v_, kk_, lw_, a_, k_, o_) in enumerate(((rf, vf, kkf, lwf, af, kf, of),
                                                        (rb, vb, kkb, lwb, ab, kb, ob))):
        masks = _scan_masks(reverse=(d == 1))
        cum = _dot3(masks["cum_mat"], lw_[...])
        for p in range(n_pairs):
            sl = slice(p * LANES, (p + 1) * LANES)
            chains.append((r_[:, sl], lw_[:, sl], k_[:, sl], v_[:, sl], kk_[:, sl], a_[:, sl],
                           masks, d == 1, cum[:, sl]))
            states.append(state[d, p])
            outs.append((o_, sl, d, p))
    for (o, new), (o_, sl, d, p) in zip(_scan_chunk(chains, states), outs):
        o_[:, sl] = o
        state[d, p] = new

    @pl.when(c == pl.num_programs(1) - 1)
    def _():
        sfin[0] = state[...]


def rwkv_scan(r, v, kk, lw, a, kd, s0_bd, row0, n_seq, seq_len):
    width = r.shape[1]
    n_tok = n_seq * seq_len
    n_pairs = width // LANES
    n_chunk = seq_len // SCAN_CHUNK
    c0 = row0 // SCAN_CHUNK
    blk = (SCAN_CHUNK, width)
    dblk = (None, SCAN_CHUNK, width)
    fwd = lambda s, c: (s * n_chunk + c, 0)
    bwd = lambda s, c: (s * n_chunk + n_chunk - 1 - c, 0)
    fwd_in = lambda s, c: (c0 + s * n_chunk + c, 0)
    bwd_in = lambda s, c: (c0 + s * n_chunk + n_chunk - 1 - c, 0)
    fwd_d = lambda s, c: (0, c0 + s * n_chunk + c, 0)
    bwd_d = lambda s, c: (1, c0 + s * n_chunk + n_chunk - 1 - c, 0)
    sspec = pl.BlockSpec((1, 2, n_pairs, LANES, LANES), lambda s, c: (s, 0, 0, 0, 0))
    in_specs = ([pl.BlockSpec(blk, fwd_in)] * 3 + [pl.BlockSpec(dblk, fwd_d)] * 3
                + [pl.BlockSpec(blk, bwd_in)] * 3 + [pl.BlockSpec(dblk, bwd_d)] * 3 + [sspec])
    out_specs = [pl.BlockSpec(blk, fwd), pl.BlockSpec(blk, bwd), sspec]
    out_shape = [jax.ShapeDtypeStruct((n_tok, width), F32)] * 2 + [
        jax.ShapeDtypeStruct(s0_bd.shape, F32)]
    return pl.pallas_call(
        _scan_kernel,
        grid=(n_seq, n_chunk),
        in_specs=in_specs,
        out_specs=out_specs,
        out_shape=out_shape,
        scratch_shapes=[pltpu.VMEM((2, n_pairs, LANES, LANES), F32)],
        compiler_params=pltpu.CompilerParams(
            dimension_semantics=("parallel", "arbitrary"), vmem_limit_bytes=VMEM_LIMIT),
        name="rwkv_scan",
    )(r, v, kk, lw, a, kd, r, v, kk, lw, a, kd, s0_bd)


MOD_ROWS = 8
ROW_BLOCK = 256
RMS_EPS = 1e-6
LN_EPS = 1e-5
RW_GN_EPS = 64e-5


def _mod_kernel(c_ref, w_ref, b_ref, o_ref):
    c = c_ref[...]
    s = c * jax.nn.sigmoid(c)
    o_ref[...] = _dot(s, w_ref[...]) + b_ref[...]


def modulation(cond, w_mod, b_mod):
    n, d = cond.shape
    cols = w_mod.shape[1]
    blk = 1024
    return pl.pallas_call(
        _mod_kernel,
        grid=(cols // blk,),
        in_specs=[pl.BlockSpec((n, d), lambda j: (0, 0)),
                  pl.BlockSpec((d, blk), lambda j: (0, j)),
                  pl.BlockSpec((1, blk), lambda j: (0, j))],
        out_specs=pl.BlockSpec((n, blk), lambda j: (0, j)),
        out_shape=jax.ShapeDtypeStruct((n, cols), F32),
        name="modulation",
    )(cond, w_mod, b_mod)


def _modulate(x, shift, scale):
    ms = jnp.mean(x * x, axis=-1, keepdims=True)
    return x * lax.rsqrt(ms + RMS_EPS) * (1.0 + scale) + shift


def _inproj_kernel(x_ref, mod_ref, w_ref, *o_refs):
    h = _modulate(x_ref[...], mod_ref[0, 0:1, :], mod_ref[0, 1:2, :]).astype(BF16)
    start = 0
    for o_ref in o_refs:
        n = o_ref.shape[1]
        o_ref[...] = jnp.dot(h, w_ref[:, start:start + n], preferred_element_type=F32)
        start += n


def in_projection(x, mod_blocks, w_in, layer, sizes):
    n_tok, d = x.shape
    row = lambda i: (i, 0)
    return pl.pallas_call(
        _inproj_kernel,
        grid=(n_tok // ROW_BLOCK,),
        in_specs=[pl.BlockSpec((ROW_BLOCK, d), row),
                  pl.BlockSpec((1, MOD_ROWS, d), lambda i: (i, 0, 0)),
                  pl.BlockSpec((None, d, w_in.shape[2]), lambda i: (layer, 0, 0),
                               pipeline_mode=pl.Buffered(1))],
        out_specs=[pl.BlockSpec((ROW_BLOCK, n), row) for n in sizes],
        out_shape=[jax.ShapeDtypeStruct((n_tok, n), F32) for n in sizes],
        compiler_params=pltpu.CompilerParams(
            dimension_semantics=("parallel",), vmem_limit_bytes=VMEM_LIMIT),
        name="in_projection",
    )(x, mod_blocks, w_in)


HEAD_DIM = 64


def _group_ones():
    r = lax.broadcasted_iota(jnp.int32, (LANES, LANES), 0) // HEAD_DIM
    c = lax.broadcasted_iota(jnp.int32, (LANES, LANES), 1) // HEAD_DIM
    return jnp.where(r == c, 1.0, 0.0).astype(BF16)


def _group_sum(x, ones):
    hi = x.astype(BF16)
    lo = (x - hi.astype(F32)).astype(BF16)
    d = functools.partial(jnp.dot, preferred_element_type=F32)
    tiles = [slice(s, s + LANES) for s in range(0, x.shape[1], LANES)]
    sums = [d(hi[:, t], ones) + d(lo[:, t], ones) for t in tiles]
    return sums[0] if len(sums) == 1 else jnp.concatenate(sums, axis=1)


RW_WIDTH = 512
RW_COLS = 3 * RW_WIDTH + 3 * LANES
RW_VEC_ROWS = 8


def _rwkv_prep_kernel(z_ref, hp_ref, hn_ref, mu_ref, vec_ref, w2_ref, a2_ref, g2_ref,
                      r_ref, v_ref, kk_ref, lw_ref, a_ref, kd_ref, bonus_ref, gate_ref):
    z = z_ref[...]
    rb = z.shape[0]
    row = lax.broadcasted_iota(jnp.int32, (rb, 1), 0)
    z_prev = jnp.where(row == 0, hp_ref[0], pltpu.roll(z, 1, 0))
    z_next = jnp.where(row == rb - 1, hn_ref[0], pltpu.roll(z, rb - 1, 0))
    zs = z + mu_ref[0:1, :] * (z_prev - z) + mu_ref[1:2, :] * (z_next - z)
    w = RW_WIDTH
    r, k, v = zs[:, :w], zs[:, w:2 * w], zs[:, 2 * w:3 * w]
    zw = zs[:, 3 * w:3 * w + LANES]
    za = zs[:, 3 * w + LANES:3 * w + 2 * LANES]
    zg = zs[:, 3 * w + 2 * LANES:]
    ones = _group_ones()
    kk = k * vec_ref[0:1, :]
    kk = kk * lax.rsqrt(_group_sum(kk * kk, ones) + 1e-12)
    r_ref[...] = r
    v_ref[...] = v
    kk_ref[...] = kk
    tz = jnp.tanh(zw)
    for d in range(2):
        w_raw = vec_ref[3 + d:4 + d, :] + _dot(tz, w2_ref[d])
        lw_ref[d] = -math.exp(-0.5) * jax.nn.sigmoid(w_raw)
        a = jax.nn.sigmoid(vec_ref[5 + d:6 + d, :] + _dot(za, a2_ref[d]))
        a_ref[d] = a
        kd_ref[d] = k * (1.0 + (a - 1.0) * vec_ref[1:2, :])
    bonus_ref[...] = _group_sum(r * k * vec_ref[2:3, :], ones) * v
    gate_ref[...] = _dot(jax.nn.sigmoid(zg), g2_ref[...])


def rwkv_prep(z_rw, halo_prev, halo_next, mu, vecs, w2_pad, a2_pad, g2):
    n_tok = z_rw.shape[0]
    w = RW_WIDTH
    row = lambda i: (i, 0)
    const2 = lambda i: (0, 0)
    const3 = lambda i: (0, 0, 0)
    halo = pl.BlockSpec((1, 1, RW_COLS), lambda i: (i, 0, 0))
    one = pl.BlockSpec((ROW_BLOCK, w), row)
    two = pl.BlockSpec((2, ROW_BLOCK, w), lambda i: (0, i, 0))
    s1 = jax.ShapeDtypeStruct((n_tok, w), F32)
    s2 = jax.ShapeDtypeStruct((2, n_tok, w), F32)
    return pl.pallas_call(
        _rwkv_prep_kernel,
        grid=(n_tok // ROW_BLOCK,),
        in_specs=[pl.BlockSpec((ROW_BLOCK, RW_COLS), row), halo, halo,
                  pl.BlockSpec(mu.shape, const2), pl.BlockSpec(vecs.shape, const2),
                  pl.BlockSpec(w2_pad.shape, const3), pl.BlockSpec(a2_pad.shape, const3),
                  pl.BlockSpec(g2.shape, const2)],
        out_specs=[one, one, one, two, two, two, one, one],
        out_shape=[s1, s1, s1, s2, s2, s2, s1, s1],
        compiler_params=pltpu.CompilerParams(
            dimension_semantics=("parallel",), vmem_limit_bytes=VMEM_LIMIT),
        name="rwkv_prep",
    )(z_rw, halo_prev, halo_next, mu, vecs, w2_pad, a2_pad, g2)


CONV_WIDTH = 512
CONV_KERNEL = 31
CONV_PAD = 16
CONV_ROWS = 64


def _conv_kernel(z_ref, w_ref, vec_ref, o_ref, pad_ref):
    t = z_ref.shape[0]
    cw = CONV_WIDTH
    half = CONV_KERNEL // 2
    zeros = jnp.zeros((CONV_PAD, cw), F32)
    pad_ref[0:CONV_PAD, :] = zeros
    pad_ref[CONV_PAD + t:CONV_PAD + t + CONV_PAD, :] = zeros
    pad_ref[CONV_PAD:CONV_PAD + t, :] = z_ref[:, :cw] * jax.nn.sigmoid(z_ref[:, cw:])

    def chunk(i, carry):
        base = pl.multiple_of(i * CONV_ROWS, CONV_ROWS)
        window = pad_ref[pl.ds(base, CONV_ROWS + 2 * CONV_PAD), :]
        span = CONV_ROWS + 2 * CONV_PAD - SUBLANES
        shifted = [window[rho:rho + span] for rho in range(SUBLANES)]
        acc = jnp.zeros((CONV_ROWS, cw), F32)
        for j in range(CONV_KERNEL):
            lo = CONV_PAD - half + j
            tile = (lo // SUBLANES) * SUBLANES
            acc = acc + w_ref[j:j + 1, :] * shifted[lo % SUBLANES][tile:tile + CONV_ROWS]
        u = acc + vec_ref[0:1, :]
        mu = jnp.mean(u, axis=-1, keepdims=True)
        cen = u - mu
        var = jnp.mean(cen * cen, axis=-1, keepdims=True)
        y = cen * lax.rsqrt(var + LN_EPS) * vec_ref[1:2, :] + vec_ref[2:3, :]
        o_ref[pl.ds(base, CONV_ROWS), :] = y * jax.nn.sigmoid(y)
        return carry

    lax.fori_loop(0, t // CONV_ROWS, chunk, 0)


def conv_module(z_cv, conv_w, vecs, row0, n_seq, seq_len):
    cw = CONV_WIDTH
    blk0 = row0 // seq_len
    return pl.pallas_call(
        _conv_kernel,
        grid=(n_seq,),
        in_specs=[pl.BlockSpec((seq_len, 2 * cw), lambda i: (blk0 + i, 0)),
                  pl.BlockSpec(conv_w.shape, lambda i: (0, 0)),
                  pl.BlockSpec(vecs.shape, lambda i: (0, 0))],
        out_specs=pl.BlockSpec((seq_len, cw), lambda i: (i, 0)),
        out_shape=jax.ShapeDtypeStruct((n_seq * seq_len, cw), F32),
        scratch_shapes=[pltpu.VMEM((seq_len + 2 * CONV_PAD, cw), F32)],
        compiler_params=pltpu.CompilerParams(
            dimension_semantics=("parallel",), vmem_limit_bytes=VMEM_LIMIT),
        name="conv_module",
    )(z_cv, conv_w, vecs)


ATT_Q_BLOCK = 256
ATT_HEADS = 8
ATT_KV_HEADS = 2
ATT_GROUP = ATT_HEADS // ATT_KV_HEADS


def _rot_partner():
    r = lax.broadcasted_iota(jnp.int32, (LANES, LANES), 0)
    c = lax.broadcasted_iota(jnp.int32, (LANES, LANES), 1)
    lo = (r % 32) < 16
    return jnp.where((c == r + 16) & lo, 1.0, jnp.where((c == r - 16) & ~lo, -1.0, 0.0)).astype(BF16)


def _attn_kernel(*refs, rotary):
    if rotary:
        (zq_ref, zkv_ref, nq_ref, nk_ref, cosq_ref, sinq_ref, cosk_ref, sink_ref, ck_ref, cv_ref,
         o_ref) = refs
    else:
        zq_ref, zkv_ref, nq_ref, nk_ref, o_ref, kn_ref = refs
    tq = zq_ref.shape[0]
    ones = _group_ones()
    lane = lax.broadcasted_iota(jnp.int32, (1, LANES), 1)
    first = lane < HEAD_DIM
    row_head = lax.broadcasted_iota(jnp.int32, (2 * tq, LANES), 0) // tq
    own = row_head == (lax.broadcasted_iota(jnp.int32, (2 * tq, LANES), 1) // HEAD_DIM)

    def norm(x, g):
        return x * lax.rsqrt(_group_sum(x * x, ones) * (1.0 / HEAD_DIM) + RMS_EPS) * g

    def rope(x, cos, sin):
        return x * cos + jnp.dot(x.astype(BF16), _rot_partner(), preferred_element_type=F32) * sin

    def dup(x):
        sw = pltpu.roll(x, HEAD_DIM, 1)
        return jnp.where(first, x, sw), jnp.where(first, sw, x)

    k = norm(zkv_ref[:, :LANES], nk_ref[...])
    v = zkv_ref[:, LANES:]
    if rotary:
        k = rope(k, cosk_ref[...], sink_ref[...])
        ck, cv = dup(ck_ref[...]), dup(cv_ref[...])
    else:
        kn_ref[...] = k
    kd, vd = dup(k), dup(v)
    for pair in range(ATT_HEADS // 2):
        g = pair // (ATT_GROUP // 2)
        q = norm(zq_ref[:, pair * LANES:(pair + 1) * LANES], nq_ref[...])
        if rotary:
            q = rope(q, cosq_ref[...], sinq_ref[...])
        q = q * (HEAD_DIM ** -0.5)
        qs = jnp.where(own, jnp.concatenate([q, q], axis=0), 0.0)
        s = _dot_nt(qs, kd[g])
        m = jnp.max(s, axis=-1, keepdims=True)
        if rotary:
            s2 = _dot_nt(qs, ck[g])
            m = jnp.maximum(m, jnp.max(s2, axis=-1, keepdims=True))
            p2 = jnp.exp(s2 - m)
        p = jnp.exp(s - m)
        den = jnp.sum(p, axis=-1, keepdims=True)
        o = _dot(p, vd[g])
        if rotary:
            den = den + jnp.sum(p2, axis=-1, keepdims=True)
            o = o + _dot(p2, cv[g])
        o = o / den
        o_ref[:, pair * LANES:(pair + 1) * LANES] = jnp.where(first, o[:tq], o[tq:])


def attention(z_q, z_kv, nq, nk, row0, n_seq, seq_len, rope=None, cache=None):
    qb = min(ATT_Q_BLOCK, seq_len)
    nqb = seq_len // qb
    qblk0 = row0 // qb
    sblk0 = row0 // seq_len
    wq = z_q.shape[1]
    in_specs = [pl.BlockSpec((qb, wq), lambda b, t: (qblk0 + b * nqb + t, 0)),
                pl.BlockSpec((seq_len, 2 * LANES), lambda b, t: (sblk0 + b, 0)),
                pl.BlockSpec((1, LANES), lambda b, t: (0, 0)),
                pl.BlockSpec((1, LANES), lambda b, t: (0, 0))]
    args = [z_q, z_kv, nq, nk]
    out_q = pl.BlockSpec((qb, wq), lambda b, t: (b * nqb + t, 0))
    y_shape = jax.ShapeDtypeStruct((n_seq * seq_len, wq), F32)
    if rope is None:
        out_specs = [out_q, pl.BlockSpec((seq_len, LANES), lambda b, t: (b, 0))]
        out_shape = [y_shape, jax.ShapeDtypeStruct((n_seq * seq_len, LANES), F32)]
    else:
        past = cache[0].shape[0] // n_seq
        in_specs += [pl.BlockSpec((qb, LANES), lambda b, t: (t, 0))] * 2
        in_specs += [pl.BlockSpec((seq_len, LANES), lambda b, t: (0, 0))] * 2
        in_specs += [pl.BlockSpec((past, LANES), lambda b, t: (b, 0))] * 2
        args += [rope[0], rope[1], rope[0], rope[1], cache[0], cache[1]]
        out_specs = out_q
        out_shape = y_shape
    return pl.pallas_call(
        functools.partial(_attn_kernel, rotary=rope is not None),
        grid=(n_seq, nqb),
        in_specs=in_specs,
        out_specs=out_specs,
        out_shape=out_shape,
        compiler_params=pltpu.CompilerParams(
            dimension_semantics=("parallel", "arbitrary"), vmem_limit_bytes=VMEM_LIMIT),
        name="attention",
    )(*args)


def _merge_kernel(x_ref, mod_ref, bonus_ref, gate_ref, zg_ref,
                  ofc_ref, ofl_ref, obc_ref, obl_ref, cvc_ref, cvl_ref, atc_ref, atl_ref,
                  gn_ref, wr_ref, wc_ref, wa_ref, wo_ref, wq_ref, xo_ref, h2_ref, q_ref, *, n_ctx_blocks):
    d = x_ref.shape[1]
    zg = zg_ref[...]
    is_ctx = pl.program_id(0) < n_ctx_blocks
    pick = lambda c_ref, l_ref: jnp.where(is_ctx, c_ref[...], l_ref[...])
    ones = _group_ones()
    o = pick(ofc_ref, ofl_ref) + pick(obc_ref, obl_ref)
    cen = o - _group_sum(o, ones) * (1.0 / HEAD_DIM)
    var = _group_sum(cen * cen, ones) * (1.0 / HEAD_DIM)
    o = cen * lax.rsqrt(var + RW_GN_EPS) * gn_ref[0:1, :] + gn_ref[1:2, :]
    y_rw = (o + bonus_ref[...]) * gate_ref[...]
    merged = (jax.nn.sigmoid(zg[:, :d]) * _dot(y_rw, wr_ref[...])
              + jax.nn.sigmoid(zg[:, d:2 * d]) * _dot(pick(cvc_ref, cvl_ref), wc_ref[...])
              + jax.nn.sigmoid(zg[:, 2 * d:]) * _dot(pick(atc_ref, atl_ref), wa_ref[...]))
    x = x_ref[...] + mod_ref[0, 2:3, :] * _dot(merged, wo_ref[...])
    xo_ref[...] = x
    h2 = _modulate(x, mod_ref[0, 3:4, :], mod_ref[0, 4:5, :]).astype(BF16)
    h2_ref[...] = h2
    q_ref[...] = jnp.dot(h2, wq_ref[...], preferred_element_type=F32)


def merge_project(x, mod_blocks, bonus, gate, z_gate, o_f, o_b, y_cv, y_at, gn, w_r, w_c, w_a, w_o, w_q):
    n_tok, d = x.shape
    nc = o_f[0].shape[0] // ROW_BLOCK
    row = lambda i: (i, 0)
    const = lambda i: (0, 0)
    ctx_row = lambda i: (jnp.minimum(i, nc - 1), 0)
    lat_row = lambda i: (jnp.maximum(i - nc, 0), 0)
    acts = [bonus, gate, z_gate]
    weights = [gn, w_r, w_c, w_a, w_o, w_q]
    pair_specs, pair_args = [], []
    for c_arr, l_arr in (o_f, o_b, y_cv, y_at):
        pair_specs += [pl.BlockSpec((ROW_BLOCK, c_arr.shape[1]), ctx_row),
                       pl.BlockSpec((ROW_BLOCK, l_arr.shape[1]), lat_row)]
        pair_args += [c_arr, l_arr]
    return pl.pallas_call(
        functools.partial(_merge_kernel, n_ctx_blocks=nc),
        grid=(n_tok // ROW_BLOCK,),
        in_specs=[pl.BlockSpec((ROW_BLOCK, d), row),
                  pl.BlockSpec((1, MOD_ROWS, d), lambda i: (i, 0, 0))]
                 + [pl.BlockSpec((ROW_BLOCK, a.shape[1]), row) for a in acts]
                 + pair_specs
                 + [pl.BlockSpec(w.shape, const) for w in weights],
        out_specs=[pl.BlockSpec((ROW_BLOCK, d), row), pl.BlockSpec((ROW_BLOCK, d), row),
                   pl.BlockSpec((ROW_BLOCK, w_q.shape[1]), row)],
        out_shape=[jax.ShapeDtypeStruct((n_tok, d), F32), jax.ShapeDtypeStruct((n_tok, d), BF16),
                   jax.ShapeDtypeStruct((n_tok, w_q.shape[1]), F32)],
        compiler_params=pltpu.CompilerParams(
            dimension_semantics=("parallel",), vmem_limit_bytes=VMEM_LIMIT),
        name="merge_project",
    )(x, mod_blocks, *acts, *pair_args, *weights)


PEER_HEADS = 8
PEER_N_KEYS = 128
PEER_HALF = 64
PEER_TOPK = 16
PEER_SEL_BLOCK = 128
PEER_LOOP_UNROLL = 16
PEER_RELAYOUT = 16
NEG = -1e30


def _dot3_nt(a, b):
    a_hi = a.astype(BF16)
    b_hi = b.astype(BF16)
    a_lo = (a - a_hi.astype(F32)).astype(BF16)
    b_lo = (b - b_hi.astype(F32)).astype(BF16)
    d = lambda x, y: lax.dot_general(x, y, (((1,), (1,)), ((), ())), preferred_element_type=F32)
    return d(a_hi, b_hi) + (d(a_hi, b_lo) + d(a_lo, b_hi))


def _extract_max(s, code):
    m = jnp.max(s, axis=0, keepdims=True)
    pos = jnp.min(jnp.where(s == m, code, 1e9), axis=0, keepdims=True)
    return m, pos, jnp.where(code == pos, NEG, s)


def _pair_candidates(a0, a1):
    k = PEER_TOPK
    sub = lax.broadcasted_iota(jnp.int32, (k, a0.shape[1]), 0).astype(F32)
    sub8 = sub[:8]
    vals = [a0[0:1] + a1]
    codes = [sub]
    for r0 in range(1, 8):
        limit = k // (r0 + 1)
        vals.append(jnp.where(sub8 < limit, a0[r0:r0 + 1] + a1[:8], NEG))
        codes.append(sub8 + float(r0 * k))
    vals.append(a0[8:] + a1[0:1])
    codes.append((sub8 + 8.0) * float(k))
    return jnp.concatenate(vals, axis=0), jnp.concatenate(codes, axis=0)


def _peer_select_kernel(q_ref, keys_ref, w_ref, top_v, top_i, slot_i, slot_j, slot_g,
                        rows_i, rows_j, rows_g, w3_ref):
    k = PEER_TOPK
    ct = q_ref.shape[0]
    key_code = lax.broadcasted_iota(jnp.int32, (PEER_N_KEYS, ct), 0).astype(F32)
    rank = lax.broadcasted_iota(jnp.int32, (k, ct), 0).astype(F32)
    for h in range(PEER_HEADS):
        st = _dot3_nt(keys_ref[h], q_ref[:, h * LANES:(h + 1) * LANES])
        s = [st[:PEER_N_KEYS], st[PEER_N_KEYS:]]
        for it in range(k):
            for p in range(2):
                m, pos, s[p] = _extract_max(s[p], key_code)
                top_v[h, p, it:it + 1, :] = m
                top_i[h, p, it:it + 1, :] = pos
    for h0 in range(0, PEER_HEADS, 2):
        heads = (h0, h0 + 1)
        cand, code, best, z = {}, {}, {}, {}
        for h in heads:
            a0, a1 = top_v[h, 0], top_v[h, 1]
            cand[h], code[h] = _pair_candidates(a0, a1)
            best[h] = a0[0:1] + a1[0:1]
            z[h] = jnp.zeros_like(best[h])
        for it in range(k):
            for h in heads:
                m, pos, cand[h] = _extract_max(cand[h], code[h])
                r0 = jnp.floor(pos * (1.0 / k))
                r1 = pos - r0 * k
                e = jnp.exp(m - best[h])
                z[h] = z[h] + e
                slot = h * k + it
                slot_i[slot:slot + 1, :] = jnp.sum(jnp.where(rank == r0, top_i[h, 0], 0.0), axis=0,
                                                   keepdims=True)
                slot_j[slot:slot + 1, :] = jnp.sum(jnp.where(rank == r1, top_i[h, 1], 0.0), axis=0,
                                                   keepdims=True)
                slot_g[slot:slot + 1, :] = e
        for h in heads:
            slot_g[h * k:(h + 1) * k, :] = slot_g[h * k:(h + 1) * k, :] / z[h]

    rows_i[...] = slot_i[...].T
    rows_j[...] = slot_j[...].T
    rows_g[...] = slot_g[...].T
    n = PEER_N_KEYS
    sub = lax.broadcasted_iota(jnp.int32, (n, n), 0).astype(F32)

    def per_token(c, carry):
        a_t = jnp.where(sub == rows_i[pl.ds(c, 1), :], rows_g[pl.ds(c, 1), :], 0.0)
        b_t = jnp.where(sub == rows_j[pl.ds(c, 1), :], 1.0, 0.0)
        w3_ref[c] = _dot_nt(a_t, b_t).astype(BF16)
        return carry

    lax.fori_loop(0, ct, per_token, 0, unroll=PEER_LOOP_UNROLL)

    g = PEER_RELAYOUT
    r_out = lax.broadcasted_iota(jnp.int32, (g * g, g * g), 0)
    r_in = lax.broadcasted_iota(jnp.int32, (g * g, g * g), 1)
    perm = jnp.where(r_in == (r_out % g) * g + r_out // g, 1.0, 0.0).astype(BF16)

    for ib in range(n // g):
        for t in range(ct // g):
            z = w3_ref[t * g:(t + 1) * g, ib * g:(ib + 1) * g, :].reshape(g * g, n)
            y = jnp.dot(perm, z, preferred_element_type=F32)
            for kk in range(g):
                col = (ib * g + kk) * n
                w_ref[t * g:(t + 1) * g, col:col + n] = y[kk * g:(kk + 1) * g].astype(BF16)


def peer_select(q, keys_padded):
    n_tok, qd = q.shape
    ct = PEER_SEL_BLOCK
    n = PEER_N_KEYS
    slots = PEER_HEADS * PEER_TOPK
    return pl.pallas_call(
        _peer_select_kernel,
        grid=(n_tok // ct,),
        in_specs=[pl.BlockSpec((ct, qd), lambda i: (i, 0)),
                  pl.BlockSpec(keys_padded.shape, lambda i: (0, 0, 0))],
        out_specs=pl.BlockSpec((ct, n * n), lambda i: (i, 0)),
        out_shape=jax.ShapeDtypeStruct((n_tok, n * n), BF16),
        scratch_shapes=[pltpu.VMEM((PEER_HEADS, 2, PEER_TOPK, ct), F32),
                        pltpu.VMEM((PEER_HEADS, 2, PEER_TOPK, ct), F32),
                        pltpu.VMEM((slots, ct), F32), pltpu.VMEM((slots, ct), F32),
                        pltpu.VMEM((slots, ct), F32),
                        pltpu.VMEM((ct, slots), F32), pltpu.VMEM((ct, slots), F32),
                        pltpu.VMEM((ct, slots), F32),
                        pltpu.VMEM((ct, n, n), BF16)],
        compiler_params=pltpu.CompilerParams(
            dimension_semantics=("parallel",), vmem_limit_bytes=VMEM_LIMIT),
        name="peer_select",
    )(q, keys_padded)


PEER_TOKEN_BLOCK = 1024
PEER_EXPERT_BLOCK = 1024


def _peer_kernel(h_ref, w_ref, u_ref, v_ref, x_ref, mod_ref, o_ref, acc_ref):
    e = pl.program_id(1)

    @pl.when(e == 0)
    def _():
        acc_ref[...] = jnp.zeros_like(acc_ref)

    s = _dot_nt(h_ref[...], u_ref[...])
    act = 0.5 * s * (1.0 + lax.erf(s * (2.0 ** -0.5)))
    acc_ref[...] += jnp.dot(act.astype(BF16) * w_ref[...], v_ref[...], preferred_element_type=F32)

    @pl.when(e == pl.num_programs(1) - 1)
    def _():
        for j in range(mod_ref.shape[0]):
            rows = slice(j * ROW_BLOCK, (j + 1) * ROW_BLOCK)
            o_ref[rows, :] = x_ref[rows, :] + mod_ref[j, 5:6, :] * acc_ref[rows, :]


def peer_dense(h2, w_sel, u_tabs, v_tabs, layer, x, mod_blocks):
    n_tok, d = x.shape
    n_exp = u_tabs.shape[1]
    tb, eb = PEER_TOKEN_BLOCK, PEER_EXPERT_BLOCK
    tok = lambda i, e: (i, 0)
    return pl.pallas_call(
        _peer_kernel,
        grid=(n_tok // tb, n_exp // eb),
        in_specs=[pl.BlockSpec((tb, d), tok),
                  pl.BlockSpec((tb, eb), lambda i, e: (i, e)),
                  pl.BlockSpec((None, eb, d), lambda i, e: (layer, e, 0)),
                  pl.BlockSpec((None, eb, d), lambda i, e: (layer, e, 0)),
                  pl.BlockSpec((tb, d), tok),
                  pl.BlockSpec((tb // ROW_BLOCK, MOD_ROWS, d), lambda i, e: (i, 0, 0))],
        out_specs=pl.BlockSpec((tb, d), tok),
        out_shape=jax.ShapeDtypeStruct((n_tok, d), F32),
        scratch_shapes=[pltpu.VMEM((tb, d), F32)],
        compiler_params=pltpu.CompilerParams(
            dimension_semantics=("parallel", "arbitrary"), vmem_limit_bytes=VMEM_LIMIT),
        name="peer_dense",
    )(h2, w_sel, u_tabs, v_tabs, x, mod_blocks)


RW_HEADS = RW_WIDTH // RW_HEAD_DIM
ROT_HALF = HEAD_DIM // 2
ROPE_THETA = 10000.0
GRID_W = 64


def _bd_from_states(s):
    a = jnp.swapaxes(s, -1, -2)
    n, d, h, m, _ = a.shape
    a = a.reshape(n, d, h // 2, 2, m, m)
    z = jnp.zeros_like(a[:, :, :, 0])
    top = jnp.concatenate([a[:, :, :, 0], z], axis=-1)
    bot = jnp.concatenate([z, a[:, :, :, 1]], axis=-1)
    return jnp.concatenate([top, bot], axis=-2)


def _states_from_bd(bd):
    n, d, p, _, _ = bd.shape
    m = RW_HEAD_DIM
    a = jnp.stack([bd[:, :, :, :m, :m], bd[:, :, :, m:, m:]], axis=3).reshape(n, d, 2 * p, m, m)
    return jnp.swapaxes(a, -1, -2)


def _rope_tables(t):
    rows = t // GRID_W
    row, col = jnp.meshgrid(jnp.arange(rows), jnp.arange(GRID_W), indexing='ij')
    inv = ROPE_THETA ** (-jnp.arange(0, ROT_HALF, 2, dtype=F32) / ROT_HALF)
    ang_r = row.reshape(-1, 1).astype(F32) * inv
    ang_c = col.reshape(-1, 1).astype(F32) * inv
    cr, sr, cc, sc = jnp.cos(ang_r), jnp.sin(ang_r), jnp.cos(ang_c), jnp.sin(ang_c)
    cos = jnp.concatenate([cr, cr, cc, cc], axis=-1)
    sin = jnp.concatenate([sr, sr, sc, sc], axis=-1)
    return jnp.tile(cos, (1, LANES // HEAD_DIM)), jnp.tile(sin, (1, LANES // HEAD_DIM))


def _pad_rows(w, top):
    z = jnp.zeros_like(w)
    return jnp.concatenate([w, z] if top else [z, w], axis=0)


def kernel(x_prompt, x_sample, c, cache_k, cache_v, state_wkv, c_ctx, w_mod, b_mod, w_in, shift_mu_prev, shift_mu_next, rw_w0, rw_w2, rw_a0, rw_a2, rw_g2, rw_k_k, rw_k_a, rw_r_k, rw_gn_w, rw_gn_b, w_out_rwkv, conv_w, conv_b, conv_ln_w, conv_ln_b, w_out_conv, q_norm, k_norm, w_out_attn, w_o, peer_wq, peer_keys, peer_u, peer_v):
    nb, seq, d = x_prompt.shape
    db, dseq, _ = x_sample.shape
    depth = w_in.shape[0]
    n_ctx = nb * seq
    n_lat = db * dseq
    n_tok = n_ctx + n_lat
    x = jnp.concatenate([x_prompt.reshape(n_ctx, d), x_sample.reshape(n_lat, d)], axis=0)

    cond = jnp.concatenate([c_ctx[None], c, jnp.zeros((MOD_ROWS - 1 - db, d), F32)], axis=0)
    blk_cond = jnp.concatenate([jnp.zeros((n_ctx // ROW_BLOCK,), jnp.int32),
                                1 + jnp.arange(n_lat // ROW_BLOCK, dtype=jnp.int32) // (dseq // ROW_BLOCK)])
    n_blk = n_tok // ROW_BLOCK
    blk_row = jnp.arange(n_blk) * ROW_BLOCK
    blk_pos = jnp.where(blk_row < n_ctx, blk_row % seq, (blk_row - n_ctx) % dseq)
    blk_len = jnp.where(blk_row < n_ctx, seq, dseq)
    starts_seq = (blk_pos == 0)[:, None]
    ends_seq = (blk_pos + ROW_BLOCK == blk_len)[:, None]
    rope = _rope_tables(dseq)
    in_sizes = (RW_COLS, 2 * CONV_WIDTH, ATT_HEADS * HEAD_DIM, 2 * ATT_KV_HEADS * HEAD_DIM, 3 * d)
    past = cache_k.shape[2]

    w_in_bf, u_bf, v_bf = w_in.astype(BF16), peer_u.astype(BF16), peer_v.astype(BF16)
    ctx_k, ctx_v, ctx_s = [], [], []
    for l in range(depth):
        mod = modulation(cond, w_mod[l], b_mod[l][None])
        mod = mod.reshape(MOD_ROWS, 6, d)[blk_cond]
        mod_blocks = jnp.concatenate([mod, jnp.zeros((mod.shape[0], MOD_ROWS - 6, d), F32)], axis=1)

        z_rw, z_cv, z_q, z_kv, z_gate = in_projection(x, mod_blocks, w_in_bf, l, in_sizes)

        zb = z_rw.reshape(n_blk, ROW_BLOCK, RW_COLS)
        halo_prev = jnp.where(starts_seq, 0.0, jnp.roll(zb[:, -1], 1, axis=0))[:, None]
        halo_next = jnp.where(ends_seq, 0.0, jnp.roll(zb[:, 0], -1, axis=0))[:, None]
        mu = jnp.stack([shift_mu_prev[l], shift_mu_next[l]])
        vecs = jnp.stack([rw_k_k[l], rw_k_a[l], rw_r_k[l], rw_w0[l, 0], rw_w0[l, 1], rw_a0[l, 0],
                          rw_a0[l, 1], jnp.zeros((RW_WIDTH,), F32)])
        w2_pad = jnp.stack([_pad_rows(rw_w2[l, 0], True), _pad_rows(rw_w2[l, 1], False)]).astype(BF16)
        a2_pad = jnp.stack([_pad_rows(rw_a2[l, 0], True), _pad_rows(rw_a2[l, 1], False)]).astype(BF16)
        r, v, kk, lw, aa, kd, bonus, gate = rwkv_prep(z_rw, halo_prev, halo_next, mu, vecs, w2_pad, a2_pad,
                                                      rw_g2[l].astype(BF16))
        s0_ctx = jnp.zeros((nb, 2, RW_HEADS // 2, LANES, LANES), F32)
        of_c, ob_c, sfin = rwkv_scan(r, v, kk, lw, aa, kd, s0_ctx, 0, nb, seq)
        of_l, ob_l, _ = rwkv_scan(r, v, kk, lw, aa, kd, _bd_from_states(state_wkv[:, l]), n_ctx, db, dseq)
        ctx_s.append(_states_from_bd(sfin))

        cvec = jnp.stack([conv_b[l], conv_ln_w[l], conv_ln_b[l]] + [jnp.zeros((CONV_WIDTH,), F32)] * 5)
        y_cv = (conv_module(z_cv, conv_w[l], cvec, 0, nb, seq),
                conv_module(z_cv, conv_w[l], cvec, n_ctx, db, dseq))

        nq = jnp.tile(q_norm[l], LANES // HEAD_DIM)[None]
        nk = jnp.tile(k_norm[l], LANES // HEAD_DIM)[None]
        y_c, k_ctx = attention(z_q, z_kv, nq, nk, 0, nb, seq)
        y_l = attention(z_q, z_kv, nq, nk, n_ctx, db, dseq, rope=rope,
                        cache=(cache_k[:, l].reshape(db * past, LANES), cache_v[:, l].reshape(db * past, LANES)))
        ctx_k.append(k_ctx.reshape(nb, seq, ATT_KV_HEADS, HEAD_DIM))
        ctx_v.append(z_kv[:n_ctx, LANES:].reshape(nb, seq, ATT_KV_HEADS, HEAD_DIM))

        x, h2, pq = merge_project(x, mod_blocks, bonus, gate, z_gate,
                                  (of_c, of_l), (ob_c, ob_l), y_cv, (y_c, y_l),
                                  jnp.stack([rw_gn_w[l], rw_gn_b[l]]),
                                  w_out_rwkv[l].astype(BF16), w_out_conv[l].astype(BF16),
                                  w_out_attn[l].astype(BF16), w_o[l].astype(BF16), peer_wq[l].astype(BF16))

        kz = jnp.zeros_like(peer_keys[l][:, 0])
        keys_padded = jnp.concatenate([jnp.concatenate([peer_keys[l][:, 0], kz], axis=-1),
                                       jnp.concatenate([kz, peer_keys[l][:, 1]], axis=-1)], axis=1)
        w_sel = peer_select(pq, keys_padded)
        x = peer_dense(h2, w_sel, u_bf, v_bf, l, x, mod_blocks)

    new_cache_k = jnp.stack(ctx_k, axis=1)
    new_cache_v = jnp.stack(ctx_v, axis=1)
    new_state = jnp.stack(ctx_s, axis=1)
    return (x[:n_ctx].reshape(nb, seq, d), x[n_ctx:].reshape(db, dseq, d),
            new_cache_k, new_cache_v, new_state)
```

```python
import functools
import math

import jax
import jax.numpy as jnp
from jax import lax
from jax.experimental import pallas as pl
from jax.experimental.pallas import tpu as pltpu

F32 = jnp.float32
BF16 = jnp.bfloat16

LANES = 128
SUBLANES = 8
SCAN_CHUNK = 64
SCAN_SUB = 16
RW_HEAD_DIM = 64
HEADS_PER_PAIR = LANES // RW_HEAD_DIM
VMEM_LIMIT = 56 * 1024 * 1024


def _dot(a, b):
    return jnp.dot(a.astype(BF16), b.astype(BF16), preferred_element_type=F32)


def _dot_nt(a, b):
    return lax.dot_general(a.astype(BF16), b.astype(BF16), (((1,), (1,)), ((), ())),
                           preferred_element_type=F32)


def _dot_tn(a, b):
    return lax.dot_general(a.astype(BF16), b.astype(BF16), (((0,), (0,)), ((), ())),
                           preferred_element_type=F32)


def _dot3(a, b):
    a_hi = a.astype(BF16)
    b_hi = b.astype(BF16)
    a_lo = (a - a_hi.astype(F32)).astype(BF16)
    b_lo = (b - b_hi.astype(F32)).astype(BF16)
    d = functools.partial(jnp.dot, preferred_element_type=F32)
    return d(a_hi, b_hi) + (d(a_hi, b_lo) + d(a_lo, b_hi))


def _scan_masks(reverse):
    C = SCAN_CHUNK
    P = HEADS_PER_PAIR * C
    row = lax.broadcasted_iota(jnp.int32, (P, LANES), 0)
    col = lax.broadcasted_iota(jnp.int32, (P, LANES), 1)
    tr = row % C
    tc = col % C
    t_i = lax.broadcasted_iota(jnp.int32, (C, C), 0)
    s_i = lax.broadcasted_iota(jnp.int32, (C, C), 1)
    return dict(
        same_head=(row // C) == (col // RW_HEAD_DIM),
        incl=(tr >= tc) if not reverse else (tr <= tc),
        strict=(tr > tc) if not reverse else (tr < tc),
        diag_blk=(tr // SCAN_SUB) == (tc // SCAN_SUB),
        eye=row == col,
        cum_mat=jnp.where((t_i >= s_i) if not reverse else (t_i <= s_i), 1.0, 0.0).astype(F32))


def _scan_chunk(chains, states):
    C = SCAN_CHUNK
    n = range(len(chains))
    mk = [ch[6] for ch in chains]
    cum = [ch[8] for ch in chains]
    pre = []
    for i in n:
        r, lw, k, v, kk, a, m, reverse, _ = chains[i]
        g = jnp.exp(cum[i])
        g_prev = jnp.exp(cum[i] - lw)
        g_inv = jnp.exp(-cum[i])
        g_end = jnp.exp(cum[i][C - 1:C, :] if not reverse else cum[i][0:1, :])
        bt = kk * a * g_inv
        kt = k * g_inv
        stack = lambda x, m=m: jnp.where(m["same_head"], jnp.concatenate([x, x], axis=0), 0.0)
        pre.append(dict(kap=stack(kk * g_prev), rt=stack(r * g), bt=stack(bt), kt=stack(kt),
                        v=stack(v), bh=stack(bt * g_end), kh=stack(kt * g_end), g_end=g_end))
    P = HEADS_PER_PAIR * C
    tri = [_dot_nt(jnp.concatenate([pre[i]["kap"], pre[i]["rt"]], axis=0),
                   jnp.concatenate([pre[i]["bt"], pre[i]["kt"]], axis=0)) for i in n]
    lb = [jnp.where(mk[i]["strict"], tri[i][:P, :P], 0.0) for i in n]
    lk = [jnp.where(mk[i]["strict"], tri[i][:P, P:], 0.0) for i in n]
    pb = [jnp.where(mk[i]["incl"], tri[i][P:, :P], 0.0) for i in n]
    pk = [jnp.where(mk[i]["incl"], tri[i][P:, P:], 0.0) for i in n]
    lkv = [_dot(lk[i], pre[i]["v"]) for i in n]

    d = [jnp.where(mk[i]["diag_blk"], lb[i], 0.0) for i in n]
    inv = [jnp.where(mk[i]["eye"], 1.0, 0.0) - d[i] for i in n]
    p = d
    for _ in range(3):
        p = [_dot(p[i], p[i]) for i in n]
        inv = [inv[i] + _dot(inv[i], p[i]) for i in n]
    x = [_dot(inv[i], jnp.concatenate([lb[i] - d[i], pre[i]["kap"], lkv[i]], axis=1)) for i in n]
    e = [x[i][:, :LANES] for i in n]
    rhs = [x[i][:, LANES:] for i in n]
    e2 = [_dot(e[i], e[i]) for i in n]
    t = [rhs[i] + _dot(e2[i], rhs[i]) for i in n]
    gy = [t[i] - _dot(e[i], t[i]) for i in n]

    bhgy = [_dot_tn(pre[i]["bh"], gy[i]) for i in n]
    m_mat = [jnp.where(mk[i]["eye"], pre[i]["g_end"], 0.0) - bhgy[i][:, :LANES] for i in n]
    n_mat = [_dot_tn(pre[i]["kh"], pre[i]["v"]) - bhgy[i][:, LANES:] for i in n]
    pbgy = [_dot(pb[i], gy[i]) for i in n]
    q = [pre[i]["rt"] - pbgy[i][:, :LANES] for i in n]
    z = [_dot(pk[i], pre[i]["v"]) - pbgy[i][:, LANES:] for i in n]
    o_st = [_dot(q[i], states[i]) + z[i] for i in n]
    new = [_dot3(m_mat[i], states[i]) + n_mat[i] for i in n]
    return [(o_st[i][:C] + o_st[i][C:], new[i]) for i in n]


def _scan_kernel(rf, vf, kkf, lwf, af, kf, rb, vb, kkb, lwb, ab, kb, s0, of, ob, sfin, state):
    c = pl.program_id(1)
    n_pairs = rf.shape[-1] // LANES

    @pl.when(c == 0)
    def _():
        state[...] = s0[0]

    chains, states, outs = [], [], []
    for d, (r_, v_, kk_, lw_, a_, k_, o_) in enumerate(((rf, vf, kkf, lwf, af, kf, of),
                                                        (rb, vb, kkb, lwb, ab, kb, ob))):
        masks = _scan_masks(reverse=(d == 1))
        cum = _dot3(masks["cum_mat"], lw_[...])
        for p in range(n_pairs):
            sl = slice(p * LANES, (p + 1) * LANES)
            chains.append((r_[:, sl], lw_[:, sl], k_[:, sl], v_[:, sl], kk_[:, sl], a_[:, sl],
                           masks, d == 1, cum[:, sl]))
            states.append(state[d, p])
            outs.append((o_, sl, d, p))
    for (o, new), (o_, sl, d, p) in zip(_scan_chunk(chains, states), outs):
        o_[:, sl] = o
        state[d, p] = new

    @pl.when(c == pl.num_programs(1) - 1)
    def _():
        sfin[0] = state[...]


def rwkv_scan(r, v, kk, lw, a, kd, s0_bd, row0, n_seq, seq_len):
    width = r.shape[1]
    n_tok = n_seq * seq_len
    n_pairs = width // LANES
    n_chunk = seq_len // SCAN_CHUNK
    c0 = row0 // SCAN_CHUNK
    blk = (SCAN_CHUNK, width)
    dblk = (None, SCAN_CHUNK, width)
    fwd = lambda s, c: (s * n_chunk + c, 0)
    bwd = lambda s, c: (s * n_chunk + n_chunk - 1 - c, 0)
    fwd_in = lambda s, c: (c0 + s * n_chunk + c, 0)
    bwd_in = lambda s, c: (c0 + s * n_chunk + n_chunk - 1 - c, 0)
    fwd_d = lambda s, c: (0, c0 + s * n_chunk + c, 0)
    bwd_d = lambda s, c: (1, c0 + s * n_chunk + n_chunk - 1 - c, 0)
    sspec = pl.BlockSpec((1, 2, n_pairs, LANES, LANES), lambda s, c: (s, 0, 0, 0, 0))
    in_specs = ([pl.BlockSpec(blk, fwd_in)] * 3 + [pl.BlockSpec(dblk, fwd_d)] * 3
                + [pl.BlockSpec(blk, bwd_in)] * 3 + [pl.BlockSpec(dblk, bwd_d)] * 3 + [sspec])
    out_specs = [pl.BlockSpec(blk, fwd), pl.BlockSpec(blk, bwd), sspec]
    out_shape = [jax.ShapeDtypeStruct((n_tok, width), F32)] * 2 + [
        jax.ShapeDtypeStruct(s0_bd.shape, F32)]
    return pl.pallas_call(
        _scan_kernel,
        grid=(n_seq, n_chunk),
        in_specs=in_specs,
        out_specs=out_specs,
        out_shape=out_shape,
        scratch_shapes=[pltpu.VMEM((2, n_pairs, LANES, LANES), F32)],
        compiler_params=pltpu.CompilerParams(
            dimension_semantics=("parallel", "arbitrary"), vmem_limit_bytes=VMEM_LIMIT),
        name="rwkv_scan",
    )(r, v, kk, lw, a, kd, r, v, kk, lw, a, kd, s0_bd)


MOD_ROWS = 8
ROW_BLOCK = 256
RMS_EPS = 1e-6
LN_EPS = 1e-5
RW_GN_EPS = 64e-5


def _mod_kernel(c_ref, w_ref, b_ref, o_ref):
    c = c_ref[...]
    s = c * jax.nn.sigmoid(c)
    o_ref[...] = _dot(s, w_ref[...]) + b_ref[...]


def modulation(cond, w_mod, b_mod):
    n, d = cond.shape
    cols = w_mod.shape[1]
    blk = 1024
    return pl.pallas_call(
        _mod_kernel,
        grid=(cols // blk,),
        in_specs=[pl.BlockSpec((n, d), lambda j: (0, 0)),
                  pl.BlockSpec((d, blk), lambda j: (0, j)),
                  pl.BlockSpec((1, blk), lambda j: (0, j))],
        out_specs=pl.BlockSpec((n, blk), lambda j: (0, j)),
        out_shape=jax.ShapeDtypeStruct((n, cols), F32),
        name="modulation",
    )(cond, w_mod, b_mod)


def _modulate(x, shift, scale):
    ms = jnp.mean(x * x, axis=-1, keepdims=True)
    return x * lax.rsqrt(ms + RMS_EPS) * (1.0 + scale) + shift


def _inproj_kernel(x_ref, mod_ref, w_ref, *o_refs):
    h = _modulate(x_ref[...], mod_ref[0, 0:1, :], mod_ref[0, 1:2, :]).astype(BF16)
    start = 0
    for o_ref in o_refs:
        n = o_ref.shape[1]
        o_ref[...] = jnp.dot(h, w_ref[:, start:start + n], preferred_element_type=F32)
        start += n


def in_projection(x, mod_blocks, w_in, layer, sizes):
    n_tok, d = x.shape
    row = lambda i: (i, 0)
    return pl.pallas_call(
        _inproj_kernel,
        grid=(n_tok // ROW_BLOCK,),
        in_specs=[pl.BlockSpec((ROW_BLOCK, d), row),
                  pl.BlockSpec((1, MOD_ROWS, d), lambda i: (i, 0, 0)),
                  pl.BlockSpec((None, d, w_in.shape[2]), lambda i: (layer, 0, 0),
                               pipeline_mode=pl.Buffered(1))],
        out_specs=[pl.BlockSpec((ROW_BLOCK, n), row) for n in sizes],
        out_shape=[jax.ShapeDtypeStruct((n_tok, n), F32) for n in sizes],
        compiler_params=pltpu.CompilerParams(
            dimension_semantics=("parallel",), vmem_limit_bytes=VMEM_LIMIT),
        name="in_projection",
    )(x, mod_blocks, w_in)


HEAD_DIM = 64


def _group_ones():
    r = lax.broadcasted_iota(jnp.int32, (LANES, LANES), 0) // HEAD_DIM
    c = lax.broadcasted_iota(jnp.int32, (LANES, LANES), 1) // HEAD_DIM
    return jnp.where(r == c, 1.0, 0.0).astype(BF16)


def _group_sum(x, ones):
    hi = x.astype(BF16)
    lo = (x - hi.astype(F32)).astype(BF16)
    d = functools.partial(jnp.dot, preferred_element_type=F32)
    tiles = [slice(s, s + LANES) for s in range(0, x.shape[1], LANES)]
    sums = [d(hi[:, t], ones) + d(lo[:, t], ones) for t in tiles]
    return sums[0] if len(sums) == 1 else jnp.concatenate(sums, axis=1)


RW_WIDTH = 512
RW_COLS = 3 * RW_WIDTH + 3 * LANES
RW_VEC_ROWS = 8


def _rwkv_prep_kernel(z_ref, hp_ref, hn_ref, mu_ref, vec_ref, w2_ref, a2_ref, g2_ref,
                      r_ref, v_ref, kk_ref, lw_ref, a_ref, kd_ref, bonus_ref, gate_ref):
    z = z_ref[...]
    rb = z.shape[0]
    row = lax.broadcasted_iota(jnp.int32, (rb, 1), 0)
    z_prev = jnp.where(row == 0, hp_ref[0], pltpu.roll(z, 1, 0))
    z_next = jnp.where(row == rb - 1, hn_ref[0], pltpu.roll(z, rb - 1, 0))
    zs = z + mu_ref[0:1, :] * (z_prev - z) + mu_ref[1:2, :] * (z_next - z)
    w = RW_WIDTH
    r, k, v = zs[:, :w], zs[:, w:2 * w], zs[:, 2 * w:3 * w]
    zw = zs[:, 3 * w:3 * w + LANES]
    za = zs[:, 3 * w + LANES:3 * w + 2 * LANES]
    zg = zs[:, 3 * w + 2 * LANES:]
    ones = _group_ones()
    kk = k * vec_ref[0:1, :]
    kk = kk * lax.rsqrt(_group_sum(kk * kk, ones) + 1e-12)
    r_ref[...] = r
    v_ref[...] = v
    kk_ref[...] = kk
    tz = jnp.tanh(zw)
    for d in range(2):
        w_raw = vec_ref[3 + d:4 + d, :] + _dot(tz, w2_ref[d])
        lw_ref[d] = -math.exp(-0.5) * jax.nn.sigmoid(w_raw)
        a = jax.nn.sigmoid(vec_ref[5 + d:6 + d, :] + _dot(za, a2_ref[d]))
        a_ref[d] = a
        kd_ref[d] = k * (1.0 + (a - 1.0) * vec_ref[1:2, :])
    bonus_ref[...] = _group_sum(r * k * vec_ref[2:3, :], ones) * v
    gate_ref[...] = _dot(jax.nn.sigmoid(zg), g2_ref[...])


def rwkv_prep(z_rw, halo_prev, halo_next, mu, vecs, w2_pad, a2_pad, g2):
    n_tok = z_rw.shape[0]
    w = RW_WIDTH
    row = lambda i: (i, 0)
    const2 = lambda i: (0, 0)
    const3 = lambda i: (0, 0, 0)
    halo = pl.BlockSpec((1, 1, RW_COLS), lambda i: (i, 0, 0))
    one = pl.BlockSpec((ROW_BLOCK, w), row)
    two = pl.BlockSpec((2, ROW_BLOCK, w), lambda i: (0, i, 0))
    s1 = jax.ShapeDtypeStruct((n_tok, w), F32)
    s2 = jax.ShapeDtypeStruct((2, n_tok, w), F32)
    return pl.pallas_call(
        _rwkv_prep_kernel,
        grid=(n_tok // ROW_BLOCK,),
        in_specs=[pl.BlockSpec((ROW_BLOCK, RW_COLS), row), halo, halo,
                  pl.BlockSpec(mu.shape, const2), pl.BlockSpec(vecs.shape, const2),
                  pl.BlockSpec(w2_pad.shape, const3), pl.BlockSpec(a2_pad.shape, const3),
                  pl.BlockSpec(g2.shape, const2)],
        out_specs=[one, one, one, two, two, two, one, one],
        out_shape=[s1, s1, s1, s2, s2, s2, s1, s1],
        compiler_params=pltpu.CompilerParams(
            dimension_semantics=("parallel",), vmem_limit_bytes=VMEM_LIMIT),
        name="rwkv_prep",
    )(z_rw, halo_prev, halo_next, mu, vecs, w2_pad, a2_pad, g2)


CONV_WIDTH = 512
CONV_KERNEL = 31
CONV_PAD = 16
CONV_ROWS = 64


def _conv_kernel(z_ref, w_ref, vec_ref, o_ref, pad_ref):
    t = z_ref.shape[0]
    cw = CONV_WIDTH
    half = CONV_KERNEL // 2
    zeros = jnp.zeros((CONV_PAD, cw), F32)
    pad_ref[0:CONV_PAD, :] = zeros
    pad_ref[CONV_PAD + t:CONV_PAD + t + CONV_PAD, :] = zeros
    pad_ref[CONV_PAD:CONV_PAD + t, :] = z_ref[:, :cw] * jax.nn.sigmoid(z_ref[:, cw:])

    def chunk(i, carry):
        base = pl.multiple_of(i * CONV_ROWS, CONV_ROWS)
        window = pad_ref[pl.ds(base, CONV_ROWS + 2 * CONV_PAD), :]
        span = CONV_ROWS + 2 * CONV_PAD - SUBLANES
        shifted = [window[rho:rho + span] for rho in range(SUBLANES)]
        acc = jnp.zeros((CONV_ROWS, cw), F32)
        for j in range(CONV_KERNEL):
            lo = CONV_PAD - half + j
            tile = (lo // SUBLANES) * SUBLANES
            acc = acc + w_ref[j:j + 1, :] * shifted[lo % SUBLANES][tile:tile + CONV_ROWS]
        u = acc + vec_ref[0:1, :]
        mu = jnp.mean(u, axis=-1, keepdims=True)
        cen = u - mu
        var = jnp.mean(cen * cen, axis=-1, keepdims=True)
        y = cen * lax.rsqrt(var + LN_EPS) * vec_ref[1:2, :] + vec_ref[2:3, :]
        o_ref[pl.ds(base, CONV_ROWS), :] = y * jax.nn.sigmoid(y)
        return carry

    lax.fori_loop(0, t // CONV_ROWS, chunk, 0)


def conv_module(z_cv, conv_w, vecs, row0, n_seq, seq_len):
    cw = CONV_WIDTH
    blk0 = row0 // seq_len
    return pl.pallas_call(
        _conv_kernel,
        grid=(n_seq,),
        in_specs=[pl.BlockSpec((seq_len, 2 * cw), lambda i: (blk0 + i, 0)),
                  pl.BlockSpec(conv_w.shape, lambda i: (0, 0)),
                  pl.BlockSpec(vecs.shape, lambda i: (0, 0))],
        out_specs=pl.BlockSpec((seq_len, cw), lambda i: (i, 0)),
        out_shape=jax.ShapeDtypeStruct((n_seq * seq_len, cw), F32),
        scratch_shapes=[pltpu.VMEM((seq_len + 2 * CONV_PAD, cw), F32)],
        compiler_params=pltpu.CompilerParams(
            dimension_semantics=("parallel",), vmem_limit_bytes=VMEM_LIMIT),
        name="conv_module",
    )(z_cv, conv_w, vecs)


ATT_Q_BLOCK = 256
ATT_HEADS = 8
ATT_KV_HEADS = 2
ATT_GROUP = ATT_HEADS // ATT_KV_HEADS


def _rot_partner():
    r = lax.broadcasted_iota(jnp.int32, (LANES, LANES), 0)
    c = lax.broadcasted_iota(jnp.int32, (LANES, LANES), 1)
    lo = (r % 32) < 16
    return jnp.where((c == r + 16) & lo, 1.0, jnp.where((c == r - 16) & ~lo, -1.0, 0.0)).astype(BF16)


def _attn_kernel(*refs, rotary):
    if rotary:
        (zq_ref, zkv_ref, nq_ref, nk_ref, cosq_ref, sinq_ref, cosk_ref, sink_ref, ck_ref, cv_ref,
         o_ref) = refs
    else:
        zq_ref, zkv_ref, nq_ref, nk_ref, o_ref, kn_ref = refs
    tq = zq_ref.shape[0]
    ones = _group_ones()
    lane = lax.broadcasted_iota(jnp.int32, (1, LANES), 1)
    first = lane < HEAD_DIM
    row_head = lax.broadcasted_iota(jnp.int32, (2 * tq, LANES), 0) // tq
    own = row_head == (lax.broadcasted_iota(jnp.int32, (2 * tq, LANES), 1) // HEAD_DIM)

    def norm(x, g):
        return x * lax.rsqrt(_group_sum(x * x, ones) * (1.0 / HEAD_DIM) + RMS_EPS) * g

    def rope(x, cos, sin):
        return x * cos + jnp.dot(x.astype(BF16), _rot_partner(), preferred_element_type=F32) * sin

    def dup(x):
        sw = pltpu.roll(x, HEAD_DIM, 1)
        return jnp.where(first, x, sw), jnp.where(first, sw, x)

    k = norm(zkv_ref[:, :LANES], nk_ref[...])
    v = zkv_ref[:, LANES:]
    if rotary:
        k = rope(k, cosk_ref[...], sink_ref[...])
        ck, cv = dup(ck_ref[...]), dup(cv_ref[...])
    else:
        kn_ref[...] = k
    kd, vd = dup(k), dup(v)
    for pair in range(ATT_HEADS // 2):
        g = pair // (ATT_GROUP // 2)
        q = norm(zq_ref[:, pair * LANES:(pair + 1) * LANES], nq_ref[...])
        if rotary:
            q = rope(q, cosq_ref[...], sinq_ref[...])
        q = q * (HEAD_DIM ** -0.5)
        qs = jnp.where(own, jnp.concatenate([q, q], axis=0), 0.0)
        s = _dot_nt(qs, kd[g])
        m = jnp.max(s, axis=-1, keepdims=True)
        if rotary:
            s2 = _dot_nt(qs, ck[g])
            m = jnp.maximum(m, jnp.max(s2, axis=-1, keepdims=True))
            p2 = jnp.exp(s2 - m)
        p = jnp.exp(s - m)
        den = jnp.sum(p, axis=-1, keepdims=True)
        o = _dot(p, vd[g])
        if rotary:
            den = den + jnp.sum(p2, axis=-1, keepdims=True)
            o = o + _dot(p2, cv[g])
        o = o / den
        o_ref[:, pair * LANES:(pair + 1) * LANES] = jnp.where(first, o[:tq], o[tq:])


def attention(z_q, z_kv, nq, nk, row0, n_seq, seq_len, rope=None, cache=None):
    qb = min(ATT_Q_BLOCK, seq_len)
    nqb = seq_len // qb
    qblk0 = row0 // qb
    sblk0 = row0 // seq_len
    wq = z_q.shape[1]
    in_specs = [pl.BlockSpec((qb, wq), lambda b, t: (qblk0 + b * nqb + t, 0)),
                pl.BlockSpec((seq_len, 2 * LANES), lambda b, t: (sblk0 + b, 0)),
                pl.BlockSpec((1, LANES), lambda b, t: (0, 0)),
                pl.BlockSpec((1, LANES), lambda b, t: (0, 0))]
    args = [z_q, z_kv, nq, nk]
    out_q = pl.BlockSpec((qb, wq), lambda b, t: (b * nqb + t, 0))
    y_shape = jax.ShapeDtypeStruct((n_seq * seq_len, wq), F32)
    if rope is None:
        out_specs = [out_q, pl.BlockSpec((seq_len, LANES), lambda b, t: (b, 0))]
        out_shape = [y_shape, jax.ShapeDtypeStruct((n_seq * seq_len, LANES), F32)]
    else:
        past = cache[0].shape[0] // n_seq
        in_specs += [pl.BlockSpec((qb, LANES), lambda b, t: (t, 0))] * 2
        in_specs += [pl.BlockSpec((seq_len, LANES), lambda b, t: (0, 0))] * 2
        in_specs += [pl.BlockSpec((past, LANES), lambda b, t: (b, 0))] * 2
        args += [rope[0], rope[1], rope[0], rope[1], cache[0], cache[1]]
        out_specs = out_q
        out_shape = y_shape
    return pl.pallas_call(
        functools.partial(_attn_kernel, rotary=rope is not None),
        grid=(n_seq, nqb),
        in_specs=in_specs,
        out_specs=out_specs,
        out_shape=out_shape,
        compiler_params=pltpu.CompilerParams(
            dimension_semantics=("parallel", "arbitrary"), vmem_limit_bytes=VMEM_LIMIT),
        name="attention",
    )(*args)


def _merge_kernel(x_ref, mod_ref, bonus_ref, gate_ref, zg_ref,
                  ofc_ref, ofl_ref, obc_ref, obl_ref, cvc_ref, cvl_ref, atc_ref, atl_ref,
                  gn_ref, wr_ref, wc_ref, wa_ref, wo_ref, wq_ref, xo_ref, h2_ref, q_ref, *, n_ctx_blocks):
    d = x_ref.shape[1]
    zg = zg_ref[...]
    is_ctx = pl.program_id(0) < n_ctx_blocks
    pick = lambda c_ref, l_ref: jnp.where(is_ctx, c_ref[...], l_ref[...])
    ones = _group_ones()
    o = pick(ofc_ref, ofl_ref) + pick(obc_ref, obl_ref)
    cen = o - _group_sum(o, ones) * (1.0 / HEAD_DIM)
    var = _group_sum(cen * cen, ones) * (1.0 / HEAD_DIM)
    o = cen * lax.rsqrt(var + RW_GN_EPS) * gn_ref[0:1, :] + gn_ref[1:2, :]
    y_rw = (o + bonus_ref[...]) * gate_ref[...]
    merged = (jax.nn.sigmoid(zg[:, :d]) * _dot(y_rw, wr_ref[...])
              + jax.nn.sigmoid(zg[:, d:2 * d]) * _dot(pick(cvc_ref, cvl_ref), wc_ref[...])
              + jax.nn.sigmoid(zg[:, 2 * d:]) * _dot(pick(atc_ref, atl_ref), wa_ref[...]))
    x = x_ref[...] + mod_ref[0, 2:3, :] * _dot(merged, wo_ref[...])
    xo_ref[...] = x
    h2 = _modulate(x, mod_ref[0, 3:4, :], mod_ref[0, 4:5, :]).astype(BF16)
    h2_ref[...] = h2
    q_ref[...] = jnp.dot(h2, wq_ref[...], preferred_element_type=F32)


def merge_project(x, mod_blocks, bonus, gate, z_gate, o_f, o_b, y_cv, y_at, gn, w_r, w_c, w_a, w_o, w_q):
    n_tok, d = x.shape
    nc = o_f[0].shape[0] // ROW_BLOCK
    row = lambda i: (i, 0)
    const = lambda i: (0, 0)
    ctx_row = lambda i: (jnp.minimum(i, nc - 1), 0)
    lat_row = lambda i: (jnp.maximum(i - nc, 0), 0)
    acts = [bonus, gate, z_gate]
    weights = [gn, w_r, w_c, w_a, w_o, w_q]
    pair_specs, pair_args = [], []
    for c_arr, l_arr in (o_f, o_b, y_cv, y_at):
        pair_specs += [pl.BlockSpec((ROW_BLOCK, c_arr.shape[1]), ctx_row),
                       pl.BlockSpec((ROW_BLOCK, l_arr.shape[1]), lat_row)]
        pair_args += [c_arr, l_arr]
    return pl.pallas_call(
        functools.partial(_merge_kernel, n_ctx_blocks=nc),
        grid=(n_tok // ROW_BLOCK,),
        in_specs=[pl.BlockSpec((ROW_BLOCK, d), row),
                  pl.BlockSpec((1, MOD_ROWS, d), lambda i: (i, 0, 0))]
                 + [pl.BlockSpec((ROW_BLOCK, a.shape[1]), row) for a in acts]
                 + pair_specs
                 + [pl.BlockSpec(w.shape, const) for w in weights],
        out_specs=[pl.BlockSpec((ROW_BLOCK, d), row), pl.BlockSpec((ROW_BLOCK, d), row),
                   pl.BlockSpec((ROW_BLOCK, w_q.shape[1]), row)],
        out_shape=[jax.ShapeDtypeStruct((n_tok, d), F32), jax.ShapeDtypeStruct((n_tok, d), BF16),
                   jax.ShapeDtypeStruct((n_tok, w_q.shape[1]), F32)],
        compiler_params=pltpu.CompilerParams(
            dimension_semantics=("parallel",), vmem_limit_bytes=VMEM_LIMIT),
        name="merge_project",
    )(x, mod_blocks, *acts, *pair_args, *weights)


PEER_HEADS = 8
PEER_N_KEYS = 128
PEER_HALF = 64
PEER_TOPK = 16
PEER_SEL_BLOCK = 128
PEER_LOOP_UNROLL = 16
PEER_RELAYOUT = 16
NEG = -1e30


def _dot3_nt(a, b):
    a_hi = a.astype(BF16)
    b_hi = b.astype(BF16)
    a_lo = (a - a_hi.astype(F32)).astype(BF16)
    b_lo = (b - b_hi.astype(F32)).astype(BF16)
    d = lambda x, y: lax.dot_general(x, y, (((1,), (1,)), ((), ())), preferred_element_type=F32)
    return d(a_hi, b_hi) + (d(a_hi, b_lo) + d(a_lo, b_hi))


def _extract_max(s, code):
    vals = [s[i:i + SUBLANES] for i in range(0, s.shape[0], SUBLANES)]
    codes = [code[i:i + SUBLANES] for i in range(0, s.shape[0], SUBLANES)]
    while len(vals) > 1:
        nv, nc = [], []
        for i in range(0, len(vals) - 1, 2):
            keep = vals[i] >= vals[i + 1]
            nv.append(jnp.maximum(vals[i], vals[i + 1]))
            nc.append(jnp.where(keep, codes[i], codes[i + 1]))
        if len(vals) % 2:
            nv.append(vals[-1])
            nc.append(codes[-1])
        vals, codes = nv, nc
    m = jnp.max(vals[0], axis=0, keepdims=True)
    pos = jnp.min(jnp.where(vals[0] == m, codes[0], 1e9), axis=0, keepdims=True)
    return m, pos, jnp.where(code == pos, NEG, s)


def _pair_candidates(a0, a1):
    k = PEER_TOPK
    sub = lax.broadcasted_iota(jnp.int32, (k, a0.shape[1]), 0).astype(F32)
    sub8 = sub[:8]
    vals = [a0[0:1] + a1]
    codes = [sub]
    for r0 in range(1, 8):
        limit = k // (r0 + 1)
        vals.append(jnp.where(sub8 < limit, a0[r0:r0 + 1] + a1[:8], NEG))
        codes.append(sub8 + float(r0 * k))
    vals.append(a0[8:] + a1[0:1])
    codes.append((sub8 + 8.0) * float(k))
    return jnp.concatenate(vals, axis=0), jnp.concatenate(codes, axis=0)


def _peer_select_kernel(q_ref, keys_ref, w_ref, top_v, top_i, slot_i, slot_j, slot_g,
                        rows_i, rows_j, rows_g, w3_ref):
    k = PEER_TOPK
    ct = q_ref.shape[0]
    key_code = lax.broadcasted_iota(jnp.int32, (PEER_N_KEYS, ct), 0).astype(F32)
    rank = lax.broadcasted_iota(jnp.int32, (k, ct), 0).astype(F32)
    for h in range(PEER_HEADS):
        st = _dot3_nt(keys_ref[h], q_ref[:, h * LANES:(h + 1) * LANES])
        s = [st[:PEER_N_KEYS], st[PEER_N_KEYS:]]
        for it in range(k):
            for p in range(2):
                m, pos, s[p] = _extract_max(s[p], key_code)
                top_v[h, p, it:it + 1, :] = m
                top_i[h, p, it:it + 1, :] = pos
    for h0 in range(0, PEER_HEADS, 2):
        heads = (h0, h0 + 1)
        cand, code, best, z = {}, {}, {}, {}
        for h in heads:
            a0, a1 = top_v[h, 0], top_v[h, 1]
            cand[h], code[h] = _pair_candidates(a0, a1)
            best[h] = a0[0:1] + a1[0:1]
            z[h] = jnp.zeros_like(best[h])
        for it in range(k):
            for h in heads:
                m, pos, cand[h] = _extract_max(cand[h], code[h])
                r0 = jnp.floor(pos * (1.0 / k))
                r1 = pos - r0 * k
                e = jnp.exp(m - best[h])
                z[h] = z[h] + e
                slot = h * k + it
                slot_i[slot:slot + 1, :] = jnp.sum(jnp.where(rank == r0, top_i[h, 0], 0.0), axis=0,
                                                   keepdims=True)
                slot_j[slot:slot + 1, :] = jnp.sum(jnp.where(rank == r1, top_i[h, 1], 0.0), axis=0,
                                                   keepdims=True)
                slot_g[slot:slot + 1, :] = e
        for h in heads:
            slot_g[h * k:(h + 1) * k, :] = slot_g[h * k:(h + 1) * k, :] / z[h]

    rows_i[...] = slot_i[...].T
    rows_j[...] = slot_j[...].T
    rows_g[...] = slot_g[...].T
    n = PEER_N_KEYS
    sub = lax.broadcasted_iota(jnp.int32, (n, n), 0).astype(F32)

    def per_token(c, carry):
        a_t = jnp.where(sub == rows_i[pl.ds(c, 1), :], rows_g[pl.ds(c, 1), :], 0.0)
        b_t = jnp.where(sub == rows_j[pl.ds(c, 1), :], 1.0, 0.0)
        w3_ref[c] = _dot_nt(a_t, b_t).astype(BF16)
        return carry

    lax.fori_loop(0, ct, per_token, 0, unroll=PEER_LOOP_UNROLL)

    g = PEER_RELAYOUT
    r_out = lax.broadcasted_iota(jnp.int32, (g * g, g * g), 0)
    r_in = lax.broadcasted_iota(jnp.int32, (g * g, g * g), 1)
    perm = jnp.where(r_in == (r_out % g) * g + r_out // g, 1.0, 0.0).astype(BF16)

    for ib in range(n // g):
        for t in range(ct // g):
            z = w3_ref[t * g:(t + 1) * g, ib * g:(ib + 1) * g, :].reshape(g * g, n)
            y = jnp.dot(perm, z, preferred_element_type=F32)
            for kk in range(g):
                col = (ib * g + kk) * n
                w_ref[t * g:(t + 1) * g, col:col + n] = y[kk * g:(kk + 1) * g].astype(BF16)


def peer_select(q, keys_padded):
    n_tok, qd = q.shape
    ct = PEER_SEL_BLOCK
    n = PEER_N_KEYS
    slots = PEER_HEADS * PEER_TOPK
    return pl.pallas_call(
        _peer_select_kernel,
        grid=(n_tok // ct,),
        in_specs=[pl.BlockSpec((ct, qd), lambda i: (i, 0)),
                  pl.BlockSpec(keys_padded.shape, lambda i: (0, 0, 0))],
        out_specs=pl.BlockSpec((ct, n * n), lambda i: (i, 0)),
        out_shape=jax.ShapeDtypeStruct((n_tok, n * n), BF16),
        scratch_shapes=[pltpu.VMEM((PEER_HEADS, 2, PEER_TOPK, ct), F32),
                        pltpu.VMEM((PEER_HEADS, 2, PEER_TOPK, ct), F32),
                        pltpu.VMEM((slots, ct), F32), pltpu.VMEM((slots, ct), F32),
                        pltpu.VMEM((slots, ct), F32),
                        pltpu.VMEM((ct, slots), F32), pltpu.VMEM((ct, slots), F32),
                        pltpu.VMEM((ct, slots), F32),
                        pltpu.VMEM((ct, n, n), BF16)],
        compiler_params=pltpu.CompilerParams(
            dimension_semantics=("parallel",), vmem_limit_bytes=VMEM_LIMIT),
        name="peer_select",
    )(q, keys_padded)


PEER_TOKEN_BLOCK = 1024
PEER_EXPERT_BLOCK = 1024


def _peer_kernel(h_ref, w_ref, u_ref, v_ref, x_ref, mod_ref, o_ref, acc_ref):
    e = pl.program_id(1)

    @pl.when(e == 0)
    def _():
        acc_ref[...] = jnp.zeros_like(acc_ref)

    s = _dot_nt(h_ref[...], u_ref[...])
    act = 0.5 * s * (1.0 + lax.erf(s * (2.0 ** -0.5)))
    acc_ref[...] += jnp.dot(act.astype(BF16) * w_ref[...], v_ref[...], preferred_element_type=F32)

    @pl.when(e == pl.num_programs(1) - 1)
    def _():
        for j in range(mod_ref.shape[0]):
            rows = slice(j * ROW_BLOCK, (j + 1) * ROW_BLOCK)
            o_ref[rows, :] = x_ref[rows, :] + mod_ref[j, 5:6, :] * acc_ref[rows, :]


def peer_dense(h2, w_sel, u_tabs, v_tabs, layer, x, mod_blocks):
    n_tok, d = x.shape
    n_exp = u_tabs.shape[1]
    tb, eb = PEER_TOKEN_BLOCK, PEER_EXPERT_BLOCK
    tok = lambda i, e: (i, 0)
    return pl.pallas_call(
        _peer_kernel,
        grid=(n_tok // tb, n_exp // eb),
        in_specs=[pl.BlockSpec((tb, d), tok),
                  pl.BlockSpec((tb, eb), lambda i, e: (i, e)),
                  pl.BlockSpec((None, eb, d), lambda i, e: (layer, e, 0)),
                  pl.BlockSpec((None, eb, d), lambda i, e: (layer, e, 0)),
                  pl.BlockSpec((tb, d), tok),
                  pl.BlockSpec((tb // ROW_BLOCK, MOD_ROWS, d), lambda i, e: (i, 0, 0))],
        out_specs=pl.BlockSpec((tb, d), tok),
        out_shape=jax.ShapeDtypeStruct((n_tok, d), F32),
        scratch_shapes=[pltpu.VMEM((tb, d), F32)],
        compiler_params=pltpu.CompilerParams(
            dimension_semantics=("parallel", "arbitrary"), vmem_limit_bytes=VMEM_LIMIT),
        name="peer_dense",
    )(h2, w_sel, u_tabs, v_tabs, x, mod_blocks)


RW_HEADS = RW_WIDTH // RW_HEAD_DIM
ROT_HALF = HEAD_DIM // 2
ROPE_THETA = 10000.0
GRID_W = 64


def _bd_from_states(s):
    a = jnp.swapaxes(s, -1, -2)
    n, d, h, m, _ = a.shape
    a = a.reshape(n, d, h // 2, 2, m, m)
    z = jnp.zeros_like(a[:, :, :, 0])
    top = jnp.concatenate([a[:, :, :, 0], z], axis=-1)
    bot = jnp.concatenate([z, a[:, :, :, 1]], axis=-1)
    return jnp.concatenate([top, bot], axis=-2)


def _states_from_bd(bd):
    n, d, p, _, _ = bd.shape
    m = RW_HEAD_DIM
    a = jnp.stack([bd[:, :, :, :m, :m], bd[:, :, :, m:, m:]], axis=3).reshape(n, d, 2 * p, m, m)
    return jnp.swapaxes(a, -1, -2)


def _rope_tables(t):
    rows = t // GRID_W
    row, col = jnp.meshgrid(jnp.arange(rows), jnp.arange(GRID_W), indexing='ij')
    inv = ROPE_THETA ** (-jnp.arange(0, ROT_HALF, 2, dtype=F32) / ROT_HALF)
    ang_r = row.reshape(-1, 1).astype(F32) * inv
    ang_c = col.reshape(-1, 1).astype(F32) * inv
    cr, sr, cc, sc = jnp.cos(ang_r), jnp.sin(ang_r), jnp.cos(ang_c), jnp.sin(ang_c)
    cos = jnp.concatenate([cr, cr, cc, cc], axis=-1)
    sin = jnp.concatenate([sr, sr, sc, sc], axis=-1)
    return jnp.tile(cos, (1, LANES // HEAD_DIM)), jnp.tile(sin, (1, LANES // HEAD_DIM))


def _pad_rows(w, top):
    z = jnp.zeros_like(w)
    return jnp.concatenate([w, z] if top else [z, w], axis=0)


def kernel(x_prompt, x_sample, c, cache_k, cache_v, state_wkv, c_ctx, w_mod, b_mod, w_in, shift_mu_prev, shift_mu_next, rw_w0, rw_w2, rw_a0, rw_a2, rw_g2, rw_k_k, rw_k_a, rw_r_k, rw_gn_w, rw_gn_b, w_out_rwkv, conv_w, conv_b, conv_ln_w, conv_ln_b, w_out_conv, q_norm, k_norm, w_out_attn, w_o, peer_wq, peer_keys, peer_u, peer_v):
    nb, seq, d = x_prompt.shape
    db, dseq, _ = x_sample.shape
    depth = w_in.shape[0]
    n_ctx = nb * seq
    n_lat = db * dseq
    n_tok = n_ctx + n_lat
    x = jnp.concatenate([x_prompt.reshape(n_ctx, d), x_sample.reshape(n_lat, d)], axis=0)

    cond = jnp.concatenate([c_ctx[None], c, jnp.zeros((MOD_ROWS - 1 - db, d), F32)], axis=0)
    blk_cond = jnp.concatenate([jnp.zeros((n_ctx // ROW_BLOCK,), jnp.int32),
                                1 + jnp.arange(n_lat // ROW_BLOCK, dtype=jnp.int32) // (dseq // ROW_BLOCK)])
    n_blk = n_tok // ROW_BLOCK
    blk_row = jnp.arange(n_blk) * ROW_BLOCK
    blk_pos = jnp.where(blk_row < n_ctx, blk_row % seq, (blk_row - n_ctx) % dseq)
    blk_len = jnp.where(blk_row < n_ctx, seq, dseq)
    starts_seq = (blk_pos == 0)[:, None]
    ends_seq = (blk_pos + ROW_BLOCK == blk_len)[:, None]
    rope = _rope_tables(dseq)
    in_sizes = (RW_COLS, 2 * CONV_WIDTH, ATT_HEADS * HEAD_DIM, 2 * ATT_KV_HEADS * HEAD_DIM, 3 * d)
    past = cache_k.shape[2]

    w_in_bf, u_bf, v_bf = w_in.astype(BF16), peer_u.astype(BF16), peer_v.astype(BF16)
    ctx_k, ctx_v, ctx_s = [], [], []
    for l in range(depth):
        mod = modulation(cond, w_mod[l], b_mod[l][None])
        mod = mod.reshape(MOD_ROWS, 6, d)[blk_cond]
        mod_blocks = jnp.concatenate([mod, jnp.zeros((mod.shape[0], MOD_ROWS - 6, d), F32)], axis=1)

        z_rw, z_cv, z_q, z_kv, z_gate = in_projection(x, mod_blocks, w_in_bf, l, in_sizes)

        zb = z_rw.reshape(n_blk, ROW_BLOCK, RW_COLS)
        halo_prev = jnp.where(starts_seq, 0.0, jnp.roll(zb[:, -1], 1, axis=0))[:, None]
        halo_next = jnp.where(ends_seq, 0.0, jnp.roll(zb[:, 0], -1, axis=0))[:, None]
        mu = jnp.stack([shift_mu_prev[l], shift_mu_next[l]])
        vecs = jnp.stack([rw_k_k[l], rw_k_a[l], rw_r_k[l], rw_w0[l, 0], rw_w0[l, 1], rw_a0[l, 0],
                          rw_a0[l, 1], jnp.zeros((RW_WIDTH,), F32)])
        w2_pad = jnp.stack([_pad_rows(rw_w2[l, 0], True), _pad_rows(rw_w2[l, 1], False)]).astype(BF16)
        a2_pad = jnp.stack([_pad_rows(rw_a2[l, 0], True), _pad_rows(rw_a2[l, 1], False)]).astype(BF16)
        r, v, kk, lw, aa, kd, bonus, gate = rwkv_prep(z_rw, halo_prev, halo_next, mu, vecs, w2_pad, a2_pad,
                                                      rw_g2[l].astype(BF16))
        s0_ctx = jnp.zeros((nb, 2, RW_HEADS // 2, LANES, LANES), F32)
        of_c, ob_c, sfin = rwkv_scan(r, v, kk, lw, aa, kd, s0_ctx, 0, nb, seq)
        of_l, ob_l, _ = rwkv_scan(r, v, kk, lw, aa, kd, _bd_from_states(state_wkv[:, l]), n_ctx, db, dseq)
        ctx_s.append(_states_from_bd(sfin))

        cvec = jnp.stack([conv_b[l], conv_ln_w[l], conv_ln_b[l]] + [jnp.zeros((CONV_WIDTH,), F32)] * 5)
        y_cv = (conv_module(z_cv, conv_w[l], cvec, 0, nb, seq),
                conv_module(z_cv, conv_w[l], cvec, n_ctx, db, dseq))

        nq = jnp.tile(q_norm[l], LANES // HEAD_DIM)[None]
        nk = jnp.tile(k_norm[l], LANES // HEAD_DIM)[None]
        y_c, k_ctx = attention(z_q, z_kv, nq, nk, 0, nb, seq)
        y_l = attention(z_q, z_kv, nq, nk, n_ctx, db, dseq, rope=rope,
                        cache=(cache_k[:, l].reshape(db * past, LANES), cache_v[:, l].reshape(db * past, LANES)))
        ctx_k.append(k_ctx.reshape(nb, seq, ATT_KV_HEADS, HEAD_DIM))
        ctx_v.append(z_kv[:n_ctx, LANES:].reshape(nb, seq, ATT_KV_HEADS, HEAD_DIM))

        x, h2, pq = merge_project(x, mod_blocks, bonus, gate, z_gate,
                                  (of_c, of_l), (ob_c, ob_l), y_cv, (y_c, y_l),
                                  jnp.stack([rw_gn_w[l], rw_gn_b[l]]),
                                  w_out_rwkv[l].astype(BF16), w_out_conv[l].astype(BF16),
                                  w_out_attn[l].astype(BF16), w_o[l].astype(BF16), peer_wq[l].astype(BF16))

        kz = jnp.zeros_like(peer_keys[l][:, 0])
        keys_padded = jnp.concatenate([jnp.concatenate([peer_keys[l][:, 0], kz], axis=-1),
                                       jnp.concatenate([kz, peer_keys[l][:, 1]], axis=-1)], axis=1)
        w_sel = peer_select(pq, keys_padded)
        x = peer_dense(h2, w_sel, u_bf, v_bf, l, x, mod_blocks)

    new_cache_k = jnp.stack(ctx_k, axis=1)
    new_cache_v = jnp.stack(ctx_v, axis=1)
    new_state = jnp.stack(ctx_s, axis=1)
    return (x[:n_ctx].reshape(nb, seq, d), x[n_ctx:].reshape(db, dseq, d),
            new_cache_k, new_cache_v, new_state)
```

```python
import functools
import math

import jax
import jax.numpy as jnp
from jax import lax
from jax.experimental import pallas as pl
from jax.experimental.pallas import tpu as pltpu

F32 = jnp.float32
BF16 = jnp.bfloat16

LANES = 128
SUBLANES = 8
SCAN_CHUNK = 64
SCAN_SUB = 16
RW_HEAD_DIM = 64
HEADS_PER_PAIR = LANES // RW_HEAD_DIM
VMEM_LIMIT = 56 * 1024 * 1024


def _dot(a, b):
    return jnp.dot(a.astype(BF16), b.astype(BF16), preferred_element_type=F32)


def _dot_nt(a, b):
    return lax.dot_general(a.astype(BF16), b.astype(BF16), (((1,), (1,)), ((), ())),
                           preferred_element_type=F32)


def _dot_tn(a, b):
    return lax.dot_general(a.astype(BF16), b.astype(BF16), (((0,), (0,)), ((), ())),
                           preferred_element_type=F32)


def _dot3(a, b):
    a_hi = a.astype(BF16)
    b_hi = b.astype(BF16)
    a_lo = (a - a_hi.astype(F32)).astype(BF16)
    b_lo = (b - b_hi.astype(F32)).astype(BF16)
    d = functools.partial(jnp.dot, preferred_element_type=F32)
    return d(a_hi, b_hi) + (d(a_hi, b_lo) + d(a_lo, b_hi))


def _scan_masks(reverse):
    C = SCAN_CHUNK
    P = HEADS_PER_PAIR * C
    row = lax.broadcasted_iota(jnp.int32, (P, LANES), 0)
    col = lax.broadcasted_iota(jnp.int32, (P, LANES), 1)
    tr = row % C
    tc = col % C
    t_i = lax.broadcasted_iota(jnp.int32, (C, C), 0)
    s_i = lax.broadcasted_iota(jnp.int32, (C, C), 1)
    return dict(
        same_head=(row // C) == (col // RW_HEAD_DIM),
        incl=(tr >= tc) if not reverse else (tr <= tc),
        strict=(tr > tc) if not reverse else (tr < tc),
        diag_blk=(tr // SCAN_SUB) == (tc // SCAN_SUB),
        eye=row == col,
        cum_mat=jnp.where((t_i >= s_i) if not reverse else (t_i <= s_i), 1.0, 0.0).astype(F32))


def _scan_chunk(chains, states):
    C = SCAN_CHUNK
    n = range(len(chains))
    mk = [ch[6] for ch in chains]
    cum = [ch[8] for ch in chains]
    pre = []
    for i in n:
        r, lw, k, v, kk, a, m, reverse, _ = chains[i]
        g = jnp.exp(cum[i])
        g_prev = jnp.exp(cum[i] - lw)
        g_inv = jnp.exp(-cum[i])
        g_end = jnp.exp(cum[i][C - 1:C, :] if not reverse else cum[i][0:1, :])
        bt = kk * a * g_inv
        kt = k * g_inv
        stack = lambda x, m=m: jnp.where(m["same_head"], jnp.concatenate([x, x], axis=0), 0.0)
        pre.append(dict(kap=stack(kk * g_prev), rt=stack(r * g), bt=stack(bt), kt=stack(kt),
                        v=stack(v), bh=stack(bt * g_end), kh=stack(kt * g_end), g_end=g_end))
    P = HEADS_PER_PAIR * C
    tri = [_dot_nt(jnp.concatenate([pre[i]["kap"], pre[i]["rt"]], axis=0),
                   jnp.concatenate([pre[i]["bt"], pre[i]["kt"]], axis=0)) for i in n]
    lb = [jnp.where(mk[i]["strict"], tri[i][:P, :P], 0.0) for i in n]
    lk = [jnp.where(mk[i]["strict"], tri[i][:P, P:], 0.0) for i in n]
    pb = [jnp.where(mk[i]["incl"], tri[i][P:, :P], 0.0) for i in n]
    pk = [jnp.where(mk[i]["incl"], tri[i][P:, P:], 0.0) for i in n]
    lkv = [_dot(lk[i], pre[i]["v"]) for i in n]

    d = [jnp.where(mk[i]["diag_blk"], lb[i], 0.0) for i in n]
    inv = [jnp.where(mk[i]["eye"], 1.0, 0.0) - d[i] for i in n]
    p = d
    for _ in range(3):
        p = [_dot(p[i], p[i]) for i in n]
        inv = [inv[i] + _dot(inv[i], p[i]) for i in n]
    x = [_dot(inv[i], jnp.concatenate([lb[i] - d[i], pre[i]["kap"], lkv[i]], axis=1)) for i in n]
    e = [x[i][:, :LANES] for i in n]
    rhs = [x[i][:, LANES:] for i in n]
    e2 = [_dot(e[i], e[i]) for i in n]
    t = [rhs[i] + _dot(e2[i], rhs[i]) for i in n]
    gy = [t[i] - _dot(e[i], t[i]) for i in n]

    bhgy = [_dot_tn(pre[i]["bh"], gy[i]) for i in n]
    m_mat = [jnp.where(mk[i]["eye"], pre[i]["g_end"], 0.0) - bhgy[i][:, :LANES] for i in n]
    n_mat = [_dot_tn(pre[i]["kh"], pre[i]["v"]) - bhgy[i][:, LANES:] for i in n]
    pbgy = [_dot(pb[i], gy[i]) for i in n]
    q = [pre[i]["rt"] - pbgy[i][:, :LANES] for i in n]
    z = [_dot(pk[i], pre[i]["v"]) - pbgy[i][:, LANES:] for i in n]
    o_st = [_dot(q[i], states[i]) + z[i] for i in n]
    new = [_dot3(m_mat[i], states[i]) + n_mat[i] for i in n]
    return [(o_st[i][:C] + o_st[i][C:], new[i]) for i in n]


def _scan_kernel(rf, vf, kkf, lwf, af, kf, rb, vb, kkb, lwb, ab, kb, s0, of, ob, sfin, state):
    c = pl.program_id(1)
    n_pairs = rf.shape[-1] // LANES

    @pl.when(c == 0)
    def _():
        state[...] = s0[0]

    chains, states, outs = [], [], []
    for d, (r_, v_, kk_, lw_, a_, k_, o_) in enumerate(((rf, vf, kkf, lwf, af, kf, of),
                                                        (rb, vb, kkb, lwb, ab, kb, ob))):
        masks = _scan_masks(reverse=(d == 1))
        cum = _dot3(masks["cum_mat"], lw_[...])
        for p in range(n_pairs):
            sl = slice(p * LANES, (p + 1) * LANES)
            chains.append((r_[:, sl], lw_[:, sl], k_[:, sl], v_[:, sl], kk_[:, sl], a_[:, sl],
                           masks, d == 1, cum[:, sl]))
            states.append(state[d, p])
            outs.append((o_, sl, d, p))
    for (o, new), (o_, sl, d, p) in zip(_scan_chunk(chains, states), outs):
        o_[:, sl] = o
        state[d, p] = new

    @pl.when(c == pl.num_programs(1) - 1)
    def _():
        sfin[0] = state[...]


def rwkv_scan(r, v, kk, lw, a, kd, s0_bd, row0, n_seq, seq_len):
    width = r.shape[1]
    n_tok = n_seq * seq_len
    n_pairs = width // LANES
    n_chunk = seq_len // SCAN_CHUNK
    c0 = row0 // SCAN_CHUNK
    blk = (SCAN_CHUNK, width)
    dblk = (None, SCAN_CHUNK, width)
    fwd = lambda s, c: (s * n_chunk + c, 0)
    bwd = lambda s, c: (s * n_chunk + n_chunk - 1 - c, 0)
    fwd_in = lambda s, c: (c0 + s * n_chunk + c, 0)
    bwd_in = lambda s, c: (c0 + s * n_chunk + n_chunk - 1 - c, 0)
    fwd_d = lambda s, c: (0, c0 + s * n_chunk + c, 0)
    bwd_d = lambda s, c: (1, c0 + s * n_chunk + n_chunk - 1 - c, 0)
    sspec = pl.BlockSpec((1, 2, n_pairs, LANES, LANES), lambda s, c: (s, 0, 0, 0, 0))
    in_specs = ([pl.BlockSpec(blk, fwd_in)] * 3 + [pl.BlockSpec(dblk, fwd_d)] * 3
                + [pl.BlockSpec(blk, bwd_in)] * 3 + [pl.BlockSpec(dblk, bwd_d)] * 3 + [sspec])
    out_specs = [pl.BlockSpec(blk, fwd), pl.BlockSpec(blk, bwd), sspec]
    out_shape = [jax.ShapeDtypeStruct((n_tok, width), F32)] * 2 + [
        jax.ShapeDtypeStruct(s0_bd.shape, F32)]
    return pl.pallas_call(
        _scan_kernel,
        grid=(n_seq, n_chunk),
        in_specs=in_specs,
        out_specs=out_specs,
        out_shape=out_shape,
        scratch_shapes=[pltpu.VMEM((2, n_pairs, LANES, LANES), F32)],
        compiler_params=pltpu.CompilerParams(
            dimension_semantics=("parallel", "arbitrary"), vmem_limit_bytes=VMEM_LIMIT),
        name="rwkv_scan",
    )(r, v, kk, lw, a, kd, r, v, kk, lw, a, kd, s0_bd)


MOD_ROWS = 8
ROW_BLOCK = 256
RMS_EPS = 1e-6
LN_EPS = 1e-5
RW_GN_EPS = 64e-5


def _mod_kernel(c_ref, w_ref, b_ref, o_ref):
    c = c_ref[...]
    s = c * jax.nn.sigmoid(c)
    o_ref[...] = _dot(s, w_ref[...]) + b_ref[...]


def modulation(cond, w_mod, b_mod):
    n, d = cond.shape
    cols = w_mod.shape[1]
    blk = 1024
    return pl.pallas_call(
        _mod_kernel,
        grid=(cols // blk,),
        in_specs=[pl.BlockSpec((n, d), lambda j: (0, 0)),
                  pl.BlockSpec((d, blk), lambda j: (0, j)),
                  pl.BlockSpec((1, blk), lambda j: (0, j))],
        out_specs=pl.BlockSpec((n, blk), lambda j: (0, j)),
        out_shape=jax.ShapeDtypeStruct((n, cols), F32),
        name="modulation",
    )(cond, w_mod, b_mod)


def _modulate(x, shift, scale):
    ms = jnp.mean(x * x, axis=-1, keepdims=True)
    return x * lax.rsqrt(ms + RMS_EPS) * (1.0 + scale) + shift


def _inproj_kernel(x_ref, mod_ref, w_ref, *o_refs):
    h = _modulate(x_ref[...], mod_ref[0, 0:1, :], mod_ref[0, 1:2, :]).astype(BF16)
    start = 0
    for o_ref in o_refs:
        n = o_ref.shape[1]
        o_ref[...] = jnp.dot(h, w_ref[:, start:start + n], preferred_element_type=F32)
        start += n


def in_projection(x, mod_blocks, w_in, layer, sizes):
    n_tok, d = x.shape
    row = lambda i: (i, 0)
    return pl.pallas_call(
        _inproj_kernel,
        grid=(n_tok // ROW_BLOCK,),
        in_specs=[pl.BlockSpec((ROW_BLOCK, d), row),
                  pl.BlockSpec((1, MOD_ROWS, d), lambda i: (i, 0, 0)),
                  pl.BlockSpec((None, d, w_in.shape[2]), lambda i: (layer, 0, 0),
                               pipeline_mode=pl.Buffered(1))],
        out_specs=[pl.BlockSpec((ROW_BLOCK, n), row) for n in sizes],
        out_shape=[jax.ShapeDtypeStruct((n_tok, n), F32) for n in sizes],
        compiler_params=pltpu.CompilerParams(
            dimension_semantics=("parallel",), vmem_limit_bytes=VMEM_LIMIT),
        name="in_projection",
    )(x, mod_blocks, w_in)


HEAD_DIM = 64


def _group_ones():
    r = lax.broadcasted_iota(jnp.int32, (LANES, LANES), 0) // HEAD_DIM
    c = lax.broadcasted_iota(jnp.int32, (LANES, LANES), 1) // HEAD_DIM
    return jnp.where(r == c, 1.0, 0.0).astype(BF16)


def _group_sum(x, ones):
    hi = x.astype(BF16)
    lo = (x - hi.astype(F32)).astype(BF16)
    d = functools.partial(jnp.dot, preferred_element_type=F32)
    tiles = [slice(s, s + LANES) for s in range(0, x.shape[1], LANES)]
    sums = [d(hi[:, t], ones) + d(lo[:, t], ones) for t in tiles]
    return sums[0] if len(sums) == 1 else jnp.concatenate(sums, axis=1)


RW_WIDTH = 512
RW_COLS = 3 * RW_WIDTH + 3 * LANES
RW_VEC_ROWS = 8


def _rwkv_prep_kernel(z_ref, hp_ref, hn_ref, mu_ref, vec_ref, w2_ref, a2_ref, g2_ref,
                      r_ref, v_ref, kk_ref, lw_ref, a_ref, kd_ref, bonus_ref, gate_ref):
    z = z_ref[...]
    rb = z.shape[0]
    row = lax.broadcasted_iota(jnp.int32, (rb, 1), 0)
    z_prev = jnp.where(row == 0, hp_ref[0], pltpu.roll(z, 1, 0))
    z_next = jnp.where(row == rb - 1, hn_ref[0], pltpu.roll(z, rb - 1, 0))
    zs = z + mu_ref[0:1, :] * (z_prev - z) + mu_ref[1:2, :] * (z_next - z)
    w = RW_WIDTH
    r, k, v = zs[:, :w], zs[:, w:2 * w], zs[:, 2 * w:3 * w]
    zw = zs[:, 3 * w:3 * w + LANES]
    za = zs[:, 3 * w + LANES:3 * w + 2 * LANES]
    zg = zs[:, 3 * w + 2 * LANES:]
    ones = _group_ones()
    kk = k * vec_ref[0:1, :]
    kk = kk * lax.rsqrt(_group_sum(kk * kk, ones) + 1e-12)
    r_ref[...] = r
    v_ref[...] = v
    kk_ref[...] = kk
    tz = jnp.tanh(zw)
    for d in range(2):
        w_raw = vec_ref[3 + d:4 + d, :] + _dot(tz, w2_ref[d])
        lw_ref[d] = -math.exp(-0.5) * jax.nn.sigmoid(w_raw)
        a = jax.nn.sigmoid(vec_ref[5 + d:6 + d, :] + _dot(za, a2_ref[d]))
        a_ref[d] = a
        kd_ref[d] = k * (1.0 + (a - 1.0) * vec_ref[1:2, :])
    bonus_ref[...] = _group_sum(r * k * vec_ref[2:3, :], ones) * v
    gate_ref[...] = _dot(jax.nn.sigmoid(zg), g2_ref[...])


def rwkv_prep(z_rw, halo_prev, halo_next, mu, vecs, w2_pad, a2_pad, g2):
    n_tok = z_rw.shape[0]
    w = RW_WIDTH
    row = lambda i: (i, 0)
    const2 = lambda i: (0, 0)
    const3 = lambda i: (0, 0, 0)
    halo = pl.BlockSpec((1, 1, RW_COLS), lambda i: (i, 0, 0))
    one = pl.BlockSpec((ROW_BLOCK, w), row)
    two = pl.BlockSpec((2, ROW_BLOCK, w), lambda i: (0, i, 0))
    s1 = jax.ShapeDtypeStruct((n_tok, w), F32)
    s2 = jax.ShapeDtypeStruct((2, n_tok, w), F32)
    return pl.pallas_call(
        _rwkv_prep_kernel,
        grid=(n_tok // ROW_BLOCK,),
        in_specs=[pl.BlockSpec((ROW_BLOCK, RW_COLS), row), halo, halo,
                  pl.BlockSpec(mu.shape, const2), pl.BlockSpec(vecs.shape, const2),
                  pl.BlockSpec(w2_pad.shape, const3), pl.BlockSpec(a2_pad.shape, const3),
                  pl.BlockSpec(g2.shape, const2)],
        out_specs=[one, one, one, two, two, two, one, one],
        out_shape=[s1, s1, s1, s2, s2, s2, s1, s1],
        compiler_params=pltpu.CompilerParams(
            dimension_semantics=("parallel",), vmem_limit_bytes=VMEM_LIMIT),
        name="rwkv_prep",
    )(z_rw, halo_prev, halo_next, mu, vecs, w2_pad, a2_pad, g2)


CONV_WIDTH = 512
CONV_KERNEL = 31
CONV_PAD = 16
CONV_ROWS = 64


def _conv_kernel(z_ref, w_ref, vec_ref, o_ref, pad_ref):
    t = z_ref.shape[0]
    cw = CONV_WIDTH
    half = CONV_KERNEL // 2
    zeros = jnp.zeros((CONV_PAD, cw), F32)
    pad_ref[0:CONV_PAD, :] = zeros
    pad_ref[CONV_PAD + t:CONV_PAD + t + CONV_PAD, :] = zeros
    pad_ref[CONV_PAD:CONV_PAD + t, :] = z_ref[:, :cw] * jax.nn.sigmoid(z_ref[:, cw:])

    def chunk(i, carry):
        base = pl.multiple_of(i * CONV_ROWS, CONV_ROWS)
        window = pad_ref[pl.ds(base, CONV_ROWS + 2 * CONV_PAD), :]
        span = CONV_ROWS + 2 * CONV_PAD - SUBLANES
        shifted = [window[rho:rho + span] for rho in range(SUBLANES)]
        acc = jnp.zeros((CONV_ROWS, cw), F32)
        for j in range(CONV_KERNEL):
            lo = CONV_PAD - half + j
            tile = (lo // SUBLANES) * SUBLANES
            acc = acc + w_ref[j:j + 1, :] * shifted[lo % SUBLANES][tile:tile + CONV_ROWS]
        u = acc + vec_ref[0:1, :]
        mu = jnp.mean(u, axis=-1, keepdims=True)
        cen = u - mu
        var = jnp.mean(cen * cen, axis=-1, keepdims=True)
        y = cen * lax.rsqrt(var + LN_EPS) * vec_ref[1:2, :] + vec_ref[2:3, :]
        o_ref[pl.ds(base, CONV_ROWS), :] = y * jax.nn.sigmoid(y)
        return carry

    lax.fori_loop(0, t // CONV_ROWS, chunk, 0)


def conv_module(z_cv, conv_w, vecs, row0, n_seq, seq_len):
    cw = CONV_WIDTH
    blk0 = row0 // seq_len
    return pl.pallas_call(
        _conv_kernel,
        grid=(n_seq,),
        in_specs=[pl.BlockSpec((seq_len, 2 * cw), lambda i: (blk0 + i, 0)),
                  pl.BlockSpec(conv_w.shape, lambda i: (0, 0)),
                  pl.BlockSpec(vecs.shape, lambda i: (0, 0))],
        out_specs=pl.BlockSpec((seq_len, cw), lambda i: (i, 0)),
        out_shape=jax.ShapeDtypeStruct((n_seq * seq_len, cw), F32),
        scratch_shapes=[pltpu.VMEM((seq_len + 2 * CONV_PAD, cw), F32)],
        compiler_params=pltpu.CompilerParams(
            dimension_semantics=("parallel",), vmem_limit_bytes=VMEM_LIMIT),
        name="conv_module",
    )(z_cv, conv_w, vecs)


ATT_Q_BLOCK = 256
ATT_HEADS = 8
ATT_KV_HEADS = 2
ATT_GROUP = ATT_HEADS // ATT_KV_HEADS


def _rot_partner():
    r = lax.broadcasted_iota(jnp.int32, (LANES, LANES), 0)
    c = lax.broadcasted_iota(jnp.int32, (LANES, LANES), 1)
    lo = (r % 32) < 16
    return jnp.where((c == r + 16) & lo, 1.0, jnp.where((c == r - 16) & ~lo, -1.0, 0.0)).astype(BF16)


def _attn_kernel(*refs, rotary):
    if rotary:
        (zq_ref, zkv_ref, nq_ref, nk_ref, cosq_ref, sinq_ref, cosk_ref, sink_ref, ck_ref, cv_ref,
         o_ref) = refs
    else:
        zq_ref, zkv_ref, nq_ref, nk_ref, o_ref, kn_ref = refs
    tq = zq_ref.shape[0]
    ones = _group_ones()
    lane = lax.broadcasted_iota(jnp.int32, (1, LANES), 1)
    first = lane < HEAD_DIM
    row_head = lax.broadcasted_iota(jnp.int32, (2 * tq, LANES), 0) // tq
    own = row_head == (lax.broadcasted_iota(jnp.int32, (2 * tq, LANES), 1) // HEAD_DIM)

    def norm(x, g):
        return x * lax.rsqrt(_group_sum(x * x, ones) * (1.0 / HEAD_DIM) + RMS_EPS) * g

    def rope(x, cos, sin):
        return x * cos + jnp.dot(x.astype(BF16), _rot_partner(), preferred_element_type=F32) * sin

    def dup(x):
        sw = pltpu.roll(x, HEAD_DIM, 1)
        return jnp.where(first, x, sw), jnp.where(first, sw, x)

    k = norm(zkv_ref[:, :LANES], nk_ref[...])
    v = zkv_ref[:, LANES:]
    if rotary:
        k = rope(k, cosk_ref[...], sink_ref[...])
        ck, cv = dup(ck_ref[...]), dup(cv_ref[...])
    else:
        kn_ref[...] = k
    kd, vd = dup(k), dup(v)
    for pair in range(ATT_HEADS // 2):
        g = pair // (ATT_GROUP // 2)
        q = norm(zq_ref[:, pair * LANES:(pair + 1) * LANES], nq_ref[...])
        if rotary:
            q = rope(q, cosq_ref[...], sinq_ref[...])
        q = q * (HEAD_DIM ** -0.5)
        qs = jnp.where(own, jnp.concatenate([q, q], axis=0), 0.0)
        s = _dot_nt(qs, kd[g])
        m = jnp.max(s, axis=-1, keepdims=True)
        if rotary:
            s2 = _dot_nt(qs, ck[g])
            m = jnp.maximum(m, jnp.max(s2, axis=-1, keepdims=True))
            p2 = jnp.exp(s2 - m)
        p = jnp.exp(s - m)
        den = jnp.sum(p, axis=-1, keepdims=True)
        o = _dot(p, vd[g])
        if rotary:
            den = den + jnp.sum(p2, axis=-1, keepdims=True)
            o = o + _dot(p2, cv[g])
        o = o / den
        o_ref[:, pair * LANES:(pair + 1) * LANES] = jnp.where(first, o[:tq], o[tq:])


def attention(z_q, z_kv, nq, nk, row0, n_seq, seq_len, rope=None, cache=None):
    qb = min(ATT_Q_BLOCK, seq_len)
    nqb = seq_len // qb
    qblk0 = row0 // qb
    sblk0 = row0 // seq_len
    wq = z_q.shape[1]
    in_specs = [pl.BlockSpec((qb, wq), lambda b, t: (qblk0 + b * nqb + t, 0)),
                pl.BlockSpec((seq_len, 2 * LANES), lambda b, t: (sblk0 + b, 0)),
                pl.BlockSpec((1, LANES), lambda b, t: (0, 0)),
                pl.BlockSpec((1, LANES), lambda b, t: (0, 0))]
    args = [z_q, z_kv, nq, nk]
    out_q = pl.BlockSpec((qb, wq), lambda b, t: (b * nqb + t, 0))
    y_shape = jax.ShapeDtypeStruct((n_seq * seq_len, wq), F32)
    if rope is None:
        out_specs = [out_q, pl.BlockSpec((seq_len, LANES), lambda b, t: (b, 0))]
        out_shape = [y_shape, jax.ShapeDtypeStruct((n_seq * seq_len, LANES), F32)]
    else:
        past = cache[0].shape[0] // n_seq
        in_specs += [pl.BlockSpec((qb, LANES), lambda b, t: (t, 0))] * 2
        in_specs += [pl.BlockSpec((seq_len, LANES), lambda b, t: (0, 0))] * 2
        in_specs += [pl.BlockSpec((past, LANES), lambda b, t: (b, 0))] * 2
        args += [rope[0], rope[1], rope[0], rope[1], cache[0], cache[1]]
        out_specs = out_q
        out_shape = y_shape
    return pl.pallas_call(
        functools.partial(_attn_kernel, rotary=rope is not None),
        grid=(n_seq, nqb),
        in_specs=in_specs,
        out_specs=out_specs,
        out_shape=out_shape,
        compiler_params=pltpu.CompilerParams(
            dimension_semantics=("parallel", "arbitrary"), vmem_limit_bytes=VMEM_LIMIT),
        name="attention",
    )(*args)


def _merge_kernel(x_ref, mod_ref, bonus_ref, gate_ref, zg_ref,
                  ofc_ref, ofl_ref, obc_ref, obl_ref, cvc_ref, cvl_ref, atc_ref, atl_ref,
                  gn_ref, wr_ref, wc_ref, wa_ref, wo_ref, wq_ref, xo_ref, h2_ref, q_ref, *, n_ctx_blocks):
    d = x_ref.shape[1]
    zg = zg_ref[...]
    is_ctx = pl.program_id(0) < n_ctx_blocks
    pick = lambda c_ref, l_ref: jnp.where(is_ctx, c_ref[...], l_ref[...])
    ones = _group_ones()
    o = pick(ofc_ref, ofl_ref) + pick(obc_ref, obl_ref)
    cen = o - _group_sum(o, ones) * (1.0 / HEAD_DIM)
    var = _group_sum(cen * cen, ones) * (1.0 / HEAD_DIM)
    o = cen * lax.rsqrt(var + RW_GN_EPS) * gn_ref[0:1, :] + gn_ref[1:2, :]
    y_rw = (o + bonus_ref[...]) * gate_ref[...]
    merged = (jax.nn.sigmoid(zg[:, :d]) * _dot(y_rw, wr_ref[...])
              + jax.nn.sigmoid(zg[:, d:2 * d]) * _dot(pick(cvc_ref, cvl_ref), wc_ref[...])
              + jax.nn.sigmoid(zg[:, 2 * d:]) * _dot(pick(atc_ref, atl_ref), wa_ref[...]))
    x = x_ref[...] + mod_ref[0, 2:3, :] * _dot(merged, wo_ref[...])
    xo_ref[...] = x
    h2 = _modulate(x, mod_ref[0, 3:4, :], mod_ref[0, 4:5, :]).astype(BF16)
    h2_ref[...] = h2
    q_ref[...] = jnp.dot(h2, wq_ref[...], preferred_element_type=F32)


def merge_project(x, mod_blocks, bonus, gate, z_gate, o_f, o_b, y_cv, y_at, gn, w_r, w_c, w_a, w_o, w_q):
    n_tok, d = x.shape
    nc = o_f[0].shape[0] // ROW_BLOCK
    row = lambda i: (i, 0)
    const = lambda i: (0, 0)
    ctx_row = lambda i: (jnp.minimum(i, nc - 1), 0)
    lat_row = lambda i: (jnp.maximum(i - nc, 0), 0)
    acts = [bonus, gate, z_gate]
    weights = [gn, w_r, w_c, w_a, w_o, w_q]
    pair_specs, pair_args = [], []
    for c_arr, l_arr in (o_f, o_b, y_cv, y_at):
        pair_specs += [pl.BlockSpec((ROW_BLOCK, c_arr.shape[1]), ctx_row),
                       pl.BlockSpec((ROW_BLOCK, l_arr.shape[1]), lat_row)]
        pair_args += [c_arr, l_arr]
    return pl.pallas_call(
        functools.partial(_merge_kernel, n_ctx_blocks=nc),
        grid=(n_tok // ROW_BLOCK,),
        in_specs=[pl.BlockSpec((ROW_BLOCK, d), row),
                  pl.BlockSpec((1, MOD_ROWS, d), lambda i: (i, 0, 0))]
                 + [pl.BlockSpec((ROW_BLOCK, a.shape[1]), row) for a in acts]
                 + pair_specs
                 + [pl.BlockSpec(w.shape, const) for w in weights],
        out_specs=[pl.BlockSpec((ROW_BLOCK, d), row), pl.BlockSpec((ROW_BLOCK, d), row),
                   pl.BlockSpec((ROW_BLOCK, w_q.shape[1]), row)],
        out_shape=[jax.ShapeDtypeStruct((n_tok, d), F32), jax.ShapeDtypeStruct((n_tok, d), BF16),
                   jax.ShapeDtypeStruct((n_tok, w_q.shape[1]), F32)],
        compiler_params=pltpu.CompilerParams(
            dimension_semantics=("parallel",), vmem_limit_bytes=VMEM_LIMIT),
        name="merge_project",
    )(x, mod_blocks, *acts, *pair_args, *weights)


PEER_HEADS = 8
PEER_N_KEYS = 128
PEER_HALF = 64
PEER_TOPK = 16
PEER_SEL_BLOCK = 128
PEER_LOOP_UNROLL = 32
PEER_RELAYOUT = 16
NEG = -1e30


def _dot3_nt(a, b):
    a_hi = a.astype(BF16)
    b_hi = b.astype(BF16)
    a_lo = (a - a_hi.astype(F32)).astype(BF16)
    b_lo = (b - b_hi.astype(F32)).astype(BF16)
    d = lambda x, y: lax.dot_general(x, y, (((1,), (1,)), ((), ())), preferred_element_type=F32)
    return d(a_hi, b_hi) + (d(a_hi, b_lo) + d(a_lo, b_hi))


def _extract_max(s, code):
    vals = [s[i:i + SUBLANES] for i in range(0, s.shape[0], SUBLANES)]
    codes = [code[i:i + SUBLANES] for i in range(0, s.shape[0], SUBLANES)]
    while len(vals) > 1:
        nv, nc = [], []
        for i in range(0, len(vals) - 1, 2):
            keep = vals[i] >= vals[i + 1]
            nv.append(jnp.maximum(vals[i], vals[i + 1]))
            nc.append(jnp.where(keep, codes[i], codes[i + 1]))
        if len(vals) % 2:
            nv.append(vals[-1])
            nc.append(codes[-1])
        vals, codes = nv, nc
    m = jnp.max(vals[0], axis=0, keepdims=True)
    pos = jnp.min(jnp.where(vals[0] == m, codes[0], 1e9), axis=0, keepdims=True)
    return m, pos, jnp.where(code == pos, NEG, s)


def _pair_candidates(a0, a1):
    k = PEER_TOPK
    sub = lax.broadcasted_iota(jnp.int32, (k, a0.shape[1]), 0).astype(F32)
    sub8 = sub[:8]
    vals = [a0[0:1] + a1]
    codes = [sub]
    for r0 in range(1, 8):
        limit = k // (r0 + 1)
        vals.append(jnp.where(sub8 < limit, a0[r0:r0 + 1] + a1[:8], NEG))
        codes.append(sub8 + float(r0 * k))
    vals.append(a0[8:] + a1[0:1])
    codes.append((sub8 + 8.0) * float(k))
    return jnp.concatenate(vals, axis=0), jnp.concatenate(codes, axis=0)


def _peer_select_kernel(q_ref, keys_ref, w_ref, top_v, top_i, slot_i, slot_j, slot_g,
                        rows_i, rows_j, rows_g, w3_ref):
    k = PEER_TOPK
    ct = q_ref.shape[0]
    key_code = lax.broadcasted_iota(jnp.int32, (PEER_N_KEYS, ct), 0).astype(F32)
    rank = lax.broadcasted_iota(jnp.int32, (k, ct), 0).astype(F32)
    for h in range(PEER_HEADS):
        st = _dot3_nt(keys_ref[h], q_ref[:, h * LANES:(h + 1) * LANES])
        s = [st[:PEER_N_KEYS], st[PEER_N_KEYS:]]
        for it in range(k):
            for p in range(2):
                m, pos, s[p] = _extract_max(s[p], key_code)
                top_v[h, p, it:it + 1, :] = m
                top_i[h, p, it:it + 1, :] = pos
    for h0 in range(0, PEER_HEADS, 2):
        heads = (h0, h0 + 1)
        cand, code, best, z = {}, {}, {}, {}
        for h in heads:
            a0, a1 = top_v[h, 0], top_v[h, 1]
            cand[h], code[h] = _pair_candidates(a0, a1)
            best[h] = a0[0:1] + a1[0:1]
            z[h] = jnp.zeros_like(best[h])
        for it in range(k):
            for h in heads:
                m, pos, cand[h] = _extract_max(cand[h], code[h])
                r0 = jnp.floor(pos * (1.0 / k))
                r1 = pos - r0 * k
                e = jnp.exp(m - best[h])
                z[h] = z[h] + e
                slot = h * k + it
                slot_i[slot:slot + 1, :] = jnp.sum(jnp.where(rank == r0, top_i[h, 0], 0.0), axis=0,
                                                   keepdims=True)
                slot_j[slot:slot + 1, :] = jnp.sum(jnp.where(rank == r1, top_i[h, 1], 0.0), axis=0,
                                                   keepdims=True)
                slot_g[slot:slot + 1, :] = e
        for h in heads:
            slot_g[h * k:(h + 1) * k, :] = slot_g[h * k:(h + 1) * k, :] / z[h]

    rows_i[...] = slot_i[...].T
    rows_j[...] = slot_j[...].T
    rows_g[...] = slot_g[...].T
    n = PEER_N_KEYS
    sub = lax.broadcasted_iota(jnp.int32, (n, n), 0).astype(F32)

    def per_token(c, carry):
        a_t = jnp.where(sub == rows_i[pl.ds(c, 1), :], rows_g[pl.ds(c, 1), :], 0.0)
        b_t = jnp.where(sub == rows_j[pl.ds(c, 1), :], 1.0, 0.0)
        w3_ref[c] = _dot_nt(a_t, b_t).astype(BF16)
        return carry

    lax.fori_loop(0, ct, per_token, 0, unroll=PEER_LOOP_UNROLL)

    g = PEER_RELAYOUT
    r_out = lax.broadcasted_iota(jnp.int32, (g * g, g * g), 0)
    r_in = lax.broadcasted_iota(jnp.int32, (g * g, g * g), 1)
    perm = jnp.where(r_in == (r_out % g) * g + r_out // g, 1.0, 0.0).astype(BF16)

    for ib in range(n // g):
        for t in range(ct // g):
            z = w3_ref[t * g:(t + 1) * g, ib * g:(ib + 1) * g, :].reshape(g * g, n)
            y = jnp.dot(perm, z, preferred_element_type=F32)
            for kk in range(g):
                col = (ib * g + kk) * n
                w_ref[t * g:(t + 1) * g, col:col + n] = y[kk * g:(kk + 1) * g].astype(BF16)


def peer_select(q, keys_padded):
    n_tok, qd = q.shape
    ct = PEER_SEL_BLOCK
    n = PEER_N_KEYS
    slots = PEER_HEADS * PEER_TOPK
    return pl.pallas_call(
        _peer_select_kernel,
        grid=(n_tok // ct,),
        in_specs=[pl.BlockSpec((ct, qd), lambda i: (i, 0)),
                  pl.BlockSpec(keys_padded.shape, lambda i: (0, 0, 0))],
        out_specs=pl.BlockSpec((ct, n * n), lambda i: (i, 0)),
        out_shape=jax.ShapeDtypeStruct((n_tok, n * n), BF16),
        scratch_shapes=[pltpu.VMEM((PEER_HEADS, 2, PEER_TOPK, ct), F32),
                        pltpu.VMEM((PEER_HEADS, 2, PEER_TOPK, ct), F32),
                        pltpu.VMEM((slots, ct), F32), pltpu.VMEM((slots, ct), F32),
                        pltpu.VMEM((slots, ct), F32),
                        pltpu.VMEM((ct, slots), F32), pltpu.VMEM((ct, slots), F32),
                        pltpu.VMEM((ct, slots), F32),
                        pltpu.VMEM((ct, n, n), BF16)],
        compiler_params=pltpu.CompilerParams(
            dimension_semantics=("parallel",), vmem_limit_bytes=VMEM_LIMIT),
        name="peer_select",
    )(q, keys_padded)


PEER_TOKEN_BLOCK = 1024
PEER_EXPERT_BLOCK = 1024


def _peer_kernel(h_ref, w_ref, u_ref, v_ref, x_ref, mod_ref, o_ref, acc_ref):
    e = pl.program_id(1)

    @pl.when(e == 0)
    def _():
        acc_ref[...] = jnp.zeros_like(acc_ref)

    s = _dot_nt(h_ref[...], u_ref[...])
    act = 0.5 * s * (1.0 + lax.erf(s * (2.0 ** -0.5)))
    acc_ref[...] += jnp.dot(act.astype(BF16) * w_ref[...], v_ref[...], preferred_element_type=F32)

    @pl.when(e == pl.num_programs(1) - 1)
    def _():
        for j in range(mod_ref.shape[0]):
            rows = slice(j * ROW_BLOCK, (j + 1) * ROW_BLOCK)
            o_ref[rows, :] = x_ref[rows, :] + mod_ref[j, 5:6, :] * acc_ref[rows, :]


def peer_dense(h2, w_sel, u_tabs, v_tabs, layer, x, mod_blocks):
    n_tok, d = x.shape
    n_exp = u_tabs.shape[1]
    tb, eb = PEER_TOKEN_BLOCK, PEER_EXPERT_BLOCK
    tok = lambda i, e: (i, 0)
    return pl.pallas_call(
        _peer_kernel,
        grid=(n_tok // tb, n_exp // eb),
        in_specs=[pl.BlockSpec((tb, d), tok),
                  pl.BlockSpec((tb, eb), lambda i, e: (i, e)),
                  pl.BlockSpec((None, eb, d), lambda i, e: (layer, e, 0)),
                  pl.BlockSpec((None, eb, d), lambda i, e: (layer, e, 0)),
                  pl.BlockSpec((tb, d), tok),
                  pl.BlockSpec((tb // ROW_BLOCK, MOD_ROWS, d), lambda i, e: (i, 0, 0))],
        out_specs=pl.BlockSpec((tb, d), tok),
        out_shape=jax.ShapeDtypeStruct((n_tok, d), F32),
        scratch_shapes=[pltpu.VMEM((tb, d), F32)],
        compiler_params=pltpu.CompilerParams(
            dimension_semantics=("parallel", "arbitrary"), vmem_limit_bytes=VMEM_LIMIT),
        name="peer_dense",
    )(h2, w_sel, u_tabs, v_tabs, x, mod_blocks)


RW_HEADS = RW_WIDTH // RW_HEAD_DIM
ROT_HALF = HEAD_DIM // 2
ROPE_THETA = 10000.0
GRID_W = 64


def _bd_from_states(s):
    a = jnp.swapaxes(s, -1, -2)
    n, d, h, m, _ = a.shape
    a = a.reshape(n, d, h // 2, 2, m, m)
    z = jnp.zeros_like(a[:, :, :, 0])
    top = jnp.concatenate([a[:, :, :, 0], z], axis=-1)
    bot = jnp.concatenate([z, a[:, :, :, 1]], axis=-1)
    return jnp.concatenate([top, bot], axis=-2)


def _states_from_bd(bd):
    n, d, p, _, _ = bd.shape
    m = RW_HEAD_DIM
    a = jnp.stack([bd[:, :, :, :m, :m], bd[:, :, :, m:, m:]], axis=3).reshape(n, d, 2 * p, m, m)
    return jnp.swapaxes(a, -1, -2)


def _rope_tables(t):
    rows = t // GRID_W
    row, col = jnp.meshgrid(jnp.arange(rows), jnp.arange(GRID_W), indexing='ij')
    inv = ROPE_THETA ** (-jnp.arange(0, ROT_HALF, 2, dtype=F32) / ROT_HALF)
    ang_r = row.reshape(-1, 1).astype(F32) * inv
    ang_c = col.reshape(-1, 1).astype(F32) * inv
    cr, sr, cc, sc = jnp.cos(ang_r), jnp.sin(ang_r), jnp.cos(ang_c), jnp.sin(ang_c)
    cos = jnp.concatenate([cr, cr, cc, cc], axis=-1)
    sin = jnp.concatenate([sr, sr, sc, sc], axis=-1)
    return jnp.tile(cos, (1, LANES // HEAD_DIM)), jnp.tile(sin, (1, LANES // HEAD_DIM))


def _pad_rows(w, top):
    z = jnp.zeros_like(w)
    return jnp.concatenate([w, z] if top else [z, w], axis=0)


def kernel(x_prompt, x_sample, c, cache_k, cache_v, state_wkv, c_ctx, w_mod, b_mod, w_in, shift_mu_prev, shift_mu_next, rw_w0, rw_w2, rw_a0, rw_a2, rw_g2, rw_k_k, rw_k_a, rw_r_k, rw_gn_w, rw_gn_b, w_out_rwkv, conv_w, conv_b, conv_ln_w, conv_ln_b, w_out_conv, q_norm, k_norm, w_out_attn, w_o, peer_wq, peer_keys, peer_u, peer_v):
    nb, seq, d = x_prompt.shape
    db, dseq, _ = x_sample.shape
    depth = w_in.shape[0]
    n_ctx = nb * seq
    n_lat = db * dseq
    n_tok = n_ctx + n_lat
    x = jnp.concatenate([x_prompt.reshape(n_ctx, d), x_sample.reshape(n_lat, d)], axis=0)

    cond = jnp.concatenate([c_ctx[None], c, jnp.zeros((MOD_ROWS - 1 - db, d), F32)], axis=0)
    blk_cond = jnp.concatenate([jnp.zeros((n_ctx // ROW_BLOCK,), jnp.int32),
                                1 + jnp.arange(n_lat // ROW_BLOCK, dtype=jnp.int32) // (dseq // ROW_BLOCK)])
    n_blk = n_tok // ROW_BLOCK
    blk_row = jnp.arange(n_blk) * ROW_BLOCK
    blk_pos = jnp.where(blk_row < n_ctx, blk_row % seq, (blk_row - n_ctx) % dseq)
    blk_len = jnp.where(blk_row < n_ctx, seq, dseq)
    starts_seq = (blk_pos == 0)[:, None]
    ends_seq = (blk_pos + ROW_BLOCK == blk_len)[:, None]
    rope = _rope_tables(dseq)
    in_sizes = (RW_COLS, 2 * CONV_WIDTH, ATT_HEADS * HEAD_DIM, 2 * ATT_KV_HEADS * HEAD_DIM, 3 * d)
    past = cache_k.shape[2]

    w_in_bf, u_bf, v_bf = w_in.astype(BF16), peer_u.astype(BF16), peer_v.astype(BF16)
    ctx_k, ctx_v, ctx_s = [], [], []
    for l in range(depth):
        mod = modulation(cond, w_mod[l], b_mod[l][None])
        mod = mod.reshape(MOD_ROWS, 6, d)[blk_cond]
        mod_blocks = jnp.concatenate([mod, jnp.zeros((mod.shape[0], MOD_ROWS - 6, d), F32)], axis=1)

        z_rw, z_cv, z_q, z_kv, z_gate = in_projection(x, mod_blocks, w_in_bf, l, in_sizes)

        zb = z_rw.reshape(n_blk, ROW_BLOCK, RW_COLS)
        halo_prev = jnp.where(starts_seq, 0.0, jnp.roll(zb[:, -1], 1, axis=0))[:, None]
        halo_next = jnp.where(ends_seq, 0.0, jnp.roll(zb[:, 0], -1, axis=0))[:, None]
        mu = jnp.stack([shift_mu_prev[l], shift_mu_next[l]])
        vecs = jnp.stack([rw_k_k[l], rw_k_a[l], rw_r_k[l], rw_w0[l, 0], rw_w0[l, 1], rw_a0[l, 0],
                          rw_a0[l, 1], jnp.zeros((RW_WIDTH,), F32)])
        w2_pad = jnp.stack([_pad_rows(rw_w2[l, 0], True), _pad_rows(rw_w2[l, 1], False)]).astype(BF16)
        a2_pad = jnp.stack([_pad_rows(rw_a2[l, 0], True), _pad_rows(rw_a2[l, 1], False)]).astype(BF16)
        r, v, kk, lw, aa, kd, bonus, gate = rwkv_prep(z_rw, halo_prev, halo_next, mu, vecs, w2_pad, a2_pad,
                                                      rw_g2[l].astype(BF16))
        s0_ctx = jnp.zeros((nb, 2, RW_HEADS // 2, LANES, LANES), F32)
        of_c, ob_c, sfin = rwkv_scan(r, v, kk, lw, aa, kd, s0_ctx, 0, nb, seq)
        of_l, ob_l, _ = rwkv_scan(r, v, kk, lw, aa, kd, _bd_from_states(state_wkv[:, l]), n_ctx, db, dseq)
        ctx_s.append(_states_from_bd(sfin))

        cvec = jnp.stack([conv_b[l], conv_ln_w[l], conv_ln_b[l]] + [jnp.zeros((CONV_WIDTH,), F32)] * 5)
        y_cv = (conv_module(z_cv, conv_w[l], cvec, 0, nb, seq),
                conv_module(z_cv, conv_w[l], cvec, n_ctx, db, dseq))

        nq = jnp.tile(q_norm[l], LANES // HEAD_DIM)[None]
        nk = jnp.tile(k_norm[l], LANES // HEAD_DIM)[None]
        y_c, k_ctx = attention(z_q, z_kv, nq, nk, 0, nb, seq)
        y_l = attention(z_q, z_kv, nq, nk, n_ctx, db, dseq, rope=rope,
                        cache=(cache_k[:, l].reshape(db * past, LANES), cache_v[:, l].reshape(db * past, LANES)))
        ctx_k.append(k_ctx.reshape(nb, seq, ATT_KV_HEADS, HEAD_DIM))
        ctx_v.append(z_kv[:n_ctx, LANES:].reshape(nb, seq, ATT_KV_HEADS, HEAD_DIM))

        x, h2, pq = merge_project(x, mod_blocks, bonus, gate, z_gate,
                                  (of_c, of_l), (ob_c, ob_l), y_cv, (y_c, y_l),
                                  jnp.stack([rw_gn_w[l], rw_gn_b[l]]),
                                  w_out_rwkv[l].astype(BF16), w_out_conv[l].astype(BF16),
                                  w_out_attn[l].astype(BF16), w_o[l].astype(BF16), peer_wq[l].astype(BF16))

        kz = jnp.zeros_like(peer_keys[l][:, 0])
        keys_padded = jnp.concatenate([jnp.concatenate([peer_keys[l][:, 0], kz], axis=-1),
                                       jnp.concatenate([kz, peer_keys[l][:, 1]], axis=-1)], axis=1)
        w_sel = peer_select(pq, keys_padded)
        x = peer_dense(h2, w_sel, u_bf, v_bf, l, x, mod_blocks)

    new_cache_k = jnp.stack(ctx_k, axis=1)
    new_cache_v = jnp.stack(ctx_v, axis=1)
    new_state = jnp.stack(ctx_s, axis=1)
    return (x[:n_ctx].reshape(nb, seq, d), x[n_ctx:].reshape(db, dseq, d),
            new_cache_k, new_cache_v, new_state)
```

```python
import functools
import math

import jax
import jax.numpy as jnp
from jax import lax
from jax.experimental import pallas as pl
from jax.experimental.pallas import tpu as pltpu

F32 = jnp.float32
BF16 = jnp.bfloat16

LANES = 128
SUBLANES = 8
SCAN_CHUNK = 64
SCAN_SUB = 16
RW_HEAD_DIM = 64
HEADS_PER_PAIR = LANES // RW_HEAD_DIM
VMEM_LIMIT = 56 * 1024 * 1024


def _dot(a, b):
    return jnp.dot(a.astype(BF16), b.astype(BF16), preferred_element_type=F32)


def _dot_nt(a, b):
    return lax.dot_general(a.astype(BF16), b.astype(BF16), (((1,), (1,)), ((), ())),
                           preferred_element_type=F32)


def _dot_tn(a, b):
    return lax.dot_general(a.astype(BF16), b.astype(BF16), (((0,), (0,)), ((), ())),
                           preferred_element_type=F32)


def _dot3(a, b):
    a_hi = a.astype(BF16)
    b_hi = b.astype(BF16)
    a_lo = (a - a_hi.astype(F32)).astype(BF16)
    b_lo = (b - b_hi.astype(F32)).astype(BF16)
    d = functools.partial(jnp.dot, preferred_element_type=F32)
    return d(a_hi, b_hi) + (d(a_hi, b_lo) + d(a_lo, b_hi))


def _scan_masks(reverse):
    C = SCAN_CHUNK
    P = HEADS_PER_PAIR * C
    row = lax.broadcasted_iota(jnp.int32, (P, LANES), 0)
    col = lax.broadcasted_iota(jnp.int32, (P, LANES), 1)
    tr = row % C
    tc = col % C
    t_i = lax.broadcasted_iota(jnp.int32, (C, C), 0)
    s_i = lax.broadcasted_iota(jnp.int32, (C, C), 1)
    return dict(
        same_head=(row // C) == (col // RW_HEAD_DIM),
        incl=(tr >= tc) if not reverse else (tr <= tc),
        strict=(tr > tc) if not reverse else (tr < tc),
        diag_blk=(tr // SCAN_SUB) == (tc // SCAN_SUB),
        eye=row == col,
        cum_mat=jnp.where((t_i >= s_i) if not reverse else (t_i <= s_i), 1.0, 0.0).astype(F32))


def _scan_chunk(chains, states):
    C = SCAN_CHUNK
    n = range(len(chains))
    mk = [ch[6] for ch in chains]
    cum = [ch[8] for ch in chains]
    pre = []
    for i in n:
        r, lw, k, v, kk, a, m, reverse, _ = chains[i]
        g = jnp.exp(cum[i])
        g_prev = jnp.exp(cum[i] - lw)
        g_inv = jnp.exp(-cum[i])
        g_end = jnp.exp(cum[i][C - 1:C, :] if not reverse else cum[i][0:1, :])
        bt = kk * a * g_inv
        kt = k * g_inv
        stack = lambda x, m=m: jnp.where(m["same_head"], jnp.concatenate([x, x], axis=0), 0.0)
        pre.append(dict(kap=stack(kk * g_prev), rt=stack(r * g), bt=stack(bt), kt=stack(kt),
                        v=stack(v), bh=stack(bt * g_end), kh=stack(kt * g_end), g_end=g_end))
    P = HEADS_PER_PAIR * C
    tri = [_dot_nt(jnp.concatenate([pre[i]["kap"], pre[i]["rt"]], axis=0),
                   jnp.concatenate([pre[i]["bt"], pre[i]["kt"]], axis=0)) for i in n]
    lb = [jnp.where(mk[i]["strict"], tri[i][:P, :P], 0.0) for i in n]
    lk = [jnp.where(mk[i]["strict"], tri[i][:P, P:], 0.0) for i in n]
    pb = [jnp.where(mk[i]["incl"], tri[i][P:, :P], 0.0) for i in n]
    pk = [jnp.where(mk[i]["incl"], tri[i][P:, P:], 0.0) for i in n]
    lkv = [_dot(lk[i], pre[i]["v"]) for i in n]

    d = [jnp.where(mk[i]["diag_blk"], lb[i], 0.0) for i in n]
    inv = [jnp.where(mk[i]["eye"], 1.0, 0.0) - d[i] for i in n]
    p = d
    for _ in range(3):
        p = [_dot(p[i], p[i]) for i in n]
        inv = [inv[i] + _dot(inv[i], p[i]) for i in n]
    x = [_dot(inv[i], jnp.concatenate([lb[i] - d[i], pre[i]["kap"], lkv[i]], axis=1)) for i in n]
    e = [x[i][:, :LANES] for i in n]
    rhs = [x[i][:, LANES:] for i in n]
    e2 = [_dot(e[i], e[i]) for i in n]
    t = [rhs[i] + _dot(e2[i], rhs[i]) for i in n]
    gy = [t[i] - _dot(e[i], t[i]) for i in n]

    bhgy = [_dot_tn(pre[i]["bh"], gy[i]) for i in n]
    m_mat = [jnp.where(mk[i]["eye"], pre[i]["g_end"], 0.0) - bhgy[i][:, :LANES] for i in n]
    n_mat = [_dot_tn(pre[i]["kh"], pre[i]["v"]) - bhgy[i][:, LANES:] for i in n]
    pbgy = [_dot(pb[i], gy[i]) for i in n]
    q = [pre[i]["rt"] - pbgy[i][:, :LANES] for i in n]
    z = [_dot(pk[i], pre[i]["v"]) - pbgy[i][:, LANES:] for i in n]
    o_st = [_dot(q[i], states[i]) + z[i] for i in n]
    new = [_dot3(m_mat[i], states[i]) + n_mat[i] for i in n]
    return [(o_st[i][:C] + o_st[i][C:], new[i]) for i in n]


def _scan_kernel(rf, vf, kkf, lwf, af, kf, rb, vb, kkb, lwb, ab, kb, s0, of, ob, sfin, state):
    c = pl.program_id(1)
    n_pairs = rf.shape[-1] // LANES

    @pl.when(c == 0)
    def _():
        state[...] = s0[0]

    chains, states, outs = [], [], []
    for d, (r_, v_, kk_, lw_, a_, k_, o_) in enumerate(((rf, vf, kkf, lwf, af, kf, of),
                                                        (rb, vb, kkb, lwb, ab, kb, ob))):
        masks = _scan_masks(reverse=(d == 1))
        cum = _dot3(masks["cum_mat"], lw_[...])
        for p in range(n_pairs):
            sl = slice(p * LANES, (p + 1) * LANES)
            chains.append((r_[:, sl], lw_[:, sl], k_[:, sl], v_[:, sl], kk_[:, sl], a_[:, sl],
                           masks, d == 1, cum[:, sl]))
            states.append(state[d, p])
            outs.append((o_, sl, d, p))
    for (o, new), (o_, sl, d, p) in zip(_scan_chunk(chains, states), outs):
        o_[:, sl] = o
        state[d, p] = new

    @pl.when(c == pl.num_programs(1) - 1)
    def _():
        sfin[0] = state[...]


def rwkv_scan(r, v, kk, lw, a, kd, s0_bd, row0, n_seq, seq_len):
    width = r.shape[1]
    n_tok = n_seq * seq_len
    n_pairs = width // LANES
    n_chunk = seq_len // SCAN_CHUNK
    c0 = row0 // SCAN_CHUNK
    blk = (SCAN_CHUNK, width)
    dblk = (None, SCAN_CHUNK, width)
    fwd = lambda s, c: (s * n_chunk + c, 0)
    bwd = lambda s, c: (s * n_chunk + n_chunk - 1 - c, 0)
    fwd_in = lambda s, c: (c0 + s * n_chunk + c, 0)
    bwd_in = lambda s, c: (c0 + s * n_chunk + n_chunk - 1 - c, 0)
    fwd_d = lambda s, c: (0, c0 + s * n_chunk + c, 0)
    bwd_d = lambda s, c: (1, c0 + s * n_chunk + n_chunk - 1 - c, 0)
    sspec = pl.BlockSpec((1, 2, n_pairs, LANES, LANES), lambda s, c: (s, 0, 0, 0, 0))
    in_specs = ([pl.BlockSpec(blk, fwd_in)] * 3 + [pl.BlockSpec(dblk, fwd_d)] * 3
                + [pl.BlockSpec(blk, bwd_in)] * 3 + [pl.BlockSpec(dblk, bwd_d)] * 3 + [sspec])
    out_specs = [pl.BlockSpec(blk, fwd), pl.BlockSpec(blk, bwd), sspec]
    out_shape = [jax.ShapeDtypeStruct((n_tok, width), F32)] * 2 + [
        jax.ShapeDtypeStruct(s0_bd.shape, F32)]
    return pl.pallas_call(
        _scan_kernel,
        grid=(n_seq, n_chunk),
        in_specs=in_specs,
        out_specs=out_specs,
        out_shape=out_shape,
        scratch_shapes=[pltpu.VMEM((2, n_pairs, LANES, LANES), F32)],
        compiler_params=pltpu.CompilerParams(
            dimension_semantics=("parallel", "arbitrary"), vmem_limit_bytes=VMEM_LIMIT),
        name="rwkv_scan",
    )(r, v, kk, lw, a, kd, r, v, kk, lw, a, kd, s0_bd)


MOD_ROWS = 8
ROW_BLOCK = 256
RMS_EPS = 1e-6
LN_EPS = 1e-5
RW_GN_EPS = 64e-5


def _mod_kernel(c_ref, w_ref, b_ref, o_ref):
    c = c_ref[...]
    s = c * jax.nn.sigmoid(c)
    o_ref[...] = _dot(s, w_ref[...]) + b_ref[...]


def modulation(cond, w_mod, b_mod):
    n, d = cond.shape
    cols = w_mod.shape[1]
    blk = 1024
    return pl.pallas_call(
        _mod_kernel,
        grid=(cols // blk,),
        in_specs=[pl.BlockSpec((n, d), lambda j: (0, 0)),
                  pl.BlockSpec((d, blk), lambda j: (0, j)),
                  pl.BlockSpec((1, blk), lambda j: (0, j))],
        out_specs=pl.BlockSpec((n, blk), lambda j: (0, j)),
        out_shape=jax.ShapeDtypeStruct((n, cols), F32),
        name="modulation",
    )(cond, w_mod, b_mod)


def _modulate(x, shift, scale):
    ms = jnp.mean(x * x, axis=-1, keepdims=True)
    return x * lax.rsqrt(ms + RMS_EPS) * (1.0 + scale) + shift


def _inproj_kernel(x_ref, mod_ref, w_ref, *o_refs):
    h = _modulate(x_ref[...], mod_ref[0, 0:1, :], mod_ref[0, 1:2, :]).astype(BF16)
    start = 0
    for o_ref in o_refs:
        n = o_ref.shape[1]
        o_ref[...] = jnp.dot(h, w_ref[:, start:start + n], preferred_element_type=F32)
        start += n


def in_projection(x, mod_blocks, w_in, layer, sizes):
    n_tok, d = x.shape
    row = lambda i: (i, 0)
    return pl.pallas_call(
        _inproj_kernel,
        grid=(n_tok // ROW_BLOCK,),
        in_specs=[pl.BlockSpec((ROW_BLOCK, d), row),
                  pl.BlockSpec((1, MOD_ROWS, d), lambda i: (i, 0, 0)),
                  pl.BlockSpec((None, d, w_in.shape[2]), lambda i: (layer, 0, 0),
                               pipeline_mode=pl.Buffered(1))],
        out_specs=[pl.BlockSpec((ROW_BLOCK, n), row) for n in sizes],
        out_shape=[jax.ShapeDtypeStruct((n_tok, n), F32) for n in sizes],
        compiler_params=pltpu.CompilerParams(
            dimension_semantics=("parallel",), vmem_limit_bytes=VMEM_LIMIT),
        name="in_projection",
    )(x, mod_blocks, w_in)


HEAD_DIM = 64


def _group_ones():
    r = lax.broadcasted_iota(jnp.int32, (LANES, LANES), 0) // HEAD_DIM
    c = lax.broadcasted_iota(jnp.int32, (LANES, LANES), 1) // HEAD_DIM
    return jnp.where(r == c, 1.0, 0.0).astype(BF16)


def _group_sum(x, ones):
    hi = x.astype(BF16)
    lo = (x - hi.astype(F32)).astype(BF16)
    d = functools.partial(jnp.dot, preferred_element_type=F32)
    tiles = [slice(s, s + LANES) for s in range(0, x.shape[1], LANES)]
    sums = [d(hi[:, t], ones) + d(lo[:, t], ones) for t in tiles]
    return sums[0] if len(sums) == 1 else jnp.concatenate(sums, axis=1)


RW_WIDTH = 512
RW_COLS = 3 * RW_WIDTH + 3 * LANES
RW_VEC_ROWS = 8


def _rwkv_prep_kernel(z_ref, hp_ref, hn_ref, mu_ref, vec_ref, w2_ref, a2_ref, g2_ref,
                      r_ref, v_ref, kk_ref, lw_ref, a_ref, kd_ref, bonus_ref, gate_ref):
    z = z_ref[...]
    rb = z.shape[0]
    row = lax.broadcasted_iota(jnp.int32, (rb, 1), 0)
    z_prev = jnp.where(row == 0, hp_ref[0], pltpu.roll(z, 1, 0))
    z_next = jnp.where(row == rb - 1, hn_ref[0], pltpu.roll(z, rb - 1, 0))
    zs = z + mu_ref[0:1, :] * (z_prev - z) + mu_ref[1:2, :] * (z_next - z)
    w = RW_WIDTH
    r, k, v = zs[:, :w], zs[:, w:2 * w], zs[:, 2 * w:3 * w]
    zw = zs[:, 3 * w:3 * w + LANES]
    za = zs[:, 3 * w + LANES:3 * w + 2 * LANES]
    zg = zs[:, 3 * w + 2 * LANES:]
    ones = _group_ones()
    kk = k * vec_ref[0:1, :]
    kk = kk * lax.rsqrt(_group_sum(kk * kk, ones) + 1e-12)
    r_ref[...] = r
    v_ref[...] = v
    kk_ref[...] = kk
    tz = jnp.tanh(zw)
    for d in range(2):
        w_raw = vec_ref[3 + d:4 + d, :] + _dot(tz, w2_ref[d])
        lw_ref[d] = -math.exp(-0.5) * jax.nn.sigmoid(w_raw)
        a = jax.nn.sigmoid(vec_ref[5 + d:6 + d, :] + _dot(za, a2_ref[d]))
        a_ref[d] = a
        kd_ref[d] = k * (1.0 + (a - 1.0) * vec_ref[1:2, :])
    bonus_ref[...] = _group_sum(r * k * vec_ref[2:3, :], ones) * v
    gate_ref[...] = _dot(jax.nn.sigmoid(zg), g2_ref[...])


def rwkv_prep(z_rw, halo_prev, halo_next, mu, vecs, w2_pad, a2_pad, g2):
    n_tok = z_rw.shape[0]
    w = RW_WIDTH
    row = lambda i: (i, 0)
    const2 = lambda i: (0, 0)
    const3 = lambda i: (0, 0, 0)
    halo = pl.BlockSpec((1, 1, RW_COLS), lambda i: (i, 0, 0))
    one = pl.BlockSpec((ROW_BLOCK, w), row)
    two = pl.BlockSpec((2, ROW_BLOCK, w), lambda i: (0, i, 0))
    s1 = jax.ShapeDtypeStruct((n_tok, w), F32)
    s2 = jax.ShapeDtypeStruct((2, n_tok, w), F32)
    return pl.pallas_call(
        _rwkv_prep_kernel,
        grid=(n_tok // ROW_BLOCK,),
        in_specs=[pl.BlockSpec((ROW_BLOCK, RW_COLS), row), halo, halo,
                  pl.BlockSpec(mu.shape, const2), pl.BlockSpec(vecs.shape, const2),
                  pl.BlockSpec(w2_pad.shape, const3), pl.BlockSpec(a2_pad.shape, const3),
                  pl.BlockSpec(g2.shape, const2)],
        out_specs=[one, one, one, two, two, two, one, one],
        out_shape=[s1, s1, s1, s2, s2, s2, s1, s1],
        compiler_params=pltpu.CompilerParams(
            dimension_semantics=("parallel",), vmem_limit_bytes=VMEM_LIMIT),
        name="rwkv_prep",
    )(z_rw, halo_prev, halo_next, mu, vecs, w2_pad, a2_pad, g2)


CONV_WIDTH = 512
CONV_KERNEL = 31
CONV_PAD = 16
CONV_ROWS = 64


def _conv_kernel(z_ref, w_ref, vec_ref, o_ref, pad_ref):
    t = z_ref.shape[0]
    cw = CONV_WIDTH
    half = CONV_KERNEL // 2
    zeros = jnp.zeros((CONV_PAD, cw), F32)
    pad_ref[0:CONV_PAD, :] = zeros
    pad_ref[CONV_PAD + t:CONV_PAD + t + CONV_PAD, :] = zeros
    pad_ref[CONV_PAD:CONV_PAD + t, :] = z_ref[:, :cw] * jax.nn.sigmoid(z_ref[:, cw:])

    def chunk(i, carry):
        base = pl.multiple_of(i * CONV_ROWS, CONV_ROWS)
        window = pad_ref[pl.ds(base, CONV_ROWS + 2 * CONV_PAD), :]
        span = CONV_ROWS + 2 * CONV_PAD - SUBLANES
        shifted = [window[rho:rho + span] for rho in range(SUBLANES)]
        acc = jnp.zeros((CONV_ROWS, cw), F32)
        for j in range(CONV_KERNEL):
            lo = CONV_PAD - half + j
            tile = (lo // SUBLANES) * SUBLANES
            acc = acc + w_ref[j:j + 1, :] * shifted[lo % SUBLANES][tile:tile + CONV_ROWS]
        u = acc + vec_ref[0:1, :]
        mu = jnp.mean(u, axis=-1, keepdims=True)
        cen = u - mu
        var = jnp.mean(cen * cen, axis=-1, keepdims=True)
        y = cen * lax.rsqrt(var + LN_EPS) * vec_ref[1:2, :] + vec_ref[2:3, :]
        o_ref[pl.ds(base, CONV_ROWS), :] = y * jax.nn.sigmoid(y)
        return carry

    lax.fori_loop(0, t // CONV_ROWS, chunk, 0)


def conv_module(z_cv, conv_w, vecs, row0, n_seq, seq_len):
    cw = CONV_WIDTH
    blk0 = row0 // seq_len
    return pl.pallas_call(
        _conv_kernel,
        grid=(n_seq,),
        in_specs=[pl.BlockSpec((seq_len, 2 * cw), lambda i: (blk0 + i, 0)),
                  pl.BlockSpec(conv_w.shape, lambda i: (0, 0)),
                  pl.BlockSpec(vecs.shape, lambda i: (0, 0))],
        out_specs=pl.BlockSpec((seq_len, cw), lambda i: (i, 0)),
        out_shape=jax.ShapeDtypeStruct((n_seq * seq_len, cw), F32),
        scratch_shapes=[pltpu.VMEM((seq_len + 2 * CONV_PAD, cw), F32)],
        compiler_params=pltpu.CompilerParams(
            dimension_semantics=("parallel",), vmem_limit_bytes=VMEM_LIMIT),
        name="conv_module",
    )(z_cv, conv_w, vecs)


ATT_Q_BLOCK = 256
ATT_HEADS = 8
ATT_KV_HEADS = 2
ATT_GROUP = ATT_HEADS // ATT_KV_HEADS


def _rot_partner():
    r = lax.broadcasted_iota(jnp.int32, (LANES, LANES), 0)
    c = lax.broadcasted_iota(jnp.int32, (LANES, LANES), 1)
    lo = (r % 32) < 16
    return jnp.where((c == r + 16) & lo, 1.0, jnp.where((c == r - 16) & ~lo, -1.0, 0.0)).astype(BF16)


def _attn_kernel(*refs, rotary):
    if rotary:
        (zq_ref, zkv_ref, nq_ref, nk_ref, cosq_ref, sinq_ref, cosk_ref, sink_ref, ck_ref, cv_ref,
         o_ref) = refs
    else:
        zq_ref, zkv_ref, nq_ref, nk_ref, o_ref, kn_ref = refs
    tq = zq_ref.shape[0]
    ones = _group_ones()
    lane = lax.broadcasted_iota(jnp.int32, (1, LANES), 1)
    first = lane < HEAD_DIM
    row_head = lax.broadcasted_iota(jnp.int32, (2 * tq, LANES), 0) // tq
    own = row_head == (lax.broadcasted_iota(jnp.int32, (2 * tq, LANES), 1) // HEAD_DIM)

    def norm(x, g):
        return x * lax.rsqrt(_group_sum(x * x, ones) * (1.0 / HEAD_DIM) + RMS_EPS) * g

    def rope(x, cos, sin):
        return x * cos + jnp.dot(x.astype(BF16), _rot_partner(), preferred_element_type=F32) * sin

    def dup(x):
        sw = pltpu.roll(x, HEAD_DIM, 1)
        return jnp.where(first, x, sw), jnp.where(first, sw, x)

    k = norm(zkv_ref[:, :LANES], nk_ref[...])
    v = zkv_ref[:, LANES:]
    if rotary:
        k = rope(k, cosk_ref[...], sink_ref[...])
        ck, cv = dup(ck_ref[...]), dup(cv_ref[...])
    else:
        kn_ref[...] = k
    kd, vd = dup(k), dup(v)
    for pair in range(ATT_HEADS // 2):
        g = pair // (ATT_GROUP // 2)
        q = norm(zq_ref[:, pair * LANES:(pair + 1) * LANES], nq_ref[...])
        if rotary:
            q = rope(q, cosq_ref[...], sinq_ref[...])
        q = q * (HEAD_DIM ** -0.5)
        qs = jnp.where(own, jnp.concatenate([q, q], axis=0), 0.0)
        s = _dot_nt(qs, kd[g])
        m = jnp.max(s, axis=-1, keepdims=True)
        if rotary:
            s2 = _dot_nt(qs, ck[g])
            m = jnp.maximum(m, jnp.max(s2, axis=-1, keepdims=True))
            p2 = jnp.exp(s2 - m)
        p = jnp.exp(s - m)
        den = jnp.sum(p, axis=-1, keepdims=True)
        o = _dot(p, vd[g])
        if rotary:
            den = den + jnp.sum(p2, axis=-1, keepdims=True)
            o = o + _dot(p2, cv[g])
        o = o / den
        o_ref[:, pair * LANES:(pair + 1) * LANES] = jnp.where(first, o[:tq], o[tq:])


def attention(z_q, z_kv, nq, nk, row0, n_seq, seq_len, rope=None, cache=None):
    qb = min(ATT_Q_BLOCK, seq_len)
    nqb = seq_len // qb
    qblk0 = row0 // qb
    sblk0 = row0 // seq_len
    wq = z_q.shape[1]
    in_specs = [pl.BlockSpec((qb, wq), lambda b, t: (qblk0 + b * nqb + t, 0)),
                pl.BlockSpec((seq_len, 2 * LANES), lambda b, t: (sblk0 + b, 0)),
                pl.BlockSpec((1, LANES), lambda b, t: (0, 0)),
                pl.BlockSpec((1, LANES), lambda b, t: (0, 0))]
    args = [z_q, z_kv, nq, nk]
    out_q = pl.BlockSpec((qb, wq), lambda b, t: (b * nqb + t, 0))
    y_shape = jax.ShapeDtypeStruct((n_seq * seq_len, wq), F32)
    if rope is None:
        out_specs = [out_q, pl.BlockSpec((seq_len, LANES), lambda b, t: (b, 0))]
        out_shape = [y_shape, jax.ShapeDtypeStruct((n_seq * seq_len, LANES), F32)]
    else:
        past = cache[0].shape[0] // n_seq
        in_specs += [pl.BlockSpec((qb, LANES), lambda b, t: (t, 0))] * 2
        in_specs += [pl.BlockSpec((seq_len, LANES), lambda b, t: (0, 0))] * 2
        in_specs += [pl.BlockSpec((past, LANES), lambda b, t: (b, 0))] * 2
        args += [rope[0], rope[1], rope[0], rope[1], cache[0], cache[1]]
        out_specs = out_q
        out_shape = y_shape
    return pl.pallas_call(
        functools.partial(_attn_kernel, rotary=rope is not None),
        grid=(n_seq, nqb),
        in_specs=in_specs,
        out_specs=out_specs,
        out_shape=out_shape,
        compiler_params=pltpu.CompilerParams(
            dimension_semantics=("parallel", "arbitrary"), vmem_limit_bytes=VMEM_LIMIT),
        name="attention",
    )(*args)


def _merge_kernel(x_ref, mod_ref, bonus_ref, gate_ref, zg_ref,
                  ofc_ref, ofl_ref, obc_ref, obl_ref, cvc_ref, cvl_ref, atc_ref, atl_ref,
                  gn_ref, wr_ref, wc_ref, wa_ref, wo_ref, wq_ref, xo_ref, h2_ref, q_ref, *, n_ctx_blocks):
    d = x_ref.shape[1]
    zg = zg_ref[...]
    is_ctx = pl.program_id(0) < n_ctx_blocks
    pick = lambda c_ref, l_ref: jnp.where(is_ctx, c_ref[...], l_ref[...])
    ones = _group_ones()
    o = pick(ofc_ref, ofl_ref) + pick(obc_ref, obl_ref)
    cen = o - _group_sum(o, ones) * (1.0 / HEAD_DIM)
    var = _group_sum(cen * cen, ones) * (1.0 / HEAD_DIM)
    o = cen * lax.rsqrt(var + RW_GN_EPS) * gn_ref[0:1, :] + gn_ref[1:2, :]
    y_rw = (o + bonus_ref[...]) * gate_ref[...]
    merged = (jax.nn.sigmoid(zg[:, :d]) * _dot(y_rw, wr_ref[...])
              + jax.nn.sigmoid(zg[:, d:2 * d]) * _dot(pick(cvc_ref, cvl_ref), wc_ref[...])
              + jax.nn.sigmoid(zg[:, 2 * d:]) * _dot(pick(atc_ref, atl_ref), wa_ref[...]))
    x = x_ref[...] + mod_ref[0, 2:3, :] * _dot(merged, wo_ref[...])
    xo_ref[...] = x
    h2 = _modulate(x, mod_ref[0, 3:4, :], mod_ref[0, 4:5, :]).astype(BF16)
    h2_ref[...] = h2
    q_ref[...] = jnp.dot(h2, wq_ref[...], preferred_element_type=F32)


def merge_project(x, mod_blocks, bonus, gate, z_gate, o_f, o_b, y_cv, y_at, gn, w_r, w_c, w_a, w_o, w_q):
    n_tok, d = x.shape
    nc = o_f[0].shape[0] // ROW_BLOCK
    row = lambda i: (i, 0)
    const = lambda i: (0, 0)
    ctx_row = lambda i: (jnp.minimum(i, nc - 1), 0)
    lat_row = lambda i: (jnp.maximum(i - nc, 0), 0)
    acts = [bonus, gate, z_gate]
    weights = [gn, w_r, w_c, w_a, w_o, w_q]
    pair_specs, pair_args = [], []
    for c_arr, l_arr in (o_f, o_b, y_cv, y_at):
        pair_specs += [pl.BlockSpec((ROW_BLOCK, c_arr.shape[1]), ctx_row),
                       pl.BlockSpec((ROW_BLOCK, l_arr.shape[1]), lat_row)]
        pair_args += [c_arr, l_arr]
    return pl.pallas_call(
        functools.partial(_merge_kernel, n_ctx_blocks=nc),
        grid=(n_tok // ROW_BLOCK,),
        in_specs=[pl.BlockSpec((ROW_BLOCK, d), row),
                  pl.BlockSpec((1, MOD_ROWS, d), lambda i: (i, 0, 0))]
                 + [pl.BlockSpec((ROW_BLOCK, a.shape[1]), row) for a in acts]
                 + pair_specs
                 + [pl.BlockSpec(w.shape, const) for w in weights],
        out_specs=[pl.BlockSpec((ROW_BLOCK, d), row), pl.BlockSpec((ROW_BLOCK, d), row),
                   pl.BlockSpec((ROW_BLOCK, w_q.shape[1]), row)],
        out_shape=[jax.ShapeDtypeStruct((n_tok, d), F32), jax.ShapeDtypeStruct((n_tok, d), BF16),
                   jax.ShapeDtypeStruct((n_tok, w_q.shape[1]), F32)],
        compiler_params=pltpu.CompilerParams(
            dimension_semantics=("parallel",), vmem_limit_bytes=VMEM_LIMIT),
        name="merge_project",
    )(x, mod_blocks, *acts, *pair_args, *weights)


PEER_HEADS = 8
PEER_N_KEYS = 128
PEER_HALF = 64
PEER_TOPK = 16
PEER_SEL_BLOCK = 128
PEER_LOOP_UNROLL = 128
PEER_RELAYOUT = 16
NEG = -1e30


def _dot3_nt(a, b):
    a_hi = a.astype(BF16)
    b_hi = b.astype(BF16)
    a_lo = (a - a_hi.astype(F32)).astype(BF16)
    b_lo = (b - b_hi.astype(F32)).astype(BF16)
    d = lambda x, y: lax.dot_general(x, y, (((1,), (1,)), ((), ())), preferred_element_type=F32)
    return d(a_hi, b_hi) + (d(a_hi, b_lo) + d(a_lo, b_hi))


def _extract_max(s, code):
    vals = [s[i:i + SUBLANES] for i in range(0, s.shape[0], SUBLANES)]
    codes = [code[i:i + SUBLANES] for i in range(0, s.shape[0], SUBLANES)]
    while len(vals) > 1:
        nv, nc = [], []
        for i in range(0, len(vals) - 1, 2):
            keep = vals[i] >= vals[i + 1]
            nv.append(jnp.maximum(vals[i], vals[i + 1]))
            nc.append(jnp.where(keep, codes[i], codes[i + 1]))
        if len(vals) % 2:
            nv.append(vals[-1])
            nc.append(codes[-1])
        vals, codes = nv, nc
    m = jnp.max(vals[0], axis=0, keepdims=True)
    pos = jnp.min(jnp.where(vals[0] == m, codes[0], 1e9), axis=0, keepdims=True)
    return m, pos, jnp.where(code == pos, NEG, s)


def _pair_candidates(a0, a1):
    k = PEER_TOPK
    sub = lax.broadcasted_iota(jnp.int32, (k, a0.shape[1]), 0).astype(F32)
    sub8 = sub[:8]
    vals = [a0[0:1] + a1]
    codes = [sub]
    for r0 in range(1, 8):
        limit = k // (r0 + 1)
        vals.append(jnp.where(sub8 < limit, a0[r0:r0 + 1] + a1[:8], NEG))
        codes.append(sub8 + float(r0 * k))
    vals.append(a0[8:] + a1[0:1])
    codes.append((sub8 + 8.0) * float(k))
    return jnp.concatenate(vals, axis=0), jnp.concatenate(codes, axis=0)


def _peer_select_kernel(q_ref, keys_ref, w_ref, top_v, top_i, slot_i, slot_j, slot_g,
                        rows_i, rows_j, rows_g, w3_ref):
    k = PEER_TOPK
    ct = q_ref.shape[0]
    key_code = lax.broadcasted_iota(jnp.int32, (PEER_N_KEYS, ct), 0).astype(F32)
    rank = lax.broadcasted_iota(jnp.int32, (k, ct), 0).astype(F32)
    for h in range(PEER_HEADS):
        st = _dot3_nt(keys_ref[h], q_ref[:, h * LANES:(h + 1) * LANES])
        s = [st[:PEER_N_KEYS], st[PEER_N_KEYS:]]
        for it in range(k):
            for p in range(2):
                m, pos, s[p] = _extract_max(s[p], key_code)
                top_v[h, p, it:it + 1, :] = m
                top_i[h, p, it:it + 1, :] = pos
    for h0 in range(0, PEER_HEADS, 2):
        heads = (h0, h0 + 1)
        cand, code, best, z = {}, {}, {}, {}
        for h in heads:
            a0, a1 = top_v[h, 0], top_v[h, 1]
            cand[h], code[h] = _pair_candidates(a0, a1)
            best[h] = a0[0:1] + a1[0:1]
            z[h] = jnp.zeros_like(best[h])
        for it in range(k):
            for h in heads:
                m, pos, cand[h] = _extract_max(cand[h], code[h])
                r0 = jnp.floor(pos * (1.0 / k))
                r1 = pos - r0 * k
                e = jnp.exp(m - best[h])
                z[h] = z[h] + e
                slot = h * k + it
                slot_i[slot:slot + 1, :] = jnp.sum(jnp.where(rank == r0, top_i[h, 0], 0.0), axis=0,
                                                   keepdims=True)
                slot_j[slot:slot + 1, :] = jnp.sum(jnp.where(rank == r1, top_i[h, 1], 0.0), axis=0,
                                                   keepdims=True)
                slot_g[slot:slot + 1, :] = e
        for h in heads:
            slot_g[h * k:(h + 1) * k, :] = slot_g[h * k:(h + 1) * k, :] / z[h]

    rows_i[...] = slot_i[...].T
    rows_j[...] = slot_j[...].T
    rows_g[...] = slot_g[...].T
    n = PEER_N_KEYS
    sub = lax.broadcasted_iota(jnp.int32, (n, n), 0).astype(F32)

    def per_token(c, carry):
        a_t = jnp.where(sub == rows_i[pl.ds(c, 1), :], rows_g[pl.ds(c, 1), :], 0.0)
        b_t = jnp.where(sub == rows_j[pl.ds(c, 1), :], 1.0, 0.0)
        w3_ref[c] = _dot_nt(a_t, b_t).astype(BF16)
        return carry

    lax.fori_loop(0, ct, per_token, 0, unroll=PEER_LOOP_UNROLL)

    g = PEER_RELAYOUT
    r_out = lax.broadcasted_iota(jnp.int32, (g * g, g * g), 0)
    r_in = lax.broadcasted_iota(jnp.int32, (g * g, g * g), 1)
    perm = jnp.where(r_in == (r_out % g) * g + r_out // g, 1.0, 0.0).astype(BF16)

    for ib in range(n // g):
        for t in range(ct // g):
            z = w3_ref[t * g:(t + 1) * g, ib * g:(ib + 1) * g, :].reshape(g * g, n)
            y = jnp.dot(perm, z, preferred_element_type=F32)
            for kk in range(g):
                col = (ib * g + kk) * n
                w_ref[t * g:(t + 1) * g, col:col + n] = y[kk * g:(kk + 1) * g].astype(BF16)


def peer_select(q, keys_padded):
    n_tok, qd = q.shape
    ct = PEER_SEL_BLOCK
    n = PEER_N_KEYS
    slots = PEER_HEADS * PEER_TOPK
    return pl.pallas_call(
        _peer_select_kernel,
        grid=(n_tok // ct,),
        in_specs=[pl.BlockSpec((ct, qd), lambda i: (i, 0)),
                  pl.BlockSpec(keys_padded.shape, lambda i: (0, 0, 0))],
        out_specs=pl.BlockSpec((ct, n * n), lambda i: (i, 0)),
        out_shape=jax.ShapeDtypeStruct((n_tok, n * n), BF16),
        scratch_shapes=[pltpu.VMEM((PEER_HEADS, 2, PEER_TOPK, ct), F32),
                        pltpu.VMEM((PEER_HEADS, 2, PEER_TOPK, ct), F32),
                        pltpu.VMEM((slots, ct), F32), pltpu.VMEM((slots, ct), F32),
                        pltpu.VMEM((slots, ct), F32),
                        pltpu.VMEM((ct, slots), F32), pltpu.VMEM((ct, slots), F32),
                        pltpu.VMEM((ct, slots), F32),
                        pltpu.VMEM((ct, n, n), BF16)],
        compiler_params=pltpu.CompilerParams(
            dimension_semantics=("parallel",), vmem_limit_bytes=VMEM_LIMIT),
        name="peer_select",
    )(q, keys_padded)


PEER_TOKEN_BLOCK = 1024
PEER_EXPERT_BLOCK = 1024


def _peer_kernel(h_ref, w_ref, u_ref, v_ref, x_ref, mod_ref, o_ref, acc_ref):
    e = pl.program_id(1)

    @pl.when(e == 0)
    def _():
        acc_ref[...] = jnp.zeros_like(acc_ref)

    s = _dot_nt(h_ref[...], u_ref[...])
    act = 0.5 * s * (1.0 + lax.erf(s * (2.0 ** -0.5)))
    acc_ref[...] += jnp.dot(act.astype(BF16) * w_ref[...], v_ref[...], preferred_element_type=F32)

    @pl.when(e == pl.num_programs(1) - 1)
    def _():
        for j in range(mod_ref.shape[0]):
            rows = slice(j * ROW_BLOCK, (j + 1) * ROW_BLOCK)
            o_ref[rows, :] = x_ref[rows, :] + mod_ref[j, 5:6, :] * acc_ref[rows, :]


def peer_dense(h2, w_sel, u_tabs, v_tabs, layer, x, mod_blocks):
    n_tok, d = x.shape
    n_exp = u_tabs.shape[1]
    tb, eb = PEER_TOKEN_BLOCK, PEER_EXPERT_BLOCK
    tok = lambda i, e: (i, 0)
    return pl.pallas_call(
        _peer_kernel,
        grid=(n_tok // tb, n_exp // eb),
        in_specs=[pl.BlockSpec((tb, d), tok),
                  pl.BlockSpec((tb, eb), lambda i, e: (i, e)),
                  pl.BlockSpec((None, eb, d), lambda i, e: (layer, e, 0)),
                  pl.BlockSpec((None, eb, d), lambda i, e: (layer, e, 0)),
                  pl.BlockSpec((tb, d), tok),
                  pl.BlockSpec((tb // ROW_BLOCK, MOD_ROWS, d), lambda i, e: (i, 0, 0))],
        out_specs=pl.BlockSpec((tb, d), tok),
        out_shape=jax.ShapeDtypeStruct((n_tok, d), F32),
        scratch_shapes=[pltpu.VMEM((tb, d), F32)],
        compiler_params=pltpu.CompilerParams(
            dimension_semantics=("parallel", "arbitrary"), vmem_limit_bytes=VMEM_LIMIT),
        name="peer_dense",
    )(h2, w_sel, u_tabs, v_tabs, x, mod_blocks)


RW_HEADS = RW_WIDTH // RW_HEAD_DIM
ROT_HALF = HEAD_DIM // 2
ROPE_THETA = 10000.0
GRID_W = 64


def _bd_from_states(s):
    a = jnp.swapaxes(s, -1, -2)
    n, d, h, m, _ = a.shape
    a = a.reshape(n, d, h // 2, 2, m, m)
    z = jnp.zeros_like(a[:, :, :, 0])
    top = jnp.concatenate([a[:, :, :, 0], z], axis=-1)
    bot = jnp.concatenate([z, a[:, :, :, 1]], axis=-1)
    return jnp.concatenate([top, bot], axis=-2)


def _states_from_bd(bd):
    n, d, p, _, _ = bd.shape
    m = RW_HEAD_DIM
    a = jnp.stack([bd[:, :, :, :m, :m], bd[:, :, :, m:, m:]], axis=3).reshape(n, d, 2 * p, m, m)
    return jnp.swapaxes(a, -1, -2)


def _rope_tables(t):
    rows = t // GRID_W
    row, col = jnp.meshgrid(jnp.arange(rows), jnp.arange(GRID_W), indexing='ij')
    inv = ROPE_THETA ** (-jnp.arange(0, ROT_HALF, 2, dtype=F32) / ROT_HALF)
    ang_r = row.reshape(-1, 1).astype(F32) * inv
    ang_c = col.reshape(-1, 1).astype(F32) * inv
    cr, sr, cc, sc = jnp.cos(ang_r), jnp.sin(ang_r), jnp.cos(ang_c), jnp.sin(ang_c)
    cos = jnp.concatenate([cr, cr, cc, cc], axis=-1)
    sin = jnp.concatenate([sr, sr, sc, sc], axis=-1)
    return jnp.tile(cos, (1, LANES // HEAD_DIM)), jnp.tile(sin, (1, LANES // HEAD_DIM))


def _pad_rows(w, top):
    z = jnp.zeros_like(w)
    return jnp.concatenate([w, z] if top else [z, w], axis=0)


def kernel(x_prompt, x_sample, c, cache_k, cache_v, state_wkv, c_ctx, w_mod, b_mod, w_in, shift_mu_prev, shift_mu_next, rw_w0, rw_w2, rw_a0, rw_a2, rw_g2, rw_k_k, rw_k_a, rw_r_k, rw_gn_w, rw_gn_b, w_out_rwkv, conv_w, conv_b, conv_ln_w, conv_ln_b, w_out_conv, q_norm, k_norm, w_out_attn, w_o, peer_wq, peer_keys, peer_u, peer_v):
    nb, seq, d = x_prompt.shape
    db, dseq, _ = x_sample.shape
    depth = w_in.shape[0]
    n_ctx = nb * seq
    n_lat = db * dseq
    n_tok = n_ctx + n_lat
    x = jnp.concatenate([x_prompt.reshape(n_ctx, d), x_sample.reshape(n_lat, d)], axis=0)

    cond = jnp.concatenate([c_ctx[None], c, jnp.zeros((MOD_ROWS - 1 - db, d), F32)], axis=0)
    blk_cond = jnp.concatenate([jnp.zeros((n_ctx // ROW_BLOCK,), jnp.int32),
                                1 + jnp.arange(n_lat // ROW_BLOCK, dtype=jnp.int32) // (dseq // ROW_BLOCK)])
    n_blk = n_tok // ROW_BLOCK
    blk_row = jnp.arange(n_blk) * ROW_BLOCK
    blk_pos = jnp.where(blk_row < n_ctx, blk_row % seq, (blk_row - n_ctx) % dseq)
    blk_len = jnp.where(blk_row < n_ctx, seq, dseq)
    starts_seq = (blk_pos == 0)[:, None]
    ends_seq = (blk_pos + ROW_BLOCK == blk_len)[:, None]
    rope = _rope_tables(dseq)
    in_sizes = (RW_COLS, 2 * CONV_WIDTH, ATT_HEADS * HEAD_DIM, 2 * ATT_KV_HEADS * HEAD_DIM, 3 * d)
    past = cache_k.shape[2]

    w_in_bf, u_bf, v_bf = w_in.astype(BF16), peer_u.astype(BF16), peer_v.astype(BF16)
    ctx_k, ctx_v, ctx_s = [], [], []
    for l in range(depth):
        mod = modulation(cond, w_mod[l], b_mod[l][None])
        mod = mod.reshape(MOD_ROWS, 6, d)[blk_cond]
        mod_blocks = jnp.concatenate([mod, jnp.zeros((mod.shape[0], MOD_ROWS - 6, d), F32)], axis=1)

        z_rw, z_cv, z_q, z_kv, z_gate = in_projection(x, mod_blocks, w_in_bf, l, in_sizes)

        zb = z_rw.reshape(n_blk, ROW_BLOCK, RW_COLS)
        halo_prev = jnp.where(starts_seq, 0.0, jnp.roll(zb[:, -1], 1, axis=0))[:, None]
        halo_next = jnp.where(ends_seq, 0.0, jnp.roll(zb[:, 0], -1, axis=0))[:, None]
        mu = jnp.stack([shift_mu_prev[l], shift_mu_next[l]])
        vecs = jnp.stack([rw_k_k[l], rw_k_a[l], rw_r_k[l], rw_w0[l, 0], rw_w0[l, 1], rw_a0[l, 0],
                          rw_a0[l, 1], jnp.zeros((RW_WIDTH,), F32)])
        w2_pad = jnp.stack([_pad_rows(rw_w2[l, 0], True), _pad_rows(rw_w2[l, 1], False)]).astype(BF16)
        a2_pad = jnp.stack([_pad_rows(rw_a2[l, 0], True), _pad_rows(rw_a2[l, 1], False)]).astype(BF16)
        r, v, kk, lw, aa, kd, bonus, gate = rwkv_prep(z_rw, halo_prev, halo_next, mu, vecs, w2_pad, a2_pad,
                                                      rw_g2[l].astype(BF16))
        s0_ctx = jnp.zeros((nb, 2, RW_HEADS // 2, LANES, LANES), F32)
        of_c, ob_c, sfin = rwkv_scan(r, v, kk, lw, aa, kd, s0_ctx, 0, nb, seq)
        of_l, ob_l, _ = rwkv_scan(r, v, kk, lw, aa, kd, _bd_from_states(state_wkv[:, l]), n_ctx, db, dseq)
        ctx_s.append(_states_from_bd(sfin))

        cvec = jnp.stack([conv_b[l], conv_ln_w[l], conv_ln_b[l]] + [jnp.zeros((CONV_WIDTH,), F32)] * 5)
        y_cv = (conv_module(z_cv, conv_w[l], cvec, 0, nb, seq),
                conv_module(z_cv, conv_w[l], cvec, n_ctx, db, dseq))

        nq = jnp.tile(q_norm[l], LANES // HEAD_DIM)[None]
        nk = jnp.tile(k_norm[l], LANES // HEAD_DIM)[None]
        y_c, k_ctx = attention(z_q, z_kv, nq, nk, 0, nb, seq)
        y_l = attention(z_q, z_kv, nq, nk, n_ctx, db, dseq, rope=rope,
                        cache=(cache_k[:, l].reshape(db * past, LANES), cache_v[:, l].reshape(db * past, LANES)))
        ctx_k.append(k_ctx.reshape(nb, seq, ATT_KV_HEADS, HEAD_DIM))
        ctx_v.append(z_kv[:n_ctx, LANES:].reshape(nb, seq, ATT_KV_HEADS, HEAD_DIM))

        x, h2, pq = merge_project(x, mod_blocks, bonus, gate, z_gate,
                                  (of_c, of_l), (ob_c, ob_l), y_cv, (y_c, y_l),
                                  jnp.stack([rw_gn_w[l], rw_gn_b[l]]),
                                  w_out_rwkv[l].astype(BF16), w_out_conv[l].astype(BF16),
                                  w_out_attn[l].astype(BF16), w_o[l].astype(BF16), peer_wq[l].astype(BF16))

        kz = jnp.zeros_like(peer_keys[l][:, 0])
        keys_padded = jnp.concatenate([jnp.concatenate([peer_keys[l][:, 0], kz], axis=-1),
                                       jnp.concatenate([kz, peer_keys[l][:, 1]], axis=-1)], axis=1)
        w_sel = peer_select(pq, keys_padded)
        x = peer_dense(h2, w_sel, u_bf, v_bf, l, x, mod_blocks)

    new_cache_k = jnp.stack(ctx_k, axis=1)
    new_cache_v = jnp.stack(ctx_v, axis=1)
    new_state = jnp.stack(ctx_s, axis=1)
    return (x[:n_ctx].reshape(nb, seq, d), x[n_ctx:].reshape(db, dseq, d),
            new_cache_k, new_cache_v, new_state)
```

```python
import functools
import math

import jax
import jax.numpy as jnp
from jax import lax
from jax.experimental import pallas as pl
from jax.experimental.pallas import tpu as pltpu

F32 = jnp.float32
BF16 = jnp.bfloat16

LANES = 128
SUBLANES = 8
SCAN_CHUNK = 64
SCAN_SUB = 16
RW_HEAD_DIM = 64
HEADS_PER_PAIR = LANES // RW_HEAD_DIM
VMEM_LIMIT = 56 * 1024 * 1024


def _dot(a, b):
    return jnp.dot(a.astype(BF16), b.astype(BF16), preferred_element_type=F32)


def _dot_nt(a, b):
    return lax.dot_general(a.astype(BF16), b.astype(BF16), (((1,), (1,)), ((), ())),
                           preferred_element_type=F32)


def _dot_tn(a, b):
    return lax.dot_general(a.astype(BF16), b.astype(BF16), (((0,), (0,)), ((), ())),
                           preferred_element_type=F32)


def _dot3(a, b):
    a_hi = a.astype(BF16)
    b_hi = b.astype(BF16)
    a_lo = (a - a_hi.astype(F32)).astype(BF16)
    b_lo = (b - b_hi.astype(F32)).astype(BF16)
    d = functools.partial(jnp.dot, preferred_element_type=F32)
    return d(a_hi, b_hi) + (d(a_hi, b_lo) + d(a_lo, b_hi))


def _scan_masks(reverse):
    C = SCAN_CHUNK
    P = HEADS_PER_PAIR * C
    row = lax.broadcasted_iota(jnp.int32, (P, LANES), 0)
    col = lax.broadcasted_iota(jnp.int32, (P, LANES), 1)
    tr = row % C
    tc = col % C
    t_i = lax.broadcasted_iota(jnp.int32, (C, C), 0)
    s_i = lax.broadcasted_iota(jnp.int32, (C, C), 1)
    return dict(
        same_head=(row // C) == (col // RW_HEAD_DIM),
        incl=(tr >= tc) if not reverse else (tr <= tc),
        strict=(tr > tc) if not reverse else (tr < tc),
        diag_blk=(tr // SCAN_SUB) == (tc // SCAN_SUB),
        eye=row == col,
        cum_mat=jnp.where((t_i >= s_i) if not reverse else (t_i <= s_i), 1.0, 0.0).astype(F32))


def _scan_chunk(chains, states):
    C = SCAN_CHUNK
    n = range(len(chains))
    mk = [ch[6] for ch in chains]
    cum = [ch[8] for ch in chains]
    pre = []
    for i in n:
        r, lw, k, v, kk, a, m, reverse, _ = chains[i]
        g = jnp.exp(cum[i])
        g_prev = jnp.exp(cum[i] - lw)
        g_inv = jnp.exp(-cum[i])
        g_end = jnp.exp(cum[i][C - 1:C, :] if not reverse else cum[i][0:1, :])
        bt = kk * a * g_inv
        kt = k * g_inv
        stack = lambda x, m=m: jnp.where(m["same_head"], jnp.concatenate([x, x], axis=0), 0.0)
        pre.append(dict(kap=stack(kk * g_prev), rt=stack(r * g), bt=stack(bt), kt=stack(kt),
                        v=stack(v), bh=stack(bt * g_end), kh=stack(kt * g_end), g_end=g_end))
    P = HEADS_PER_PAIR * C
    tri = [_dot_nt(jnp.concatenate([pre[i]["kap"], pre[i]["rt"]], axis=0),
                   jnp.concatenate([pre[i]["bt"], pre[i]["kt"]], axis=0)) for i in n]
    lb = [jnp.where(mk[i]["strict"], tri[i][:P, :P], 0.0) for i in n]
    lk = [jnp.where(mk[i]["strict"], tri[i][:P, P:], 0.0) for i in n]
    pb = [jnp.where(mk[i]["incl"], tri[i][P:, :P], 0.0) for i in n]
    pk = [jnp.where(mk[i]["incl"], tri[i][P:, P:], 0.0) for i in n]
    lkv = [_dot(lk[i], pre[i]["v"]) for i in n]

    d = [jnp.where(mk[i]["diag_blk"], lb[i], 0.0) for i in n]
    inv = [jnp.where(mk[i]["eye"], 1.0, 0.0) - d[i] for i in n]
    p = d
    for _ in range(3):
        p = [_dot(p[i], p[i]) for i in n]
        inv = [inv[i] + _dot(inv[i], p[i]) for i in n]
    x = [_dot(inv[i], jnp.concatenate([lb[i] - d[i], pre[i]["kap"], lkv[i]], axis=1)) for i in n]
    e = [x[i][:, :LANES] for i in n]
    rhs = [x[i][:, LANES:] for i in n]
    e2 = [_dot(e[i], e[i]) for i in n]
    t = [rhs[i] + _dot(e2[i], rhs[i]) for i in n]
    gy = [t[i] - _dot(e[i], t[i]) for i in n]

    bhgy = [_dot_tn(pre[i]["bh"], gy[i]) for i in n]
    m_mat = [jnp.where(mk[i]["eye"], pre[i]["g_end"], 0.0) - bhgy[i][:, :LANES] for i in n]
    n_mat = [_dot_tn(pre[i]["kh"], pre[i]["v"]) - bhgy[i][:, LANES:] for i in n]
    pbgy = [_dot(pb[i], gy[i]) for i in n]
    q = [pre[i]["rt"] - pbgy[i][:, :LANES] for i in n]
    z = [_dot(pk[i], pre[i]["v"]) - pbgy[i][:, LANES:] for i in n]
    o_st = [_dot(q[i], states[i]) + z[i] for i in n]
    new = [_dot3(m_mat[i], states[i]) + n_mat[i] for i in n]
    return [(o_st[i][:C] + o_st[i][C:], new[i]) for i in n]


def _scan_kernel(rf, vf, kkf, lwf, af, kf, rb, vb, kkb, lwb, ab, kb, s0, of, ob, sfin, state):
    c = pl.program_id(1)
    n_pairs = rf.shape[-1] // LANES

    @pl.when(c == 0)
    def _():
        state[...] = s0[0]

    chains, states, outs = [], [], []
    for d, (r_, v_, kk_, lw_, a_, k_, o_) in enumerate(((rf, vf, kkf, lwf, af, kf, of),
                                                        (rb, vb, kkb, lwb, ab, kb, ob))):
        masks = _scan_masks(reverse=(d == 1))
        cum = _dot3(masks["cum_mat"], lw_[...])
        for p in range(n_pairs):
            sl = slice(p * LANES, (p + 1) * LANES)
            chains.append((r_[:, sl], lw_[:, sl], k_[:, sl], v_[:, sl], kk_[:, sl], a_[:, sl],
                           masks, d == 1, cum[:, sl]))
            states.append(state[d, p])
            outs.append((o_, sl, d, p))
    for (o, new), (o_, sl, d, p) in zip(_scan_chunk(chains, states), outs):
        o_[:, sl] = o
        state[d, p] = new

    @pl.when(c == pl.num_programs(1) - 1)
    def _():
        sfin[0] = state[...]


def rwkv_scan(r, v, kk, lw, a, kd, s0_bd, row0, n_seq, seq_len):
    width = r.shape[1]
    n_tok = n_seq * seq_len
    n_pairs = width // LANES
    n_chunk = seq_len // SCAN_CHUNK
    c0 = row0 // SCAN_CHUNK
    blk = (SCAN_CHUNK, width)
    dblk = (None, SCAN_CHUNK, width)
    fwd = lambda s, c: (s * n_chunk + c, 0)
    bwd = lambda s, c: (s * n_chunk + n_chunk - 1 - c, 0)
    fwd_in = lambda s, c: (c0 + s * n_chunk + c, 0)
    bwd_in = lambda s, c: (c0 + s * n_chunk + n_chunk - 1 - c, 0)
    fwd_d = lambda s, c: (0, c0 + s * n_chunk + c, 0)
    bwd_d = lambda s, c: (1, c0 + s * n_chunk + n_chunk - 1 - c, 0)
    sspec = pl.BlockSpec((1, 2, n_pairs, LANES, LANES), lambda s, c: (s, 0, 0, 0, 0))
    in_specs = ([pl.BlockSpec(blk, fwd_in)] * 3 + [pl.BlockSpec(dblk, fwd_d)] * 3
                + [pl.BlockSpec(blk, bwd_in)] * 3 + [pl.BlockSpec(dblk, bwd_d)] * 3 + [sspec])
    out_specs = [pl.BlockSpec(blk, fwd), pl.BlockSpec(blk, bwd), sspec]
    out_shape = [jax.ShapeDtypeStruct((n_tok, width), F32)] * 2 + [
        jax.ShapeDtypeStruct(s0_bd.shape, F32)]
    return pl.pallas_call(
        _scan_kernel,
        grid=(n_seq, n_chunk),
        in_specs=in_specs,
        out_specs=out_specs,
        out_shape=out_shape,
        scratch_shapes=[pltpu.VMEM((2, n_pairs, LANES, LANES), F32)],
        compiler_params=pltpu.CompilerParams(
            dimension_semantics=("parallel", "arbitrary"), vmem_limit_bytes=VMEM_LIMIT),
        name="rwkv_scan",
    )(r, v, kk, lw, a, kd, r, v, kk, lw, a, kd, s0_bd)


MOD_ROWS = 8
ROW_BLOCK = 256
RMS_EPS = 1e-6
LN_EPS = 1e-5
RW_GN_EPS = 64e-5


def _mod_kernel(c_ref, w_ref, b_ref, o_ref):
    c = c_ref[...]
    s = c * jax.nn.sigmoid(c)
    o_ref[...] = _dot(s, w_ref[...]) + b_ref[...]


def modulation(cond, w_mod, b_mod):
    n, d = cond.shape
    cols = w_mod.shape[1]
    blk = 1024
    return pl.pallas_call(
        _mod_kernel,
        grid=(cols // blk,),
        in_specs=[pl.BlockSpec((n, d), lambda j: (0, 0)),
                  pl.BlockSpec((d, blk), lambda j: (0, j)),
                  pl.BlockSpec((1, blk), lambda j: (0, j))],
        out_specs=pl.BlockSpec((n, blk), lambda j: (0, j)),
        out_shape=jax.ShapeDtypeStruct((n, cols), F32),
        name="modulation",
    )(cond, w_mod, b_mod)


def _modulate(x, shift, scale):
    ms = jnp.mean(x * x, axis=-1, keepdims=True)
    return x * lax.rsqrt(ms + RMS_EPS) * (1.0 + scale) + shift


def _inproj_kernel(x_ref, mod_ref, w_ref, *o_refs):
    h = _modulate(x_ref[...], mod_ref[0, 0:1, :], mod_ref[0, 1:2, :]).astype(BF16)
    start = 0
    for o_ref in o_refs:
        n = o_ref.shape[1]
        o_ref[...] = jnp.dot(h, w_ref[:, start:start + n], preferred_element_type=F32)
        start += n


def in_projection(x, mod_blocks, w_in, layer, sizes):
    n_tok, d = x.shape
    row = lambda i: (i, 0)
    return pl.pallas_call(
        _inproj_kernel,
        grid=(n_tok // ROW_BLOCK,),
        in_specs=[pl.BlockSpec((ROW_BLOCK, d), row),
                  pl.BlockSpec((1, MOD_ROWS, d), lambda i: (i, 0, 0)),
                  pl.BlockSpec((None, d, w_in.shape[2]), lambda i: (layer, 0, 0),
                               pipeline_mode=pl.Buffered(1))],
        out_specs=[pl.BlockSpec((ROW_BLOCK, n), row) for n in sizes],
        out_shape=[jax.ShapeDtypeStruct((n_tok, n), F32) for n in sizes],
        compiler_params=pltpu.CompilerParams(
            dimension_semantics=("parallel",), vmem_limit_bytes=VMEM_LIMIT),
        name="in_projection",
    )(x, mod_blocks, w_in)


HEAD_DIM = 64


def _group_ones():
    r = lax.broadcasted_iota(jnp.int32, (LANES, LANES), 0) // HEAD_DIM
    c = lax.broadcasted_iota(jnp.int32, (LANES, LANES), 1) // HEAD_DIM
    return jnp.where(r == c, 1.0, 0.0).astype(BF16)


def _group_sum(x, ones):
    hi = x.astype(BF16)
    lo = (x - hi.astype(F32)).astype(BF16)
    d = functools.partial(jnp.dot, preferred_element_type=F32)
    tiles = [slice(s, s + LANES) for s in range(0, x.shape[1], LANES)]
    sums = [d(hi[:, t], ones) + d(lo[:, t], ones) for t in tiles]
    return sums[0] if len(sums) == 1 else jnp.concatenate(sums, axis=1)


RW_WIDTH = 512
RW_COLS = 3 * RW_WIDTH + 3 * LANES
RW_VEC_ROWS = 8


def _rwkv_prep_kernel(z_ref, hp_ref, hn_ref, mu_ref, vec_ref, w2_ref, a2_ref, g2_ref,
                      r_ref, v_ref, kk_ref, lw_ref, a_ref, kd_ref, bonus_ref, gate_ref):
    z = z_ref[...]
    rb = z.shape[0]
    row = lax.broadcasted_iota(jnp.int32, (rb, 1), 0)
    z_prev = jnp.where(row == 0, hp_ref[0], pltpu.roll(z, 1, 0))
    z_next = jnp.where(row == rb - 1, hn_ref[0], pltpu.roll(z, rb - 1, 0))
    zs = z + mu_ref[0:1, :] * (z_prev - z) + mu_ref[1:2, :] * (z_next - z)
    w = RW_WIDTH
    r, k, v = zs[:, :w], zs[:, w:2 * w], zs[:, 2 * w:3 * w]
    zw = zs[:, 3 * w:3 * w + LANES]
    za = zs[:, 3 * w + LANES:3 * w + 2 * LANES]
    zg = zs[:, 3 * w + 2 * LANES:]
    ones = _group_ones()
    kk = k * vec_ref[0:1, :]
    kk = kk * lax.rsqrt(_group_sum(kk * kk, ones) + 1e-12)
    r_ref[...] = r
    v_ref[...] = v
    kk_ref[...] = kk
    tz = jnp.tanh(zw)
    for d in range(2):
        w_raw = vec_ref[3 + d:4 + d, :] + _dot(tz, w2_ref[d])
        lw_ref[d] = -math.exp(-0.5) * jax.nn.sigmoid(w_raw)
        a = jax.nn.sigmoid(vec_ref[5 + d:6 + d, :] + _dot(za, a2_ref[d]))
        a_ref[d] = a
        kd_ref[d] = k * (1.0 + (a - 1.0) * vec_ref[1:2, :])
    bonus_ref[...] = _group_sum(r * k * vec_ref[2:3, :], ones) * v
    gate_ref[...] = _dot(jax.nn.sigmoid(zg), g2_ref[...])


def rwkv_prep(z_rw, halo_prev, halo_next, mu, vecs, w2_pad, a2_pad, g2):
    n_tok = z_rw.shape[0]
    w = RW_WIDTH
    row = lambda i: (i, 0)
    const2 = lambda i: (0, 0)
    const3 = lambda i: (0, 0, 0)
    halo = pl.BlockSpec((1, 1, RW_COLS), lambda i: (i, 0, 0))
    one = pl.BlockSpec((ROW_BLOCK, w), row)
    two = pl.BlockSpec((2, ROW_BLOCK, w), lambda i: (0, i, 0))
    s1 = jax.ShapeDtypeStruct((n_tok, w), F32)
    s2 = jax.ShapeDtypeStruct((2, n_tok, w), F32)
    return pl.pallas_call(
        _rwkv_prep_kernel,
        grid=(n_tok // ROW_BLOCK,),
        in_specs=[pl.BlockSpec((ROW_BLOCK, RW_COLS), row), halo, halo,
                  pl.BlockSpec(mu.shape, const2), pl.BlockSpec(vecs.shape, const2),
                  pl.BlockSpec(w2_pad.shape, const3), pl.BlockSpec(a2_pad.shape, const3),
                  pl.BlockSpec(g2.shape, const2)],
        out_specs=[one, one, one, two, two, two, one, one],
        out_shape=[s1, s1, s1, s2, s2, s2, s1, s1],
        compiler_params=pltpu.CompilerParams(
            dimension_semantics=("parallel",), vmem_limit_bytes=VMEM_LIMIT),
        name="rwkv_prep",
    )(z_rw, halo_prev, halo_next, mu, vecs, w2_pad, a2_pad, g2)


CONV_WIDTH = 512
CONV_KERNEL = 31
CONV_PAD = 16
CONV_ROWS = 64


def _conv_kernel(z_ref, w_ref, vec_ref, o_ref, pad_ref):
    t = z_ref.shape[0]
    cw = CONV_WIDTH
    half = CONV_KERNEL // 2
    zeros = jnp.zeros((CONV_PAD, cw), F32)
    pad_ref[0:CONV_PAD, :] = zeros
    pad_ref[CONV_PAD + t:CONV_PAD + t + CONV_PAD, :] = zeros
    pad_ref[CONV_PAD:CONV_PAD + t, :] = z_ref[:, :cw] * jax.nn.sigmoid(z_ref[:, cw:])

    def chunk(i, carry):
        base = pl.multiple_of(i * CONV_ROWS, CONV_ROWS)
        window = pad_ref[pl.ds(base, CONV_ROWS + 2 * CONV_PAD), :]
        span = CONV_ROWS + 2 * CONV_PAD - SUBLANES
        shifted = [window[rho:rho + span] for rho in range(SUBLANES)]
        acc = jnp.zeros((CONV_ROWS, cw), F32)
        for j in range(CONV_KERNEL):
            lo = CONV_PAD - half + j
            tile = (lo // SUBLANES) * SUBLANES
            acc = acc + w_ref[j:j + 1, :] * shifted[lo % SUBLANES][tile:tile + CONV_ROWS]
        u = acc + vec_ref[0:1, :]
        mu = jnp.mean(u, axis=-1, keepdims=True)
        cen = u - mu
        var = jnp.mean(cen * cen, axis=-1, keepdims=True)
        y = cen * lax.rsqrt(var + LN_EPS) * vec_ref[1:2, :] + vec_ref[2:3, :]
        o_ref[pl.ds(base, CONV_ROWS), :] = y * jax.nn.sigmoid(y)
        return carry

    lax.fori_loop(0, t // CONV_ROWS, chunk, 0)


def conv_module(z_cv, conv_w, vecs, row0, n_seq, seq_len):
    cw = CONV_WIDTH
    blk0 = row0 // seq_len
    return pl.pallas_call(
        _conv_kernel,
        grid=(n_seq,),
        in_specs=[pl.BlockSpec((seq_len, 2 * cw), lambda i: (blk0 + i, 0)),
                  pl.BlockSpec(conv_w.shape, lambda i: (0, 0)),
                  pl.BlockSpec(vecs.shape, lambda i: (0, 0))],
        out_specs=pl.BlockSpec((seq_len, cw), lambda i: (i, 0)),
        out_shape=jax.ShapeDtypeStruct((n_seq * seq_len, cw), F32),
        scratch_shapes=[pltpu.VMEM((seq_len + 2 * CONV_PAD, cw), F32)],
        compiler_params=pltpu.CompilerParams(
            dimension_semantics=("parallel",), vmem_limit_bytes=VMEM_LIMIT),
        name="conv_module",
    )(z_cv, conv_w, vecs)


ATT_Q_BLOCK = 256
ATT_HEADS = 8
ATT_KV_HEADS = 2
ATT_GROUP = ATT_HEADS // ATT_KV_HEADS


def _rot_partner():
    r = lax.broadcasted_iota(jnp.int32, (LANES, LANES), 0)
    c = lax.broadcasted_iota(jnp.int32, (LANES, LANES), 1)
    lo = (r % 32) < 16
    return jnp.where((c == r + 16) & lo, 1.0, jnp.where((c == r - 16) & ~lo, -1.0, 0.0)).astype(BF16)


def _attn_kernel(*refs, rotary):
    if rotary:
        (zq_ref, zkv_ref, nq_ref, nk_ref, cosq_ref, sinq_ref, cosk_ref, sink_ref, ck_ref, cv_ref,
         o_ref) = refs
    else:
        zq_ref, zkv_ref, nq_ref, nk_ref, o_ref, kn_ref = refs
    tq = zq_ref.shape[0]
    ones = _group_ones()
    lane = lax.broadcasted_iota(jnp.int32, (1, LANES), 1)
    first = lane < HEAD_DIM
    row_head = lax.broadcasted_iota(jnp.int32, (2 * tq, LANES), 0) // tq
    own = row_head == (lax.broadcasted_iota(jnp.int32, (2 * tq, LANES), 1) // HEAD_DIM)

    def norm(x, g):
        return x * lax.rsqrt(_group_sum(x * x, ones) * (1.0 / HEAD_DIM) + RMS_EPS) * g

    def rope(x, cos, sin):
        return x * cos + jnp.dot(x.astype(BF16), _rot_partner(), preferred_element_type=F32) * sin

    def dup(x):
        sw = pltpu.roll(x, HEAD_DIM, 1)
        return jnp.where(first, x, sw), jnp.where(first, sw, x)

    k = norm(zkv_ref[:, :LANES], nk_ref[...])
    v = zkv_ref[:, LANES:]
    if rotary:
        k = rope(k, cosk_ref[...], sink_ref[...])
        ck, cv = dup(ck_ref[...]), dup(cv_ref[...])
    else:
        kn_ref[...] = k
    kd, vd = dup(k), dup(v)
    for pair in range(ATT_HEADS // 2):
        g = pair // (ATT_GROUP // 2)
        q = norm(zq_ref[:, pair * LANES:(pair + 1) * LANES], nq_ref[...])
        if rotary:
            q = rope(q, cosq_ref[...], sinq_ref[...])
        q = q * (HEAD_DIM ** -0.5)
        qs = jnp.where(own, jnp.concatenate([q, q], axis=0), 0.0)
        s = _dot_nt(qs, kd[g])
        m = jnp.max(s, axis=-1, keepdims=True)
        if rotary:
            s2 = _dot_nt(qs, ck[g])
            m = jnp.maximum(m, jnp.max(s2, axis=-1, keepdims=True))
            p2 = jnp.exp(s2 - m)
        p = jnp.exp(s - m)
        den = jnp.sum(p, axis=-1, keepdims=True)
        o = _dot(p, vd[g])
        if rotary:
            den = den + jnp.sum(p2, axis=-1, keepdims=True)
            o = o + _dot(p2, cv[g])
        o = o / den
        o_ref[:, pair * LANES:(pair + 1) * LANES] = jnp.where(first, o[:tq], o[tq:])


def attention(z_q, z_kv, nq, nk, row0, n_seq, seq_len, rope=None, cache=None):
    qb = min(ATT_Q_BLOCK, seq_len)
    nqb = seq_len // qb
    qblk0 = row0 // qb
    sblk0 = row0 // seq_len
    wq = z_q.shape[1]
    in_specs = [pl.BlockSpec((qb, wq), lambda b, t: (qblk0 + b * nqb + t, 0)),
                pl.BlockSpec((seq_len, 2 * LANES), lambda b, t: (sblk0 + b, 0)),
                pl.BlockSpec((1, LANES), lambda b, t: (0, 0)),
                pl.BlockSpec((1, LANES), lambda b, t: (0, 0))]
    args = [z_q, z_kv, nq, nk]
    out_q = pl.BlockSpec((qb, wq), lambda b, t: (b * nqb + t, 0))
    y_shape = jax.ShapeDtypeStruct((n_seq * seq_len, wq), F32)
    if rope is None:
        out_specs = [out_q, pl.BlockSpec((seq_len, LANES), lambda b, t: (b, 0))]
        out_shape = [y_shape, jax.ShapeDtypeStruct((n_seq * seq_len, LANES), F32)]
    else:
        past = cache[0].shape[0] // n_seq
        in_specs += [pl.BlockSpec((qb, LANES), lambda b, t: (t, 0))] * 2
        in_specs += [pl.BlockSpec((seq_len, LANES), lambda b, t: (0, 0))] * 2
        in_specs += [pl.BlockSpec((past, LANES), lambda b, t: (b, 0))] * 2
        args += [rope[0], rope[1], rope[0], rope[1], cache[0], cache[1]]
        out_specs = out_q
        out_shape = y_shape
    return pl.pallas_call(
        functools.partial(_attn_kernel, rotary=rope is not None),
        grid=(n_seq, nqb),
        in_specs=in_specs,
        out_specs=out_specs,
        out_shape=out_shape,
        compiler_params=pltpu.CompilerParams(
            dimension_semantics=("parallel", "arbitrary"), vmem_limit_bytes=VMEM_LIMIT),
        name="attention",
    )(*args)


def _merge_kernel(x_ref, mod_ref, bonus_ref, gate_ref, zg_ref,
                  ofc_ref, ofl_ref, obc_ref, obl_ref, cvc_ref, cvl_ref, atc_ref, atl_ref,
                  gn_ref, wr_ref, wc_ref, wa_ref, wo_ref, wq_ref, xo_ref, h2_ref, q_ref, *, n_ctx_blocks):
    d = x_ref.shape[1]
    zg = zg_ref[...]
    is_ctx = pl.program_id(0) < n_ctx_blocks
    pick = lambda c_ref, l_ref: jnp.where(is_ctx, c_ref[...], l_ref[...])
    ones = _group_ones()
    o = pick(ofc_ref, ofl_ref) + pick(obc_ref, obl_ref)
    cen = o - _group_sum(o, ones) * (1.0 / HEAD_DIM)
    var = _group_sum(cen * cen, ones) * (1.0 / HEAD_DIM)
    o = cen * lax.rsqrt(var + RW_GN_EPS) * gn_ref[0:1, :] + gn_ref[1:2, :]
    y_rw = (o + bonus_ref[...]) * gate_ref[...]
    merged = (jax.nn.sigmoid(zg[:, :d]) * _dot(y_rw, wr_ref[...])
              + jax.nn.sigmoid(zg[:, d:2 * d]) * _dot(pick(cvc_ref, cvl_ref), wc_ref[...])
              + jax.nn.sigmoid(zg[:, 2 * d:]) * _dot(pick(atc_ref, atl_ref), wa_ref[...]))
    x = x_ref[...] + mod_ref[0, 2:3, :] * _dot(merged, wo_ref[...])
    xo_ref[...] = x
    h2 = _modulate(x, mod_ref[0, 3:4, :], mod_ref[0, 4:5, :]).astype(BF16)
    h2_ref[...] = h2
    q_ref[...] = jnp.dot(h2, wq_ref[...], preferred_element_type=F32)


def merge_project(x, mod_blocks, bonus, gate, z_gate, o_f, o_b, y_cv, y_at, gn, w_r, w_c, w_a, w_o, w_q):
    n_tok, d = x.shape
    nc = o_f[0].shape[0] // ROW_BLOCK
    row = lambda i: (i, 0)
    const = lambda i: (0, 0)
    ctx_row = lambda i: (jnp.minimum(i, nc - 1), 0)
    lat_row = lambda i: (jnp.maximum(i - nc, 0), 0)
    acts = [bonus, gate, z_gate]
    weights = [gn, w_r, w_c, w_a, w_o, w_q]
    pair_specs, pair_args = [], []
    for c_arr, l_arr in (o_f, o_b, y_cv, y_at):
        pair_specs += [pl.BlockSpec((ROW_BLOCK, c_arr.shape[1]), ctx_row),
                       pl.BlockSpec((ROW_BLOCK, l_arr.shape[1]), lat_row)]
        pair_args += [c_arr, l_arr]
    return pl.pallas_call(
        functools.partial(_merge_kernel, n_ctx_blocks=nc),
        grid=(n_tok // ROW_BLOCK,),
        in_specs=[pl.BlockSpec((ROW_BLOCK, d), row),
                  pl.BlockSpec((1, MOD_ROWS, d), lambda i: (i, 0, 0))]
                 + [pl.BlockSpec((ROW_BLOCK, a.shape[1]), row) for a in acts]
                 + pair_specs
                 + [pl.BlockSpec(w.shape, const) for w in weights],
        out_specs=[pl.BlockSpec((ROW_BLOCK, d), row), pl.BlockSpec((ROW_BLOCK, d), row),
                   pl.BlockSpec((ROW_BLOCK, w_q.shape[1]), row)],
        out_shape=[jax.ShapeDtypeStruct((n_tok, d), F32), jax.ShapeDtypeStruct((n_tok, d), BF16),
                   jax.ShapeDtypeStruct((n_tok, w_q.shape[1]), F32)],
        compiler_params=pltpu.CompilerParams(
            dimension_semantics=("parallel",), vmem_limit_bytes=VMEM_LIMIT),
        name="merge_project",
    )(x, mod_blocks, *acts, *pair_args, *weights)


PEER_HEADS = 8
PEER_N_KEYS = 128
PEER_HALF = 64
PEER_TOPK = 16
PEER_SEL_BLOCK = 128
PEER_RELAYOUT = 16
NEG = -1e30


def _dot3_nt(a, b):
    a_hi = a.astype(BF16)
    b_hi = b.astype(BF16)
    a_lo = (a - a_hi.astype(F32)).astype(BF16)
    b_lo = (b - b_hi.astype(F32)).astype(BF16)
    d = lambda x, y: lax.dot_general(x, y, (((1,), (1,)), ((), ())), preferred_element_type=F32)
    return d(a_hi, b_hi) + (d(a_hi, b_lo) + d(a_lo, b_hi))


def _extract_max(s, code):
    vals = [s[i:i + SUBLANES] for i in range(0, s.shape[0], SUBLANES)]
    codes = [code[i:i + SUBLANES] for i in range(0, s.shape[0], SUBLANES)]
    while len(vals) > 1:
        nv, nc = [], []
        for i in range(0, len(vals) - 1, 2):
            keep = vals[i] >= vals[i + 1]
            nv.append(jnp.maximum(vals[i], vals[i + 1]))
            nc.append(jnp.where(keep, codes[i], codes[i + 1]))
        if len(vals) % 2:
            nv.append(vals[-1])
            nc.append(codes[-1])
        vals, codes = nv, nc
    m = jnp.max(vals[0], axis=0, keepdims=True)
    pos = jnp.min(jnp.where(vals[0] == m, codes[0], 1e9), axis=0, keepdims=True)
    return m, pos, jnp.where(code == pos, NEG, s)


def _pair_candidates(a0, a1):
    k = PEER_TOPK
    sub = lax.broadcasted_iota(jnp.int32, (k, a0.shape[1]), 0).astype(F32)
    sub8 = sub[:8]
    vals = [a0[0:1] + a1]
    codes = [sub]
    for r0 in range(1, 8):
        limit = k // (r0 + 1)
        vals.append(jnp.where(sub8 < limit, a0[r0:r0 + 1] + a1[:8], NEG))
        codes.append(sub8 + float(r0 * k))
    vals.append(a0[8:] + a1[0:1])
    codes.append((sub8 + 8.0) * float(k))
    return jnp.concatenate(vals, axis=0), jnp.concatenate(codes, axis=0)


def _peer_select_kernel(q_ref, keys_ref, w_ref, top_v, top_i, slot_i, slot_j, slot_g,
                        rows_i, rows_j, rows_g, w3_ref):
    k = PEER_TOPK
    ct = q_ref.shape[0]
    key_code = lax.broadcasted_iota(jnp.int32, (PEER_N_KEYS, ct), 0).astype(F32)
    rank = lax.broadcasted_iota(jnp.int32, (k, ct), 0).astype(F32)
    for h in range(PEER_HEADS):
        st = _dot3_nt(keys_ref[h], q_ref[:, h * LANES:(h + 1) * LANES])
        s = [st[:PEER_N_KEYS], st[PEER_N_KEYS:]]
        for it in range(k):
            for p in range(2):
                m, pos, s[p] = _extract_max(s[p], key_code)
                top_v[h, p, it:it + 1, :] = m
                top_i[h, p, it:it + 1, :] = pos
    for h0 in range(0, PEER_HEADS, 2):
        heads = (h0, h0 + 1)
        cand, code, best, z = {}, {}, {}, {}
        for h in heads:
            a0, a1 = top_v[h, 0], top_v[h, 1]
            cand[h], code[h] = _pair_candidates(a0, a1)
            best[h] = a0[0:1] + a1[0:1]
            z[h] = jnp.zeros_like(best[h])
        for it in range(k):
            for h in heads:
                m, pos, cand[h] = _extract_max(cand[h], code[h])
                r0 = jnp.floor(pos * (1.0 / k))
                r1 = pos - r0 * k
                e = jnp.exp(m - best[h])
                z[h] = z[h] + e
                slot = h * k + it
                slot_i[slot:slot + 1, :] = jnp.sum(jnp.where(rank == r0, top_i[h, 0], 0.0), axis=0,
                                                   keepdims=True)
                slot_j[slot:slot + 1, :] = jnp.sum(jnp.where(rank == r1, top_i[h, 1], 0.0), axis=0,
                                                   keepdims=True)
                slot_g[slot:slot + 1, :] = e
        for h in heads:
            slot_g[h * k:(h + 1) * k, :] = slot_g[h * k:(h + 1) * k, :] / z[h]

    rows_i[...] = slot_i[...].T
    rows_j[...] = slot_j[...].T
    rows_g[...] = slot_g[...].T
    n = PEER_N_KEYS
    sub = lax.broadcasted_iota(jnp.int32, (n, n), 0).astype(F32)

    for c in range(ct):
        a_t = jnp.where(sub == rows_i[c:c + 1, :], rows_g[c:c + 1, :], 0.0)
        b_t = jnp.where(sub == rows_j[c:c + 1, :], 1.0, 0.0)
        w3_ref[c] = _dot_nt(a_t, b_t).astype(BF16)

    g = PEER_RELAYOUT
    r_out = lax.broadcasted_iota(jnp.int32, (g * g, g * g), 0)
    r_in = lax.broadcasted_iota(jnp.int32, (g * g, g * g), 1)
    perm = jnp.where(r_in == (r_out % g) * g + r_out // g, 1.0, 0.0).astype(BF16)

    for ib in range(n // g):
        for t in range(ct // g):
            z = w3_ref[t * g:(t + 1) * g, ib * g:(ib + 1) * g, :].reshape(g * g, n)
            y = jnp.dot(perm, z, preferred_element_type=F32)
            for kk in range(g):
                col = (ib * g + kk) * n
                w_ref[t * g:(t + 1) * g, col:col + n] = y[kk * g:(kk + 1) * g].astype(BF16)


def peer_select(q, keys_padded):
    n_tok, qd = q.shape
    ct = PEER_SEL_BLOCK
    n = PEER_N_KEYS
    slots = PEER_HEADS * PEER_TOPK
    return pl.pallas_call(
        _peer_select_kernel,
        grid=(n_tok // ct,),
        in_specs=[pl.BlockSpec((ct, qd), lambda i: (i, 0)),
                  pl.BlockSpec(keys_padded.shape, lambda i: (0, 0, 0))],
        out_specs=pl.BlockSpec((ct, n * n), lambda i: (i, 0)),
        out_shape=jax.ShapeDtypeStruct((n_tok, n * n), BF16),
        scratch_shapes=[pltpu.VMEM((PEER_HEADS, 2, PEER_TOPK, ct), F32),
                        pltpu.VMEM((PEER_HEADS, 2, PEER_TOPK, ct), F32),
                        pltpu.VMEM((slots, ct), F32), pltpu.VMEM((slots, ct), F32),
                        pltpu.VMEM((slots, ct), F32),
                        pltpu.VMEM((ct, slots), F32), pltpu.VMEM((ct, slots), F32),
                        pltpu.VMEM((ct, slots), F32),
                        pltpu.VMEM((ct, n, n), BF16)],
        compiler_params=pltpu.CompilerParams(
            dimension_semantics=("parallel",), vmem_limit_bytes=VMEM_LIMIT),
        name="peer_select",
    )(q, keys_padded)


PEER_TOKEN_BLOCK = 1024
PEER_EXPERT_BLOCK = 1024


def _peer_kernel(h_ref, w_ref, u_ref, v_ref, x_ref, mod_ref, o_ref, acc_ref):
    e = pl.program_id(1)

    @pl.when(e == 0)
    def _():
        acc_ref[...] = jnp.zeros_like(acc_ref)

    s = _dot_nt(h_ref[...], u_ref[...])
    act = 0.5 * s * (1.0 + lax.erf(s * (2.0 ** -0.5)))
    acc_ref[...] += jnp.dot(act.astype(BF16) * w_ref[...], v_ref[...], preferred_element_type=F32)

    @pl.when(e == pl.num_programs(1) - 1)
    def _():
        for j in range(mod_ref.shape[0]):
            rows = slice(j * ROW_BLOCK, (j + 1) * ROW_BLOCK)
            o_ref[rows, :] = x_ref[rows, :] + mod_ref[j, 5:6, :] * acc_ref[rows, :]


def peer_dense(h2, w_sel, u_tabs, v_tabs, layer, x, mod_blocks):
    n_tok, d = x.shape
    n_exp = u_tabs.shape[1]
    tb, eb = PEER_TOKEN_BLOCK, PEER_EXPERT_BLOCK
    tok = lambda i, e: (i, 0)
    return pl.pallas_call(
        _peer_kernel,
        grid=(n_tok // tb, n_exp // eb),
        in_specs=[pl.BlockSpec((tb, d), tok),
                  pl.BlockSpec((tb, eb), lambda i, e: (i, e)),
                  pl.BlockSpec((None, eb, d), lambda i, e: (layer, e, 0)),
                  pl.BlockSpec((None, eb, d), lambda i, e: (layer, e, 0)),
                  pl.BlockSpec((tb, d), tok),
                  pl.BlockSpec((tb // ROW_BLOCK, MOD_ROWS, d), lambda i, e: (i, 0, 0))],
        out_specs=pl.BlockSpec((tb, d), tok),
        out_shape=jax.ShapeDtypeStruct((n_tok, d), F32),
        scratch_shapes=[pltpu.VMEM((tb, d), F32)],
        compiler_params=pltpu.CompilerParams(
            dimension_semantics=("parallel", "arbitrary"), vmem_limit_bytes=VMEM_LIMIT),
        name="peer_dense",
    )(h2, w_sel, u_tabs, v_tabs, x, mod_blocks)


RW_HEADS = RW_WIDTH // RW_HEAD_DIM
ROT_HALF = HEAD_DIM // 2
ROPE_THETA = 10000.0
GRID_W = 64


def _bd_from_states(s):
    a = jnp.swapaxes(s, -1, -2)
    n, d, h, m, _ = a.shape
    a = a.reshape(n, d, h // 2, 2, m, m)
    z = jnp.zeros_like(a[:, :, :, 0])
    top = jnp.concatenate([a[:, :, :, 0], z], axis=-1)
    bot = jnp.concatenate([z, a[:, :, :, 1]], axis=-1)
    return jnp.concatenate([top, bot], axis=-2)


def _states_from_bd(bd):
    n, d, p, _, _ = bd.shape
    m = RW_HEAD_DIM
    a = jnp.stack([bd[:, :, :, :m, :m], bd[:, :, :, m:, m:]], axis=3).reshape(n, d, 2 * p, m, m)
    return jnp.swapaxes(a, -1, -2)


def _rope_tables(t):
    rows = t // GRID_W
    row, col = jnp.meshgrid(jnp.arange(rows), jnp.arange(GRID_W), indexing='ij')
    inv = ROPE_THETA ** (-jnp.arange(0, ROT_HALF, 2, dtype=F32) / ROT_HALF)
    ang_r = row.reshape(-1, 1).astype(F32) * inv
    ang_c = col.reshape(-1, 1).astype(F32) * inv
    cr, sr, cc, sc = jnp.cos(ang_r), jnp.sin(ang_r), jnp.cos(ang_c), jnp.sin(ang_c)
    cos = jnp.concatenate([cr, cr, cc, cc], axis=-1)
    sin = jnp.concatenate([sr, sr, sc, sc], axis=-1)
    return jnp.tile(cos, (1, LANES // HEAD_DIM)), jnp.tile(sin, (1, LANES // HEAD_DIM))


def _pad_rows(w, top):
    z = jnp.zeros_like(w)
    return jnp.concatenate([w, z] if top else [z, w], axis=0)


def kernel(x_prompt, x_sample, c, cache_k, cache_v, state_wkv, c_ctx, w_mod, b_mod, w_in, shift_mu_prev, shift_mu_next, rw_w0, rw_w2, rw_a0, rw_a2, rw_g2, rw_k_k, rw_k_a, rw_r_k, rw_gn_w, rw_gn_b, w_out_rwkv, conv_w, conv_b, conv_ln_w, conv_ln_b, w_out_conv, q_norm, k_norm, w_out_attn, w_o, peer_wq, peer_keys, peer_u, peer_v):
    nb, seq, d = x_prompt.shape
    db, dseq, _ = x_sample.shape
    depth = w_in.shape[0]
    n_ctx = nb * seq
    n_lat = db * dseq
    n_tok = n_ctx + n_lat
    x = jnp.concatenate([x_prompt.reshape(n_ctx, d), x_sample.reshape(n_lat, d)], axis=0)

    cond = jnp.concatenate([c_ctx[None], c, jnp.zeros((MOD_ROWS - 1 - db, d), F32)], axis=0)
    blk_cond = jnp.concatenate([jnp.zeros((n_ctx // ROW_BLOCK,), jnp.int32),
                                1 + jnp.arange(n_lat // ROW_BLOCK, dtype=jnp.int32) // (dseq // ROW_BLOCK)])
    n_blk = n_tok // ROW_BLOCK
    blk_row = jnp.arange(n_blk) * ROW_BLOCK
    blk_pos = jnp.where(blk_row < n_ctx, blk_row % seq, (blk_row - n_ctx) % dseq)
    blk_len = jnp.where(blk_row < n_ctx, seq, dseq)
    starts_seq = (blk_pos == 0)[:, None]
    ends_seq = (blk_pos + ROW_BLOCK == blk_len)[:, None]
    rope = _rope_tables(dseq)
    in_sizes = (RW_COLS, 2 * CONV_WIDTH, ATT_HEADS * HEAD_DIM, 2 * ATT_KV_HEADS * HEAD_DIM, 3 * d)
    past = cache_k.shape[2]

    w_in_bf, u_bf, v_bf = w_in.astype(BF16), peer_u.astype(BF16), peer_v.astype(BF16)
    ctx_k, ctx_v, ctx_s = [], [], []
    for l in range(depth):
        mod = modulation(cond, w_mod[l], b_mod[l][None])
        mod = mod.reshape(MOD_ROWS, 6, d)[blk_cond]
        mod_blocks = jnp.concatenate([mod, jnp.zeros((mod.shape[0], MOD_ROWS - 6, d), F32)], axis=1)

        z_rw, z_cv, z_q, z_kv, z_gate = in_projection(x, mod_blocks, w_in_bf, l, in_sizes)

        zb = z_rw.reshape(n_blk, ROW_BLOCK, RW_COLS)
        halo_prev = jnp.where(starts_seq, 0.0, jnp.roll(zb[:, -1], 1, axis=0))[:, None]
        halo_next = jnp.where(ends_seq, 0.0, jnp.roll(zb[:, 0], -1, axis=0))[:, None]
        mu = jnp.stack([shift_mu_prev[l], shift_mu_next[l]])
        vecs = jnp.stack([rw_k_k[l], rw_k_a[l], rw_r_k[l], rw_w0[l, 0], rw_w0[l, 1], rw_a0[l, 0],
                          rw_a0[l, 1], jnp.zeros((RW_WIDTH,), F32)])
        w2_pad = jnp.stack([_pad_rows(rw_w2[l, 0], True), _pad_rows(rw_w2[l, 1], False)]).astype(BF16)
        a2_pad = jnp.stack([_pad_rows(rw_a2[l, 0], True), _pad_rows(rw_a2[l, 1], False)]).astype(BF16)
        r, v, kk, lw, aa, kd, bonus, gate = rwkv_prep(z_rw, halo_prev, halo_next, mu, vecs, w2_pad, a2_pad,
                                                      rw_g2[l].astype(BF16))
        s0_ctx = jnp.zeros((nb, 2, RW_HEADS // 2, LANES, LANES), F32)
        of_c, ob_c, sfin = rwkv_scan(r, v, kk, lw, aa, kd, s0_ctx, 0, nb, seq)
        of_l, ob_l, _ = rwkv_scan(r, v, kk, lw, aa, kd, _bd_from_states(state_wkv[:, l]), n_ctx, db, dseq)
        ctx_s.append(_states_from_bd(sfin))

        cvec = jnp.stack([conv_b[l], conv_ln_w[l], conv_ln_b[l]] + [jnp.zeros((CONV_WIDTH,), F32)] * 5)
        y_cv = (conv_module(z_cv, conv_w[l], cvec, 0, nb, seq),
                conv_module(z_cv, conv_w[l], cvec, n_ctx, db, dseq))

        nq = jnp.tile(q_norm[l], LANES // HEAD_DIM)[None]
        nk = jnp.tile(k_norm[l], LANES // HEAD_DIM)[None]
        y_c, k_ctx = attention(z_q, z_kv, nq, nk, 0, nb, seq)
        y_l = attention(z_q, z_kv, nq, nk, n_ctx, db, dseq, rope=rope,
                        cache=(cache_k[:, l].reshape(db * past, LANES), cache_v[:, l].reshape(db * past, LANES)))
        ctx_k.append(k_ctx.reshape(nb, seq, ATT_KV_HEADS, HEAD_DIM))
        ctx_v.append(z_kv[:n_ctx, LANES:].reshape(nb, seq, ATT_KV_HEADS, HEAD_DIM))

        x, h2, pq = merge_project(x, mod_blocks, bonus, gate, z_gate,
                                  (of_c, of_l), (ob_c, ob_l), y_cv, (y_c, y_l),
                                  jnp.stack([rw_gn_w[l], rw_gn_b[l]]),
                                  w_out_rwkv[l].astype(BF16), w_out_conv[l].astype(BF16),
                                  w_out_attn[l].astype(BF16), w_o[l].astype(BF16), peer_wq[l].astype(BF16))

        kz = jnp.zeros_like(peer_keys[l][:, 0])
        keys_padded = jnp.concatenate([jnp.concatenate([peer_keys[l][:, 0], kz], axis=-1),
                                       jnp.concatenate([kz, peer_keys[l][:, 1]], axis=-1)], axis=1)
        w_sel = peer_select(pq, keys_padded)
        x = peer_dense(h2, w_sel, u_bf, v_bf, l, x, mod_blocks)

    new_cache_k = jnp.stack(ctx_k, axis=1)
    new_cache_v = jnp.stack(ctx_v, axis=1)
    new_state = jnp.stack(ctx_s, axis=1)
    return (x[:n_ctx].reshape(nb, seq, d), x[n_ctx:].reshape(db, dseq, d),
            new_cache_k, new_cache_v, new_state)
```

```python
import functools
import math

import jax
import jax.numpy as jnp
from jax import lax
from jax.experimental import pallas as pl
from jax.experimental.pallas import tpu as pltpu

F32 = jnp.float32
BF16 = jnp.bfloat16

LANES = 128
SUBLANES = 8
SCAN_CHUNK = 64
SCAN_SUB = 16
RW_HEAD_DIM = 64
HEADS_PER_PAIR = LANES // RW_HEAD_DIM
VMEM_LIMIT = 56 * 1024 * 1024


def _dot(a, b):
    return jnp.dot(a.astype(BF16), b.astype(BF16), preferred_element_type=F32)


def _dot_nt(a, b):
    return lax.dot_general(a.astype(BF16), b.astype(BF16), (((1,), (1,)), ((), ())),
                           preferred_element_type=F32)


def _dot_tn(a, b):
    return lax.dot_general(a.astype(BF16), b.astype(BF16), (((0,), (0,)), ((), ())),
                           preferred_element_type=F32)


def _dot3(a, b):
    a_hi = a.astype(BF16)
    b_hi = b.astype(BF16)
    a_lo = (a - a_hi.astype(F32)).astype(BF16)
    b_lo = (b - b_hi.astype(F32)).astype(BF16)
    d = functools.partial(jnp.dot, preferred_element_type=F32)
    return d(a_hi, b_hi) + (d(a_hi, b_lo) + d(a_lo, b_hi))


def _scan_masks(reverse):
    C = SCAN_CHUNK
    P = HEADS_PER_PAIR * C
    row = lax.broadcasted_iota(jnp.int32, (P, LANES), 0)
    col = lax.broadcasted_iota(jnp.int32, (P, LANES), 1)
    tr = row % C
    tc = col % C
    t_i = lax.broadcasted_iota(jnp.int32, (C, C), 0)
    s_i = lax.broadcasted_iota(jnp.int32, (C, C), 1)
    return dict(
        same_head=(row // C) == (col // RW_HEAD_DIM),
        incl=(tr >= tc) if not reverse else (tr <= tc),
        strict=(tr > tc) if not reverse else (tr < tc),
        diag_blk=(tr // SCAN_SUB) == (tc // SCAN_SUB),
        eye=row == col,
        cum_mat=jnp.where((t_i >= s_i) if not reverse else (t_i <= s_i), 1.0, 0.0).astype(F32))


def _scan_chunk(chains, states):
    C = SCAN_CHUNK
    n = range(len(chains))
    mk = [ch[6] for ch in chains]
    cum = [ch[8] for ch in chains]
    pre = []
    for i in n:
        r, lw, k, v, kk, a, m, reverse, _ = chains[i]
        g = jnp.exp(cum[i])
        g_prev = jnp.exp(cum[i] - lw)
        g_inv = jnp.exp(-cum[i])
        g_end = jnp.exp(cum[i][C - 1:C, :] if not reverse else cum[i][0:1, :])
        bt = kk * a * g_inv
        kt = k * g_inv
        stack = lambda x, m=m: jnp.where(m["same_head"], jnp.concatenate([x, x], axis=0), 0.0)
        pre.append(dict(kap=stack(kk * g_prev), rt=stack(r * g), bt=stack(bt), kt=stack(kt),
                        v=stack(v), bh=stack(bt * g_end), kh=stack(kt * g_end), g_end=g_end))
    P = HEADS_PER_PAIR * C
    tri = [_dot_nt(jnp.concatenate([pre[i]["kap"], pre[i]["rt"]], axis=0),
                   jnp.concatenate([pre[i]["bt"], pre[i]["kt"]], axis=0)) for i in n]
    lb = [jnp.where(mk[i]["strict"], tri[i][:P, :P], 0.0) for i in n]
    lk = [jnp.where(mk[i]["strict"], tri[i][:P, P:], 0.0) for i in n]
    pb = [jnp.where(mk[i]["incl"], tri[i][P:, :P], 0.0) for i in n]
    pk = [jnp.where(mk[i]["incl"], tri[i][P:, P:], 0.0) for i in n]
    lkv = [_dot(lk[i], pre[i]["v"]) for i in n]

    d = [jnp.where(mk[i]["diag_blk"], lb[i], 0.0) for i in n]
    inv = [jnp.where(mk[i]["eye"], 1.0, 0.0) - d[i] for i in n]
    p = d
    for _ in range(3):
        p = [_dot(p[i], p[i]) for i in n]
        inv = [inv[i] + _dot(inv[i], p[i]) for i in n]
    x = [_dot(inv[i], jnp.concatenate([lb[i] - d[i], pre[i]["kap"], lkv[i]], axis=1)) for i in n]
    e = [x[i][:, :LANES] for i in n]
    rhs = [x[i][:, LANES:] for i in n]
    e2 = [_dot(e[i], e[i]) for i in n]
    t = [rhs[i] + _dot(e2[i], rhs[i]) for i in n]
    gy = [t[i] - _dot(e[i], t[i]) for i in n]

    bhgy = [_dot_tn(pre[i]["bh"], gy[i]) for i in n]
    m_mat = [jnp.where(mk[i]["eye"], pre[i]["g_end"], 0.0) - bhgy[i][:, :LANES] for i in n]
    n_mat = [_dot_tn(pre[i]["kh"], pre[i]["v"]) - bhgy[i][:, LANES:] for i in n]
    pbgy = [_dot(pb[i], gy[i]) for i in n]
    q = [pre[i]["rt"] - pbgy[i][:, :LANES] for i in n]
    z = [_dot(pk[i], pre[i]["v"]) - pbgy[i][:, LANES:] for i in n]
    o_st = [_dot(q[i], states[i]) + z[i] for i in n]
    new = [_dot3(m_mat[i], states[i]) + n_mat[i] for i in n]
    return [(o_st[i][:C] + o_st[i][C:], new[i]) for i in n]


def _scan_kernel(rf, vf, kkf, lwf, af, kf, rb, vb, kkb, lwb, ab, kb, s0, of, ob, sfin, state):
    c = pl.program_id(1)
    n_pairs = rf.shape[-1] // LANES

    @pl.when(c == 0)
    def _():
        state[...] = s0[0]

    chains, states, outs = [], [], []
    for d, (r_, v_, kk_, lw_, a_, k_, o_) in enumerate(((rf, vf, kkf, lwf, af, kf, of),
                                                        (rb, vb, kkb, lwb, ab, kb, ob))):
        masks = _scan_masks(reverse=(d == 1))
        cum = _dot3(masks["cum_mat"], lw_[...])
        for p in range(n_pairs):
            sl = slice(p * LANES, (p + 1) * LANES)
            chains.append((r_[:, sl], lw_[:, sl], k_[:, sl], v_[:, sl], kk_[:, sl], a_[:, sl],
                           masks, d == 1, cum[:, sl]))
            states.append(state[d, p])
            outs.append((o_, sl, d, p))
    for (o, new), (o_, sl, d, p) in zip(_scan_chunk(chains, states), outs):
        o_[:, sl] = o
        state[d, p] = new

    @pl.when(c == pl.num_programs(1) - 1)
    def _():
        sfin[0] = state[...]


def rwkv_scan(r, v, kk, lw, a, kd, s0_bd, row0, n_seq, seq_len):
    width = r.shape[1]
    n_tok = n_seq * seq_len
    n_pairs = width // LANES
    n_chunk = seq_len // SCAN_CHUNK
    c0 = row0 // SCAN_CHUNK
    blk = (SCAN_CHUNK, width)
    dblk = (None, SCAN_CHUNK, width)
    fwd = lambda s, c: (s * n_chunk + c, 0)
    bwd = lambda s, c: (s * n_chunk + n_chunk - 1 - c, 0)
    fwd_in = lambda s, c: (c0 + s * n_chunk + c, 0)
    bwd_in = lambda s, c: (c0 + s * n_chunk + n_chunk - 1 - c, 0)
    fwd_d = lambda s, c: (0, c0 + s * n_chunk + c, 0)
    bwd_d = lambda s, c: (1, c0 + s * n_chunk + n_chunk - 1 - c, 0)
    sspec = pl.BlockSpec((1, 2, n_pairs, LANES, LANES), lambda s, c: (s, 0, 0, 0, 0))
    in_specs = ([pl.BlockSpec(blk, fwd_in)] * 3 + [pl.BlockSpec(dblk, fwd_d)] * 3
                + [pl.BlockSpec(blk, bwd_in)] * 3 + [pl.BlockSpec(dblk, bwd_d)] * 3 + [sspec])
    out_specs = [pl.BlockSpec(blk, fwd), pl.BlockSpec(blk, bwd), sspec]
    out_shape = [jax.ShapeDtypeStruct((n_tok, width), F32)] * 2 + [
        jax.ShapeDtypeStruct(s0_bd.shape, F32)]
    return pl.pallas_call(
        _scan_kernel,
        grid=(n_seq, n_chunk),
        in_specs=in_specs,
        out_specs=out_specs,
        out_shape=out_shape,
        scratch_shapes=[pltpu.VMEM((2, n_pairs, LANES, LANES), F32)],
        compiler_params=pltpu.CompilerParams(
            dimension_semantics=("parallel", "arbitrary"), vmem_limit_bytes=VMEM_LIMIT),
        name="rwkv_scan",
    )(r, v, kk, lw, a, kd, r, v, kk, lw, a, kd, s0_bd)


MOD_ROWS = 8
ROW_BLOCK = 256
RMS_EPS = 1e-6
LN_EPS = 1e-5
RW_GN_EPS = 64e-5


def _mod_kernel(c_ref, w_ref, b_ref, o_ref):
    c = c_ref[...]
    s = c * jax.nn.sigmoid(c)
    o_ref[...] = _dot(s, w_ref[...]) + b_ref[...]


def modulation(cond, w_mod, b_mod):
    n, d = cond.shape
    cols = w_mod.shape[1]
    blk = 1024
    return pl.pallas_call(
        _mod_kernel,
        grid=(cols // blk,),
        in_specs=[pl.BlockSpec((n, d), lambda j: (0, 0)),
                  pl.BlockSpec((d, blk), lambda j: (0, j)),
                  pl.BlockSpec((1, blk), lambda j: (0, j))],
        out_specs=pl.BlockSpec((n, blk), lambda j: (0, j)),
        out_shape=jax.ShapeDtypeStruct((n, cols), F32),
        name="modulation",
    )(cond, w_mod, b_mod)


def _modulate(x, shift, scale):
    ms = jnp.mean(x * x, axis=-1, keepdims=True)
    return x * lax.rsqrt(ms + RMS_EPS) * (1.0 + scale) + shift


def _inproj_kernel(x_ref, mod_ref, w_ref, *o_refs):
    h = _modulate(x_ref[...], mod_ref[0, 0:1, :], mod_ref[0, 1:2, :]).astype(BF16)
    start = 0
    for o_ref in o_refs:
        n = o_ref.shape[1]
        o_ref[...] = jnp.dot(h, w_ref[:, start:start + n], preferred_element_type=F32)
        start += n


def in_projection(x, mod_blocks, w_in, layer, sizes):
    n_tok, d = x.shape
    row = lambda i: (i, 0)
    return pl.pallas_call(
        _inproj_kernel,
        grid=(n_tok // ROW_BLOCK,),
        in_specs=[pl.BlockSpec((ROW_BLOCK, d), row),
                  pl.BlockSpec((1, MOD_ROWS, d), lambda i: (i, 0, 0)),
                  pl.BlockSpec((None, d, w_in.shape[2]), lambda i: (layer, 0, 0),
                               pipeline_mode=pl.Buffered(1))],
        out_specs=[pl.BlockSpec((ROW_BLOCK, n), row) for n in sizes],
        out_shape=[jax.ShapeDtypeStruct((n_tok, n), F32) for n in sizes],
        compiler_params=pltpu.CompilerParams(
            dimension_semantics=("parallel",), vmem_limit_bytes=VMEM_LIMIT),
        name="in_projection",
    )(x, mod_blocks, w_in)


HEAD_DIM = 64


def _group_ones():
    r = lax.broadcasted_iota(jnp.int32, (LANES, LANES), 0) // HEAD_DIM
    c = lax.broadcasted_iota(jnp.int32, (LANES, LANES), 1) // HEAD_DIM
    return jnp.where(r == c, 1.0, 0.0).astype(BF16)


def _group_sum(x, ones):
    hi = x.astype(BF16)
    lo = (x - hi.astype(F32)).astype(BF16)
    d = functools.partial(jnp.dot, preferred_element_type=F32)
    tiles = [slice(s, s + LANES) for s in range(0, x.shape[1], LANES)]
    sums = [d(hi[:, t], ones) + d(lo[:, t], ones) for t in tiles]
    return sums[0] if len(sums) == 1 else jnp.concatenate(sums, axis=1)


RW_WIDTH = 512
RW_COLS = 3 * RW_WIDTH + 3 * LANES


def _rwkv_prep_kernel(z_ref, hp_ref, hn_ref, mu_ref, vec_ref, w2_ref, a2_ref, g2_ref,
                      r_ref, v_ref, kk_ref, lw_ref, a_ref, kd_ref, bonus_ref, gate_ref):
    z = z_ref[...]
    rb = z.shape[0]
    row = lax.broadcasted_iota(jnp.int32, (rb, 1), 0)
    z_prev = jnp.where(row == 0, hp_ref[0], pltpu.roll(z, 1, 0))
    z_next = jnp.where(row == rb - 1, hn_ref[0], pltpu.roll(z, rb - 1, 0))
    zs = z + mu_ref[0:1, :] * (z_prev - z) + mu_ref[1:2, :] * (z_next - z)
    w = RW_WIDTH
    r, k, v = zs[:, :w], zs[:, w:2 * w], zs[:, 2 * w:3 * w]
    zw = zs[:, 3 * w:3 * w + LANES]
    za = zs[:, 3 * w + LANES:3 * w + 2 * LANES]
    zg = zs[:, 3 * w + 2 * LANES:]
    ones = _group_ones()
    kk = k * vec_ref[0:1, :]
    kk = kk * lax.rsqrt(_group_sum(kk * kk, ones) + 1e-12)
    r_ref[...] = r
    v_ref[...] = v
    kk_ref[...] = kk
    tz = jnp.tanh(zw)
    for d in range(2):
        w_raw = vec_ref[3 + d:4 + d, :] + _dot(tz, w2_ref[d])
        lw_ref[d] = -math.exp(-0.5) * jax.nn.sigmoid(w_raw)
        a = jax.nn.sigmoid(vec_ref[5 + d:6 + d, :] + _dot(za, a2_ref[d]))
        a_ref[d] = a
        kd_ref[d] = k * (1.0 + (a - 1.0) * vec_ref[1:2, :])
    bonus_ref[...] = _group_sum(r * k * vec_ref[2:3, :], ones) * v
    gate_ref[...] = _dot(jax.nn.sigmoid(zg), g2_ref[...])


def rwkv_prep(z_rw, halo_prev, halo_next, mu, vecs, w2_pad, a2_pad, g2):
    n_tok = z_rw.shape[0]
    w = RW_WIDTH
    row = lambda i: (i, 0)
    const2 = lambda i: (0, 0)
    const3 = lambda i: (0, 0, 0)
    halo = pl.BlockSpec((1, 1, RW_COLS), lambda i: (i, 0, 0))
    one = pl.BlockSpec((ROW_BLOCK, w), row)
    two = pl.BlockSpec((2, ROW_BLOCK, w), lambda i: (0, i, 0))
    s1 = jax.ShapeDtypeStruct((n_tok, w), F32)
    s2 = jax.ShapeDtypeStruct((2, n_tok, w), F32)
    return pl.pallas_call(
        _rwkv_prep_kernel,
        grid=(n_tok // ROW_BLOCK,),
        in_specs=[pl.BlockSpec((ROW_BLOCK, RW_COLS), row), halo, halo,
                  pl.BlockSpec(mu.shape, const2), pl.BlockSpec(vecs.shape, const2),
                  pl.BlockSpec(w2_pad.shape, const3), pl.BlockSpec(a2_pad.shape, const3),
                  pl.BlockSpec(g2.shape, const2)],
        out_specs=[one, one, one, two, two, two, one, one],
        out_shape=[s1, s1, s1, s2, s2, s2, s1, s1],
        compiler_params=pltpu.CompilerParams(
            dimension_semantics=("parallel",), vmem_limit_bytes=VMEM_LIMIT),
        name="rwkv_prep",
    )(z_rw, halo_prev, halo_next, mu, vecs, w2_pad, a2_pad, g2)


CONV_WIDTH = 512
CONV_KERNEL = 31
CONV_PAD = 16
CONV_ROWS = 64


def _conv_kernel(z_ref, w_ref, vec_ref, o_ref, pad_ref):
    t = z_ref.shape[0]
    cw = CONV_WIDTH
    half = CONV_KERNEL // 2
    zeros = jnp.zeros((CONV_PAD, cw), F32)
    pad_ref[0:CONV_PAD, :] = zeros
    pad_ref[CONV_PAD + t:CONV_PAD + t + CONV_PAD, :] = zeros
    pad_ref[CONV_PAD:CONV_PAD + t, :] = z_ref[:, :cw] * jax.nn.sigmoid(z_ref[:, cw:])

    def chunk(i, carry):
        base = pl.multiple_of(i * CONV_ROWS, CONV_ROWS)
        window = pad_ref[pl.ds(base, CONV_ROWS + 2 * CONV_PAD), :]
        span = CONV_ROWS + 2 * CONV_PAD - SUBLANES
        shifted = [window[rho:rho + span] for rho in range(SUBLANES)]
        acc = jnp.zeros((CONV_ROWS, cw), F32)
        for j in range(CONV_KERNEL):
            lo = CONV_PAD - half + j
            tile = (lo // SUBLANES) * SUBLANES
            acc = acc + w_ref[j:j + 1, :] * shifted[lo % SUBLANES][tile:tile + CONV_ROWS]
        u = acc + vec_ref[0:1, :]
        mu = jnp.mean(u, axis=-1, keepdims=True)
        cen = u - mu
        var = jnp.mean(cen * cen, axis=-1, keepdims=True)
        y = cen * lax.rsqrt(var + LN_EPS) * vec_ref[1:2, :] + vec_ref[2:3, :]
        o_ref[pl.ds(base, CONV_ROWS), :] = y * jax.nn.sigmoid(y)
        return carry

    lax.fori_loop(0, t // CONV_ROWS, chunk, 0)


def conv_module(z_cv, conv_w, vecs, row0, n_seq, seq_len):
    cw = CONV_WIDTH
    blk0 = row0 // seq_len
    return pl.pallas_call(
        _conv_kernel,
        grid=(n_seq,),
        in_specs=[pl.BlockSpec((seq_len, 2 * cw), lambda i: (blk0 + i, 0)),
                  pl.BlockSpec(conv_w.shape, lambda i: (0, 0)),
                  pl.BlockSpec(vecs.shape, lambda i: (0, 0))],
        out_specs=pl.BlockSpec((seq_len, cw), lambda i: (i, 0)),
        out_shape=jax.ShapeDtypeStruct((n_seq * seq_len, cw), F32),
        scratch_shapes=[pltpu.VMEM((seq_len + 2 * CONV_PAD, cw), F32)],
        compiler_params=pltpu.CompilerParams(
            dimension_semantics=("parallel",), vmem_limit_bytes=VMEM_LIMIT),
        name="conv_module",
    )(z_cv, conv_w, vecs)


ATT_Q_BLOCK = 256
ATT_HEADS = 8
ATT_KV_HEADS = 2
ATT_GROUP = ATT_HEADS // ATT_KV_HEADS


def _rot_partner():
    r = lax.broadcasted_iota(jnp.int32, (LANES, LANES), 0)
    c = lax.broadcasted_iota(jnp.int32, (LANES, LANES), 1)
    lo = (r % 32) < 16
    return jnp.where((c == r + 16) & lo, 1.0, jnp.where((c == r - 16) & ~lo, -1.0, 0.0)).astype(BF16)


def _attn_kernel(*refs, rotary):
    if rotary:
        (zq_ref, zkv_ref, nq_ref, nk_ref, cosq_ref, sinq_ref, cosk_ref, sink_ref, ck_ref, cv_ref,
         o_ref) = refs
    else:
        zq_ref, zkv_ref, nq_ref, nk_ref, o_ref, kn_ref = refs
    tq = zq_ref.shape[0]
    ones = _group_ones()
    lane = lax.broadcasted_iota(jnp.int32, (1, LANES), 1)
    first = lane < HEAD_DIM
    row_head = lax.broadcasted_iota(jnp.int32, (2 * tq, LANES), 0) // tq
    own = row_head == (lax.broadcasted_iota(jnp.int32, (2 * tq, LANES), 1) // HEAD_DIM)

    def norm(x, g):
        return x * lax.rsqrt(_group_sum(x * x, ones) * (1.0 / HEAD_DIM) + RMS_EPS) * g

    def rope(x, cos, sin):
        return x * cos + jnp.dot(x.astype(BF16), _rot_partner(), preferred_element_type=F32) * sin

    def dup(x):
        sw = pltpu.roll(x, HEAD_DIM, 1)
        return jnp.where(first, x, sw), jnp.where(first, sw, x)

    k = norm(zkv_ref[:, :LANES], nk_ref[...])
    v = zkv_ref[:, LANES:]
    if rotary:
        k = rope(k, cosk_ref[...], sink_ref[...])
        ck, cv = dup(ck_ref[...]), dup(cv_ref[...])
    else:
        kn_ref[...] = k
    kd, vd = dup(k), dup(v)
    for pair in range(ATT_HEADS // 2):
        g = pair // (ATT_GROUP // 2)
        q = norm(zq_ref[:, pair * LANES:(pair + 1) * LANES], nq_ref[...])
        if rotary:
            q = rope(q, cosq_ref[...], sinq_ref[...])
        q = q * (HEAD_DIM ** -0.5)
        qs = jnp.where(own, jnp.concatenate([q, q], axis=0), 0.0)
        s = _dot_nt(qs, kd[g])
        m = jnp.max(s, axis=-1, keepdims=True)
        if rotary:
            s2 = _dot_nt(qs, ck[g])
            m = jnp.maximum(m, jnp.max(s2, axis=-1, keepdims=True))
            p2 = jnp.exp(s2 - m)
        p = jnp.exp(s - m)
        den = jnp.sum(p, axis=-1, keepdims=True)
        o = _dot(p, vd[g])
        if rotary:
            den = den + jnp.sum(p2, axis=-1, keepdims=True)
            o = o + _dot(p2, cv[g])
        o = o / den
        o_ref[:, pair * LANES:(pair + 1) * LANES] = jnp.where(first, o[:tq], o[tq:])


def attention(z_q, z_kv, nq, nk, row0, n_seq, seq_len, rope=None, cache=None):
    qb = min(ATT_Q_BLOCK, seq_len)
    nqb = seq_len // qb
    qblk0 = row0 // qb
    sblk0 = row0 // seq_len
    wq = z_q.shape[1]
    in_specs = [pl.BlockSpec((qb, wq), lambda b, t: (qblk0 + b * nqb + t, 0)),
                pl.BlockSpec((seq_len, 2 * LANES), lambda b, t: (sblk0 + b, 0)),
                pl.BlockSpec((1, LANES), lambda b, t: (0, 0)),
                pl.BlockSpec((1, LANES), lambda b, t: (0, 0))]
    args = [z_q, z_kv, nq, nk]
    out_q = pl.BlockSpec((qb, wq), lambda b, t: (b * nqb + t, 0))
    y_shape = jax.ShapeDtypeStruct((n_seq * seq_len, wq), F32)
    if rope is None:
        out_specs = [out_q, pl.BlockSpec((seq_len, LANES), lambda b, t: (b, 0))]
        out_shape = [y_shape, jax.ShapeDtypeStruct((n_seq * seq_len, LANES), F32)]
    else:
        past = cache[0].shape[0] // n_seq
        in_specs += [pl.BlockSpec((qb, LANES), lambda b, t: (t, 0))] * 2
        in_specs += [pl.BlockSpec((seq_len, LANES), lambda b, t: (0, 0))] * 2
        in_specs += [pl.BlockSpec((past, LANES), lambda b, t: (b, 0))] * 2
        args += [rope[0], rope[1], rope[0], rope[1], cache[0], cache[1]]
        out_specs = out_q
        out_shape = y_shape
    return pl.pallas_call(
        functools.partial(_attn_kernel, rotary=rope is not None),
        grid=(n_seq, nqb),
        in_specs=in_specs,
        out_specs=out_specs,
        out_shape=out_shape,
        compiler_params=pltpu.CompilerParams(
            dimension_semantics=("parallel", "arbitrary"), vmem_limit_bytes=VMEM_LIMIT),
        name="attention",
    )(*args)


def _merge_kernel(x_ref, mod_ref, bonus_ref, gate_ref, zg_ref,
                  ofc_ref, ofl_ref, obc_ref, obl_ref, cvc_ref, cvl_ref, atc_ref, atl_ref,
                  gn_ref, wr_ref, wc_ref, wa_ref, wo_ref, wq_ref, xo_ref, h2_ref, q_ref, *, n_ctx_blocks):
    d = x_ref.shape[1]
    zg = zg_ref[...]
    is_ctx = pl.program_id(0) < n_ctx_blocks
    pick = lambda c_ref, l_ref: jnp.where(is_ctx, c_ref[...], l_ref[...])
    ones = _group_ones()
    o = pick(ofc_ref, ofl_ref) + pick(obc_ref, obl_ref)
    cen = o - _group_sum(o, ones) * (1.0 / HEAD_DIM)
    var = _group_sum(cen * cen, ones) * (1.0 / HEAD_DIM)
    o = cen * lax.rsqrt(var + RW_GN_EPS) * gn_ref[0:1, :] + gn_ref[1:2, :]
    y_rw = (o + bonus_ref[...]) * gate_ref[...]
    merged = (jax.nn.sigmoid(zg[:, :d]) * _dot(y_rw, wr_ref[...])
              + jax.nn.sigmoid(zg[:, d:2 * d]) * _dot(pick(cvc_ref, cvl_ref), wc_ref[...])
              + jax.nn.sigmoid(zg[:, 2 * d:]) * _dot(pick(atc_ref, atl_ref), wa_ref[...]))
    x = x_ref[...] + mod_ref[0, 2:3, :] * _dot(merged, wo_ref[...])
    xo_ref[...] = x
    h2 = _modulate(x, mod_ref[0, 3:4, :], mod_ref[0, 4:5, :]).astype(BF16)
    h2_ref[...] = h2
    q_ref[...] = jnp.dot(h2, wq_ref[...], preferred_element_type=F32)


def merge_project(x, mod_blocks, bonus, gate, z_gate, o_f, o_b, y_cv, y_at, gn, w_r, w_c, w_a, w_o, w_q):
    n_tok, d = x.shape
    nc = o_f[0].shape[0] // ROW_BLOCK
    row = lambda i: (i, 0)
    const = lambda i: (0, 0)
    ctx_row = lambda i: (jnp.minimum(i, nc - 1), 0)
    lat_row = lambda i: (jnp.maximum(i - nc, 0), 0)
    acts = [bonus, gate, z_gate]
    weights = [gn, w_r, w_c, w_a, w_o, w_q]
    pair_specs, pair_args = [], []
    for c_arr, l_arr in (o_f, o_b, y_cv, y_at):
        pair_specs += [pl.BlockSpec((ROW_BLOCK, c_arr.shape[1]), ctx_row),
                       pl.BlockSpec((ROW_BLOCK, l_arr.shape[1]), lat_row)]
        pair_args += [c_arr, l_arr]
    return pl.pallas_call(
        functools.partial(_merge_kernel, n_ctx_blocks=nc),
        grid=(n_tok // ROW_BLOCK,),
        in_specs=[pl.BlockSpec((ROW_BLOCK, d), row),
                  pl.BlockSpec((1, MOD_ROWS, d), lambda i: (i, 0, 0))]
                 + [pl.BlockSpec((ROW_BLOCK, a.shape[1]), row) for a in acts]
                 + pair_specs
                 + [pl.BlockSpec(w.shape, const) for w in weights],
        out_specs=[pl.BlockSpec((ROW_BLOCK, d), row), pl.BlockSpec((ROW_BLOCK, d), row),
                   pl.BlockSpec((ROW_BLOCK, w_q.shape[1]), row)],
        out_shape=[jax.ShapeDtypeStruct((n_tok, d), F32), jax.ShapeDtypeStruct((n_tok, d), BF16),
                   jax.ShapeDtypeStruct((n_tok, w_q.shape[1]), F32)],
        compiler_params=pltpu.CompilerParams(
            dimension_semantics=("parallel",), vmem_limit_bytes=VMEM_LIMIT),
        name="merge_project",
    )(x, mod_blocks, *acts, *pair_args, *weights)


PEER_HEADS = 8
PEER_N_KEYS = 128
PEER_TOPK = 16
PEER_SEL_BLOCK = 128
PEER_RELAYOUT = 16
NEG = -1e30


def _dot3_nt(a, b):
    a_hi = a.astype(BF16)
    b_hi = b.astype(BF16)
    a_lo = (a - a_hi.astype(F32)).astype(BF16)
    b_lo = (b - b_hi.astype(F32)).astype(BF16)
    d = lambda x, y: lax.dot_general(x, y, (((1,), (1,)), ((), ())), preferred_element_type=F32)
    return d(a_hi, b_hi) + (d(a_hi, b_lo) + d(a_lo, b_hi))


def _extract_max(s, code):
    vals = [s[i:i + SUBLANES] for i in range(0, s.shape[0], SUBLANES)]
    codes = [code[i:i + SUBLANES] for i in range(0, s.shape[0], SUBLANES)]
    while len(vals) > 1:
        nv, nc = [], []
        for i in range(0, len(vals) - 1, 2):
            keep = vals[i] >= vals[i + 1]
            nv.append(jnp.maximum(vals[i], vals[i + 1]))
            nc.append(jnp.where(keep, codes[i], codes[i + 1]))
        if len(vals) % 2:
            nv.append(vals[-1])
            nc.append(codes[-1])
        vals, codes = nv, nc
    m = jnp.max(vals[0], axis=0, keepdims=True)
    pos = jnp.min(jnp.where(vals[0] == m, codes[0], 1e9), axis=0, keepdims=True)
    return m, pos, jnp.where(code == pos, NEG, s)


def _pair_candidates(a0, a1):
    k = PEER_TOPK
    sub = lax.broadcasted_iota(jnp.int32, (k, a0.shape[1]), 0).astype(F32)
    sub8 = sub[:8]
    vals = [a0[0:1] + a1]
    codes = [sub]
    for r0 in range(1, 8):
        limit = k // (r0 + 1)
        vals.append(jnp.where(sub8 < limit, a0[r0:r0 + 1] + a1[:8], NEG))
        codes.append(sub8 + float(r0 * k))
    vals.append(a0[8:] + a1[0:1])
    codes.append((sub8 + 8.0) * float(k))
    return jnp.concatenate(vals, axis=0), jnp.concatenate(codes, axis=0)


def _peer_select_kernel(q_ref, keys_ref, w_ref, top_v, top_i, slot_i, slot_j, slot_g,
                        rows_i, rows_j, rows_g, w3_ref):
    k = PEER_TOPK
    ct = q_ref.shape[0]
    key_code = lax.broadcasted_iota(jnp.int32, (PEER_N_KEYS, ct), 0).astype(F32)
    rank = lax.broadcasted_iota(jnp.int32, (k, ct), 0).astype(F32)
    for h in range(PEER_HEADS):
        st = _dot3_nt(keys_ref[h], q_ref[:, h * LANES:(h + 1) * LANES])
        s = [st[:PEER_N_KEYS], st[PEER_N_KEYS:]]
        for it in range(k):
            for p in range(2):
                m, pos, s[p] = _extract_max(s[p], key_code)
                top_v[h, p, it:it + 1, :] = m
                top_i[h, p, it:it + 1, :] = pos
    for h0 in range(0, PEER_HEADS, 2):
        heads = (h0, h0 + 1)
        cand, code, best, z = {}, {}, {}, {}
        for h in heads:
            a0, a1 = top_v[h, 0], top_v[h, 1]
            cand[h], code[h] = _pair_candidates(a0, a1)
            best[h] = a0[0:1] + a1[0:1]
            z[h] = jnp.zeros_like(best[h])
        for it in range(k):
            for h in heads:
                m, pos, cand[h] = _extract_max(cand[h], code[h])
                r0 = jnp.floor(pos * (1.0 / k))
                r1 = pos - r0 * k
                e = jnp.exp(m - best[h])
                z[h] = z[h] + e
                slot = h * k + it
                slot_i[slot:slot + 1, :] = jnp.sum(jnp.where(rank == r0, top_i[h, 0], 0.0), axis=0,
                                                   keepdims=True)
                slot_j[slot:slot + 1, :] = jnp.sum(jnp.where(rank == r1, top_i[h, 1], 0.0), axis=0,
                                                   keepdims=True)
                slot_g[slot:slot + 1, :] = e
        for h in heads:
            slot_g[h * k:(h + 1) * k, :] = slot_g[h * k:(h + 1) * k, :] / z[h]

    rows_i[...] = slot_i[...].T
    rows_j[...] = slot_j[...].T
    rows_g[...] = slot_g[...].T
    n = PEER_N_KEYS
    sub = lax.broadcasted_iota(jnp.int32, (n, n), 0).astype(F32)

    for c in range(ct):
        a_t = jnp.where(sub == rows_i[c:c + 1, :], rows_g[c:c + 1, :], 0.0)
        b_t = jnp.where(sub == rows_j[c:c + 1, :], 1.0, 0.0)
        w3_ref[c] = _dot_nt(a_t, b_t).astype(BF16)

    g = PEER_RELAYOUT
    r_out = lax.broadcasted_iota(jnp.int32, (g * g, g * g), 0)
    r_in = lax.broadcasted_iota(jnp.int32, (g * g, g * g), 1)
    perm = jnp.where(r_in == (r_out % g) * g + r_out // g, 1.0, 0.0).astype(BF16)

    for ib in range(n // g):
        for t in range(ct // g):
            z = w3_ref[t * g:(t + 1) * g, ib * g:(ib + 1) * g, :].reshape(g * g, n)
            y = jnp.dot(perm, z, preferred_element_type=F32)
            for kk in range(g):
                col = (ib * g + kk) * n
                w_ref[t * g:(t + 1) * g, col:col + n] = y[kk * g:(kk + 1) * g].astype(BF16)


def peer_select(q, keys_padded):
    n_tok, qd = q.shape
    ct = PEER_SEL_BLOCK
    n = PEER_N_KEYS
    slots = PEER_HEADS * PEER_TOPK
    return pl.pallas_call(
        _peer_select_kernel,
        grid=(n_tok // ct,),
        in_specs=[pl.BlockSpec((ct, qd), lambda i: (i, 0)),
                  pl.BlockSpec(keys_padded.shape, lambda i: (0, 0, 0))],
        out_specs=pl.BlockSpec((ct, n * n), lambda i: (i, 0)),
        out_shape=jax.ShapeDtypeStruct((n_tok, n * n), BF16),
        scratch_shapes=[pltpu.VMEM((PEER_HEADS, 2, PEER_TOPK, ct), F32),
                        pltpu.VMEM((PEER_HEADS, 2, PEER_TOPK, ct), F32),
                        pltpu.VMEM((slots, ct), F32), pltpu.VMEM((slots, ct), F32),
                        pltpu.VMEM((slots, ct), F32),
                        pltpu.VMEM((ct, slots), F32), pltpu.VMEM((ct, slots), F32),
                        pltpu.VMEM((ct, slots), F32),
                        pltpu.VMEM((ct, n, n), BF16)],
        compiler_params=pltpu.CompilerParams(
            dimension_semantics=("parallel",), vmem_limit_bytes=VMEM_LIMIT),
        name="peer_select",
    )(q, keys_padded)


PEER_TOKEN_BLOCK = 1024
PEER_EXPERT_BLOCK = 1024


def _peer_kernel(h_ref, w_ref, u_ref, v_ref, x_ref, mod_ref, o_ref, acc_ref):
    e = pl.program_id(1)

    @pl.when(e == 0)
    def _():
        acc_ref[...] = jnp.zeros_like(acc_ref)

    s = _dot_nt(h_ref[...], u_ref[...])
    act = 0.5 * s * (1.0 + lax.erf(s * (2.0 ** -0.5)))
    acc_ref[...] += jnp.dot(act.astype(BF16) * w_ref[...], v_ref[...], preferred_element_type=F32)

    @pl.when(e == pl.num_programs(1) - 1)
    def _():
        for j in range(mod_ref.shape[0]):
            rows = slice(j * ROW_BLOCK, (j + 1) * ROW_BLOCK)
            o_ref[rows, :] = x_ref[rows, :] + mod_ref[j, 5:6, :] * acc_ref[rows, :]


def peer_dense(h2, w_sel, u_tabs, v_tabs, layer, x, mod_blocks):
    n_tok, d = x.shape
    n_exp = u_tabs.shape[1]
    tb, eb = PEER_TOKEN_BLOCK, PEER_EXPERT_BLOCK
    tok = lambda i, e: (i, 0)
    return pl.pallas_call(
        _peer_kernel,
        grid=(n_tok // tb, n_exp // eb),
        in_specs=[pl.BlockSpec((tb, d), tok),
                  pl.BlockSpec((tb, eb), lambda i, e: (i, e)),
                  pl.BlockSpec((None, eb, d), lambda i, e: (layer, e, 0)),
                  pl.BlockSpec((None, eb, d), lambda i, e: (layer, e, 0)),
                  pl.BlockSpec((tb, d), tok),
                  pl.BlockSpec((tb // ROW_BLOCK, MOD_ROWS, d), lambda i, e: (i, 0, 0))],
        out_specs=pl.BlockSpec((tb, d), tok),
        out_shape=jax.ShapeDtypeStruct((n_tok, d), F32),
        scratch_shapes=[pltpu.VMEM((tb, d), F32)],
        compiler_params=pltpu.CompilerParams(
            dimension_semantics=("parallel", "arbitrary"), vmem_limit_bytes=VMEM_LIMIT),
        name="peer_dense",
    )(h2, w_sel, u_tabs, v_tabs, x, mod_blocks)


RW_HEADS = RW_WIDTH // RW_HEAD_DIM
ROT_HALF = HEAD_DIM // 2
ROPE_THETA = 10000.0
GRID_W = 64


def _bd_from_states(s):
    a = jnp.swapaxes(s, -1, -2)
    n, d, h, m, _ = a.shape
    a = a.reshape(n, d, h // 2, 2, m, m)
    z = jnp.zeros_like(a[:, :, :, 0])
    top = jnp.concatenate([a[:, :, :, 0], z], axis=-1)
    bot = jnp.concatenate([z, a[:, :, :, 1]], axis=-1)
    return jnp.concatenate([top, bot], axis=-2)


def _states_from_bd(bd):
    n, d, p, _, _ = bd.shape
    m = RW_HEAD_DIM
    a = jnp.stack([bd[:, :, :, :m, :m], bd[:, :, :, m:, m:]], axis=3).reshape(n, d, 2 * p, m, m)
    return jnp.swapaxes(a, -1, -2)


def _rope_tables(t):
    rows = t // GRID_W
    row, col = jnp.meshgrid(jnp.arange(rows), jnp.arange(GRID_W), indexing='ij')
    inv = ROPE_THETA ** (-jnp.arange(0, ROT_HALF, 2, dtype=F32) / ROT_HALF)
    ang_r = row.reshape(-1, 1).astype(F32) * inv
    ang_c = col.reshape(-1, 1).astype(F32) * inv
    cr, sr, cc, sc = jnp.cos(ang_r), jnp.sin(ang_r), jnp.cos(ang_c), jnp.sin(ang_c)
    cos = jnp.concatenate([cr, cr, cc, cc], axis=-1)
    sin = jnp.concatenate([sr, sr, sc, sc], axis=-1)
    return jnp.tile(cos, (1, LANES // HEAD_DIM)), jnp.tile(sin, (1, LANES // HEAD_DIM))


def _pad_rows(w, top):
    z = jnp.zeros_like(w)
    return jnp.concatenate([w, z] if top else [z, w], axis=0)


def kernel(x_prompt, x_sample, c, cache_k, cache_v, state_wkv, c_ctx, w_mod, b_mod, w_in, shift_mu_prev, shift_mu_next, rw_w0, rw_w2, rw_a0, rw_a2, rw_g2, rw_k_k, rw_k_a, rw_r_k, rw_gn_w, rw_gn_b, w_out_rwkv, conv_w, conv_b, conv_ln_w, conv_ln_b, w_out_conv, q_norm, k_norm, w_out_attn, w_o, peer_wq, peer_keys, peer_u, peer_v):
    nb, seq, d = x_prompt.shape
    db, dseq, _ = x_sample.shape
    depth = w_in.shape[0]
    n_ctx = nb * seq
    n_lat = db * dseq
    n_tok = n_ctx + n_lat
    x = jnp.concatenate([x_prompt.reshape(n_ctx, d), x_sample.reshape(n_lat, d)], axis=0)

    cond = jnp.concatenate([c_ctx[None], c, jnp.zeros((MOD_ROWS - 1 - db, d), F32)], axis=0)
    blk_cond = jnp.concatenate([jnp.zeros((n_ctx // ROW_BLOCK,), jnp.int32),
                                1 + jnp.arange(n_lat // ROW_BLOCK, dtype=jnp.int32) // (dseq // ROW_BLOCK)])
    n_blk = n_tok // ROW_BLOCK
    blk_row = jnp.arange(n_blk) * ROW_BLOCK
    blk_pos = jnp.where(blk_row < n_ctx, blk_row % seq, (blk_row - n_ctx) % dseq)
    blk_len = jnp.where(blk_row < n_ctx, seq, dseq)
    starts_seq = (blk_pos == 0)[:, None]
    ends_seq = (blk_pos + ROW_BLOCK == blk_len)[:, None]
    rope = _rope_tables(dseq)
    in_sizes = (RW_COLS, 2 * CONV_WIDTH, ATT_HEADS * HEAD_DIM, 2 * ATT_KV_HEADS * HEAD_DIM, 3 * d)
    past = cache_k.shape[2]

    w_in_bf, u_bf, v_bf = w_in.astype(BF16), peer_u.astype(BF16), peer_v.astype(BF16)
    ctx_k, ctx_v, ctx_s = [], [], []
    for l in range(depth):
        mod = modulation(cond, w_mod[l], b_mod[l][None])
        mod = mod.reshape(MOD_ROWS, 6, d)[blk_cond]
        mod_blocks = jnp.concatenate([mod, jnp.zeros((mod.shape[0], MOD_ROWS - 6, d), F32)], axis=1)

        z_rw, z_cv, z_q, z_kv, z_gate = in_projection(x, mod_blocks, w_in_bf, l, in_sizes)

        zb = z_rw.reshape(n_blk, ROW_BLOCK, RW_COLS)
        halo_prev = jnp.where(starts_seq, 0.0, jnp.roll(zb[:, -1], 1, axis=0))[:, None]
        halo_next = jnp.where(ends_seq, 0.0, jnp.roll(zb[:, 0], -1, axis=0))[:, None]
        mu = jnp.stack([shift_mu_prev[l], shift_mu_next[l]])
        vecs = jnp.stack([rw_k_k[l], rw_k_a[l], rw_r_k[l], rw_w0[l, 0], rw_w0[l, 1], rw_a0[l, 0],
                          rw_a0[l, 1], jnp.zeros((RW_WIDTH,), F32)])
        w2_pad = jnp.stack([_pad_rows(rw_w2[l, 0], True), _pad_rows(rw_w2[l, 1], False)]).astype(BF16)
        a2_pad = jnp.stack([_pad_rows(rw_a2[l, 0], True), _pad_rows(rw_a2[l, 1], False)]).astype(BF16)
        r, v, kk, lw, aa, kd, bonus, gate = rwkv_prep(z_rw, halo_prev, halo_next, mu, vecs, w2_pad, a2_pad,
                                                      rw_g2[l].astype(BF16))
        s0_ctx = jnp.zeros((nb, 2, RW_HEADS // 2, LANES, LANES), F32)
        of_c, ob_c, sfin = rwkv_scan(r, v, kk, lw, aa, kd, s0_ctx, 0, nb, seq)
        of_l, ob_l, _ = rwkv_scan(r, v, kk, lw, aa, kd, _bd_from_states(state_wkv[:, l]), n_ctx, db, dseq)
        ctx_s.append(_states_from_bd(sfin))

        cvec = jnp.stack([conv_b[l], conv_ln_w[l], conv_ln_b[l]] + [jnp.zeros((CONV_WIDTH,), F32)] * 5)
        y_cv = (conv_module(z_cv, conv_w[l], cvec, 0, nb, seq),
                conv_module(z_cv, conv_w[l], cvec, n_ctx, db, dseq))

        nq = jnp.tile(q_norm[l], LANES // HEAD_DIM)[None]
        nk = jnp.tile(k_norm[l], LANES // HEAD_DIM)[None]
        y_c, k_ctx = attention(z_q, z_kv, nq, nk, 0, nb, seq)
        y_l = attention(z_q, z_kv, nq, nk, n_ctx, db, dseq, rope=rope,
                        cache=(cache_k[:, l].reshape(db * past, LANES), cache_v[:, l].reshape(db * past, LANES)))
        ctx_k.append(k_ctx.reshape(nb, seq, ATT_KV_HEADS, HEAD_DIM))
        ctx_v.append(z_kv[:n_ctx, LANES:].reshape(nb, seq, ATT_KV_HEADS, HEAD_DIM))

        x, h2, pq = merge_project(x, mod_blocks, bonus, gate, z_gate,
                                  (of_c, of_l), (ob_c, ob_l), y_cv, (y_c, y_l),
                                  jnp.stack([rw_gn_w[l], rw_gn_b[l]]),
                                  w_out_rwkv[l].astype(BF16), w_out_conv[l].astype(BF16),
                                  w_out_attn[l].astype(BF16), w_o[l].astype(BF16), peer_wq[l].astype(BF16))

        kz = jnp.zeros_like(peer_keys[l][:, 0])
        keys_padded = jnp.concatenate([jnp.concatenate([peer_keys[l][:, 0], kz], axis=-1),
                                       jnp.concatenate([kz, peer_keys[l][:, 1]], axis=-1)], axis=1)
        w_sel = peer_select(pq, keys_padded)
        x = peer_dense(h2, w_sel, u_bf, v_bf, l, x, mod_blocks)

    new_cache_k = jnp.stack(ctx_k, axis=1)
    new_cache_v = jnp.stack(ctx_v, axis=1)
    new_state = jnp.stack(ctx_s, axis=1)
    return (x[:n_ctx].reshape(nb, seq, d), x[n_ctx:].reshape(db, dseq, d),
            new_cache_k, new_cache_v, new_state)
```
